```python
import jax, jax.numpy as jnp
from jax import lax
import numpy as np

D_MODEL = 1024
BATCH = 16
SEQ = 256
DEPTH = 1
DEC_BATCH = 2
DEC_SEQ = 2048
PAST_LEN = 256

GRID_W = 64
MLA_HEADS = 8
Q_RANK = 384
KV_RANK = 256
NOPE_DIM = 64
ROPE_DIM = 32
MLA_V_DIM = 64
ROPE_BASE = 10000.0
Q_BLOCK = 128
MLA_SCALE = (NOPE_DIM + ROPE_DIM) ** -0.5
MLSTM_HEADS = 4
MLSTM_DK = 128
MLSTM_DV = 256
MLSTM_CHUNK = 64
FFN_HIDDEN = ((8 * D_MODEL // 3 + 255) // 256) * 256
EPS = 1e-6
IN_SIZES = (Q_RANK, KV_RANK, ROPE_DIM,
            MLSTM_HEADS * MLSTM_DK, MLSTM_HEADS * MLSTM_DK, MLSTM_HEADS * MLSTM_DV,
            4 * MLSTM_HEADS, MLSTM_HEADS * MLSTM_DV, 2 * D_MODEL)
N_IN = Q_RANK + KV_RANK + ROPE_DIM + 2 * MLSTM_HEADS * MLSTM_DK + 2 * MLSTM_HEADS * MLSTM_DV + 4 * MLSTM_HEADS + 2 * D_MODEL

kernel_name = "hybrid_mla_mlstm_diffusion_step"


def rmsnorm(x, g):
    xf = x.astype(jnp.float32)
    y = xf * lax.rsqrt(jnp.mean(xf * xf, axis=-1, keepdims=True) + EPS)
    return (y * g.astype(jnp.float32)).astype(x.dtype)


def adaln(cond, w_mod, b_mod):
    mod = jax.nn.silu(cond) @ w_mod + b_mod
    mod = mod.reshape(cond.shape[0], 1, 6, D_MODEL)
    return [mod[:, :, i] for i in range(6)]


def grid_angles(n_tokens):
    rows = n_tokens // GRID_W
    r, col = jnp.meshgrid(jnp.arange(rows, dtype=jnp.float32), jnp.arange(GRID_W, dtype=jnp.float32), indexing='ij')
    half = ROPE_DIM // 2
    inv = ROPE_BASE ** (-jnp.arange(0, half, 2, dtype=jnp.float32) / half)
    return r.reshape(-1, 1) * inv, col.reshape(-1, 1) * inv


def rope_2d(x, ang_r, ang_c):
    xf = x.astype(jnp.float32)

    def rot(xh, ang):
        x1, x2 = jnp.split(xh, 2, axis=-1)
        cs, sn = jnp.cos(ang), jnp.sin(ang)
        return jnp.concatenate([x1 * cs - x2 * sn, x1 * sn + x2 * cs], axis=-1)

    half = ROPE_DIM // 2
    out = jnp.concatenate([rot(xf[..., :half], ang_r), rot(xf[..., half:], ang_c)], axis=-1)
    return out.astype(x.dtype)


def mla_attend(q_nope, q_pe, k_nope, k_pe, v):
    B, Sq, H, _ = q_nope.shape
    nb = Sq // Q_BLOCK
    qn = jnp.moveaxis(q_nope.reshape(B, nb, Q_BLOCK, H, NOPE_DIM), 1, 0)
    qp = jnp.moveaxis(q_pe.reshape(B, nb, Q_BLOCK, H, ROPE_DIM), 1, 0)

    def block(args):
        qn_b, qp_b = args
        s = jnp.einsum('bqhd,bkhd->bhqk', qn_b, k_nope) + jnp.einsum('bqhr,bkr->bhqk', qp_b, k_pe)
        p = jax.nn.softmax(s.astype(jnp.float32) * MLA_SCALE, axis=-1)
        return jnp.einsum('bhqk,bkhv->bqhv', p.astype(v.dtype), v)

    out = lax.map(block, (qn, qp))
    return jnp.moveaxis(out, 0, 1).reshape(B, Sq, H * MLA_V_DIM)


def mlstm_scan(q, k, v, ig, lf, C0, n0, m0):
    B, S, H, _ = q.shape
    L = MLSTM_CHUNK
    nc = S // L
    f32 = jnp.float32

    def chunks(t):
        t = t.astype(f32).reshape((B, nc, L, H) + t.shape[3:])
        return jnp.moveaxis(jnp.moveaxis(t, 1, 0), 3, 2)

    mask = jnp.tril(jnp.ones((L, L), dtype=bool))

    def step(carry, xs):
        C, n, m = carry
        qc, kc, vc, ic, fc = xs
        b = jnp.cumsum(fc, axis=-1)
        log_d = jnp.where(mask, b[..., :, None] - b[..., None, :] + ic[..., None, :], -jnp.inf)
        inter = b + m[..., None]
        m_row = jnp.maximum(inter, jnp.max(log_d, axis=-1))
        d = jnp.exp(log_d - m_row[..., None])
        w_inter = jnp.exp(inter - m_row)
        s = jnp.einsum('bhjd,bhsd->bhjs', qc, kc) * d
        num = jnp.einsum('bhjs,bhsv->bhjv', s, vc) + w_inter[..., None] * jnp.einsum('bhvd,bhjd->bhjv', C, qc)
        den = jnp.sum(s, axis=-1) + w_inter * jnp.einsum('bhd,bhjd->bhj', n, qc)
        h = num / jnp.maximum(jnp.abs(den), jnp.exp(-m_row))[..., None]
        b_last = b[..., -1]
        log_w = b_last[..., None] - b + ic
        m_new = jnp.maximum(b_last + m, jnp.max(log_w, axis=-1))
        w = jnp.exp(log_w - m_new[..., None])
        decay = jnp.exp(b_last + m - m_new)
        C_new = decay[..., None, None] * C + jnp.einsum('bhs,bhsv,bhsd->bhvd', w, vc, kc)
        n_new = decay[..., None] * n + jnp.einsum('bhs,bhsd->bhd', w, kc)
        return (C_new, n_new, m_new), h

    (C, n, m), h = lax.scan(step, (C0.astype(f32), n0.astype(f32), m0.astype(f32)),
                            (chunks(q), chunks(k), chunks(v), chunks(ig), chunks(lf)))
    h = jnp.moveaxis(jnp.moveaxis(h, 0, 1), 2, 3).reshape(B, S, H, v.shape[-1])
    return h.astype(v.dtype), (C, n, m)


def bidirectional_mlstm(q, k, v, ig_f, lf_f, ig_b, lf_b, C0, n0, m0):
    h_f, (Cf, nf, mf) = mlstm_scan(q, k, v, ig_f, lf_f, C0[:, 0], n0[:, 0], m0[:, 0])
    flip = lambda t: jnp.flip(t, axis=1)
    h_b, (Cb, nb, mb) = mlstm_scan(flip(q), flip(k), flip(v), flip(ig_b), flip(lf_b), C0[:, 1], n0[:, 1], m0[:, 1])
    h = h_f + flip(h_b)
    return h, (jnp.stack([Cf, Cb], axis=1), jnp.stack([nf, nb], axis=1), jnp.stack([mf, mb], axis=1))


def mixer(h, p, ctx, angles, state0):
    B, S, _ = h.shape
    z = h @ p['w_in']
    offsets = np.cumsum(IN_SIZES)[:-1].tolist()
    zq, zkv, zkpe, zmq, zmk, zmv, zgate, zmo, zbr = jnp.split(z, offsets, axis=-1)
    q = (rmsnorm(zq, p['g_q_norm']) @ p['w_uq']).reshape(B, S, MLA_HEADS, NOPE_DIM + ROPE_DIM)
    q_nope, q_pe = q[..., :NOPE_DIM], q[..., NOPE_DIM:]
    ckv = rmsnorm(zkv, p['g_kv_norm'])
    kpe = zkpe
    if angles is None:
        ckv_all, kpe_all = ckv, kpe
    else:
        ang_r, ang_c = angles
        q_pe = rope_2d(q_pe, ang_r[:, None, :], ang_c[:, None, :])
        ckv_all = jnp.concatenate([ckv, ctx[0].astype(ckv.dtype)], axis=1)
        kpe_all = jnp.concatenate([rope_2d(kpe, ang_r, ang_c), ctx[1].astype(kpe.dtype)], axis=1)
    Sk = ckv_all.shape[1]
    kv = (ckv_all @ p['w_ukv']).reshape(B, Sk, MLA_HEADS, NOPE_DIM + MLA_V_DIM)
    att = mla_attend(q_nope, q_pe, kv[..., :NOPE_DIM], kpe_all, kv[..., NOPE_DIM:])
    mq = zmq.reshape(B, S, MLSTM_HEADS, MLSTM_DK)
    mk = zmk.reshape(B, S, MLSTM_HEADS, MLSTM_DK) * (MLSTM_DK ** -0.5)
    mv = zmv.reshape(B, S, MLSTM_HEADS, MLSTM_DV)
    gates = (zgate + p['b_gates']).astype(jnp.float32).reshape(B, S, 4, MLSTM_HEADS)
    ig_f, ig_b = gates[:, :, 0], gates[:, :, 1]
    lf_f, lf_b = jax.nn.log_sigmoid(gates[:, :, 2]), jax.nn.log_sigmoid(gates[:, :, 3])
    hm, new_state = bidirectional_mlstm(mq, mk, mv, ig_f, lf_f, ig_b, lf_b, *state0)
    hm = rmsnorm(hm, p['g_mlstm_norm'].reshape(MLSTM_HEADS, MLSTM_DV)).reshape(B, S, MLSTM_HEADS * MLSTM_DV)
    hm = hm * jax.nn.sigmoid(zmo)
    g_a, g_b = jnp.split(jax.nn.sigmoid(zbr), 2, axis=-1)
    out = (g_a * (att @ p['w_o_mla']) + g_b * (hm @ p['w_o_mlstm'])) @ p['w_out']
    return out, ckv, kpe, new_state


def block(x, cond, p, ctx, angles, state0):
    sh1, sc1, gt1, sh2, sc2, gt2 = adaln(cond, p['w_mod'], p['b_mod'])
    h = rmsnorm(x, p['g_norm_mix']) * (1.0 + sc1) + sh1
    out, ckv, kpe, st = mixer(h, p, ctx, angles, state0)
    x = x + gt1 * out
    h = rmsnorm(x, p['g_norm_ffn']) * (1.0 + sc2) + sh2
    a, u = jnp.split(h @ p['w_ffn_in'], 2, axis=-1)
    x = x + gt2 * ((jax.nn.silu(a) * u) @ p['w_ffn_out'])
    return x, ckv, kpe, st


def setup_inputs(seed: int = 0) -> dict:
    key = jax.random.key(seed)
    ks = jax.random.split(key, 32)
    f32 = jnp.float32
    D = D_MODEL

    def nrm(k, shape, scale=1.0):
        return scale * jax.random.normal(k, shape, f32)

    def gain(k, shape):
        return 1.0 + 0.1 * nrm(k, shape)

    i_bias = 0.1 * nrm(ks[13], (DEPTH, 2 * MLSTM_HEADS))
    f_bias = jnp.tile(jnp.linspace(3.0, 6.0, MLSTM_HEADS), 2)[None, :] + 0.1 * nrm(ks[14], (DEPTH, 2 * MLSTM_HEADS))
    return {
        'x_prompt': nrm(ks[0], (BATCH, SEQ, D)),
        'x_sample': nrm(ks[1], (DEC_BATCH, DEC_SEQ, D)),
        'cache_ckv': nrm(ks[2], (DEC_BATCH, DEPTH, PAST_LEN, KV_RANK)),
        'cache_krope': nrm(ks[3], (DEC_BATCH, DEPTH, PAST_LEN, ROPE_DIM)),
        'state_C': nrm(ks[4], (DEC_BATCH, DEPTH, 2, MLSTM_HEADS, MLSTM_DV, MLSTM_DK), 0.1),
        'state_n': nrm(ks[5], (DEC_BATCH, DEPTH, 2, MLSTM_HEADS, MLSTM_DK), 0.1),
        'state_m': nrm(ks[6], (DEC_BATCH, DEPTH, 2, MLSTM_HEADS), 0.5),
        'c': nrm(ks[7], (DEC_BATCH, D)),
        'c_ctx': nrm(ks[8], (D,)),
        'w_mod': nrm(ks[9], (DEPTH, D, 6 * D), 0.5 * D ** -0.5),
        'b_mod': nrm(ks[10], (DEPTH, 6 * D), 0.02),
        'g_norm_mix': gain(ks[11], (DEPTH, D)),
        'w_in': nrm(ks[12], (DEPTH, D, N_IN), D ** -0.5),
        'b_gates': jnp.concatenate([i_bias, f_bias], axis=-1),
        'g_q_norm': gain(ks[15], (DEPTH, Q_RANK)),
        'w_uq': nrm(ks[16], (DEPTH, Q_RANK, MLA_HEADS * (NOPE_DIM + ROPE_DIM)), Q_RANK ** -0.5),
        'g_kv_norm': gain(ks[17], (DEPTH, KV_RANK)),
        'w_ukv': nrm(ks[18], (DEPTH, KV_RANK, MLA_HEADS * (NOPE_DIM + MLA_V_DIM)), KV_RANK ** -0.5),
        'g_mlstm_norm': gain(ks[19], (DEPTH, MLSTM_HEADS * MLSTM_DV)),
        'w_o_mla': nrm(ks[20], (DEPTH, MLA_HEADS * MLA_V_DIM, D), (MLA_HEADS * MLA_V_DIM) ** -0.5),
        'w_o_mlstm': nrm(ks[21], (DEPTH, MLSTM_HEADS * MLSTM_DV, D), (MLSTM_HEADS * MLSTM_DV) ** -0.5),
        'w_out': nrm(ks[22], (DEPTH, D, D), D ** -0.5),
        'g_norm_ffn': gain(ks[23], (DEPTH, D)),
        'w_ffn_in': nrm(ks[24], (DEPTH, D, 2 * FFN_HIDDEN), D ** -0.5),
        'w_ffn_out': nrm(ks[25], (DEPTH, FFN_HIDDEN, D), FFN_HIDDEN ** -0.5),
        'g_final': gain(ks[26], (D,)),
    }


def reference(x_prompt, x_sample, cache_ckv, cache_krope, state_C, state_n, state_m, c, c_ctx,
              w_mod, b_mod, g_norm_mix, w_in, b_gates, g_q_norm, w_uq, g_kv_norm, w_ukv,
              g_mlstm_norm, w_o_mla, w_o_mlstm, w_out, g_norm_ffn, w_ffn_in, w_ffn_out, g_final):
    def layer_params(l):
        return {'w_mod': w_mod[l], 'b_mod': b_mod[l], 'g_norm_mix': g_norm_mix[l], 'w_in': w_in[l],
                'b_gates': b_gates[l], 'g_q_norm': g_q_norm[l], 'w_uq': w_uq[l], 'g_kv_norm': g_kv_norm[l],
                'w_ukv': w_ukv[l], 'g_mlstm_norm': g_mlstm_norm[l], 'w_o_mla': w_o_mla[l],
                'w_o_mlstm': w_o_mlstm[l], 'w_out': w_out[l], 'g_norm_ffn': g_norm_ffn[l],
                'w_ffn_in': w_ffn_in[l], 'w_ffn_out': w_ffn_out[l]}

    bp = x_prompt.shape[0]
    C0 = jnp.zeros((bp, 2, MLSTM_HEADS, MLSTM_DV, MLSTM_DK), jnp.float32)
    n0 = jnp.zeros((bp, 2, MLSTM_HEADS, MLSTM_DK), jnp.float32)
    m0 = jnp.zeros((bp, 2, MLSTM_HEADS), jnp.float32)
    xp = x_prompt
    ckv_l, kpe_l, C_l, n_l, m_l = [], [], [], [], []
    for l in range(DEPTH):
        xp, ckv, kpe, (Cc, nc_, mc) = block(xp, c_ctx[None, :], layer_params(l), None, None, (C0, n0, m0))
        ckv_l.append(ckv); kpe_l.append(kpe); C_l.append(Cc); n_l.append(nc_); m_l.append(mc)
    y_prompt = rmsnorm(xp, g_final)
    new_ckv = jnp.stack(ckv_l, axis=1)
    new_krope = jnp.stack(kpe_l, axis=1)
    new_C = jnp.stack(C_l, axis=1)
    new_n = jnp.stack(n_l, axis=1)
    new_m = jnp.stack(m_l, axis=1)

    angles = grid_angles(x_sample.shape[1])
    xs = x_sample
    for l in range(DEPTH):
        xs, _, _, _ = block(xs, c, layer_params(l), (cache_ckv[:, l], cache_krope[:, l]), angles,
                            (state_C[:, l], state_n[:, l], state_m[:, l]))
    y_sample = rmsnorm(xs, g_final)
    return (y_prompt, y_sample, new_ckv, new_krope, new_C, new_n, new_m)
```

```python
import functools
import math

import jax
import jax.numpy as jnp
from jax import lax
from jax.experimental import pallas as pl
from jax.experimental.pallas import tpu as pltpu

F32 = jnp.float32
BF16 = jnp.bfloat16

D_MODEL = 1024
GRID_W = 64
MLA_HEADS = 8
Q_RANK = 384
KV_RANK = 256
NOPE_DIM = 64
ROPE_DIM = 32
MLA_V_DIM = 64
ROPE_BASE = 10000.0
MLA_SCALE = (NOPE_DIM + ROPE_DIM) ** -0.5
MLSTM_HEADS = 4
MLSTM_DK = 128
MLSTM_DV = 256
FFN_HIDDEN = ((8 * D_MODEL // 3 + 255) // 256) * 256
EPS = 1e-6

LANES = 128
HEAD_BLOCK = LANES
ROPE_LANE0 = NOPE_DIM
N_DIRHEAD = 2 * MLSTM_HEADS
MLSTM_CHUNK = 256
TOKEN_TILE = 256
Q_TILE = 256
VMEM_LIMIT = 56 * 1024 * 1024

_SEG_WIDTHS = (Q_RANK, KV_RANK, LANES, MLSTM_HEADS * MLSTM_DK, MLSTM_HEADS * MLSTM_DK,
               MLSTM_HEADS * MLSTM_DV, LANES, MLSTM_HEADS * MLSTM_DV, 2 * D_MODEL)
_SEG_OFF = tuple(sum(_SEG_WIDTHS[:i]) for i in range(len(_SEG_WIDTHS) + 1))
SEG_Q, SEG_KV, SEG_KPE, SEG_MQ, SEG_MK, SEG_MV, SEG_GATE, SEG_MO, SEG_BR = (
    (_SEG_OFF[i], _SEG_OFF[i + 1]) for i in range(9))
N_IN_PACKED = _SEG_OFF[-1]


def _dot(a, b):
    return jnp.dot(a, b, preferred_element_type=F32)


def _dot_nt(a, b):
    return lax.dot_general(a, b, (((1,), (1,)), ((), ())), preferred_element_type=F32)


def _dot_tn(a, b):
    return lax.dot_general(a, b, (((0,), (0,)), ((), ())), preferred_element_type=F32)


def _rms(x, g):
    ms = jnp.mean(x * x, axis=-1, keepdims=True)
    return x * lax.rsqrt(ms + EPS) * g


def _sigmoid(x):
    return 1.0 / (1.0 + jnp.exp(-x))


def _const_spec(shape):
    nd = len(shape)
    return pl.BlockSpec(shape, lambda *_: (0,) * nd, pipeline_mode=pl.Buffered(1))


def _mod_spec(mod, tiles_per_seq):
    if mod.shape[0] == 1:
        return pl.BlockSpec((None, 6, D_MODEL), lambda i: (0, 0, 0))
    return pl.BlockSpec((None, 6, D_MODEL), lambda i: (i // tiles_per_seq, 0, 0))


def _params(sem):
    return pltpu.CompilerParams(dimension_semantics=sem, vmem_limit_bytes=VMEM_LIMIT)


def _adaln_kernel(c_ref, w_ref, b_ref, o_ref):
    c = c_ref[...]
    s = c * _sigmoid(c)
    o_ref[...] = _dot(s.astype(BF16), w_ref[...].astype(BF16)) + b_ref[...]


def _adaln(cond8, w_mod, b_mod):
    n = w_mod.shape[1]
    tn = 1536
    return pl.pallas_call(
        _adaln_kernel,
        grid=(n // tn,),
        in_specs=[pl.BlockSpec((8, D_MODEL), lambda j: (0, 0)),
                  pl.BlockSpec((D_MODEL, tn), lambda j: (0, j)),
                  pl.BlockSpec((1, tn), lambda j: (0, j))],
        out_specs=pl.BlockSpec((8, tn), lambda j: (0, j)),
        out_shape=jax.ShapeDtypeStruct((8, n), F32),
        compiler_params=_params(("arbitrary",)),
        name="adaln",
    )(cond8, w_mod, b_mod)


def _rope_block(x, cos, sin_lo, sin_hi):
    return x * cos + pltpu.roll(x, LANES - 8, 1) * sin_lo + pltpu.roll(x, 8, 1) * sin_hi


def _inproj_kernel(*refs, rope):
    if rope:
        (x_ref, mod_ref, cos_ref, slo_ref, shi_ref, gmix_ref, win_ref, gq_ref, wuq_ref, gkv_ref, wukv_ref,
         bg_ref, q_ref, kc_ref, v_ref, mq_ref, mk_ref, mv_ref, gate_ref, smo_ref, sbr_ref) = refs
        cos, slo, shi = cos_ref[...], slo_ref[...], shi_ref[...]
    else:
        (x_ref, mod_ref, gmix_ref, win_ref, gq_ref, wuq_ref, gkv_ref, wukv_ref,
         bg_ref, q_ref, kc_ref, v_ref, mq_ref, mk_ref, mv_ref, gate_ref, smo_ref, sbr_ref,
         ckv_ref, kpe_ref) = refs
    mod = mod_ref[...]
    h = _rms(x_ref[...], gmix_ref[...]) * (1.0 + mod[1:2]) + mod[0:1]
    hb = h.astype(BF16)

    def proj(seg):
        return _dot(hb, win_ref[:, seg[0]:seg[1]])

    qn = _rms(proj(SEG_Q), gq_ref[...]).astype(BF16)
    q = _dot(qn, wuq_ref[...])
    for hd in range(MLA_HEADS):
        sl = slice(hd * HEAD_BLOCK, (hd + 1) * HEAD_BLOCK)
        qh = q[:, sl]
        if rope:
            qh = _rope_block(qh, cos, slo, shi)
        q_ref[:, sl] = (qh * MLA_SCALE).astype(BF16)

    ckv = _rms(proj(SEG_KV), gkv_ref[...])
    zkpe = proj(SEG_KPE)
    if not rope:
        ckv_ref[...] = ckv
        kpe_ref[...] = zkpe[:, :ROPE_DIM]
    kp = pltpu.roll(zkpe, ROPE_LANE0, 1)
    if rope:
        kp = _rope_block(kp, cos, slo, shi)
    kvn = _dot(ckv.astype(BF16), wukv_ref[...])
    for hd in range(MLA_HEADS):
        sl = slice(hd * HEAD_BLOCK, (hd + 1) * HEAD_BLOCK)
        kc_ref[:, sl] = (kvn[:, sl] + kp).astype(BF16)
    v_ref[...] = kvn[:, MLA_HEADS * HEAD_BLOCK:].astype(BF16)

    mq_ref[...] = proj(SEG_MQ).astype(BF16)
    mk_ref[...] = (proj(SEG_MK) * (MLSTM_DK ** -0.5)).astype(BF16)
    mv_ref[...] = proj(SEG_MV).astype(BF16)
    gate_ref[...] = proj(SEG_GATE) + bg_ref[...]
    smo_ref[...] = _sigmoid(proj(SEG_MO)).astype(BF16)
    sbr_ref[...] = _sigmoid(proj(SEG_BR)).astype(BF16)


def _inproj(x, mod, rope_tabs, gmix, w_in_p, gq, w_uq_p, gkv, w_ukv_p, bg, *, seq_len):
    n_tok = x.shape[0]
    tm = TOKEN_TILE
    tiles_per_seq = seq_len // tm
    rope = rope_tabs is not None
    tok = lambda c: pl.BlockSpec((tm, c), lambda i: (i, 0))
    in_specs = [tok(D_MODEL), _mod_spec(mod, tiles_per_seq)]
    args = [x, mod]
    if rope:
        in_specs += [pl.BlockSpec((tm, LANES), lambda i: (i % tiles_per_seq, 0))] * 3
        args += list(rope_tabs)
    in_specs += [_const_spec(a.shape) for a in (gmix, w_in_p, gq, w_uq_p, gkv, w_ukv_p, bg)]
    args += [gmix, w_in_p, gq, w_uq_p, gkv, w_ukv_p, bg]
    widths = [(MLA_HEADS * HEAD_BLOCK, BF16), (MLA_HEADS * HEAD_BLOCK, BF16), (MLA_HEADS * MLA_V_DIM, BF16),
              (MLSTM_HEADS * MLSTM_DK, BF16), (MLSTM_HEADS * MLSTM_DK, BF16), (MLSTM_HEADS * MLSTM_DV, BF16),
              (LANES, F32), (MLSTM_HEADS * MLSTM_DV, BF16), (2 * D_MODEL, BF16)]
    if not rope:
        widths += [(KV_RANK, F32), (ROPE_DIM, F32)]
    return pl.pallas_call(
        functools.partial(_inproj_kernel, rope=rope),
        grid=(n_tok // tm,),
        in_specs=in_specs,
        out_specs=[tok(c) for c, _ in widths],
        out_shape=[jax.ShapeDtypeStruct((n_tok, c), dt) for c, dt in widths],
        compiler_params=_params(("parallel",)),
        name="inproj_lat" if rope else "inproj_ctx",
    )(*args)


def _cachekv_kernel(ckv_ref, kp_ref, wukv_ref, kc_ref, v_ref):
    kvn = _dot(ckv_ref[...].astype(BF16), wukv_ref[...])
    kp = kp_ref[...]
    for hd in range(MLA_HEADS):
        sl = slice(hd * HEAD_BLOCK, (hd + 1) * HEAD_BLOCK)
        kc_ref[:, sl] = (kvn[:, sl] + kp).astype(BF16)
    v_ref[...] = kvn[:, MLA_HEADS * HEAD_BLOCK:].astype(BF16)


def _cachekv(cache_ckv, krope_blk, w_ukv_p):
    b, past, _ = cache_ckv.shape
    return pl.pallas_call(
        _cachekv_kernel,
        grid=(b,),
        in_specs=[pl.BlockSpec((None, past, KV_RANK), lambda i: (i, 0, 0)),
                  pl.BlockSpec((None, past, LANES), lambda i: (i, 0, 0)),
                  _const_spec(w_ukv_p.shape)],
        out_specs=[pl.BlockSpec((None, past, MLA_HEADS * HEAD_BLOCK), lambda i: (i, 0, 0)),
                   pl.BlockSpec((None, past, MLA_HEADS * MLA_V_DIM), lambda i: (i, 0, 0))],
        out_shape=[jax.ShapeDtypeStruct((b, past, MLA_HEADS * HEAD_BLOCK), BF16),
                   jax.ShapeDtypeStruct((b, past, MLA_HEADS * MLA_V_DIM), BF16)],
        compiler_params=_params(("parallel",)),
        name="cachekv",
    )(cache_ckv, krope_blk, w_ukv_p)


def _attn_kernel(*refs, has_cache):
    if has_cache:
        q_ref, k_ref, v_ref, kc_ref, vc_ref, o_ref = refs
    else:
        q_ref, k_ref, v_ref, o_ref = refs
    tq = q_ref.shape[0]
    lane = lax.broadcasted_iota(jnp.int32, (tq, LANES), 1)
    for pair in range(MLA_HEADS // 2):
        vsl = slice(pair * LANES, (pair + 1) * LANES)
        outs = []
        for j in range(2):
            hd = 2 * pair + j
            sl = slice(hd * HEAD_BLOCK, (hd + 1) * HEAD_BLOCK)
            qh = q_ref[:, sl]
            s = _dot_nt(qh, k_ref[:, sl])
            m = jnp.max(s, axis=-1, keepdims=True)
            if has_cache:
                sc = _dot_nt(qh, kc_ref[:, sl])
                m = jnp.maximum(m, jnp.max(sc, axis=-1, keepdims=True))
            p = jnp.exp(s - m)
            l = jnp.sum(p, axis=-1, keepdims=True)
            o = _dot(p.astype(BF16), v_ref[:, vsl])
            if has_cache:
                pc = jnp.exp(sc - m)
                l = l + jnp.sum(pc, axis=-1, keepdims=True)
                o = o + _dot(pc.astype(BF16), vc_ref[:, vsl])
            outs.append(o * (1.0 / l))
        o_ref[:, vsl] = jnp.where(lane < MLA_V_DIM, outs[0], outs[1]).astype(BF16)


def _attention(q, k, v, cache=None):
    b, s, _ = q.shape
    tq = Q_TILE
    kw, vw = MLA_HEADS * HEAD_BLOCK, MLA_HEADS * MLA_V_DIM
    in_specs = [pl.BlockSpec((None, tq, kw), lambda i, j: (i, j, 0)),
                pl.BlockSpec((None, s, kw), lambda i, j: (i, 0, 0)),
                pl.BlockSpec((None, s, vw), lambda i, j: (i, 0, 0))]
    args = [q, k, v]
    if cache is not None:
        past = cache[0].shape[1]
        in_specs += [pl.BlockSpec((None, past, kw), lambda i, j: (i, 0, 0)),
                     pl.BlockSpec((None, past, vw), lambda i, j: (i, 0, 0))]
        args += list(cache)
    return pl.pallas_call(
        functools.partial(_attn_kernel, has_cache=cache is not None),
        grid=(b, s // tq),
        in_specs=in_specs,
        out_specs=pl.BlockSpec((None, tq, vw), lambda i, j: (i, j, 0)),
        out_shape=jax.ShapeDtypeStruct((b, s, vw), BF16),
        compiler_params=_params(("parallel", "arbitrary")),
        name="attn_lat" if cache is not None else "attn_ctx",
    )(*args)


def _lane_pick(x, k):
    lane = lax.broadcasted_iota(jnp.int32, x.shape, 1)
    return jnp.sum(jnp.where(lane == k, x, 0.0), axis=1, keepdims=True)


def _prefix_sum_rows(x):
    n = x.shape[0]
    r = lax.broadcasted_iota(jnp.int32, (n, n), 0)
    c = lax.broadcasted_iota(jnp.int32, (n, n), 1)
    tri = jnp.where(c <= r, 1.0, 0.0).astype(BF16)
    hi = x.astype(BF16)
    r1 = x - hi.astype(F32)
    mid = r1.astype(BF16)
    lo = (r1 - mid.astype(F32)).astype(BF16)
    return _dot(tri, hi) + _dot(tri, mid) + _dot(tri, lo)


def _gate_prep(g):
    n = g.shape[0]
    lane = lax.broadcasted_iota(jnp.int32, (n, LANES), 1)
    fpre = pltpu.roll(g, LANES - N_DIRHEAD, 1)
    lf = jnp.minimum(fpre, 0.0) - jnp.log(1.0 + jnp.exp(-jnp.abs(fpre)))
    binc = _prefix_sum_rows(lf)
    tot = binc[n - 1:n, :]
    b = jnp.where(lane < MLSTM_HEADS, binc, tot - binc + lf)
    u = g - b
    return b, u, u.T, tot


def _mlstm_dir(s0, q, k32, v, u_row, u_col, b_col, tot_k, *, fwd, state):
    n = s0.shape[0]
    r = lax.broadcasted_iota(jnp.int32, (n, n), 0)
    c = lax.broadcasted_iota(jnp.int32, (n, n), 1)
    mask = (c <= r) if fwd else (c >= r)
    a = jnp.where(mask, u_row, -jnp.inf)
    m_prev = 0.0 if state is None else state[2]
    g = jnp.maximum(jnp.max(a, axis=1, keepdims=True), m_prev)
    g_last = jnp.maximum(jnp.max(u_row, axis=1, keepdims=True), m_prev)
    sd = s0 * jnp.exp(a - g)
    den = jnp.sum(sd, axis=1, keepdims=True)
    num = _dot(sd.astype(BF16), v)
    if state is not None:
        c_prev, n_prev, _ = state
        w_inter = jnp.exp(m_prev - g)
        num = num + w_inter * _dot_nt(q, c_prev.astype(BF16))
        den = den + w_inter * jnp.sum(q.astype(F32) * n_prev, axis=1, keepdims=True)
    floor = jnp.exp(-(b_col + g))
    h = num * (1.0 / jnp.maximum(jnp.abs(den), floor))
    wk = jnp.exp(u_col - g_last) * k32
    c_new = _dot_tn(v, wk.astype(BF16))
    n_new = jnp.sum(wk, axis=0, keepdims=True)
    if state is not None:
        decay = jnp.exp(m_prev - g_last)
        c_new = decay * c_prev + c_new
        n_new = decay * n_prev + n_new
    return h, c_new, n_new, tot_k + g_last


def _mlstm_ctx_kernel(q_ref, k_ref, v_ref, g_ref, h_ref, c_ref, n_ref, m_ref):
    b, u, ut, tot = _gate_prep(g_ref[...])
    lane = lax.broadcasted_iota(jnp.int32, (1, LANES), 1)
    m_row = jnp.zeros((1, LANES), F32)
    for hd in range(MLSTM_HEADS):
        q = q_ref[:, hd * MLSTM_DK:(hd + 1) * MLSTM_DK]
        k = k_ref[:, hd * MLSTM_DK:(hd + 1) * MLSTM_DK]
        v = v_ref[:, hd * MLSTM_DV:(hd + 1) * MLSTM_DV]
        s0 = _dot_nt(q, k)
        k32 = k.astype(F32)
        h_sum = None
        for fwd in (True, False):
            kk = hd if fwd else MLSTM_HEADS + hd
            h, c_new, n_new, m_new = _mlstm_dir(
                s0, q, k32, v, ut[kk:kk + 1, :], _lane_pick(u, kk), _lane_pick(b, kk), _lane_pick(tot, kk),
                fwd=fwd, state=None)
            h_sum = h if h_sum is None else h_sum + h
            c_ref[kk] = c_new
            n_ref[kk:kk + 1, :] = n_new
            m_row = jnp.where(lane == kk, m_new, m_row)
        h_ref[:, hd * MLSTM_DV:(hd + 1) * MLSTM_DV] = h_sum
    m_ref[...] = m_row


def _mlstm_ctx(mq, mk, mv, gates):
    b, s, _ = mq.shape
    seq = lambda c: pl.BlockSpec((None, s, c), lambda i: (i, 0, 0))
    return pl.pallas_call(
        _mlstm_ctx_kernel,
        grid=(b,),
        in_specs=[seq(MLSTM_HEADS * MLSTM_DK), seq(MLSTM_HEADS * MLSTM_DK), seq(MLSTM_HEADS * MLSTM_DV), seq(LANES)],
        out_specs=[seq(MLSTM_HEADS * MLSTM_DV),
                   pl.BlockSpec((None, N_DIRHEAD, MLSTM_DV, MLSTM_DK), lambda i: (i, 0, 0, 0)),
                   pl.BlockSpec((None, N_DIRHEAD, MLSTM_DK), lambda i: (i, 0, 0)),
                   pl.BlockSpec((None, 1, LANES), lambda i: (i, 0, 0))],
        out_shape=[jax.ShapeDtypeStruct((b, s, MLSTM_HEADS * MLSTM_DV), F32),
                   jax.ShapeDtypeStruct((b, N_DIRHEAD, MLSTM_DV, MLSTM_DK), F32),
                   jax.ShapeDtypeStruct((b, N_DIRHEAD, MLSTM_DK), F32),
                   jax.ShapeDtypeStruct((b, 1, LANES), F32)],
        compiler_params=_params(("parallel",)),
        name="mlstm_ctx",
    )(mq, mk, mv, gates)


def _mlstm_lat_kernel(qf_ref, kf_ref, vf_ref, gf_ref, qb_ref, kb_ref, vb_ref, gb_ref, c0_ref, n0_ref, m0_ref,
                      h_ref, c_ref, n_ref, m_ref, *, n_chunks):
    step = pl.program_id(1)

    @pl.when(step == 0)
    def _():
        c_ref[...] = c0_ref[...]
        n_ref[...] = n0_ref[...]
        m_ref[...] = m0_ref[...]
        h_ref[...] = jnp.zeros_like(h_ref)

    lane = lax.broadcasted_iota(jnp.int32, (1, LANES), 1)
    m_old = m_ref[...]
    m_row = m_old
    for fwd in (True, False):
        q_ref, k_ref, v_ref, g_ref = (qf_ref, kf_ref, vf_ref, gf_ref) if fwd else (qb_ref, kb_ref, vb_ref, gb_ref)
        chunk = step if fwd else n_chunks - 1 - step
        row0 = pl.multiple_of(chunk * MLSTM_CHUNK, MLSTM_CHUNK)
        b, u, ut, tot = _gate_prep(g_ref[...])
        for hd in range(MLSTM_HEADS):
            kk = hd if fwd else MLSTM_HEADS + hd
            q = q_ref[:, hd * MLSTM_DK:(hd + 1) * MLSTM_DK]
            k = k_ref[:, hd * MLSTM_DK:(hd + 1) * MLSTM_DK]
            v = v_ref[:, hd * MLSTM_DV:(hd + 1) * MLSTM_DV]
            s0 = _dot_nt(q, k)
            state = (c_ref[kk], n_ref[kk:kk + 1, :], _lane_pick(m_old, kk))
            h, c_new, n_new, m_new = _mlstm_dir(
                s0, q, k.astype(F32), v, ut[kk:kk + 1, :], _lane_pick(u, kk), _lane_pick(b, kk),
                _lane_pick(tot, kk), fwd=fwd, state=state)
            c_ref[kk] = c_new
            n_ref[kk:kk + 1, :] = n_new
            m_row = jnp.where(lane == kk, m_new, m_row)
            h_ref[pl.ds(row0, MLSTM_CHUNK), hd * MLSTM_DV:(hd + 1) * MLSTM_DV] += h
    m_ref[...] = m_row


def _mlstm_lat(mq, mk, mv, gates, c0, n0, m0):
    b, s, _ = mq.shape
    lc = MLSTM_CHUNK
    nc = s // lc
    fw = lambda c: pl.BlockSpec((None, lc, c), lambda i, j: (i, j, 0))
    bw = lambda c: pl.BlockSpec((None, lc, c), lambda i, j: (i, nc - 1 - j, 0))
    dk, dv = MLSTM_HEADS * MLSTM_DK, MLSTM_HEADS * MLSTM_DV
    return pl.pallas_call(
        functools.partial(_mlstm_lat_kernel, n_chunks=nc),
        grid=(b, nc),
        in_specs=[fw(dk), fw(dk), fw(dv), fw(LANES), bw(dk), bw(dk), bw(dv), bw(LANES),
                  pl.BlockSpec((None, N_DIRHEAD, MLSTM_DV, MLSTM_DK), lambda i, j: (i, 0, 0, 0)),
                  pl.BlockSpec((None, N_DIRHEAD, MLSTM_DK), lambda i, j: (i, 0, 0)),
                  pl.BlockSpec((None, 1, LANES), lambda i, j: (i, 0, 0))],
        out_specs=pl.BlockSpec((None, s, dv), lambda i, j: (i, 0, 0)),
        out_shape=jax.ShapeDtypeStruct((b, s, dv), F32),
        scratch_shapes=[pltpu.VMEM((N_DIRHEAD, MLSTM_DV, MLSTM_DK), F32),
                        pltpu.VMEM((N_DIRHEAD, MLSTM_DK), F32),
                        pltpu.VMEM((1, LANES), F32)],
        compiler_params=_params(("parallel", "arbitrary")),
        name="mlstm_lat",
    )(mq, mk, mv, gates, mq, mk, mv, gates, c0, n0, m0)


def _tail_kernel(x_ref, mod_ref, att_ref, hs_ref, smo_ref, sbr_ref, gml_ref, womla_ref, womlstm_ref, wout_ref,
                 gffn_ref, wfin_ref, wfout_ref, gfin_ref, y_ref):
    mod = mod_ref[...]
    a = _dot(att_ref[...], womla_ref[...])
    gml = gml_ref[...]
    parts = []
    for hd in range(MLSTM_HEADS):
        sl = slice(hd * MLSTM_DV, (hd + 1) * MLSTM_DV)
        parts.append((_rms(hs_ref[:, sl], gml[:, sl]) * smo_ref[:, sl].astype(F32)).astype(BF16))
    bm = _dot(jnp.concatenate(parts, axis=1), womlstm_ref[...])
    merged = sbr_ref[:, :D_MODEL].astype(F32) * a + sbr_ref[:, D_MODEL:].astype(F32) * bm
    x1 = x_ref[...] + mod[2:3] * _dot(merged.astype(BF16), wout_ref[...])
    h2 = (_rms(x1, gffn_ref[...]) * (1.0 + mod[4:5]) + mod[3:4]).astype(BF16)
    ga = _dot(h2, wfin_ref[:, :FFN_HIDDEN])
    gu = _dot(h2, wfin_ref[:, FFN_HIDDEN:])
    act = (ga * _sigmoid(ga) * gu).astype(BF16)
    x2 = x1 + mod[5:6] * _dot(act, wfout_ref[...])
    y_ref[...] = _rms(x2, gfin_ref[...])


def _tail(x, mod, att, hs, smo, sbr, gml, w_o_mla, w_o_mlstm, w_out, gffn, w_ffn_in, w_ffn_out, gfin, *, seq_len, name):
    n_tok = x.shape[0]
    tm = TOKEN_TILE
    tiles_per_seq = seq_len // tm
    tok = lambda c: pl.BlockSpec((tm, c), lambda i: (i, 0))
    consts = (gml, w_o_mla, w_o_mlstm, w_out, gffn, w_ffn_in, w_ffn_out, gfin)
    return pl.pallas_call(
        _tail_kernel,
        grid=(n_tok // tm,),
        in_specs=[tok(D_MODEL), _mod_spec(mod, tiles_per_seq),
                  tok(att.shape[1]), tok(hs.shape[1]), tok(smo.shape[1]), tok(sbr.shape[1])]
                 + [_const_spec(a.shape) for a in consts],
        out_specs=tok(D_MODEL),
        out_shape=jax.ShapeDtypeStruct((n_tok, D_MODEL), F32),
        compiler_params=_params(("parallel",)),
        name=name,
    )(x, mod, att, hs, smo, sbr, *consts)


def _pack_weights(w_in, w_uq, w_ukv, b_gates):
    offs = [0]
    for w in (Q_RANK, KV_RANK, ROPE_DIM, MLSTM_HEADS * MLSTM_DK, MLSTM_HEADS * MLSTM_DK, MLSTM_HEADS * MLSTM_DV,
              4 * MLSTM_HEADS, MLSTM_HEADS * MLSTM_DV, 2 * D_MODEL):
        offs.append(offs[-1] + w)
    segs = []
    for i, wpad in enumerate(_SEG_WIDTHS):
        seg = w_in[:, offs[i]:offs[i + 1]]
        segs.append(jnp.pad(seg, ((0, 0), (0, wpad - seg.shape[1]))))
    w_in_p = jnp.concatenate(segs, axis=1).astype(BF16)
    qd = NOPE_DIM + ROPE_DIM
    w_uq_p = jnp.pad(w_uq.reshape(Q_RANK, MLA_HEADS, qd), ((0, 0), (0, 0), (0, HEAD_BLOCK - qd)))
    w_uq_p = w_uq_p.reshape(Q_RANK, MLA_HEADS * HEAD_BLOCK).astype(BF16)
    kv = w_ukv.reshape(KV_RANK, MLA_HEADS, NOPE_DIM + MLA_V_DIM)
    wk = jnp.pad(kv[:, :, :NOPE_DIM], ((0, 0), (0, 0), (0, HEAD_BLOCK - NOPE_DIM))).reshape(KV_RANK, -1)
    wv = kv[:, :, NOPE_DIM:].reshape(KV_RANK, -1)
    w_ukv_p = jnp.concatenate([wk, wv], axis=1).astype(BF16)
    bg = jnp.pad(b_gates, ((0, 0), (0, LANES - b_gates.shape[1])))
    return w_in_p, w_uq_p, w_ukv_p, bg


def _rope_tables(n_tokens):
    pos = jnp.arange(n_tokens, dtype=jnp.int32)
    row = (pos // GRID_W).astype(F32)[:, None]
    col = (pos % GRID_W).astype(F32)[:, None]
    half = ROPE_DIM // 2
    inv = ROPE_BASE ** (-jnp.arange(0, half, 2, dtype=F32) / half)
    lane = jnp.arange(LANES)
    r = lane - ROPE_LANE0
    in_rope = (r >= 0) & (r < ROPE_DIM)
    rr = jnp.clip(r, 0, ROPE_DIM - 1)
    ang = jnp.where((rr // half == 0)[None, :], row * inv[rr % (half // 2)][None, :], col * inv[rr % (half // 2)][None, :])
    first = (rr % half) < (half // 2)
    cos = jnp.where(in_rope[None, :], jnp.cos(ang), 1.0)
    sin = jnp.sin(ang)
    sin_lo = jnp.where((in_rope & first)[None, :], -sin, 0.0)
    sin_hi = jnp.where((in_rope & ~first)[None, :], sin, 0.0)
    return cos, sin_lo, sin_hi


def kernel(x_prompt, x_sample, cache_ckv, cache_krope, state_C, state_n, state_m, c, c_ctx, w_mod, b_mod, g_norm_mix,
           w_in, b_gates, g_q_norm, w_uq, g_kv_norm, w_ukv, g_mlstm_norm, w_o_mla, w_o_mlstm, w_out, g_norm_ffn,
           w_ffn_in, w_ffn_out, g_final):
    bp, sp, d = x_prompt.shape
    bs, ss, _ = x_sample.shape
    layer = 0
    assert w_mod.shape[0] == 1 and sp == MLSTM_CHUNK and ss % MLSTM_CHUNK == 0

    cond = jnp.concatenate([c_ctx[None, :], c, jnp.zeros((8 - 1 - bs, d), F32)], axis=0)
    mod = _adaln(cond, w_mod[layer], b_mod[layer][None, :]).reshape(8, 6, d)
    mod_ctx, mod_lat = mod[0:1], mod[1:1 + bs]

    w_in_p, w_uq_p, w_ukv_p, bg = _pack_weights(w_in[layer], w_uq[layer], w_ukv[layer], b_gates[layer][None, :])
    row = lambda g: g[layer][None, :]
    shared_in = (row(g_norm_mix), w_in_p, row(g_q_norm), w_uq_p, row(g_kv_norm), w_ukv_p, bg)
    tail_w = (row(g_mlstm_norm), w_o_mla[layer].astype(BF16), w_o_mlstm[layer].astype(BF16), w_out[layer].astype(BF16),
              row(g_norm_ffn), w_ffn_in[layer].astype(BF16), w_ffn_out[layer].astype(BF16), g_final[None, :])

    xc = x_prompt.reshape(bp * sp, d)
    q, kc, v, mq, mk, mv, gates, smo, sbr, ckv, kpe = _inproj(xc, mod_ctx, None, *shared_in, seq_len=sp)
    seq3 = lambda a, b_, s_: a.reshape(b_, s_, a.shape[-1])
    att = _attention(seq3(q, bp, sp), seq3(kc, bp, sp), seq3(v, bp, sp))
    hs, new_c, new_n, new_m = _mlstm_ctx(seq3(mq, bp, sp), seq3(mk, bp, sp), seq3(mv, bp, sp), seq3(gates, bp, sp))
    y_prompt = _tail(xc, mod_ctx, att.reshape(bp * sp, -1), hs.reshape(bp * sp, -1), smo, sbr, *tail_w,
                     seq_len=sp, name="tail_ctx").reshape(bp, sp, d)
    new_ckv = ckv.reshape(bp, 1, sp, KV_RANK)
    new_krope = kpe.reshape(bp, 1, sp, ROPE_DIM)
    new_C = new_c.reshape(bp, 1, 2, MLSTM_HEADS, MLSTM_DV, MLSTM_DK)
    new_N = new_n.reshape(bp, 1, 2, MLSTM_HEADS, MLSTM_DK)
    new_M = new_m[:, 0, :N_DIRHEAD].reshape(bp, 1, 2, MLSTM_HEADS)

    xl = x_sample.reshape(bs * ss, d)
    q, kc, v, mq, mk, mv, gates, smo, sbr = _inproj(xl, mod_lat, _rope_tables(ss), *shared_in, seq_len=ss)
    krope_blk = jnp.pad(cache_krope[:, layer], ((0, 0), (0, 0), (ROPE_LANE0, LANES - ROPE_LANE0 - ROPE_DIM)))
    cache = _cachekv(cache_ckv[:, layer], krope_blk, w_ukv_p)
    att = _attention(seq3(q, bs, ss), seq3(kc, bs, ss), seq3(v, bs, ss), cache)
    c0 = state_C[:, layer].reshape(bs, N_DIRHEAD, MLSTM_DV, MLSTM_DK)
    n0 = state_n[:, layer].reshape(bs, N_DIRHEAD, MLSTM_DK)
    m0 = jnp.pad(state_m[:, layer].reshape(bs, 1, N_DIRHEAD), ((0, 0), (0, 0), (0, LANES - N_DIRHEAD)))
    hs = _mlstm_lat(seq3(mq, bs, ss), seq3(mk, bs, ss), seq3(mv, bs, ss), seq3(gates, bs, ss), c0, n0, m0)
    y_sample = _tail(xl, mod_lat, att.reshape(bs * ss, -1), hs.reshape(bs * ss, -1), smo, sbr, *tail_w,
                     seq_len=ss, name="tail_lat").reshape(bs, ss, d)
    return (y_prompt, y_sample, new_ckv, new_krope, new_C, new_N, new_M)
```

```python
import functools
import math

import jax
import jax.numpy as jnp
import numpy as np
from jax import lax
from jax.experimental import pallas as pl
from jax.experimental.pallas import tpu as pltpu

F32 = jnp.float32
BF16 = jnp.bfloat16

D_MODEL = 1024
GRID_W = 64
MLA_HEADS = 8
Q_RANK = 384
KV_RANK = 256
NOPE_DIM = 64
ROPE_DIM = 32
MLA_V_DIM = 64
ROPE_BASE = 10000.0
MLA_SCALE = (NOPE_DIM + ROPE_DIM) ** -0.5
MLSTM_HEADS = 4
MLSTM_DK = 128
MLSTM_DV = 256
FFN_HIDDEN = ((8 * D_MODEL // 3 + 255) // 256) * 256
EPS = 1e-6

LANES = 128
HEAD_BLOCK = LANES
ROPE_LANE0 = NOPE_DIM
N_DIRHEAD = 2 * MLSTM_HEADS
MLSTM_CHUNK = 256
Q_PRESCALE = MLA_SCALE * math.log2(math.e)
V_SLAB = 2 * LANES
V_WIDTH = (MLA_HEADS // 2) * V_SLAB
TOKEN_TILE = 512
Q_TILE = 256
VMEM_LIMIT = 56 * 1024 * 1024

_SEG_WIDTHS = (Q_RANK, KV_RANK, LANES, MLSTM_HEADS * MLSTM_DK, MLSTM_HEADS * MLSTM_DK,
               MLSTM_HEADS * MLSTM_DV, LANES, MLSTM_HEADS * MLSTM_DV, 2 * D_MODEL)
_SEG_OFF = tuple(sum(_SEG_WIDTHS[:i]) for i in range(len(_SEG_WIDTHS) + 1))
SEG_Q, SEG_KV, SEG_KPE, SEG_MQ, SEG_MK, SEG_MV, SEG_GATE, SEG_MO, SEG_BR = (
    (_SEG_OFF[i], _SEG_OFF[i + 1]) for i in range(9))
N_IN_PACKED = _SEG_OFF[-1]


def _dot(a, b):
    return jnp.dot(a, b, preferred_element_type=F32)


def _dot_nt(a, b):
    return lax.dot_general(a, b, (((1,), (1,)), ((), ())), preferred_element_type=F32)


def _dot_tn(a, b):
    return lax.dot_general(a, b, (((0,), (0,)), ((), ())), preferred_element_type=F32)


def _rms(x, g):
    ms = jnp.mean(x * x, axis=-1, keepdims=True)
    return x * lax.rsqrt(ms + EPS) * g


def _sigmoid(x):
    return 1.0 / (1.0 + jnp.exp(-x))


def _const_spec(shape):
    nd = len(shape)
    return pl.BlockSpec(shape, lambda *_: (0,) * nd, pipeline_mode=pl.Buffered(1))


def _mod_spec(mod, tiles_per_seq):
    if mod.shape[0] == 1:
        return pl.BlockSpec((None, 6, D_MODEL), lambda i: (0, 0, 0))
    return pl.BlockSpec((None, 6, D_MODEL), lambda i: (i // tiles_per_seq, 0, 0))


def _params(sem):
    return pltpu.CompilerParams(dimension_semantics=sem, vmem_limit_bytes=VMEM_LIMIT)


def _adaln_kernel(c_ref, w_ref, b_ref, o_ref):
    c = c_ref[...]
    s = c * _sigmoid(c)
    o_ref[...] = _dot(s.astype(BF16), w_ref[...].astype(BF16)) + b_ref[...]


def _adaln(cond8, w_mod, b_mod):
    n = w_mod.shape[1]
    tn = 1536
    return pl.pallas_call(
        _adaln_kernel,
        grid=(n // tn,),
        in_specs=[pl.BlockSpec((8, D_MODEL), lambda j: (0, 0)),
                  pl.BlockSpec((D_MODEL, tn), lambda j: (0, j)),
                  pl.BlockSpec((1, tn), lambda j: (0, j))],
        out_specs=pl.BlockSpec((8, tn), lambda j: (0, j)),
        out_shape=jax.ShapeDtypeStruct((8, n), F32),
        compiler_params=_params(("arbitrary",)),
        name="adaln",
    )(cond8, w_mod, b_mod)


def _rope_block(x, cos, sin_lo, sin_hi):
    return x * cos + pltpu.roll(x, LANES - 8, 1) * sin_lo + pltpu.roll(x, 8, 1) * sin_hi


def _store_kv(kvn, kp, kc_ref, v_ref):
    for hd in range(MLA_HEADS):
        sl = slice(hd * HEAD_BLOCK, (hd + 1) * HEAD_BLOCK)
        kc_ref[:, sl] = (kvn[:, sl] + kp).astype(BF16)
    v0 = MLA_HEADS * HEAD_BLOCK
    ones = jnp.ones((kvn.shape[0], LANES), BF16)
    for pair in range(MLA_HEADS // 2):
        v_ref[:, pair * V_SLAB:pair * V_SLAB + LANES] = kvn[:, v0 + pair * LANES:v0 + (pair + 1) * LANES].astype(BF16)
        v_ref[:, pair * V_SLAB + LANES:(pair + 1) * V_SLAB] = ones


def _inproj_kernel(*refs, rope):
    if rope:
        (x_ref, mod_ref, cos_ref, slo_ref, shi_ref, gmix_ref, win_ref, gq_ref, wuq_ref, gkv_ref, wukv_ref,
         bg_ref, q_ref, kc_ref, v_ref, mq_ref, mk_ref, mv_ref, gate_ref, smo_ref, sbr_ref) = refs
        cos, slo, shi = cos_ref[...], slo_ref[...], shi_ref[...]
    else:
        (x_ref, mod_ref, gmix_ref, win_ref, gq_ref, wuq_ref, gkv_ref, wukv_ref,
         bg_ref, q_ref, kc_ref, v_ref, mq_ref, mk_ref, mv_ref, gate_ref, smo_ref, sbr_ref,
         ckv_ref, kpe_ref) = refs
    mod = mod_ref[...]
    h = _rms(x_ref[...], gmix_ref[...]) * (1.0 + mod[1:2]) + mod[0:1]
    hb = h.astype(BF16)

    def proj(seg):
        return _dot(hb, win_ref[:, seg[0]:seg[1]])

    qn = _rms(proj(SEG_Q), gq_ref[...]).astype(BF16)
    q = _dot(qn, wuq_ref[...])
    for hd in range(MLA_HEADS):
        sl = slice(hd * HEAD_BLOCK, (hd + 1) * HEAD_BLOCK)
        qh = q[:, sl]
        if rope:
            qh = _rope_block(qh, cos, slo, shi)
        q_ref[:, sl] = (qh * Q_PRESCALE).astype(BF16)

    ckv = _rms(proj(SEG_KV), gkv_ref[...])
    zkpe = proj(SEG_KPE)
    if not rope:
        ckv_ref[...] = ckv
        kpe_ref[...] = zkpe[:, :ROPE_DIM]
    kp = pltpu.roll(zkpe, ROPE_LANE0, 1)
    if rope:
        kp = _rope_block(kp, cos, slo, shi)
    kvn = _dot(ckv.astype(BF16), wukv_ref[...])
    _store_kv(kvn, kp, kc_ref, v_ref)

    mq_ref[...] = proj(SEG_MQ).astype(BF16)
    mk_ref[...] = (proj(SEG_MK) * (MLSTM_DK ** -0.5)).astype(BF16)
    mv_ref[...] = proj(SEG_MV).astype(BF16)
    gate_ref[...] = proj(SEG_GATE) + bg_ref[...]
    smo_ref[...] = _sigmoid(proj(SEG_MO)).astype(BF16)
    sbr_ref[...] = _sigmoid(proj(SEG_BR)).astype(BF16)


def _inproj(x, mod, rope_tabs, gmix, w_in_p, gq, w_uq_p, gkv, w_ukv_p, bg, *, seq_len):
    n_tok = x.shape[0]
    tm = TOKEN_TILE
    tiles_per_seq = seq_len // tm
    rope = rope_tabs is not None
    tok = lambda c: pl.BlockSpec((tm, c), lambda i: (i, 0))
    in_specs = [tok(D_MODEL), _mod_spec(mod, tiles_per_seq)]
    args = [x, mod]
    if rope:
        in_specs += [pl.BlockSpec((tm, LANES), lambda i: (i % tiles_per_seq, 0))] * 3
        args += list(rope_tabs)
    in_specs += [_const_spec(a.shape) for a in (gmix, w_in_p, gq, w_uq_p, gkv, w_ukv_p, bg)]
    args += [gmix, w_in_p, gq, w_uq_p, gkv, w_ukv_p, bg]
    widths = [(MLA_HEADS * HEAD_BLOCK, BF16), (MLA_HEADS * HEAD_BLOCK, BF16), (V_WIDTH, BF16),
              (MLSTM_HEADS * MLSTM_DK, BF16), (MLSTM_HEADS * MLSTM_DK, BF16), (MLSTM_HEADS * MLSTM_DV, BF16),
              (LANES, F32), (MLSTM_HEADS * MLSTM_DV, BF16), (2 * D_MODEL, BF16)]
    if not rope:
        widths += [(KV_RANK, F32), (ROPE_DIM, F32)]
    return pl.pallas_call(
        functools.partial(_inproj_kernel, rope=rope),
        grid=(n_tok // tm,),
        in_specs=in_specs,
        out_specs=[tok(c) for c, _ in widths],
        out_shape=[jax.ShapeDtypeStruct((n_tok, c), dt) for c, dt in widths],
        compiler_params=_params(("parallel",)),
        name="inproj_lat" if rope else "inproj_ctx",
    )(*args)


def _cachekv_kernel(ckv_ref, kp_ref, wukv_ref, kc_ref, v_ref):
    kvn = _dot(ckv_ref[...].astype(BF16), wukv_ref[...])
    _store_kv(kvn, kp_ref[...], kc_ref, v_ref)


def _cachekv(cache_ckv, krope_blk, w_ukv_p):
    b, past, _ = cache_ckv.shape
    return pl.pallas_call(
        _cachekv_kernel,
        grid=(b,),
        in_specs=[pl.BlockSpec((None, past, KV_RANK), lambda i: (i, 0, 0)),
                  pl.BlockSpec((None, past, LANES), lambda i: (i, 0, 0)),
                  _const_spec(w_ukv_p.shape)],
        out_specs=[pl.BlockSpec((None, past, MLA_HEADS * HEAD_BLOCK), lambda i: (i, 0, 0)),
                   pl.BlockSpec((None, past, V_WIDTH), lambda i: (i, 0, 0))],
        out_shape=[jax.ShapeDtypeStruct((b, past, MLA_HEADS * HEAD_BLOCK), BF16),
                   jax.ShapeDtypeStruct((b, past, V_WIDTH), BF16)],
        compiler_params=_params(("parallel",)),
        name="cachekv",
    )(cache_ckv, krope_blk, w_ukv_p)


def _attn_kernel(*refs, has_cache):
    if has_cache:
        q_ref, k_ref, v_ref, kc_ref, vc_ref, o_ref = refs
    else:
        q_ref, k_ref, v_ref, o_ref = refs
    tq = q_ref.shape[0]
    lane = lax.broadcasted_iota(jnp.int32, (tq, LANES), 1)
    for pair in range(MLA_HEADS // 2):
        vsl = slice(pair * V_SLAB, (pair + 1) * V_SLAB)
        outs = []
        for j in range(2):
            hd = 2 * pair + j
            sl = slice(hd * HEAD_BLOCK, (hd + 1) * HEAD_BLOCK)
            qh = q_ref[:, sl]
            s = _dot_nt(qh, k_ref[:, sl])
            m = jnp.max(s, axis=-1, keepdims=True)
            if has_cache:
                sc = _dot_nt(qh, kc_ref[:, sl])
                m = jnp.maximum(m, jnp.max(sc, axis=-1, keepdims=True))
            o = _dot(jnp.exp2(s - m).astype(BF16), v_ref[:, vsl])
            if has_cache:
                o = o + _dot(jnp.exp2(sc - m).astype(BF16), vc_ref[:, vsl])
            outs.append(o[:, :LANES] * (1.0 / o[:, LANES:]))
        o_ref[:, pair * LANES:(pair + 1) * LANES] = jnp.where(lane < MLA_V_DIM, outs[0], outs[1]).astype(BF16)


def _attention(q, k, v, cache=None):
    b, s, _ = q.shape
    tq = Q_TILE
    kw, vw, ow = MLA_HEADS * HEAD_BLOCK, V_WIDTH, MLA_HEADS * MLA_V_DIM
    in_specs = [pl.BlockSpec((None, tq, kw), lambda i, j: (i, j, 0)),
                pl.BlockSpec((None, s, kw), lambda i, j: (i, 0, 0)),
                pl.BlockSpec((None, s, vw), lambda i, j: (i, 0, 0))]
    args = [q, k, v]
    if cache is not None:
        past = cache[0].shape[1]
        in_specs += [pl.BlockSpec((None, past, kw), lambda i, j: (i, 0, 0)),
                     pl.BlockSpec((None, past, vw), lambda i, j: (i, 0, 0))]
        args += list(cache)
    return pl.pallas_call(
        functools.partial(_attn_kernel, has_cache=cache is not None),
        grid=(b, s // tq),
        in_specs=in_specs,
        out_specs=pl.BlockSpec((None, tq, ow), lambda i, j: (i, j, 0)),
        out_shape=jax.ShapeDtypeStruct((b, s, ow), BF16),
        compiler_params=_params(("parallel", "arbitrary")),
        name="attn_lat" if cache is not None else "attn_ctx",
    )(*args)


def _lane_pick(x, k):
    lane = lax.broadcasted_iota(jnp.int32, x.shape, 1)
    return jnp.sum(jnp.where(lane == k, x, 0.0), axis=1, keepdims=True)


def _prefix_sum_rows(x):
    n = x.shape[0]
    r = lax.broadcasted_iota(jnp.int32, (n, n), 0)
    c = lax.broadcasted_iota(jnp.int32, (n, n), 1)
    tri = jnp.where(c <= r, 1.0, 0.0).astype(BF16)
    hi = x.astype(BF16)
    r1 = x - hi.astype(F32)
    mid = r1.astype(BF16)
    lo = (r1 - mid.astype(F32)).astype(BF16)
    return _dot(tri, hi) + _dot(tri, mid) + _dot(tri, lo)


def _gate_prep(g):
    n = g.shape[0]
    lane = lax.broadcasted_iota(jnp.int32, (n, LANES), 1)
    fpre = pltpu.roll(g, LANES - N_DIRHEAD, 1)
    lf = jnp.minimum(fpre, 0.0) - jnp.log(1.0 + jnp.exp(-jnp.abs(fpre)))
    binc = _prefix_sum_rows(lf)
    tot = binc[n - 1:n, :]
    b = jnp.where(lane < MLSTM_HEADS, binc, tot - binc + lf)
    u = g - b
    return b, u, u.T, tot


def _mlstm_dir(s0, q, k32, v, u_row, u_col, b_col, tot_k, *, fwd, state):
    n = s0.shape[0]
    r = lax.broadcasted_iota(jnp.int32, (n, n), 0)
    c = lax.broadcasted_iota(jnp.int32, (n, n), 1)
    mask = (c <= r) if fwd else (c >= r)
    a = jnp.where(mask, u_row, -jnp.inf)
    m_prev = 0.0 if state is None else state[2]
    g = jnp.maximum(jnp.max(a, axis=1, keepdims=True), m_prev)
    g_last = jnp.maximum(jnp.max(u_row, axis=1, keepdims=True), m_prev)
    sd = s0 * jnp.exp(a - g)
    den = jnp.sum(sd, axis=1, keepdims=True)
    num = _dot(sd.astype(BF16), v)
    if state is not None:
        c_prev, n_prev, _ = state
        w_inter = jnp.exp(m_prev - g)
        num = num + w_inter * _dot_nt(q, c_prev.astype(BF16))
        den = den + w_inter * jnp.sum(q.astype(F32) * n_prev, axis=1, keepdims=True)
    floor = jnp.exp(-(b_col + g))
    h = num * (1.0 / jnp.maximum(jnp.abs(den), floor))
    wk = jnp.exp(u_col - g_last) * k32
    c_new = _dot_tn(v, wk.astype(BF16))
    n_new = jnp.sum(wk, axis=0, keepdims=True)
    if state is not None:
        decay = jnp.exp(m_prev - g_last)
        c_new = decay * c_prev + c_new
        n_new = decay * n_prev + n_new
    return h, c_new, n_new, tot_k + g_last


def _mlstm_ctx_kernel(q_ref, k_ref, v_ref, g_ref, h_ref, c_ref, n_ref, m_ref):
    b, u, ut, tot = _gate_prep(g_ref[...])
    lane = lax.broadcasted_iota(jnp.int32, (1, LANES), 1)
    m_row = jnp.zeros((1, LANES), F32)
    for hd in range(MLSTM_HEADS):
        q = q_ref[:, hd * MLSTM_DK:(hd + 1) * MLSTM_DK]
        k = k_ref[:, hd * MLSTM_DK:(hd + 1) * MLSTM_DK]
        v = v_ref[:, hd * MLSTM_DV:(hd + 1) * MLSTM_DV]
        s0 = _dot_nt(q, k)
        k32 = k.astype(F32)
        h_sum = None
        for fwd in (True, False):
            kk = hd if fwd else MLSTM_HEADS + hd
            h, c_new, n_new, m_new = _mlstm_dir(
                s0, q, k32, v, ut[kk:kk + 1, :], _lane_pick(u, kk), _lane_pick(b, kk), _lane_pick(tot, kk),
                fwd=fwd, state=None)
            h_sum = h if h_sum is None else h_sum + h
            c_ref[kk] = c_new
            n_ref[kk:kk + 1, :] = n_new
            m_row = jnp.where(lane == kk, m_new, m_row)
        h_ref[:, hd * MLSTM_DV:(hd + 1) * MLSTM_DV] = h_sum
    m_ref[...] = m_row


def _mlstm_ctx(mq, mk, mv, gates):
    b, s, _ = mq.shape
    seq = lambda c: pl.BlockSpec((None, s, c), lambda i: (i, 0, 0))
    return pl.pallas_call(
        _mlstm_ctx_kernel,
        grid=(b,),
        in_specs=[seq(MLSTM_HEADS * MLSTM_DK), seq(MLSTM_HEADS * MLSTM_DK), seq(MLSTM_HEADS * MLSTM_DV), seq(LANES)],
        out_specs=[seq(MLSTM_HEADS * MLSTM_DV),
                   pl.BlockSpec((None, N_DIRHEAD, MLSTM_DV, MLSTM_DK), lambda i: (i, 0, 0, 0)),
                   pl.BlockSpec((None, N_DIRHEAD, MLSTM_DK), lambda i: (i, 0, 0)),
                   pl.BlockSpec((None, 1, LANES), lambda i: (i, 0, 0))],
        out_shape=[jax.ShapeDtypeStruct((b, s, MLSTM_HEADS * MLSTM_DV), F32),
                   jax.ShapeDtypeStruct((b, N_DIRHEAD, MLSTM_DV, MLSTM_DK), F32),
                   jax.ShapeDtypeStruct((b, N_DIRHEAD, MLSTM_DK), F32),
                   jax.ShapeDtypeStruct((b, 1, LANES), F32)],
        compiler_params=_params(("parallel",)),
        name="mlstm_ctx",
    )(mq, mk, mv, gates)


def _mlstm_lat_kernel(qf_ref, kf_ref, vf_ref, gf_ref, qb_ref, kb_ref, vb_ref, gb_ref, c0_ref, n0_ref, m0_ref,
                      h_ref, c_ref, n_ref, m_ref, *, n_chunks):
    step = pl.program_id(1)

    @pl.when(step == 0)
    def _():
        c_ref[...] = c0_ref[...]
        n_ref[...] = n0_ref[...]
        m_ref[...] = m0_ref[...]
        h_ref[...] = jnp.zeros_like(h_ref)

    lane = lax.broadcasted_iota(jnp.int32, (1, LANES), 1)
    m_old = m_ref[...]
    m_row = m_old
    for fwd in (True, False):
        q_ref, k_ref, v_ref, g_ref = (qf_ref, kf_ref, vf_ref, gf_ref) if fwd else (qb_ref, kb_ref, vb_ref, gb_ref)
        chunk = step if fwd else n_chunks - 1 - step
        row0 = pl.multiple_of(chunk * MLSTM_CHUNK, MLSTM_CHUNK)
        b, u, ut, tot = _gate_prep(g_ref[...])
        for hd in range(MLSTM_HEADS):
            kk = hd if fwd else MLSTM_HEADS + hd
            q = q_ref[:, hd * MLSTM_DK:(hd + 1) * MLSTM_DK]
            k = k_ref[:, hd * MLSTM_DK:(hd + 1) * MLSTM_DK]
            v = v_ref[:, hd * MLSTM_DV:(hd + 1) * MLSTM_DV]
            s0 = _dot_nt(q, k)
            state = (c_ref[kk], n_ref[kk:kk + 1, :], _lane_pick(m_old, kk))
            h, c_new, n_new, m_new = _mlstm_dir(
                s0, q, k.astype(F32), v, ut[kk:kk + 1, :], _lane_pick(u, kk), _lane_pick(b, kk),
                _lane_pick(tot, kk), fwd=fwd, state=state)
            c_ref[kk] = c_new
            n_ref[kk:kk + 1, :] = n_new
            m_row = jnp.where(lane == kk, m_new, m_row)
            h_ref[pl.ds(row0, MLSTM_CHUNK), hd * MLSTM_DV:(hd + 1) * MLSTM_DV] += h
    m_ref[...] = m_row


def _mlstm_lat(mq, mk, mv, gates, c0, n0, m0):
    b, s, _ = mq.shape
    lc = MLSTM_CHUNK
    nc = s // lc
    fw = lambda c: pl.BlockSpec((None, lc, c), lambda i, j: (i, j, 0))
    bw = lambda c: pl.BlockSpec((None, lc, c), lambda i, j: (i, nc - 1 - j, 0))
    dk, dv = MLSTM_HEADS * MLSTM_DK, MLSTM_HEADS * MLSTM_DV
    return pl.pallas_call(
        functools.partial(_mlstm_lat_kernel, n_chunks=nc),
        grid=(b, nc),
        in_specs=[fw(dk), fw(dk), fw(dv), fw(LANES), bw(dk), bw(dk), bw(dv), bw(LANES),
                  pl.BlockSpec((None, N_DIRHEAD, MLSTM_DV, MLSTM_DK), lambda i, j: (i, 0, 0, 0)),
                  pl.BlockSpec((None, N_DIRHEAD, MLSTM_DK), lambda i, j: (i, 0, 0)),
                  pl.BlockSpec((None, 1, LANES), lambda i, j: (i, 0, 0))],
        out_specs=pl.BlockSpec((None, s, dv), lambda i, j: (i, 0, 0)),
        out_shape=jax.ShapeDtypeStruct((b, s, dv), F32),
        scratch_shapes=[pltpu.VMEM((N_DIRHEAD, MLSTM_DV, MLSTM_DK), F32),
                        pltpu.VMEM((N_DIRHEAD, MLSTM_DK), F32),
                        pltpu.VMEM((1, LANES), F32)],
        compiler_params=_params(("parallel", "arbitrary")),
        name="mlstm_lat",
    )(mq, mk, mv, gates, mq, mk, mv, gates, c0, n0, m0)


def _tail_kernel(x_ref, mod_ref, att_ref, hs_ref, smo_ref, sbr_ref, gml_ref, womla_ref, womlstm_ref, wout_ref,
                 gffn_ref, wfin_ref, wfout_ref, gfin_ref, y_ref):
    mod = mod_ref[...]
    a = _dot(att_ref[...], womla_ref[...])
    gml = gml_ref[...]
    parts = []
    for hd in range(MLSTM_HEADS):
        sl = slice(hd * MLSTM_DV, (hd + 1) * MLSTM_DV)
        parts.append((_rms(hs_ref[:, sl], gml[:, sl]) * smo_ref[:, sl].astype(F32)).astype(BF16))
    bm = _dot(jnp.concatenate(parts, axis=1), womlstm_ref[...])
    merged = sbr_ref[:, :D_MODEL].astype(F32) * a + sbr_ref[:, D_MODEL:].astype(F32) * bm
    x1 = x_ref[...] + mod[2:3] * _dot(merged.astype(BF16), wout_ref[...])
    h2 = (_rms(x1, gffn_ref[...]) * (1.0 + mod[4:5]) + mod[3:4]).astype(BF16)
    ga = _dot(h2, wfin_ref[:, :FFN_HIDDEN])
    gu = _dot(h2, wfin_ref[:, FFN_HIDDEN:])
    act = (ga * _sigmoid(ga) * gu).astype(BF16)
    x2 = x1 + mod[5:6] * _dot(act, wfout_ref[...])
    y_ref[...] = _rms(x2, gfin_ref[...])


def _tail(x, mod, att, hs, smo, sbr, gml, w_o_mla, w_o_mlstm, w_out, gffn, w_ffn_in, w_ffn_out, gfin, *, seq_len, name):
    n_tok = x.shape[0]
    tm = TOKEN_TILE
    tiles_per_seq = seq_len // tm
    tok = lambda c: pl.BlockSpec((tm, c), lambda i: (i, 0))
    consts = (gml, w_o_mla, w_o_mlstm, w_out, gffn, w_ffn_in, w_ffn_out, gfin)
    return pl.pallas_call(
        _tail_kernel,
        grid=(n_tok // tm,),
        in_specs=[tok(D_MODEL), _mod_spec(mod, tiles_per_seq),
                  tok(att.shape[1]), tok(hs.shape[1]), tok(smo.shape[1]), tok(sbr.shape[1])]
                 + [_const_spec(a.shape) for a in consts],
        out_specs=tok(D_MODEL),
        out_shape=jax.ShapeDtypeStruct((n_tok, D_MODEL), F32),
        compiler_params=_params(("parallel",)),
        name=name,
    )(x, mod, att, hs, smo, sbr, *consts)


_IN_SIZES = (Q_RANK, KV_RANK, ROPE_DIM, MLSTM_HEADS * MLSTM_DK, MLSTM_HEADS * MLSTM_DK, MLSTM_HEADS * MLSTM_DV,
             4 * MLSTM_HEADS, MLSTM_HEADS * MLSTM_DV, 2 * D_MODEL)
_IN_OFF = tuple(sum(_IN_SIZES[:i]) for i in range(len(_IN_SIZES) + 1))
PACK_ROWS = 128


def _pack_in_kernel(w_ref, o_ref):
    lane = lax.broadcasted_iota(jnp.int32, (PACK_ROWS, LANES), 1)
    w = w_ref[...]
    for i, width in enumerate(_IN_SIZES):
        src, dst = _IN_OFF[i], _SEG_OFF[i]
        if width % LANES == 0:
            o_ref[:, dst:dst + width] = w[:, src:src + width].astype(BF16)
        else:
            blk = w[:, src:src + LANES]
            o_ref[:, dst:dst + LANES] = jnp.where(lane < width, blk, 0.0).astype(BF16)


def _pack_in(w_in):
    k, n = w_in.shape
    return pl.pallas_call(
        _pack_in_kernel,
        grid=(k // PACK_ROWS,),
        in_specs=[pl.BlockSpec((PACK_ROWS, n), lambda i: (i, 0))],
        out_specs=pl.BlockSpec((PACK_ROWS, N_IN_PACKED), lambda i: (i, 0)),
        out_shape=jax.ShapeDtypeStruct((k, N_IN_PACKED), BF16),
        compiler_params=_params(("parallel",)),
        name="pack_in",
    )(w_in)


def _pack_small(w_uq, w_ukv, b_gates):
    qd = NOPE_DIM + ROPE_DIM
    w_uq_p = jnp.pad(w_uq.reshape(Q_RANK, MLA_HEADS, qd), ((0, 0), (0, 0), (0, HEAD_BLOCK - qd)))
    w_uq_p = w_uq_p.reshape(Q_RANK, MLA_HEADS * HEAD_BLOCK).astype(BF16)
    kv = w_ukv.reshape(KV_RANK, MLA_HEADS, NOPE_DIM + MLA_V_DIM)
    wk = jnp.pad(kv[:, :, :NOPE_DIM], ((0, 0), (0, 0), (0, HEAD_BLOCK - NOPE_DIM))).reshape(KV_RANK, -1)
    wv = kv[:, :, NOPE_DIM:].reshape(KV_RANK, -1)
    w_ukv_p = jnp.concatenate([wk, wv], axis=1).astype(BF16)
    bg = jnp.pad(b_gates, ((0, 0), (0, LANES - b_gates.shape[1])))
    return w_uq_p, w_ukv_p, bg


def _rope_tables(n_tokens):
    pos = np.arange(n_tokens)
    row = (pos // GRID_W).astype(np.float64)[:, None]
    col = (pos % GRID_W).astype(np.float64)[:, None]
    half = ROPE_DIM // 2
    inv = (np.float32(ROPE_BASE) ** (-np.arange(0, half, 2, dtype=np.float32) / np.float32(half))).astype(np.float64)
    r = np.arange(LANES) - ROPE_LANE0
    in_rope = (r >= 0) & (r < ROPE_DIM)
    rr = np.clip(r, 0, ROPE_DIM - 1)
    freq = inv[rr % (half // 2)][None, :]
    ang = np.where((rr // half == 0)[None, :], row * freq, col * freq).astype(np.float32).astype(np.float64)
    first = (rr % half) < (half // 2)
    cos = np.where(in_rope[None, :], np.cos(ang), 1.0)
    sin = np.sin(ang)
    sin_lo = np.where((in_rope & first)[None, :], -sin, 0.0)
    sin_hi = np.where((in_rope & ~first)[None, :], sin, 0.0)
    return tuple(jnp.asarray(t, dtype=F32) for t in (cos, sin_lo, sin_hi))


def kernel(x_prompt, x_sample, cache_ckv, cache_krope, state_C, state_n, state_m, c, c_ctx, w_mod, b_mod, g_norm_mix,
           w_in, b_gates, g_q_norm, w_uq, g_kv_norm, w_ukv, g_mlstm_norm, w_o_mla, w_o_mlstm, w_out, g_norm_ffn,
           w_ffn_in, w_ffn_out, g_final):
    bp, sp, d = x_prompt.shape
    bs, ss, _ = x_sample.shape
    layer = 0
    assert w_mod.shape[0] == 1 and sp == MLSTM_CHUNK and ss % MLSTM_CHUNK == 0

    cond = jnp.concatenate([c_ctx[None, :], c, jnp.zeros((8 - 1 - bs, d), F32)], axis=0)
    mod = _adaln(cond, w_mod[layer], b_mod[layer][None, :]).reshape(8, 6, d)
    mod_ctx, mod_lat = mod[0:1], mod[1:1 + bs]

    w_in_p = _pack_in(w_in[layer])
    w_uq_p, w_ukv_p, bg = _pack_small(w_uq[layer], w_ukv[layer], b_gates[layer][None, :])
    row = lambda g: g[layer][None, :]
    shared_in = (row(g_norm_mix), w_in_p, row(g_q_norm), w_uq_p, row(g_kv_norm), w_ukv_p, bg)
    tail_w = (row(g_mlstm_norm), w_o_mla[layer].astype(BF16), w_o_mlstm[layer].astype(BF16), w_out[layer].astype(BF16),
              row(g_norm_ffn), w_ffn_in[layer].astype(BF16), w_ffn_out[layer].astype(BF16), g_final[None, :])

    xc = x_prompt.reshape(bp * sp, d)
    q, kc, v, mq, mk, mv, gates, smo, sbr, ckv, kpe = _inproj(xc, mod_ctx, None, *shared_in, seq_len=sp)
    seq3 = lambda a, b_, s_: a.reshape(b_, s_, a.shape[-1])
    att = _attention(seq3(q, bp, sp), seq3(kc, bp, sp), seq3(v, bp, sp))
    hs, new_c, new_n, new_m = _mlstm_ctx(seq3(mq, bp, sp), seq3(mk, bp, sp), seq3(mv, bp, sp), seq3(gates, bp, sp))
    y_prompt = _tail(xc, mod_ctx, att.reshape(bp * sp, -1), hs.reshape(bp * sp, -1), smo, sbr, *tail_w,
                     seq_len=sp, name="tail_ctx").reshape(bp, sp, d)
    new_ckv = ckv.reshape(bp, 1, sp, KV_RANK)
    new_krope = kpe.reshape(bp, 1, sp, ROPE_DIM)
    new_C = new_c.reshape(bp, 1, 2, MLSTM_HEADS, MLSTM_DV, MLSTM_DK)
    new_N = new_n.reshape(bp, 1, 2, MLSTM_HEADS, MLSTM_DK)
    new_M = new_m[:, 0, :N_DIRHEAD].reshape(bp, 1, 2, MLSTM_HEADS)

    xl = x_sample.reshape(bs * ss, d)
    q, kc, v, mq, mk, mv, gates, smo, sbr = _inproj(xl, mod_lat, _rope_tables(ss), *shared_in, seq_len=ss)
    krope_blk = jnp.pad(cache_krope[:, layer], ((0, 0), (0, 0), (ROPE_LANE0, LANES - ROPE_LANE0 - ROPE_DIM)))
    cache = _cachekv(cache_ckv[:, layer], krope_blk, w_ukv_p)
    att = _attention(seq3(q, bs, ss), seq3(kc, bs, ss), seq3(v, bs, ss), cache)
    c0 = state_C[:, layer].reshape(bs, N_DIRHEAD, MLSTM_DV, MLSTM_DK)
    n0 = state_n[:, layer].reshape(bs, N_DIRHEAD, MLSTM_DK)
    m0 = jnp.pad(state_m[:, layer].reshape(bs, 1, N_DIRHEAD), ((0, 0), (0, 0), (0, LANES - N_DIRHEAD)))
    hs = _mlstm_lat(seq3(mq, bs, ss), seq3(mk, bs, ss), seq3(mv, bs, ss), seq3(gates, bs, ss), c0, n0, m0)
    y_sample = _tail(xl, mod_lat, att.reshape(bs * ss, -1), hs.reshape(bs * ss, -1), smo, sbr, *tail_w,
                     seq_len=ss, name="tail_lat").reshape(bs, ss, d)
    return (y_prompt, y_sample, new_ckv, new_krope, new_C, new_N, new_M)
```

```python
import functools
import math

import jax
import jax.numpy as jnp
import numpy as np
from jax import lax
from jax.experimental import pallas as pl
from jax.experimental.pallas import tpu as pltpu

F32 = jnp.float32
BF16 = jnp.bfloat16

D_MODEL = 1024
GRID_W = 64
MLA_HEADS = 8
Q_RANK = 384
KV_RANK = 256
NOPE_DIM = 64
ROPE_DIM = 32
MLA_V_DIM = 64
ROPE_BASE = 10000.0
MLA_SCALE = (NOPE_DIM + ROPE_DIM) ** -0.5
MLSTM_HEADS = 4
MLSTM_DK = 128
MLSTM_DV = 256
FFN_HIDDEN = ((8 * D_MODEL // 3 + 255) // 256) * 256
EPS = 1e-6

LANES = 128
HEAD_BLOCK = LANES
ROPE_LANE0 = NOPE_DIM
N_DIRHEAD = 2 * MLSTM_HEADS
MLSTM_CHUNK = 256
Q_PRESCALE = MLA_SCALE * math.log2(math.e)
V_SLAB = 2 * LANES
V_WIDTH = (MLA_HEADS // 2) * V_SLAB
TOKEN_TILE = 512
Q_TILE = 256
VMEM_LIMIT = 56 * 1024 * 1024

_SEG_WIDTHS = (Q_RANK, KV_RANK, LANES, MLSTM_HEADS * MLSTM_DK, MLSTM_HEADS * MLSTM_DK,
               MLSTM_HEADS * MLSTM_DV, LANES, MLSTM_HEADS * MLSTM_DV, 2 * D_MODEL)
_SEG_OFF = tuple(sum(_SEG_WIDTHS[:i]) for i in range(len(_SEG_WIDTHS) + 1))
SEG_Q, SEG_KV, SEG_KPE, SEG_MQ, SEG_MK, SEG_MV, SEG_GATE, SEG_MO, SEG_BR = (
    (_SEG_OFF[i], _SEG_OFF[i + 1]) for i in range(9))
N_IN_PACKED = _SEG_OFF[-1]


def _dot(a, b):
    return jnp.dot(a, b, preferred_element_type=F32)


def _dot_nt(a, b):
    return lax.dot_general(a, b, (((1,), (1,)), ((), ())), preferred_element_type=F32)


def _dot_tn(a, b):
    return lax.dot_general(a, b, (((0,), (0,)), ((), ())), preferred_element_type=F32)


def _rms(x, g):
    ms = jnp.mean(x * x, axis=-1, keepdims=True)
    return x * lax.rsqrt(ms + EPS) * g


def _sigmoid(x):
    return 1.0 / (1.0 + jnp.exp(-x))


def _const_spec(shape):
    nd = len(shape)
    return pl.BlockSpec(shape, lambda *_: (0,) * nd, pipeline_mode=pl.Buffered(1))


def _mod_spec(mod, tiles_per_seq):
    if mod.shape[0] == 1:
        return pl.BlockSpec((None, 6, D_MODEL), lambda i: (0, 0, 0))
    return pl.BlockSpec((None, 6, D_MODEL), lambda i: (i // tiles_per_seq, 0, 0))


def _params(sem):
    return pltpu.CompilerParams(dimension_semantics=sem, vmem_limit_bytes=VMEM_LIMIT)


def _adaln_kernel(c_ref, w_ref, b_ref, o_ref):
    c = c_ref[...]
    s = c * _sigmoid(c)
    o_ref[...] = _dot(s.astype(BF16), w_ref[...].astype(BF16)) + b_ref[...]


def _adaln(cond8, w_mod, b_mod):
    n = w_mod.shape[1]
    tn = 1536
    return pl.pallas_call(
        _adaln_kernel,
        grid=(n // tn,),
        in_specs=[pl.BlockSpec((8, D_MODEL), lambda j: (0, 0)),
                  pl.BlockSpec((D_MODEL, tn), lambda j: (0, j)),
                  pl.BlockSpec((1, tn), lambda j: (0, j))],
        out_specs=pl.BlockSpec((8, tn), lambda j: (0, j)),
        out_shape=jax.ShapeDtypeStruct((8, n), F32),
        compiler_params=_params(("arbitrary",)),
        name="adaln",
    )(cond8, w_mod, b_mod)


def _rope_block(x, cos, sin_lo, sin_hi):
    return x * cos + pltpu.roll(x, LANES - 8, 1) * sin_lo + pltpu.roll(x, 8, 1) * sin_hi


def _store_kv(kvn, kp, kc_ref, v_ref):
    for hd in range(MLA_HEADS):
        sl = slice(hd * HEAD_BLOCK, (hd + 1) * HEAD_BLOCK)
        kc_ref[:, sl] = (kvn[:, sl] + kp).astype(BF16)
    v0 = MLA_HEADS * HEAD_BLOCK
    ones = jnp.ones((kvn.shape[0], LANES), BF16)
    for pair in range(MLA_HEADS // 2):
        v_ref[:, pair * V_SLAB:pair * V_SLAB + LANES] = kvn[:, v0 + pair * LANES:v0 + (pair + 1) * LANES].astype(BF16)
        v_ref[:, pair * V_SLAB + LANES:(pair + 1) * V_SLAB] = ones


def _inproj_kernel(*refs, rope):
    if rope:
        (x_ref, mod_ref, cos_ref, slo_ref, shi_ref, gmix_ref, win_ref, gq_ref, wuq_ref, gkv_ref, wukv_ref,
         bg_ref, q_ref, kc_ref, v_ref, mq_ref, mk_ref, mv_ref, gate_ref, smo_ref, sbr_ref) = refs
        cos, slo, shi = cos_ref[...], slo_ref[...], shi_ref[...]
    else:
        (x_ref, mod_ref, gmix_ref, win_ref, gq_ref, wuq_ref, gkv_ref, wukv_ref,
         bg_ref, q_ref, kc_ref, v_ref, mq_ref, mk_ref, mv_ref, gate_ref, smo_ref, sbr_ref,
         ckv_ref, kpe_ref) = refs
    mod = mod_ref[...]
    h = _rms(x_ref[...], gmix_ref[...]) * (1.0 + mod[1:2]) + mod[0:1]
    hb = h.astype(BF16)

    def proj(seg):
        return _dot_nt(hb, win_ref[seg[0]:seg[1], :])

    qn = _rms(proj(SEG_Q), gq_ref[...]).astype(BF16)
    q = _dot(qn, wuq_ref[...])
    for hd in range(MLA_HEADS):
        sl = slice(hd * HEAD_BLOCK, (hd + 1) * HEAD_BLOCK)
        qh = q[:, sl]
        if rope:
            qh = _rope_block(qh, cos, slo, shi)
        q_ref[:, sl] = (qh * Q_PRESCALE).astype(BF16)

    ckv = _rms(proj(SEG_KV), gkv_ref[...])
    zkpe = proj(SEG_KPE)
    if not rope:
        ckv_ref[...] = ckv
        kpe_ref[...] = zkpe[:, :ROPE_DIM]
    kp = pltpu.roll(zkpe, ROPE_LANE0, 1)
    if rope:
        kp = _rope_block(kp, cos, slo, shi)
    kvn = _dot(ckv.astype(BF16), wukv_ref[...])
    _store_kv(kvn, kp, kc_ref, v_ref)

    mq_ref[...] = proj(SEG_MQ).astype(BF16)
    mk_ref[...] = (proj(SEG_MK) * (MLSTM_DK ** -0.5)).astype(BF16)
    mv_ref[...] = proj(SEG_MV).astype(BF16)
    gate_ref[...] = proj(SEG_GATE) + bg_ref[...]
    smo_ref[...] = _sigmoid(proj(SEG_MO)).astype(BF16)
    sbr_ref[...] = _sigmoid(proj(SEG_BR)).astype(BF16)


def _inproj(x, mod, rope_tabs, gmix, w_in_p, gq, w_uq_p, gkv, w_ukv_p, bg, *, seq_len):
    n_tok = x.shape[0]
    tm = TOKEN_TILE
    tiles_per_seq = seq_len // tm
    rope = rope_tabs is not None
    tok = lambda c: pl.BlockSpec((tm, c), lambda i: (i, 0))
    in_specs = [tok(D_MODEL), _mod_spec(mod, tiles_per_seq)]
    args = [x, mod]
    if rope:
        in_specs += [pl.BlockSpec((tm, LANES), lambda i: (i % tiles_per_seq, 0))] * 3
        args += list(rope_tabs)
    in_specs += [_const_spec(a.shape) for a in (gmix, w_in_p, gq, w_uq_p, gkv, w_ukv_p, bg)]
    args += [gmix, w_in_p, gq, w_uq_p, gkv, w_ukv_p, bg]
    widths = [(MLA_HEADS * HEAD_BLOCK, BF16), (MLA_HEADS * HEAD_BLOCK, BF16), (V_WIDTH, BF16),
              (MLSTM_HEADS * MLSTM_DK, BF16), (MLSTM_HEADS * MLSTM_DK, BF16), (MLSTM_HEADS * MLSTM_DV, BF16),
              (LANES, F32), (MLSTM_HEADS * MLSTM_DV, BF16), (2 * D_MODEL, BF16)]
    if not rope:
        widths += [(KV_RANK, F32), (ROPE_DIM, F32)]
    return pl.pallas_call(
        functools.partial(_inproj_kernel, rope=rope),
        grid=(n_tok // tm,),
        in_specs=in_specs,
        out_specs=[tok(c) for c, _ in widths],
        out_shape=[jax.ShapeDtypeStruct((n_tok, c), dt) for c, dt in widths],
        compiler_params=_params(("parallel",)),
        name="inproj_lat" if rope else "inproj_ctx",
    )(*args)


def _cachekv_kernel(ckv_ref, kp_ref, wukv_ref, kc_ref, v_ref):
    kvn = _dot(ckv_ref[...].astype(BF16), wukv_ref[...])
    _store_kv(kvn, kp_ref[...], kc_ref, v_ref)


def _cachekv(cache_ckv, krope_blk, w_ukv_p):
    b, past, _ = cache_ckv.shape
    return pl.pallas_call(
        _cachekv_kernel,
        grid=(b,),
        in_specs=[pl.BlockSpec((None, past, KV_RANK), lambda i: (i, 0, 0)),
                  pl.BlockSpec((None, past, LANES), lambda i: (i, 0, 0)),
                  _const_spec(w_ukv_p.shape)],
        out_specs=[pl.BlockSpec((None, past, MLA_HEADS * HEAD_BLOCK), lambda i: (i, 0, 0)),
                   pl.BlockSpec((None, past, V_WIDTH), lambda i: (i, 0, 0))],
        out_shape=[jax.ShapeDtypeStruct((b, past, MLA_HEADS * HEAD_BLOCK), BF16),
                   jax.ShapeDtypeStruct((b, past, V_WIDTH), BF16)],
        compiler_params=_params(("parallel",)),
        name="cachekv",
    )(cache_ckv, krope_blk, w_ukv_p)


def _attn_kernel(*refs, has_cache):
    if has_cache:
        q_ref, k_ref, v_ref, kc_ref, vc_ref, o_ref = refs
    else:
        q_ref, k_ref, v_ref, o_ref = refs
    tq = q_ref.shape[0]
    lane = lax.broadcasted_iota(jnp.int32, (tq, LANES), 1)
    for pair in range(MLA_HEADS // 2):
        vsl = slice(pair * V_SLAB, (pair + 1) * V_SLAB)
        outs = []
        for j in range(2):
            hd = 2 * pair + j
            sl = slice(hd * HEAD_BLOCK, (hd + 1) * HEAD_BLOCK)
            qh = q_ref[:, sl]
            s = _dot_nt(qh, k_ref[:, sl])
            m = jnp.max(s, axis=-1, keepdims=True)
            if has_cache:
                sc = _dot_nt(qh, kc_ref[:, sl])
                m = jnp.maximum(m, jnp.max(sc, axis=-1, keepdims=True))
            o = _dot(jnp.exp2(s - m).astype(BF16), v_ref[:, vsl])
            if has_cache:
                o = o + _dot(jnp.exp2(sc - m).astype(BF16), vc_ref[:, vsl])
            outs.append(o[:, :LANES] * (1.0 / o[:, LANES:]))
        o_ref[:, pair * LANES:(pair + 1) * LANES] = jnp.where(lane < MLA_V_DIM, outs[0], outs[1]).astype(BF16)


def _attention(q, k, v, cache=None):
    b, s, _ = q.shape
    tq = Q_TILE
    kw, vw, ow = MLA_HEADS * HEAD_BLOCK, V_WIDTH, MLA_HEADS * MLA_V_DIM
    in_specs = [pl.BlockSpec((None, tq, kw), lambda i, j: (i, j, 0)),
                pl.BlockSpec((None, s, kw), lambda i, j: (i, 0, 0)),
                pl.BlockSpec((None, s, vw), lambda i, j: (i, 0, 0))]
    args = [q, k, v]
    if cache is not None:
        past = cache[0].shape[1]
        in_specs += [pl.BlockSpec((None, past, kw), lambda i, j: (i, 0, 0)),
                     pl.BlockSpec((None, past, vw), lambda i, j: (i, 0, 0))]
        args += list(cache)
    return pl.pallas_call(
        functools.partial(_attn_kernel, has_cache=cache is not None),
        grid=(b, s // tq),
        in_specs=in_specs,
        out_specs=pl.BlockSpec((None, tq, ow), lambda i, j: (i, j, 0)),
        out_shape=jax.ShapeDtypeStruct((b, s, ow), BF16),
        compiler_params=_params(("parallel", "arbitrary")),
        name="attn_lat" if cache is not None else "attn_ctx",
    )(*args)


def _lane_pick(x, k):
    lane = lax.broadcasted_iota(jnp.int32, x.shape, 1)
    return jnp.sum(jnp.where(lane == k, x, 0.0), axis=1, keepdims=True)


def _prefix_sum_rows(x):
    n = x.shape[0]
    r = lax.broadcasted_iota(jnp.int32, (n, n), 0)
    c = lax.broadcasted_iota(jnp.int32, (n, n), 1)
    tri = jnp.where(c <= r, 1.0, 0.0).astype(BF16)
    hi = x.astype(BF16)
    r1 = x - hi.astype(F32)
    mid = r1.astype(BF16)
    lo = (r1 - mid.astype(F32)).astype(BF16)
    return _dot(tri, hi) + _dot(tri, mid) + _dot(tri, lo)


def _gate_prep(g):
    n = g.shape[0]
    lane = lax.broadcasted_iota(jnp.int32, (n, LANES), 1)
    fpre = pltpu.roll(g, LANES - N_DIRHEAD, 1)
    lf = jnp.minimum(fpre, 0.0) - jnp.log(1.0 + jnp.exp(-jnp.abs(fpre)))
    binc = _prefix_sum_rows(lf)
    tot = binc[n - 1:n, :]
    b = jnp.where(lane < MLSTM_HEADS, binc, tot - binc + lf)
    u = g - b
    return b, u, u.T, tot


def _mlstm_dir(s0, q, k32, v, u_row, u_col, b_col, tot_k, *, fwd, state):
    n = s0.shape[0]
    r = lax.broadcasted_iota(jnp.int32, (n, n), 0)
    c = lax.broadcasted_iota(jnp.int32, (n, n), 1)
    mask = (c <= r) if fwd else (c >= r)
    a = jnp.where(mask, u_row, -jnp.inf)
    m_prev = 0.0 if state is None else state[2]
    g = jnp.maximum(jnp.max(a, axis=1, keepdims=True), m_prev)
    g_last = jnp.maximum(jnp.max(u_row, axis=1, keepdims=True), m_prev)
    sd = s0 * jnp.exp(a - g)
    den = jnp.sum(sd, axis=1, keepdims=True)
    num = _dot(sd.astype(BF16), v)
    if state is not None:
        c_prev, n_prev, _ = state
        w_inter = jnp.exp(m_prev - g)
        num = num + w_inter * _dot_nt(q, c_prev.astype(BF16))
        den = den + w_inter * jnp.sum(q.astype(F32) * n_prev, axis=1, keepdims=True)
    floor = jnp.exp(-(b_col + g))
    h = num * (1.0 / jnp.maximum(jnp.abs(den), floor))
    wk = jnp.exp(u_col - g_last) * k32
    c_new = _dot_tn(v, wk.astype(BF16))
    n_new = jnp.sum(wk, axis=0, keepdims=True)
    if state is not None:
        decay = jnp.exp(m_prev - g_last)
        c_new = decay * c_prev + c_new
        n_new = decay * n_prev + n_new
    return h, c_new, n_new, tot_k + g_last


def _mlstm_ctx_kernel(q_ref, k_ref, v_ref, g_ref, h_ref, c_ref, n_ref, m_ref):
    b, u, ut, tot = _gate_prep(g_ref[...])
    lane = lax.broadcasted_iota(jnp.int32, (1, LANES), 1)
    m_row = jnp.zeros((1, LANES), F32)
    for hd in range(MLSTM_HEADS):
        q = q_ref[:, hd * MLSTM_DK:(hd + 1) * MLSTM_DK]
        k = k_ref[:, hd * MLSTM_DK:(hd + 1) * MLSTM_DK]
        v = v_ref[:, hd * MLSTM_DV:(hd + 1) * MLSTM_DV]
        s0 = _dot_nt(q, k)
        k32 = k.astype(F32)
        h_sum = None
        for fwd in (True, False):
            kk = hd if fwd else MLSTM_HEADS + hd
            h, c_new, n_new, m_new = _mlstm_dir(
                s0, q, k32, v, ut[kk:kk + 1, :], _lane_pick(u, kk), _lane_pick(b, kk), _lane_pick(tot, kk),
                fwd=fwd, state=None)
            h_sum = h if h_sum is None else h_sum + h
            c_ref[kk] = c_new
            n_ref[kk:kk + 1, :] = n_new
            m_row = jnp.where(lane == kk, m_new, m_row)
        h_ref[:, hd * MLSTM_DV:(hd + 1) * MLSTM_DV] = h_sum
    m_ref[...] = m_row


def _mlstm_ctx(mq, mk, mv, gates):
    b, s, _ = mq.shape
    seq = lambda c: pl.BlockSpec((None, s, c), lambda i: (i, 0, 0))
    return pl.pallas_call(
        _mlstm_ctx_kernel,
        grid=(b,),
        in_specs=[seq(MLSTM_HEADS * MLSTM_DK), seq(MLSTM_HEADS * MLSTM_DK), seq(MLSTM_HEADS * MLSTM_DV), seq(LANES)],
        out_specs=[seq(MLSTM_HEADS * MLSTM_DV),
                   pl.BlockSpec((None, N_DIRHEAD, MLSTM_DV, MLSTM_DK), lambda i: (i, 0, 0, 0)),
                   pl.BlockSpec((None, N_DIRHEAD, MLSTM_DK), lambda i: (i, 0, 0)),
                   pl.BlockSpec((None, 1, LANES), lambda i: (i, 0, 0))],
        out_shape=[jax.ShapeDtypeStruct((b, s, MLSTM_HEADS * MLSTM_DV), F32),
                   jax.ShapeDtypeStruct((b, N_DIRHEAD, MLSTM_DV, MLSTM_DK), F32),
                   jax.ShapeDtypeStruct((b, N_DIRHEAD, MLSTM_DK), F32),
                   jax.ShapeDtypeStruct((b, 1, LANES), F32)],
        compiler_params=_params(("parallel",)),
        name="mlstm_ctx",
    )(mq, mk, mv, gates)


def _mlstm_lat_kernel(qf_ref, kf_ref, vf_ref, gf_ref, qb_ref, kb_ref, vb_ref, gb_ref, c0_ref, n0_ref, m0_ref,
                      h_ref, c_ref, n_ref, m_ref, *, n_chunks):
    step = pl.program_id(1)

    @pl.when(step == 0)
    def _():
        c_ref[...] = c0_ref[...]
        n_ref[...] = n0_ref[...]
        m_ref[...] = m0_ref[...]
        h_ref[...] = jnp.zeros_like(h_ref)

    lane = lax.broadcasted_iota(jnp.int32, (1, LANES), 1)
    m_old = m_ref[...]
    m_row = m_old
    for fwd in (True, False):
        q_ref, k_ref, v_ref, g_ref = (qf_ref, kf_ref, vf_ref, gf_ref) if fwd else (qb_ref, kb_ref, vb_ref, gb_ref)
        chunk = step if fwd else n_chunks - 1 - step
        row0 = pl.multiple_of(chunk * MLSTM_CHUNK, MLSTM_CHUNK)
        b, u, ut, tot = _gate_prep(g_ref[...])
        for hd in range(MLSTM_HEADS):
            kk = hd if fwd else MLSTM_HEADS + hd
            q = q_ref[:, hd * MLSTM_DK:(hd + 1) * MLSTM_DK]
            k = k_ref[:, hd * MLSTM_DK:(hd + 1) * MLSTM_DK]
            v = v_ref[:, hd * MLSTM_DV:(hd + 1) * MLSTM_DV]
            s0 = _dot_nt(q, k)
            state = (c_ref[kk], n_ref[kk:kk + 1, :], _lane_pick(m_old, kk))
            h, c_new, n_new, m_new = _mlstm_dir(
                s0, q, k.astype(F32), v, ut[kk:kk + 1, :], _lane_pick(u, kk), _lane_pick(b, kk),
                _lane_pick(tot, kk), fwd=fwd, state=state)
            c_ref[kk] = c_new
            n_ref[kk:kk + 1, :] = n_new
            m_row = jnp.where(lane == kk, m_new, m_row)
            h_ref[pl.ds(row0, MLSTM_CHUNK), hd * MLSTM_DV:(hd + 1) * MLSTM_DV] += h
    m_ref[...] = m_row


def _mlstm_lat(mq, mk, mv, gates, c0, n0, m0):
    b, s, _ = mq.shape
    lc = MLSTM_CHUNK
    nc = s // lc
    fw = lambda c: pl.BlockSpec((None, lc, c), lambda i, j: (i, j, 0))
    bw = lambda c: pl.BlockSpec((None, lc, c), lambda i, j: (i, nc - 1 - j, 0))
    dk, dv = MLSTM_HEADS * MLSTM_DK, MLSTM_HEADS * MLSTM_DV
    return pl.pallas_call(
        functools.partial(_mlstm_lat_kernel, n_chunks=nc),
        grid=(b, nc),
        in_specs=[fw(dk), fw(dk), fw(dv), fw(LANES), bw(dk), bw(dk), bw(dv), bw(LANES),
                  pl.BlockSpec((None, N_DIRHEAD, MLSTM_DV, MLSTM_DK), lambda i, j: (i, 0, 0, 0)),
                  pl.BlockSpec((None, N_DIRHEAD, MLSTM_DK), lambda i, j: (i, 0, 0)),
                  pl.BlockSpec((None, 1, LANES), lambda i, j: (i, 0, 0))],
        out_specs=pl.BlockSpec((None, s, dv), lambda i, j: (i, 0, 0)),
        out_shape=jax.ShapeDtypeStruct((b, s, dv), F32),
        scratch_shapes=[pltpu.VMEM((N_DIRHEAD, MLSTM_DV, MLSTM_DK), F32),
                        pltpu.VMEM((N_DIRHEAD, MLSTM_DK), F32),
                        pltpu.VMEM((1, LANES), F32)],
        compiler_params=_params(("parallel", "arbitrary")),
        name="mlstm_lat",
    )(mq, mk, mv, gates, mq, mk, mv, gates, c0, n0, m0)


def _tail_kernel(x_ref, mod_ref, att_ref, hs_ref, smo_ref, sbr_ref, gml_ref, womla_ref, womlstm_ref, wout_ref,
                 gffn_ref, wfin_ref, wfout_ref, gfin_ref, y_ref):
    mod = mod_ref[...]
    a = _dot(att_ref[...], womla_ref[...])
    gml = gml_ref[...]
    parts = []
    for hd in range(MLSTM_HEADS):
        sl = slice(hd * MLSTM_DV, (hd + 1) * MLSTM_DV)
        parts.append((_rms(hs_ref[:, sl], gml[:, sl]) * smo_ref[:, sl].astype(F32)).astype(BF16))
    bm = _dot(jnp.concatenate(parts, axis=1), womlstm_ref[...])
    merged = sbr_ref[:, :D_MODEL].astype(F32) * a + sbr_ref[:, D_MODEL:].astype(F32) * bm
    x1 = x_ref[...] + mod[2:3] * _dot(merged.astype(BF16), wout_ref[...])
    h2 = (_rms(x1, gffn_ref[...]) * (1.0 + mod[4:5]) + mod[3:4]).astype(BF16)
    ga = _dot(h2, wfin_ref[:, :FFN_HIDDEN])
    gu = _dot(h2, wfin_ref[:, FFN_HIDDEN:])
    act = (ga * _sigmoid(ga) * gu).astype(BF16)
    x2 = x1 + mod[5:6] * _dot(act, wfout_ref[...])
    y_ref[...] = _rms(x2, gfin_ref[...])


def _tail(x, mod, att, hs, smo, sbr, gml, w_o_mla, w_o_mlstm, w_out, gffn, w_ffn_in, w_ffn_out, gfin, *, seq_len, name):
    n_tok = x.shape[0]
    tm = TOKEN_TILE
    tiles_per_seq = seq_len // tm
    tok = lambda c: pl.BlockSpec((tm, c), lambda i: (i, 0))
    consts = (gml, w_o_mla, w_o_mlstm, w_out, gffn, w_ffn_in, w_ffn_out, gfin)
    return pl.pallas_call(
        _tail_kernel,
        grid=(n_tok // tm,),
        in_specs=[tok(D_MODEL), _mod_spec(mod, tiles_per_seq),
                  tok(att.shape[1]), tok(hs.shape[1]), tok(smo.shape[1]), tok(sbr.shape[1])]
                 + [_const_spec(a.shape) for a in consts],
        out_specs=tok(D_MODEL),
        out_shape=jax.ShapeDtypeStruct((n_tok, D_MODEL), F32),
        compiler_params=_params(("parallel",)),
        name=name,
    )(x, mod, att, hs, smo, sbr, *consts)


_IN_SIZES = (Q_RANK, KV_RANK, ROPE_DIM, MLSTM_HEADS * MLSTM_DK, MLSTM_HEADS * MLSTM_DK, MLSTM_HEADS * MLSTM_DV,
             4 * MLSTM_HEADS, MLSTM_HEADS * MLSTM_DV, 2 * D_MODEL)
_IN_OFF = tuple(sum(_IN_SIZES[:i]) for i in range(len(_IN_SIZES) + 1))
PACK_ROWS = LANES
_PAD_AFTER = tuple((_SEG_OFF[i + 1] // PACK_ROWS, _SEG_WIDTHS[i] - _IN_SIZES[i], _IN_SIZES[i] % PACK_ROWS)
                   for i in range(len(_IN_SIZES)) if _SEG_WIDTHS[i] != _IN_SIZES[i])


def _pack_src_row(p):
    src = p * PACK_ROWS
    for first_piece_after, pad, _ in _PAD_AFTER:
        src = src - jnp.where(p >= first_piece_after, pad, 0)
    return pl.multiple_of(src, math.gcd(PACK_ROWS, *(pad for _, pad, _ in _PAD_AFTER)))


def _pack_in_kernel(w_ref, o_ref):
    p = pl.program_id(0)
    valid = PACK_ROWS
    for first_piece_after, _, width in _PAD_AFTER:
        valid = jnp.where(p == first_piece_after - 1, width, valid)
    row = lax.broadcasted_iota(jnp.int32, w_ref.shape, 0)
    o_ref[...] = jnp.where(row < valid, w_ref[...], 0.0).astype(BF16)


def _pack_in(w_in_t):
    n, k = w_in_t.shape
    return pl.pallas_call(
        _pack_in_kernel,
        grid=(N_IN_PACKED // PACK_ROWS,),
        in_specs=[pl.BlockSpec((pl.Element(PACK_ROWS), pl.Element(k)), lambda p: (_pack_src_row(p), 0))],
        out_specs=pl.BlockSpec((PACK_ROWS, k), lambda p: (p, 0)),
        out_shape=jax.ShapeDtypeStruct((N_IN_PACKED, k), BF16),
        compiler_params=_params(("parallel",)),
        name="pack_in",
    )(w_in_t)


def _pack_small(w_uq, w_ukv, b_gates):
    qd = NOPE_DIM + ROPE_DIM
    w_uq_p = jnp.pad(w_uq.reshape(Q_RANK, MLA_HEADS, qd), ((0, 0), (0, 0), (0, HEAD_BLOCK - qd)))
    w_uq_p = w_uq_p.reshape(Q_RANK, MLA_HEADS * HEAD_BLOCK).astype(BF16)
    kv = w_ukv.reshape(KV_RANK, MLA_HEADS, NOPE_DIM + MLA_V_DIM)
    wk = jnp.pad(kv[:, :, :NOPE_DIM], ((0, 0), (0, 0), (0, HEAD_BLOCK - NOPE_DIM))).reshape(KV_RANK, -1)
    wv = kv[:, :, NOPE_DIM:].reshape(KV_RANK, -1)
    w_ukv_p = jnp.concatenate([wk, wv], axis=1).astype(BF16)
    bg = jnp.pad(b_gates, ((0, 0), (0, LANES - b_gates.shape[1])))
    return w_uq_p, w_ukv_p, bg


def _rope_tables(n_tokens):
    pos = np.arange(n_tokens)
    row = (pos // GRID_W).astype(np.float64)[:, None]
    col = (pos % GRID_W).astype(np.float64)[:, None]
    half = ROPE_DIM // 2
    inv = (np.float32(ROPE_BASE) ** (-np.arange(0, half, 2, dtype=np.float32) / np.float32(half))).astype(np.float64)
    r = np.arange(LANES) - ROPE_LANE0
    in_rope = (r >= 0) & (r < ROPE_DIM)
    rr = np.clip(r, 0, ROPE_DIM - 1)
    freq = inv[rr % (half // 2)][None, :]
    ang = np.where((rr // half == 0)[None, :], row * freq, col * freq).astype(np.float32).astype(np.float64)
    first = (rr % half) < (half // 2)
    cos = np.where(in_rope[None, :], np.cos(ang), 1.0)
    sin = np.sin(ang)
    sin_lo = np.where((in_rope & first)[None, :], -sin, 0.0)
    sin_hi = np.where((in_rope & ~first)[None, :], sin, 0.0)
    return tuple(jnp.asarray(t, dtype=F32) for t in (cos, sin_lo, sin_hi))


def kernel(x_prompt, x_sample, cache_ckv, cache_krope, state_C, state_n, state_m, c, c_ctx, w_mod, b_mod, g_norm_mix,
           w_in, b_gates, g_q_norm, w_uq, g_kv_norm, w_ukv, g_mlstm_norm, w_o_mla, w_o_mlstm, w_out, g_norm_ffn,
           w_ffn_in, w_ffn_out, g_final):
    bp, sp, d = x_prompt.shape
    bs, ss, _ = x_sample.shape
    layer = 0
    assert w_mod.shape[0] == 1 and sp == MLSTM_CHUNK and ss % MLSTM_CHUNK == 0

    cond = jnp.concatenate([c_ctx[None, :], c, jnp.zeros((8 - 1 - bs, d), F32)], axis=0)
    mod = _adaln(cond, w_mod[layer], b_mod[layer][None, :]).reshape(8, 6, d)
    mod_ctx, mod_lat = mod[0:1], mod[1:1 + bs]

    w_in_p = _pack_in(w_in[layer].T)
    w_uq_p, w_ukv_p, bg = _pack_small(w_uq[layer], w_ukv[layer], b_gates[layer][None, :])
    row = lambda g: g[layer][None, :]
    shared_in = (row(g_norm_mix), w_in_p, row(g_q_norm), w_uq_p, row(g_kv_norm), w_ukv_p, bg)
    tail_w = (row(g_mlstm_norm), w_o_mla[layer].astype(BF16), w_o_mlstm[layer].astype(BF16), w_out[layer].astype(BF16),
              row(g_norm_ffn), w_ffn_in[layer].astype(BF16), w_ffn_out[layer].astype(BF16), g_final[None, :])

    xc = x_prompt.reshape(bp * sp, d)
    q, kc, v, mq, mk, mv, gates, smo, sbr, ckv, kpe = _inproj(xc, mod_ctx, None, *shared_in, seq_len=sp)
    seq3 = lambda a, b_, s_: a.reshape(b_, s_, a.shape[-1])
    att = _attention(seq3(q, bp, sp), seq3(kc, bp, sp), seq3(v, bp, sp))
    hs, new_c, new_n, new_m = _mlstm_ctx(seq3(mq, bp, sp), seq3(mk, bp, sp), seq3(mv, bp, sp), seq3(gates, bp, sp))
    y_prompt = _tail(xc, mod_ctx, att.reshape(bp * sp, -1), hs.reshape(bp * sp, -1), smo, sbr, *tail_w,
                     seq_len=sp, name="tail_ctx").reshape(bp, sp, d)
    new_ckv = ckv.reshape(bp, 1, sp, KV_RANK)
    new_krope = kpe.reshape(bp, 1, sp, ROPE_DIM)
    new_C = new_c.reshape(bp, 1, 2, MLSTM_HEADS, MLSTM_DV, MLSTM_DK)
    new_N = new_n.reshape(bp, 1, 2, MLSTM_HEADS, MLSTM_DK)
    new_M = new_m[:, 0, :N_DIRHEAD].reshape(bp, 1, 2, MLSTM_HEADS)

    xl = x_sample.reshape(bs * ss, d)
    q, kc, v, mq, mk, mv, gates, smo, sbr = _inproj(xl, mod_lat, _rope_tables(ss), *shared_in, seq_len=ss)
    krope_blk = jnp.pad(cache_krope[:, layer], ((0, 0), (0, 0), (ROPE_LANE0, LANES - ROPE_LANE0 - ROPE_DIM)))
    cache = _cachekv(cache_ckv[:, layer], krope_blk, w_ukv_p)
    att = _attention(seq3(q, bs, ss), seq3(kc, bs, ss), seq3(v, bs, ss), cache)
    c0 = state_C[:, layer].reshape(bs, N_DIRHEAD, MLSTM_DV, MLSTM_DK)
    n0 = state_n[:, layer].reshape(bs, N_DIRHEAD, MLSTM_DK)
    m0 = jnp.pad(state_m[:, layer].reshape(bs, 1, N_DIRHEAD), ((0, 0), (0, 0), (0, LANES - N_DIRHEAD)))
    hs = _mlstm_lat(seq3(mq, bs, ss), seq3(mk, bs, ss), seq3(mv, bs, ss), seq3(gates, bs, ss), c0, n0, m0)
    y_sample = _tail(xl, mod_lat, att.reshape(bs * ss, -1), hs.reshape(bs * ss, -1), smo, sbr, *tail_w,
                     seq_len=ss, name="tail_lat").reshape(bs, ss, d)
    return (y_prompt, y_sample, new_ckv, new_krope, new_C, new_N, new_M)
```

```python
import functools
import math

import jax
import jax.numpy as jnp
import numpy as np
from jax import lax
from jax.experimental import pallas as pl
from jax.experimental.pallas import tpu as pltpu

F32 = jnp.float32
BF16 = jnp.bfloat16

D_MODEL = 1024
GRID_W = 64
MLA_HEADS = 8
Q_RANK = 384
KV_RANK = 256
NOPE_DIM = 64
ROPE_DIM = 32
MLA_V_DIM = 64
ROPE_BASE = 10000.0
MLA_SCALE = (NOPE_DIM + ROPE_DIM) ** -0.5
MLSTM_HEADS = 4
MLSTM_DK = 128
MLSTM_DV = 256
FFN_HIDDEN = ((8 * D_MODEL // 3 + 255) // 256) * 256
EPS = 1e-6

LANES = 128
HEAD_BLOCK = LANES
ROPE_LANE0 = NOPE_DIM
N_DIRHEAD = 2 * MLSTM_HEADS
MLSTM_CHUNK = 256
LOG2E = math.log2(math.e)
Q_PRESCALE = MLA_SCALE * LOG2E
V_SLAB = 2 * LANES
V_WIDTH = (MLA_HEADS // 2) * V_SLAB
TOKEN_TILE = 512
Q_TILE = 256
VMEM_LIMIT = 56 * 1024 * 1024

_SEG_WIDTHS = (Q_RANK, KV_RANK, LANES, MLSTM_HEADS * MLSTM_DK, MLSTM_HEADS * MLSTM_DK,
               MLSTM_HEADS * MLSTM_DV, LANES, MLSTM_HEADS * MLSTM_DV, 2 * D_MODEL)
_SEG_OFF = tuple(sum(_SEG_WIDTHS[:i]) for i in range(len(_SEG_WIDTHS) + 1))
SEG_Q, SEG_KV, SEG_KPE, SEG_MQ, SEG_MK, SEG_MV, SEG_GATE, SEG_MO, SEG_BR = (
    (_SEG_OFF[i], _SEG_OFF[i + 1]) for i in range(9))
N_IN_PACKED = _SEG_OFF[-1]


def _dot(a, b):
    return jnp.dot(a, b, preferred_element_type=F32)


def _dot_nt(a, b):
    return lax.dot_general(a, b, (((1,), (1,)), ((), ())), preferred_element_type=F32)


def _dot_tn(a, b):
    return lax.dot_general(a, b, (((0,), (0,)), ((), ())), preferred_element_type=F32)


def _rms(x, g):
    ms = jnp.mean(x * x, axis=-1, keepdims=True)
    return x * lax.rsqrt(ms + EPS) * g


def _sigmoid(x):
    return 1.0 / (1.0 + jnp.exp(-x))


def _const_spec(shape):
    nd = len(shape)
    return pl.BlockSpec(shape, lambda *_: (0,) * nd, pipeline_mode=pl.Buffered(1))


def _mod_spec(mod, tiles_per_seq):
    if mod.shape[0] == 1:
        return pl.BlockSpec((None, 6, D_MODEL), lambda i: (0, 0, 0))
    return pl.BlockSpec((None, 6, D_MODEL), lambda i: (i // tiles_per_seq, 0, 0))


def _params(sem):
    return pltpu.CompilerParams(dimension_semantics=sem, vmem_limit_bytes=VMEM_LIMIT)


def _adaln_kernel(c_ref, w_ref, b_ref, o_ref):
    c = c_ref[...]
    s = c * _sigmoid(c)
    o_ref[...] = _dot(s.astype(BF16), w_ref[...].astype(BF16)) + b_ref[...]


def _adaln(cond8, w_mod, b_mod):
    n = w_mod.shape[1]
    tn = 1536
    return pl.pallas_call(
        _adaln_kernel,
        grid=(n // tn,),
        in_specs=[pl.BlockSpec((8, D_MODEL), lambda j: (0, 0)),
                  pl.BlockSpec((D_MODEL, tn), lambda j: (0, j)),
                  pl.BlockSpec((1, tn), lambda j: (0, j))],
        out_specs=pl.BlockSpec((8, tn), lambda j: (0, j)),
        out_shape=jax.ShapeDtypeStruct((8, n), F32),
        compiler_params=_params(("arbitrary",)),
        name="adaln",
    )(cond8, w_mod, b_mod)


def _rope_block(x, cos, sin_lo, sin_hi):
    return x * cos + pltpu.roll(x, LANES - 8, 1) * sin_lo + pltpu.roll(x, 8, 1) * sin_hi


def _store_kv(kvn, kp, kc_ref, v_ref):
    for hd in range(MLA_HEADS):
        sl = slice(hd * HEAD_BLOCK, (hd + 1) * HEAD_BLOCK)
        kc_ref[:, sl] = (kvn[:, sl] + kp).astype(BF16)
    v0 = MLA_HEADS * HEAD_BLOCK
    ones = jnp.ones((kvn.shape[0], LANES), BF16)
    for pair in range(MLA_HEADS // 2):
        v_ref[:, pair * V_SLAB:pair * V_SLAB + LANES] = kvn[:, v0 + pair * LANES:v0 + (pair + 1) * LANES].astype(BF16)
        v_ref[:, pair * V_SLAB + LANES:(pair + 1) * V_SLAB] = ones


def _inproj_kernel(*refs, rope):
    if rope:
        (x_ref, mod_ref, cos_ref, slo_ref, shi_ref, gmix_ref, win_ref, gq_ref, wuq_ref, gkv_ref, wukv_ref,
         bg_ref, q_ref, kc_ref, v_ref, mq_ref, mk_ref, mv_ref, gate_ref, smo_ref, sbr_ref) = refs
        cos, slo, shi = cos_ref[...], slo_ref[...], shi_ref[...]
    else:
        (x_ref, mod_ref, gmix_ref, win_ref, gq_ref, wuq_ref, gkv_ref, wukv_ref,
         bg_ref, q_ref, kc_ref, v_ref, mq_ref, mk_ref, mv_ref, gate_ref, smo_ref, sbr_ref,
         ckv_ref, kpe_ref) = refs
    mod = mod_ref[...]
    h = _rms(x_ref[...], gmix_ref[...]) * (1.0 + mod[1:2]) + mod[0:1]
    hb = h.astype(BF16)

    def proj(seg):
        return _dot_nt(hb, win_ref[seg[0]:seg[1], :])

    qn = _rms(proj(SEG_Q), gq_ref[...]).astype(BF16)
    q = _dot(qn, wuq_ref[...])
    for hd in range(MLA_HEADS):
        sl = slice(hd * HEAD_BLOCK, (hd + 1) * HEAD_BLOCK)
        qh = q[:, sl]
        if rope:
            qh = _rope_block(qh, cos, slo, shi)
        q_ref[:, sl] = (qh * Q_PRESCALE).astype(BF16)

    ckv = _rms(proj(SEG_KV), gkv_ref[...])
    zkpe = proj(SEG_KPE)
    if not rope:
        ckv_ref[...] = ckv
        kpe_ref[...] = zkpe[:, :ROPE_DIM]
    kp = pltpu.roll(zkpe, ROPE_LANE0, 1)
    if rope:
        kp = _rope_block(kp, cos, slo, shi)
    kvn = _dot(ckv.astype(BF16), wukv_ref[...])
    _store_kv(kvn, kp, kc_ref, v_ref)

    mq_ref[...] = proj(SEG_MQ).astype(BF16)
    mk_ref[...] = (proj(SEG_MK) * (MLSTM_DK ** -0.5)).astype(BF16)
    mv_ref[...] = proj(SEG_MV).astype(BF16)
    gate_ref[...] = proj(SEG_GATE) + bg_ref[...]
    smo_ref[...] = _sigmoid(proj(SEG_MO)).astype(BF16)
    sbr_ref[...] = _sigmoid(proj(SEG_BR)).astype(BF16)


def _inproj(x, mod, rope_tabs, gmix, w_in_p, gq, w_uq_p, gkv, w_ukv_p, bg, *, seq_len):
    n_tok = x.shape[0]
    tm = TOKEN_TILE
    tiles_per_seq = seq_len // tm
    rope = rope_tabs is not None
    tok = lambda c: pl.BlockSpec((tm, c), lambda i: (i, 0))
    in_specs = [tok(D_MODEL), _mod_spec(mod, tiles_per_seq)]
    args = [x, mod]
    if rope:
        in_specs += [pl.BlockSpec((tm, LANES), lambda i: (i % tiles_per_seq, 0))] * 3
        args += list(rope_tabs)
    in_specs += [_const_spec(a.shape) for a in (gmix, w_in_p, gq, w_uq_p, gkv, w_ukv_p, bg)]
    args += [gmix, w_in_p, gq, w_uq_p, gkv, w_ukv_p, bg]
    widths = [(MLA_HEADS * HEAD_BLOCK, BF16), (MLA_HEADS * HEAD_BLOCK, BF16), (V_WIDTH, BF16),
              (MLSTM_HEADS * MLSTM_DK, BF16), (MLSTM_HEADS * MLSTM_DK, BF16), (MLSTM_HEADS * MLSTM_DV, BF16),
              (LANES, F32), (MLSTM_HEADS * MLSTM_DV, BF16), (2 * D_MODEL, BF16)]
    if not rope:
        widths += [(KV_RANK, F32), (ROPE_DIM, F32)]
    return pl.pallas_call(
        functools.partial(_inproj_kernel, rope=rope),
        grid=(n_tok // tm,),
        in_specs=in_specs,
        out_specs=[tok(c) for c, _ in widths],
        out_shape=[jax.ShapeDtypeStruct((n_tok, c), dt) for c, dt in widths],
        compiler_params=_params(("parallel",)),
        name="inproj_lat" if rope else "inproj_ctx",
    )(*args)


def _cachekv_kernel(ckv_ref, kp_ref, wukv_ref, kc_ref, v_ref):
    kvn = _dot(ckv_ref[...].astype(BF16), wukv_ref[...])
    _store_kv(kvn, kp_ref[...], kc_ref, v_ref)


def _cachekv(cache_ckv, krope_blk, w_ukv_p):
    b, past, _ = cache_ckv.shape
    return pl.pallas_call(
        _cachekv_kernel,
        grid=(b,),
        in_specs=[pl.BlockSpec((None, past, KV_RANK), lambda i: (i, 0, 0)),
                  pl.BlockSpec((None, past, LANES), lambda i: (i, 0, 0)),
                  _const_spec(w_ukv_p.shape)],
        out_specs=[pl.BlockSpec((None, past, MLA_HEADS * HEAD_BLOCK), lambda i: (i, 0, 0)),
                   pl.BlockSpec((None, past, V_WIDTH), lambda i: (i, 0, 0))],
        out_shape=[jax.ShapeDtypeStruct((b, past, MLA_HEADS * HEAD_BLOCK), BF16),
                   jax.ShapeDtypeStruct((b, past, V_WIDTH), BF16)],
        compiler_params=_params(("parallel",)),
        name="cachekv",
    )(cache_ckv, krope_blk, w_ukv_p)


def _attn_kernel(*refs, has_cache):
    if has_cache:
        q_ref, k_ref, v_ref, kc_ref, vc_ref, o_ref = refs
    else:
        q_ref, k_ref, v_ref, o_ref = refs
    tq = q_ref.shape[0]
    lane = lax.broadcasted_iota(jnp.int32, (tq, LANES), 1)
    for pair in range(MLA_HEADS // 2):
        vsl = slice(pair * V_SLAB, (pair + 1) * V_SLAB)
        outs = []
        for j in range(2):
            hd = 2 * pair + j
            sl = slice(hd * HEAD_BLOCK, (hd + 1) * HEAD_BLOCK)
            qh = q_ref[:, sl]
            s = _dot_nt(qh, k_ref[:, sl])
            m = jnp.max(s, axis=-1, keepdims=True)
            if has_cache:
                sc = _dot_nt(qh, kc_ref[:, sl])
                m = jnp.maximum(m, jnp.max(sc, axis=-1, keepdims=True))
            o = _dot(jnp.exp2(s - m).astype(BF16), v_ref[:, vsl])
            if has_cache:
                o = o + _dot(jnp.exp2(sc - m).astype(BF16), vc_ref[:, vsl])
            outs.append(o[:, :LANES] * (1.0 / o[:, LANES:]))
        o_ref[:, pair * LANES:(pair + 1) * LANES] = jnp.where(lane < MLA_V_DIM, outs[0], outs[1]).astype(BF16)


def _attention(q, k, v, cache=None):
    b, s, _ = q.shape
    tq = Q_TILE
    kw, vw, ow = MLA_HEADS * HEAD_BLOCK, V_WIDTH, MLA_HEADS * MLA_V_DIM
    in_specs = [pl.BlockSpec((None, tq, kw), lambda i, j: (i, j, 0)),
                pl.BlockSpec((None, s, kw), lambda i, j: (i, 0, 0)),
                pl.BlockSpec((None, s, vw), lambda i, j: (i, 0, 0))]
    args = [q, k, v]
    if cache is not None:
        past = cache[0].shape[1]
        in_specs += [pl.BlockSpec((None, past, kw), lambda i, j: (i, 0, 0)),
                     pl.BlockSpec((None, past, vw), lambda i, j: (i, 0, 0))]
        args += list(cache)
    return pl.pallas_call(
        functools.partial(_attn_kernel, has_cache=cache is not None),
        grid=(b, s // tq),
        in_specs=in_specs,
        out_specs=pl.BlockSpec((None, tq, ow), lambda i, j: (i, j, 0)),
        out_shape=jax.ShapeDtypeStruct((b, s, ow), BF16),
        compiler_params=_params(("parallel", "arbitrary")),
        name="attn_lat" if cache is not None else "attn_ctx",
    )(*args)


def _lane_bcast(x, k, width=None):
    y = jnp.broadcast_to(x[:, k:k + 1], x.shape)
    reps = (width or LANES) // LANES
    return y if reps == 1 else jnp.concatenate([y] * reps, axis=1)


def _prefix_sum_rows(x):
    n = x.shape[0]
    r = lax.broadcasted_iota(jnp.int32, (n, n), 0)
    c = lax.broadcasted_iota(jnp.int32, (n, n), 1)
    tri = jnp.where(c <= r, 1.0, 0.0).astype(BF16)
    hi = x.astype(BF16)
    r1 = x - hi.astype(F32)
    mid = r1.astype(BF16)
    lo = (r1 - mid.astype(F32)).astype(BF16)
    return _dot(tri, hi) + _dot(tri, mid) + _dot(tri, lo)


def _gate_prep(g):
    n = g.shape[0]
    lane = lax.broadcasted_iota(jnp.int32, (n, LANES), 1)
    fpre = pltpu.roll(g, LANES - N_DIRHEAD, 1)
    lf = jnp.minimum(fpre, 0.0) - jnp.log(1.0 + jnp.exp(-jnp.abs(fpre)))
    binc = _prefix_sum_rows(lf)
    tot = binc[n - 1:n, :]
    b = jnp.where(lane < MLSTM_HEADS, binc, tot - binc + lf)
    return b, g - b, tot


def _chunk_setup(b, u, tot, m_prev):
    g_last = jnp.maximum(jnp.max(u, axis=0, keepdims=True), m_prev)
    u2 = u * LOG2E
    w = jnp.exp2(u2 - g_last * LOG2E)
    return u2, (b * LOG2E).T, w, tot + g_last, jnp.exp(m_prev - g_last)


def _mlstm_dir_t(s0t, vt, u_b, bt_row, m_row, *, fwd, inter=None):
    n = s0t.shape[0]
    r = lax.broadcasted_iota(jnp.int32, (n, n), 0)
    c = lax.broadcasted_iota(jnp.int32, (n, n), 1)
    a = jnp.where((r <= c) if fwd else (r >= c), u_b, -jnp.inf)
    g = jnp.maximum(jnp.max(a, axis=0, keepdims=True), m_row)
    sdt = s0t * jnp.exp2(a - g)
    den = jnp.sum(sdt, axis=0, keepdims=True)
    ht = _dot(vt, sdt.astype(BF16))
    if inter is not None:
        w_inter = jnp.exp2(m_row - g)
        ht = ht + w_inter * inter[0]
        den = den + w_inter * inter[1]
    floor = jnp.exp2(-(bt_row + g))
    return ht * (1.0 / jnp.maximum(jnp.abs(den), floor))


def _mlstm_ctx_kernel(q_ref, k_ref, v_ref, g_ref, h_ref, c_ref, n_ref, m_ref):
    n = q_ref.shape[1]
    zero_lanes = jnp.zeros((1, LANES), F32)
    zero_row = jnp.zeros((1, n), F32)
    for sq in range(q_ref.shape[0]):
        u2, bt2, w, m_new, _ = _chunk_setup(*_gate_prep(g_ref[sq]), zero_lanes)
        for hd in range(MLSTM_HEADS):
            q = q_ref[sq, :, hd * MLSTM_DK:(hd + 1) * MLSTM_DK]
            k = k_ref[sq, :, hd * MLSTM_DK:(hd + 1) * MLSTM_DK]
            vt = v_ref[sq, :, hd * MLSTM_DV:(hd + 1) * MLSTM_DV].T
            s0t = _dot_nt(k, q)
            k32 = k.astype(F32)
            ht, wks = None, []
            for fwd in (True, False):
                kk = hd if fwd else MLSTM_HEADS + hd
                ht_d = _mlstm_dir_t(s0t, vt, _lane_bcast(u2, kk, n), bt2[kk:kk + 1, :], zero_row, fwd=fwd)
                ht = ht_d if ht is None else ht + ht_d
                wk = _lane_bcast(w, kk) * k32
                n_ref[sq, kk:kk + 1, :] = jnp.sum(wk, axis=0, keepdims=True)
                wks.append(wk.astype(BF16))
            c_both = _dot(vt, jnp.concatenate(wks, axis=1))
            c_ref[sq, hd] = c_both[:, :MLSTM_DK]
            c_ref[sq, MLSTM_HEADS + hd] = c_both[:, MLSTM_DK:]
            h_ref[sq, :, hd * MLSTM_DV:(hd + 1) * MLSTM_DV] = ht.T
        m_ref[sq] = m_new


CTX_SEQS_PER_STEP = 2


def _mlstm_ctx(mq, mk, mv, gates):
    b, s, _ = mq.shape
    ns = CTX_SEQS_PER_STEP
    seq = lambda c: pl.BlockSpec((ns, s, c), lambda i: (i, 0, 0))
    return pl.pallas_call(
        _mlstm_ctx_kernel,
        grid=(b // ns,),
        in_specs=[seq(MLSTM_HEADS * MLSTM_DK), seq(MLSTM_HEADS * MLSTM_DK), seq(MLSTM_HEADS * MLSTM_DV), seq(LANES)],
        out_specs=[seq(MLSTM_HEADS * MLSTM_DV),
                   pl.BlockSpec((ns, N_DIRHEAD, MLSTM_DV, MLSTM_DK), lambda i: (i, 0, 0, 0)),
                   pl.BlockSpec((ns, N_DIRHEAD, MLSTM_DK), lambda i: (i, 0, 0)),
                   pl.BlockSpec((ns, 1, LANES), lambda i: (i, 0, 0))],
        out_shape=[jax.ShapeDtypeStruct((b, s, MLSTM_HEADS * MLSTM_DV), F32),
                   jax.ShapeDtypeStruct((b, N_DIRHEAD, MLSTM_DV, MLSTM_DK), F32),
                   jax.ShapeDtypeStruct((b, N_DIRHEAD, MLSTM_DK), F32),
                   jax.ShapeDtypeStruct((b, 1, LANES), F32)],
        compiler_params=_params(("parallel",)),
        name="mlstm_ctx",
    )(mq, mk, mv, gates)


def _mlstm_lat_kernel(qf_ref, kf_ref, vf_ref, gf_ref, qb_ref, kb_ref, vb_ref, gb_ref, c0_ref, n0_ref, m0_ref,
                      h_ref, c_ref, n_ref, m_ref, *, n_chunks):
    step = pl.program_id(1)

    @pl.when(step == 0)
    def _():
        c_ref[...] = c0_ref[...]
        n_ref[...] = n0_ref[...]
        m_ref[...] = m0_ref[...]
        h_ref[...] = jnp.zeros_like(h_ref)

    n = qf_ref.shape[0]
    gate_f, gate_b = _gate_prep(gf_ref[...]), _gate_prep(gb_ref[...])
    is_f = lax.broadcasted_iota(jnp.int32, (n, LANES), 1) < MLSTM_HEADS
    b, u, tot = (jnp.where(is_f[:x.shape[0]], x, y) for x, y in zip(gate_f, gate_b))
    m_prev = m_ref[...]
    u2, bt2, w, m_new, decay = _chunk_setup(b, u, tot, m_prev)
    m2_prev = m_prev * LOG2E
    for fwd in (True, False):
        q_ref, k_ref, v_ref = (qf_ref, kf_ref, vf_ref) if fwd else (qb_ref, kb_ref, vb_ref)
        chunk = step if fwd else n_chunks - 1 - step
        row0 = pl.multiple_of(chunk * MLSTM_CHUNK, MLSTM_CHUNK)
        for hd in range(MLSTM_HEADS):
            kk = hd if fwd else MLSTM_HEADS + hd
            q = q_ref[:, hd * MLSTM_DK:(hd + 1) * MLSTM_DK]
            k = k_ref[:, hd * MLSTM_DK:(hd + 1) * MLSTM_DK]
            vt = v_ref[:, hd * MLSTM_DV:(hd + 1) * MLSTM_DV].T
            c_prev, n_prev = c_ref[kk], n_ref[kk:kk + 1, :]
            n_rows = jnp.broadcast_to(n_prev, (8, MLSTM_DK)).astype(BF16)
            inter = (_dot_nt(c_prev.astype(BF16), q), _dot_nt(n_rows, q)[0:1, :])
            ht = _mlstm_dir_t(_dot_nt(k, q), vt, _lane_bcast(u2, kk, n), bt2[kk:kk + 1, :],
                              _lane_bcast(m2_prev, kk, n), fwd=fwd, inter=inter)
            wk = _lane_bcast(w, kk) * k.astype(F32)
            dk = _lane_bcast(decay, kk)
            c_ref[kk] = dk * c_prev + _dot(vt, wk.astype(BF16))
            n_ref[kk:kk + 1, :] = dk * n_prev + jnp.sum(wk, axis=0, keepdims=True)
            h_ref[pl.ds(row0, MLSTM_CHUNK), hd * MLSTM_DV:(hd + 1) * MLSTM_DV] += ht.T
    m_ref[...] = m_new


def _mlstm_lat(mq, mk, mv, gates, c0, n0, m0):
    b, s, _ = mq.shape
    lc = MLSTM_CHUNK
    nc = s // lc
    fw = lambda c: pl.BlockSpec((None, lc, c), lambda i, j: (i, j, 0))
    bw = lambda c: pl.BlockSpec((None, lc, c), lambda i, j: (i, nc - 1 - j, 0))
    dk, dv = MLSTM_HEADS * MLSTM_DK, MLSTM_HEADS * MLSTM_DV
    return pl.pallas_call(
        functools.partial(_mlstm_lat_kernel, n_chunks=nc),
        grid=(b, nc),
        in_specs=[fw(dk), fw(dk), fw(dv), fw(LANES), bw(dk), bw(dk), bw(dv), bw(LANES),
                  pl.BlockSpec((None, N_DIRHEAD, MLSTM_DV, MLSTM_DK), lambda i, j: (i, 0, 0, 0)),
                  pl.BlockSpec((None, N_DIRHEAD, MLSTM_DK), lambda i, j: (i, 0, 0)),
                  pl.BlockSpec((None, 1, LANES), lambda i, j: (i, 0, 0))],
        out_specs=pl.BlockSpec((None, s, dv), lambda i, j: (i, 0, 0)),
        out_shape=jax.ShapeDtypeStruct((b, s, dv), F32),
        scratch_shapes=[pltpu.VMEM((N_DIRHEAD, MLSTM_DV, MLSTM_DK), F32),
                        pltpu.VMEM((N_DIRHEAD, MLSTM_DK), F32),
                        pltpu.VMEM((1, LANES), F32)],
        compiler_params=_params(("parallel", "arbitrary")),
        name="mlstm_lat",
    )(mq, mk, mv, gates, mq, mk, mv, gates, c0, n0, m0)


def _tail_kernel(x_ref, mod_ref, att_ref, hs_ref, smo_ref, sbr_ref, gml_ref, womla_ref, womlstm_ref, wout_ref,
                 gffn_ref, wfin_ref, wfout_ref, gfin_ref, y_ref):
    mod = mod_ref[...]
    a = _dot(att_ref[...], womla_ref[...])
    gml = gml_ref[...]
    parts = []
    for hd in range(MLSTM_HEADS):
        sl = slice(hd * MLSTM_DV, (hd + 1) * MLSTM_DV)
        parts.append((_rms(hs_ref[:, sl], gml[:, sl]) * smo_ref[:, sl].astype(F32)).astype(BF16))
    bm = _dot(jnp.concatenate(parts, axis=1), womlstm_ref[...])
    merged = sbr_ref[:, :D_MODEL].astype(F32) * a + sbr_ref[:, D_MODEL:].astype(F32) * bm
    x1 = x_ref[...] + mod[2:3] * _dot(merged.astype(BF16), wout_ref[...])
    h2 = (_rms(x1, gffn_ref[...]) * (1.0 + mod[4:5]) + mod[3:4]).astype(BF16)
    ga = _dot(h2, wfin_ref[:, :FFN_HIDDEN])
    gu = _dot(h2, wfin_ref[:, FFN_HIDDEN:])
    act = (ga * _sigmoid(ga) * gu).astype(BF16)
    x2 = x1 + mod[5:6] * _dot(act, wfout_ref[...])
    y_ref[...] = _rms(x2, gfin_ref[...])


def _tail(x, mod, att, hs, smo, sbr, gml, w_o_mla, w_o_mlstm, w_out, gffn, w_ffn_in, w_ffn_out, gfin, *, seq_len, name):
    n_tok = x.shape[0]
    tm = TOKEN_TILE
    tiles_per_seq = seq_len // tm
    tok = lambda c: pl.BlockSpec((tm, c), lambda i: (i, 0))
    consts = (gml, w_o_mla, w_o_mlstm, w_out, gffn, w_ffn_in, w_ffn_out, gfin)
    return pl.pallas_call(
        _tail_kernel,
        grid=(n_tok // tm,),
        in_specs=[tok(D_MODEL), _mod_spec(mod, tiles_per_seq),
                  tok(att.shape[1]), tok(hs.shape[1]), tok(smo.shape[1]), tok(sbr.shape[1])]
                 + [_const_spec(a.shape) for a in consts],
        out_specs=tok(D_MODEL),
        out_shape=jax.ShapeDtypeStruct((n_tok, D_MODEL), F32),
        compiler_params=_params(("parallel",)),
        name=name,
    )(x, mod, att, hs, smo, sbr, *consts)


_IN_SIZES = (Q_RANK, KV_RANK, ROPE_DIM, MLSTM_HEADS * MLSTM_DK, MLSTM_HEADS * MLSTM_DK, MLSTM_HEADS * MLSTM_DV,
             4 * MLSTM_HEADS, MLSTM_HEADS * MLSTM_DV, 2 * D_MODEL)
_IN_OFF = tuple(sum(_IN_SIZES[:i]) for i in range(len(_IN_SIZES) + 1))
PACK_ROWS = LANES
_PAD_AFTER = tuple((_SEG_OFF[i + 1] // PACK_ROWS, _SEG_WIDTHS[i] - _IN_SIZES[i], _IN_SIZES[i] % PACK_ROWS)
                   for i in range(len(_IN_SIZES)) if _SEG_WIDTHS[i] != _IN_SIZES[i])


PACK_PIECES = 8
N_PACK_STEPS = pl.cdiv(N_IN_PACKED // PACK_ROWS, PACK_PIECES)
_N_IN = _IN_OFF[-1]


def _pack_src_row(p):
    src = p * PACK_ROWS
    for first_piece_after, pad, _ in _PAD_AFTER:
        src = src - jnp.where(p >= first_piece_after, pad, 0)
    src = jnp.minimum(src, _N_IN - PACK_ROWS)
    return pl.multiple_of(src, math.gcd(PACK_ROWS, _N_IN, *(pad for _, pad, _ in _PAD_AFTER)))


def _pack_in_kernel(*refs):
    w_refs, o_ref = refs[:PACK_PIECES], refs[PACK_PIECES]
    row = lax.broadcasted_iota(jnp.int32, w_refs[0].shape, 0)
    for j, w_ref in enumerate(w_refs):
        p = pl.program_id(0) * PACK_PIECES + j
        valid = jnp.where(p < N_IN_PACKED // PACK_ROWS, PACK_ROWS, 0)
        for first_piece_after, _, width in _PAD_AFTER:
            valid = jnp.where(p == first_piece_after - 1, width, valid)
        o_ref[j * PACK_ROWS:(j + 1) * PACK_ROWS, :] = jnp.where(row < valid, w_ref[...], 0.0).astype(BF16)


def _pack_in(w_in_t):
    n, k = w_in_t.shape
    piece = lambda j: pl.BlockSpec((pl.Element(PACK_ROWS), pl.Element(k)),
                                   lambda i: (_pack_src_row(i * PACK_PIECES + j), 0))
    return pl.pallas_call(
        _pack_in_kernel,
        grid=(N_PACK_STEPS,),
        in_specs=[piece(j) for j in range(PACK_PIECES)],
        out_specs=pl.BlockSpec((PACK_PIECES * PACK_ROWS, k), lambda i: (i, 0)),
        out_shape=jax.ShapeDtypeStruct((N_PACK_STEPS * PACK_PIECES * PACK_ROWS, k), BF16),
        compiler_params=_params(("parallel",)),
        name="pack_in",
    )(*([w_in_t] * PACK_PIECES))


def _pack_small(w_uq, w_ukv, b_gates):
    qd = NOPE_DIM + ROPE_DIM
    w_uq_p = jnp.pad(w_uq.reshape(Q_RANK, MLA_HEADS, qd), ((0, 0), (0, 0), (0, HEAD_BLOCK - qd)))
    w_uq_p = w_uq_p.reshape(Q_RANK, MLA_HEADS * HEAD_BLOCK).astype(BF16)
    kv = w_ukv.reshape(KV_RANK, MLA_HEADS, NOPE_DIM + MLA_V_DIM)
    wk = jnp.pad(kv[:, :, :NOPE_DIM], ((0, 0), (0, 0), (0, HEAD_BLOCK - NOPE_DIM))).reshape(KV_RANK, -1)
    wv = kv[:, :, NOPE_DIM:].reshape(KV_RANK, -1)
    w_ukv_p = jnp.concatenate([wk, wv], axis=1).astype(BF16)
    bg = jnp.pad(b_gates, ((0, 0), (0, LANES - b_gates.shape[1])))
    return w_uq_p, w_ukv_p, bg


def _rope_tables(n_tokens):
    pos = np.arange(n_tokens)
    row = (pos // GRID_W).astype(np.float64)[:, None]
    col = (pos % GRID_W).astype(np.float64)[:, None]
    half = ROPE_DIM // 2
    inv = (np.float32(ROPE_BASE) ** (-np.arange(0, half, 2, dtype=np.float32) / np.float32(half))).astype(np.float64)
    r = np.arange(LANES) - ROPE_LANE0
    in_rope = (r >= 0) & (r < ROPE_DIM)
    rr = np.clip(r, 0, ROPE_DIM - 1)
    freq = inv[rr % (half // 2)][None, :]
    ang = np.where((rr // half == 0)[None, :], row * freq, col * freq).astype(np.float32).astype(np.float64)
    first = (rr % half) < (half // 2)
    cos = np.where(in_rope[None, :], np.cos(ang), 1.0)
    sin = np.sin(ang)
    sin_lo = np.where((in_rope & first)[None, :], -sin, 0.0)
    sin_hi = np.where((in_rope & ~first)[None, :], sin, 0.0)
    return tuple(jnp.asarray(t, dtype=F32) for t in (cos, sin_lo, sin_hi))


def kernel(x_prompt, x_sample, cache_ckv, cache_krope, state_C, state_n, state_m, c, c_ctx, w_mod, b_mod, g_norm_mix,
           w_in, b_gates, g_q_norm, w_uq, g_kv_norm, w_ukv, g_mlstm_norm, w_o_mla, w_o_mlstm, w_out, g_norm_ffn,
           w_ffn_in, w_ffn_out, g_final):
    bp, sp, d = x_prompt.shape
    bs, ss, _ = x_sample.shape
    layer = 0
    assert w_mod.shape[0] == 1 and sp == MLSTM_CHUNK and ss % MLSTM_CHUNK == 0

    cond = jnp.concatenate([c_ctx[None, :], c, jnp.zeros((8 - 1 - bs, d), F32)], axis=0)
    mod = _adaln(cond, w_mod[layer], b_mod[layer][None, :]).reshape(8, 6, d)
    mod_ctx, mod_lat = mod[0:1], mod[1:1 + bs]

    w_in_p = _pack_in(w_in[layer].T)
    w_uq_p, w_ukv_p, bg = _pack_small(w_uq[layer], w_ukv[layer], b_gates[layer][None, :])
    row = lambda g: g[layer][None, :]
    shared_in = (row(g_norm_mix), w_in_p, row(g_q_norm), w_uq_p, row(g_kv_norm), w_ukv_p, bg)
    tail_w = (row(g_mlstm_norm), w_o_mla[layer].astype(BF16), w_o_mlstm[layer].astype(BF16), w_out[layer].astype(BF16),
              row(g_norm_ffn), w_ffn_in[layer].astype(BF16), w_ffn_out[layer].astype(BF16), g_final[None, :])

    xc = x_prompt.reshape(bp * sp, d)
    q, kc, v, mq, mk, mv, gates, smo, sbr, ckv, kpe = _inproj(xc, mod_ctx, None, *shared_in, seq_len=sp)
    seq3 = lambda a, b_, s_: a.reshape(b_, s_, a.shape[-1])
    att = _attention(seq3(q, bp, sp), seq3(kc, bp, sp), seq3(v, bp, sp))
    hs, new_c, new_n, new_m = _mlstm_ctx(seq3(mq, bp, sp), seq3(mk, bp, sp), seq3(mv, bp, sp), seq3(gates, bp, sp))
    y_prompt = _tail(xc, mod_ctx, att.reshape(bp * sp, -1), hs.reshape(bp * sp, -1), smo, sbr, *tail_w,
                     seq_len=sp, name="tail_ctx").reshape(bp, sp, d)
    new_ckv = ckv.reshape(bp, 1, sp, KV_RANK)
    new_krope = kpe.reshape(bp, 1, sp, ROPE_DIM)
    new_C = new_c.reshape(bp, 1, 2, MLSTM_HEADS, MLSTM_DV, MLSTM_DK)
    new_N = new_n.reshape(bp, 1, 2, MLSTM_HEADS, MLSTM_DK)
    new_M = new_m[:, 0, :N_DIRHEAD].reshape(bp, 1, 2, MLSTM_HEADS)

    xl = x_sample.reshape(bs * ss, d)
    q, kc, v, mq, mk, mv, gates, smo, sbr = _inproj(xl, mod_lat, _rope_tables(ss), *shared_in, seq_len=ss)
    krope_blk = jnp.pad(cache_krope[:, layer], ((0, 0), (0, 0), (ROPE_LANE0, LANES - ROPE_LANE0 - ROPE_DIM)))
    cache = _cachekv(cache_ckv[:, layer], krope_blk, w_ukv_p)
    att = _attention(seq3(q, bs, ss), seq3(kc, bs, ss), seq3(v, bs, ss), cache)
    c0 = state_C[:, layer].reshape(bs, N_DIRHEAD, MLSTM_DV, MLSTM_DK)
    n0 = state_n[:, layer].reshape(bs, N_DIRHEAD, MLSTM_DK)
    m0 = jnp.pad(state_m[:, layer].reshape(bs, 1, N_DIRHEAD), ((0, 0), (0, 0), (0, LANES - N_DIRHEAD)))
    hs = _mlstm_lat(seq3(mq, bs, ss), seq3(mk, bs, ss), seq3(mv, bs, ss), seq3(gates, bs, ss), c0, n0, m0)
    y_sample = _tail(xl, mod_lat, att.reshape(bs * ss, -1), hs.reshape(bs * ss, -1), smo, sbr, *tail_w,
                     seq_len=ss, name="tail_lat").reshape(bs, ss, d)
    return (y_prompt, y_sample, new_ckv, new_krope, new_C, new_N, new_M)
```

```python
import functools
import math

import jax
import jax.numpy as jnp
import numpy as np
from jax import lax
from jax.experimental import pallas as pl
from jax.experimental.pallas import tpu as pltpu

F32 = jnp.float32
BF16 = jnp.bfloat16

D_MODEL = 1024
GRID_W = 64
MLA_HEADS = 8
Q_RANK = 384
KV_RANK = 256
NOPE_DIM = 64
ROPE_DIM = 32
MLA_V_DIM = 64
ROPE_BASE = 10000.0
MLA_SCALE = (NOPE_DIM + ROPE_DIM) ** -0.5
MLSTM_HEADS = 4
MLSTM_DK = 128
MLSTM_DV = 256
FFN_HIDDEN = ((8 * D_MODEL // 3 + 255) // 256) * 256
EPS = 1e-6

LANES = 128
HEAD_BLOCK = LANES
ROPE_LANE0 = NOPE_DIM
N_DIRHEAD = 2 * MLSTM_HEADS
MLSTM_CHUNK = 256
LOG2E = math.log2(math.e)
Q_PRESCALE = MLA_SCALE * LOG2E
V_SLAB = 2 * LANES
V_WIDTH = (MLA_HEADS // 2) * V_SLAB
TOKEN_TILE = 512
Q_TILE = 256
VMEM_LIMIT = 56 * 1024 * 1024

_SEG_WIDTHS = (Q_RANK, KV_RANK, LANES, MLSTM_HEADS * MLSTM_DK, MLSTM_HEADS * MLSTM_DK,
               MLSTM_HEADS * MLSTM_DV, LANES, MLSTM_HEADS * MLSTM_DV, 2 * D_MODEL)
_SEG_OFF = tuple(sum(_SEG_WIDTHS[:i]) for i in range(len(_SEG_WIDTHS) + 1))
SEG_Q, SEG_KV, SEG_KPE, SEG_MQ, SEG_MK, SEG_MV, SEG_GATE, SEG_MO, SEG_BR = (
    (_SEG_OFF[i], _SEG_OFF[i + 1]) for i in range(9))
N_IN_PACKED = _SEG_OFF[-1]


def _dot(a, b):
    return jnp.dot(a, b, preferred_element_type=F32)


def _dot_nt(a, b):
    return lax.dot_general(a, b, (((1,), (1,)), ((), ())), preferred_element_type=F32)


def _dot_tn(a, b):
    return lax.dot_general(a, b, (((0,), (0,)), ((), ())), preferred_element_type=F32)


def _rms(x, g):
    ms = jnp.mean(x * x, axis=-1, keepdims=True)
    return x * lax.rsqrt(ms + EPS) * g


def _sigmoid(x):
    return 1.0 / (1.0 + jnp.exp(-x))


def _const_spec(shape):
    nd = len(shape)
    return pl.BlockSpec(shape, lambda *_: (0,) * nd, pipeline_mode=pl.Buffered(1))


def _mod_spec(mod, tiles_per_seq):
    if mod.shape[0] == 1:
        return pl.BlockSpec((None,) + mod.shape[1:], lambda i: (0, 0, 0))
    return pl.BlockSpec((None,) + mod.shape[1:], lambda i: (i // tiles_per_seq, 0, 0))


def _params(sem):
    return pltpu.CompilerParams(dimension_semantics=sem, vmem_limit_bytes=VMEM_LIMIT)


N_MOD_MIXER = 2
N_MOD_TAIL = 4


def _adaln_block(c_ref, w_ref, b_ref, o_ref):
    c = c_ref[...]
    s = c * _sigmoid(c)
    o_ref[...] = _dot(s.astype(BF16), w_ref[...].astype(BF16)) + b_ref[...]


def _adaln(cond8, w_mod, b_mod):
    tn = D_MODEL
    return pl.pallas_call(
        _adaln_block,
        grid=(N_MOD_MIXER,),
        in_specs=[pl.BlockSpec((8, D_MODEL), lambda j: (0, 0)),
                  pl.BlockSpec((D_MODEL, tn), lambda j: (0, j)),
                  pl.BlockSpec((1, tn), lambda j: (0, j))],
        out_specs=pl.BlockSpec((8, tn), lambda j: (0, j)),
        out_shape=jax.ShapeDtypeStruct((8, N_MOD_MIXER * D_MODEL), F32),
        compiler_params=_params(("arbitrary",)),
        name="adaln",
    )(cond8, w_mod, b_mod)


def _rope_block(x, cos, sin_lo, sin_hi):
    return x * cos + pltpu.roll(x, LANES - 8, 1) * sin_lo + pltpu.roll(x, 8, 1) * sin_hi


def _store_kv(kvn, kp, kc_ref, v_ref):
    for hd in range(MLA_HEADS):
        sl = slice(hd * HEAD_BLOCK, (hd + 1) * HEAD_BLOCK)
        kc_ref[:, sl] = (kvn[:, sl] + kp).astype(BF16)
    v0 = MLA_HEADS * HEAD_BLOCK
    ones = jnp.ones((kvn.shape[0], LANES), BF16)
    for pair in range(MLA_HEADS // 2):
        v_ref[:, pair * V_SLAB:pair * V_SLAB + LANES] = kvn[:, v0 + pair * LANES:v0 + (pair + 1) * LANES].astype(BF16)
        v_ref[:, pair * V_SLAB + LANES:(pair + 1) * V_SLAB] = ones


def _inproj_kernel(*refs, rope):
    if rope:
        (x_ref, mod_ref, cos_ref, slo_ref, shi_ref, gmix_ref, win_ref, gq_ref, wuq_ref, gkv_ref, wukv_ref,
         bg_ref, q_ref, kc_ref, v_ref, mq_ref, mk_ref, mv_ref, gate_ref, smo_ref, sbr_ref) = refs
        cos, slo, shi = cos_ref[...], slo_ref[...], shi_ref[...]
    else:
        (x_ref, mod_ref, gmix_ref, win_ref, gq_ref, wuq_ref, gkv_ref, wukv_ref,
         bg_ref, q_ref, kc_ref, v_ref, mq_ref, mk_ref, mv_ref, gate_ref, smo_ref, sbr_ref,
         ckv_ref, kpe_ref) = refs
    mod = mod_ref[...]
    h = _rms(x_ref[...], gmix_ref[...]) * (1.0 + mod[1:2]) + mod[0:1]
    hb = h.astype(BF16)

    def proj(seg):
        return _dot_nt(hb, win_ref[seg[0]:seg[1], :])

    qn = _rms(proj(SEG_Q), gq_ref[...]).astype(BF16)
    q = _dot(qn, wuq_ref[...])
    for hd in range(MLA_HEADS):
        sl = slice(hd * HEAD_BLOCK, (hd + 1) * HEAD_BLOCK)
        qh = q[:, sl]
        if rope:
            qh = _rope_block(qh, cos, slo, shi)
        q_ref[:, sl] = (qh * Q_PRESCALE).astype(BF16)

    ckv = _rms(proj(SEG_KV), gkv_ref[...])
    zkpe = proj(SEG_KPE)
    if not rope:
        ckv_ref[...] = ckv
        kpe_ref[...] = zkpe[:, :ROPE_DIM]
    kp = pltpu.roll(zkpe, ROPE_LANE0, 1)
    if rope:
        kp = _rope_block(kp, cos, slo, shi)
    kvn = _dot(ckv.astype(BF16), wukv_ref[...])
    _store_kv(kvn, kp, kc_ref, v_ref)

    mq_ref[...] = proj(SEG_MQ).astype(BF16)
    mk_ref[...] = (proj(SEG_MK) * (MLSTM_DK ** -0.5)).astype(BF16)
    mv_ref[...] = proj(SEG_MV).astype(BF16)
    gate_ref[...] = proj(SEG_GATE) + bg_ref[...]
    smo_ref[...] = _sigmoid(proj(SEG_MO)).astype(BF16)
    sbr_ref[...] = _sigmoid(proj(SEG_BR)).astype(BF16)


def _inproj(x, mod, rope_tabs, gmix, w_in_p, gq, w_uq_p, gkv, w_ukv_p, bg, *, seq_len):
    n_tok = x.shape[0]
    tm = TOKEN_TILE
    tiles_per_seq = seq_len // tm
    rope = rope_tabs is not None
    tok = lambda c: pl.BlockSpec((tm, c), lambda i: (i, 0))
    in_specs = [tok(D_MODEL), _mod_spec(mod, tiles_per_seq)]
    args = [x, mod]
    if rope:
        in_specs += [pl.BlockSpec((tm, LANES), lambda i: (i % tiles_per_seq, 0))] * 3
        args += list(rope_tabs)
    in_specs += [_const_spec(a.shape) for a in (gmix, w_in_p, gq, w_uq_p, gkv, w_ukv_p, bg)]
    args += [gmix, w_in_p, gq, w_uq_p, gkv, w_ukv_p, bg]
    widths = [(MLA_HEADS * HEAD_BLOCK, BF16), (MLA_HEADS * HEAD_BLOCK, BF16), (V_WIDTH, BF16),
              (MLSTM_HEADS * MLSTM_DK, BF16), (MLSTM_HEADS * MLSTM_DK, BF16), (MLSTM_HEADS * MLSTM_DV, BF16),
              (LANES, F32), (MLSTM_HEADS * MLSTM_DV, BF16), (2 * D_MODEL, BF16)]
    if not rope:
        widths += [(KV_RANK, F32), (ROPE_DIM, F32)]
    return pl.pallas_call(
        functools.partial(_inproj_kernel, rope=rope),
        grid=(n_tok // tm,),
        in_specs=in_specs,
        out_specs=[tok(c) for c, _ in widths],
        out_shape=[jax.ShapeDtypeStruct((n_tok, c), dt) for c, dt in widths],
        compiler_params=_params(("parallel",)),
        name="inproj_lat" if rope else "inproj_ctx",
    )(*args)


def _cachekv_kernel(ckv_ref, kp_ref, wukv_ref, kc_ref, v_ref):
    kvn = _dot(ckv_ref[...].astype(BF16), wukv_ref[...])
    _store_kv(kvn, kp_ref[...], kc_ref, v_ref)


def _cachekv(cache_ckv, krope_blk, w_ukv_p):
    b, past, _ = cache_ckv.shape
    return pl.pallas_call(
        _cachekv_kernel,
        grid=(b,),
        in_specs=[pl.BlockSpec((None, past, KV_RANK), lambda i: (i, 0, 0)),
                  pl.BlockSpec((None, past, LANES), lambda i: (i, 0, 0)),
                  _const_spec(w_ukv_p.shape)],
        out_specs=[pl.BlockSpec((None, past, MLA_HEADS * HEAD_BLOCK), lambda i: (i, 0, 0)),
                   pl.BlockSpec((None, past, V_WIDTH), lambda i: (i, 0, 0))],
        out_shape=[jax.ShapeDtypeStruct((b, past, MLA_HEADS * HEAD_BLOCK), BF16),
                   jax.ShapeDtypeStruct((b, past, V_WIDTH), BF16)],
        compiler_params=_params(("parallel",)),
        name="cachekv",
    )(cache_ckv, krope_blk, w_ukv_p)


def _attn_kernel(*refs, has_cache, n_cast, has_adaln):
    n_in = 5 if has_cache else 3
    if has_cache:
        q_ref, k_ref, v_ref, kc_ref, vc_ref = refs[:n_in]
    else:
        q_ref, k_ref, v_ref = refs[:n_in]
    n_all_in = n_in + n_cast + (3 if has_adaln else 0)
    o_ref = refs[n_all_in]
    for w_ref, wb_ref in zip(refs[n_in:n_in + n_cast], refs[n_all_in + 1:n_all_in + 1 + n_cast]):
        wb_ref[...] = w_ref[...].astype(BF16)
    if has_adaln:
        _adaln_block(*refs[n_in + n_cast:n_all_in], refs[n_all_in + 1 + n_cast])
    tq = q_ref.shape[0]
    lane = lax.broadcasted_iota(jnp.int32, (tq, LANES), 1)
    for pair in range(MLA_HEADS // 2):
        vsl = slice(pair * V_SLAB, (pair + 1) * V_SLAB)
        outs = []
        for j in range(2):
            hd = 2 * pair + j
            sl = slice(hd * HEAD_BLOCK, (hd + 1) * HEAD_BLOCK)
            qh = q_ref[:, sl]
            s = _dot_nt(qh, k_ref[:, sl])
            m = jnp.max(s, axis=-1, keepdims=True)
            if has_cache:
                sc = _dot_nt(qh, kc_ref[:, sl])
                m = jnp.maximum(m, jnp.max(sc, axis=-1, keepdims=True))
            o = _dot(jnp.exp2(s - m).astype(BF16), v_ref[:, vsl])
            if has_cache:
                o = o + _dot(jnp.exp2(sc - m).astype(BF16), vc_ref[:, vsl])
            outs.append(o[:, :LANES] * (1.0 / o[:, LANES:]))
        o_ref[:, pair * LANES:(pair + 1) * LANES] = jnp.where(lane < MLA_V_DIM, outs[0], outs[1]).astype(BF16)


def _attention(q, k, v, cache=None, cast_weights=(), adaln_tail=None):
    b, s, _ = q.shape
    tq = Q_TILE
    n_steps = b * (s // tq)
    kw, vw, ow = MLA_HEADS * HEAD_BLOCK, V_WIDTH, MLA_HEADS * MLA_V_DIM
    in_specs = [pl.BlockSpec((None, tq, kw), lambda i, j: (i, j, 0)),
                pl.BlockSpec((None, s, kw), lambda i, j: (i, 0, 0)),
                pl.BlockSpec((None, s, vw), lambda i, j: (i, 0, 0))]
    args = [q, k, v]
    if cache is not None:
        past = cache[0].shape[1]
        in_specs += [pl.BlockSpec((None, past, kw), lambda i, j: (i, 0, 0)),
                     pl.BlockSpec((None, past, vw), lambda i, j: (i, 0, 0))]
        args += list(cache)
    step = lambda i, j: i * (s // tq) + j
    w_specs = [pl.BlockSpec((w.shape[0] // n_steps, w.shape[1]), lambda i, j: (step(i, j), 0)) for w in cast_weights]
    extra_in, extra_out, extra_shape = [], [], []
    if adaln_tail is not None:
        tn = N_MOD_TAIL * D_MODEL // n_steps
        col0 = N_MOD_MIXER * D_MODEL // tn
        extra_in = [pl.BlockSpec((8, D_MODEL), lambda i, j: (0, 0)),
                    pl.BlockSpec((D_MODEL, tn), lambda i, j: (0, col0 + step(i, j))),
                    pl.BlockSpec((1, tn), lambda i, j: (0, col0 + step(i, j)))]
        extra_out = [pl.BlockSpec((8, tn), lambda i, j: (0, step(i, j)))]
        extra_shape = [jax.ShapeDtypeStruct((8, N_MOD_TAIL * D_MODEL), F32)]
    outs = pl.pallas_call(
        functools.partial(_attn_kernel, has_cache=cache is not None, n_cast=len(cast_weights),
                          has_adaln=adaln_tail is not None),
        grid=(b, s // tq),
        in_specs=in_specs + w_specs + extra_in,
        out_specs=[pl.BlockSpec((None, tq, ow), lambda i, j: (i, j, 0))] + w_specs + extra_out,
        out_shape=[jax.ShapeDtypeStruct((b, s, ow), BF16)]
                  + [jax.ShapeDtypeStruct(w.shape, BF16) for w in cast_weights] + extra_shape,
        compiler_params=_params(("parallel", "arbitrary")),
        name="attn_lat" if cache is not None else "attn_ctx",
    )(*args, *cast_weights, *(adaln_tail or ()))
    return outs[0] if len(outs) == 1 else outs


def _lane_bcast(x, k, width=None):
    y = jnp.broadcast_to(x[:, k:k + 1], x.shape)
    reps = (width or LANES) // LANES
    return y if reps == 1 else jnp.concatenate([y] * reps, axis=1)


def _prefix_sum_rows(x):
    n = x.shape[0]
    r = lax.broadcasted_iota(jnp.int32, (n, n), 0)
    c = lax.broadcasted_iota(jnp.int32, (n, n), 1)
    tri = jnp.where(c <= r, 1.0, 0.0).astype(BF16)
    hi = x.astype(BF16)
    r1 = x - hi.astype(F32)
    mid = r1.astype(BF16)
    lo = (r1 - mid.astype(F32)).astype(BF16)
    return _dot(tri, hi) + _dot(tri, mid) + _dot(tri, lo)


def _gate_prep(g):
    n = g.shape[0]
    lane = lax.broadcasted_iota(jnp.int32, (n, LANES), 1)
    fpre = pltpu.roll(g, LANES - N_DIRHEAD, 1)
    lf = jnp.minimum(fpre, 0.0) - jnp.log(1.0 + jnp.exp(-jnp.abs(fpre)))
    binc = _prefix_sum_rows(lf)
    tot = binc[n - 1:n, :]
    b = jnp.where(lane < MLSTM_HEADS, binc, tot - binc + lf)
    return b, g - b, tot


def _chunk_setup(b, u, tot, m_prev):
    g_last = jnp.maximum(jnp.max(u, axis=0, keepdims=True), m_prev)
    u2 = u * LOG2E
    w = jnp.exp2(u2 - g_last * LOG2E)
    return u2, (b * LOG2E).T, w, tot + g_last, jnp.exp(m_prev - g_last)


def _mlstm_dir_t(s0t, vt, u_b, bt_row, m_row, *, fwd, inter=None):
    n = s0t.shape[0]
    r = lax.broadcasted_iota(jnp.int32, (n, n), 0)
    c = lax.broadcasted_iota(jnp.int32, (n, n), 1)
    a = jnp.where((r <= c) if fwd else (r >= c), u_b, -jnp.inf)
    g = jnp.maximum(jnp.max(a, axis=0, keepdims=True), m_row)
    sdt = s0t * jnp.exp2(a - g)
    den = jnp.sum(sdt, axis=0, keepdims=True)
    ht = _dot(vt, sdt.astype(BF16))
    if inter is not None:
        w_inter = jnp.exp2(m_row - g)
        ht = ht + w_inter * inter[0]
        den = den + w_inter * inter[1]
    floor = jnp.exp2(-(bt_row + g))
    return ht * (1.0 / jnp.maximum(jnp.abs(den), floor))


def _mlstm_ctx_kernel(q_ref, k_ref, v_ref, g_ref, h_ref, c_ref, n_ref, m_ref):
    n = q_ref.shape[1]
    zero_lanes = jnp.zeros((1, LANES), F32)
    zero_row = jnp.zeros((1, n), F32)
    for sq in range(q_ref.shape[0]):
        u2, bt2, w, m_new, _ = _chunk_setup(*_gate_prep(g_ref[sq]), zero_lanes)
        for hd in range(MLSTM_HEADS):
            q = q_ref[sq, :, hd * MLSTM_DK:(hd + 1) * MLSTM_DK]
            k = k_ref[sq, :, hd * MLSTM_DK:(hd + 1) * MLSTM_DK]
            vt = v_ref[sq, :, hd * MLSTM_DV:(hd + 1) * MLSTM_DV].T
            s0t = _dot_nt(k, q)
            k32 = k.astype(F32)
            ht, wks = None, []
            for fwd in (True, False):
                kk = hd if fwd else MLSTM_HEADS + hd
                ht_d = _mlstm_dir_t(s0t, vt, _lane_bcast(u2, kk, n), bt2[kk:kk + 1, :], zero_row, fwd=fwd)
                ht = ht_d if ht is None else ht + ht_d
                wk = _lane_bcast(w, kk) * k32
                n_ref[sq, kk:kk + 1, :] = jnp.sum(wk, axis=0, keepdims=True)
                wks.append(wk.astype(BF16))
            c_both = _dot(vt, jnp.concatenate(wks, axis=1))
            c_ref[sq, hd] = c_both[:, :MLSTM_DK]
            c_ref[sq, MLSTM_HEADS + hd] = c_both[:, MLSTM_DK:]
            h_ref[sq, :, hd * MLSTM_DV:(hd + 1) * MLSTM_DV] = ht.T
        m_ref[sq] = m_new


CTX_SEQS_PER_STEP = 2


def _mlstm_ctx(mq, mk, mv, gates):
    b, s, _ = mq.shape
    ns = CTX_SEQS_PER_STEP
    seq = lambda c: pl.BlockSpec((ns, s, c), lambda i: (i, 0, 0))
    return pl.pallas_call(
        _mlstm_ctx_kernel,
        grid=(b // ns,),
        in_specs=[seq(MLSTM_HEADS * MLSTM_DK), seq(MLSTM_HEADS * MLSTM_DK), seq(MLSTM_HEADS * MLSTM_DV), seq(LANES)],
        out_specs=[seq(MLSTM_HEADS * MLSTM_DV),
                   pl.BlockSpec((ns, N_DIRHEAD, MLSTM_DV, MLSTM_DK), lambda i: (i, 0, 0, 0)),
                   pl.BlockSpec((ns, N_DIRHEAD, MLSTM_DK), lambda i: (i, 0, 0)),
                   pl.BlockSpec((ns, 1, LANES), lambda i: (i, 0, 0))],
        out_shape=[jax.ShapeDtypeStruct((b, s, MLSTM_HEADS * MLSTM_DV), F32),
                   jax.ShapeDtypeStruct((b, N_DIRHEAD, MLSTM_DV, MLSTM_DK), F32),
                   jax.ShapeDtypeStruct((b, N_DIRHEAD, MLSTM_DK), F32),
                   jax.ShapeDtypeStruct((b, 1, LANES), F32)],
        compiler_params=_params(("parallel",)),
        name="mlstm_ctx",
    )(mq, mk, mv, gates)


def _mlstm_lat_kernel(qf_ref, kf_ref, vf_ref, gf_ref, qb_ref, kb_ref, vb_ref, gb_ref, c0_ref, n0_ref, m0_ref,
                      h_ref, c_ref, n_ref, m_ref, *, n_chunks):
    step = pl.program_id(1)

    @pl.when(step == 0)
    def _():
        c_ref[...] = c0_ref[...]
        n_ref[...] = n0_ref[...]
        m_ref[...] = m0_ref[...]
        h_ref[...] = jnp.zeros_like(h_ref)

    n = qf_ref.shape[0]
    gate_f, gate_b = _gate_prep(gf_ref[...]), _gate_prep(gb_ref[...])
    is_f = lax.broadcasted_iota(jnp.int32, (n, LANES), 1) < MLSTM_HEADS
    b, u, tot = (jnp.where(is_f[:x.shape[0]], x, y) for x, y in zip(gate_f, gate_b))
    m_prev = m_ref[...]
    u2, bt2, w, m_new, decay = _chunk_setup(b, u, tot, m_prev)
    m2_prev = m_prev * LOG2E
    for fwd in (True, False):
        q_ref, k_ref, v_ref = (qf_ref, kf_ref, vf_ref) if fwd else (qb_ref, kb_ref, vb_ref)
        chunk = step if fwd else n_chunks - 1 - step
        row0 = pl.multiple_of(chunk * MLSTM_CHUNK, MLSTM_CHUNK)
        for hd in range(MLSTM_HEADS):
            kk = hd if fwd else MLSTM_HEADS + hd
            q = q_ref[:, hd * MLSTM_DK:(hd + 1) * MLSTM_DK]
            k = k_ref[:, hd * MLSTM_DK:(hd + 1) * MLSTM_DK]
            vt = v_ref[:, hd * MLSTM_DV:(hd + 1) * MLSTM_DV].T
            c_prev, n_prev = c_ref[kk], n_ref[kk:kk + 1, :]
            n_rows = jnp.broadcast_to(n_prev, (8, MLSTM_DK)).astype(BF16)
            inter = (_dot_nt(c_prev.astype(BF16), q), _dot_nt(n_rows, q)[0:1, :])
            ht = _mlstm_dir_t(_dot_nt(k, q), vt, _lane_bcast(u2, kk, n), bt2[kk:kk + 1, :],
                              _lane_bcast(m2_prev, kk, n), fwd=fwd, inter=inter)
            wk = _lane_bcast(w, kk) * k.astype(F32)
            dk = _lane_bcast(decay, kk)
            c_ref[kk] = dk * c_prev + _dot(vt, wk.astype(BF16))
            n_ref[kk:kk + 1, :] = dk * n_prev + jnp.sum(wk, axis=0, keepdims=True)
            h_ref[pl.ds(row0, MLSTM_CHUNK), hd * MLSTM_DV:(hd + 1) * MLSTM_DV] += ht.T
    m_ref[...] = m_new


def _mlstm_lat(mq, mk, mv, gates, c0, n0, m0):
    b, s, _ = mq.shape
    lc = MLSTM_CHUNK
    nc = s // lc
    fw = lambda c: pl.BlockSpec((None, lc, c), lambda i, j: (i, j, 0))
    bw = lambda c: pl.BlockSpec((None, lc, c), lambda i, j: (i, nc - 1 - j, 0))
    dk, dv = MLSTM_HEADS * MLSTM_DK, MLSTM_HEADS * MLSTM_DV
    return pl.pallas_call(
        functools.partial(_mlstm_lat_kernel, n_chunks=nc),
        grid=(b, nc),
        in_specs=[fw(dk), fw(dk), fw(dv), fw(LANES), bw(dk), bw(dk), bw(dv), bw(LANES),
                  pl.BlockSpec((None, N_DIRHEAD, MLSTM_DV, MLSTM_DK), lambda i, j: (i, 0, 0, 0)),
                  pl.BlockSpec((None, N_DIRHEAD, MLSTM_DK), lambda i, j: (i, 0, 0)),
                  pl.BlockSpec((None, 1, LANES), lambda i, j: (i, 0, 0))],
        out_specs=pl.BlockSpec((None, s, dv), lambda i, j: (i, 0, 0)),
        out_shape=jax.ShapeDtypeStruct((b, s, dv), F32),
        scratch_shapes=[pltpu.VMEM((N_DIRHEAD, MLSTM_DV, MLSTM_DK), F32),
                        pltpu.VMEM((N_DIRHEAD, MLSTM_DK), F32),
                        pltpu.VMEM((1, LANES), F32)],
        compiler_params=_params(("parallel", "arbitrary")),
        name="mlstm_lat",
    )(mq, mk, mv, gates, mq, mk, mv, gates, c0, n0, m0)


def _tail_kernel(x_ref, mod_ref, att_ref, hs_ref, smo_ref, sbr_ref, gml_ref, womla_ref, womlstm_ref, wout_ref,
                 gffn_ref, wfin_ref, wfout_ref, gfin_ref, y_ref):
    mod = mod_ref[...]
    a = _dot(att_ref[...], womla_ref[...])
    gml = gml_ref[...]
    parts = []
    for hd in range(MLSTM_HEADS):
        sl = slice(hd * MLSTM_DV, (hd + 1) * MLSTM_DV)
        parts.append((_rms(hs_ref[:, sl], gml[:, sl]) * smo_ref[:, sl].astype(F32)).astype(BF16))
    bm = _dot(jnp.concatenate(parts, axis=1), womlstm_ref[...])
    merged = sbr_ref[:, :D_MODEL].astype(F32) * a + sbr_ref[:, D_MODEL:].astype(F32) * bm
    x1 = x_ref[...] + mod[0:1] * _dot(merged.astype(BF16), wout_ref[...])
    h2 = (_rms(x1, gffn_ref[...]) * (1.0 + mod[2:3]) + mod[1:2]).astype(BF16)
    ga = _dot(h2, wfin_ref[:, :FFN_HIDDEN])
    gu = _dot(h2, wfin_ref[:, FFN_HIDDEN:])
    act = (ga * _sigmoid(ga) * gu).astype(BF16)
    x2 = x1 + mod[3:4] * _dot(act, wfout_ref[...])
    y_ref[...] = _rms(x2, gfin_ref[...])


def _tail(x, mod, att, hs, smo, sbr, gml, w_o_mla, w_o_mlstm, w_out, gffn, w_ffn_in, w_ffn_out, gfin, *, seq_len, name):
    n_tok = x.shape[0]
    tm = TOKEN_TILE
    tiles_per_seq = seq_len // tm
    tok = lambda c: pl.BlockSpec((tm, c), lambda i: (i, 0))
    consts = (gml, w_o_mla, w_o_mlstm, w_out, gffn, w_ffn_in, w_ffn_out, gfin)
    return pl.pallas_call(
        _tail_kernel,
        grid=(n_tok // tm,),
        in_specs=[tok(D_MODEL), _mod_spec(mod, tiles_per_seq),
                  tok(att.shape[1]), tok(hs.shape[1]), tok(smo.shape[1]), tok(sbr.shape[1])]
                 + [_const_spec(a.shape) for a in consts],
        out_specs=tok(D_MODEL),
        out_shape=jax.ShapeDtypeStruct((n_tok, D_MODEL), F32),
        compiler_params=_params(("parallel",)),
        name=name,
    )(x, mod, att, hs, smo, sbr, *consts)


_IN_SIZES = (Q_RANK, KV_RANK, ROPE_DIM, MLSTM_HEADS * MLSTM_DK, MLSTM_HEADS * MLSTM_DK, MLSTM_HEADS * MLSTM_DV,
             4 * MLSTM_HEADS, MLSTM_HEADS * MLSTM_DV, 2 * D_MODEL)
_IN_OFF = tuple(sum(_IN_SIZES[:i]) for i in range(len(_IN_SIZES) + 1))
PACK_ROWS = LANES
_PAD_AFTER = tuple((_SEG_OFF[i + 1] // PACK_ROWS, _SEG_WIDTHS[i] - _IN_SIZES[i], _IN_SIZES[i] % PACK_ROWS)
                   for i in range(len(_IN_SIZES)) if _SEG_WIDTHS[i] != _IN_SIZES[i])


PACK_PIECES = 8
N_PACK_STEPS = pl.cdiv(N_IN_PACKED // PACK_ROWS, PACK_PIECES)
_N_IN = _IN_OFF[-1]


def _pack_src_row(p):
    src = p * PACK_ROWS
    for first_piece_after, pad, _ in _PAD_AFTER:
        src = src - jnp.where(p >= first_piece_after, pad, 0)
    src = jnp.minimum(src, _N_IN - PACK_ROWS)
    return pl.multiple_of(src, math.gcd(PACK_ROWS, _N_IN, *(pad for _, pad, _ in _PAD_AFTER)))


def _pack_in_kernel(*refs):
    w_refs, o_ref = refs[:PACK_PIECES], refs[PACK_PIECES]
    row = lax.broadcasted_iota(jnp.int32, w_refs[0].shape, 0)
    for j, w_ref in enumerate(w_refs):
        p = pl.program_id(0) * PACK_PIECES + j
        valid = jnp.where(p < N_IN_PACKED // PACK_ROWS, PACK_ROWS, 0)
        for first_piece_after, _, width in _PAD_AFTER:
            valid = jnp.where(p == first_piece_after - 1, width, valid)
        o_ref[j * PACK_ROWS:(j + 1) * PACK_ROWS, :] = jnp.where(row < valid, w_ref[...], 0.0).astype(BF16)


def _pack_in(w_in_t):
    n, k = w_in_t.shape
    piece = lambda j: pl.BlockSpec((pl.Element(PACK_ROWS), pl.Element(k)),
                                   lambda i: (_pack_src_row(i * PACK_PIECES + j), 0))
    return pl.pallas_call(
        _pack_in_kernel,
        grid=(N_PACK_STEPS,),
        in_specs=[piece(j) for j in range(PACK_PIECES)],
        out_specs=pl.BlockSpec((PACK_PIECES * PACK_ROWS, k), lambda i: (i, 0)),
        out_shape=jax.ShapeDtypeStruct((N_PACK_STEPS * PACK_PIECES * PACK_ROWS, k), BF16),
        compiler_params=_params(("parallel",)),
        name="pack_in",
    )(*([w_in_t] * PACK_PIECES))


def _pack_small(w_uq, w_ukv, b_gates):
    qd = NOPE_DIM + ROPE_DIM
    w_uq_p = jnp.pad(w_uq.reshape(Q_RANK, MLA_HEADS, qd), ((0, 0), (0, 0), (0, HEAD_BLOCK - qd)))
    w_uq_p = w_uq_p.reshape(Q_RANK, MLA_HEADS * HEAD_BLOCK).astype(BF16)
    kv = w_ukv.reshape(KV_RANK, MLA_HEADS, NOPE_DIM + MLA_V_DIM)
    wk = jnp.pad(kv[:, :, :NOPE_DIM], ((0, 0), (0, 0), (0, HEAD_BLOCK - NOPE_DIM))).reshape(KV_RANK, -1)
    wv = kv[:, :, NOPE_DIM:].reshape(KV_RANK, -1)
    w_ukv_p = jnp.concatenate([wk, wv], axis=1).astype(BF16)
    bg = jnp.pad(b_gates, ((0, 0), (0, LANES - b_gates.shape[1])))
    return w_uq_p, w_ukv_p, bg


def _rope_tables(n_tokens):
    pos = np.arange(n_tokens)
    row = (pos // GRID_W).astype(np.float64)[:, None]
    col = (pos % GRID_W).astype(np.float64)[:, None]
    half = ROPE_DIM // 2
    inv = (np.float32(ROPE_BASE) ** (-np.arange(0, half, 2, dtype=np.float32) / np.float32(half))).astype(np.float64)
    r = np.arange(LANES) - ROPE_LANE0
    in_rope = (r >= 0) & (r < ROPE_DIM)
    rr = np.clip(r, 0, ROPE_DIM - 1)
    freq = inv[rr % (half // 2)][None, :]
    ang = np.where((rr // half == 0)[None, :], row * freq, col * freq).astype(np.float32).astype(np.float64)
    first = (rr % half) < (half // 2)
    cos = np.where(in_rope[None, :], np.cos(ang), 1.0)
    sin = np.sin(ang)
    sin_lo = np.where((in_rope & first)[None, :], -sin, 0.0)
    sin_hi = np.where((in_rope & ~first)[None, :], sin, 0.0)
    return tuple(jnp.asarray(t, dtype=F32) for t in (cos, sin_lo, sin_hi))


def kernel(x_prompt, x_sample, cache_ckv, cache_krope, state_C, state_n, state_m, c, c_ctx, w_mod, b_mod, g_norm_mix,
           w_in, b_gates, g_q_norm, w_uq, g_kv_norm, w_ukv, g_mlstm_norm, w_o_mla, w_o_mlstm, w_out, g_norm_ffn,
           w_ffn_in, w_ffn_out, g_final):
    bp, sp, d = x_prompt.shape
    bs, ss, _ = x_sample.shape
    layer = 0
    assert w_mod.shape[0] == 1 and sp == MLSTM_CHUNK and ss % MLSTM_CHUNK == 0

    cond = jnp.concatenate([c_ctx[None, :], c, jnp.zeros((8 - 1 - bs, d), F32)], axis=0)
    adaln_args = (cond, w_mod[layer], b_mod[layer][None, :])
    mod_in = _adaln(*adaln_args).reshape(8, N_MOD_MIXER, d)
    mod_ctx, mod_lat = mod_in[0:1], mod_in[1:1 + bs]

    w_in_p = _pack_in(w_in[layer].T)
    w_uq_p, w_ukv_p, bg = _pack_small(w_uq[layer], w_ukv[layer], b_gates[layer][None, :])
    row = lambda g: g[layer][None, :]
    shared_in = (row(g_norm_mix), w_in_p, row(g_q_norm), w_uq_p, row(g_kv_norm), w_ukv_p, bg)
    seq3 = lambda a, b_, s_: a.reshape(b_, s_, a.shape[-1])

    xl = x_sample.reshape(bs * ss, d)
    q, kc, v, mq_l, mk_l, mv_l, gates_l, smo_l, sbr_l = _inproj(xl, mod_lat, _rope_tables(ss), *shared_in, seq_len=ss)
    krope_blk = jnp.pad(cache_krope[:, layer], ((0, 0), (0, 0), (ROPE_LANE0, LANES - ROPE_LANE0 - ROPE_DIM)))
    cache = _cachekv(cache_ckv[:, layer], krope_blk, w_ukv_p)
    att_l, womla_b, womlstm_b, wout_b, wfin_b, wfout_b, mod_tail = _attention(
        seq3(q, bs, ss), seq3(kc, bs, ss), seq3(v, bs, ss), cache,
        cast_weights=(w_o_mla[layer], w_o_mlstm[layer], w_out[layer], w_ffn_in[layer], w_ffn_out[layer]),
        adaln_tail=adaln_args)
    tail_w = (row(g_mlstm_norm), womla_b, womlstm_b, wout_b, row(g_norm_ffn), wfin_b, wfout_b, g_final[None, :])
    mod_tail = mod_tail.reshape(8, N_MOD_TAIL, d)
    modt_ctx, modt_lat = mod_tail[0:1], mod_tail[1:1 + bs]

    xc = x_prompt.reshape(bp * sp, d)
    q, kc, v, mq, mk, mv, gates, smo, sbr, ckv, kpe = _inproj(xc, mod_ctx, None, *shared_in, seq_len=sp)
    att = _attention(seq3(q, bp, sp), seq3(kc, bp, sp), seq3(v, bp, sp))
    hs, new_c, new_n, new_m = _mlstm_ctx(seq3(mq, bp, sp), seq3(mk, bp, sp), seq3(mv, bp, sp), seq3(gates, bp, sp))
    y_prompt = _tail(xc, modt_ctx, att.reshape(bp * sp, -1), hs.reshape(bp * sp, -1), smo, sbr, *tail_w,
                     seq_len=sp, name="tail_ctx").reshape(bp, sp, d)
    new_ckv = ckv.reshape(bp, 1, sp, KV_RANK)
    new_krope = kpe.reshape(bp, 1, sp, ROPE_DIM)
    new_C = new_c.reshape(bp, 1, 2, MLSTM_HEADS, MLSTM_DV, MLSTM_DK)
    new_N = new_n.reshape(bp, 1, 2, MLSTM_HEADS, MLSTM_DK)
    new_M = new_m[:, 0, :N_DIRHEAD].reshape(bp, 1, 2, MLSTM_HEADS)

    c0 = state_C[:, layer].reshape(bs, N_DIRHEAD, MLSTM_DV, MLSTM_DK)
    n0 = state_n[:, layer].reshape(bs, N_DIRHEAD, MLSTM_DK)
    m0 = jnp.pad(state_m[:, layer].reshape(bs, 1, N_DIRHEAD), ((0, 0), (0, 0), (0, LANES - N_DIRHEAD)))
    hs = _mlstm_lat(seq3(mq_l, bs, ss), seq3(mk_l, bs, ss), seq3(mv_l, bs, ss), seq3(gates_l, bs, ss), c0, n0, m0)
    y_sample = _tail(xl, modt_lat, att_l.reshape(bs * ss, -1), hs.reshape(bs * ss, -1), smo_l, sbr_l, *tail_w,
                     seq_len=ss, name="tail_lat").reshape(bs, ss, d)
    return (y_prompt, y_sample, new_ckv, new_krope, new_C, new_N, new_M)
```

```python
import functools
import math

import jax
import jax.numpy as jnp
import numpy as np
from jax import lax
from jax.experimental import pallas as pl
from jax.experimental.pallas import tpu as pltpu

F32 = jnp.float32
BF16 = jnp.bfloat16

D_MODEL = 1024
GRID_W = 64
MLA_HEADS = 8
Q_RANK = 384
KV_RANK = 256
NOPE_DIM = 64
ROPE_DIM = 32
MLA_V_DIM = 64
ROPE_BASE = 10000.0
MLA_SCALE = (NOPE_DIM + ROPE_DIM) ** -0.5
MLSTM_HEADS = 4
MLSTM_DK = 128
MLSTM_DV = 256
FFN_HIDDEN = ((8 * D_MODEL // 3 + 255) // 256) * 256
EPS = 1e-6

LANES = 128
HEAD_BLOCK = LANES
ROPE_LANE0 = NOPE_DIM
N_DIRHEAD = 2 * MLSTM_HEADS
MLSTM_CHUNK = 256
LOG2E = math.log2(math.e)
Q_PRESCALE = MLA_SCALE * LOG2E
V_SLAB = 2 * LANES
V_WIDTH = (MLA_HEADS // 2) * V_SLAB
TOKEN_TILE = 512
TAIL_ROWS = 256
INPROJ_ROWS = TOKEN_TILE
Q_TILE = 256
VMEM_LIMIT = 56 * 1024 * 1024

_SEG_WIDTHS = (Q_RANK, KV_RANK, LANES, MLSTM_HEADS * MLSTM_DK, MLSTM_HEADS * MLSTM_DK,
               MLSTM_HEADS * MLSTM_DV, LANES, MLSTM_HEADS * MLSTM_DV, 2 * D_MODEL)
_SEG_OFF = tuple(sum(_SEG_WIDTHS[:i]) for i in range(len(_SEG_WIDTHS) + 1))
SEG_Q, SEG_KV, SEG_KPE, SEG_MQ, SEG_MK, SEG_MV, SEG_GATE, SEG_MO, SEG_BR = (
    (_SEG_OFF[i], _SEG_OFF[i + 1]) for i in range(9))
N_IN_PACKED = _SEG_OFF[-1]


def _dot(a, b):
    return jnp.dot(a, b, preferred_element_type=F32)


def _dot_nt(a, b):
    return lax.dot_general(a, b, (((1,), (1,)), ((), ())), preferred_element_type=F32)


def _dot_tn(a, b):
    return lax.dot_general(a, b, (((0,), (0,)), ((), ())), preferred_element_type=F32)


def _rms(x, g):
    ms = jnp.mean(x * x, axis=-1, keepdims=True)
    return x * lax.rsqrt(ms + EPS) * g


def _sigmoid(x):
    return 1.0 / (1.0 + jnp.exp(-x))


def _const_spec(shape):
    nd = len(shape)
    return pl.BlockSpec(shape, lambda *_: (0,) * nd, pipeline_mode=pl.Buffered(1))


def _mod_spec(mod, tiles_per_seq):
    if mod.shape[0] == 1:
        return pl.BlockSpec((None,) + mod.shape[1:], lambda i: (0, 0, 0))
    return pl.BlockSpec((None,) + mod.shape[1:], lambda i: (i // tiles_per_seq, 0, 0))


def _params(sem):
    return pltpu.CompilerParams(dimension_semantics=sem, vmem_limit_bytes=VMEM_LIMIT)


N_MOD_MIXER = 2
N_MOD_TAIL = 4


def _adaln_block(c_ref, w_ref, b_ref, o_ref):
    c = c_ref[...]
    s = c * _sigmoid(c)
    o_ref[...] = _dot(s.astype(BF16), w_ref[...].astype(BF16)) + b_ref[...]


def _adaln(cond8, w_mod, b_mod):
    tn = D_MODEL
    return pl.pallas_call(
        _adaln_block,
        grid=(N_MOD_MIXER,),
        in_specs=[pl.BlockSpec((8, D_MODEL), lambda j: (0, 0)),
                  pl.BlockSpec((D_MODEL, tn), lambda j: (0, j)),
                  pl.BlockSpec((1, tn), lambda j: (0, j))],
        out_specs=pl.BlockSpec((8, tn), lambda j: (0, j)),
        out_shape=jax.ShapeDtypeStruct((8, N_MOD_MIXER * D_MODEL), F32),
        compiler_params=_params(("arbitrary",)),
        name="adaln",
    )(cond8, w_mod, b_mod)


def _rope_block(x, cos, sin_lo, sin_hi):
    return x * cos + pltpu.roll(x, LANES - 8, 1) * sin_lo + pltpu.roll(x, 8, 1) * sin_hi


def _store_kv(kvn, kp, kc_ref, v_ref, g=slice(None)):
    for hd in range(MLA_HEADS):
        sl = slice(hd * HEAD_BLOCK, (hd + 1) * HEAD_BLOCK)
        kc_ref[g, sl] = (kvn[:, sl] + kp).astype(BF16)
    v0 = MLA_HEADS * HEAD_BLOCK
    ones = jnp.ones((kvn.shape[0], LANES), BF16)
    for pair in range(MLA_HEADS // 2):
        v_ref[g, pair * V_SLAB:pair * V_SLAB + LANES] = kvn[:, v0 + pair * LANES:v0 + (pair + 1) * LANES].astype(BF16)
        v_ref[g, pair * V_SLAB + LANES:(pair + 1) * V_SLAB] = ones


def _inproj_kernel(*refs, rope):
    if rope:
        (x_ref, mod_ref, cos_ref, slo_ref, shi_ref, gmix_ref, win_ref, gq_ref, wuq_ref, gkv_ref, wukv_ref,
         bg_ref, q_ref, kc_ref, v_ref, mq_ref, mk_ref, mv_ref, gate_ref, smo_ref, sbr_ref) = refs
    else:
        (x_ref, mod_ref, gmix_ref, win_ref, gq_ref, wuq_ref, gkv_ref, wukv_ref,
         bg_ref, q_ref, kc_ref, v_ref, mq_ref, mk_ref, mv_ref, gate_ref, smo_ref, sbr_ref,
         ckv_ref, kpe_ref) = refs
    mod = mod_ref[...]
    for r0 in range(0, x_ref.shape[0], INPROJ_ROWS):
        g = slice(r0, r0 + INPROJ_ROWS)
        if rope:
            cos, slo, shi = cos_ref[g, :], slo_ref[g, :], shi_ref[g, :]
        h = _rms(x_ref[g, :], gmix_ref[...]) * (1.0 + mod[1:2]) + mod[0:1]
        hb = h.astype(BF16)

        def proj(seg):
            return _dot_nt(hb, win_ref[seg[0]:seg[1], :])

        qn = _rms(proj(SEG_Q), gq_ref[...]).astype(BF16)
        q = _dot(qn, wuq_ref[...])
        for hd in range(MLA_HEADS):
            sl = slice(hd * HEAD_BLOCK, (hd + 1) * HEAD_BLOCK)
            qh = q[:, sl]
            if rope:
                qh = _rope_block(qh, cos, slo, shi)
            q_ref[g, sl] = (qh * Q_PRESCALE).astype(BF16)

        ckv = _rms(proj(SEG_KV), gkv_ref[...])
        zkpe = proj(SEG_KPE)
        if not rope:
            ckv_ref[g, :] = ckv
            kpe_ref[g, :] = zkpe[:, :ROPE_DIM]
        kp = pltpu.roll(zkpe, ROPE_LANE0, 1)
        if rope:
            kp = _rope_block(kp, cos, slo, shi)
        kvn = _dot(ckv.astype(BF16), wukv_ref[...])
        _store_kv(kvn, kp, kc_ref, v_ref, g)

        mq_ref[g, :] = proj(SEG_MQ).astype(BF16)
        mk_ref[g, :] = (proj(SEG_MK) * (MLSTM_DK ** -0.5)).astype(BF16)
        mv_ref[g, :] = proj(SEG_MV).astype(BF16)
        gate_ref[g, :] = proj(SEG_GATE) + bg_ref[...]
        smo_ref[g, :] = _sigmoid(proj(SEG_MO)).astype(BF16)
        sbr_ref[g, :] = _sigmoid(proj(SEG_BR)).astype(BF16)


def _inproj(x, mod, rope_tabs, gmix, w_in_p, gq, w_uq_p, gkv, w_ukv_p, bg, *, seq_len):
    n_tok = x.shape[0]
    tm = TOKEN_TILE
    tiles_per_seq = seq_len // tm
    rope = rope_tabs is not None
    tok = lambda c: pl.BlockSpec((tm, c), lambda i: (i, 0))
    in_specs = [tok(D_MODEL), _mod_spec(mod, tiles_per_seq)]
    args = [x, mod]
    if rope:
        in_specs += [pl.BlockSpec((tm, LANES), lambda i: (i % tiles_per_seq, 0))] * 3
        args += list(rope_tabs)
    in_specs += [_const_spec(a.shape) for a in (gmix, w_in_p, gq, w_uq_p, gkv, w_ukv_p, bg)]
    args += [gmix, w_in_p, gq, w_uq_p, gkv, w_ukv_p, bg]
    widths = [(MLA_HEADS * HEAD_BLOCK, BF16), (MLA_HEADS * HEAD_BLOCK, BF16), (V_WIDTH, BF16),
              (MLSTM_HEADS * MLSTM_DK, BF16), (MLSTM_HEADS * MLSTM_DK, BF16), (MLSTM_HEADS * MLSTM_DV, BF16),
              (LANES, F32), (MLSTM_HEADS * MLSTM_DV, BF16), (2 * D_MODEL, BF16)]
    if not rope:
        widths += [(KV_RANK, F32), (ROPE_DIM, F32)]
    return pl.pallas_call(
        functools.partial(_inproj_kernel, rope=rope),
        grid=(n_tok // tm,),
        in_specs=in_specs,
        out_specs=[tok(c) for c, _ in widths],
        out_shape=[jax.ShapeDtypeStruct((n_tok, c), dt) for c, dt in widths],
        compiler_params=_params(("parallel",)),
        name="inproj_lat" if rope else "inproj_ctx",
    )(*args)


def _cachekv_kernel(ckv_ref, kp_ref, wukv_ref, kc_ref, v_ref):
    kvn = _dot(ckv_ref[...].astype(BF16), wukv_ref[...])
    _store_kv(kvn, kp_ref[...], kc_ref, v_ref)


def _cachekv(cache_ckv, krope_blk, w_ukv_p):
    b, past, _ = cache_ckv.shape
    return pl.pallas_call(
        _cachekv_kernel,
        grid=(b,),
        in_specs=[pl.BlockSpec((None, past, KV_RANK), lambda i: (i, 0, 0)),
                  pl.BlockSpec((None, past, LANES), lambda i: (i, 0, 0)),
                  _const_spec(w_ukv_p.shape)],
        out_specs=[pl.BlockSpec((None, past, MLA_HEADS * HEAD_BLOCK), lambda i: (i, 0, 0)),
                   pl.BlockSpec((None, past, V_WIDTH), lambda i: (i, 0, 0))],
        out_shape=[jax.ShapeDtypeStruct((b, past, MLA_HEADS * HEAD_BLOCK), BF16),
                   jax.ShapeDtypeStruct((b, past, V_WIDTH), BF16)],
        compiler_params=_params(("parallel",)),
        name="cachekv",
    )(cache_ckv, krope_blk, w_ukv_p)


def _attn_kernel(*refs, has_cache, n_cast, has_adaln):
    n_in = 5 if has_cache else 3
    if has_cache:
        q_ref, k_ref, v_ref, kc_ref, vc_ref = refs[:n_in]
    else:
        q_ref, k_ref, v_ref = refs[:n_in]
    n_all_in = n_in + n_cast + (3 if has_adaln else 0)
    o_ref = refs[n_all_in]
    for w_ref, wb_ref in zip(refs[n_in:n_in + n_cast], refs[n_all_in + 1:n_all_in + 1 + n_cast]):
        wb_ref[...] = w_ref[...].astype(BF16)
    if has_adaln:
        _adaln_block(*refs[n_in + n_cast:n_all_in], refs[n_all_in + 1 + n_cast])
    tq = q_ref.shape[0]
    lane = lax.broadcasted_iota(jnp.int32, (tq, LANES), 1)
    for pair in range(MLA_HEADS // 2):
        vsl = slice(pair * V_SLAB, (pair + 1) * V_SLAB)
        outs = []
        for j in range(2):
            hd = 2 * pair + j
            sl = slice(hd * HEAD_BLOCK, (hd + 1) * HEAD_BLOCK)
            qh = q_ref[:, sl]
            s = _dot_nt(qh, k_ref[:, sl])
            m = jnp.max(s, axis=-1, keepdims=True)
            if has_cache:
                sc = _dot_nt(qh, kc_ref[:, sl])
                m = jnp.maximum(m, jnp.max(sc, axis=-1, keepdims=True))
            o = _dot(jnp.exp2(s - m).astype(BF16), v_ref[:, vsl])
            if has_cache:
                o = o + _dot(jnp.exp2(sc - m).astype(BF16), vc_ref[:, vsl])
            outs.append(o[:, :LANES] * (1.0 / o[:, LANES:]))
        o_ref[:, pair * LANES:(pair + 1) * LANES] = jnp.where(lane < MLA_V_DIM, outs[0], outs[1]).astype(BF16)


def _attention(q, k, v, cache=None, cast_weights=(), adaln_tail=None):
    b, s, _ = q.shape
    tq = Q_TILE
    n_steps = b * (s // tq)
    kw, vw, ow = MLA_HEADS * HEAD_BLOCK, V_WIDTH, MLA_HEADS * MLA_V_DIM
    in_specs = [pl.BlockSpec((None, tq, kw), lambda i, j: (i, j, 0)),
                pl.BlockSpec((None, s, kw), lambda i, j: (i, 0, 0)),
                pl.BlockSpec((None, s, vw), lambda i, j: (i, 0, 0))]
    args = [q, k, v]
    if cache is not None:
        past = cache[0].shape[1]
        in_specs += [pl.BlockSpec((None, past, kw), lambda i, j: (i, 0, 0)),
                     pl.BlockSpec((None, past, vw), lambda i, j: (i, 0, 0))]
        args += list(cache)
    step = lambda i, j: i * (s // tq) + j
    w_specs = [pl.BlockSpec((w.shape[0] // n_steps, w.shape[1]), lambda i, j: (step(i, j), 0)) for w in cast_weights]
    extra_in, extra_out, extra_shape = [], [], []
    if adaln_tail is not None:
        tn = N_MOD_TAIL * D_MODEL // n_steps
        col0 = N_MOD_MIXER * D_MODEL // tn
        extra_in = [pl.BlockSpec((8, D_MODEL), lambda i, j: (0, 0)),
                    pl.BlockSpec((D_MODEL, tn), lambda i, j: (0, col0 + step(i, j))),
                    pl.BlockSpec((1, tn), lambda i, j: (0, col0 + step(i, j)))]
        extra_out = [pl.BlockSpec((8, tn), lambda i, j: (0, step(i, j)))]
        extra_shape = [jax.ShapeDtypeStruct((8, N_MOD_TAIL * D_MODEL), F32)]
    outs = pl.pallas_call(
        functools.partial(_attn_kernel, has_cache=cache is not None, n_cast=len(cast_weights),
                          has_adaln=adaln_tail is not None),
        grid=(b, s // tq),
        in_specs=in_specs + w_specs + extra_in,
        out_specs=[pl.BlockSpec((None, tq, ow), lambda i, j: (i, j, 0))] + w_specs + extra_out,
        out_shape=[jax.ShapeDtypeStruct((b, s, ow), BF16)]
                  + [jax.ShapeDtypeStruct(w.shape, BF16) for w in cast_weights] + extra_shape,
        compiler_params=_params(("parallel", "arbitrary")),
        name="attn_lat" if cache is not None else "attn_ctx",
    )(*args, *cast_weights, *(adaln_tail or ()))
    return outs[0] if len(outs) == 1 else outs


def _lane_bcast(x, k, width=None):
    y = jnp.broadcast_to(x[:, k:k + 1], x.shape)
    reps = (width or LANES) // LANES
    return y if reps == 1 else jnp.concatenate([y] * reps, axis=1)


def _prefix_sum_rows(x):
    n = x.shape[0]
    r = lax.broadcasted_iota(jnp.int32, (n, n), 0)
    c = lax.broadcasted_iota(jnp.int32, (n, n), 1)
    tri = jnp.where(c <= r, 1.0, 0.0).astype(BF16)
    hi = x.astype(BF16)
    r1 = x - hi.astype(F32)
    mid = r1.astype(BF16)
    lo = (r1 - mid.astype(F32)).astype(BF16)
    return _dot(tri, hi) + _dot(tri, mid) + _dot(tri, lo)


def _gate_prep(g):
    n = g.shape[0]
    lane = lax.broadcasted_iota(jnp.int32, (n, LANES), 1)
    fpre = pltpu.roll(g, LANES - N_DIRHEAD, 1)
    lf = jnp.minimum(fpre, 0.0) - jnp.log(1.0 + jnp.exp(-jnp.abs(fpre)))
    binc = _prefix_sum_rows(lf)
    tot = binc[n - 1:n, :]
    b = jnp.where(lane < MLSTM_HEADS, binc, tot - binc + lf)
    return b, g - b, tot


def _chunk_setup(b, u, tot, m_prev):
    g_last = jnp.maximum(jnp.max(u, axis=0, keepdims=True), m_prev)
    u2 = u * LOG2E
    w = jnp.exp2(u2 - g_last * LOG2E)
    return u2, (b * LOG2E).T, w, tot + g_last, jnp.exp(m_prev - g_last)


def _mlstm_dir_t(s0t, vt, u_b, bt_row, m_row, *, fwd, inter=None):
    n = s0t.shape[0]
    r = lax.broadcasted_iota(jnp.int32, (n, n), 0)
    c = lax.broadcasted_iota(jnp.int32, (n, n), 1)
    a = jnp.where((r <= c) if fwd else (r >= c), u_b, -jnp.inf)
    g = jnp.maximum(jnp.max(a, axis=0, keepdims=True), m_row)
    sdt = s0t * jnp.exp2(a - g)
    den = jnp.sum(sdt, axis=0, keepdims=True)
    ht = _dot(vt, sdt.astype(BF16))
    if inter is not None:
        w_inter = jnp.exp2(m_row - g)
        ht = ht + w_inter * inter[0]
        den = den + w_inter * inter[1]
    floor = jnp.exp2(-(bt_row + g))
    return ht * (1.0 / jnp.maximum(jnp.abs(den), floor))


def _mlstm_ctx_kernel(q_ref, k_ref, v_ref, g_ref, h_ref, c_ref, n_ref, m_ref):
    n = q_ref.shape[1]
    zero_lanes = jnp.zeros((1, LANES), F32)
    zero_row = jnp.zeros((1, n), F32)
    for sq in range(q_ref.shape[0]):
        u2, bt2, w, m_new, _ = _chunk_setup(*_gate_prep(g_ref[sq]), zero_lanes)
        for hd in range(MLSTM_HEADS):
            q = q_ref[sq, :, hd * MLSTM_DK:(hd + 1) * MLSTM_DK]
            k = k_ref[sq, :, hd * MLSTM_DK:(hd + 1) * MLSTM_DK]
            vt = v_ref[sq, :, hd * MLSTM_DV:(hd + 1) * MLSTM_DV].T
            s0t = _dot_nt(k, q)
            k32 = k.astype(F32)
            ht, wks = None, []
            for fwd in (True, False):
                kk = hd if fwd else MLSTM_HEADS + hd
                ht_d = _mlstm_dir_t(s0t, vt, _lane_bcast(u2, kk, n), bt2[kk:kk + 1, :], zero_row, fwd=fwd)
                ht = ht_d if ht is None else ht + ht_d
                wk = _lane_bcast(w, kk) * k32
                n_ref[sq, kk:kk + 1, :] = jnp.sum(wk, axis=0, keepdims=True)
                wks.append(wk.astype(BF16))
            c_both = _dot(vt, jnp.concatenate(wks, axis=1))
            c_ref[sq, hd] = c_both[:, :MLSTM_DK]
            c_ref[sq, MLSTM_HEADS + hd] = c_both[:, MLSTM_DK:]
            h_ref[sq, :, hd * MLSTM_DV:(hd + 1) * MLSTM_DV] = ht.T
        m_ref[sq] = m_new


CTX_SEQS_PER_STEP = 2


def _mlstm_ctx(mq, mk, mv, gates):
    b, s, _ = mq.shape
    ns = CTX_SEQS_PER_STEP
    seq = lambda c: pl.BlockSpec((ns, s, c), lambda i: (i, 0, 0))
    return pl.pallas_call(
        _mlstm_ctx_kernel,
        grid=(b // ns,),
        in_specs=[seq(MLSTM_HEADS * MLSTM_DK), seq(MLSTM_HEADS * MLSTM_DK), seq(MLSTM_HEADS * MLSTM_DV), seq(LANES)],
        out_specs=[seq(MLSTM_HEADS * MLSTM_DV),
                   pl.BlockSpec((ns, N_DIRHEAD, MLSTM_DV, MLSTM_DK), lambda i: (i, 0, 0, 0)),
                   pl.BlockSpec((ns, N_DIRHEAD, MLSTM_DK), lambda i: (i, 0, 0)),
                   pl.BlockSpec((ns, 1, LANES), lambda i: (i, 0, 0))],
        out_shape=[jax.ShapeDtypeStruct((b, s, MLSTM_HEADS * MLSTM_DV), F32),
                   jax.ShapeDtypeStruct((b, N_DIRHEAD, MLSTM_DV, MLSTM_DK), F32),
                   jax.ShapeDtypeStruct((b, N_DIRHEAD, MLSTM_DK), F32),
                   jax.ShapeDtypeStruct((b, 1, LANES), F32)],
        compiler_params=_params(("parallel",)),
        name="mlstm_ctx",
    )(mq, mk, mv, gates)


def _mlstm_lat_kernel(qf_ref, kf_ref, vf_ref, gf_ref, qb_ref, kb_ref, vb_ref, gb_ref, c0_ref, n0_ref, m0_ref,
                      h_ref, c_ref, n_ref, m_ref, *, n_chunks):
    step = pl.program_id(1)

    @pl.when(step == 0)
    def _():
        c_ref[...] = c0_ref[...]
        n_ref[...] = n0_ref[...]
        m_ref[...] = m0_ref[...]
        h_ref[...] = jnp.zeros_like(h_ref)

    n = qf_ref.shape[0]
    gate_f, gate_b = _gate_prep(gf_ref[...]), _gate_prep(gb_ref[...])
    is_f = lax.broadcasted_iota(jnp.int32, (n, LANES), 1) < MLSTM_HEADS
    b, u, tot = (jnp.where(is_f[:x.shape[0]], x, y) for x, y in zip(gate_f, gate_b))
    m_prev = m_ref[...]
    u2, bt2, w, m_new, decay = _chunk_setup(b, u, tot, m_prev)
    m2_prev = m_prev * LOG2E
    for fwd in (True, False):
        q_ref, k_ref, v_ref = (qf_ref, kf_ref, vf_ref) if fwd else (qb_ref, kb_ref, vb_ref)
        chunk = step if fwd else n_chunks - 1 - step
        row0 = pl.multiple_of(chunk * MLSTM_CHUNK, MLSTM_CHUNK)
        for hd in range(MLSTM_HEADS):
            kk = hd if fwd else MLSTM_HEADS + hd
            q = q_ref[:, hd * MLSTM_DK:(hd + 1) * MLSTM_DK]
            k = k_ref[:, hd * MLSTM_DK:(hd + 1) * MLSTM_DK]
            vt = v_ref[:, hd * MLSTM_DV:(hd + 1) * MLSTM_DV].T
            c_prev, n_prev = c_ref[kk], n_ref[kk:kk + 1, :]
            n_rows = jnp.broadcast_to(n_prev, (8, MLSTM_DK)).astype(BF16)
            inter = (_dot_nt(c_prev.astype(BF16), q), _dot_nt(n_rows, q)[0:1, :])
            ht = _mlstm_dir_t(_dot_nt(k, q), vt, _lane_bcast(u2, kk, n), bt2[kk:kk + 1, :],
                              _lane_bcast(m2_prev, kk, n), fwd=fwd, inter=inter)
            wk = _lane_bcast(w, kk) * k.astype(F32)
            dk = _lane_bcast(decay, kk)
            c_ref[kk] = dk * c_prev + _dot(vt, wk.astype(BF16))
            n_ref[kk:kk + 1, :] = dk * n_prev + jnp.sum(wk, axis=0, keepdims=True)
            h_ref[pl.ds(row0, MLSTM_CHUNK), hd * MLSTM_DV:(hd + 1) * MLSTM_DV] += ht.T
    m_ref[...] = m_new


def _mlstm_lat(mq, mk, mv, gates, c0, n0, m0):
    b, s, _ = mq.shape
    lc = MLSTM_CHUNK
    nc = s // lc
    fw = lambda c: pl.BlockSpec((None, lc, c), lambda i, j: (i, j, 0))
    bw = lambda c: pl.BlockSpec((None, lc, c), lambda i, j: (i, nc - 1 - j, 0))
    dk, dv = MLSTM_HEADS * MLSTM_DK, MLSTM_HEADS * MLSTM_DV
    return pl.pallas_call(
        functools.partial(_mlstm_lat_kernel, n_chunks=nc),
        grid=(b, nc),
        in_specs=[fw(dk), fw(dk), fw(dv), fw(LANES), bw(dk), bw(dk), bw(dv), bw(LANES),
                  pl.BlockSpec((None, N_DIRHEAD, MLSTM_DV, MLSTM_DK), lambda i, j: (i, 0, 0, 0)),
                  pl.BlockSpec((None, N_DIRHEAD, MLSTM_DK), lambda i, j: (i, 0, 0)),
                  pl.BlockSpec((None, 1, LANES), lambda i, j: (i, 0, 0))],
        out_specs=pl.BlockSpec((None, s, dv), lambda i, j: (i, 0, 0)),
        out_shape=jax.ShapeDtypeStruct((b, s, dv), F32),
        scratch_shapes=[pltpu.VMEM((N_DIRHEAD, MLSTM_DV, MLSTM_DK), F32),
                        pltpu.VMEM((N_DIRHEAD, MLSTM_DK), F32),
                        pltpu.VMEM((1, LANES), F32)],
        compiler_params=_params(("parallel", "arbitrary")),
        name="mlstm_lat",
    )(mq, mk, mv, gates, mq, mk, mv, gates, c0, n0, m0)


def _tail_kernel(x_ref, mod_ref, att_ref, hs_ref, smo_ref, sbr_ref, gml_ref, womla_ref, womlstm_ref, wout_ref,
                 gffn_ref, wfin_ref, wfout_ref, gfin_ref, y_ref):
    mod = mod_ref[...]
    gml = gml_ref[...]
    tm = x_ref.shape[0]
    groups = [slice(r, r + TAIL_ROWS) for r in range(0, tm, TAIL_ROWS)]
    a = [_dot(att_ref[g, :], womla_ref[...]) for g in groups]
    hm = []
    for g in groups:
        parts = []
        for hd in range(MLSTM_HEADS):
            sl = slice(hd * MLSTM_DV, (hd + 1) * MLSTM_DV)
            parts.append((_rms(hs_ref[g, sl], gml[:, sl]) * smo_ref[g, sl].astype(F32)).astype(BF16))
        hm.append(jnp.concatenate(parts, axis=1))
    bm = [_dot(h, womlstm_ref[...]) for h in hm]
    merged = [(sbr_ref[g, :D_MODEL].astype(F32) * ai + sbr_ref[g, D_MODEL:].astype(F32) * bi).astype(BF16)
              for g, ai, bi in zip(groups, a, bm)]
    x1 = [x_ref[g, :] + mod[0:1] * _dot(mi, wout_ref[...]) for g, mi in zip(groups, merged)]
    h2 = [(_rms(xi, gffn_ref[...]) * (1.0 + mod[2:3]) + mod[1:2]).astype(BF16) for xi in x1]
    ga = [_dot(hi, wfin_ref[:, :FFN_HIDDEN]) for hi in h2]
    gu = [_dot(hi, wfin_ref[:, FFN_HIDDEN:]) for hi in h2]
    act = [(gi * _sigmoid(gi) * ui).astype(BF16) for gi, ui in zip(ga, gu)]
    for g, xi, ci in zip(groups, x1, act):
        y_ref[g, :] = _rms(xi + mod[3:4] * _dot(ci, wfout_ref[...]), gfin_ref[...])


def _tail(x, mod, att, hs, smo, sbr, gml, w_o_mla, w_o_mlstm, w_out, gffn, w_ffn_in, w_ffn_out, gfin, *, seq_len, name):
    n_tok = x.shape[0]
    tm = TOKEN_TILE
    tiles_per_seq = seq_len // tm
    tok = lambda c: pl.BlockSpec((tm, c), lambda i: (i, 0))
    consts = (gml, w_o_mla, w_o_mlstm, w_out, gffn, w_ffn_in, w_ffn_out, gfin)
    return pl.pallas_call(
        _tail_kernel,
        grid=(n_tok // tm,),
        in_specs=[tok(D_MODEL), _mod_spec(mod, tiles_per_seq),
                  tok(att.shape[1]), tok(hs.shape[1]), tok(smo.shape[1]), tok(sbr.shape[1])]
                 + [_const_spec(a.shape) for a in consts],
        out_specs=tok(D_MODEL),
        out_shape=jax.ShapeDtypeStruct((n_tok, D_MODEL), F32),
        compiler_params=_params(("parallel",)),
        name=name,
    )(x, mod, att, hs, smo, sbr, *consts)


_IN_SIZES = (Q_RANK, KV_RANK, ROPE_DIM, MLSTM_HEADS * MLSTM_DK, MLSTM_HEADS * MLSTM_DK, MLSTM_HEADS * MLSTM_DV,
             4 * MLSTM_HEADS, MLSTM_HEADS * MLSTM_DV, 2 * D_MODEL)
_IN_OFF = tuple(sum(_IN_SIZES[:i]) for i in range(len(_IN_SIZES) + 1))
PACK_ROWS = LANES
_PAD_AFTER = tuple((_SEG_OFF[i + 1] // PACK_ROWS, _SEG_WIDTHS[i] - _IN_SIZES[i], _IN_SIZES[i] % PACK_ROWS)
                   for i in range(len(_IN_SIZES)) if _SEG_WIDTHS[i] != _IN_SIZES[i])


PACK_PIECES = 8
N_PACK_STEPS = pl.cdiv(N_IN_PACKED // PACK_ROWS, PACK_PIECES)
_N_IN = _IN_OFF[-1]


def _pack_src_row(p):
    src = p * PACK_ROWS
    for first_piece_after, pad, _ in _PAD_AFTER:
        src = src - jnp.where(p >= first_piece_after, pad, 0)
    src = jnp.minimum(src, _N_IN - PACK_ROWS)
    return pl.multiple_of(src, math.gcd(PACK_ROWS, _N_IN, *(pad for _, pad, _ in _PAD_AFTER)))


def _pack_in_kernel(*refs):
    w_refs, o_ref = refs[:PACK_PIECES], refs[PACK_PIECES]
    row = lax.broadcasted_iota(jnp.int32, w_refs[0].shape, 0)
    for j, w_ref in enumerate(w_refs):
        p = pl.program_id(0) * PACK_PIECES + j
        valid = jnp.where(p < N_IN_PACKED // PACK_ROWS, PACK_ROWS, 0)
        for first_piece_after, _, width in _PAD_AFTER:
            valid = jnp.where(p == first_piece_after - 1, width, valid)
        o_ref[j * PACK_ROWS:(j + 1) * PACK_ROWS, :] = jnp.where(row < valid, w_ref[...], 0.0).astype(BF16)


def _pack_in(w_in_t):
    n, k = w_in_t.shape
    piece = lambda j: pl.BlockSpec((pl.Element(PACK_ROWS), pl.Element(k)),
                                   lambda i: (_pack_src_row(i * PACK_PIECES + j), 0))
    return pl.pallas_call(
        _pack_in_kernel,
        grid=(N_PACK_STEPS,),
        in_specs=[piece(j) for j in range(PACK_PIECES)],
        out_specs=pl.BlockSpec((PACK_PIECES * PACK_ROWS, k), lambda i: (i, 0)),
        out_shape=jax.ShapeDtypeStruct((N_PACK_STEPS * PACK_PIECES * PACK_ROWS, k), BF16),
        compiler_params=_params(("parallel",)),
        name="pack_in",
    )(*([w_in_t] * PACK_PIECES))


def _pack_small(w_uq, w_ukv, b_gates):
    qd = NOPE_DIM + ROPE_DIM
    w_uq_p = jnp.pad(w_uq.reshape(Q_RANK, MLA_HEADS, qd), ((0, 0), (0, 0), (0, HEAD_BLOCK - qd)))
    w_uq_p = w_uq_p.reshape(Q_RANK, MLA_HEADS * HEAD_BLOCK).astype(BF16)
    kv = w_ukv.reshape(KV_RANK, MLA_HEADS, NOPE_DIM + MLA_V_DIM)
    wk = jnp.pad(kv[:, :, :NOPE_DIM], ((0, 0), (0, 0), (0, HEAD_BLOCK - NOPE_DIM))).reshape(KV_RANK, -1)
    wv = kv[:, :, NOPE_DIM:].reshape(KV_RANK, -1)
    w_ukv_p = jnp.concatenate([wk, wv], axis=1).astype(BF16)
    bg = jnp.pad(b_gates, ((0, 0), (0, LANES - b_gates.shape[1])))
    return w_uq_p, w_ukv_p, bg


def _rope_tables(n_tokens):
    pos = np.arange(n_tokens)
    row = (pos // GRID_W).astype(np.float64)[:, None]
    col = (pos % GRID_W).astype(np.float64)[:, None]
    half = ROPE_DIM // 2
    inv = (np.float32(ROPE_BASE) ** (-np.arange(0, half, 2, dtype=np.float32) / np.float32(half))).astype(np.float64)
    r = np.arange(LANES) - ROPE_LANE0
    in_rope = (r >= 0) & (r < ROPE_DIM)
    rr = np.clip(r, 0, ROPE_DIM - 1)
    freq = inv[rr % (half // 2)][None, :]
    ang = np.where((rr // half == 0)[None, :], row * freq, col * freq).astype(np.float32).astype(np.float64)
    first = (rr % half) < (half // 2)
    cos = np.where(in_rope[None, :], np.cos(ang), 1.0)
    sin = np.sin(ang)
    sin_lo = np.where((in_rope & first)[None, :], -sin, 0.0)
    sin_hi = np.where((in_rope & ~first)[None, :], sin, 0.0)
    return tuple(jnp.asarray(t, dtype=F32) for t in (cos, sin_lo, sin_hi))


def kernel(x_prompt, x_sample, cache_ckv, cache_krope, state_C, state_n, state_m, c, c_ctx, w_mod, b_mod, g_norm_mix,
           w_in, b_gates, g_q_norm, w_uq, g_kv_norm, w_ukv, g_mlstm_norm, w_o_mla, w_o_mlstm, w_out, g_norm_ffn,
           w_ffn_in, w_ffn_out, g_final):
    bp, sp, d = x_prompt.shape
    bs, ss, _ = x_sample.shape
    layer = 0
    assert w_mod.shape[0] == 1 and sp == MLSTM_CHUNK and ss % MLSTM_CHUNK == 0

    cond = jnp.concatenate([c_ctx[None, :], c, jnp.zeros((8 - 1 - bs, d), F32)], axis=0)
    adaln_args = (cond, w_mod[layer], b_mod[layer][None, :])
    mod_in = _adaln(*adaln_args).reshape(8, N_MOD_MIXER, d)
    mod_ctx, mod_lat = mod_in[0:1], mod_in[1:1 + bs]

    w_in_p = _pack_in(w_in[layer].T)
    w_uq_p, w_ukv_p, bg = _pack_small(w_uq[layer], w_ukv[layer], b_gates[layer][None, :])
    row = lambda g: g[layer][None, :]
    shared_in = (row(g_norm_mix), w_in_p, row(g_q_norm), w_uq_p, row(g_kv_norm), w_ukv_p, bg)
    seq3 = lambda a, b_, s_: a.reshape(b_, s_, a.shape[-1])

    xl = x_sample.reshape(bs * ss, d)
    q, kc, v, mq_l, mk_l, mv_l, gates_l, smo_l, sbr_l = _inproj(xl, mod_lat, _rope_tables(ss), *shared_in, seq_len=ss)
    krope_blk = jnp.pad(cache_krope[:, layer], ((0, 0), (0, 0), (ROPE_LANE0, LANES - ROPE_LANE0 - ROPE_DIM)))
    cache = _cachekv(cache_ckv[:, layer], krope_blk, w_ukv_p)
    att_l, womla_b, womlstm_b, wout_b, wfin_b, wfout_b, mod_tail = _attention(
        seq3(q, bs, ss), seq3(kc, bs, ss), seq3(v, bs, ss), cache,
        cast_weights=(w_o_mla[layer], w_o_mlstm[layer], w_out[layer], w_ffn_in[layer], w_ffn_out[layer]),
        adaln_tail=adaln_args)
    tail_w = (row(g_mlstm_norm), womla_b, womlstm_b, wout_b, row(g_norm_ffn), wfin_b, wfout_b, g_final[None, :])
    mod_tail = mod_tail.reshape(8, N_MOD_TAIL, d)
    modt_ctx, modt_lat = mod_tail[0:1], mod_tail[1:1 + bs]

    xc = x_prompt.reshape(bp * sp, d)
    q, kc, v, mq, mk, mv, gates, smo, sbr, ckv, kpe = _inproj(xc, mod_ctx, None, *shared_in, seq_len=sp)
    att = _attention(seq3(q, bp, sp), seq3(kc, bp, sp), seq3(v, bp, sp))
    hs, new_c, new_n, new_m = _mlstm_ctx(seq3(mq, bp, sp), seq3(mk, bp, sp), seq3(mv, bp, sp), seq3(gates, bp, sp))
    y_prompt = _tail(xc, modt_ctx, att.reshape(bp * sp, -1), hs.reshape(bp * sp, -1), smo, sbr, *tail_w,
                     seq_len=sp, name="tail_ctx").reshape(bp, sp, d)
    new_ckv = ckv.reshape(bp, 1, sp, KV_RANK)
    new_krope = kpe.reshape(bp, 1, sp, ROPE_DIM)
    new_C = new_c.reshape(bp, 1, 2, MLSTM_HEADS, MLSTM_DV, MLSTM_DK)
    new_N = new_n.reshape(bp, 1, 2, MLSTM_HEADS, MLSTM_DK)
    new_M = new_m[:, 0, :N_DIRHEAD].reshape(bp, 1, 2, MLSTM_HEADS)

    c0 = state_C[:, layer].reshape(bs, N_DIRHEAD, MLSTM_DV, MLSTM_DK)
    n0 = state_n[:, layer].reshape(bs, N_DIRHEAD, MLSTM_DK)
    m0 = jnp.pad(state_m[:, layer].reshape(bs, 1, N_DIRHEAD), ((0, 0), (0, 0), (0, LANES - N_DIRHEAD)))
    hs = _mlstm_lat(seq3(mq_l, bs, ss), seq3(mk_l, bs, ss), seq3(mv_l, bs, ss), seq3(gates_l, bs, ss), c0, n0, m0)
    y_sample = _tail(xl, modt_lat, att_l.reshape(bs * ss, -1), hs.reshape(bs * ss, -1), smo_l, sbr_l, *tail_w,
                     seq_len=ss, name="tail_lat").reshape(bs, ss, d)
    return (y_prompt, y_sample, new_ckv, new_krope, new_C, new_N, new_M)
```

```python
import functools
import math

import jax
import jax.numpy as jnp
import numpy as np
from jax import lax
from jax.experimental import pallas as pl
from jax.experimental.pallas import tpu as pltpu

F32 = jnp.float32
BF16 = jnp.bfloat16

D_MODEL = 1024
GRID_W = 64
MLA_HEADS = 8
Q_RANK = 384
KV_RANK = 256
NOPE_DIM = 64
ROPE_DIM = 32
MLA_V_DIM = 64
ROPE_BASE = 10000.0
MLA_SCALE = (NOPE_DIM + ROPE_DIM) ** -0.5
MLSTM_HEADS = 4
MLSTM_DK = 128
MLSTM_DV = 256
FFN_HIDDEN = ((8 * D_MODEL // 3 + 255) // 256) * 256
EPS = 1e-6

LANES = 128
HEAD_BLOCK = LANES
ROPE_LANE0 = NOPE_DIM
N_DIRHEAD = 2 * MLSTM_HEADS
MLSTM_CHUNK = 256
LOG2E = math.log2(math.e)
Q_PRESCALE = MLA_SCALE * LOG2E
V_SLAB = 2 * LANES
V_WIDTH = (MLA_HEADS // 2) * V_SLAB
TOKEN_TILE = 512
TAIL_ROWS = 256
INPROJ_ROWS = TOKEN_TILE
Q_TILE = 256
VMEM_LIMIT = 56 * 1024 * 1024

_SEG_WIDTHS = (Q_RANK, KV_RANK, LANES, MLSTM_HEADS * MLSTM_DK, MLSTM_HEADS * MLSTM_DK,
               MLSTM_HEADS * MLSTM_DV, LANES, MLSTM_HEADS * MLSTM_DV, 2 * D_MODEL)
_SEG_OFF = tuple(sum(_SEG_WIDTHS[:i]) for i in range(len(_SEG_WIDTHS) + 1))
SEG_Q, SEG_KV, SEG_KPE, SEG_MQ, SEG_MK, SEG_MV, SEG_GATE, SEG_MO, SEG_BR = (
    (_SEG_OFF[i], _SEG_OFF[i + 1]) for i in range(9))
N_IN_PACKED = _SEG_OFF[-1]


def _dot(a, b):
    return jnp.dot(a, b, preferred_element_type=F32)


def _dot_nt(a, b):
    return lax.dot_general(a, b, (((1,), (1,)), ((), ())), preferred_element_type=F32)


def _dot_tn(a, b):
    return lax.dot_general(a, b, (((0,), (0,)), ((), ())), preferred_element_type=F32)


def _rms(x, g):
    ms = jnp.mean(x * x, axis=-1, keepdims=True)
    return x * lax.rsqrt(ms + EPS) * g


def _sigmoid(x):
    return 1.0 / (1.0 + jnp.exp(-x))


def _const_spec(shape):
    nd = len(shape)
    return pl.BlockSpec(shape, lambda *_: (0,) * nd, pipeline_mode=pl.Buffered(1))


def _mod_spec(mod, tiles_per_seq):
    if mod.shape[0] == 1:
        return pl.BlockSpec((None,) + mod.shape[1:], lambda i: (0, 0, 0))
    return pl.BlockSpec((None,) + mod.shape[1:], lambda i: (i // tiles_per_seq, 0, 0))


def _params(sem, flags=None):
    return pltpu.CompilerParams(dimension_semantics=sem, vmem_limit_bytes=VMEM_LIMIT, flags=flags)


N_MOD_MIXER = 2
N_MOD_TAIL = 4


def _adaln_block(c_ref, w_ref, b_ref, o_ref):
    c = c_ref[...]
    s = c * _sigmoid(c)
    o_ref[...] = _dot(s.astype(BF16), w_ref[...].astype(BF16)) + b_ref[...]


def _adaln(cond8, w_mod, b_mod):
    tn = D_MODEL
    return pl.pallas_call(
        _adaln_block,
        grid=(N_MOD_MIXER,),
        in_specs=[pl.BlockSpec((8, D_MODEL), lambda j: (0, 0)),
                  pl.BlockSpec((D_MODEL, tn), lambda j: (0, j)),
                  pl.BlockSpec((1, tn), lambda j: (0, j))],
        out_specs=pl.BlockSpec((8, tn), lambda j: (0, j)),
        out_shape=jax.ShapeDtypeStruct((8, N_MOD_MIXER * D_MODEL), F32),
        compiler_params=_params(("arbitrary",)),
        name="adaln",
    )(cond8, w_mod, b_mod)


def _rope_block(x, cos, sin_lo, sin_hi):
    return x * cos + pltpu.roll(x, LANES - 8, 1) * sin_lo + pltpu.roll(x, 8, 1) * sin_hi


def _store_kv(kvn, kp, kc_ref, v_ref, g=slice(None)):
    for hd in range(MLA_HEADS):
        sl = slice(hd * HEAD_BLOCK, (hd + 1) * HEAD_BLOCK)
        kc_ref[g, sl] = (kvn[:, sl] + kp).astype(BF16)
    v0 = MLA_HEADS * HEAD_BLOCK
    ones = jnp.ones((kvn.shape[0], LANES), BF16)
    for pair in range(MLA_HEADS // 2):
        v_ref[g, pair * V_SLAB:pair * V_SLAB + LANES] = kvn[:, v0 + pair * LANES:v0 + (pair + 1) * LANES].astype(BF16)
        v_ref[g, pair * V_SLAB + LANES:(pair + 1) * V_SLAB] = ones


def _inproj_kernel(*refs, rope):
    if rope:
        (x_ref, mod_ref, cos_ref, slo_ref, shi_ref, gmix_ref, win_ref, gq_ref, wuq_ref, gkv_ref, wukv_ref,
         bg_ref, q_ref, kc_ref, v_ref, mq_ref, mk_ref, mv_ref, gate_ref, smo_ref, sbr_ref) = refs
    else:
        (x_ref, mod_ref, gmix_ref, win_ref, gq_ref, wuq_ref, gkv_ref, wukv_ref,
         bg_ref, q_ref, kc_ref, v_ref, mq_ref, mk_ref, mv_ref, gate_ref, smo_ref, sbr_ref,
         ckv_ref, kpe_ref) = refs
    mod = mod_ref[...]
    for r0 in range(0, x_ref.shape[0], INPROJ_ROWS):
        g = slice(r0, r0 + INPROJ_ROWS)
        if rope:
            cos, slo, shi = cos_ref[g, :], slo_ref[g, :], shi_ref[g, :]
        h = _rms(x_ref[g, :], gmix_ref[...]) * (1.0 + mod[1:2]) + mod[0:1]
        hb = h.astype(BF16)

        def proj(seg):
            return _dot_nt(hb, win_ref[seg[0]:seg[1], :])

        qn = _rms(proj(SEG_Q), gq_ref[...]).astype(BF16)
        q = _dot(qn, wuq_ref[...])
        for hd in range(MLA_HEADS):
            sl = slice(hd * HEAD_BLOCK, (hd + 1) * HEAD_BLOCK)
            qh = q[:, sl]
            if rope:
                qh = _rope_block(qh, cos, slo, shi)
            q_ref[g, sl] = (qh * Q_PRESCALE).astype(BF16)

        ckv = _rms(proj(SEG_KV), gkv_ref[...])
        zkpe = proj(SEG_KPE)
        if not rope:
            ckv_ref[g, :] = ckv
            kpe_ref[g, :] = zkpe[:, :ROPE_DIM]
        kp = pltpu.roll(zkpe, ROPE_LANE0, 1)
        if rope:
            kp = _rope_block(kp, cos, slo, shi)
        kvn = _dot(ckv.astype(BF16), wukv_ref[...])
        _store_kv(kvn, kp, kc_ref, v_ref, g)

        mq_ref[g, :] = proj(SEG_MQ).astype(BF16)
        mk_ref[g, :] = (proj(SEG_MK) * (MLSTM_DK ** -0.5)).astype(BF16)
        mv_ref[g, :] = proj(SEG_MV).astype(BF16)
        gate_ref[g, :] = proj(SEG_GATE) + bg_ref[...]
        smo_ref[g, :] = _sigmoid(proj(SEG_MO)).astype(BF16)
        sbr_ref[g, :] = _sigmoid(proj(SEG_BR)).astype(BF16)


def _inproj(x, mod, rope_tabs, gmix, w_in_p, gq, w_uq_p, gkv, w_ukv_p, bg, *, seq_len):
    n_tok = x.shape[0]
    tm = TOKEN_TILE
    tiles_per_seq = seq_len // tm
    rope = rope_tabs is not None
    tok = lambda c: pl.BlockSpec((tm, c), lambda i: (i, 0))
    in_specs = [tok(D_MODEL), _mod_spec(mod, tiles_per_seq)]
    args = [x, mod]
    if rope:
        in_specs += [pl.BlockSpec((tm, LANES), lambda i: (i % tiles_per_seq, 0))] * 3
        args += list(rope_tabs)
    in_specs += [_const_spec(a.shape) for a in (gmix, w_in_p, gq, w_uq_p, gkv, w_ukv_p, bg)]
    args += [gmix, w_in_p, gq, w_uq_p, gkv, w_ukv_p, bg]
    widths = [(MLA_HEADS * HEAD_BLOCK, BF16), (MLA_HEADS * HEAD_BLOCK, BF16), (V_WIDTH, BF16),
              (MLSTM_HEADS * MLSTM_DK, BF16), (MLSTM_HEADS * MLSTM_DK, BF16), (MLSTM_HEADS * MLSTM_DV, BF16),
              (LANES, F32), (MLSTM_HEADS * MLSTM_DV, BF16), (2 * D_MODEL, BF16)]
    if not rope:
        widths += [(KV_RANK, F32), (ROPE_DIM, F32)]
    return pl.pallas_call(
        functools.partial(_inproj_kernel, rope=rope),
        grid=(n_tok // tm,),
        in_specs=in_specs,
        out_specs=[tok(c) for c, _ in widths],
        out_shape=[jax.ShapeDtypeStruct((n_tok, c), dt) for c, dt in widths],
        compiler_params=_params(("parallel",)),
        name="inproj_lat" if rope else "inproj_ctx",
    )(*args)


def _cachekv_kernel(ckv_ref, kp_ref, wukv_ref, kc_ref, v_ref):
    kvn = _dot(ckv_ref[...].astype(BF16), wukv_ref[...])
    _store_kv(kvn, kp_ref[...], kc_ref, v_ref)


def _cachekv(cache_ckv, krope_blk, w_ukv_p):
    b, past, _ = cache_ckv.shape
    return pl.pallas_call(
        _cachekv_kernel,
        grid=(b,),
        in_specs=[pl.BlockSpec((None, past, KV_RANK), lambda i: (i, 0, 0)),
                  pl.BlockSpec((None, past, LANES), lambda i: (i, 0, 0)),
                  _const_spec(w_ukv_p.shape)],
        out_specs=[pl.BlockSpec((None, past, MLA_HEADS * HEAD_BLOCK), lambda i: (i, 0, 0)),
                   pl.BlockSpec((None, past, V_WIDTH), lambda i: (i, 0, 0))],
        out_shape=[jax.ShapeDtypeStruct((b, past, MLA_HEADS * HEAD_BLOCK), BF16),
                   jax.ShapeDtypeStruct((b, past, V_WIDTH), BF16)],
        compiler_params=_params(("parallel",)),
        name="cachekv",
    )(cache_ckv, krope_blk, w_ukv_p)


def _attn_kernel(*refs, has_cache, n_cast, has_adaln):
    n_in = 5 if has_cache else 3
    if has_cache:
        q_ref, k_ref, v_ref, kc_ref, vc_ref = refs[:n_in]
    else:
        q_ref, k_ref, v_ref = refs[:n_in]
    n_all_in = n_in + n_cast + (3 if has_adaln else 0)
    o_ref = refs[n_all_in]
    for w_ref, wb_ref in zip(refs[n_in:n_in + n_cast], refs[n_all_in + 1:n_all_in + 1 + n_cast]):
        wb_ref[...] = w_ref[...].astype(BF16)
    if has_adaln:
        _adaln_block(*refs[n_in + n_cast:n_all_in], refs[n_all_in + 1 + n_cast])
    tq = q_ref.shape[0]
    lane = lax.broadcasted_iota(jnp.int32, (tq, LANES), 1)
    for pair in range(MLA_HEADS // 2):
        vsl = slice(pair * V_SLAB, (pair + 1) * V_SLAB)
        outs = []
        for j in range(2):
            hd = 2 * pair + j
            sl = slice(hd * HEAD_BLOCK, (hd + 1) * HEAD_BLOCK)
            qh = q_ref[:, sl]
            s = _dot_nt(qh, k_ref[:, sl])
            m = jnp.max(s, axis=-1, keepdims=True)
            if has_cache:
                sc = _dot_nt(qh, kc_ref[:, sl])
                m = jnp.maximum(m, jnp.max(sc, axis=-1, keepdims=True))
            o = _dot(jnp.exp2(s - m).astype(BF16), v_ref[:, vsl])
            if has_cache:
                o = o + _dot(jnp.exp2(sc - m).astype(BF16), vc_ref[:, vsl])
            outs.append(o[:, :LANES] * (1.0 / o[:, LANES:]))
        o_ref[:, pair * LANES:(pair + 1) * LANES] = jnp.where(lane < MLA_V_DIM, outs[0], outs[1]).astype(BF16)


def _attention(q, k, v, cache=None, cast_weights=(), adaln_tail=None):
    b, s, _ = q.shape
    tq = Q_TILE
    n_steps = b * (s // tq)
    kw, vw, ow = MLA_HEADS * HEAD_BLOCK, V_WIDTH, MLA_HEADS * MLA_V_DIM
    in_specs = [pl.BlockSpec((None, tq, kw), lambda i, j: (i, j, 0)),
                pl.BlockSpec((None, s, kw), lambda i, j: (i, 0, 0)),
                pl.BlockSpec((None, s, vw), lambda i, j: (i, 0, 0))]
    args = [q, k, v]
    if cache is not None:
        past = cache[0].shape[1]
        in_specs += [pl.BlockSpec((None, past, kw), lambda i, j: (i, 0, 0)),
                     pl.BlockSpec((None, past, vw), lambda i, j: (i, 0, 0))]
        args += list(cache)
    step = lambda i, j: i * (s // tq) + j
    w_specs = [pl.BlockSpec((w.shape[0] // n_steps, w.shape[1]), lambda i, j: (step(i, j), 0)) for w in cast_weights]
    extra_in, extra_out, extra_shape = [], [], []
    if adaln_tail is not None:
        tn = N_MOD_TAIL * D_MODEL // n_steps
        col0 = N_MOD_MIXER * D_MODEL // tn
        extra_in = [pl.BlockSpec((8, D_MODEL), lambda i, j: (0, 0)),
                    pl.BlockSpec((D_MODEL, tn), lambda i, j: (0, col0 + step(i, j))),
                    pl.BlockSpec((1, tn), lambda i, j: (0, col0 + step(i, j)))]
        extra_out = [pl.BlockSpec((8, tn), lambda i, j: (0, step(i, j)))]
        extra_shape = [jax.ShapeDtypeStruct((8, N_MOD_TAIL * D_MODEL), F32)]
    outs = pl.pallas_call(
        functools.partial(_attn_kernel, has_cache=cache is not None, n_cast=len(cast_weights),
                          has_adaln=adaln_tail is not None),
        grid=(b, s // tq),
        in_specs=in_specs + w_specs + extra_in,
        out_specs=[pl.BlockSpec((None, tq, ow), lambda i, j: (i, j, 0))] + w_specs + extra_out,
        out_shape=[jax.ShapeDtypeStruct((b, s, ow), BF16)]
                  + [jax.ShapeDtypeStruct(w.shape, BF16) for w in cast_weights] + extra_shape,
        compiler_params=_params(("parallel", "arbitrary")),
        name="attn_lat" if cache is not None else "attn_ctx",
    )(*args, *cast_weights, *(adaln_tail or ()))
    return outs[0] if len(outs) == 1 else outs


def _lane_bcast(x, k, width=None):
    y = jnp.broadcast_to(x[:, k:k + 1], x.shape)
    reps = (width or LANES) // LANES
    return y if reps == 1 else jnp.concatenate([y] * reps, axis=1)


def _prefix_sum_rows(x):
    n = x.shape[0]
    r = lax.broadcasted_iota(jnp.int32, (n, n), 0)
    c = lax.broadcasted_iota(jnp.int32, (n, n), 1)
    tri = jnp.where(c <= r, 1.0, 0.0).astype(BF16)
    hi = x.astype(BF16)
    r1 = x - hi.astype(F32)
    mid = r1.astype(BF16)
    lo = (r1 - mid.astype(F32)).astype(BF16)
    return _dot(tri, hi) + _dot(tri, mid) + _dot(tri, lo)


def _gate_prep(g):
    n = g.shape[0]
    lane = lax.broadcasted_iota(jnp.int32, (n, LANES), 1)
    fpre = pltpu.roll(g, LANES - N_DIRHEAD, 1)
    lf = jnp.minimum(fpre, 0.0) - jnp.log(1.0 + jnp.exp(-jnp.abs(fpre)))
    binc = _prefix_sum_rows(lf)
    tot = binc[n - 1:n, :]
    b = jnp.where(lane < MLSTM_HEADS, binc, tot - binc + lf)
    return b, g - b, tot


def _chunk_setup(b, u, tot, m_prev):
    g_last = jnp.maximum(jnp.max(u, axis=0, keepdims=True), m_prev)
    u2 = u * LOG2E
    w = jnp.exp2(u2 - g_last * LOG2E)
    return u2, (b * LOG2E).T, w.T, tot + g_last, jnp.exp(m_prev - g_last)


def _state_update(vt, k, wt, kk):
    c_new = _dot((vt * wt[kk:kk + 1, :]).astype(BF16), k)
    n_new = _dot(wt[:N_DIRHEAD, :].astype(BF16), k)[kk:kk + 1, :]
    return c_new, n_new


def _mlstm_dir_t(s0t, vt, u_b, bt_row, m_row, *, fwd, inter=None):
    sdt, den, g = _dir_weights(s0t, u_b, m_row, fwd=fwd)
    return _dir_finish(_dot(vt, sdt), den, g, bt_row, m_row, inter)


def _dir_weights(s0t, u_b, m_row, *, fwd):
    n = s0t.shape[0]
    r = lax.broadcasted_iota(jnp.int32, (n, n), 0)
    c = lax.broadcasted_iota(jnp.int32, (n, n), 1)
    a = jnp.where((r <= c) if fwd else (r >= c), u_b, -jnp.inf)
    g = jnp.maximum(jnp.max(a, axis=0, keepdims=True), m_row)
    sdt = s0t * jnp.exp2(a - g)
    return sdt.astype(BF16), jnp.sum(sdt, axis=0, keepdims=True), g


def _dir_finish(ht, den, g, bt_row, m_row, inter=None):
    if inter is not None:
        w_inter = jnp.exp2(m_row - g)
        ht = ht + w_inter * inter[0]
        den = den + w_inter * inter[1]
    floor = jnp.exp2(-(bt_row + g))
    return ht * (1.0 / jnp.maximum(jnp.abs(den), floor))


def _mlstm_ctx_kernel(q_ref, k_ref, v_ref, g_ref, ht_ref, c_ref, n_ref, m_ref):
    n = q_ref.shape[1]
    zero_lanes = jnp.zeros((1, LANES), F32)
    zero_row = jnp.zeros((1, n), F32)
    dirs = ((True, 0), (False, MLSTM_HEADS))
    setup = [_chunk_setup(*_gate_prep(g_ref[sq]), zero_lanes) for sq in range(q_ref.shape[0])]
    units = [(sq, hd) for sq in range(q_ref.shape[0]) for hd in range(MLSTM_HEADS)]
    ks = [k_ref[sq, :, hd * MLSTM_DK:(hd + 1) * MLSTM_DK] for sq, hd in units]
    s0ts = [_dot_nt(k, q_ref[sq, :, hd * MLSTM_DK:(hd + 1) * MLSTM_DK]) for k, (sq, hd) in zip(ks, units)]
    vts = [v_ref[sq, :, hd * MLSTM_DV:(hd + 1) * MLSTM_DV].T for sq, hd in units]
    wts = [[_dir_weights(s0t, _lane_bcast(setup[sq][0], off + hd, n), zero_row, fwd=fwd) for fwd, off in dirs]
           for s0t, (sq, hd) in zip(s0ts, units)]
    hts = [[_dot(vt, sdt) for sdt, _, _ in wt] for vt, wt in zip(vts, wts)]
    for (sq, hd), k, vt, wt, ht2 in zip(units, ks, vts, wts, hts):
        _, bt2, w_t, _, _ = setup[sq]
        ht_ref[sq, hd * MLSTM_DV:(hd + 1) * MLSTM_DV, :] = sum(
            _dir_finish(ht_d, den, g, bt2[off + hd:off + hd + 1, :], zero_row)
            for ht_d, (_, den, g), (_, off) in zip(ht2, wt, dirs))
        for _, off in dirs:
            c_ref[sq, off + hd], n_ref[sq, off + hd:off + hd + 1, :] = _state_update(vt, k, w_t, off + hd)
    for sq in range(q_ref.shape[0]):
        m_ref[sq] = setup[sq][3]


CTX_SEQS_PER_STEP = 2


def _mlstm_ctx(mq, mk, mv, gates):
    b, s, _ = mq.shape
    ns = CTX_SEQS_PER_STEP
    seq = lambda c: pl.BlockSpec((ns, s, c), lambda i: (i, 0, 0))
    return pl.pallas_call(
        _mlstm_ctx_kernel,
        grid=(b // ns,),
        in_specs=[seq(MLSTM_HEADS * MLSTM_DK), seq(MLSTM_HEADS * MLSTM_DK), seq(MLSTM_HEADS * MLSTM_DV), seq(LANES)],
        out_specs=[pl.BlockSpec((ns, MLSTM_HEADS * MLSTM_DV, s), lambda i: (i, 0, 0)),
                   pl.BlockSpec((ns, N_DIRHEAD, MLSTM_DV, MLSTM_DK), lambda i: (i, 0, 0, 0)),
                   pl.BlockSpec((ns, N_DIRHEAD, MLSTM_DK), lambda i: (i, 0, 0)),
                   pl.BlockSpec((ns, 1, LANES), lambda i: (i, 0, 0))],
        out_shape=[jax.ShapeDtypeStruct((b, MLSTM_HEADS * MLSTM_DV, s), F32),
                   jax.ShapeDtypeStruct((b, N_DIRHEAD, MLSTM_DV, MLSTM_DK), F32),
                   jax.ShapeDtypeStruct((b, N_DIRHEAD, MLSTM_DK), F32),
                   jax.ShapeDtypeStruct((b, 1, LANES), F32)],
        compiler_params=_params(("parallel",)),
        name="mlstm_ctx",
    )(mq, mk, mv, gates)


def _mlstm_lat_kernel(qf_ref, kf_ref, vf_ref, gf_ref, qb_ref, kb_ref, vb_ref, gb_ref, c0_ref, n0_ref, m0_ref,
                      ht_ref, c_ref, n_ref, m_ref, *, n_chunks):
    step = pl.program_id(1)

    @pl.when(step == 0)
    def _():
        c_ref[...] = c0_ref[...]
        n_ref[...] = n0_ref[...]
        m_ref[...] = m0_ref[...]
        ht_ref[...] = jnp.zeros_like(ht_ref)

    n = qf_ref.shape[0]
    gate_f, gate_b = _gate_prep(gf_ref[...]), _gate_prep(gb_ref[...])
    is_f = lax.broadcasted_iota(jnp.int32, (n, LANES), 1) < MLSTM_HEADS
    b, u, tot = (jnp.where(is_f[:x.shape[0]], x, y) for x, y in zip(gate_f, gate_b))
    m_prev = m_ref[...]
    u2, bt2, w_t, m_new, decay = _chunk_setup(b, u, tot, m_prev)
    m2_prev = m_prev * LOG2E
    units = [(fwd, hd, hd if fwd else MLSTM_HEADS + hd) for fwd in (True, False) for hd in range(MLSTM_HEADS)]
    refs = {True: (qf_ref, kf_ref, vf_ref), False: (qb_ref, kb_ref, vb_ref)}
    qs = [refs[fwd][0][:, hd * MLSTM_DK:(hd + 1) * MLSTM_DK] for fwd, hd, _ in units]
    ks = [refs[fwd][1][:, hd * MLSTM_DK:(hd + 1) * MLSTM_DK] for fwd, hd, _ in units]
    vts = [refs[fwd][2][:, hd * MLSTM_DV:(hd + 1) * MLSTM_DV].T for fwd, hd, _ in units]
    s0ts = [_dot_nt(k, q) for k, q in zip(ks, qs)]
    c_prevs = [c_ref[kk] for _, _, kk in units]
    n_prevs = [n_ref[kk:kk + 1, :] for _, _, kk in units]
    inters = [(_dot_nt(c.astype(BF16), q), _dot_nt(jnp.broadcast_to(nv, (8, MLSTM_DK)).astype(BF16), q)[0:1, :])
              for c, nv, q in zip(c_prevs, n_prevs, qs)]
    m_rows = [_lane_bcast(m2_prev, kk, n) for _, _, kk in units]
    wts = [_dir_weights(s0t, _lane_bcast(u2, kk, n), m_row, fwd=fwd)
           for s0t, m_row, (fwd, _, kk) in zip(s0ts, m_rows, units)]
    hts = [_dot(vt, sdt) for vt, (sdt, _, _) in zip(vts, wts)]
    for (fwd, hd, kk), k, vt, (_, den, g), ht, inter, m_row, c_prev, n_prev in zip(
            units, ks, vts, wts, hts, inters, m_rows, c_prevs, n_prevs):
        chunk = step if fwd else n_chunks - 1 - step
        ht_ref[chunk, hd * MLSTM_DV:(hd + 1) * MLSTM_DV, :] += _dir_finish(ht, den, g, bt2[kk:kk + 1, :], m_row, inter)
        c_new, n_new = _state_update(vt, k, w_t, kk)
        dk = _lane_bcast(decay, kk)
        c_ref[kk] = dk * c_prev + c_new
        n_ref[kk:kk + 1, :] = dk * n_prev + n_new
    m_ref[...] = m_new


def _mlstm_lat(mq, mk, mv, gates, c0, n0, m0):
    b, s, _ = mq.shape
    lc = MLSTM_CHUNK
    nc = s // lc
    fw = lambda c: pl.BlockSpec((None, lc, c), lambda i, j: (i, j, 0))
    bw = lambda c: pl.BlockSpec((None, lc, c), lambda i, j: (i, nc - 1 - j, 0))
    dk, dv = MLSTM_HEADS * MLSTM_DK, MLSTM_HEADS * MLSTM_DV
    return pl.pallas_call(
        functools.partial(_mlstm_lat_kernel, n_chunks=nc),
        grid=(b, nc),
        in_specs=[fw(dk), fw(dk), fw(dv), fw(LANES), bw(dk), bw(dk), bw(dv), bw(LANES),
                  pl.BlockSpec((None, N_DIRHEAD, MLSTM_DV, MLSTM_DK), lambda i, j: (i, 0, 0, 0)),
                  pl.BlockSpec((None, N_DIRHEAD, MLSTM_DK), lambda i, j: (i, 0, 0)),
                  pl.BlockSpec((None, 1, LANES), lambda i, j: (i, 0, 0))],
        out_specs=pl.BlockSpec((nc, dv, lc), lambda i, j: (i, 0, 0)),
        out_shape=jax.ShapeDtypeStruct((b * nc, dv, lc), F32),
        scratch_shapes=[pltpu.VMEM((N_DIRHEAD, MLSTM_DV, MLSTM_DK), F32),
                        pltpu.VMEM((N_DIRHEAD, MLSTM_DK), F32),
                        pltpu.VMEM((1, LANES), F32)],
        compiler_params=_params(("parallel", "arbitrary")),
        name="mlstm_lat",
    )(mq, mk, mv, gates, mq, mk, mv, gates, c0, n0, m0)


def _tail_kernel(x_ref, mod_ref, att_ref, hst_ref, smo_ref, sbr_ref, gml_ref, womla_ref, womlstm_ref, wout_ref,
                 gffn_ref, wfin_ref, wfout_ref, gfin_ref, y_ref):
    mod = mod_ref[...]
    gml = gml_ref[...]
    tm = x_ref.shape[0]
    groups = [slice(r, r + TAIL_ROWS) for r in range(0, tm, TAIL_ROWS)]
    a = [_dot(att_ref[g, :], womla_ref[...]) for g in groups]
    hm = []
    for gi, g in enumerate(groups):
        hs = hst_ref[gi].T
        parts = []
        for hd in range(MLSTM_HEADS):
            sl = slice(hd * MLSTM_DV, (hd + 1) * MLSTM_DV)
            parts.append((_rms(hs[:, sl], gml[:, sl]) * smo_ref[g, sl].astype(F32)).astype(BF16))
        hm.append(jnp.concatenate(parts, axis=1))
    bm = [_dot(h, womlstm_ref[...]) for h in hm]
    merged = [(sbr_ref[g, :D_MODEL].astype(F32) * ai + sbr_ref[g, D_MODEL:].astype(F32) * bi).astype(BF16)
              for g, ai, bi in zip(groups, a, bm)]
    x1 = [x_ref[g, :] + mod[0:1] * _dot(mi, wout_ref[...]) for g, mi in zip(groups, merged)]
    h2 = [(_rms(xi, gffn_ref[...]) * (1.0 + mod[2:3]) + mod[1:2]).astype(BF16) for xi in x1]
    ga = [_dot(hi, wfin_ref[:, :FFN_HIDDEN]) for hi in h2]
    gu = [_dot(hi, wfin_ref[:, FFN_HIDDEN:]) for hi in h2]
    act = [(gi * _sigmoid(gi) * ui).astype(BF16) for gi, ui in zip(ga, gu)]
    for g, xi, ci in zip(groups, x1, act):
        y_ref[g, :] = _rms(xi + mod[3:4] * _dot(ci, wfout_ref[...]), gfin_ref[...])


def _tail(x, mod, att, hst, smo, sbr, gml, w_o_mla, w_o_mlstm, w_out, gffn, w_ffn_in, w_ffn_out, gfin, *, seq_len, name):
    n_tok = x.shape[0]
    tm = TOKEN_TILE
    tiles_per_seq = seq_len // tm
    assert hst.shape[0] * TAIL_ROWS == n_tok and hst.shape[2] == TAIL_ROWS
    tok = lambda c: pl.BlockSpec((tm, c), lambda i: (i, 0))
    consts = (gml, w_o_mla, w_o_mlstm, w_out, gffn, w_ffn_in, w_ffn_out, gfin)
    return pl.pallas_call(
        _tail_kernel,
        grid=(n_tok // tm,),
        in_specs=[tok(D_MODEL), _mod_spec(mod, tiles_per_seq), tok(att.shape[1]),
                  pl.BlockSpec((tm // TAIL_ROWS,) + hst.shape[1:], lambda i: (i, 0, 0)),
                  tok(smo.shape[1]), tok(sbr.shape[1])]
                 + [_const_spec(a.shape) for a in consts],
        out_specs=tok(D_MODEL),
        out_shape=jax.ShapeDtypeStruct((n_tok, D_MODEL), F32),
        compiler_params=_params(("parallel",)),
        name=name,
    )(x, mod, att, hst, smo, sbr, *consts)


_IN_SIZES = (Q_RANK, KV_RANK, ROPE_DIM, MLSTM_HEADS * MLSTM_DK, MLSTM_HEADS * MLSTM_DK, MLSTM_HEADS * MLSTM_DV,
             4 * MLSTM_HEADS, MLSTM_HEADS * MLSTM_DV, 2 * D_MODEL)
_IN_OFF = tuple(sum(_IN_SIZES[:i]) for i in range(len(_IN_SIZES) + 1))
PACK_ROWS = LANES
_PAD_AFTER = tuple((_SEG_OFF[i + 1] // PACK_ROWS, _SEG_WIDTHS[i] - _IN_SIZES[i], _IN_SIZES[i] % PACK_ROWS)
                   for i in range(len(_IN_SIZES)) if _SEG_WIDTHS[i] != _IN_SIZES[i])


PACK_PIECES = 8
N_PACK_STEPS = pl.cdiv(N_IN_PACKED // PACK_ROWS, PACK_PIECES)
_N_IN = _IN_OFF[-1]


def _pack_src_row(p):
    src = p * PACK_ROWS
    for first_piece_after, pad, _ in _PAD_AFTER:
        src = src - jnp.where(p >= first_piece_after, pad, 0)
    src = jnp.minimum(src, _N_IN - PACK_ROWS)
    return pl.multiple_of(src, math.gcd(PACK_ROWS, _N_IN, *(pad for _, pad, _ in _PAD_AFTER)))


def _pack_in_kernel(*refs):
    w_refs, o_ref = refs[:PACK_PIECES], refs[PACK_PIECES]
    row = lax.broadcasted_iota(jnp.int32, w_refs[0].shape, 0)
    for j, w_ref in enumerate(w_refs):
        p = pl.program_id(0) * PACK_PIECES + j
        valid = jnp.where(p < N_IN_PACKED // PACK_ROWS, PACK_ROWS, 0)
        for first_piece_after, _, width in _PAD_AFTER:
            valid = jnp.where(p == first_piece_after - 1, width, valid)
        o_ref[j * PACK_ROWS:(j + 1) * PACK_ROWS, :] = jnp.where(row < valid, w_ref[...], 0.0).astype(BF16)


def _pack_in(w_in_t):
    n, k = w_in_t.shape
    piece = lambda j: pl.BlockSpec((pl.Element(PACK_ROWS), pl.Element(k)),
                                   lambda i: (_pack_src_row(i * PACK_PIECES + j), 0))
    return pl.pallas_call(
        _pack_in_kernel,
        grid=(N_PACK_STEPS,),
        in_specs=[piece(j) for j in range(PACK_PIECES)],
        out_specs=pl.BlockSpec((PACK_PIECES * PACK_ROWS, k), lambda i: (i, 0)),
        out_shape=jax.ShapeDtypeStruct((N_PACK_STEPS * PACK_PIECES * PACK_ROWS, k), BF16),
        compiler_params=_params(("parallel",)),
        name="pack_in",
    )(*([w_in_t] * PACK_PIECES))


def _pack_small(w_uq, w_ukv, b_gates):
    qd = NOPE_DIM + ROPE_DIM
    w_uq_p = jnp.pad(w_uq.reshape(Q_RANK, MLA_HEADS, qd), ((0, 0), (0, 0), (0, HEAD_BLOCK - qd)))
    w_uq_p = w_uq_p.reshape(Q_RANK, MLA_HEADS * HEAD_BLOCK).astype(BF16)
    kv = w_ukv.reshape(KV_RANK, MLA_HEADS, NOPE_DIM + MLA_V_DIM)
    wk = jnp.pad(kv[:, :, :NOPE_DIM], ((0, 0), (0, 0), (0, HEAD_BLOCK - NOPE_DIM))).reshape(KV_RANK, -1)
    wv = kv[:, :, NOPE_DIM:].reshape(KV_RANK, -1)
    w_ukv_p = jnp.concatenate([wk, wv], axis=1).astype(BF16)
    bg = jnp.pad(b_gates, ((0, 0), (0, LANES - b_gates.shape[1])))
    return w_uq_p, w_ukv_p, bg


def _rope_tables(n_tokens):
    pos = np.arange(n_tokens)
    row = (pos // GRID_W).astype(np.float64)[:, None]
    col = (pos % GRID_W).astype(np.float64)[:, None]
    half = ROPE_DIM // 2
    inv = (np.float32(ROPE_BASE) ** (-np.arange(0, half, 2, dtype=np.float32) / np.float32(half))).astype(np.float64)
    r = np.arange(LANES) - ROPE_LANE0
    in_rope = (r >= 0) & (r < ROPE_DIM)
    rr = np.clip(r, 0, ROPE_DIM - 1)
    freq = inv[rr % (half // 2)][None, :]
    ang = np.where((rr // half == 0)[None, :], row * freq, col * freq).astype(np.float32).astype(np.float64)
    first = (rr % half) < (half // 2)
    cos = np.where(in_rope[None, :], np.cos(ang), 1.0)
    sin = np.sin(ang)
    sin_lo = np.where((in_rope & first)[None, :], -sin, 0.0)
    sin_hi = np.where((in_rope & ~first)[None, :], sin, 0.0)
    return tuple(jnp.asarray(t, dtype=F32) for t in (cos, sin_lo, sin_hi))


def kernel(x_prompt, x_sample, cache_ckv, cache_krope, state_C, state_n, state_m, c, c_ctx, w_mod, b_mod, g_norm_mix,
           w_in, b_gates, g_q_norm, w_uq, g_kv_norm, w_ukv, g_mlstm_norm, w_o_mla, w_o_mlstm, w_out, g_norm_ffn,
           w_ffn_in, w_ffn_out, g_final):
    bp, sp, d = x_prompt.shape
    bs, ss, _ = x_sample.shape
    layer = 0
    assert w_mod.shape[0] == 1 and sp == MLSTM_CHUNK and ss % MLSTM_CHUNK == 0

    cond = jnp.concatenate([c_ctx[None, :], c, jnp.zeros((8 - 1 - bs, d), F32)], axis=0)
    adaln_args = (cond, w_mod[layer], b_mod[layer][None, :])
    mod_in = _adaln(*adaln_args).reshape(8, N_MOD_MIXER, d)
    mod_ctx, mod_lat = mod_in[0:1], mod_in[1:1 + bs]

    w_in_p = _pack_in(w_in[layer].T)
    w_uq_p, w_ukv_p, bg = _pack_small(w_uq[layer], w_ukv[layer], b_gates[layer][None, :])
    row = lambda g: g[layer][None, :]
    shared_in = (row(g_norm_mix), w_in_p, row(g_q_norm), w_uq_p, row(g_kv_norm), w_ukv_p, bg)
    seq3 = lambda a, b_, s_: a.reshape(b_, s_, a.shape[-1])

    xl = x_sample.reshape(bs * ss, d)
    q, kc, v, mq_l, mk_l, mv_l, gates_l, smo_l, sbr_l = _inproj(xl, mod_lat, _rope_tables(ss), *shared_in, seq_len=ss)
    krope_blk = jnp.pad(cache_krope[:, layer], ((0, 0), (0, 0), (ROPE_LANE0, LANES - ROPE_LANE0 - ROPE_DIM)))
    cache = _cachekv(cache_ckv[:, layer], krope_blk, w_ukv_p)
    att_l, womla_b, womlstm_b, wout_b, wfin_b, wfout_b, mod_tail = _attention(
        seq3(q, bs, ss), seq3(kc, bs, ss), seq3(v, bs, ss), cache,
        cast_weights=(w_o_mla[layer], w_o_mlstm[layer], w_out[layer], w_ffn_in[layer], w_ffn_out[layer]),
        adaln_tail=adaln_args)
    tail_w = (row(g_mlstm_norm), womla_b, womlstm_b, wout_b, row(g_norm_ffn), wfin_b, wfout_b, g_final[None, :])
    mod_tail = mod_tail.reshape(8, N_MOD_TAIL, d)
    modt_ctx, modt_lat = mod_tail[0:1], mod_tail[1:1 + bs]

    xc = x_prompt.reshape(bp * sp, d)
    q, kc, v, mq, mk, mv, gates, smo, sbr, ckv, kpe = _inproj(xc, mod_ctx, None, *shared_in, seq_len=sp)
    att = _attention(seq3(q, bp, sp), seq3(kc, bp, sp), seq3(v, bp, sp))
    hs, new_c, new_n, new_m = _mlstm_ctx(seq3(mq, bp, sp), seq3(mk, bp, sp), seq3(mv, bp, sp), seq3(gates, bp, sp))
    y_prompt = _tail(xc, modt_ctx, att.reshape(bp * sp, -1), hs, smo, sbr, *tail_w,
                     seq_len=sp, name="tail_ctx").reshape(bp, sp, d)
    new_ckv = ckv.reshape(bp, 1, sp, KV_RANK)
    new_krope = kpe.reshape(bp, 1, sp, ROPE_DIM)
    new_C = new_c.reshape(bp, 1, 2, MLSTM_HEADS, MLSTM_DV, MLSTM_DK)
    new_N = new_n.reshape(bp, 1, 2, MLSTM_HEADS, MLSTM_DK)
    new_M = new_m[:, 0, :N_DIRHEAD].reshape(bp, 1, 2, MLSTM_HEADS)

    c0 = state_C[:, layer].reshape(bs, N_DIRHEAD, MLSTM_DV, MLSTM_DK)
    n0 = state_n[:, layer].reshape(bs, N_DIRHEAD, MLSTM_DK)
    m0 = jnp.pad(state_m[:, layer].reshape(bs, 1, N_DIRHEAD), ((0, 0), (0, 0), (0, LANES - N_DIRHEAD)))
    hs = _mlstm_lat(seq3(mq_l, bs, ss), seq3(mk_l, bs, ss), seq3(mv_l, bs, ss), seq3(gates_l, bs, ss), c0, n0, m0)
    y_sample = _tail(xl, modt_lat, att_l.reshape(bs * ss, -1), hs, smo_l, sbr_l, *tail_w,
                     seq_len=ss, name="tail_lat").reshape(bs, ss, d)
    return (y_prompt, y_sample, new_ckv, new_krope, new_C, new_N, new_M)
```

```python
import functools
import math

import jax
import jax.numpy as jnp
import numpy as np
from jax import lax
from jax.experimental import pallas as pl
from jax.experimental.pallas import tpu as pltpu

F32 = jnp.float32
BF16 = jnp.bfloat16

D_MODEL = 1024
GRID_W = 64
MLA_HEADS = 8
Q_RANK = 384
KV_RANK = 256
NOPE_DIM = 64
ROPE_DIM = 32
MLA_V_DIM = 64
ROPE_BASE = 10000.0
MLA_SCALE = (NOPE_DIM + ROPE_DIM) ** -0.5
MLSTM_HEADS = 4
MLSTM_DK = 128
MLSTM_DV = 256
FFN_HIDDEN = ((8 * D_MODEL // 3 + 255) // 256) * 256
EPS = 1e-6

LANES = 128
HEAD_BLOCK = LANES
ROPE_LANE0 = NOPE_DIM
N_DIRHEAD = 2 * MLSTM_HEADS
MLSTM_CHUNK = 256
LOG2E = math.log2(math.e)
Q_PRESCALE = MLA_SCALE * LOG2E
V_SLAB = 2 * LANES
V_WIDTH = (MLA_HEADS // 2) * V_SLAB
TOKEN_TILE = 512
TAIL_ROWS = 256
INPROJ_ROWS = TOKEN_TILE
Q_TILE = 256
ATTN_GROUP = 8
VMEM_LIMIT = 56 * 1024 * 1024

_SEG_WIDTHS = (Q_RANK, KV_RANK, LANES, MLSTM_HEADS * MLSTM_DK, MLSTM_HEADS * MLSTM_DK,
               MLSTM_HEADS * MLSTM_DV, LANES, MLSTM_HEADS * MLSTM_DV, 2 * D_MODEL)
_SEG_OFF = tuple(sum(_SEG_WIDTHS[:i]) for i in range(len(_SEG_WIDTHS) + 1))
SEG_Q, SEG_KV, SEG_KPE, SEG_MQ, SEG_MK, SEG_MV, SEG_GATE, SEG_MO, SEG_BR = (
    (_SEG_OFF[i], _SEG_OFF[i + 1]) for i in range(9))
N_IN_PACKED = _SEG_OFF[-1]


def _dot(a, b):
    return jnp.dot(a, b, preferred_element_type=F32)


def _dot_nt(a, b):
    return lax.dot_general(a, b, (((1,), (1,)), ((), ())), preferred_element_type=F32)


def _dot_tn(a, b):
    return lax.dot_general(a, b, (((0,), (0,)), ((), ())), preferred_element_type=F32)


def _rms(x, g):
    ms = jnp.mean(x * x, axis=-1, keepdims=True)
    return x * lax.rsqrt(ms + EPS) * g


def _sigmoid(x):
    return 1.0 / (1.0 + jnp.exp(-x))


def _const_spec(shape):
    nd = len(shape)
    return pl.BlockSpec(shape, lambda *_: (0,) * nd, pipeline_mode=pl.Buffered(1))


def _mod_spec(mod, tiles_per_seq):
    if mod.shape[0] == 1:
        return pl.BlockSpec((None,) + mod.shape[1:], lambda i: (0, 0, 0))
    return pl.BlockSpec((None,) + mod.shape[1:], lambda i: (i // tiles_per_seq, 0, 0))


def _params(sem, flags=None):
    return pltpu.CompilerParams(dimension_semantics=sem, vmem_limit_bytes=VMEM_LIMIT, flags=flags)


N_MOD_MIXER = 2
N_MOD_TAIL = 4


def _adaln_block(c_ref, w_ref, b_ref, o_ref):
    c = c_ref[...]
    s = c * _sigmoid(c)
    o_ref[...] = _dot(s.astype(BF16), w_ref[...].astype(BF16)) + b_ref[...]


def _adaln(cond8, w_mod, b_mod):
    tn = D_MODEL
    return pl.pallas_call(
        _adaln_block,
        grid=(N_MOD_MIXER,),
        in_specs=[pl.BlockSpec((8, D_MODEL), lambda j: (0, 0)),
                  pl.BlockSpec((D_MODEL, tn), lambda j: (0, j)),
                  pl.BlockSpec((1, tn), lambda j: (0, j))],
        out_specs=pl.BlockSpec((8, tn), lambda j: (0, j)),
        out_shape=jax.ShapeDtypeStruct((8, N_MOD_MIXER * D_MODEL), F32),
        compiler_params=_params(("arbitrary",)),
        name="adaln",
    )(cond8, w_mod, b_mod)


def _rope_block(x, cos, sin_lo, sin_hi):
    return x * cos + pltpu.roll(x, LANES - 8, 1) * sin_lo + pltpu.roll(x, 8, 1) * sin_hi


def _store_kv(kvn, kp, kc_ref, v_ref, g=slice(None)):
    for hd in range(MLA_HEADS):
        sl = slice(hd * HEAD_BLOCK, (hd + 1) * HEAD_BLOCK)
        kc_ref[g, sl] = (kvn[:, sl] + kp).astype(BF16)
    v0 = MLA_HEADS * HEAD_BLOCK
    ones = jnp.ones((kvn.shape[0], LANES), BF16)
    for pair in range(MLA_HEADS // 2):
        v_ref[g, pair * V_SLAB:pair * V_SLAB + LANES] = kvn[:, v0 + pair * LANES:v0 + (pair + 1) * LANES].astype(BF16)
        v_ref[g, pair * V_SLAB + LANES:(pair + 1) * V_SLAB] = ones


def _inproj_kernel(*refs, rope):
    if rope:
        (x_ref, mod_ref, cos_ref, slo_ref, shi_ref, gmix_ref, win_ref, gq_ref, wuq_ref, gkv_ref, wukv_ref,
         bg_ref, q_ref, kc_ref, v_ref, mq_ref, mk_ref, mv_ref, gate_ref, smo_ref, sbr_ref) = refs
    else:
        (x_ref, mod_ref, gmix_ref, win_ref, gq_ref, wuq_ref, gkv_ref, wukv_ref,
         bg_ref, q_ref, kc_ref, v_ref, mq_ref, mk_ref, mv_ref, gate_ref, smo_ref, sbr_ref,
         ckv_ref, kpe_ref) = refs
    mod = mod_ref[...]
    for r0 in range(0, x_ref.shape[0], INPROJ_ROWS):
        g = slice(r0, r0 + INPROJ_ROWS)
        if rope:
            cos, slo, shi = cos_ref[g, :], slo_ref[g, :], shi_ref[g, :]
        h = _rms(x_ref[g, :], gmix_ref[...]) * (1.0 + mod[1:2]) + mod[0:1]
        hb = h.astype(BF16)

        def proj(seg):
            return _dot_nt(hb, win_ref[seg[0]:seg[1], :])

        qn = _rms(proj(SEG_Q), gq_ref[...]).astype(BF16)
        ckv = _rms(proj(SEG_KV), gkv_ref[...])
        zkpe = proj(SEG_KPE)
        sbr_ref[g, :] = _sigmoid(proj(SEG_BR)).astype(BF16)
        smo_ref[g, :] = _sigmoid(proj(SEG_MO)).astype(BF16)
        gate_ref[g, :] = proj(SEG_GATE) + bg_ref[...]

        q = _dot(qn, wuq_ref[...])
        for hd in range(MLA_HEADS):
            sl = slice(hd * HEAD_BLOCK, (hd + 1) * HEAD_BLOCK)
            qh = q[:, sl]
            if rope:
                qh = _rope_block(qh, cos, slo, shi)
            q_ref[g, sl] = (qh * Q_PRESCALE).astype(BF16)

        if not rope:
            ckv_ref[g, :] = ckv
            kpe_ref[g, :] = zkpe[:, :ROPE_DIM]
        kp = pltpu.roll(zkpe, ROPE_LANE0, 1)
        if rope:
            kp = _rope_block(kp, cos, slo, shi)
        kvn = _dot(ckv.astype(BF16), wukv_ref[...])
        _store_kv(kvn, kp, kc_ref, v_ref, g)

        mv_ref[g, :] = proj(SEG_MV).astype(BF16)
        mk_ref[g, :] = (proj(SEG_MK) * (MLSTM_DK ** -0.5)).astype(BF16)
        mq_ref[g, :] = proj(SEG_MQ).astype(BF16)


def _inproj(x, mod, rope_tabs, gmix, w_in_p, gq, w_uq_p, gkv, w_ukv_p, bg, *, seq_len):
    n_tok = x.shape[0]
    tm = TOKEN_TILE
    tiles_per_seq = seq_len // tm
    rope = rope_tabs is not None
    tok = lambda c: pl.BlockSpec((tm, c), lambda i: (i, 0))
    in_specs = [tok(D_MODEL), _mod_spec(mod, tiles_per_seq)]
    args = [x, mod]
    if rope:
        in_specs += [pl.BlockSpec((tm, LANES), lambda i: (i % tiles_per_seq, 0))] * 3
        args += list(rope_tabs)
    in_specs += [_const_spec(a.shape) for a in (gmix, w_in_p, gq, w_uq_p, gkv, w_ukv_p, bg)]
    args += [gmix, w_in_p, gq, w_uq_p, gkv, w_ukv_p, bg]
    widths = [(MLA_HEADS * HEAD_BLOCK, BF16), (MLA_HEADS * HEAD_BLOCK, BF16), (V_WIDTH, BF16),
              (MLSTM_HEADS * MLSTM_DK, BF16), (MLSTM_HEADS * MLSTM_DK, BF16), (MLSTM_HEADS * MLSTM_DV, BF16),
              (LANES, F32), (MLSTM_HEADS * MLSTM_DV, BF16), (2 * D_MODEL, BF16)]
    if not rope:
        widths += [(KV_RANK, F32), (ROPE_DIM, F32)]
    return pl.pallas_call(
        functools.partial(_inproj_kernel, rope=rope),
        grid=(n_tok // tm,),
        in_specs=in_specs,
        out_specs=[tok(c) for c, _ in widths],
        out_shape=[jax.ShapeDtypeStruct((n_tok, c), dt) for c, dt in widths],
        compiler_params=_params(("parallel",)),
        name="inproj_lat" if rope else "inproj_ctx",
    )(*args)


def _cachekv_kernel(ckv_ref, kp_ref, wukv_ref, kc_ref, v_ref):
    kvn = _dot(ckv_ref[...].astype(BF16), wukv_ref[...])
    _store_kv(kvn, kp_ref[...], kc_ref, v_ref)


def _cachekv(cache_ckv, krope_blk, w_ukv_p):
    b, past, _ = cache_ckv.shape
    return pl.pallas_call(
        _cachekv_kernel,
        grid=(b,),
        in_specs=[pl.BlockSpec((None, past, KV_RANK), lambda i: (i, 0, 0)),
                  pl.BlockSpec((None, past, LANES), lambda i: (i, 0, 0)),
                  _const_spec(w_ukv_p.shape)],
        out_specs=[pl.BlockSpec((None, past, MLA_HEADS * HEAD_BLOCK), lambda i: (i, 0, 0)),
                   pl.BlockSpec((None, past, V_WIDTH), lambda i: (i, 0, 0))],
        out_shape=[jax.ShapeDtypeStruct((b, past, MLA_HEADS * HEAD_BLOCK), BF16),
                   jax.ShapeDtypeStruct((b, past, V_WIDTH), BF16)],
        compiler_params=_params(("parallel",)),
        name="cachekv",
    )(cache_ckv, krope_blk, w_ukv_p)


def _attn_kernel(*refs, has_cache, n_cast, has_adaln):
    n_in = 5 if has_cache else 3
    if has_cache:
        q_ref, k_ref, v_ref, kc_ref, vc_ref = refs[:n_in]
    else:
        q_ref, k_ref, v_ref = refs[:n_in]
    n_all_in = n_in + n_cast + (3 if has_adaln else 0)
    o_ref = refs[n_all_in]
    for w_ref, wb_ref in zip(refs[n_in:n_in + n_cast], refs[n_all_in + 1:n_all_in + 1 + n_cast]):
        wb_ref[...] = w_ref[...].astype(BF16)
    if has_adaln:
        _adaln_block(*refs[n_in + n_cast:n_all_in], refs[n_all_in + 1 + n_cast])
    tq = q_ref.shape[0]
    lane = lax.broadcasted_iota(jnp.int32, (tq, LANES), 1)
    for h0 in range(0, MLA_HEADS, ATTN_GROUP):
        heads = range(h0, h0 + ATTN_GROUP)
        hsl = [slice(hd * HEAD_BLOCK, (hd + 1) * HEAD_BLOCK) for hd in heads]
        vsl = [slice(hd // 2 * V_SLAB, (hd // 2 + 1) * V_SLAB) for hd in heads]
        s = [_dot_nt(q_ref[:, sl], k_ref[:, sl]) for sl in hsl]
        m = [jnp.max(si, axis=-1, keepdims=True) for si in s]
        if has_cache:
            sc = [_dot_nt(q_ref[:, sl], kc_ref[:, sl]) for sl in hsl]
            m = [jnp.maximum(mi, jnp.max(ci, axis=-1, keepdims=True)) for mi, ci in zip(m, sc)]
        p = [jnp.exp2(si - mi).astype(BF16) for si, mi in zip(s, m)]
        o = [_dot(pi, v_ref[:, sl]) for pi, sl in zip(p, vsl)]
        if has_cache:
            pc = [jnp.exp2(ci - mi).astype(BF16) for ci, mi in zip(sc, m)]
            o = [oi + _dot(pi, vc_ref[:, sl]) for oi, pi, sl in zip(o, pc, vsl)]
        outs = [oi[:, :LANES] * (1.0 / oi[:, LANES:]) for oi in o]
        for j in range(0, ATTN_GROUP, 2):
            pair = (h0 + j) // 2
            o_ref[:, pair * LANES:(pair + 1) * LANES] = jnp.where(lane < MLA_V_DIM, outs[j], outs[j + 1]).astype(BF16)


def _attention(q, k, v, cache=None, cast_weights=(), adaln_tail=None):
    b, s, _ = q.shape
    tq = Q_TILE
    n_steps = b * (s // tq)
    kw, vw, ow = MLA_HEADS * HEAD_BLOCK, V_WIDTH, MLA_HEADS * MLA_V_DIM
    in_specs = [pl.BlockSpec((None, tq, kw), lambda i, j: (i, j, 0)),
                pl.BlockSpec((None, s, kw), lambda i, j: (i, 0, 0)),
                pl.BlockSpec((None, s, vw), lambda i, j: (i, 0, 0))]
    args = [q, k, v]
    if cache is not None:
        past = cache[0].shape[1]
        in_specs += [pl.BlockSpec((None, past, kw), lambda i, j: (i, 0, 0)),
                     pl.BlockSpec((None, past, vw), lambda i, j: (i, 0, 0))]
        args += list(cache)
    step = lambda i, j: i * (s // tq) + j
    w_specs = [pl.BlockSpec((w.shape[0] // n_steps, w.shape[1]), lambda i, j: (step(i, j), 0)) for w in cast_weights]
    extra_in, extra_out, extra_shape = [], [], []
    if adaln_tail is not None:
        tn = N_MOD_TAIL * D_MODEL // n_steps
        col0 = N_MOD_MIXER * D_MODEL // tn
        extra_in = [pl.BlockSpec((8, D_MODEL), lambda i, j: (0, 0)),
                    pl.BlockSpec((D_MODEL, tn), lambda i, j: (0, col0 + step(i, j))),
                    pl.BlockSpec((1, tn), lambda i, j: (0, col0 + step(i, j)))]
        extra_out = [pl.BlockSpec((8, tn), lambda i, j: (0, step(i, j)))]
        extra_shape = [jax.ShapeDtypeStruct((8, N_MOD_TAIL * D_MODEL), F32)]
    outs = pl.pallas_call(
        functools.partial(_attn_kernel, has_cache=cache is not None, n_cast=len(cast_weights),
                          has_adaln=adaln_tail is not None),
        grid=(b, s // tq),
        in_specs=in_specs + w_specs + extra_in,
        out_specs=[pl.BlockSpec((None, tq, ow), lambda i, j: (i, j, 0))] + w_specs + extra_out,
        out_shape=[jax.ShapeDtypeStruct((b, s, ow), BF16)]
                  + [jax.ShapeDtypeStruct(w.shape, BF16) for w in cast_weights] + extra_shape,
        compiler_params=_params(("parallel", "arbitrary")),
        name="attn_lat" if cache is not None else "attn_ctx",
    )(*args, *cast_weights, *(adaln_tail or ()))
    return outs[0] if len(outs) == 1 else outs


def _lane_bcast(x, k, width=None):
    y = jnp.broadcast_to(x[:, k:k + 1], x.shape)
    reps = (width or LANES) // LANES
    return y if reps == 1 else jnp.concatenate([y] * reps, axis=1)


def _prefix_sum_rows(x):
    n = x.shape[0]
    r = lax.broadcasted_iota(jnp.int32, (n, n), 0)
    c = lax.broadcasted_iota(jnp.int32, (n, n), 1)
    tri = jnp.where(c <= r, 1.0, 0.0).astype(BF16)
    hi = x.astype(BF16)
    r1 = x - hi.astype(F32)
    mid = r1.astype(BF16)
    lo = (r1 - mid.astype(F32)).astype(BF16)
    return _dot(tri, hi) + _dot(tri, mid) + _dot(tri, lo)


def _gate_prep(g):
    n = g.shape[0]
    lane = lax.broadcasted_iota(jnp.int32, (n, LANES), 1)
    fpre = pltpu.roll(g, LANES - N_DIRHEAD, 1)
    lf = jnp.minimum(fpre, 0.0) - jnp.log(1.0 + jnp.exp(-jnp.abs(fpre)))
    binc = _prefix_sum_rows(lf)
    tot = binc[n - 1:n, :]
    b = jnp.where(lane < MLSTM_HEADS, binc, tot - binc + lf)
    return b, g - b, tot


def _chunk_setup(b, u, tot, m_prev):
    g_last = jnp.maximum(jnp.max(u, axis=0, keepdims=True), m_prev)
    u2 = u * LOG2E
    w = jnp.exp2(u2 - g_last * LOG2E)
    return u2, (b * LOG2E).T, w.T, tot + g_last, jnp.exp(m_prev - g_last)


def _state_update(vt, k, wt, kk):
    c_new = _dot((vt * wt[kk:kk + 1, :]).astype(BF16), k)
    n_new = _dot(wt[:N_DIRHEAD, :].astype(BF16), k)[kk:kk + 1, :]
    return c_new, n_new


def _mlstm_dir_t(s0t, vt, u_b, bt_row, m_row, *, fwd, inter=None):
    sdt, den, g = _dir_weights(s0t, u_b, m_row, fwd=fwd)
    return _dir_finish(_dot(vt, sdt), den, g, bt_row, m_row, inter)


def _dir_weights(s0t, u_b, m_row, *, fwd):
    n = s0t.shape[0]
    r = lax.broadcasted_iota(jnp.int32, (n, n), 0)
    c = lax.broadcasted_iota(jnp.int32, (n, n), 1)
    a = jnp.where((r <= c) if fwd else (r >= c), u_b, -jnp.inf)
    g = jnp.maximum(jnp.max(a, axis=0, keepdims=True), m_row)
    sdt = s0t * jnp.exp2(a - g)
    return sdt.astype(BF16), jnp.sum(sdt, axis=0, keepdims=True), g


def _dir_finish(ht, den, g, bt_row, m_row, inter=None):
    if inter is not None:
        w_inter = jnp.exp2(m_row - g)
        ht = ht + w_inter * inter[0]
        den = den + w_inter * inter[1]
    floor = jnp.exp2(-(bt_row + g))
    return ht * (1.0 / jnp.maximum(jnp.abs(den), floor))


def _mlstm_ctx_kernel(q_ref, k_ref, v_ref, g_ref, ht_ref, c_ref, n_ref, m_ref):
    n = q_ref.shape[1]
    zero_lanes = jnp.zeros((1, LANES), F32)
    zero_row = jnp.zeros((1, n), F32)
    dirs = ((True, 0), (False, MLSTM_HEADS))
    setup = [_chunk_setup(*_gate_prep(g_ref[sq]), zero_lanes) for sq in range(q_ref.shape[0])]
    units = [(sq, hd) for sq in range(q_ref.shape[0]) for hd in range(MLSTM_HEADS)]
    ks = [k_ref[sq, :, hd * MLSTM_DK:(hd + 1) * MLSTM_DK] for sq, hd in units]
    s0ts = [_dot_nt(k, q_ref[sq, :, hd * MLSTM_DK:(hd + 1) * MLSTM_DK]) for k, (sq, hd) in zip(ks, units)]
    vts = [v_ref[sq, :, hd * MLSTM_DV:(hd + 1) * MLSTM_DV].T for sq, hd in units]
    wts = [[_dir_weights(s0t, _lane_bcast(setup[sq][0], off + hd, n), zero_row, fwd=fwd) for fwd, off in dirs]
           for s0t, (sq, hd) in zip(s0ts, units)]
    hts = [[_dot(vt, sdt) for sdt, _, _ in wt] for vt, wt in zip(vts, wts)]
    for (sq, hd), k, vt, wt, ht2 in zip(units, ks, vts, wts, hts):
        _, bt2, w_t, _, _ = setup[sq]
        ht_ref[sq, hd * MLSTM_DV:(hd + 1) * MLSTM_DV, :] = sum(
            _dir_finish(ht_d, den, g, bt2[off + hd:off + hd + 1, :], zero_row)
            for ht_d, (_, den, g), (_, off) in zip(ht2, wt, dirs))
        for _, off in dirs:
            c_ref[sq, off + hd], n_ref[sq, off + hd:off + hd + 1, :] = _state_update(vt, k, w_t, off + hd)
    for sq in range(q_ref.shape[0]):
        m_ref[sq] = setup[sq][3]


CTX_SEQS_PER_STEP = 2


def _mlstm_ctx(mq, mk, mv, gates):
    b, s, _ = mq.shape
    ns = CTX_SEQS_PER_STEP
    seq = lambda c: pl.BlockSpec((ns, s, c), lambda i: (i, 0, 0))
    return pl.pallas_call(
        _mlstm_ctx_kernel,
        grid=(b // ns,),
        in_specs=[seq(MLSTM_HEADS * MLSTM_DK), seq(MLSTM_HEADS * MLSTM_DK), seq(MLSTM_HEADS * MLSTM_DV), seq(LANES)],
        out_specs=[pl.BlockSpec((ns, MLSTM_HEADS * MLSTM_DV, s), lambda i: (i, 0, 0)),
                   pl.BlockSpec((ns, N_DIRHEAD, MLSTM_DV, MLSTM_DK), lambda i: (i, 0, 0, 0)),
                   pl.BlockSpec((ns, N_DIRHEAD, MLSTM_DK), lambda i: (i, 0, 0)),
                   pl.BlockSpec((ns, 1, LANES), lambda i: (i, 0, 0))],
        out_shape=[jax.ShapeDtypeStruct((b, MLSTM_HEADS * MLSTM_DV, s), F32),
                   jax.ShapeDtypeStruct((b, N_DIRHEAD, MLSTM_DV, MLSTM_DK), F32),
                   jax.ShapeDtypeStruct((b, N_DIRHEAD, MLSTM_DK), F32),
                   jax.ShapeDtypeStruct((b, 1, LANES), F32)],
        compiler_params=_params(("parallel",)),
        name="mlstm_ctx",
    )(mq, mk, mv, gates)


def _mlstm_lat_kernel(qf_ref, kf_ref, vf_ref, gf_ref, qb_ref, kb_ref, vb_ref, gb_ref, c0_ref, n0_ref, m0_ref,
                      ht_ref, c_ref, n_ref, m_ref, *, n_chunks):
    step = pl.program_id(1)

    @pl.when(step == 0)
    def _():
        c_ref[...] = c0_ref[...]
        n_ref[...] = n0_ref[...]
        m_ref[...] = m0_ref[...]
        ht_ref[...] = jnp.zeros_like(ht_ref)

    n = qf_ref.shape[0]
    gate_f, gate_b = _gate_prep(gf_ref[...]), _gate_prep(gb_ref[...])
    is_f = lax.broadcasted_iota(jnp.int32, (n, LANES), 1) < MLSTM_HEADS
    b, u, tot = (jnp.where(is_f[:x.shape[0]], x, y) for x, y in zip(gate_f, gate_b))
    m_prev = m_ref[...]
    u2, bt2, w_t, m_new, decay = _chunk_setup(b, u, tot, m_prev)
    m2_prev = m_prev * LOG2E
    units = [(fwd, hd, hd if fwd else MLSTM_HEADS + hd) for fwd in (True, False) for hd in range(MLSTM_HEADS)]
    refs = {True: (qf_ref, kf_ref, vf_ref), False: (qb_ref, kb_ref, vb_ref)}
    qs = [refs[fwd][0][:, hd * MLSTM_DK:(hd + 1) * MLSTM_DK] for fwd, hd, _ in units]
    ks = [refs[fwd][1][:, hd * MLSTM_DK:(hd + 1) * MLSTM_DK] for fwd, hd, _ in units]
    vts = [refs[fwd][2][:, hd * MLSTM_DV:(hd + 1) * MLSTM_DV].T for fwd, hd, _ in units]
    s0ts = [_dot_nt(k, q) for k, q in zip(ks, qs)]
    c_prevs = [c_ref[kk] for _, _, kk in units]
    n_prevs = [n_ref[kk:kk + 1, :] for _, _, kk in units]
    inters = [(_dot_nt(c.astype(BF16), q), _dot_nt(jnp.broadcast_to(nv, (8, MLSTM_DK)).astype(BF16), q)[0:1, :])
              for c, nv, q in zip(c_prevs, n_prevs, qs)]
    m_rows = [_lane_bcast(m2_prev, kk, n) for _, _, kk in units]
    wts = [_dir_weights(s0t, _lane_bcast(u2, kk, n), m_row, fwd=fwd)
           for s0t, m_row, (fwd, _, kk) in zip(s0ts, m_rows, units)]
    hts = [_dot(vt, sdt) for vt, (sdt, _, _) in zip(vts, wts)]
    for (fwd, hd, kk), k, vt, (_, den, g), ht, inter, m_row, c_prev, n_prev in zip(
            units, ks, vts, wts, hts, inters, m_rows, c_prevs, n_prevs):
        chunk = step if fwd else n_chunks - 1 - step
        ht_ref[chunk, hd * MLSTM_DV:(hd + 1) * MLSTM_DV, :] += _dir_finish(ht, den, g, bt2[kk:kk + 1, :], m_row, inter)
        c_new, n_new = _state_update(vt, k, w_t, kk)
        dk = _lane_bcast(decay, kk)
        c_ref[kk] = dk * c_prev + c_new
        n_ref[kk:kk + 1, :] = dk * n_prev + n_new
    m_ref[...] = m_new


def _mlstm_lat(mq, mk, mv, gates, c0, n0, m0):
    b, s, _ = mq.shape
    lc = MLSTM_CHUNK
    nc = s // lc
    fw = lambda c: pl.BlockSpec((None, lc, c), lambda i, j: (i, j, 0))
    bw = lambda c: pl.BlockSpec((None, lc, c), lambda i, j: (i, nc - 1 - j, 0))
    dk, dv = MLSTM_HEADS * MLSTM_DK, MLSTM_HEADS * MLSTM_DV
    return pl.pallas_call(
        functools.partial(_mlstm_lat_kernel, n_chunks=nc),
        grid=(b, nc),
        in_specs=[fw(dk), fw(dk), fw(dv), fw(LANES), bw(dk), bw(dk), bw(dv), bw(LANES),
                  pl.BlockSpec((None, N_DIRHEAD, MLSTM_DV, MLSTM_DK), lambda i, j: (i, 0, 0, 0)),
                  pl.BlockSpec((None, N_DIRHEAD, MLSTM_DK), lambda i, j: (i, 0, 0)),
                  pl.BlockSpec((None, 1, LANES), lambda i, j: (i, 0, 0))],
        out_specs=pl.BlockSpec((nc, dv, lc), lambda i, j: (i, 0, 0)),
        out_shape=jax.ShapeDtypeStruct((b * nc, dv, lc), F32),
        scratch_shapes=[pltpu.VMEM((N_DIRHEAD, MLSTM_DV, MLSTM_DK), F32),
                        pltpu.VMEM((N_DIRHEAD, MLSTM_DK), F32),
                        pltpu.VMEM((1, LANES), F32)],
        compiler_params=_params(("parallel", "arbitrary")),
        name="mlstm_lat",
    )(mq, mk, mv, gates, mq, mk, mv, gates, c0, n0, m0)


def _tail_kernel(x_ref, mod_ref, att_ref, hst_ref, smo_ref, sbr_ref, gml_ref, womla_ref, womlstm_ref, wout_ref,
                 gffn_ref, wfin_ref, wfout_ref, gfin_ref, y_ref):
    mod = mod_ref[...]
    gml = gml_ref[...]
    tm = x_ref.shape[0]
    groups = [slice(r, r + TAIL_ROWS) for r in range(0, tm, TAIL_ROWS)]
    a = [_dot(att_ref[g, :], womla_ref[...]) for g in groups]
    hm = []
    for gi, g in enumerate(groups):
        hs = hst_ref[gi].T
        parts = []
        for hd in range(MLSTM_HEADS):
            sl = slice(hd * MLSTM_DV, (hd + 1) * MLSTM_DV)
            parts.append((_rms(hs[:, sl], gml[:, sl]) * smo_ref[g, sl].astype(F32)).astype(BF16))
        hm.append(jnp.concatenate(parts, axis=1))
    bm = [_dot(h, womlstm_ref[...]) for h in hm]
    merged = [(sbr_ref[g, :D_MODEL].astype(F32) * ai + sbr_ref[g, D_MODEL:].astype(F32) * bi).astype(BF16)
              for g, ai, bi in zip(groups, a, bm)]
    x1 = [x_ref[g, :] + mod[0:1] * _dot(mi, wout_ref[...]) for g, mi in zip(groups, merged)]
    h2 = [(_rms(xi, gffn_ref[...]) * (1.0 + mod[2:3]) + mod[1:2]).astype(BF16) for xi in x1]
    ga = [_dot(hi, wfin_ref[:, :FFN_HIDDEN]) for hi in h2]
    gu = [_dot(hi, wfin_ref[:, FFN_HIDDEN:]) for hi in h2]
    act = [(gi * _sigmoid(gi) * ui).astype(BF16) for gi, ui in zip(ga, gu)]
    for g, xi, ci in zip(groups, x1, act):
        y_ref[g, :] = _rms(xi + mod[3:4] * _dot(ci, wfout_ref[...]), gfin_ref[...])


def _tail(x, mod, att, hst, smo, sbr, gml, w_o_mla, w_o_mlstm, w_out, gffn, w_ffn_in, w_ffn_out, gfin, *, seq_len, name):
    n_tok = x.shape[0]
    tm = TOKEN_TILE
    tiles_per_seq = seq_len // tm
    assert hst.shape[0] * TAIL_ROWS == n_tok and hst.shape[2] == TAIL_ROWS
    tok = lambda c: pl.BlockSpec((tm, c), lambda i: (i, 0))
    consts = (gml, w_o_mla, w_o_mlstm, w_out, gffn, w_ffn_in, w_ffn_out, gfin)
    return pl.pallas_call(
        _tail_kernel,
        grid=(n_tok // tm,),
        in_specs=[tok(D_MODEL), _mod_spec(mod, tiles_per_seq), tok(att.shape[1]),
                  pl.BlockSpec((tm // TAIL_ROWS,) + hst.shape[1:], lambda i: (i, 0, 0)),
                  tok(smo.shape[1]), tok(sbr.shape[1])]
                 + [_const_spec(a.shape) for a in consts],
        out_specs=tok(D_MODEL),
        out_shape=jax.ShapeDtypeStruct((n_tok, D_MODEL), F32),
        compiler_params=_params(("parallel",)),
        name=name,
    )(x, mod, att, hst, smo, sbr, *consts)


_IN_SIZES = (Q_RANK, KV_RANK, ROPE_DIM, MLSTM_HEADS * MLSTM_DK, MLSTM_HEADS * MLSTM_DK, MLSTM_HEADS * MLSTM_DV,
             4 * MLSTM_HEADS, MLSTM_HEADS * MLSTM_DV, 2 * D_MODEL)
_IN_OFF = tuple(sum(_IN_SIZES[:i]) for i in range(len(_IN_SIZES) + 1))
PACK_ROWS = LANES
_PAD_AFTER = tuple((_SEG_OFF[i + 1] // PACK_ROWS, _SEG_WIDTHS[i] - _IN_SIZES[i], _IN_SIZES[i] % PACK_ROWS)
                   for i in range(len(_IN_SIZES)) if _SEG_WIDTHS[i] != _IN_SIZES[i])


PACK_PIECES = 8
N_PACK_STEPS = pl.cdiv(N_IN_PACKED // PACK_ROWS, PACK_PIECES)
_N_IN = _IN_OFF[-1]


def _pack_src_row(p):
    src = p * PACK_ROWS
    for first_piece_after, pad, _ in _PAD_AFTER:
        src = src - jnp.where(p >= first_piece_after, pad, 0)
    src = jnp.minimum(src, _N_IN - PACK_ROWS)
    return pl.multiple_of(src, math.gcd(PACK_ROWS, _N_IN, *(pad for _, pad, _ in _PAD_AFTER)))


def _pack_in_kernel(*refs):
    w_refs, o_ref = refs[:PACK_PIECES], refs[PACK_PIECES]
    row = lax.broadcasted_iota(jnp.int32, w_refs[0].shape, 0)
    for j, w_ref in enumerate(w_refs):
        p = pl.program_id(0) * PACK_PIECES + j
        valid = jnp.where(p < N_IN_PACKED // PACK_ROWS, PACK_ROWS, 0)
        for first_piece_after, _, width in _PAD_AFTER:
            valid = jnp.where(p == first_piece_after - 1, width, valid)
        o_ref[j * PACK_ROWS:(j + 1) * PACK_ROWS, :] = jnp.where(row < valid, w_ref[...], 0.0).astype(BF16)


def _pack_in(w_in_t):
    n, k = w_in_t.shape
    piece = lambda j: pl.BlockSpec((pl.Element(PACK_ROWS), pl.Element(k)),
                                   lambda i: (_pack_src_row(i * PACK_PIECES + j), 0))
    return pl.pallas_call(
        _pack_in_kernel,
        grid=(N_PACK_STEPS,),
        in_specs=[piece(j) for j in range(PACK_PIECES)],
        out_specs=pl.BlockSpec((PACK_PIECES * PACK_ROWS, k), lambda i: (i, 0)),
        out_shape=jax.ShapeDtypeStruct((N_PACK_STEPS * PACK_PIECES * PACK_ROWS, k), BF16),
        compiler_params=_params(("parallel",)),
        name="pack_in",
    )(*([w_in_t] * PACK_PIECES))


def _pack_small(w_uq, w_ukv, b_gates):
    qd = NOPE_DIM + ROPE_DIM
    w_uq_p = jnp.pad(w_uq.reshape(Q_RANK, MLA_HEADS, qd), ((0, 0), (0, 0), (0, HEAD_BLOCK - qd)))
    w_uq_p = w_uq_p.reshape(Q_RANK, MLA_HEADS * HEAD_BLOCK).astype(BF16)
    kv = w_ukv.reshape(KV_RANK, MLA_HEADS, NOPE_DIM + MLA_V_DIM)
    wk = jnp.pad(kv[:, :, :NOPE_DIM], ((0, 0), (0, 0), (0, HEAD_BLOCK - NOPE_DIM))).reshape(KV_RANK, -1)
    wv = kv[:, :, NOPE_DIM:].reshape(KV_RANK, -1)
    w_ukv_p = jnp.concatenate([wk, wv], axis=1).astype(BF16)
    bg = jnp.pad(b_gates, ((0, 0), (0, LANES - b_gates.shape[1])))
    return w_uq_p, w_ukv_p, bg


def _rope_tables(n_tokens):
    pos = np.arange(n_tokens)
    row = (pos // GRID_W).astype(np.float64)[:, None]
    col = (pos % GRID_W).astype(np.float64)[:, None]
    half = ROPE_DIM // 2
    inv = (np.float32(ROPE_BASE) ** (-np.arange(0, half, 2, dtype=np.float32) / np.float32(half))).astype(np.float64)
    r = np.arange(LANES) - ROPE_LANE0
    in_rope = (r >= 0) & (r < ROPE_DIM)
    rr = np.clip(r, 0, ROPE_DIM - 1)
    freq = inv[rr % (half // 2)][None, :]
    ang = np.where((rr // half == 0)[None, :], row * freq, col * freq).astype(np.float32).astype(np.float64)
    first = (rr % half) < (half // 2)
    cos = np.where(in_rope[None, :], np.cos(ang), 1.0)
    sin = np.sin(ang)
    sin_lo = np.where((in_rope & first)[None, :], -sin, 0.0)
    sin_hi = np.where((in_rope & ~first)[None, :], sin, 0.0)
    return tuple(jnp.asarray(t, dtype=F32) for t in (cos, sin_lo, sin_hi))


def kernel(x_prompt, x_sample, cache_ckv, cache_krope, state_C, state_n, state_m, c, c_ctx, w_mod, b_mod, g_norm_mix,
           w_in, b_gates, g_q_norm, w_uq, g_kv_norm, w_ukv, g_mlstm_norm, w_o_mla, w_o_mlstm, w_out, g_norm_ffn,
           w_ffn_in, w_ffn_out, g_final):
    bp, sp, d = x_prompt.shape
    bs, ss, _ = x_sample.shape
    layer = 0
    assert w_mod.shape[0] == 1 and sp == MLSTM_CHUNK and ss % MLSTM_CHUNK == 0

    cond = jnp.concatenate([c_ctx[None, :], c, jnp.zeros((8 - 1 - bs, d), F32)], axis=0)
    adaln_args = (cond, w_mod[layer], b_mod[layer][None, :])
    mod_in = _adaln(*adaln_args).reshape(8, N_MOD_MIXER, d)
    mod_ctx, mod_lat = mod_in[0:1], mod_in[1:1 + bs]

    w_in_p = _pack_in(w_in[layer].T)
    w_uq_p, w_ukv_p, bg = _pack_small(w_uq[layer], w_ukv[layer], b_gates[layer][None, :])
    row = lambda g: g[layer][None, :]
    shared_in = (row(g_norm_mix), w_in_p, row(g_q_norm), w_uq_p, row(g_kv_norm), w_ukv_p, bg)
    seq3 = lambda a, b_, s_: a.reshape(b_, s_, a.shape[-1])

    xl = x_sample.reshape(bs * ss, d)
    q, kc, v, mq_l, mk_l, mv_l, gates_l, smo_l, sbr_l = _inproj(xl, mod_lat, _rope_tables(ss), *shared_in, seq_len=ss)
    krope_blk = jnp.pad(cache_krope[:, layer], ((0, 0), (0, 0), (ROPE_LANE0, LANES - ROPE_LANE0 - ROPE_DIM)))
    cache = _cachekv(cache_ckv[:, layer], krope_blk, w_ukv_p)
    att_l, womla_b, womlstm_b, wout_b, wfin_b, wfout_b, mod_tail = _attention(
        seq3(q, bs, ss), seq3(kc, bs, ss), seq3(v, bs, ss), cache,
        cast_weights=(w_o_mla[layer], w_o_mlstm[layer], w_out[layer], w_ffn_in[layer], w_ffn_out[layer]),
        adaln_tail=adaln_args)
    tail_w = (row(g_mlstm_norm), womla_b, womlstm_b, wout_b, row(g_norm_ffn), wfin_b, wfout_b, g_final[None, :])
    mod_tail = mod_tail.reshape(8, N_MOD_TAIL, d)
    modt_ctx, modt_lat = mod_tail[0:1], mod_tail[1:1 + bs]

    xc = x_prompt.reshape(bp * sp, d)
    q, kc, v, mq, mk, mv, gates, smo, sbr, ckv, kpe = _inproj(xc, mod_ctx, None, *shared_in, seq_len=sp)
    att = _attention(seq3(q, bp, sp), seq3(kc, bp, sp), seq3(v, bp, sp))
    hs, new_c, new_n, new_m = _mlstm_ctx(seq3(mq, bp, sp), seq3(mk, bp, sp), seq3(mv, bp, sp), seq3(gates, bp, sp))
    y_prompt = _tail(xc, modt_ctx, att.reshape(bp * sp, -1), hs, smo, sbr, *tail_w,
                     seq_len=sp, name="tail_ctx").reshape(bp, sp, d)
    new_ckv = ckv.reshape(bp, 1, sp, KV_RANK)
    new_krope = kpe.reshape(bp, 1, sp, ROPE_DIM)
    new_C = new_c.reshape(bp, 1, 2, MLSTM_HEADS, MLSTM_DV, MLSTM_DK)
    new_N = new_n.reshape(bp, 1, 2, MLSTM_HEADS, MLSTM_DK)
    new_M = new_m[:, 0, :N_DIRHEAD].reshape(bp, 1, 2, MLSTM_HEADS)

    c0 = state_C[:, layer].reshape(bs, N_DIRHEAD, MLSTM_DV, MLSTM_DK)
    n0 = state_n[:, layer].reshape(bs, N_DIRHEAD, MLSTM_DK)
    m0 = jnp.pad(state_m[:, layer].reshape(bs, 1, N_DIRHEAD), ((0, 0), (0, 0), (0, LANES - N_DIRHEAD)))
    hs = _mlstm_lat(seq3(mq_l, bs, ss), seq3(mk_l, bs, ss), seq3(mv_l, bs, ss), seq3(gates_l, bs, ss), c0, n0, m0)
    y_sample = _tail(xl, modt_lat, att_l.reshape(bs * ss, -1), hs, smo_l, sbr_l, *tail_w,
                     seq_len=ss, name="tail_lat").reshape(bs, ss, d)
    return (y_prompt, y_sample, new_ckv, new_krope, new_C, new_N, new_M)
```

```python
import functools
import math

import jax
import jax.numpy as jnp
import numpy as np
from jax import lax
from jax.experimental import pallas as pl
from jax.experimental.pallas import tpu as pltpu

F32 = jnp.float32
BF16 = jnp.bfloat16

D_MODEL = 1024
GRID_W = 64
MLA_HEADS = 8
Q_RANK = 384
KV_RANK = 256
NOPE_DIM = 64
ROPE_DIM = 32
MLA_V_DIM = 64
ROPE_BASE = 10000.0
MLA_SCALE = (NOPE_DIM + ROPE_DIM) ** -0.5
MLSTM_HEADS = 4
MLSTM_DK = 128
MLSTM_DV = 256
FFN_HIDDEN = ((8 * D_MODEL // 3 + 255) // 256) * 256
EPS = 1e-6

LANES = 128
HEAD_BLOCK = LANES
ROPE_LANE0 = NOPE_DIM
N_DIRHEAD = 2 * MLSTM_HEADS
MLSTM_CHUNK = 256
LOG2E = math.log2(math.e)
Q_PRESCALE = MLA_SCALE * LOG2E
V_SLAB = 2 * LANES
V_WIDTH = (MLA_HEADS // 2) * V_SLAB
TOKEN_TILE = 512
TAIL_ROWS = 256
INPROJ_ROWS = TOKEN_TILE
Q_TILE = 256
ATTN_GROUP = 8
VMEM_LIMIT = 56 * 1024 * 1024

_SEG_WIDTHS = (Q_RANK, KV_RANK, LANES, MLSTM_HEADS * MLSTM_DK, MLSTM_HEADS * MLSTM_DK,
               MLSTM_HEADS * MLSTM_DV, LANES, MLSTM_HEADS * MLSTM_DV, 2 * D_MODEL)
_SEG_OFF = tuple(sum(_SEG_WIDTHS[:i]) for i in range(len(_SEG_WIDTHS) + 1))
SEG_Q, SEG_KV, SEG_KPE, SEG_MQ, SEG_MK, SEG_MV, SEG_GATE, SEG_MO, SEG_BR = (
    (_SEG_OFF[i], _SEG_OFF[i + 1]) for i in range(9))
N_IN_PACKED = _SEG_OFF[-1]


def _dot(a, b):
    return jnp.dot(a, b, preferred_element_type=F32)


def _dot_nt(a, b):
    return lax.dot_general(a, b, (((1,), (1,)), ((), ())), preferred_element_type=F32)


def _dot_tn(a, b):
    return lax.dot_general(a, b, (((0,), (0,)), ((), ())), preferred_element_type=F32)


def _rms(x, g):
    ms = jnp.mean(x * x, axis=-1, keepdims=True)
    return x * lax.rsqrt(ms + EPS) * g


def _sigmoid(x):
    return 1.0 / (1.0 + jnp.exp(-x))


def _const_spec(shape):
    nd = len(shape)
    return pl.BlockSpec(shape, lambda *_: (0,) * nd, pipeline_mode=pl.Buffered(1))


def _mod_spec(mod, tiles_per_seq):
    if mod.shape[0] == 1:
        return pl.BlockSpec((None,) + mod.shape[1:], lambda i: (0, 0, 0))
    return pl.BlockSpec((None,) + mod.shape[1:], lambda i: (i // tiles_per_seq, 0, 0))


def _params(sem, flags=None, vmem_mib=None):
    limit = VMEM_LIMIT if vmem_mib is None else vmem_mib * 1024 * 1024
    return pltpu.CompilerParams(dimension_semantics=sem, vmem_limit_bytes=limit, flags=flags)


N_MOD_MIXER = 2
N_MOD_TAIL = 4


def _adaln_block(c_ref, w_ref, b_ref, o_ref):
    c = c_ref[...]
    s = c * _sigmoid(c)
    o_ref[...] = _dot(s.astype(BF16), w_ref[...].astype(BF16)) + b_ref[...]


def _adaln(cond8, w_mod, b_mod):
    tn = D_MODEL
    return pl.pallas_call(
        _adaln_block,
        grid=(N_MOD_MIXER,),
        in_specs=[pl.BlockSpec((8, D_MODEL), lambda j: (0, 0)),
                  pl.BlockSpec((D_MODEL, tn), lambda j: (0, j)),
                  pl.BlockSpec((1, tn), lambda j: (0, j))],
        out_specs=pl.BlockSpec((8, tn), lambda j: (0, j)),
        out_shape=jax.ShapeDtypeStruct((8, N_MOD_MIXER * D_MODEL), F32),
        compiler_params=_params(("arbitrary",), vmem_mib=24),
        name="adaln",
    )(cond8, w_mod, b_mod)


def _rope_block(x, cos, sin_lo, sin_hi):
    return x * cos + pltpu.roll(x, LANES - 8, 1) * sin_lo + pltpu.roll(x, 8, 1) * sin_hi


def _store_kv(kvn, kp, kc_ref, v_ref, g=slice(None)):
    for hd in range(MLA_HEADS):
        sl = slice(hd * HEAD_BLOCK, (hd + 1) * HEAD_BLOCK)
        kc_ref[g, sl] = (kvn[:, sl] + kp).astype(BF16)
    v0 = MLA_HEADS * HEAD_BLOCK
    ones = jnp.ones((kvn.shape[0], LANES), BF16)
    for pair in range(MLA_HEADS // 2):
        v_ref[g, pair * V_SLAB:pair * V_SLAB + LANES] = kvn[:, v0 + pair * LANES:v0 + (pair + 1) * LANES].astype(BF16)
        v_ref[g, pair * V_SLAB + LANES:(pair + 1) * V_SLAB] = ones


def _inproj_kernel(*refs, rope):
    if rope:
        (x_ref, mod_ref, cos_ref, slo_ref, shi_ref, gmix_ref, win_ref, gq_ref, wuq_ref, gkv_ref, wukv_ref,
         bg_ref, q_ref, kc_ref, v_ref, mq_ref, mk_ref, mv_ref, gate_ref, smo_ref, sbr_ref) = refs
    else:
        (x_ref, mod_ref, gmix_ref, win_ref, gq_ref, wuq_ref, gkv_ref, wukv_ref,
         bg_ref, q_ref, kc_ref, v_ref, mq_ref, mk_ref, mv_ref, gate_ref, smo_ref, sbr_ref,
         ckv_ref, kpe_ref) = refs
    mod = mod_ref[...]
    for r0 in range(0, x_ref.shape[0], INPROJ_ROWS):
        g = slice(r0, r0 + INPROJ_ROWS)
        if rope:
            cos, slo, shi = cos_ref[g, :], slo_ref[g, :], shi_ref[g, :]
        h = _rms(x_ref[g, :], gmix_ref[...]) * (1.0 + mod[1:2]) + mod[0:1]
        hb = h.astype(BF16)

        def proj(seg):
            return _dot_nt(hb, win_ref[seg[0]:seg[1], :])

        qn = _rms(proj(SEG_Q), gq_ref[...]).astype(BF16)
        ckv = _rms(proj(SEG_KV), gkv_ref[...])
        zkpe = proj(SEG_KPE)
        sbr_ref[g, :] = _sigmoid(proj(SEG_BR)).astype(BF16)
        smo_ref[g, :] = _sigmoid(proj(SEG_MO)).astype(BF16)
        gate_ref[g, :] = proj(SEG_GATE) + bg_ref[...]

        q = _dot(qn, wuq_ref[...])
        for hd in range(MLA_HEADS):
            sl = slice(hd * HEAD_BLOCK, (hd + 1) * HEAD_BLOCK)
            qh = q[:, sl]
            if rope:
                qh = _rope_block(qh, cos, slo, shi)
            q_ref[g, sl] = (qh * Q_PRESCALE).astype(BF16)

        if not rope:
            ckv_ref[g, :] = ckv
            kpe_ref[g, :] = zkpe[:, :ROPE_DIM]
        kp = pltpu.roll(zkpe, ROPE_LANE0, 1)
        if rope:
            kp = _rope_block(kp, cos, slo, shi)
        kvn = _dot(ckv.astype(BF16), wukv_ref[...])
        _store_kv(kvn, kp, kc_ref, v_ref, g)

        mv_ref[g, :] = proj(SEG_MV).astype(BF16)
        mk_ref[g, :] = (proj(SEG_MK) * (MLSTM_DK ** -0.5)).astype(BF16)
        mq_ref[g, :] = proj(SEG_MQ).astype(BF16)


def _inproj(x, mod, rope_tabs, gmix, w_in_p, gq, w_uq_p, gkv, w_ukv_p, bg, *, seq_len):
    n_tok = x.shape[0]
    tm = TOKEN_TILE
    tiles_per_seq = seq_len // tm
    rope = rope_tabs is not None
    tok = lambda c: pl.BlockSpec((tm, c), lambda i: (i, 0))
    in_specs = [tok(D_MODEL), _mod_spec(mod, tiles_per_seq)]
    args = [x, mod]
    if rope:
        in_specs += [pl.BlockSpec((tm, LANES), lambda i: (i % tiles_per_seq, 0))] * 3
        args += list(rope_tabs)
    in_specs += [_const_spec(a.shape) for a in (gmix, w_in_p, gq, w_uq_p, gkv, w_ukv_p, bg)]
    args += [gmix, w_in_p, gq, w_uq_p, gkv, w_ukv_p, bg]
    widths = [(MLA_HEADS * HEAD_BLOCK, BF16), (MLA_HEADS * HEAD_BLOCK, BF16), (V_WIDTH, BF16),
              (MLSTM_HEADS * MLSTM_DK, BF16), (MLSTM_HEADS * MLSTM_DK, BF16), (MLSTM_HEADS * MLSTM_DV, BF16),
              (LANES, F32), (MLSTM_HEADS * MLSTM_DV, BF16), (2 * D_MODEL, BF16)]
    if not rope:
        widths += [(KV_RANK, F32), (ROPE_DIM, F32)]
    return pl.pallas_call(
        functools.partial(_inproj_kernel, rope=rope),
        grid=(n_tok // tm,),
        in_specs=in_specs,
        out_specs=[tok(c) for c, _ in widths],
        out_shape=[jax.ShapeDtypeStruct((n_tok, c), dt) for c, dt in widths],
        compiler_params=_params(("parallel",), vmem_mib=44),
        name="inproj_lat" if rope else "inproj_ctx",
    )(*args)


def _cachekv_kernel(ckv_ref, kp_ref, wukv_ref, kc_ref, v_ref):
    kvn = _dot(ckv_ref[...].astype(BF16), wukv_ref[...])
    _store_kv(kvn, kp_ref[...], kc_ref, v_ref)


def _cachekv(cache_ckv, krope_blk, w_ukv_p):
    b, past, _ = cache_ckv.shape
    return pl.pallas_call(
        _cachekv_kernel,
        grid=(b,),
        in_specs=[pl.BlockSpec((None, past, KV_RANK), lambda i: (i, 0, 0)),
                  pl.BlockSpec((None, past, LANES), lambda i: (i, 0, 0)),
                  _const_spec(w_ukv_p.shape)],
        out_specs=[pl.BlockSpec((None, past, MLA_HEADS * HEAD_BLOCK), lambda i: (i, 0, 0)),
                   pl.BlockSpec((None, past, V_WIDTH), lambda i: (i, 0, 0))],
        out_shape=[jax.ShapeDtypeStruct((b, past, MLA_HEADS * HEAD_BLOCK), BF16),
                   jax.ShapeDtypeStruct((b, past, V_WIDTH), BF16)],
        compiler_params=_params(("parallel",), vmem_mib=16),
        name="cachekv",
    )(cache_ckv, krope_blk, w_ukv_p)


def _attn_kernel(*refs, has_cache, n_cast, has_adaln):
    n_in = 5 if has_cache else 3
    if has_cache:
        q_ref, k_ref, v_ref, kc_ref, vc_ref = refs[:n_in]
    else:
        q_ref, k_ref, v_ref = refs[:n_in]
    n_all_in = n_in + n_cast + (3 if has_adaln else 0)
    o_ref = refs[n_all_in]
    for w_ref, wb_ref in zip(refs[n_in:n_in + n_cast], refs[n_all_in + 1:n_all_in + 1 + n_cast]):
        wb_ref[...] = w_ref[...].astype(BF16)
    if has_adaln:
        _adaln_block(*refs[n_in + n_cast:n_all_in], refs[n_all_in + 1 + n_cast])
    tq = q_ref.shape[0]
    lane = lax.broadcasted_iota(jnp.int32, (tq, LANES), 1)
    for h0 in range(0, MLA_HEADS, ATTN_GROUP):
        heads = range(h0, h0 + ATTN_GROUP)
        hsl = [slice(hd * HEAD_BLOCK, (hd + 1) * HEAD_BLOCK) for hd in heads]
        vsl = [slice(hd // 2 * V_SLAB, (hd // 2 + 1) * V_SLAB) for hd in heads]
        s = [_dot_nt(q_ref[:, sl], k_ref[:, sl]) for sl in hsl]
        m = [jnp.max(si, axis=-1, keepdims=True) for si in s]
        if has_cache:
            sc = [_dot_nt(q_ref[:, sl], kc_ref[:, sl]) for sl in hsl]
            m = [jnp.maximum(mi, jnp.max(ci, axis=-1, keepdims=True)) for mi, ci in zip(m, sc)]
        p = [jnp.exp2(si - mi).astype(BF16) for si, mi in zip(s, m)]
        o = [_dot(pi, v_ref[:, sl]) for pi, sl in zip(p, vsl)]
        if has_cache:
            pc = [jnp.exp2(ci - mi).astype(BF16) for ci, mi in zip(sc, m)]
            o = [oi + _dot(pi, vc_ref[:, sl]) for oi, pi, sl in zip(o, pc, vsl)]
        outs = [oi[:, :LANES] * (1.0 / oi[:, LANES:]) for oi in o]
        for j in range(0, ATTN_GROUP, 2):
            pair = (h0 + j) // 2
            o_ref[:, pair * LANES:(pair + 1) * LANES] = jnp.where(lane < MLA_V_DIM, outs[j], outs[j + 1]).astype(BF16)


def _attention(q, k, v, cache=None, cast_weights=(), adaln_tail=None):
    b, s, _ = q.shape
    tq = Q_TILE
    n_steps = b * (s // tq)
    kw, vw, ow = MLA_HEADS * HEAD_BLOCK, V_WIDTH, MLA_HEADS * MLA_V_DIM
    in_specs = [pl.BlockSpec((None, tq, kw), lambda i, j: (i, j, 0)),
                pl.BlockSpec((None, s, kw), lambda i, j: (i, 0, 0)),
                pl.BlockSpec((None, s, vw), lambda i, j: (i, 0, 0))]
    args = [q, k, v]
    if cache is not None:
        past = cache[0].shape[1]
        in_specs += [pl.BlockSpec((None, past, kw), lambda i, j: (i, 0, 0)),
                     pl.BlockSpec((None, past, vw), lambda i, j: (i, 0, 0))]
        args += list(cache)
    step = lambda i, j: i * (s // tq) + j
    w_specs = [pl.BlockSpec((w.shape[0] // n_steps, w.shape[1]), lambda i, j: (step(i, j), 0)) for w in cast_weights]
    extra_in, extra_out, extra_shape = [], [], []
    if adaln_tail is not None:
        tn = N_MOD_TAIL * D_MODEL // n_steps
        col0 = N_MOD_MIXER * D_MODEL // tn
        extra_in = [pl.BlockSpec((8, D_MODEL), lambda i, j: (0, 0)),
                    pl.BlockSpec((D_MODEL, tn), lambda i, j: (0, col0 + step(i, j))),
                    pl.BlockSpec((1, tn), lambda i, j: (0, col0 + step(i, j)))]
        extra_out = [pl.BlockSpec((8, tn), lambda i, j: (0, step(i, j)))]
        extra_shape = [jax.ShapeDtypeStruct((8, N_MOD_TAIL * D_MODEL), F32)]
    outs = pl.pallas_call(
        functools.partial(_attn_kernel, has_cache=cache is not None, n_cast=len(cast_weights),
                          has_adaln=adaln_tail is not None),
        grid=(b, s // tq),
        in_specs=in_specs + w_specs + extra_in,
        out_specs=[pl.BlockSpec((None, tq, ow), lambda i, j: (i, j, 0))] + w_specs + extra_out,
        out_shape=[jax.ShapeDtypeStruct((b, s, ow), BF16)]
                  + [jax.ShapeDtypeStruct(w.shape, BF16) for w in cast_weights] + extra_shape,
        compiler_params=_params(("parallel", "arbitrary"), vmem_mib=52 if cache is not None else 16),
        name="attn_lat" if cache is not None else "attn_ctx",
    )(*args, *cast_weights, *(adaln_tail or ()))
    return outs[0] if len(outs) == 1 else outs


def _lane_bcast(x, k, width=None):
    y = jnp.broadcast_to(x[:, k:k + 1], x.shape)
    reps = (width or LANES) // LANES
    return y if reps == 1 else jnp.concatenate([y] * reps, axis=1)


def _prefix_sum_rows(x):
    n = x.shape[0]
    r = lax.broadcasted_iota(jnp.int32, (n, n), 0)
    c = lax.broadcasted_iota(jnp.int32, (n, n), 1)
    tri = jnp.where(c <= r, 1.0, 0.0).astype(BF16)
    hi = x.astype(BF16)
    r1 = x - hi.astype(F32)
    mid = r1.astype(BF16)
    lo = (r1 - mid.astype(F32)).astype(BF16)
    return _dot(tri, hi) + _dot(tri, mid) + _dot(tri, lo)


def _gate_prep(g):
    n = g.shape[0]
    lane = lax.broadcasted_iota(jnp.int32, (n, LANES), 1)
    fpre = pltpu.roll(g, LANES - N_DIRHEAD, 1)
    lf = jnp.minimum(fpre, 0.0) - jnp.log(1.0 + jnp.exp(-jnp.abs(fpre)))
    binc = _prefix_sum_rows(lf)
    tot = binc[n - 1:n, :]
    b = jnp.where(lane < MLSTM_HEADS, binc, tot - binc + lf)
    return b, g - b, tot


def _chunk_setup(b, u, tot, m_prev):
    g_last = jnp.maximum(jnp.max(u, axis=0, keepdims=True), m_prev)
    u2 = u * LOG2E
    w = jnp.exp2(u2 - g_last * LOG2E)
    return u2, (b * LOG2E).T, w.T, tot + g_last, jnp.exp(m_prev - g_last)


def _state_update(vt, k, wt, kk):
    c_new = _dot((vt * wt[kk:kk + 1, :]).astype(BF16), k)
    n_new = _dot(wt[:N_DIRHEAD, :].astype(BF16), k)[kk:kk + 1, :]
    return c_new, n_new


def _mlstm_dir_t(s0t, vt, u_b, bt_row, m_row, *, fwd, inter=None):
    sdt, den, g = _dir_weights(s0t, u_b, m_row, fwd=fwd)
    return _dir_finish(_dot(vt, sdt), den, g, bt_row, m_row, inter)


def _dir_weights(s0t, u_b, m_row, *, fwd):
    n = s0t.shape[0]
    r = lax.broadcasted_iota(jnp.int32, (n, n), 0)
    c = lax.broadcasted_iota(jnp.int32, (n, n), 1)
    a = jnp.where((r <= c) if fwd else (r >= c), u_b, -jnp.inf)
    g = jnp.maximum(jnp.max(a, axis=0, keepdims=True), m_row)
    sdt = s0t * jnp.exp2(a - g)
    return sdt.astype(BF16), jnp.sum(sdt, axis=0, keepdims=True), g


def _dir_finish(ht, den, g, bt_row, m_row, inter=None):
    if inter is not None:
        w_inter = jnp.exp2(m_row - g)
        ht = ht + w_inter * inter[0]
        den = den + w_inter * inter[1]
    floor = jnp.exp2(-(bt_row + g))
    return ht * (1.0 / jnp.maximum(jnp.abs(den), floor))


def _mlstm_ctx_kernel(q_ref, k_ref, v_ref, g_ref, ht_ref, c_ref, n_ref, m_ref):
    n = q_ref.shape[1]
    zero_lanes = jnp.zeros((1, LANES), F32)
    zero_row = jnp.zeros((1, n), F32)
    dirs = ((True, 0), (False, MLSTM_HEADS))
    setup = [_chunk_setup(*_gate_prep(g_ref[sq]), zero_lanes) for sq in range(q_ref.shape[0])]
    units = [(sq, hd) for sq in range(q_ref.shape[0]) for hd in range(MLSTM_HEADS)]
    ks = [k_ref[sq, :, hd * MLSTM_DK:(hd + 1) * MLSTM_DK] for sq, hd in units]
    s0ts = [_dot_nt(k, q_ref[sq, :, hd * MLSTM_DK:(hd + 1) * MLSTM_DK]) for k, (sq, hd) in zip(ks, units)]
    vts = [v_ref[sq, :, hd * MLSTM_DV:(hd + 1) * MLSTM_DV].T for sq, hd in units]
    wts = [[_dir_weights(s0t, _lane_bcast(setup[sq][0], off + hd, n), zero_row, fwd=fwd) for fwd, off in dirs]
           for s0t, (sq, hd) in zip(s0ts, units)]
    hts = [[_dot(vt, sdt) for sdt, _, _ in wt] for vt, wt in zip(vts, wts)]
    for (sq, hd), k, vt, wt, ht2 in zip(units, ks, vts, wts, hts):
        _, bt2, w_t, _, _ = setup[sq]
        ht_ref[sq, hd * MLSTM_DV:(hd + 1) * MLSTM_DV, :] = sum(
            _dir_finish(ht_d, den, g, bt2[off + hd:off + hd + 1, :], zero_row)
            for ht_d, (_, den, g), (_, off) in zip(ht2, wt, dirs))
        for _, off in dirs:
            c_ref[sq, off + hd], n_ref[sq, off + hd:off + hd + 1, :] = _state_update(vt, k, w_t, off + hd)
    for sq in range(q_ref.shape[0]):
        m_ref[sq] = setup[sq][3]


CTX_SEQS_PER_STEP = 2


def _mlstm_ctx(mq, mk, mv, gates):
    b, s, _ = mq.shape
    ns = CTX_SEQS_PER_STEP
    seq = lambda c: pl.BlockSpec((ns, s, c), lambda i: (i, 0, 0))
    return pl.pallas_call(
        _mlstm_ctx_kernel,
        grid=(b // ns,),
        in_specs=[seq(MLSTM_HEADS * MLSTM_DK), seq(MLSTM_HEADS * MLSTM_DK), seq(MLSTM_HEADS * MLSTM_DV), seq(LANES)],
        out_specs=[pl.BlockSpec((ns, MLSTM_HEADS * MLSTM_DV, s), lambda i: (i, 0, 0)),
                   pl.BlockSpec((ns, N_DIRHEAD, MLSTM_DV, MLSTM_DK), lambda i: (i, 0, 0, 0)),
                   pl.BlockSpec((ns, N_DIRHEAD, MLSTM_DK), lambda i: (i, 0, 0)),
                   pl.BlockSpec((ns, 1, LANES), lambda i: (i, 0, 0))],
        out_shape=[jax.ShapeDtypeStruct((b, MLSTM_HEADS * MLSTM_DV, s), F32),
                   jax.ShapeDtypeStruct((b, N_DIRHEAD, MLSTM_DV, MLSTM_DK), F32),
                   jax.ShapeDtypeStruct((b, N_DIRHEAD, MLSTM_DK), F32),
                   jax.ShapeDtypeStruct((b, 1, LANES), F32)],
        compiler_params=_params(("parallel",), vmem_mib=24),
        name="mlstm_ctx",
    )(mq, mk, mv, gates)


def _mlstm_lat_kernel(qf_ref, kf_ref, vf_ref, gf_ref, qb_ref, kb_ref, vb_ref, gb_ref, c0_ref, n0_ref, m0_ref,
                      ht_ref, c_ref, n_ref, m_ref, *, n_chunks):
    step = pl.program_id(1)

    @pl.when(step == 0)
    def _():
        c_ref[...] = c0_ref[...]
        n_ref[...] = n0_ref[...]
        m_ref[...] = m0_ref[...]
        ht_ref[...] = jnp.zeros_like(ht_ref)

    n = qf_ref.shape[0]
    gate_f, gate_b = _gate_prep(gf_ref[...]), _gate_prep(gb_ref[...])
    is_f = lax.broadcasted_iota(jnp.int32, (n, LANES), 1) < MLSTM_HEADS
    b, u, tot = (jnp.where(is_f[:x.shape[0]], x, y) for x, y in zip(gate_f, gate_b))
    m_prev = m_ref[...]
    u2, bt2, w_t, m_new, decay = _chunk_setup(b, u, tot, m_prev)
    m2_prev = m_prev * LOG2E
    units = [(fwd, hd, hd if fwd else MLSTM_HEADS + hd) for fwd in (True, False) for hd in range(MLSTM_HEADS)]
    refs = {True: (qf_ref, kf_ref, vf_ref), False: (qb_ref, kb_ref, vb_ref)}
    qs = [refs[fwd][0][:, hd * MLSTM_DK:(hd + 1) * MLSTM_DK] for fwd, hd, _ in units]
    ks = [refs[fwd][1][:, hd * MLSTM_DK:(hd + 1) * MLSTM_DK] for fwd, hd, _ in units]
    vts = [refs[fwd][2][:, hd * MLSTM_DV:(hd + 1) * MLSTM_DV].T for fwd, hd, _ in units]
    s0ts = [_dot_nt(k, q) for k, q in zip(ks, qs)]
    c_prevs = [c_ref[kk] for _, _, kk in units]
    n_prevs = [n_ref[kk:kk + 1, :] for _, _, kk in units]
    inters = [(_dot_nt(c.astype(BF16), q), _dot_nt(jnp.broadcast_to(nv, (8, MLSTM_DK)).astype(BF16), q)[0:1, :])
              for c, nv, q in zip(c_prevs, n_prevs, qs)]
    m_rows = [_lane_bcast(m2_prev, kk, n) for _, _, kk in units]
    wts = [_dir_weights(s0t, _lane_bcast(u2, kk, n), m_row, fwd=fwd)
           for s0t, m_row, (fwd, _, kk) in zip(s0ts, m_rows, units)]
    hts = [_dot(vt, sdt) for vt, (sdt, _, _) in zip(vts, wts)]
    for (fwd, hd, kk), k, vt, (_, den, g), ht, inter, m_row, c_prev, n_prev in zip(
            units, ks, vts, wts, hts, inters, m_rows, c_prevs, n_prevs):
        chunk = step if fwd else n_chunks - 1 - step
        ht_ref[chunk, hd * MLSTM_DV:(hd + 1) * MLSTM_DV, :] += _dir_finish(ht, den, g, bt2[kk:kk + 1, :], m_row, inter)
        c_new, n_new = _state_update(vt, k, w_t, kk)
        dk = _lane_bcast(decay, kk)
        c_ref[kk] = dk * c_prev + c_new
        n_ref[kk:kk + 1, :] = dk * n_prev + n_new
    m_ref[...] = m_new


def _mlstm_lat(mq, mk, mv, gates, c0, n0, m0):
    b, s, _ = mq.shape
    lc = MLSTM_CHUNK
    nc = s // lc
    fw = lambda c: pl.BlockSpec((None, lc, c), lambda i, j: (i, j, 0))
    bw = lambda c: pl.BlockSpec((None, lc, c), lambda i, j: (i, nc - 1 - j, 0))
    dk, dv = MLSTM_HEADS * MLSTM_DK, MLSTM_HEADS * MLSTM_DV
    return pl.pallas_call(
        functools.partial(_mlstm_lat_kernel, n_chunks=nc),
        grid=(b, nc),
        in_specs=[fw(dk), fw(dk), fw(dv), fw(LANES), bw(dk), bw(dk), bw(dv), bw(LANES),
                  pl.BlockSpec((None, N_DIRHEAD, MLSTM_DV, MLSTM_DK), lambda i, j: (i, 0, 0, 0)),
                  pl.BlockSpec((None, N_DIRHEAD, MLSTM_DK), lambda i, j: (i, 0, 0)),
                  pl.BlockSpec((None, 1, LANES), lambda i, j: (i, 0, 0))],
        out_specs=pl.BlockSpec((nc, dv, lc), lambda i, j: (i, 0, 0)),
        out_shape=jax.ShapeDtypeStruct((b * nc, dv, lc), F32),
        scratch_shapes=[pltpu.VMEM((N_DIRHEAD, MLSTM_DV, MLSTM_DK), F32),
                        pltpu.VMEM((N_DIRHEAD, MLSTM_DK), F32),
                        pltpu.VMEM((1, LANES), F32)],
        compiler_params=_params(("parallel", "arbitrary"), vmem_mib=36),
        name="mlstm_lat",
    )(mq, mk, mv, gates, mq, mk, mv, gates, c0, n0, m0)


def _tail_kernel(x_ref, mod_ref, att_ref, hst_ref, smo_ref, sbr_ref, gml_ref, womla_ref, womlstm_ref, wout_ref,
                 gffn_ref, wfin_ref, wfout_ref, gfin_ref, y_ref):
    mod = mod_ref[...]
    gml = gml_ref[...]
    tm = x_ref.shape[0]
    groups = [slice(r, r + TAIL_ROWS) for r in range(0, tm, TAIL_ROWS)]
    a = [_dot(att_ref[g, :], womla_ref[...]) for g in groups]
    hm = []
    for gi, g in enumerate(groups):
        hs = hst_ref[gi].T
        parts = []
        for hd in range(MLSTM_HEADS):
            sl = slice(hd * MLSTM_DV, (hd + 1) * MLSTM_DV)
            parts.append((_rms(hs[:, sl], gml[:, sl]) * smo_ref[g, sl].astype(F32)).astype(BF16))
        hm.append(jnp.concatenate(parts, axis=1))
    bm = [_dot(h, womlstm_ref[...]) for h in hm]
    merged = [(sbr_ref[g, :D_MODEL].astype(F32) * ai + sbr_ref[g, D_MODEL:].astype(F32) * bi).astype(BF16)
              for g, ai, bi in zip(groups, a, bm)]
    x1 = [x_ref[g, :] + mod[0:1] * _dot(mi, wout_ref[...]) for g, mi in zip(groups, merged)]
    h2 = [(_rms(xi, gffn_ref[...]) * (1.0 + mod[2:3]) + mod[1:2]).astype(BF16) for xi in x1]
    ga = [_dot(hi, wfin_ref[:, :FFN_HIDDEN]) for hi in h2]
    gu = [_dot(hi, wfin_ref[:, FFN_HIDDEN:]) for hi in h2]
    act = [(gi * _sigmoid(gi) * ui).astype(BF16) for gi, ui in zip(ga, gu)]
    for g, xi, ci in zip(groups, x1, act):
        y_ref[g, :] = _rms(xi + mod[3:4] * _dot(ci, wfout_ref[...]), gfin_ref[...])


def _tail(x, mod, att, hst, smo, sbr, gml, w_o_mla, w_o_mlstm, w_out, gffn, w_ffn_in, w_ffn_out, gfin, *, seq_len, name):
    n_tok = x.shape[0]
    tm = TOKEN_TILE
    tiles_per_seq = seq_len // tm
    assert hst.shape[0] * TAIL_ROWS == n_tok and hst.shape[2] == TAIL_ROWS
    tok = lambda c: pl.BlockSpec((tm, c), lambda i: (i, 0))
    consts = (gml, w_o_mla, w_o_mlstm, w_out, gffn, w_ffn_in, w_ffn_out, gfin)
    return pl.pallas_call(
        _tail_kernel,
        grid=(n_tok // tm,),
        in_specs=[tok(D_MODEL), _mod_spec(mod, tiles_per_seq), tok(att.shape[1]),
                  pl.BlockSpec((tm // TAIL_ROWS,) + hst.shape[1:], lambda i: (i, 0, 0)),
                  tok(smo.shape[1]), tok(sbr.shape[1])]
                 + [_const_spec(a.shape) for a in consts],
        out_specs=tok(D_MODEL),
        out_shape=jax.ShapeDtypeStruct((n_tok, D_MODEL), F32),
        compiler_params=_params(("parallel",)),
        name=name,
    )(x, mod, att, hst, smo, sbr, *consts)


_IN_SIZES = (Q_RANK, KV_RANK, ROPE_DIM, MLSTM_HEADS * MLSTM_DK, MLSTM_HEADS * MLSTM_DK, MLSTM_HEADS * MLSTM_DV,
             4 * MLSTM_HEADS, MLSTM_HEADS * MLSTM_DV, 2 * D_MODEL)
_IN_OFF = tuple(sum(_IN_SIZES[:i]) for i in range(len(_IN_SIZES) + 1))
PACK_ROWS = LANES
_PAD_AFTER = tuple((_SEG_OFF[i + 1] // PACK_ROWS, _SEG_WIDTHS[i] - _IN_SIZES[i], _IN_SIZES[i] % PACK_ROWS)
                   for i in range(len(_IN_SIZES)) if _SEG_WIDTHS[i] != _IN_SIZES[i])


PACK_PIECES = 8
N_PACK_STEPS = pl.cdiv(N_IN_PACKED // PACK_ROWS, PACK_PIECES)
_N_IN = _IN_OFF[-1]


def _pack_src_row(p):
    src = p * PACK_ROWS
    for first_piece_after, pad, _ in _PAD_AFTER:
        src = src - jnp.where(p >= first_piece_after, pad, 0)
    src = jnp.minimum(src, _N_IN - PACK_ROWS)
    return pl.multiple_of(src, math.gcd(PACK_ROWS, _N_IN, *(pad for _, pad, _ in _PAD_AFTER)))


def _pack_in_kernel(*refs):
    w_refs, o_ref = refs[:PACK_PIECES], refs[PACK_PIECES]
    row = lax.broadcasted_iota(jnp.int32, w_refs[0].shape, 0)
    for j, w_ref in enumerate(w_refs):
        p = pl.program_id(0) * PACK_PIECES + j
        valid = jnp.where(p < N_IN_PACKED // PACK_ROWS, PACK_ROWS, 0)
        for first_piece_after, _, width in _PAD_AFTER:
            valid = jnp.where(p == first_piece_after - 1, width, valid)
        o_ref[j * PACK_ROWS:(j + 1) * PACK_ROWS, :] = jnp.where(row < valid, w_ref[...], 0.0).astype(BF16)


def _pack_in(w_in_t):
    n, k = w_in_t.shape
    piece = lambda j: pl.BlockSpec((pl.Element(PACK_ROWS), pl.Element(k)),
                                   lambda i: (_pack_src_row(i * PACK_PIECES + j), 0))
    return pl.pallas_call(
        _pack_in_kernel,
        grid=(N_PACK_STEPS,),
        in_specs=[piece(j) for j in range(PACK_PIECES)],
        out_specs=pl.BlockSpec((PACK_PIECES * PACK_ROWS, k), lambda i: (i, 0)),
        out_shape=jax.ShapeDtypeStruct((N_PACK_STEPS * PACK_PIECES * PACK_ROWS, k), BF16),
        compiler_params=_params(("parallel",), vmem_mib=24),
        name="pack_in",
    )(*([w_in_t] * PACK_PIECES))


def _pack_small(w_uq, w_ukv, b_gates):
    qd = NOPE_DIM + ROPE_DIM
    w_uq_p = jnp.pad(w_uq.reshape(Q_RANK, MLA_HEADS, qd), ((0, 0), (0, 0), (0, HEAD_BLOCK - qd)))
    w_uq_p = w_uq_p.reshape(Q_RANK, MLA_HEADS * HEAD_BLOCK).astype(BF16)
    kv = w_ukv.reshape(KV_RANK, MLA_HEADS, NOPE_DIM + MLA_V_DIM)
    wk = jnp.pad(kv[:, :, :NOPE_DIM], ((0, 0), (0, 0), (0, HEAD_BLOCK - NOPE_DIM))).reshape(KV_RANK, -1)
    wv = kv[:, :, NOPE_DIM:].reshape(KV_RANK, -1)
    w_ukv_p = jnp.concatenate([wk, wv], axis=1).astype(BF16)
    bg = jnp.pad(b_gates, ((0, 0), (0, LANES - b_gates.shape[1])))
    return w_uq_p, w_ukv_p, bg


def _rope_tables(n_tokens):
    pos = np.arange(n_tokens)
    row = (pos // GRID_W).astype(np.float64)[:, None]
    col = (pos % GRID_W).astype(np.float64)[:, None]
    half = ROPE_DIM // 2
    inv = (np.float32(ROPE_BASE) ** (-np.arange(0, half, 2, dtype=np.float32) / np.float32(half))).astype(np.float64)
    r = np.arange(LANES) - ROPE_LANE0
    in_rope = (r >= 0) & (r < ROPE_DIM)
    rr = np.clip(r, 0, ROPE_DIM - 1)
    freq = inv[rr % (half // 2)][None, :]
    ang = np.where((rr // half == 0)[None, :], row * freq, col * freq).astype(np.float32).astype(np.float64)
    first = (rr % half) < (half // 2)
    cos = np.where(in_rope[None, :], np.cos(ang), 1.0)
    sin = np.sin(ang)
    sin_lo = np.where((in_rope & first)[None, :], -sin, 0.0)
    sin_hi = np.where((in_rope & ~first)[None, :], sin, 0.0)
    return tuple(jnp.asarray(t, dtype=F32) for t in (cos, sin_lo, sin_hi))


def kernel(x_prompt, x_sample, cache_ckv, cache_krope, state_C, state_n, state_m, c, c_ctx, w_mod, b_mod, g_norm_mix,
           w_in, b_gates, g_q_norm, w_uq, g_kv_norm, w_ukv, g_mlstm_norm, w_o_mla, w_o_mlstm, w_out, g_norm_ffn,
           w_ffn_in, w_ffn_out, g_final):
    bp, sp, d = x_prompt.shape
    bs, ss, _ = x_sample.shape
    layer = 0
    assert w_mod.shape[0] == 1 and sp == MLSTM_CHUNK and ss % MLSTM_CHUNK == 0

    cond = jnp.concatenate([c_ctx[None, :], c, jnp.zeros((8 - 1 - bs, d), F32)], axis=0)
    adaln_args = (cond, w_mod[layer], b_mod[layer][None, :])
    mod_in = _adaln(*adaln_args).reshape(8, N_MOD_MIXER, d)
    mod_ctx, mod_lat = mod_in[0:1], mod_in[1:1 + bs]

    w_in_p = _pack_in(w_in[layer].T)
    w_uq_p, w_ukv_p, bg = _pack_small(w_uq[layer], w_ukv[layer], b_gates[layer][None, :])
    row = lambda g: g[layer][None, :]
    shared_in = (row(g_norm_mix), w_in_p, row(g_q_norm), w_uq_p, row(g_kv_norm), w_ukv_p, bg)
    seq3 = lambda a, b_, s_: a.reshape(b_, s_, a.shape[-1])

    xl = x_sample.reshape(bs * ss, d)
    q, kc, v, mq_l, mk_l, mv_l, gates_l, smo_l, sbr_l = _inproj(xl, mod_lat, _rope_tables(ss), *shared_in, seq_len=ss)
    krope_blk = jnp.pad(cache_krope[:, layer], ((0, 0), (0, 0), (ROPE_LANE0, LANES - ROPE_LANE0 - ROPE_DIM)))
    cache = _cachekv(cache_ckv[:, layer], krope_blk, w_ukv_p)
    att_l, womla_b, womlstm_b, wout_b, wfin_b, wfout_b, mod_tail = _attention(
        seq3(q, bs, ss), seq3(kc, bs, ss), seq3(v, bs, ss), cache,
        cast_weights=(w_o_mla[layer], w_o_mlstm[layer], w_out[layer], w_ffn_in[layer], w_ffn_out[layer]),
        adaln_tail=adaln_args)
    tail_w = (row(g_mlstm_norm), womla_b, womlstm_b, wout_b, row(g_norm_ffn), wfin_b, wfout_b, g_final[None, :])
    mod_tail = mod_tail.reshape(8, N_MOD_TAIL, d)
    modt_ctx, modt_lat = mod_tail[0:1], mod_tail[1:1 + bs]

    xc = x_prompt.reshape(bp * sp, d)
    q, kc, v, mq, mk, mv, gates, smo, sbr, ckv, kpe = _inproj(xc, mod_ctx, None, *shared_in, seq_len=sp)
    att = _attention(seq3(q, bp, sp), seq3(kc, bp, sp), seq3(v, bp, sp))
    hs, new_c, new_n, new_m = _mlstm_ctx(seq3(mq, bp, sp), seq3(mk, bp, sp), seq3(mv, bp, sp), seq3(gates, bp, sp))
    y_prompt = _tail(xc, modt_ctx, att.reshape(bp * sp, -1), hs, smo, sbr, *tail_w,
                     seq_len=sp, name="tail_ctx").reshape(bp, sp, d)
    new_ckv = ckv.reshape(bp, 1, sp, KV_RANK)
    new_krope = kpe.reshape(bp, 1, sp, ROPE_DIM)
    new_C = new_c.reshape(bp, 1, 2, MLSTM_HEADS, MLSTM_DV, MLSTM_DK)
    new_N = new_n.reshape(bp, 1, 2, MLSTM_HEADS, MLSTM_DK)
    new_M = new_m[:, 0, :N_DIRHEAD].reshape(bp, 1, 2, MLSTM_HEADS)

    c0 = state_C[:, layer].reshape(bs, N_DIRHEAD, MLSTM_DV, MLSTM_DK)
    n0 = state_n[:, layer].reshape(bs, N_DIRHEAD, MLSTM_DK)
    m0 = jnp.pad(state_m[:, layer].reshape(bs, 1, N_DIRHEAD), ((0, 0), (0, 0), (0, LANES - N_DIRHEAD)))
    hs = _mlstm_lat(seq3(mq_l, bs, ss), seq3(mk_l, bs, ss), seq3(mv_l, bs, ss), seq3(gates_l, bs, ss), c0, n0, m0)
    y_sample = _tail(xl, modt_lat, att_l.reshape(bs * ss, -1), hs, smo_l, sbr_l, *tail_w,
                     seq_len=ss, name="tail_lat").reshape(bs, ss, d)
    return (y_prompt, y_sample, new_ckv, new_krope, new_C, new_N, new_M)
```

```python
import functools
import math

import jax
import jax.numpy as jnp
import numpy as np
from jax import lax
from jax.experimental import pallas as pl
from jax.experimental.pallas import tpu as pltpu

F32 = jnp.float32
BF16 = jnp.bfloat16

D_MODEL = 1024
GRID_W = 64
MLA_HEADS = 8
Q_RANK = 384
KV_RANK = 256
NOPE_DIM = 64
ROPE_DIM = 32
MLA_V_DIM = 64
ROPE_BASE = 10000.0
MLA_SCALE = (NOPE_DIM + ROPE_DIM) ** -0.5
MLSTM_HEADS = 4
MLSTM_DK = 128
MLSTM_DV = 256
FFN_HIDDEN = ((8 * D_MODEL // 3 + 255) // 256) * 256
EPS = 1e-6

LANES = 128
HEAD_BLOCK = LANES
ROPE_LANE0 = NOPE_DIM
N_DIRHEAD = 2 * MLSTM_HEADS
MLSTM_CHUNK = 256
LOG2E = math.log2(math.e)
Q_PRESCALE = MLA_SCALE * LOG2E
V_SLAB = 2 * LANES
V_WIDTH = (MLA_HEADS // 2) * V_SLAB
TOKEN_TILE = 512
TAIL_ROWS = 256
INPROJ_ROWS = TOKEN_TILE
Q_TILE = 256
CTX_SEQS_PER_STEP = 4
VMEM_LIMIT = 56 * 1024 * 1024

_SEG_WIDTHS = (Q_RANK, KV_RANK, LANES, MLSTM_HEADS * MLSTM_DK, MLSTM_HEADS * MLSTM_DK,
               MLSTM_HEADS * MLSTM_DV, LANES, MLSTM_HEADS * MLSTM_DV, 2 * D_MODEL)
_SEG_OFF = tuple(sum(_SEG_WIDTHS[:i]) for i in range(len(_SEG_WIDTHS) + 1))
SEG_Q, SEG_KV, SEG_KPE, SEG_MQ, SEG_MK, SEG_MV, SEG_GATE, SEG_MO, SEG_BR = (
    (_SEG_OFF[i], _SEG_OFF[i + 1]) for i in range(9))
N_IN_PACKED = _SEG_OFF[-1]


def _dot(a, b):
    return jnp.dot(a, b, preferred_element_type=F32)


def _dot_nt(a, b):
    return lax.dot_general(a, b, (((1,), (1,)), ((), ())), preferred_element_type=F32)


def _dot_tn(a, b):
    return lax.dot_general(a, b, (((0,), (0,)), ((), ())), preferred_element_type=F32)


def _rms(x, g):
    ms = jnp.mean(x * x, axis=-1, keepdims=True)
    return x * lax.rsqrt(ms + EPS) * g


def _sigmoid(x):
    return 1.0 / (1.0 + jnp.exp(-x))


def _const_spec(shape):
    nd = len(shape)
    return pl.BlockSpec(shape, lambda *_: (0,) * nd, pipeline_mode=pl.Buffered(1))


def _mod_spec(mod, tiles_per_seq):
    if mod.shape[0] == 1:
        return pl.BlockSpec((None,) + mod.shape[1:], lambda i: (0, 0, 0))
    return pl.BlockSpec((None,) + mod.shape[1:], lambda i: (i // tiles_per_seq, 0, 0))


def _params(sem, flags=None, vmem_mib=None):
    limit = VMEM_LIMIT if vmem_mib is None else vmem_mib * 1024 * 1024
    return pltpu.CompilerParams(dimension_semantics=sem, vmem_limit_bytes=limit, flags=flags)


N_MOD_MIXER = 2
N_MOD_TAIL = 4


def _adaln_block(c_ref, w_ref, b_ref, o_ref):
    c = c_ref[...]
    s = c * _sigmoid(c)
    o_ref[...] = _dot(s.astype(BF16), w_ref[...].astype(BF16)) + b_ref[...]


def _adaln(cond8, w_mod, b_mod):
    tn = D_MODEL
    return pl.pallas_call(
        _adaln_block,
        grid=(N_MOD_MIXER,),
        in_specs=[pl.BlockSpec((8, D_MODEL), lambda j: (0, 0)),
                  pl.BlockSpec((D_MODEL, tn), lambda j: (0, j)),
                  pl.BlockSpec((1, tn), lambda j: (0, j))],
        out_specs=pl.BlockSpec((8, tn), lambda j: (0, j)),
        out_shape=jax.ShapeDtypeStruct((8, N_MOD_MIXER * D_MODEL), F32),
        compiler_params=_params(("arbitrary",)),
        name="adaln",
    )(cond8, w_mod, b_mod)


def _rope_block(x, cos, sin_lo, sin_hi):
    return x * cos + pltpu.roll(x, LANES - 8, 1) * sin_lo + pltpu.roll(x, 8, 1) * sin_hi


def _store_kv(kvn, kp, kc_ref, v_ref, g=slice(None)):
    for hd in range(MLA_HEADS):
        sl = slice(hd * HEAD_BLOCK, (hd + 1) * HEAD_BLOCK)
        kc_ref[g, sl] = (kvn[:, sl] + kp).astype(BF16)
    v0 = MLA_HEADS * HEAD_BLOCK
    ones = jnp.ones((kvn.shape[0], LANES), BF16)
    for pair in range(MLA_HEADS // 2):
        v_ref[g, pair * V_SLAB:pair * V_SLAB + LANES] = kvn[:, v0 + pair * LANES:v0 + (pair + 1) * LANES].astype(BF16)
        v_ref[g, pair * V_SLAB + LANES:(pair + 1) * V_SLAB] = ones


def _inproj_kernel(*refs, rope):
    if rope:
        (x_ref, mod_ref, cos_ref, slo_ref, shi_ref, gmix_ref, win_ref, gq_ref, wuq_ref, gkv_ref, wukv_ref,
         bg_ref, q_ref, kc_ref, v_ref, mq_ref, mk_ref, mv_ref, gate_ref, smo_ref, sbr_ref) = refs
    else:
        (x_ref, mod_ref, gmix_ref, win_ref, gq_ref, wuq_ref, gkv_ref, wukv_ref,
         bg_ref, q_ref, kc_ref, v_ref, mq_ref, mk_ref, mv_ref, gate_ref, smo_ref, sbr_ref,
         ckv_ref, kpe_ref) = refs
    mod = mod_ref[...]
    for r0 in range(0, x_ref.shape[0], INPROJ_ROWS):
        g = slice(r0, r0 + INPROJ_ROWS)
        if rope:
            cos, slo, shi = cos_ref[g, :], slo_ref[g, :], shi_ref[g, :]
        h = _rms(x_ref[g, :], gmix_ref[...]) * (1.0 + mod[1:2]) + mod[0:1]
        hb = h.astype(BF16)

        def proj(seg):
            return _dot_nt(hb, win_ref[seg[0]:seg[1], :])

        qn = _rms(proj(SEG_Q), gq_ref[...]).astype(BF16)
        ckv = _rms(proj(SEG_KV), gkv_ref[...])
        zkpe = proj(SEG_KPE)
        sbr_ref[g, :] = _sigmoid(proj(SEG_BR)).astype(BF16)
        smo_ref[g, :] = _sigmoid(proj(SEG_MO)).astype(BF16)
        gate_ref[g, :] = proj(SEG_GATE) + bg_ref[...]

        q = _dot(qn, wuq_ref[...])
        for hd in range(MLA_HEADS):
            sl = slice(hd * HEAD_BLOCK, (hd + 1) * HEAD_BLOCK)
            qh = q[:, sl]
            if rope:
                qh = _rope_block(qh, cos, slo, shi)
            q_ref[g, sl] = (qh * Q_PRESCALE).astype(BF16)

        if not rope:
            ckv_ref[g, :] = ckv
            kpe_ref[g, :] = zkpe[:, :ROPE_DIM]
        kp = pltpu.roll(zkpe, ROPE_LANE0, 1)
        if rope:
            kp = _rope_block(kp, cos, slo, shi)
        kvn = _dot(ckv.astype(BF16), wukv_ref[...])
        _store_kv(kvn, kp, kc_ref, v_ref, g)

        mv_ref[g, :] = proj(SEG_MV).astype(BF16)
        mk_ref[g, :] = (proj(SEG_MK) * (MLSTM_DK ** -0.5)).astype(BF16)
        mq_ref[g, :] = proj(SEG_MQ).astype(BF16)


def _inproj(x, mod, rope_tabs, gmix, w_in_p, gq, w_uq_p, gkv, w_ukv_p, bg, *, seq_len):
    n_tok = x.shape[0]
    tm = TOKEN_TILE
    tiles_per_seq = seq_len // tm
    rope = rope_tabs is not None
    tok = lambda c: pl.BlockSpec((tm, c), lambda i: (i, 0))
    in_specs = [tok(D_MODEL), _mod_spec(mod, tiles_per_seq)]
    args = [x, mod]
    if rope:
        in_specs += [pl.BlockSpec((tm, LANES), lambda i: (i % tiles_per_seq, 0))] * 3
        args += list(rope_tabs)
    in_specs += [_const_spec(a.shape) for a in (gmix, w_in_p, gq, w_uq_p, gkv, w_ukv_p, bg)]
    args += [gmix, w_in_p, gq, w_uq_p, gkv, w_ukv_p, bg]
    widths = [(MLA_HEADS * HEAD_BLOCK, BF16), (MLA_HEADS * HEAD_BLOCK, BF16), (V_WIDTH, BF16),
              (MLSTM_HEADS * MLSTM_DK, BF16), (MLSTM_HEADS * MLSTM_DK, BF16), (MLSTM_HEADS * MLSTM_DV, BF16),
              (LANES, F32), (MLSTM_HEADS * MLSTM_DV, BF16), (2 * D_MODEL, BF16)]
    if not rope:
        widths += [(KV_RANK, F32), (ROPE_DIM, F32)]
    return pl.pallas_call(
        functools.partial(_inproj_kernel, rope=rope),
        grid=(n_tok // tm,),
        in_specs=in_specs,
        out_specs=[tok(c) for c, _ in widths],
        out_shape=[jax.ShapeDtypeStruct((n_tok, c), dt) for c, dt in widths],
        compiler_params=_params(("parallel",)),
        name="inproj_lat" if rope else "inproj_ctx",
    )(*args)


def _cachekv_kernel(ckv_ref, kp_ref, wukv_ref, kc_ref, v_ref):
    kvn = _dot(ckv_ref[...].astype(BF16), wukv_ref[...])
    _store_kv(kvn, kp_ref[...], kc_ref, v_ref)


def _cachekv(cache_ckv, krope_blk, w_ukv_p):
    b, past, _ = cache_ckv.shape
    return pl.pallas_call(
        _cachekv_kernel,
        grid=(b,),
        in_specs=[pl.BlockSpec((None, past, KV_RANK), lambda i: (i, 0, 0)),
                  pl.BlockSpec((None, past, LANES), lambda i: (i, 0, 0)),
                  _const_spec(w_ukv_p.shape)],
        out_specs=[pl.BlockSpec((None, past, MLA_HEADS * HEAD_BLOCK), lambda i: (i, 0, 0)),
                   pl.BlockSpec((None, past, V_WIDTH), lambda i: (i, 0, 0))],
        out_shape=[jax.ShapeDtypeStruct((b, past, MLA_HEADS * HEAD_BLOCK), BF16),
                   jax.ShapeDtypeStruct((b, past, V_WIDTH), BF16)],
        compiler_params=_params(("parallel",)),
        name="cachekv",
    )(cache_ckv, krope_blk, w_ukv_p)


def _attn_kernel(*refs, has_cache, n_cast, has_adaln):
    n_in = 5 if has_cache else 3
    if has_cache:
        q_ref, k_ref, v_ref, kc_ref, vc_ref = refs[:n_in]
    else:
        q_ref, k_ref, v_ref = refs[:n_in]
    n_all_in = n_in + n_cast + (3 if has_adaln else 0)
    o_ref = refs[n_all_in]
    for w_ref, wb_ref in zip(refs[n_in:n_in + n_cast], refs[n_all_in + 1:n_all_in + 1 + n_cast]):
        wb_ref[...] = w_ref[...].astype(BF16)
    if has_adaln:
        _adaln_block(*refs[n_in + n_cast:n_all_in], refs[n_all_in + 1 + n_cast])
    tq = q_ref.shape[1]
    lane = lax.broadcasted_iota(jnp.int32, (tq, LANES), 1)
    for sq in range(q_ref.shape[0]):
        hsl = [slice(hd * HEAD_BLOCK, (hd + 1) * HEAD_BLOCK) for hd in range(MLA_HEADS)]
        vsl = [slice(hd // 2 * V_SLAB, (hd // 2 + 1) * V_SLAB) for hd in range(MLA_HEADS)]
        s = [_dot_nt(q_ref[sq, :, sl], k_ref[sq, :, sl]) for sl in hsl]
        m = [jnp.max(si, axis=-1, keepdims=True) for si in s]
        if has_cache:
            sc = [_dot_nt(q_ref[sq, :, sl], kc_ref[sq, :, sl]) for sl in hsl]
            m = [jnp.maximum(mi, jnp.max(ci, axis=-1, keepdims=True)) for mi, ci in zip(m, sc)]
        p = [jnp.exp2(si - mi).astype(BF16) for si, mi in zip(s, m)]
        o = [_dot(pi, v_ref[sq, :, sl]) for pi, sl in zip(p, vsl)]
        if has_cache:
            pc = [jnp.exp2(ci - mi).astype(BF16) for ci, mi in zip(sc, m)]
            o = [oi + _dot(pi, vc_ref[sq, :, sl]) for oi, pi, sl in zip(o, pc, vsl)]
        outs = [oi[:, :LANES] * (1.0 / oi[:, LANES:]) for oi in o]
        for pair in range(MLA_HEADS // 2):
            o_ref[sq, :, pair * LANES:(pair + 1) * LANES] = jnp.where(
                lane < MLA_V_DIM, outs[2 * pair], outs[2 * pair + 1]).astype(BF16)


def _attention(q, k, v, cache=None, cast_weights=(), adaln_tail=None, seqs_per_step=1):
    nb, s, _ = q.shape
    tq = Q_TILE
    ns = seqs_per_step
    b = nb // ns
    n_steps = b * (s // tq)
    kw, vw, ow = MLA_HEADS * HEAD_BLOCK, V_WIDTH, MLA_HEADS * MLA_V_DIM
    in_specs = [pl.BlockSpec((ns, tq, kw), lambda i, j: (i, j, 0)),
                pl.BlockSpec((ns, s, kw), lambda i, j: (i, 0, 0)),
                pl.BlockSpec((ns, s, vw), lambda i, j: (i, 0, 0))]
    args = [q, k, v]
    if cache is not None:
        past = cache[0].shape[1]
        in_specs += [pl.BlockSpec((ns, past, kw), lambda i, j: (i, 0, 0)),
                     pl.BlockSpec((ns, past, vw), lambda i, j: (i, 0, 0))]
        args += list(cache)
    step = lambda i, j: i * (s // tq) + j
    w_specs = [pl.BlockSpec((w.shape[0] // n_steps, w.shape[1]), lambda i, j: (step(i, j), 0)) for w in cast_weights]
    extra_in, extra_out, extra_shape = [], [], []
    if adaln_tail is not None:
        tn = N_MOD_TAIL * D_MODEL // n_steps
        col0 = N_MOD_MIXER * D_MODEL // tn
        extra_in = [pl.BlockSpec((8, D_MODEL), lambda i, j: (0, 0)),
                    pl.BlockSpec((D_MODEL, tn), lambda i, j: (0, col0 + step(i, j))),
                    pl.BlockSpec((1, tn), lambda i, j: (0, col0 + step(i, j)))]
        extra_out = [pl.BlockSpec((8, tn), lambda i, j: (0, step(i, j)))]
        extra_shape = [jax.ShapeDtypeStruct((8, N_MOD_TAIL * D_MODEL), F32)]
    outs = pl.pallas_call(
        functools.partial(_attn_kernel, has_cache=cache is not None, n_cast=len(cast_weights),
                          has_adaln=adaln_tail is not None),
        grid=(b, s // tq),
        in_specs=in_specs + w_specs + extra_in,
        out_specs=[pl.BlockSpec((ns, tq, ow), lambda i, j: (i, j, 0))] + w_specs + extra_out,
        out_shape=[jax.ShapeDtypeStruct((nb, s, ow), BF16)]
                  + [jax.ShapeDtypeStruct(w.shape, BF16) for w in cast_weights] + extra_shape,
        compiler_params=_params(("parallel", "arbitrary")),
        name="attn_lat" if cache is not None else "attn_ctx",
    )(*args, *cast_weights, *(adaln_tail or ()))
    return outs[0] if len(outs) == 1 else outs


def _lane_bcast(x, k, width=None):
    y = jnp.broadcast_to(x[:, k:k + 1], x.shape)
    reps = (width or LANES) // LANES
    return y if reps == 1 else jnp.concatenate([y] * reps, axis=1)


def _prefix_sum_rows(x):
    n = x.shape[0]
    r = lax.broadcasted_iota(jnp.int32, (n, n), 0)
    c = lax.broadcasted_iota(jnp.int32, (n, n), 1)
    tri = jnp.where(c <= r, 1.0, 0.0).astype(BF16)
    hi = x.astype(BF16)
    r1 = x - hi.astype(F32)
    mid = r1.astype(BF16)
    lo = (r1 - mid.astype(F32)).astype(BF16)
    return _dot(tri, hi) + _dot(tri, mid) + _dot(tri, lo)


def _gate_prep(g):
    n = g.shape[0]
    lane = lax.broadcasted_iota(jnp.int32, (n, LANES), 1)
    fpre = pltpu.roll(g, LANES - N_DIRHEAD, 1)
    lf = jnp.minimum(fpre, 0.0) - jnp.log(1.0 + jnp.exp(-jnp.abs(fpre)))
    binc = _prefix_sum_rows(lf)
    tot = binc[n - 1:n, :]
    b = jnp.where(lane < MLSTM_HEADS, binc, tot - binc + lf)
    return b, g - b, tot


def _chunk_setup(b, u, tot, m_prev):
    g_last = jnp.maximum(jnp.max(u, axis=0, keepdims=True), m_prev)
    u2 = u * LOG2E
    w = jnp.exp2(u2 - g_last * LOG2E)
    return u2, (b * LOG2E).T, w.T, tot + g_last, jnp.exp(m_prev - g_last)


def _state_update(vt, k, wt, kk):
    c_new = _dot((vt * wt[kk:kk + 1, :]).astype(BF16), k)
    n_new = _dot(wt[:N_DIRHEAD, :].astype(BF16), k)[kk:kk + 1, :]
    return c_new, n_new


def _mlstm_dir_t(s0t, vt, u_b, bt_row, m_row, *, fwd, inter=None):
    sdt, den, g = _dir_weights(s0t, u_b, m_row, fwd=fwd)
    return _dir_finish(_dot(vt, sdt), den, g, bt_row, m_row, inter)


def _dir_weights(s0t, u_b, m_row, *, fwd):
    n = s0t.shape[0]
    r = lax.broadcasted_iota(jnp.int32, (n, n), 0)
    c = lax.broadcasted_iota(jnp.int32, (n, n), 1)
    a = jnp.where((r <= c) if fwd else (r >= c), u_b, -jnp.inf)
    g = jnp.maximum(jnp.max(a, axis=0, keepdims=True), m_row)
    sdt = s0t * jnp.exp2(a - g)
    return sdt.astype(BF16), jnp.sum(sdt, axis=0, keepdims=True), g


def _dir_finish(ht, den, g, bt_row, m_row, inter=None):
    if inter is not None:
        w_inter = jnp.exp2(m_row - g)
        ht = ht + w_inter * inter[0]
        den = den + w_inter * inter[1]
    floor = jnp.exp2(-(bt_row + g))
    return ht * (1.0 / jnp.maximum(jnp.abs(den), floor))


def _mlstm_ctx_kernel(q_ref, k_ref, v_ref, g_ref, ht_ref, c_ref, n_ref, m_ref):
    n = q_ref.shape[1]
    zero_lanes = jnp.zeros((1, LANES), F32)
    zero_row = jnp.zeros((1, n), F32)
    dirs = ((True, 0), (False, MLSTM_HEADS))
    setup = [_chunk_setup(*_gate_prep(g_ref[sq]), zero_lanes) for sq in range(q_ref.shape[0])]
    units = [(sq, hd) for sq in range(q_ref.shape[0]) for hd in range(MLSTM_HEADS)]
    ks = [k_ref[sq, :, hd * MLSTM_DK:(hd + 1) * MLSTM_DK] for sq, hd in units]
    s0ts = [_dot_nt(k, q_ref[sq, :, hd * MLSTM_DK:(hd + 1) * MLSTM_DK]) for k, (sq, hd) in zip(ks, units)]
    vts = [v_ref[sq, :, hd * MLSTM_DV:(hd + 1) * MLSTM_DV].T for sq, hd in units]
    wts = [[_dir_weights(s0t, _lane_bcast(setup[sq][0], off + hd, n), zero_row, fwd=fwd) for fwd, off in dirs]
           for s0t, (sq, hd) in zip(s0ts, units)]
    hts = [[_dot(vt, sdt) for sdt, _, _ in wt] for vt, wt in zip(vts, wts)]
    for (sq, hd), k, vt, wt, ht2 in zip(units, ks, vts, wts, hts):
        _, bt2, w_t, _, _ = setup[sq]
        ht_ref[sq, hd * MLSTM_DV:(hd + 1) * MLSTM_DV, :] = sum(
            _dir_finish(ht_d, den, g, bt2[off + hd:off + hd + 1, :], zero_row)
            for ht_d, (_, den, g), (_, off) in zip(ht2, wt, dirs))
        for _, off in dirs:
            c_ref[sq, off + hd], n_ref[sq, off + hd:off + hd + 1, :] = _state_update(vt, k, w_t, off + hd)
    for sq in range(q_ref.shape[0]):
        m_ref[sq] = setup[sq][3]


def _mlstm_ctx(mq, mk, mv, gates):
    b, s, _ = mq.shape
    ns = CTX_SEQS_PER_STEP
    seq = lambda c: pl.BlockSpec((ns, s, c), lambda i: (i, 0, 0))
    return pl.pallas_call(
        _mlstm_ctx_kernel,
        grid=(b // ns,),
        in_specs=[seq(MLSTM_HEADS * MLSTM_DK), seq(MLSTM_HEADS * MLSTM_DK), seq(MLSTM_HEADS * MLSTM_DV), seq(LANES)],
        out_specs=[pl.BlockSpec((ns, MLSTM_HEADS * MLSTM_DV, s), lambda i: (i, 0, 0)),
                   pl.BlockSpec((ns, N_DIRHEAD, MLSTM_DV, MLSTM_DK), lambda i: (i, 0, 0, 0)),
                   pl.BlockSpec((ns, N_DIRHEAD, MLSTM_DK), lambda i: (i, 0, 0)),
                   pl.BlockSpec((ns, 1, LANES), lambda i: (i, 0, 0))],
        out_shape=[jax.ShapeDtypeStruct((b, MLSTM_HEADS * MLSTM_DV, s), F32),
                   jax.ShapeDtypeStruct((b, N_DIRHEAD, MLSTM_DV, MLSTM_DK), F32),
                   jax.ShapeDtypeStruct((b, N_DIRHEAD, MLSTM_DK), F32),
                   jax.ShapeDtypeStruct((b, 1, LANES), F32)],
        compiler_params=_params(("parallel",)),
        name="mlstm_ctx",
    )(mq, mk, mv, gates)


def _mlstm_lat_kernel(qf_ref, kf_ref, vf_ref, gf_ref, qb_ref, kb_ref, vb_ref, gb_ref, c0_ref, n0_ref, m0_ref,
                      ht_ref, c_ref, n_ref, m_ref, *, n_chunks):
    step = pl.program_id(1)

    @pl.when(step == 0)
    def _():
        c_ref[...] = c0_ref[...]
        n_ref[...] = n0_ref[...]
        m_ref[...] = m0_ref[...]
        ht_ref[...] = jnp.zeros_like(ht_ref)

    n = qf_ref.shape[0]
    gate_f, gate_b = _gate_prep(gf_ref[...]), _gate_prep(gb_ref[...])
    is_f = lax.broadcasted_iota(jnp.int32, (n, LANES), 1) < MLSTM_HEADS
    b, u, tot = (jnp.where(is_f[:x.shape[0]], x, y) for x, y in zip(gate_f, gate_b))
    m_prev = m_ref[...]
    u2, bt2, w_t, m_new, decay = _chunk_setup(b, u, tot, m_prev)
    m2_prev = m_prev * LOG2E
    units = [(fwd, hd, hd if fwd else MLSTM_HEADS + hd) for fwd in (True, False) for hd in range(MLSTM_HEADS)]
    refs = {True: (qf_ref, kf_ref, vf_ref), False: (qb_ref, kb_ref, vb_ref)}
    qs = [refs[fwd][0][:, hd * MLSTM_DK:(hd + 1) * MLSTM_DK] for fwd, hd, _ in units]
    ks = [refs[fwd][1][:, hd * MLSTM_DK:(hd + 1) * MLSTM_DK] for fwd, hd, _ in units]
    vts = [refs[fwd][2][:, hd * MLSTM_DV:(hd + 1) * MLSTM_DV].T for fwd, hd, _ in units]
    s0ts = [_dot_nt(k, q) for k, q in zip(ks, qs)]
    c_prevs = [c_ref[kk] for _, _, kk in units]
    n_prevs = [n_ref[kk:kk + 1, :] for _, _, kk in units]
    inters = [(_dot_nt(c.astype(BF16), q), _dot_nt(jnp.broadcast_to(nv, (8, MLSTM_DK)).astype(BF16), q)[0:1, :])
              for c, nv, q in zip(c_prevs, n_prevs, qs)]
    m_rows = [_lane_bcast(m2_prev, kk, n) for _, _, kk in units]
    wts = [_dir_weights(s0t, _lane_bcast(u2, kk, n), m_row, fwd=fwd)
           for s0t, m_row, (fwd, _, kk) in zip(s0ts, m_rows, units)]
    hts = [_dot(vt, sdt) for vt, (sdt, _, _) in zip(vts, wts)]
    for (fwd, hd, kk), k, vt, (_, den, g), ht, inter, m_row, c_prev, n_prev in zip(
            units, ks, vts, wts, hts, inters, m_rows, c_prevs, n_prevs):
        chunk = step if fwd else n_chunks - 1 - step
        ht_ref[chunk, hd * MLSTM_DV:(hd + 1) * MLSTM_DV, :] += _dir_finish(ht, den, g, bt2[kk:kk + 1, :], m_row, inter)
        c_new, n_new = _state_update(vt, k, w_t, kk)
        dk = _lane_bcast(decay, kk)
        c_ref[kk] = dk * c_prev + c_new
        n_ref[kk:kk + 1, :] = dk * n_prev + n_new
    m_ref[...] = m_new


def _mlstm_lat(mq, mk, mv, gates, c0, n0, m0):
    b, s, _ = mq.shape
    lc = MLSTM_CHUNK
    nc = s // lc
    fw = lambda c: pl.BlockSpec((None, lc, c), lambda i, j: (i, j, 0))
    bw = lambda c: pl.BlockSpec((None, lc, c), lambda i, j: (i, nc - 1 - j, 0))
    dk, dv = MLSTM_HEADS * MLSTM_DK, MLSTM_HEADS * MLSTM_DV
    return pl.pallas_call(
        functools.partial(_mlstm_lat_kernel, n_chunks=nc),
        grid=(b, nc),
        in_specs=[fw(dk), fw(dk), fw(dv), fw(LANES), bw(dk), bw(dk), bw(dv), bw(LANES),
                  pl.BlockSpec((None, N_DIRHEAD, MLSTM_DV, MLSTM_DK), lambda i, j: (i, 0, 0, 0)),
                  pl.BlockSpec((None, N_DIRHEAD, MLSTM_DK), lambda i, j: (i, 0, 0)),
                  pl.BlockSpec((None, 1, LANES), lambda i, j: (i, 0, 0))],
        out_specs=pl.BlockSpec((nc, dv, lc), lambda i, j: (i, 0, 0)),
        out_shape=jax.ShapeDtypeStruct((b * nc, dv, lc), F32),
        scratch_shapes=[pltpu.VMEM((N_DIRHEAD, MLSTM_DV, MLSTM_DK), F32),
                        pltpu.VMEM((N_DIRHEAD, MLSTM_DK), F32),
                        pltpu.VMEM((1, LANES), F32)],
        compiler_params=_params(("parallel", "arbitrary")),
        name="mlstm_lat",
    )(mq, mk, mv, gates, mq, mk, mv, gates, c0, n0, m0)


def _tail_kernel(x_ref, mod_ref, att_ref, hst_ref, smo_ref, sbr_ref, gml_ref, womla_ref, womlstm_ref, wout_ref,
                 gffn_ref, wfin_ref, wfout_ref, gfin_ref, y_ref):
    mod = mod_ref[...]
    gml = gml_ref[...]
    tm = x_ref.shape[0]
    groups = [slice(r, r + TAIL_ROWS) for r in range(0, tm, TAIL_ROWS)]
    a = [_dot(att_ref[g, :], womla_ref[...]) for g in groups]
    hm = []
    for gi, g in enumerate(groups):
        hs = hst_ref[gi].T
        parts = []
        for hd in range(MLSTM_HEADS):
            sl = slice(hd * MLSTM_DV, (hd + 1) * MLSTM_DV)
            parts.append((_rms(hs[:, sl], gml[:, sl]) * smo_ref[g, sl].astype(F32)).astype(BF16))
        hm.append(jnp.concatenate(parts, axis=1))
    bm = [_dot(h, womlstm_ref[...]) for h in hm]
    merged = [(sbr_ref[g, :D_MODEL].astype(F32) * ai + sbr_ref[g, D_MODEL:].astype(F32) * bi).astype(BF16)
              for g, ai, bi in zip(groups, a, bm)]
    x1 = [x_ref[g, :] + mod[0:1] * _dot(mi, wout_ref[...]) for g, mi in zip(groups, merged)]
    h2 = [(_rms(xi, gffn_ref[...]) * (1.0 + mod[2:3]) + mod[1:2]).astype(BF16) for xi in x1]
    ga = [_dot(hi, wfin_ref[:, :FFN_HIDDEN]) for hi in h2]
    gu = [_dot(hi, wfin_ref[:, FFN_HIDDEN:]) for hi in h2]
    act = [(gi * _sigmoid(gi) * ui).astype(BF16) for gi, ui in zip(ga, gu)]
    for g, xi, ci in zip(groups, x1, act):
        y_ref[g, :] = _rms(xi + mod[3:4] * _dot(ci, wfout_ref[...]), gfin_ref[...])


def _tail(x, mod, att, hst, smo, sbr, gml, w_o_mla, w_o_mlstm, w_out, gffn, w_ffn_in, w_ffn_out, gfin, *, seq_len, name):
    n_tok = x.shape[0]
    tm = TOKEN_TILE
    tiles_per_seq = seq_len // tm
    assert hst.shape[0] * TAIL_ROWS == n_tok and hst.shape[2] == TAIL_ROWS
    tok = lambda c: pl.BlockSpec((tm, c), lambda i: (i, 0))
    consts = (gml, w_o_mla, w_o_mlstm, w_out, gffn, w_ffn_in, w_ffn_out, gfin)
    return pl.pallas_call(
        _tail_kernel,
        grid=(n_tok // tm,),
        in_specs=[tok(D_MODEL), _mod_spec(mod, tiles_per_seq), tok(att.shape[1]),
                  pl.BlockSpec((tm // TAIL_ROWS,) + hst.shape[1:], lambda i: (i, 0, 0)),
                  tok(smo.shape[1]), tok(sbr.shape[1])]
                 + [_const_spec(a.shape) for a in consts],
        out_specs=tok(D_MODEL),
        out_shape=jax.ShapeDtypeStruct((n_tok, D_MODEL), F32),
        compiler_params=_params(("parallel",)),
        name=name,
    )(x, mod, att, hst, smo, sbr, *consts)


_IN_SIZES = (Q_RANK, KV_RANK, ROPE_DIM, MLSTM_HEADS * MLSTM_DK, MLSTM_HEADS * MLSTM_DK, MLSTM_HEADS * MLSTM_DV,
             4 * MLSTM_HEADS, MLSTM_HEADS * MLSTM_DV, 2 * D_MODEL)
_IN_OFF = tuple(sum(_IN_SIZES[:i]) for i in range(len(_IN_SIZES) + 1))
PACK_ROWS = LANES
_PAD_AFTER = tuple((_SEG_OFF[i + 1] // PACK_ROWS, _SEG_WIDTHS[i] - _IN_SIZES[i], _IN_SIZES[i] % PACK_ROWS)
                   for i in range(len(_IN_SIZES)) if _SEG_WIDTHS[i] != _IN_SIZES[i])


PACK_PIECES = 8
N_PACK_STEPS = pl.cdiv(N_IN_PACKED // PACK_ROWS, PACK_PIECES)
_N_IN = _IN_OFF[-1]


def _pack_src_row(p):
    src = p * PACK_ROWS
    for first_piece_after, pad, _ in _PAD_AFTER:
        src = src - jnp.where(p >= first_piece_after, pad, 0)
    src = jnp.minimum(src, _N_IN - PACK_ROWS)
    return pl.multiple_of(src, math.gcd(PACK_ROWS, _N_IN, *(pad for _, pad, _ in _PAD_AFTER)))


def _pack_in_kernel(*refs):
    w_refs, o_ref = refs[:PACK_PIECES], refs[PACK_PIECES]
    row = lax.broadcasted_iota(jnp.int32, w_refs[0].shape, 0)
    for j, w_ref in enumerate(w_refs):
        p = pl.program_id(0) * PACK_PIECES + j
        valid = jnp.where(p < N_IN_PACKED // PACK_ROWS, PACK_ROWS, 0)
        for first_piece_after, _, width in _PAD_AFTER:
            valid = jnp.where(p == first_piece_after - 1, width, valid)
        o_ref[j * PACK_ROWS:(j + 1) * PACK_ROWS, :] = jnp.where(row < valid, w_ref[...], 0.0).astype(BF16)


def _pack_in(w_in_t):
    n, k = w_in_t.shape
    piece = lambda j: pl.BlockSpec((pl.Element(PACK_ROWS), pl.Element(k)),
                                   lambda i: (_pack_src_row(i * PACK_PIECES + j), 0))
    return pl.pallas_call(
        _pack_in_kernel,
        grid=(N_PACK_STEPS,),
        in_specs=[piece(j) for j in range(PACK_PIECES)],
        out_specs=pl.BlockSpec((PACK_PIECES * PACK_ROWS, k), lambda i: (i, 0)),
        out_shape=jax.ShapeDtypeStruct((N_PACK_STEPS * PACK_PIECES * PACK_ROWS, k), BF16),
        compiler_params=_params(("parallel",)),
        name="pack_in",
    )(*([w_in_t] * PACK_PIECES))


def _pack_small(w_uq, w_ukv, b_gates):
    qd = NOPE_DIM + ROPE_DIM
    w_uq_p = jnp.pad(w_uq.reshape(Q_RANK, MLA_HEADS, qd), ((0, 0), (0, 0), (0, HEAD_BLOCK - qd)))
    w_uq_p = w_uq_p.reshape(Q_RANK, MLA_HEADS * HEAD_BLOCK).astype(BF16)
    kv = w_ukv.reshape(KV_RANK, MLA_HEADS, NOPE_DIM + MLA_V_DIM)
    wk = jnp.pad(kv[:, :, :NOPE_DIM], ((0, 0), (0, 0), (0, HEAD_BLOCK - NOPE_DIM))).reshape(KV_RANK, -1)
    wv = kv[:, :, NOPE_DIM:].reshape(KV_RANK, -1)
    w_ukv_p = jnp.concatenate([wk, wv], axis=1).astype(BF16)
    bg = jnp.pad(b_gates, ((0, 0), (0, LANES - b_gates.shape[1])))
    return w_uq_p, w_ukv_p, bg


def _rope_tables(n_tokens):
    pos = np.arange(n_tokens)
    row = (pos // GRID_W).astype(np.float64)[:, None]
    col = (pos % GRID_W).astype(np.float64)[:, None]
    half = ROPE_DIM // 2
    inv = (np.float32(ROPE_BASE) ** (-np.arange(0, half, 2, dtype=np.float32) / np.float32(half))).astype(np.float64)
    r = np.arange(LANES) - ROPE_LANE0
    in_rope = (r >= 0) & (r < ROPE_DIM)
    rr = np.clip(r, 0, ROPE_DIM - 1)
    freq = inv[rr % (half // 2)][None, :]
    ang = np.where((rr // half == 0)[None, :], row * freq, col * freq).astype(np.float32).astype(np.float64)
    first = (rr % half) < (half // 2)
    cos = np.where(in_rope[None, :], np.cos(ang), 1.0)
    sin = np.sin(ang)
    sin_lo = np.where((in_rope & first)[None, :], -sin, 0.0)
    sin_hi = np.where((in_rope & ~first)[None, :], sin, 0.0)
    return tuple(jnp.asarray(t, dtype=F32) for t in (cos, sin_lo, sin_hi))


def kernel(x_prompt, x_sample, cache_ckv, cache_krope, state_C, state_n, state_m, c, c_ctx, w_mod, b_mod, g_norm_mix,
           w_in, b_gates, g_q_norm, w_uq, g_kv_norm, w_ukv, g_mlstm_norm, w_o_mla, w_o_mlstm, w_out, g_norm_ffn,
           w_ffn_in, w_ffn_out, g_final):
    bp, sp, d = x_prompt.shape
    bs, ss, _ = x_sample.shape
    layer = 0
    assert w_mod.shape[0] == 1 and sp == MLSTM_CHUNK and ss % MLSTM_CHUNK == 0

    cond = jnp.concatenate([c_ctx[None, :], c, jnp.zeros((8 - 1 - bs, d), F32)], axis=0)
    adaln_args = (cond, w_mod[layer], b_mod[layer][None, :])
    mod_in = _adaln(*adaln_args).reshape(8, N_MOD_MIXER, d)
    mod_ctx, mod_lat = mod_in[0:1], mod_in[1:1 + bs]

    w_in_p = _pack_in(w_in[layer].T)
    w_uq_p, w_ukv_p, bg = _pack_small(w_uq[layer], w_ukv[layer], b_gates[layer][None, :])
    row = lambda g: g[layer][None, :]
    shared_in = (row(g_norm_mix), w_in_p, row(g_q_norm), w_uq_p, row(g_kv_norm), w_ukv_p, bg)
    seq3 = lambda a, b_, s_: a.reshape(b_, s_, a.shape[-1])

    xl = x_sample.reshape(bs * ss, d)
    q, kc, v, mq_l, mk_l, mv_l, gates_l, smo_l, sbr_l = _inproj(xl, mod_lat, _rope_tables(ss), *shared_in, seq_len=ss)
    krope_blk = jnp.pad(cache_krope[:, layer], ((0, 0), (0, 0), (ROPE_LANE0, LANES - ROPE_LANE0 - ROPE_DIM)))
    cache = _cachekv(cache_ckv[:, layer], krope_blk, w_ukv_p)
    att_l, womla_b, womlstm_b, wout_b, wfin_b, wfout_b, mod_tail = _attention(
        seq3(q, bs, ss), seq3(kc, bs, ss), seq3(v, bs, ss), cache,
        cast_weights=(w_o_mla[layer], w_o_mlstm[layer], w_out[layer], w_ffn_in[layer], w_ffn_out[layer]),
        adaln_tail=adaln_args)
    tail_w = (row(g_mlstm_norm), womla_b, womlstm_b, wout_b, row(g_norm_ffn), wfin_b, wfout_b, g_final[None, :])
    mod_tail = mod_tail.reshape(8, N_MOD_TAIL, d)
    modt_ctx, modt_lat = mod_tail[0:1], mod_tail[1:1 + bs]

    xc = x_prompt.reshape(bp * sp, d)
    q, kc, v, mq, mk, mv, gates, smo, sbr, ckv, kpe = _inproj(xc, mod_ctx, None, *shared_in, seq_len=sp)
    att = _attention(seq3(q, bp, sp), seq3(kc, bp, sp), seq3(v, bp, sp), seqs_per_step=CTX_SEQS_PER_STEP)
    hs, new_c, new_n, new_m = _mlstm_ctx(seq3(mq, bp, sp), seq3(mk, bp, sp), seq3(mv, bp, sp), seq3(gates, bp, sp))
    y_prompt = _tail(xc, modt_ctx, att.reshape(bp * sp, -1), hs, smo, sbr, *tail_w,
                     seq_len=sp, name="tail_ctx").reshape(bp, sp, d)
    new_ckv = ckv.reshape(bp, 1, sp, KV_RANK)
    new_krope = kpe.reshape(bp, 1, sp, ROPE_DIM)
    new_C = new_c.reshape(bp, 1, 2, MLSTM_HEADS, MLSTM_DV, MLSTM_DK)
    new_N = new_n.reshape(bp, 1, 2, MLSTM_HEADS, MLSTM_DK)
    new_M = new_m[:, 0, :N_DIRHEAD].reshape(bp, 1, 2, MLSTM_HEADS)

    c0 = state_C[:, layer].reshape(bs, N_DIRHEAD, MLSTM_DV, MLSTM_DK)
    n0 = state_n[:, layer].reshape(bs, N_DIRHEAD, MLSTM_DK)
    m0 = jnp.pad(state_m[:, layer].reshape(bs, 1, N_DIRHEAD), ((0, 0), (0, 0), (0, LANES - N_DIRHEAD)))
    hs = _mlstm_lat(seq3(mq_l, bs, ss), seq3(mk_l, bs, ss), seq3(mv_l, bs, ss), seq3(gates_l, bs, ss), c0, n0, m0)
    y_sample = _tail(xl, modt_lat, att_l.reshape(bs * ss, -1), hs, smo_l, sbr_l, *tail_w,
                     seq_len=ss, name="tail_lat").reshape(bs, ss, d)
    return (y_prompt, y_sample, new_ckv, new_krope, new_C, new_N, new_M)
```

```python
import functools
import math

import jax
import jax.numpy as jnp
import numpy as np
from jax import lax
from jax.experimental import pallas as pl
from jax.experimental.pallas import tpu as pltpu

F32 = jnp.float32
BF16 = jnp.bfloat16

D_MODEL = 1024
GRID_W = 64
MLA_HEADS = 8
Q_RANK = 384
KV_RANK = 256
NOPE_DIM = 64
ROPE_DIM = 32
MLA_V_DIM = 64
ROPE_BASE = 10000.0
MLA_SCALE = (NOPE_DIM + ROPE_DIM) ** -0.5
MLSTM_HEADS = 4
MLSTM_DK = 128
MLSTM_DV = 256
FFN_HIDDEN = ((8 * D_MODEL // 3 + 255) // 256) * 256
EPS = 1e-6

LANES = 128
HEAD_BLOCK = LANES
ROPE_LANE0 = NOPE_DIM
N_DIRHEAD = 2 * MLSTM_HEADS
MLSTM_CHUNK = 256
LOG2E = math.log2(math.e)
Q_PRESCALE = MLA_SCALE * LOG2E
V_SLAB = 2 * LANES
V_WIDTH = (MLA_HEADS // 2) * V_SLAB
TOKEN_TILE = 512
TAIL_ROWS = 256
INPROJ_ROWS = TOKEN_TILE
Q_TILE = 256
CTX_SEQS_PER_STEP = 4
VMEM_LIMIT = 56 * 1024 * 1024

_SEG_WIDTHS = (Q_RANK, KV_RANK, LANES, MLSTM_HEADS * MLSTM_DK, MLSTM_HEADS * MLSTM_DK,
               MLSTM_HEADS * MLSTM_DV, LANES, MLSTM_HEADS * MLSTM_DV, 2 * D_MODEL)
_SEG_OFF = tuple(sum(_SEG_WIDTHS[:i]) for i in range(len(_SEG_WIDTHS) + 1))
SEG_Q, SEG_KV, SEG_KPE, SEG_MQ, SEG_MK, SEG_MV, SEG_GATE, SEG_MO, SEG_BR = (
    (_SEG_OFF[i], _SEG_OFF[i + 1]) for i in range(9))
N_IN_PACKED = _SEG_OFF[-1]


def _dot(a, b):
    return jnp.dot(a, b, preferred_element_type=F32)


def _dot_nt(a, b):
    return lax.dot_general(a, b, (((1,), (1,)), ((), ())), preferred_element_type=F32)


def _dot_tn(a, b):
    return lax.dot_general(a, b, (((0,), (0,)), ((), ())), preferred_element_type=F32)


def _rms(x, g):
    ms = jnp.mean(x * x, axis=-1, keepdims=True)
    return x * lax.rsqrt(ms + EPS) * g


def _sigmoid(x):
    return 1.0 / (1.0 + jnp.exp(-x))


def _const_spec(shape):
    nd = len(shape)
    return pl.BlockSpec(shape, lambda *_: (0,) * nd, pipeline_mode=pl.Buffered(1))


def _mod_chunks(mod_ref, row0, tiles_per_seq):
    row = row0 if tiles_per_seq is None else row0 + pl.program_id(0) // tiles_per_seq
    m = mod_ref[pl.ds(row, 1), :]
    return [m[:, k * D_MODEL:(k + 1) * D_MODEL] for k in range(m.shape[1] // D_MODEL)]


def _params(sem, flags=None, vmem_mib=None):
    limit = VMEM_LIMIT if vmem_mib is None else vmem_mib * 1024 * 1024
    return pltpu.CompilerParams(dimension_semantics=sem, vmem_limit_bytes=limit, flags=flags)


N_MOD_MIXER = 2
N_MOD_TAIL = 4


COND_ROWS = 8


def _adaln_block(cctx_ref, c_ref, w_ref, b_ref, o_ref):
    pad = jnp.zeros((COND_ROWS - 1 - c_ref.shape[0], D_MODEL), F32)
    c = jnp.concatenate([cctx_ref[...], c_ref[...], pad], axis=0)
    s = c * _sigmoid(c)
    o_ref[...] = _dot(s.astype(BF16), w_ref[...].astype(BF16)) + b_ref[...]


def _adaln(c_ctx, c, w_mod, b_mod):
    tn = D_MODEL
    return pl.pallas_call(
        _adaln_block,
        grid=(N_MOD_MIXER,),
        in_specs=[pl.BlockSpec(c_ctx.shape, lambda j: (0, 0)),
                  pl.BlockSpec(c.shape, lambda j: (0, 0)),
                  pl.BlockSpec((D_MODEL, tn), lambda j: (0, j)),
                  pl.BlockSpec((1, tn), lambda j: (0, j))],
        out_specs=pl.BlockSpec((COND_ROWS, tn), lambda j: (0, j)),
        out_shape=jax.ShapeDtypeStruct((COND_ROWS, N_MOD_MIXER * D_MODEL), F32),
        compiler_params=_params(("arbitrary",)),
        name="adaln",
    )(c_ctx, c, w_mod, b_mod)


def _rope_block(x, cos, sin_lo, sin_hi):
    return x * cos + pltpu.roll(x, LANES - 8, 1) * sin_lo + pltpu.roll(x, 8, 1) * sin_hi


def _store_kv(kvn, kp, kc_ref, v_ref, g=slice(None)):
    for hd in range(MLA_HEADS):
        sl = slice(hd * HEAD_BLOCK, (hd + 1) * HEAD_BLOCK)
        kc_ref[g, sl] = (kvn[:, sl] + kp).astype(BF16)
    v0 = MLA_HEADS * HEAD_BLOCK
    ones = jnp.ones((kvn.shape[0], LANES), BF16)
    for pair in range(MLA_HEADS // 2):
        v_ref[g, pair * V_SLAB:pair * V_SLAB + LANES] = kvn[:, v0 + pair * LANES:v0 + (pair + 1) * LANES].astype(BF16)
        v_ref[g, pair * V_SLAB + LANES:(pair + 1) * V_SLAB] = ones


def _inproj_kernel(*refs, rope, mod_row0, tiles_per_seq):
    if rope:
        (x_ref, mod_ref, cos_ref, slo_ref, shi_ref, gmix_ref, win_ref, gq_ref, wuq_ref, gkv_ref, wukv_ref,
         bg_ref, q_ref, kc_ref, v_ref, mq_ref, mk_ref, mv_ref, gate_ref, smo_ref, sbr_ref) = refs
    else:
        (x_ref, mod_ref, gmix_ref, win_ref, gq_ref, wuq_ref, gkv_ref, wukv_ref,
         bg_ref, q_ref, kc_ref, v_ref, mq_ref, mk_ref, mv_ref, gate_ref, smo_ref, sbr_ref,
         ckv_ref, kpet_ref) = refs
    shift, scale = _mod_chunks(mod_ref, mod_row0, tiles_per_seq)
    for r0 in range(0, x_ref.shape[0], INPROJ_ROWS):
        g = slice(r0, r0 + INPROJ_ROWS)
        if rope:
            cos, slo, shi = cos_ref[g, :], slo_ref[g, :], shi_ref[g, :]
        h = _rms(x_ref[g, :], gmix_ref[...]) * (1.0 + scale) + shift
        hb = h.astype(BF16)

        def proj(seg):
            return _dot_nt(hb, win_ref[seg[0]:seg[1], :])

        qn = _rms(proj(SEG_Q), gq_ref[...]).astype(BF16)
        ckv = _rms(proj(SEG_KV), gkv_ref[...])
        zkpe = proj(SEG_KPE)
        sbr_ref[g, :] = _sigmoid(proj(SEG_BR)).astype(BF16)
        smo_ref[g, :] = _sigmoid(proj(SEG_MO)).astype(BF16)
        gate_ref[g, :] = proj(SEG_GATE) + bg_ref[...]

        q = _dot(qn, wuq_ref[...])
        for hd in range(MLA_HEADS):
            sl = slice(hd * HEAD_BLOCK, (hd + 1) * HEAD_BLOCK)
            qh = q[:, sl]
            if rope:
                qh = _rope_block(qh, cos, slo, shi)
            q_ref[g, sl] = (qh * Q_PRESCALE).astype(BF16)

        if not rope:
            ckv_ref[g, :] = ckv
            kpe_t = zkpe.T
            seq = kpet_ref.shape[2]
            for sq in range(INPROJ_ROWS // seq):
                kpet_ref[r0 // seq + sq] = kpe_t[:ROPE_DIM, sq * seq:(sq + 1) * seq]
        kp = pltpu.roll(zkpe, ROPE_LANE0, 1)
        if rope:
            kp = _rope_block(kp, cos, slo, shi)
        kvn = _dot(ckv.astype(BF16), wukv_ref[...])
        _store_kv(kvn, kp, kc_ref, v_ref, g)

        mv_ref[g, :] = proj(SEG_MV).astype(BF16)
        mk_ref[g, :] = (proj(SEG_MK) * (MLSTM_DK ** -0.5)).astype(BF16)
        mq_ref[g, :] = proj(SEG_MQ).astype(BF16)


def _inproj(x, mod, rope_tabs, gmix, w_in_p, gq, w_uq_p, gkv, w_ukv_p, bg, *, seq_len):
    n_tok = x.shape[0]
    tm = TOKEN_TILE
    tiles_per_seq = seq_len // tm
    rope = rope_tabs is not None
    tok = lambda c: pl.BlockSpec((tm, c), lambda i: (i, 0))
    in_specs = [tok(D_MODEL), pl.BlockSpec(mod.shape, lambda i: (0, 0))]
    args = [x, mod]
    if rope:
        in_specs += [pl.BlockSpec((tm, LANES), lambda i: (i % tiles_per_seq, 0))] * 3
        args += list(rope_tabs)
    in_specs += [_const_spec(a.shape) for a in (gmix, w_in_p, gq, w_uq_p, gkv, w_ukv_p, bg)]
    args += [gmix, w_in_p, gq, w_uq_p, gkv, w_ukv_p, bg]
    widths = [(MLA_HEADS * HEAD_BLOCK, BF16), (MLA_HEADS * HEAD_BLOCK, BF16), (V_WIDTH, BF16),
              (MLSTM_HEADS * MLSTM_DK, BF16), (MLSTM_HEADS * MLSTM_DK, BF16), (MLSTM_HEADS * MLSTM_DV, BF16),
              (LANES, F32), (MLSTM_HEADS * MLSTM_DV, BF16), (2 * D_MODEL, BF16)]
    out_specs = [tok(c) for c, _ in widths]
    out_shape = [jax.ShapeDtypeStruct((n_tok, c), dt) for c, dt in widths]
    if not rope:
        out_specs += [tok(KV_RANK), pl.BlockSpec((tm // seq_len, ROPE_DIM, seq_len), lambda i: (i, 0, 0))]
        out_shape += [jax.ShapeDtypeStruct((n_tok, KV_RANK), F32),
                      jax.ShapeDtypeStruct((n_tok // seq_len, ROPE_DIM, seq_len), F32)]
    return pl.pallas_call(
        functools.partial(_inproj_kernel, rope=rope, mod_row0=1 if rope else 0,
                          tiles_per_seq=tiles_per_seq if rope else None),
        grid=(n_tok // tm,),
        in_specs=in_specs,
        out_specs=out_specs,
        out_shape=out_shape,
        compiler_params=_params(("parallel",)),
        name="inproj_lat" if rope else "inproj_ctx",
    )(*args)


def _cachekv_kernel(ckv_ref, kp_ref, wukv_ref, kc_ref, v_ref):
    kvn = _dot(ckv_ref[...].astype(BF16), wukv_ref[...])
    _store_kv(kvn, kp_ref[...], kc_ref, v_ref)


def _cachekv(cache_ckv, krope_blk, w_ukv_p):
    b, past, _ = cache_ckv.shape
    return pl.pallas_call(
        _cachekv_kernel,
        grid=(b,),
        in_specs=[pl.BlockSpec((None, past, KV_RANK), lambda i: (i, 0, 0)),
                  pl.BlockSpec((None, past, LANES), lambda i: (i, 0, 0)),
                  _const_spec(w_ukv_p.shape)],
        out_specs=[pl.BlockSpec((None, past, MLA_HEADS * HEAD_BLOCK), lambda i: (i, 0, 0)),
                   pl.BlockSpec((None, past, V_WIDTH), lambda i: (i, 0, 0))],
        out_shape=[jax.ShapeDtypeStruct((b, past, MLA_HEADS * HEAD_BLOCK), BF16),
                   jax.ShapeDtypeStruct((b, past, V_WIDTH), BF16)],
        compiler_params=_params(("parallel",)),
        name="cachekv",
    )(cache_ckv, krope_blk, w_ukv_p)


def _attn_kernel(*refs, has_cache, n_cast, has_adaln):
    n_in = 5 if has_cache else 3
    if has_cache:
        q_ref, k_ref, v_ref, kc_ref, vc_ref = refs[:n_in]
    else:
        q_ref, k_ref, v_ref = refs[:n_in]
    n_all_in = n_in + n_cast + (4 if has_adaln else 0)
    o_ref = refs[n_all_in]
    for w_ref, wb_ref in zip(refs[n_in:n_in + n_cast], refs[n_all_in + 1:n_all_in + 1 + n_cast]):
        wb_ref[...] = w_ref[...].astype(BF16)
    if has_adaln:
        _adaln_block(*refs[n_in + n_cast:n_all_in], refs[n_all_in + 1 + n_cast])
    tq = q_ref.shape[1]
    lane = lax.broadcasted_iota(jnp.int32, (tq, LANES), 1)
    for sq in range(q_ref.shape[0]):
        hsl = [slice(hd * HEAD_BLOCK, (hd + 1) * HEAD_BLOCK) for hd in range(MLA_HEADS)]
        vsl = [slice(hd // 2 * V_SLAB, (hd // 2 + 1) * V_SLAB) for hd in range(MLA_HEADS)]
        s = [_dot_nt(q_ref[sq, :, sl], k_ref[sq, :, sl]) for sl in hsl]
        m = [jnp.max(si, axis=-1, keepdims=True) for si in s]
        if has_cache:
            sc = [_dot_nt(q_ref[sq, :, sl], kc_ref[sq, :, sl]) for sl in hsl]
            m = [jnp.maximum(mi, jnp.max(ci, axis=-1, keepdims=True)) for mi, ci in zip(m, sc)]
        p = [jnp.exp2(si - mi).astype(BF16) for si, mi in zip(s, m)]
        o = [_dot(pi, v_ref[sq, :, sl]) for pi, sl in zip(p, vsl)]
        if has_cache:
            pc = [jnp.exp2(ci - mi).astype(BF16) for ci, mi in zip(sc, m)]
            o = [oi + _dot(pi, vc_ref[sq, :, sl]) for oi, pi, sl in zip(o, pc, vsl)]
        outs = [oi[:, :LANES] * (1.0 / oi[:, LANES:]) for oi in o]
        for pair in range(MLA_HEADS // 2):
            o_ref[sq, :, pair * LANES:(pair + 1) * LANES] = jnp.where(
                lane < MLA_V_DIM, outs[2 * pair], outs[2 * pair + 1]).astype(BF16)


def _attention(q, k, v, cache=None, cast_weights=(), adaln_tail=None, seqs_per_step=1):
    nb, s, _ = q.shape
    tq = Q_TILE
    ns = seqs_per_step
    b = nb // ns
    n_steps = b * (s // tq)
    kw, vw, ow = MLA_HEADS * HEAD_BLOCK, V_WIDTH, MLA_HEADS * MLA_V_DIM
    in_specs = [pl.BlockSpec((ns, tq, kw), lambda i, j: (i, j, 0)),
                pl.BlockSpec((ns, s, kw), lambda i, j: (i, 0, 0)),
                pl.BlockSpec((ns, s, vw), lambda i, j: (i, 0, 0))]
    args = [q, k, v]
    if cache is not None:
        past = cache[0].shape[1]
        in_specs += [pl.BlockSpec((ns, past, kw), lambda i, j: (i, 0, 0)),
                     pl.BlockSpec((ns, past, vw), lambda i, j: (i, 0, 0))]
        args += list(cache)
    step = lambda i, j: i * (s // tq) + j
    w_specs = [pl.BlockSpec((w.shape[0] // n_steps, w.shape[1]), lambda i, j: (step(i, j), 0)) for w in cast_weights]
    extra_in, extra_out, extra_shape = [], [], []
    if adaln_tail is not None:
        tn = N_MOD_TAIL * D_MODEL // n_steps
        col0 = N_MOD_MIXER * D_MODEL // tn
        extra_in = [pl.BlockSpec(adaln_tail[0].shape, lambda i, j: (0, 0)),
                    pl.BlockSpec(adaln_tail[1].shape, lambda i, j: (0, 0)),
                    pl.BlockSpec((D_MODEL, tn), lambda i, j: (0, col0 + step(i, j))),
                    pl.BlockSpec((1, tn), lambda i, j: (0, col0 + step(i, j)))]
        extra_out = [pl.BlockSpec((COND_ROWS, tn), lambda i, j: (0, step(i, j)))]
        extra_shape = [jax.ShapeDtypeStruct((COND_ROWS, N_MOD_TAIL * D_MODEL), F32)]
    outs = pl.pallas_call(
        functools.partial(_attn_kernel, has_cache=cache is not None, n_cast=len(cast_weights),
                          has_adaln=adaln_tail is not None),
        grid=(b, s // tq),
        in_specs=in_specs + w_specs + extra_in,
        out_specs=[pl.BlockSpec((ns, tq, ow), lambda i, j: (i, j, 0))] + w_specs + extra_out,
        out_shape=[jax.ShapeDtypeStruct((nb, s, ow), BF16)]
                  + [jax.ShapeDtypeStruct(w.shape, BF16) for w in cast_weights] + extra_shape,
        compiler_params=_params(("parallel", "arbitrary")),
        name="attn_lat" if cache is not None else "attn_ctx",
    )(*args, *cast_weights, *(adaln_tail or ()))
    return outs[0] if len(outs) == 1 else outs


def _lane_bcast(x, k, width=None):
    y = jnp.broadcast_to(x[:, k:k + 1], x.shape)
    reps = (width or LANES) // LANES
    return y if reps == 1 else jnp.concatenate([y] * reps, axis=1)


def _prefix_sum_rows(x):
    n = x.shape[0]
    r = lax.broadcasted_iota(jnp.int32, (n, n), 0)
    c = lax.broadcasted_iota(jnp.int32, (n, n), 1)
    tri = jnp.where(c <= r, 1.0, 0.0).astype(BF16)
    hi = x.astype(BF16)
    r1 = x - hi.astype(F32)
    mid = r1.astype(BF16)
    lo = (r1 - mid.astype(F32)).astype(BF16)
    return _dot(tri, hi) + _dot(tri, mid) + _dot(tri, lo)


def _gate_prep(g):
    n = g.shape[0]
    lane = lax.broadcasted_iota(jnp.int32, (n, LANES), 1)
    fpre = pltpu.roll(g, LANES - N_DIRHEAD, 1)
    lf = jnp.minimum(fpre, 0.0) - jnp.log(1.0 + jnp.exp(-jnp.abs(fpre)))
    binc = _prefix_sum_rows(lf)
    tot = binc[n - 1:n, :]
    b = jnp.where(lane < MLSTM_HEADS, binc, tot - binc + lf)
    return b, g - b, tot


def _chunk_setup(b, u, tot, m_prev):
    g_last = jnp.maximum(jnp.max(u, axis=0, keepdims=True), m_prev)
    u2 = u * LOG2E
    w = jnp.exp2(u2 - g_last * LOG2E)
    return u2, (b * LOG2E).T, w.T, tot + g_last, jnp.exp(m_prev - g_last)


def _state_update(vt, k, wt, kk):
    c_new = _dot((vt * wt[kk:kk + 1, :]).astype(BF16), k)
    n_new = _dot(wt[:N_DIRHEAD, :].astype(BF16), k)[kk:kk + 1, :]
    return c_new, n_new


def _mlstm_dir_t(s0t, vt, u_b, bt_row, m_row, *, fwd, inter=None):
    sdt, den, g = _dir_weights(s0t, u_b, m_row, fwd=fwd)
    return _dir_finish(_dot(vt, sdt), den, g, bt_row, m_row, inter)


def _dir_weights(s0t, u_b, m_row, *, fwd):
    n = s0t.shape[0]
    r = lax.broadcasted_iota(jnp.int32, (n, n), 0)
    c = lax.broadcasted_iota(jnp.int32, (n, n), 1)
    a = jnp.where((r <= c) if fwd else (r >= c), u_b, -jnp.inf)
    g = jnp.maximum(jnp.max(a, axis=0, keepdims=True), m_row)
    sdt = s0t * jnp.exp2(a - g)
    return sdt.astype(BF16), jnp.sum(sdt, axis=0, keepdims=True), g


def _dir_finish(ht, den, g, bt_row, m_row, inter=None):
    if inter is not None:
        w_inter = jnp.exp2(m_row - g)
        ht = ht + w_inter * inter[0]
        den = den + w_inter * inter[1]
    floor = jnp.exp2(-(bt_row + g))
    return ht * (1.0 / jnp.maximum(jnp.abs(den), floor))


def _mlstm_ctx_kernel(q_ref, k_ref, v_ref, g_ref, ht_ref, c_ref, n_ref, m_ref):
    n = q_ref.shape[1]
    zero_lanes = jnp.zeros((1, LANES), F32)
    zero_row = jnp.zeros((1, n), F32)
    dirs = ((True, 0), (False, MLSTM_HEADS))
    setup = [_chunk_setup(*_gate_prep(g_ref[sq]), zero_lanes) for sq in range(q_ref.shape[0])]
    units = [(sq, hd) for sq in range(q_ref.shape[0]) for hd in range(MLSTM_HEADS)]
    ks = [k_ref[sq, :, hd * MLSTM_DK:(hd + 1) * MLSTM_DK] for sq, hd in units]
    s0ts = [_dot_nt(k, q_ref[sq, :, hd * MLSTM_DK:(hd + 1) * MLSTM_DK]) for k, (sq, hd) in zip(ks, units)]
    vts = [v_ref[sq, :, hd * MLSTM_DV:(hd + 1) * MLSTM_DV].T for sq, hd in units]
    wts = [[_dir_weights(s0t, _lane_bcast(setup[sq][0], off + hd, n), zero_row, fwd=fwd) for fwd, off in dirs]
           for s0t, (sq, hd) in zip(s0ts, units)]
    hts = [[_dot(vt, sdt) for sdt, _, _ in wt] for vt, wt in zip(vts, wts)]
    for (sq, hd), k, vt, wt, ht2 in zip(units, ks, vts, wts, hts):
        _, bt2, w_t, _, _ = setup[sq]
        ht_ref[sq, hd * MLSTM_DV:(hd + 1) * MLSTM_DV, :] = sum(
            _dir_finish(ht_d, den, g, bt2[off + hd:off + hd + 1, :], zero_row)
            for ht_d, (_, den, g), (_, off) in zip(ht2, wt, dirs))
        for _, off in dirs:
            c_ref[sq, off + hd], n_ref[sq, off + hd:off + hd + 1, :] = _state_update(vt, k, w_t, off + hd)
    for sq in range(q_ref.shape[0]):
        m_ref[sq] = setup[sq][3]


def _mlstm_ctx(mq, mk, mv, gates):
    b, s, _ = mq.shape
    ns = CTX_SEQS_PER_STEP
    seq = lambda c: pl.BlockSpec((ns, s, c), lambda i: (i, 0, 0))
    return pl.pallas_call(
        _mlstm_ctx_kernel,
        grid=(b // ns,),
        in_specs=[seq(MLSTM_HEADS * MLSTM_DK), seq(MLSTM_HEADS * MLSTM_DK), seq(MLSTM_HEADS * MLSTM_DV), seq(LANES)],
        out_specs=[pl.BlockSpec((ns, MLSTM_HEADS * MLSTM_DV, s), lambda i: (i, 0, 0)),
                   pl.BlockSpec((ns, N_DIRHEAD, MLSTM_DV, MLSTM_DK), lambda i: (i, 0, 0, 0)),
                   pl.BlockSpec((ns, N_DIRHEAD, MLSTM_DK), lambda i: (i, 0, 0)),
                   pl.BlockSpec((ns, 1, LANES), lambda i: (i, 0, 0))],
        out_shape=[jax.ShapeDtypeStruct((b, MLSTM_HEADS * MLSTM_DV, s), F32),
                   jax.ShapeDtypeStruct((b, N_DIRHEAD, MLSTM_DV, MLSTM_DK), F32),
                   jax.ShapeDtypeStruct((b, N_DIRHEAD, MLSTM_DK), F32),
                   jax.ShapeDtypeStruct((b, 1, LANES), F32)],
        compiler_params=_params(("parallel",)),
        name="mlstm_ctx",
    )(mq, mk, mv, gates)


def _mlstm_lat_kernel(qf_ref, kf_ref, vf_ref, gf_ref, qb_ref, kb_ref, vb_ref, gb_ref, c0_ref, n0_ref, m0_ref,
                      ht_ref, c_ref, n_ref, m_ref, *, n_chunks):
    step = pl.program_id(1)

    @pl.when(step == 0)
    def _():
        c_ref[...] = c0_ref[...]
        n_ref[...] = n0_ref[...]
        m_ref[...] = m0_ref[...]
        ht_ref[...] = jnp.zeros_like(ht_ref)

    n = qf_ref.shape[0]
    gate_f, gate_b = _gate_prep(gf_ref[...]), _gate_prep(gb_ref[...])
    is_f = lax.broadcasted_iota(jnp.int32, (n, LANES), 1) < MLSTM_HEADS
    b, u, tot = (jnp.where(is_f[:x.shape[0]], x, y) for x, y in zip(gate_f, gate_b))
    m_prev = m_ref[...]
    u2, bt2, w_t, m_new, decay = _chunk_setup(b, u, tot, m_prev)
    m2_prev = m_prev * LOG2E
    units = [(fwd, hd, hd if fwd else MLSTM_HEADS + hd) for fwd in (True, False) for hd in range(MLSTM_HEADS)]
    refs = {True: (qf_ref, kf_ref, vf_ref), False: (qb_ref, kb_ref, vb_ref)}
    qs = [refs[fwd][0][:, hd * MLSTM_DK:(hd + 1) * MLSTM_DK] for fwd, hd, _ in units]
    ks = [refs[fwd][1][:, hd * MLSTM_DK:(hd + 1) * MLSTM_DK] for fwd, hd, _ in units]
    vts = [refs[fwd][2][:, hd * MLSTM_DV:(hd + 1) * MLSTM_DV].T for fwd, hd, _ in units]
    s0ts = [_dot_nt(k, q) for k, q in zip(ks, qs)]
    c_prevs = [c_ref[kk] for _, _, kk in units]
    n_prevs = [n_ref[kk:kk + 1, :] for _, _, kk in units]
    inters = [(_dot_nt(c.astype(BF16), q), _dot_nt(jnp.broadcast_to(nv, (8, MLSTM_DK)).astype(BF16), q)[0:1, :])
              for c, nv, q in zip(c_prevs, n_prevs, qs)]
    m_rows = [_lane_bcast(m2_prev, kk, n) for _, _, kk in units]
    wts = [_dir_weights(s0t, _lane_bcast(u2, kk, n), m_row, fwd=fwd)
           for s0t, m_row, (fwd, _, kk) in zip(s0ts, m_rows, units)]
    hts = [_dot(vt, sdt) for vt, (sdt, _, _) in zip(vts, wts)]
    for (fwd, hd, kk), k, vt, (_, den, g), ht, inter, m_row, c_prev, n_prev in zip(
            units, ks, vts, wts, hts, inters, m_rows, c_prevs, n_prevs):
        chunk = step if fwd else n_chunks - 1 - step
        ht_ref[chunk, hd * MLSTM_DV:(hd + 1) * MLSTM_DV, :] += _dir_finish(ht, den, g, bt2[kk:kk + 1, :], m_row, inter)
        c_new, n_new = _state_update(vt, k, w_t, kk)
        dk = _lane_bcast(decay, kk)
        c_ref[kk] = dk * c_prev + c_new
        n_ref[kk:kk + 1, :] = dk * n_prev + n_new
    m_ref[...] = m_new


def _mlstm_lat(mq, mk, mv, gates, c0, n0, m0):
    b, s, _ = mq.shape
    lc = MLSTM_CHUNK
    nc = s // lc
    fw = lambda c: pl.BlockSpec((None, lc, c), lambda i, j: (i, j, 0))
    bw = lambda c: pl.BlockSpec((None, lc, c), lambda i, j: (i, nc - 1 - j, 0))
    dk, dv = MLSTM_HEADS * MLSTM_DK, MLSTM_HEADS * MLSTM_DV
    return pl.pallas_call(
        functools.partial(_mlstm_lat_kernel, n_chunks=nc),
        grid=(b, nc),
        in_specs=[fw(dk), fw(dk), fw(dv), fw(LANES), bw(dk), bw(dk), bw(dv), bw(LANES),
                  pl.BlockSpec((None, N_DIRHEAD, MLSTM_DV, MLSTM_DK), lambda i, j: (i, 0, 0, 0)),
                  pl.BlockSpec((None, N_DIRHEAD, MLSTM_DK), lambda i, j: (i, 0, 0)),
                  pl.BlockSpec((None, 1, LANES), lambda i, j: (i, 0, 0))],
        out_specs=pl.BlockSpec((nc, dv, lc), lambda i, j: (i, 0, 0)),
        out_shape=jax.ShapeDtypeStruct((b * nc, dv, lc), F32),
        scratch_shapes=[pltpu.VMEM((N_DIRHEAD, MLSTM_DV, MLSTM_DK), F32),
                        pltpu.VMEM((N_DIRHEAD, MLSTM_DK), F32),
                        pltpu.VMEM((1, LANES), F32)],
        compiler_params=_params(("parallel", "arbitrary")),
        name="mlstm_lat",
    )(mq, mk, mv, gates, mq, mk, mv, gates, c0, n0, m0)


def _tail_kernel(x_ref, mod_ref, att_ref, hst_ref, smo_ref, sbr_ref, gml_ref, womla_ref, womlstm_ref, wout_ref,
                 gffn_ref, wfin_ref, wfout_ref, gfin_ref, y_ref, *, mod_row0, tiles_per_seq):
    gate_mix, shift_ffn, scale_ffn, gate_ffn = _mod_chunks(mod_ref, mod_row0, tiles_per_seq)
    gml = gml_ref[...]
    tm = x_ref.shape[0]
    groups = [slice(r, r + TAIL_ROWS) for r in range(0, tm, TAIL_ROWS)]
    a = [_dot(att_ref[g, :], womla_ref[...]) for g in groups]
    hm = []
    for gi, g in enumerate(groups):
        hs = hst_ref[gi].T
        parts = []
        for hd in range(MLSTM_HEADS):
            sl = slice(hd * MLSTM_DV, (hd + 1) * MLSTM_DV)
            parts.append((_rms(hs[:, sl], gml[:, sl]) * smo_ref[g, sl].astype(F32)).astype(BF16))
        hm.append(jnp.concatenate(parts, axis=1))
    bm = [_dot(h, womlstm_ref[...]) for h in hm]
    merged = [(sbr_ref[g, :D_MODEL].astype(F32) * ai + sbr_ref[g, D_MODEL:].astype(F32) * bi).astype(BF16)
              for g, ai, bi in zip(groups, a, bm)]
    x1 = [x_ref[g, :] + gate_mix * _dot(mi, wout_ref[...]) for g, mi in zip(groups, merged)]
    h2 = [(_rms(xi, gffn_ref[...]) * (1.0 + scale_ffn) + shift_ffn).astype(BF16) for xi in x1]
    ga = [_dot(hi, wfin_ref[:, :FFN_HIDDEN]) for hi in h2]
    gu = [_dot(hi, wfin_ref[:, FFN_HIDDEN:]) for hi in h2]
    act = [(gi * _sigmoid(gi) * ui).astype(BF16) for gi, ui in zip(ga, gu)]
    for g, xi, ci in zip(groups, x1, act):
        y_ref[g, :] = _rms(xi + gate_ffn * _dot(ci, wfout_ref[...]), gfin_ref[...])


def _tail(x, mod, att, hst, smo, sbr, gml, w_o_mla, w_o_mlstm, w_out, gffn, w_ffn_in, w_ffn_out, gfin, *,
          seq_len, latent):
    n_tok = x.shape[0]
    tm = TOKEN_TILE
    assert hst.shape[0] * TAIL_ROWS == n_tok and hst.shape[2] == TAIL_ROWS
    tok = lambda c: pl.BlockSpec((tm, c), lambda i: (i, 0))
    consts = (gml, w_o_mla, w_o_mlstm, w_out, gffn, w_ffn_in, w_ffn_out, gfin)
    return pl.pallas_call(
        functools.partial(_tail_kernel, mod_row0=1 if latent else 0, tiles_per_seq=seq_len // tm if latent else None),
        grid=(n_tok // tm,),
        in_specs=[tok(D_MODEL), pl.BlockSpec(mod.shape, lambda i: (0, 0)), tok(att.shape[1]),
                  pl.BlockSpec((tm // TAIL_ROWS,) + hst.shape[1:], lambda i: (i, 0, 0)),
                  tok(smo.shape[1]), tok(sbr.shape[1])]
                 + [_const_spec(a.shape) for a in consts],
        out_specs=tok(D_MODEL),
        out_shape=jax.ShapeDtypeStruct((n_tok, D_MODEL), F32),
        compiler_params=_params(("parallel",)),
        name="tail_lat" if latent else "tail_ctx",
    )(x, mod, att, hst, smo, sbr, *consts)


_IN_SIZES = (Q_RANK, KV_RANK, ROPE_DIM, MLSTM_HEADS * MLSTM_DK, MLSTM_HEADS * MLSTM_DK, MLSTM_HEADS * MLSTM_DV,
             4 * MLSTM_HEADS, MLSTM_HEADS * MLSTM_DV, 2 * D_MODEL)
_IN_OFF = tuple(sum(_IN_SIZES[:i]) for i in range(len(_IN_SIZES) + 1))
PACK_ROWS = LANES
_PAD_AFTER = tuple((_SEG_OFF[i + 1] // PACK_ROWS, _SEG_WIDTHS[i] - _IN_SIZES[i], _IN_SIZES[i] % PACK_ROWS)
                   for i in range(len(_IN_SIZES)) if _SEG_WIDTHS[i] != _IN_SIZES[i])


PACK_PIECES = 8
N_PACK_STEPS = pl.cdiv(N_IN_PACKED // PACK_ROWS, PACK_PIECES)
_N_IN = _IN_OFF[-1]


def _pack_src_row(p):
    src = p * PACK_ROWS
    for first_piece_after, pad, _ in _PAD_AFTER:
        src = src - jnp.where(p >= first_piece_after, pad, 0)
    src = jnp.minimum(src, _N_IN - PACK_ROWS)
    return pl.multiple_of(src, math.gcd(PACK_ROWS, _N_IN, *(pad for _, pad, _ in _PAD_AFTER)))


def _pack_in_kernel(*refs):
    w_refs, o_ref = refs[:PACK_PIECES], refs[PACK_PIECES]
    row = lax.broadcasted_iota(jnp.int32, w_refs[0].shape, 0)
    for j, w_ref in enumerate(w_refs):
        p = pl.program_id(0) * PACK_PIECES + j
        valid = jnp.where(p < N_IN_PACKED // PACK_ROWS, PACK_ROWS, 0)
        for first_piece_after, _, width in _PAD_AFTER:
            valid = jnp.where(p == first_piece_after - 1, width, valid)
        o_ref[j * PACK_ROWS:(j + 1) * PACK_ROWS, :] = jnp.where(row < valid, w_ref[...], 0.0).astype(BF16)


def _pack_in(w_in_t):
    n, k = w_in_t.shape
    piece = lambda j: pl.BlockSpec((pl.Element(PACK_ROWS), pl.Element(k)),
                                   lambda i: (_pack_src_row(i * PACK_PIECES + j), 0))
    return pl.pallas_call(
        _pack_in_kernel,
        grid=(N_PACK_STEPS,),
        in_specs=[piece(j) for j in range(PACK_PIECES)],
        out_specs=pl.BlockSpec((PACK_PIECES * PACK_ROWS, k), lambda i: (i, 0)),
        out_shape=jax.ShapeDtypeStruct((N_PACK_STEPS * PACK_PIECES * PACK_ROWS, k), BF16),
        compiler_params=_params(("parallel",)),
        name="pack_in",
    )(*([w_in_t] * PACK_PIECES))


def _pack_small(w_uq, w_ukv, b_gates):
    qd = NOPE_DIM + ROPE_DIM
    w_uq_p = jnp.pad(w_uq.reshape(Q_RANK, MLA_HEADS, qd), ((0, 0), (0, 0), (0, HEAD_BLOCK - qd)))
    w_uq_p = w_uq_p.reshape(Q_RANK, MLA_HEADS * HEAD_BLOCK).astype(BF16)
    kv = w_ukv.reshape(KV_RANK, MLA_HEADS, NOPE_DIM + MLA_V_DIM)
    wk = jnp.pad(kv[:, :, :NOPE_DIM], ((0, 0), (0, 0), (0, HEAD_BLOCK - NOPE_DIM))).reshape(KV_RANK, -1)
    wv = kv[:, :, NOPE_DIM:].reshape(KV_RANK, -1)
    w_ukv_p = jnp.concatenate([wk, wv], axis=1).astype(BF16)
    bg = jnp.pad(b_gates, ((0, 0), (0, LANES - b_gates.shape[1])))
    return w_uq_p, w_ukv_p, bg


def _rope_tables(n_tokens):
    pos = np.arange(n_tokens)
    row = (pos // GRID_W).astype(np.float64)[:, None]
    col = (pos % GRID_W).astype(np.float64)[:, None]
    half = ROPE_DIM // 2
    inv = (np.float32(ROPE_BASE) ** (-np.arange(0, half, 2, dtype=np.float32) / np.float32(half))).astype(np.float64)
    r = np.arange(LANES) - ROPE_LANE0
    in_rope = (r >= 0) & (r < ROPE_DIM)
    rr = np.clip(r, 0, ROPE_DIM - 1)
    freq = inv[rr % (half // 2)][None, :]
    ang = np.where((rr // half == 0)[None, :], row * freq, col * freq).astype(np.float32).astype(np.float64)
    first = (rr % half) < (half // 2)
    cos = np.where(in_rope[None, :], np.cos(ang), 1.0)
    sin = np.sin(ang)
    sin_lo = np.where((in_rope & first)[None, :], -sin, 0.0)
    sin_hi = np.where((in_rope & ~first)[None, :], sin, 0.0)
    return tuple(jnp.asarray(t, dtype=F32) for t in (cos, sin_lo, sin_hi))


def kernel(x_prompt, x_sample, cache_ckv, cache_krope, state_C, state_n, state_m, c, c_ctx, w_mod, b_mod, g_norm_mix,
           w_in, b_gates, g_q_norm, w_uq, g_kv_norm, w_ukv, g_mlstm_norm, w_o_mla, w_o_mlstm, w_out, g_norm_ffn,
           w_ffn_in, w_ffn_out, g_final):
    bp, sp, d = x_prompt.shape
    bs, ss, _ = x_sample.shape
    layer = 0
    assert w_mod.shape[0] == 1 and sp == MLSTM_CHUNK and ss % MLSTM_CHUNK == 0

    adaln_args = (c_ctx[None, :], c, w_mod[layer], b_mod[layer][None, :])
    mod_in = _adaln(*adaln_args)

    w_in_p = _pack_in(w_in[layer].T)
    w_uq_p, w_ukv_p, bg = _pack_small(w_uq[layer], w_ukv[layer], b_gates[layer][None, :])
    row = lambda g: g[layer][None, :]
    shared_in = (row(g_norm_mix), w_in_p, row(g_q_norm), w_uq_p, row(g_kv_norm), w_ukv_p, bg)
    seq3 = lambda a, b_, s_: a.reshape(b_, s_, a.shape[-1])

    xl = x_sample.reshape(bs * ss, d)
    q, kc, v, mq_l, mk_l, mv_l, gates_l, smo_l, sbr_l = _inproj(xl, mod_in, _rope_tables(ss), *shared_in, seq_len=ss)
    krope_blk = jnp.pad(cache_krope[:, layer], ((0, 0), (0, 0), (ROPE_LANE0, LANES - ROPE_LANE0 - ROPE_DIM)))
    cache = _cachekv(cache_ckv[:, layer], krope_blk, w_ukv_p)
    att_l, womla_b, womlstm_b, wout_b, wfin_b, wfout_b, mod_tail = _attention(
        seq3(q, bs, ss), seq3(kc, bs, ss), seq3(v, bs, ss), cache,
        cast_weights=(w_o_mla[layer], w_o_mlstm[layer], w_out[layer], w_ffn_in[layer], w_ffn_out[layer]),
        adaln_tail=adaln_args)
    tail_w = (row(g_mlstm_norm), womla_b, womlstm_b, wout_b, row(g_norm_ffn), wfin_b, wfout_b, g_final[None, :])

    xc = x_prompt.reshape(bp * sp, d)
    q, kc, v, mq, mk, mv, gates, smo, sbr, ckv, kpe = _inproj(xc, mod_in, None, *shared_in, seq_len=sp)
    att = _attention(seq3(q, bp, sp), seq3(kc, bp, sp), seq3(v, bp, sp), seqs_per_step=CTX_SEQS_PER_STEP)
    hs, new_c, new_n, new_m = _mlstm_ctx(seq3(mq, bp, sp), seq3(mk, bp, sp), seq3(mv, bp, sp), seq3(gates, bp, sp))
    y_prompt = _tail(xc, mod_tail, att.reshape(bp * sp, -1), hs, smo, sbr, *tail_w,
                     seq_len=sp, latent=False).reshape(bp, sp, d)
    new_ckv = ckv.reshape(bp, 1, sp, KV_RANK)
    new_krope = jnp.swapaxes(kpe, 1, 2).reshape(bp, 1, sp, ROPE_DIM)
    new_C = new_c.reshape(bp, 1, 2, MLSTM_HEADS, MLSTM_DV, MLSTM_DK)
    new_N = new_n.reshape(bp, 1, 2, MLSTM_HEADS, MLSTM_DK)
    new_M = new_m[:, 0, :N_DIRHEAD].reshape(bp, 1, 2, MLSTM_HEADS)

    c0 = state_C[:, layer].reshape(bs, N_DIRHEAD, MLSTM_DV, MLSTM_DK)
    n0 = state_n[:, layer].reshape(bs, N_DIRHEAD, MLSTM_DK)
    m0 = jnp.pad(state_m[:, layer].reshape(bs, 1, N_DIRHEAD), ((0, 0), (0, 0), (0, LANES - N_DIRHEAD)))
    hs = _mlstm_lat(seq3(mq_l, bs, ss), seq3(mk_l, bs, ss), seq3(mv_l, bs, ss), seq3(gates_l, bs, ss), c0, n0, m0)
    y_sample = _tail(xl, mod_tail, att_l.reshape(bs * ss, -1), hs, smo_l, sbr_l, *tail_w,
                     seq_len=ss, latent=True).reshape(bs, ss, d)
    return (y_prompt, y_sample, new_ckv, new_krope, new_C, new_N, new_M)
```

```python
import functools
import math

import jax
import jax.numpy as jnp
import numpy as np
from jax import lax
from jax.experimental import pallas as pl
from jax.experimental.pallas import tpu as pltpu

F32 = jnp.float32
BF16 = jnp.bfloat16

D_MODEL = 1024
GRID_W = 64
MLA_HEADS = 8
Q_RANK = 384
KV_RANK = 256
NOPE_DIM = 64
ROPE_DIM = 32
MLA_V_DIM = 64
ROPE_BASE = 10000.0
MLA_SCALE = (NOPE_DIM + ROPE_DIM) ** -0.5
MLSTM_HEADS = 4
MLSTM_DK = 128
MLSTM_DV = 256
FFN_HIDDEN = ((8 * D_MODEL // 3 + 255) // 256) * 256
EPS = 1e-6

LANES = 128
HEAD_BLOCK = LANES
ROPE_LANE0 = NOPE_DIM
N_DIRHEAD = 2 * MLSTM_HEADS
MLSTM_CHUNK = 256
LOG2E = math.log2(math.e)
Q_PRESCALE = MLA_SCALE * LOG2E
V_SLAB = 2 * LANES
V_WIDTH = (MLA_HEADS // 2) * V_SLAB
TOKEN_TILE = 512
TAIL_ROWS = 256
INPROJ_ROWS = TOKEN_TILE
Q_TILE = 256
CTX_SEQS_PER_STEP = 4
VMEM_LIMIT = 56 * 1024 * 1024

_SEG_WIDTHS = (Q_RANK, KV_RANK, LANES, MLSTM_HEADS * MLSTM_DK, MLSTM_HEADS * MLSTM_DK,
               MLSTM_HEADS * MLSTM_DV, LANES, MLSTM_HEADS * MLSTM_DV, 2 * D_MODEL)
_SEG_OFF = tuple(sum(_SEG_WIDTHS[:i]) for i in range(len(_SEG_WIDTHS) + 1))
SEG_Q, SEG_KV, SEG_KPE, SEG_MQ, SEG_MK, SEG_MV, SEG_GATE, SEG_MO, SEG_BR = (
    (_SEG_OFF[i], _SEG_OFF[i + 1]) for i in range(9))
N_IN_PACKED = _SEG_OFF[-1]


def _dot(a, b):
    return jnp.dot(a, b, preferred_element_type=F32)


def _dot_nt(a, b):
    return lax.dot_general(a, b, (((1,), (1,)), ((), ())), preferred_element_type=F32)


def _dot_tn(a, b):
    return lax.dot_general(a, b, (((0,), (0,)), ((), ())), preferred_element_type=F32)


def _rms(x, g):
    ms = jnp.mean(x * x, axis=-1, keepdims=True)
    return x * lax.rsqrt(ms + EPS) * g


def _sigmoid(x):
    return 1.0 / (1.0 + jnp.exp(-x))


def _const_spec(shape):
    nd = len(shape)
    return pl.BlockSpec(shape, lambda *_: (0,) * nd, pipeline_mode=pl.Buffered(1))


def _mod_chunks(mod_ref, row0, tiles_per_seq, step0=0):
    row = row0 if tiles_per_seq is None else row0 + (pl.program_id(0) - step0) // tiles_per_seq
    m = mod_ref[pl.ds(row, 1), :]
    return [m[:, k * D_MODEL:(k + 1) * D_MODEL] for k in range(m.shape[1] // D_MODEL)]


def _params(sem, flags=None, vmem_mib=None):
    limit = VMEM_LIMIT if vmem_mib is None else vmem_mib * 1024 * 1024
    return pltpu.CompilerParams(dimension_semantics=sem, vmem_limit_bytes=limit, flags=flags)


N_MOD_MIXER = 2
N_MOD_TAIL = 4


COND_ROWS = 8


def _adaln_block(cctx_ref, c_ref, w_ref, b_ref, o_ref):
    pad = jnp.zeros((COND_ROWS - 1 - c_ref.shape[0], D_MODEL), F32)
    c = jnp.concatenate([cctx_ref[...], c_ref[...], pad], axis=0)
    s = c * _sigmoid(c)
    o_ref[...] = _dot(s.astype(BF16), w_ref[...].astype(BF16)) + b_ref[...]


def _adaln(c_ctx, c, w_mod, b_mod):
    tn = D_MODEL
    return pl.pallas_call(
        _adaln_block,
        grid=(N_MOD_MIXER,),
        in_specs=[pl.BlockSpec(c_ctx.shape, lambda j: (0, 0)),
                  pl.BlockSpec(c.shape, lambda j: (0, 0)),
                  pl.BlockSpec((D_MODEL, tn), lambda j: (0, j)),
                  pl.BlockSpec((1, tn), lambda j: (0, j))],
        out_specs=pl.BlockSpec((COND_ROWS, tn), lambda j: (0, j)),
        out_shape=jax.ShapeDtypeStruct((COND_ROWS, N_MOD_MIXER * D_MODEL), F32),
        compiler_params=_params(("arbitrary",)),
        name="adaln",
    )(c_ctx, c, w_mod, b_mod)


def _rope_block(x, cos, sin_lo, sin_hi):
    return x * cos + pltpu.roll(x, LANES - 8, 1) * sin_lo + pltpu.roll(x, 8, 1) * sin_hi


def _store_kv(kvn, kp, kc_ref, v_ref, g=slice(None)):
    for hd in range(MLA_HEADS):
        sl = slice(hd * HEAD_BLOCK, (hd + 1) * HEAD_BLOCK)
        kc_ref[g, sl] = (kvn[:, sl] + kp).astype(BF16)
    v0 = MLA_HEADS * HEAD_BLOCK
    ones = jnp.ones((kvn.shape[0], LANES), BF16)
    for pair in range(MLA_HEADS // 2):
        v_ref[g, pair * V_SLAB:pair * V_SLAB + LANES] = kvn[:, v0 + pair * LANES:v0 + (pair + 1) * LANES].astype(BF16)
        v_ref[g, pair * V_SLAB + LANES:(pair + 1) * V_SLAB] = ones


def _inproj_kernel(*refs, rope, mod_row0, tiles_per_seq, step0=0):
    if rope:
        (x_ref, mod_ref, cos_ref, slo_ref, shi_ref, gmix_ref, win_ref, gq_ref, wuq_ref, gkv_ref, wukv_ref,
         bg_ref, q_ref, kc_ref, v_ref, mq_ref, mk_ref, mv_ref, gate_ref, smo_ref, sbr_ref) = refs
    else:
        (x_ref, mod_ref, gmix_ref, win_ref, gq_ref, wuq_ref, gkv_ref, wukv_ref,
         bg_ref, q_ref, kc_ref, v_ref, mq_ref, mk_ref, mv_ref, gate_ref, smo_ref, sbr_ref,
         ckv_ref, kpet_ref) = refs
    shift, scale = _mod_chunks(mod_ref, mod_row0, tiles_per_seq, step0)
    for r0 in range(0, x_ref.shape[0], INPROJ_ROWS):
        g = slice(r0, r0 + INPROJ_ROWS)
        if rope:
            cos, slo, shi = cos_ref[g, :], slo_ref[g, :], shi_ref[g, :]
        h = _rms(x_ref[g, :], gmix_ref[...]) * (1.0 + scale) + shift
        hb = h.astype(BF16)

        def proj(seg):
            return _dot_nt(hb, win_ref[seg[0]:seg[1], :])

        qn = _rms(proj(SEG_Q), gq_ref[...]).astype(BF16)
        ckv = _rms(proj(SEG_KV), gkv_ref[...])
        zkpe = proj(SEG_KPE)
        sbr_ref[g, :] = _sigmoid(proj(SEG_BR)).astype(BF16)
        smo_ref[g, :] = _sigmoid(proj(SEG_MO)).astype(BF16)
        gate_ref[g, :] = proj(SEG_GATE) + bg_ref[...]

        q = _dot(qn, wuq_ref[...])
        for hd in range(MLA_HEADS):
            sl = slice(hd * HEAD_BLOCK, (hd + 1) * HEAD_BLOCK)
            qh = q[:, sl]
            if rope:
                qh = _rope_block(qh, cos, slo, shi)
            q_ref[g, sl] = (qh * Q_PRESCALE).astype(BF16)

        if not rope:
            ckv_ref[g, :] = ckv
            kpe_t = zkpe.T
            seq = kpet_ref.shape[2]
            for sq in range(INPROJ_ROWS // seq):
                kpet_ref[r0 // seq + sq] = kpe_t[:ROPE_DIM, sq * seq:(sq + 1) * seq]
        kp = pltpu.roll(zkpe, ROPE_LANE0, 1)
        if rope:
            kp = _rope_block(kp, cos, slo, shi)
        kvn = _dot(ckv.astype(BF16), wukv_ref[...])
        _store_kv(kvn, kp, kc_ref, v_ref, g)

        mv_ref[g, :] = proj(SEG_MV).astype(BF16)
        mk_ref[g, :] = (proj(SEG_MK) * (MLSTM_DK ** -0.5)).astype(BF16)
        mq_ref[g, :] = proj(SEG_MQ).astype(BF16)


_INPROJ_WIDTHS = ((MLA_HEADS * HEAD_BLOCK, BF16), (MLA_HEADS * HEAD_BLOCK, BF16), (V_WIDTH, BF16),
                  (MLSTM_HEADS * MLSTM_DK, BF16), (MLSTM_HEADS * MLSTM_DK, BF16), (MLSTM_HEADS * MLSTM_DV, BF16),
                  (LANES, F32), (MLSTM_HEADS * MLSTM_DV, BF16), (2 * D_MODEL, BF16))


def _inproj_ctx(x, mod, gmix, w_in_p, gq, w_uq_p, gkv, w_ukv_p, bg, *, seq_len):
    n_tok = x.shape[0]
    tm = TOKEN_TILE
    tok = lambda c: pl.BlockSpec((tm, c), lambda i: (i, 0))
    consts = (gmix, w_in_p, gq, w_uq_p, gkv, w_ukv_p, bg)
    return pl.pallas_call(
        functools.partial(_inproj_kernel, rope=False, mod_row0=0, tiles_per_seq=None),
        grid=(n_tok // tm,),
        in_specs=[tok(D_MODEL), pl.BlockSpec(mod.shape, lambda i: (0, 0))] + [_const_spec(a.shape) for a in consts],
        out_specs=[tok(c) for c, _ in _INPROJ_WIDTHS]
                  + [tok(KV_RANK), pl.BlockSpec((tm // seq_len, ROPE_DIM, seq_len), lambda i: (i, 0, 0))],
        out_shape=[jax.ShapeDtypeStruct((n_tok, c), dt) for c, dt in _INPROJ_WIDTHS]
                  + [jax.ShapeDtypeStruct((n_tok, KV_RANK), F32),
                     jax.ShapeDtypeStruct((n_tok // seq_len, ROPE_DIM, seq_len), F32)],
        compiler_params=_params(("parallel",)),
        name="inproj_ctx",
    )(x, mod, *consts)


def _cachekv_kernel(ckv_ref, kp_ref, wukv_ref, kc_ref, v_ref):
    kvn = _dot(ckv_ref[...].astype(BF16), wukv_ref[...])
    _store_kv(kvn, kp_ref[...], kc_ref, v_ref)


def _cachekv(cache_ckv, krope_blk, w_ukv_p):
    b, past, _ = cache_ckv.shape
    return pl.pallas_call(
        _cachekv_kernel,
        grid=(b,),
        in_specs=[pl.BlockSpec((None, past, KV_RANK), lambda i: (i, 0, 0)),
                  pl.BlockSpec((None, past, LANES), lambda i: (i, 0, 0)),
                  _const_spec(w_ukv_p.shape)],
        out_specs=[pl.BlockSpec((None, past, MLA_HEADS * HEAD_BLOCK), lambda i: (i, 0, 0)),
                   pl.BlockSpec((None, past, V_WIDTH), lambda i: (i, 0, 0))],
        out_shape=[jax.ShapeDtypeStruct((b, past, MLA_HEADS * HEAD_BLOCK), BF16),
                   jax.ShapeDtypeStruct((b, past, V_WIDTH), BF16)],
        compiler_params=_params(("parallel",)),
        name="cachekv",
    )(cache_ckv, krope_blk, w_ukv_p)


def _attn_kernel(*refs, has_cache, n_cast, has_adaln):
    n_in = 5 if has_cache else 3
    if has_cache:
        q_ref, k_ref, v_ref, kc_ref, vc_ref = refs[:n_in]
    else:
        q_ref, k_ref, v_ref = refs[:n_in]
    n_all_in = n_in + n_cast + (4 if has_adaln else 0)
    o_ref = refs[n_all_in]
    for w_ref, wb_ref in zip(refs[n_in:n_in + n_cast], refs[n_all_in + 1:n_all_in + 1 + n_cast]):
        wb_ref[...] = w_ref[...].astype(BF16)
    if has_adaln:
        _adaln_block(*refs[n_in + n_cast:n_all_in], refs[n_all_in + 1 + n_cast])
    tq = q_ref.shape[1]
    lane = lax.broadcasted_iota(jnp.int32, (tq, LANES), 1)
    for sq in range(q_ref.shape[0]):
        hsl = [slice(hd * HEAD_BLOCK, (hd + 1) * HEAD_BLOCK) for hd in range(MLA_HEADS)]
        vsl = [slice(hd // 2 * V_SLAB, (hd // 2 + 1) * V_SLAB) for hd in range(MLA_HEADS)]
        s = [_dot_nt(q_ref[sq, :, sl], k_ref[sq, :, sl]) for sl in hsl]
        m = [jnp.max(si, axis=-1, keepdims=True) for si in s]
        if has_cache:
            sc = [_dot_nt(q_ref[sq, :, sl], kc_ref[sq, :, sl]) for sl in hsl]
            m = [jnp.maximum(mi, jnp.max(ci, axis=-1, keepdims=True)) for mi, ci in zip(m, sc)]
        p = [jnp.exp2(si - mi).astype(BF16) for si, mi in zip(s, m)]
        o = [_dot(pi, v_ref[sq, :, sl]) for pi, sl in zip(p, vsl)]
        if has_cache:
            pc = [jnp.exp2(ci - mi).astype(BF16) for ci, mi in zip(sc, m)]
            o = [oi + _dot(pi, vc_ref[sq, :, sl]) for oi, pi, sl in zip(o, pc, vsl)]
        outs = [oi[:, :LANES] * (1.0 / oi[:, LANES:]) for oi in o]
        for pair in range(MLA_HEADS // 2):
            o_ref[sq, :, pair * LANES:(pair + 1) * LANES] = jnp.where(
                lane < MLA_V_DIM, outs[2 * pair], outs[2 * pair + 1]).astype(BF16)


def _attention(q, k, v, cache=None, cast_weights=(), adaln_tail=None, seqs_per_step=1):
    nb, s, _ = q.shape
    tq = Q_TILE
    ns = seqs_per_step
    b = nb // ns
    n_steps = b * (s // tq)
    kw, vw, ow = MLA_HEADS * HEAD_BLOCK, V_WIDTH, MLA_HEADS * MLA_V_DIM
    in_specs = [pl.BlockSpec((ns, tq, kw), lambda i, j: (i, j, 0)),
                pl.BlockSpec((ns, s, kw), lambda i, j: (i, 0, 0)),
                pl.BlockSpec((ns, s, vw), lambda i, j: (i, 0, 0))]
    args = [q, k, v]
    if cache is not None:
        past = cache[0].shape[1]
        in_specs += [pl.BlockSpec((ns, past, kw), lambda i, j: (i, 0, 0)),
                     pl.BlockSpec((ns, past, vw), lambda i, j: (i, 0, 0))]
        args += list(cache)
    step = lambda i, j: i * (s // tq) + j
    w_specs = [pl.BlockSpec((w.shape[0] // n_steps, w.shape[1]), lambda i, j: (step(i, j), 0)) for w in cast_weights]
    extra_in, extra_out, extra_shape = [], [], []
    if adaln_tail is not None:
        tn = N_MOD_TAIL * D_MODEL // n_steps
        col0 = N_MOD_MIXER * D_MODEL // tn
        extra_in = [pl.BlockSpec(adaln_tail[0].shape, lambda i, j: (0, 0)),
                    pl.BlockSpec(adaln_tail[1].shape, lambda i, j: (0, 0)),
                    pl.BlockSpec((D_MODEL, tn), lambda i, j: (0, col0 + step(i, j))),
                    pl.BlockSpec((1, tn), lambda i, j: (0, col0 + step(i, j)))]
        extra_out = [pl.BlockSpec((COND_ROWS, tn), lambda i, j: (0, step(i, j)))]
        extra_shape = [jax.ShapeDtypeStruct((COND_ROWS, N_MOD_TAIL * D_MODEL), F32)]
    outs = pl.pallas_call(
        functools.partial(_attn_kernel, has_cache=cache is not None, n_cast=len(cast_weights),
                          has_adaln=adaln_tail is not None),
        grid=(b, s // tq),
        in_specs=in_specs + w_specs + extra_in,
        out_specs=[pl.BlockSpec((ns, tq, ow), lambda i, j: (i, j, 0))] + w_specs + extra_out,
        out_shape=[jax.ShapeDtypeStruct((nb, s, ow), BF16)]
                  + [jax.ShapeDtypeStruct(w.shape, BF16) for w in cast_weights] + extra_shape,
        compiler_params=_params(("parallel", "arbitrary")),
        name="attn_lat" if cache is not None else "attn_ctx",
    )(*args, *cast_weights, *(adaln_tail or ()))
    return outs[0] if len(outs) == 1 else outs


def _lane_bcast(x, k, width=None):
    y = jnp.broadcast_to(x[:, k:k + 1], x.shape)
    reps = (width or LANES) // LANES
    return y if reps == 1 else jnp.concatenate([y] * reps, axis=1)


def _prefix_sum_rows(x):
    n = x.shape[0]
    r = lax.broadcasted_iota(jnp.int32, (n, n), 0)
    c = lax.broadcasted_iota(jnp.int32, (n, n), 1)
    tri = jnp.where(c <= r, 1.0, 0.0).astype(BF16)
    hi = x.astype(BF16)
    r1 = x - hi.astype(F32)
    mid = r1.astype(BF16)
    lo = (r1 - mid.astype(F32)).astype(BF16)
    return _dot(tri, hi) + _dot(tri, mid) + _dot(tri, lo)


def _gate_prep(g):
    n = g.shape[0]
    lane = lax.broadcasted_iota(jnp.int32, (n, LANES), 1)
    fpre = pltpu.roll(g, LANES - N_DIRHEAD, 1)
    lf = jnp.minimum(fpre, 0.0) - jnp.log(1.0 + jnp.exp(-jnp.abs(fpre)))
    binc = _prefix_sum_rows(lf)
    tot = binc[n - 1:n, :]
    b = jnp.where(lane < MLSTM_HEADS, binc, tot - binc + lf)
    return b, g - b, tot


def _chunk_setup(b, u, tot, m_prev):
    g_last = jnp.maximum(jnp.max(u, axis=0, keepdims=True), m_prev)
    u2 = u * LOG2E
    w = jnp.exp2(u2 - g_last * LOG2E)
    return u2, (b * LOG2E).T, w.T, tot + g_last, jnp.exp(m_prev - g_last)


def _state_update(vt, k, wt, kk):
    c_new = _dot((vt * wt[kk:kk + 1, :]).astype(BF16), k)
    n_new = _dot(wt[:N_DIRHEAD, :].astype(BF16), k)[kk:kk + 1, :]
    return c_new, n_new


def _mlstm_dir_t(s0t, vt, u_b, bt_row, m_row, *, fwd, inter=None):
    sdt, den, g = _dir_weights(s0t, u_b, m_row, fwd=fwd)
    return _dir_finish(_dot(vt, sdt), den, g, bt_row, m_row, inter)


def _dir_weights(s0t, u_b, m_row, *, fwd):
    n = s0t.shape[0]
    r = lax.broadcasted_iota(jnp.int32, (n, n), 0)
    c = lax.broadcasted_iota(jnp.int32, (n, n), 1)
    a = jnp.where((r <= c) if fwd else (r >= c), u_b, -jnp.inf)
    g = jnp.maximum(jnp.max(a, axis=0, keepdims=True), m_row)
    sdt = s0t * jnp.exp2(a - g)
    return sdt.astype(BF16), jnp.sum(sdt, axis=0, keepdims=True), g


def _dir_finish(ht, den, g, bt_row, m_row, inter=None):
    if inter is not None:
        w_inter = jnp.exp2(m_row - g)
        ht = ht + w_inter * inter[0]
        den = den + w_inter * inter[1]
    floor = jnp.exp2(-(bt_row + g))
    return ht * (1.0 / jnp.maximum(jnp.abs(den), floor))


def _mlstm_ctx_kernel(q_ref, k_ref, v_ref, g_ref, ht_ref, c_ref, n_ref, m_ref):
    n = q_ref.shape[1]
    zero_lanes = jnp.zeros((1, LANES), F32)
    zero_row = jnp.zeros((1, n), F32)
    dirs = ((True, 0), (False, MLSTM_HEADS))
    setup = [_chunk_setup(*_gate_prep(g_ref[sq]), zero_lanes) for sq in range(q_ref.shape[0])]
    units = [(sq, hd) for sq in range(q_ref.shape[0]) for hd in range(MLSTM_HEADS)]
    ks = [k_ref[sq, :, hd * MLSTM_DK:(hd + 1) * MLSTM_DK] for sq, hd in units]
    s0ts = [_dot_nt(k, q_ref[sq, :, hd * MLSTM_DK:(hd + 1) * MLSTM_DK]) for k, (sq, hd) in zip(ks, units)]
    vts = [v_ref[sq, :, hd * MLSTM_DV:(hd + 1) * MLSTM_DV].T for sq, hd in units]
    wts = [[_dir_weights(s0t, _lane_bcast(setup[sq][0], off + hd, n), zero_row, fwd=fwd) for fwd, off in dirs]
           for s0t, (sq, hd) in zip(s0ts, units)]
    hts = [[_dot(vt, sdt) for sdt, _, _ in wt] for vt, wt in zip(vts, wts)]
    for (sq, hd), k, vt, wt, ht2 in zip(units, ks, vts, wts, hts):
        _, bt2, w_t, _, _ = setup[sq]
        ht_ref[sq, hd * MLSTM_DV:(hd + 1) * MLSTM_DV, :] = sum(
            _dir_finish(ht_d, den, g, bt2[off + hd:off + hd + 1, :], zero_row)
            for ht_d, (_, den, g), (_, off) in zip(ht2, wt, dirs))
        for _, off in dirs:
            c_ref[sq, off + hd], n_ref[sq, off + hd:off + hd + 1, :] = _state_update(vt, k, w_t, off + hd)
    for sq in range(q_ref.shape[0]):
        m_ref[sq] = setup[sq][3]


def _mlstm_ctx(mq, mk, mv, gates):
    b, s, _ = mq.shape
    ns = CTX_SEQS_PER_STEP
    seq = lambda c: pl.BlockSpec((ns, s, c), lambda i: (i, 0, 0))
    return pl.pallas_call(
        _mlstm_ctx_kernel,
        grid=(b // ns,),
        in_specs=[seq(MLSTM_HEADS * MLSTM_DK), seq(MLSTM_HEADS * MLSTM_DK), seq(MLSTM_HEADS * MLSTM_DV), seq(LANES)],
        out_specs=[pl.BlockSpec((ns, MLSTM_HEADS * MLSTM_DV, s), lambda i: (i, 0, 0)),
                   pl.BlockSpec((ns, N_DIRHEAD, MLSTM_DV, MLSTM_DK), lambda i: (i, 0, 0, 0)),
                   pl.BlockSpec((ns, N_DIRHEAD, MLSTM_DK), lambda i: (i, 0, 0)),
                   pl.BlockSpec((ns, 1, LANES), lambda i: (i, 0, 0))],
        out_shape=[jax.ShapeDtypeStruct((b, MLSTM_HEADS * MLSTM_DV, s), F32),
                   jax.ShapeDtypeStruct((b, N_DIRHEAD, MLSTM_DV, MLSTM_DK), F32),
                   jax.ShapeDtypeStruct((b, N_DIRHEAD, MLSTM_DK), F32),
                   jax.ShapeDtypeStruct((b, 1, LANES), F32)],
        compiler_params=_params(("parallel",)),
        name="mlstm_ctx",
    )(mq, mk, mv, gates)


def _mlstm_lat_kernel(qf_ref, kf_ref, vf_ref, gf_ref, qb_ref, kb_ref, vb_ref, gb_ref, c0_ref, n0_ref, m0_ref,
                      ht_ref, c_ref, n_ref, m_ref, *, n_chunks):
    step = pl.program_id(1)

    @pl.when(step == 0)
    def _():
        c_ref[...] = c0_ref[...]
        n_ref[...] = n0_ref[...]
        m_ref[...] = m0_ref[...]
        ht_ref[...] = jnp.zeros_like(ht_ref)

    n = qf_ref.shape[0]
    gate_f, gate_b = _gate_prep(gf_ref[...]), _gate_prep(gb_ref[...])
    is_f = lax.broadcasted_iota(jnp.int32, (n, LANES), 1) < MLSTM_HEADS
    b, u, tot = (jnp.where(is_f[:x.shape[0]], x, y) for x, y in zip(gate_f, gate_b))
    m_prev = m_ref[...]
    u2, bt2, w_t, m_new, decay = _chunk_setup(b, u, tot, m_prev)
    m2_prev = m_prev * LOG2E
    units = [(fwd, hd, hd if fwd else MLSTM_HEADS + hd) for fwd in (True, False) for hd in range(MLSTM_HEADS)]
    refs = {True: (qf_ref, kf_ref, vf_ref), False: (qb_ref, kb_ref, vb_ref)}
    qs = [refs[fwd][0][:, hd * MLSTM_DK:(hd + 1) * MLSTM_DK] for fwd, hd, _ in units]
    ks = [refs[fwd][1][:, hd * MLSTM_DK:(hd + 1) * MLSTM_DK] for fwd, hd, _ in units]
    vts = [refs[fwd][2][:, hd * MLSTM_DV:(hd + 1) * MLSTM_DV].T for fwd, hd, _ in units]
    s0ts = [_dot_nt(k, q) for k, q in zip(ks, qs)]
    c_prevs = [c_ref[kk] for _, _, kk in units]
    n_prevs = [n_ref[kk:kk + 1, :] for _, _, kk in units]
    inters = [(_dot_nt(c.astype(BF16), q), _dot_nt(jnp.broadcast_to(nv, (8, MLSTM_DK)).astype(BF16), q)[0:1, :])
              for c, nv, q in zip(c_prevs, n_prevs, qs)]
    m_rows = [_lane_bcast(m2_prev, kk, n) for _, _, kk in units]
    wts = [_dir_weights(s0t, _lane_bcast(u2, kk, n), m_row, fwd=fwd)
           for s0t, m_row, (fwd, _, kk) in zip(s0ts, m_rows, units)]
    hts = [_dot(vt, sdt) for vt, (sdt, _, _) in zip(vts, wts)]
    for (fwd, hd, kk), k, vt, (_, den, g), ht, inter, m_row, c_prev, n_prev in zip(
            units, ks, vts, wts, hts, inters, m_rows, c_prevs, n_prevs):
        chunk = step if fwd else n_chunks - 1 - step
        ht_ref[chunk, hd * MLSTM_DV:(hd + 1) * MLSTM_DV, :] += _dir_finish(ht, den, g, bt2[kk:kk + 1, :], m_row, inter)
        c_new, n_new = _state_update(vt, k, w_t, kk)
        dk = _lane_bcast(decay, kk)
        c_ref[kk] = dk * c_prev + c_new
        n_ref[kk:kk + 1, :] = dk * n_prev + n_new
    m_ref[...] = m_new


def _mlstm_lat(mq, mk, mv, gates, c0, n0, m0):
    b, s, _ = mq.shape
    lc = MLSTM_CHUNK
    nc = s // lc
    fw = lambda c: pl.BlockSpec((None, lc, c), lambda i, j: (i, j, 0))
    bw = lambda c: pl.BlockSpec((None, lc, c), lambda i, j: (i, nc - 1 - j, 0))
    dk, dv = MLSTM_HEADS * MLSTM_DK, MLSTM_HEADS * MLSTM_DV
    return pl.pallas_call(
        functools.partial(_mlstm_lat_kernel, n_chunks=nc),
        grid=(b, nc),
        in_specs=[fw(dk), fw(dk), fw(dv), fw(LANES), bw(dk), bw(dk), bw(dv), bw(LANES),
                  pl.BlockSpec((None, N_DIRHEAD, MLSTM_DV, MLSTM_DK), lambda i, j: (i, 0, 0, 0)),
                  pl.BlockSpec((None, N_DIRHEAD, MLSTM_DK), lambda i, j: (i, 0, 0)),
                  pl.BlockSpec((None, 1, LANES), lambda i, j: (i, 0, 0))],
        out_specs=pl.BlockSpec((nc, dv, lc), lambda i, j: (i, 0, 0)),
        out_shape=jax.ShapeDtypeStruct((b * nc, dv, lc), F32),
        scratch_shapes=[pltpu.VMEM((N_DIRHEAD, MLSTM_DV, MLSTM_DK), F32),
                        pltpu.VMEM((N_DIRHEAD, MLSTM_DK), F32),
                        pltpu.VMEM((1, LANES), F32)],
        compiler_params=_params(("parallel", "arbitrary")),
        name="mlstm_lat",
    )(mq, mk, mv, gates, mq, mk, mv, gates, c0, n0, m0)


def _tail_kernel(x_ref, mod_ref, att_ref, hst_ref, smo_ref, sbr_ref, gml_ref, womla_ref, womlstm_ref, wout_ref,
                 gffn_ref, wfin_ref, wfout_ref, gfin_ref, y_ref, *, mod_row0, tiles_per_seq):
    gate_mix, shift_ffn, scale_ffn, gate_ffn = _mod_chunks(mod_ref, mod_row0, tiles_per_seq)
    gml = gml_ref[...]
    tm = x_ref.shape[0]
    groups = [slice(r, r + TAIL_ROWS) for r in range(0, tm, TAIL_ROWS)]
    a = [_dot(att_ref[g, :], womla_ref[...]) for g in groups]
    hm = []
    for gi, g in enumerate(groups):
        hs = hst_ref[gi].T
        parts = []
        for hd in range(MLSTM_HEADS):
            sl = slice(hd * MLSTM_DV, (hd + 1) * MLSTM_DV)
            parts.append((_rms(hs[:, sl], gml[:, sl]) * smo_ref[g, sl].astype(F32)).astype(BF16))
        hm.append(jnp.concatenate(parts, axis=1))
    bm = [_dot(h, womlstm_ref[...]) for h in hm]
    merged = [(sbr_ref[g, :D_MODEL].astype(F32) * ai + sbr_ref[g, D_MODEL:].astype(F32) * bi).astype(BF16)
              for g, ai, bi in zip(groups, a, bm)]
    x1 = [x_ref[g, :] + gate_mix * _dot(mi, wout_ref[...]) for g, mi in zip(groups, merged)]
    h2 = [(_rms(xi, gffn_ref[...]) * (1.0 + scale_ffn) + shift_ffn).astype(BF16) for xi in x1]
    ga = [_dot(hi, wfin_ref[:, :FFN_HIDDEN]) for hi in h2]
    gu = [_dot(hi, wfin_ref[:, FFN_HIDDEN:]) for hi in h2]
    act = [(gi * _sigmoid(gi) * ui).astype(BF16) for gi, ui in zip(ga, gu)]
    for g, xi, ci in zip(groups, x1, act):
        y_ref[g, :] = _rms(xi + gate_ffn * _dot(ci, wfout_ref[...]), gfin_ref[...])


def _tail(x, mod, att, hst, smo, sbr, gml, w_o_mla, w_o_mlstm, w_out, gffn, w_ffn_in, w_ffn_out, gfin, *,
          seq_len, latent):
    n_tok = x.shape[0]
    tm = TOKEN_TILE
    assert hst.shape[0] * TAIL_ROWS == n_tok and hst.shape[2] == TAIL_ROWS
    tok = lambda c: pl.BlockSpec((tm, c), lambda i: (i, 0))
    consts = (gml, w_o_mla, w_o_mlstm, w_out, gffn, w_ffn_in, w_ffn_out, gfin)
    return pl.pallas_call(
        functools.partial(_tail_kernel, mod_row0=1 if latent else 0, tiles_per_seq=seq_len // tm if latent else None),
        grid=(n_tok // tm,),
        in_specs=[tok(D_MODEL), pl.BlockSpec(mod.shape, lambda i: (0, 0)), tok(att.shape[1]),
                  pl.BlockSpec((tm // TAIL_ROWS,) + hst.shape[1:], lambda i: (i, 0, 0)),
                  tok(smo.shape[1]), tok(sbr.shape[1])]
                 + [_const_spec(a.shape) for a in consts],
        out_specs=tok(D_MODEL),
        out_shape=jax.ShapeDtypeStruct((n_tok, D_MODEL), F32),
        compiler_params=_params(("parallel",)),
        name="tail_lat" if latent else "tail_ctx",
    )(x, mod, att, hst, smo, sbr, *consts)


_IN_SIZES = (Q_RANK, KV_RANK, ROPE_DIM, MLSTM_HEADS * MLSTM_DK, MLSTM_HEADS * MLSTM_DK, MLSTM_HEADS * MLSTM_DV,
             4 * MLSTM_HEADS, MLSTM_HEADS * MLSTM_DV, 2 * D_MODEL)
_IN_OFF = tuple(sum(_IN_SIZES[:i]) for i in range(len(_IN_SIZES) + 1))
PACK_ROWS = LANES
_PAD_AFTER = tuple((_SEG_OFF[i + 1] // PACK_ROWS, _SEG_WIDTHS[i] - _IN_SIZES[i], _IN_SIZES[i] % PACK_ROWS)
                   for i in range(len(_IN_SIZES)) if _SEG_WIDTHS[i] != _IN_SIZES[i])


PACK_PIECES = 8
N_PACK_STEPS = pl.cdiv(N_IN_PACKED // PACK_ROWS, PACK_PIECES)
_N_IN = _IN_OFF[-1]


def _pack_src_row(p):
    src = p * PACK_ROWS
    for first_piece_after, pad, _ in _PAD_AFTER:
        src = src - jnp.where(p >= first_piece_after, pad, 0)
    src = jnp.minimum(src, _N_IN - PACK_ROWS)
    return pl.multiple_of(src, math.gcd(PACK_ROWS, _N_IN, *(pad for _, pad, _ in _PAD_AFTER)))


PACK_STEP_ROWS = PACK_PIECES * PACK_ROWS
N_PACKED_ROWS = N_PACK_STEPS * PACK_STEP_ROWS


def _pack_pieces(w_refs, step, dst_refs):
    row = lax.broadcasted_iota(jnp.int32, w_refs[0].shape, 0)
    for j, w_ref in enumerate(w_refs):
        p = step * PACK_PIECES + j
        valid = jnp.where(p < N_IN_PACKED // PACK_ROWS, PACK_ROWS, 0)
        for first_piece_after, _, width in _PAD_AFTER:
            valid = jnp.where(p == first_piece_after - 1, width, valid)
        piece = jnp.where(row < valid, w_ref[...], 0.0).astype(BF16)
        for dst in dst_refs:
            dst[j * PACK_ROWS:(j + 1) * PACK_ROWS, :] = piece


N_INPROJ_LAT_IN = 11


def _inproj_lat_kernel(*refs, tiles_per_seq):
    pieces = refs[:PACK_PIECES]
    x_ref, mod_ref, cos_ref, slo_ref, shi_ref, gmix_ref, gq_ref, wuq_ref, gkv_ref, wukv_ref, bg_ref = (
        refs[PACK_PIECES:PACK_PIECES + N_INPROJ_LAT_IN])
    *tok_outs, wp_ref, w_scr = refs[PACK_PIECES + N_INPROJ_LAT_IN:]
    step = pl.program_id(0)

    @pl.when(step < N_PACK_STEPS)
    def _():
        rows = pl.ds(pl.multiple_of(step * PACK_STEP_ROWS, PACK_STEP_ROWS), PACK_STEP_ROWS)
        _pack_pieces(pieces, step, (wp_ref, w_scr.at[rows]))

    @pl.when(step >= N_PACK_STEPS)
    def _():
        _inproj_kernel(x_ref, mod_ref, cos_ref, slo_ref, shi_ref, gmix_ref, w_scr, gq_ref, wuq_ref, gkv_ref, wukv_ref,
                       bg_ref, *tok_outs, rope=True, mod_row0=1, tiles_per_seq=tiles_per_seq, step0=N_PACK_STEPS)


def _inproj_lat(x, mod, rope_tabs, w_in_t, gmix, gq, w_uq_p, gkv, w_ukv_p, bg, *, seq_len):
    n_tok = x.shape[0]
    tm = TOKEN_TILE
    tiles_per_seq = seq_len // tm
    k = w_in_t.shape[1]
    tile = lambda i: jnp.maximum(i - N_PACK_STEPS, 0)
    pack = lambda i: jnp.minimum(i, N_PACK_STEPS - 1)
    piece = lambda j: pl.BlockSpec((pl.Element(PACK_ROWS), pl.Element(k)),
                                   lambda i: (_pack_src_row(pack(i) * PACK_PIECES + j), 0))
    tok = lambda c: pl.BlockSpec((tm, c), lambda i: (tile(i), 0))
    consts = (gmix, gq, w_uq_p, gkv, w_ukv_p, bg)
    in_specs = ([piece(j) for j in range(PACK_PIECES)]
                + [tok(D_MODEL), pl.BlockSpec(mod.shape, lambda i: (0, 0))]
                + [pl.BlockSpec((tm, LANES), lambda i: (tile(i) % tiles_per_seq, 0))] * 3
                + [_const_spec(a.shape) for a in consts])
    outs = pl.pallas_call(
        functools.partial(_inproj_lat_kernel, tiles_per_seq=tiles_per_seq),
        grid=(N_PACK_STEPS + n_tok // tm,),
        in_specs=in_specs,
        out_specs=[tok(c) for c, _ in _INPROJ_WIDTHS] + [pl.BlockSpec((PACK_STEP_ROWS, k), lambda i: (pack(i), 0))],
        out_shape=[jax.ShapeDtypeStruct((n_tok, c), dt) for c, dt in _INPROJ_WIDTHS]
                  + [jax.ShapeDtypeStruct((N_PACKED_ROWS, k), BF16)],
        scratch_shapes=[pltpu.VMEM((N_PACKED_ROWS, k), BF16)],
        compiler_params=_params(("arbitrary",)),
        name="inproj_lat",
    )(*([w_in_t] * PACK_PIECES), x, mod, *rope_tabs, *consts)
    return outs[:-1], outs[-1]


def _pack_small(w_uq, w_ukv, b_gates):
    qd = NOPE_DIM + ROPE_DIM
    w_uq_p = jnp.pad(w_uq.reshape(Q_RANK, MLA_HEADS, qd), ((0, 0), (0, 0), (0, HEAD_BLOCK - qd)))
    w_uq_p = w_uq_p.reshape(Q_RANK, MLA_HEADS * HEAD_BLOCK).astype(BF16)
    kv = w_ukv.reshape(KV_RANK, MLA_HEADS, NOPE_DIM + MLA_V_DIM)
    wk = jnp.pad(kv[:, :, :NOPE_DIM], ((0, 0), (0, 0), (0, HEAD_BLOCK - NOPE_DIM))).reshape(KV_RANK, -1)
    wv = kv[:, :, NOPE_DIM:].reshape(KV_RANK, -1)
    w_ukv_p = jnp.concatenate([wk, wv], axis=1).astype(BF16)
    bg = jnp.pad(b_gates, ((0, 0), (0, LANES - b_gates.shape[1])))
    return w_uq_p, w_ukv_p, bg


def _rope_tables(n_tokens):
    pos = np.arange(n_tokens)
    row = (pos // GRID_W).astype(np.float64)[:, None]
    col = (pos % GRID_W).astype(np.float64)[:, None]
    half = ROPE_DIM // 2
    inv = (np.float32(ROPE_BASE) ** (-np.arange(0, half, 2, dtype=np.float32) / np.float32(half))).astype(np.float64)
    r = np.arange(LANES) - ROPE_LANE0
    in_rope = (r >= 0) & (r < ROPE_DIM)
    rr = np.clip(r, 0, ROPE_DIM - 1)
    freq = inv[rr % (half // 2)][None, :]
    ang = np.where((rr // half == 0)[None, :], row * freq, col * freq).astype(np.float32).astype(np.float64)
    first = (rr % half) < (half // 2)
    cos = np.where(in_rope[None, :], np.cos(ang), 1.0)
    sin = np.sin(ang)
    sin_lo = np.where((in_rope & first)[None, :], -sin, 0.0)
    sin_hi = np.where((in_rope & ~first)[None, :], sin, 0.0)
    return tuple(jnp.asarray(t, dtype=F32) for t in (cos, sin_lo, sin_hi))


def kernel(x_prompt, x_sample, cache_ckv, cache_krope, state_C, state_n, state_m, c, c_ctx, w_mod, b_mod, g_norm_mix,
           w_in, b_gates, g_q_norm, w_uq, g_kv_norm, w_ukv, g_mlstm_norm, w_o_mla, w_o_mlstm, w_out, g_norm_ffn,
           w_ffn_in, w_ffn_out, g_final):
    bp, sp, d = x_prompt.shape
    bs, ss, _ = x_sample.shape
    layer = 0
    assert w_mod.shape[0] == 1 and sp == MLSTM_CHUNK and ss % MLSTM_CHUNK == 0

    adaln_args = (c_ctx[None, :], c, w_mod[layer], b_mod[layer][None, :])
    mod_in = _adaln(*adaln_args)

    w_uq_p, w_ukv_p, bg = _pack_small(w_uq[layer], w_ukv[layer], b_gates[layer][None, :])
    row = lambda g: g[layer][None, :]
    seq3 = lambda a, b_, s_: a.reshape(b_, s_, a.shape[-1])

    xl = x_sample.reshape(bs * ss, d)
    (q, kc, v, mq_l, mk_l, mv_l, gates_l, smo_l, sbr_l), w_in_p = _inproj_lat(
        xl, mod_in, _rope_tables(ss), w_in[layer].T, row(g_norm_mix), row(g_q_norm), w_uq_p, row(g_kv_norm), w_ukv_p,
        bg, seq_len=ss)
    krope_blk = jnp.pad(cache_krope[:, layer], ((0, 0), (0, 0), (ROPE_LANE0, LANES - ROPE_LANE0 - ROPE_DIM)))
    cache = _cachekv(cache_ckv[:, layer], krope_blk, w_ukv_p)
    att_l, womla_b, womlstm_b, wout_b, wfin_b, wfout_b, mod_tail = _attention(
        seq3(q, bs, ss), seq3(kc, bs, ss), seq3(v, bs, ss), cache,
        cast_weights=(w_o_mla[layer], w_o_mlstm[layer], w_out[layer], w_ffn_in[layer], w_ffn_out[layer]),
        adaln_tail=adaln_args)
    tail_w = (row(g_mlstm_norm), womla_b, womlstm_b, wout_b, row(g_norm_ffn), wfin_b, wfout_b, g_final[None, :])

    xc = x_prompt.reshape(bp * sp, d)
    q, kc, v, mq, mk, mv, gates, smo, sbr, ckv, kpe = _inproj_ctx(
        xc, mod_in, row(g_norm_mix), w_in_p, row(g_q_norm), w_uq_p, row(g_kv_norm), w_ukv_p, bg, seq_len=sp)
    att = _attention(seq3(q, bp, sp), seq3(kc, bp, sp), seq3(v, bp, sp), seqs_per_step=CTX_SEQS_PER_STEP)
    hs, new_c, new_n, new_m = _mlstm_ctx(seq3(mq, bp, sp), seq3(mk, bp, sp), seq3(mv, bp, sp), seq3(gates, bp, sp))
    y_prompt = _tail(xc, mod_tail, att.reshape(bp * sp, -1), hs, smo, sbr, *tail_w,
                     seq_len=sp, latent=False).reshape(bp, sp, d)
    new_ckv = ckv.reshape(bp, 1, sp, KV_RANK)
    new_krope = jnp.swapaxes(kpe, 1, 2).reshape(bp, 1, sp, ROPE_DIM)
    new_C = new_c.reshape(bp, 1, 2, MLSTM_HEADS, MLSTM_DV, MLSTM_DK)
    new_N = new_n.reshape(bp, 1, 2, MLSTM_HEADS, MLSTM_DK)
    new_M = new_m[:, 0, :N_DIRHEAD].reshape(bp, 1, 2, MLSTM_HEADS)

    c0 = state_C[:, layer].reshape(bs, N_DIRHEAD, MLSTM_DV, MLSTM_DK)
    n0 = state_n[:, layer].reshape(bs, N_DIRHEAD, MLSTM_DK)
    m0 = jnp.pad(state_m[:, layer].reshape(bs, 1, N_DIRHEAD), ((0, 0), (0, 0), (0, LANES - N_DIRHEAD)))
    hs = _mlstm_lat(seq3(mq_l, bs, ss), seq3(mk_l, bs, ss), seq3(mv_l, bs, ss), seq3(gates_l, bs, ss), c0, n0, m0)
    y_sample = _tail(xl, mod_tail, att_l.reshape(bs * ss, -1), hs, smo_l, sbr_l, *tail_w,
                     seq_len=ss, latent=True).reshape(bs, ss, d)
    return (y_prompt, y_sample, new_ckv, new_krope, new_C, new_N, new_M)
```

```python
import functools
import math

import jax
import jax.numpy as jnp
import numpy as np
from jax import lax
from jax.experimental import pallas as pl
from jax.experimental.pallas import tpu as pltpu

F32 = jnp.float32
BF16 = jnp.bfloat16

D_MODEL = 1024
GRID_W = 64
MLA_HEADS = 8
Q_RANK = 384
KV_RANK = 256
NOPE_DIM = 64
ROPE_DIM = 32
MLA_V_DIM = 64
ROPE_BASE = 10000.0
MLA_SCALE = (NOPE_DIM + ROPE_DIM) ** -0.5
MLSTM_HEADS = 4
MLSTM_DK = 128
MLSTM_DV = 256
FFN_HIDDEN = ((8 * D_MODEL // 3 + 255) // 256) * 256
EPS = 1e-6

LANES = 128
HEAD_BLOCK = LANES
ROPE_LANE0 = NOPE_DIM
N_DIRHEAD = 2 * MLSTM_HEADS
MLSTM_CHUNK = 256
LOG2E = math.log2(math.e)
Q_PRESCALE = MLA_SCALE * LOG2E
V_SLAB = 2 * LANES
V_WIDTH = (MLA_HEADS // 2) * V_SLAB
TOKEN_TILE = 512
TAIL_ROWS = 256
INPROJ_ROWS = TOKEN_TILE
Q_TILE = 256
CTX_SEQS_PER_STEP = 4
VMEM_LIMIT = 56 * 1024 * 1024

_SEG_WIDTHS = (Q_RANK, KV_RANK, LANES, MLSTM_HEADS * MLSTM_DK, MLSTM_HEADS * MLSTM_DK,
               MLSTM_HEADS * MLSTM_DV, LANES, MLSTM_HEADS * MLSTM_DV, 2 * D_MODEL)
_SEG_OFF = tuple(sum(_SEG_WIDTHS[:i]) for i in range(len(_SEG_WIDTHS) + 1))
SEG_Q, SEG_KV, SEG_KPE, SEG_MQ, SEG_MK, SEG_MV, SEG_GATE, SEG_MO, SEG_BR = (
    (_SEG_OFF[i], _SEG_OFF[i + 1]) for i in range(9))
N_IN_PACKED = _SEG_OFF[-1]


def _dot(a, b):
    return jnp.dot(a, b, preferred_element_type=F32)


def _dot_nt(a, b):
    return lax.dot_general(a, b, (((1,), (1,)), ((), ())), preferred_element_type=F32)


def _dot_tn(a, b):
    return lax.dot_general(a, b, (((0,), (0,)), ((), ())), preferred_element_type=F32)


def _rms(x, g):
    ms = jnp.mean(x * x, axis=-1, keepdims=True)
    return x * lax.rsqrt(ms + EPS) * g


def _sigmoid(x):
    return 1.0 / (1.0 + jnp.exp(-x))


def _const_spec(shape):
    nd = len(shape)
    return pl.BlockSpec(shape, lambda *_: (0,) * nd, pipeline_mode=pl.Buffered(1))


def _mod_chunks(mod_ref, row0, tiles_per_seq):
    row = row0 if tiles_per_seq is None else row0 + pl.program_id(0) // tiles_per_seq
    m = mod_ref[pl.ds(row, 1), :]
    return [m[:, k * D_MODEL:(k + 1) * D_MODEL] for k in range(m.shape[1] // D_MODEL)]


def _params(sem, flags=None, vmem_mib=None):
    limit = VMEM_LIMIT if vmem_mib is None else vmem_mib * 1024 * 1024
    return pltpu.CompilerParams(dimension_semantics=sem, vmem_limit_bytes=limit, flags=flags)


N_MOD_MIXER = 2
N_MOD_TAIL = 4


COND_ROWS = 8


def _adaln_block(cctx_ref, c_ref, w_ref, b_ref, o_ref):
    pad = jnp.zeros((COND_ROWS - 1 - c_ref.shape[0], D_MODEL), F32)
    c = jnp.concatenate([cctx_ref[...], c_ref[...], pad], axis=0)
    s = c * _sigmoid(c)
    o_ref[...] = _dot(s.astype(BF16), w_ref[...].astype(BF16)) + b_ref[...]


def _adaln(c_ctx, c, w_mod, b_mod):
    tn = D_MODEL
    return pl.pallas_call(
        _adaln_block,
        grid=(N_MOD_MIXER,),
        in_specs=[pl.BlockSpec(c_ctx.shape, lambda j: (0, 0)),
                  pl.BlockSpec(c.shape, lambda j: (0, 0)),
                  pl.BlockSpec((D_MODEL, tn), lambda j: (0, j)),
                  pl.BlockSpec((1, tn), lambda j: (0, j))],
        out_specs=pl.BlockSpec((COND_ROWS, tn), lambda j: (0, j)),
        out_shape=jax.ShapeDtypeStruct((COND_ROWS, N_MOD_MIXER * D_MODEL), F32),
        compiler_params=_params(("arbitrary",)),
        name="adaln",
    )(c_ctx, c, w_mod, b_mod)


def _rope_block(x, cos, sin_lo, sin_hi):
    return x * cos + pltpu.roll(x, LANES - 8, 1) * sin_lo + pltpu.roll(x, 8, 1) * sin_hi


def _store_kv(kvn, kp, kc_ref, v_ref, g=slice(None)):
    for hd in range(MLA_HEADS):
        sl = slice(hd * HEAD_BLOCK, (hd + 1) * HEAD_BLOCK)
        kc_ref[g, sl] = (kvn[:, sl] + kp).astype(BF16)
    v0 = MLA_HEADS * HEAD_BLOCK
    ones = jnp.ones((kvn.shape[0], LANES), BF16)
    for pair in range(MLA_HEADS // 2):
        v_ref[g, pair * V_SLAB:pair * V_SLAB + LANES] = kvn[:, v0 + pair * LANES:v0 + (pair + 1) * LANES].astype(BF16)
        v_ref[g, pair * V_SLAB + LANES:(pair + 1) * V_SLAB] = ones


def _mixer_input(x, g, shift, scale):
    return (_rms(x, g) * (1.0 + scale) + shift).astype(BF16)


def _inproj_kernel(*refs, rope, mod_row0, tiles_per_seq):
    if rope:
        (x_ref, mod_ref, cos_ref, slo_ref, shi_ref, gmix_ref, win_ref, gq_ref, wuq_ref, gkv_ref, wukv_ref,
         bg_ref, q_ref, kc_ref, v_ref, mq_ref, mk_ref, mv_ref, gate_ref) = refs
    else:
        (x_ref, mod_ref, gmix_ref, win_ref, gq_ref, wuq_ref, gkv_ref, wukv_ref,
         bg_ref, q_ref, kc_ref, v_ref, mq_ref, mk_ref, mv_ref, gate_ref, ckv_ref, kpet_ref) = refs
    shift, scale = _mod_chunks(mod_ref, mod_row0, tiles_per_seq)
    for r0 in range(0, x_ref.shape[0], INPROJ_ROWS):
        g = slice(r0, r0 + INPROJ_ROWS)
        if rope:
            cos, slo, shi = cos_ref[g, :], slo_ref[g, :], shi_ref[g, :]
        hb = _mixer_input(x_ref[g, :], gmix_ref[...], shift, scale)

        def proj(seg):
            return _dot_nt(hb, win_ref[seg[0]:seg[1], :])

        qn = _rms(proj(SEG_Q), gq_ref[...]).astype(BF16)
        ckv = _rms(proj(SEG_KV), gkv_ref[...])
        zkpe = proj(SEG_KPE)
        gate_ref[g, :] = proj(SEG_GATE) + bg_ref[...]

        q = _dot(qn, wuq_ref[...])
        for hd in range(MLA_HEADS):
            sl = slice(hd * HEAD_BLOCK, (hd + 1) * HEAD_BLOCK)
            qh = q[:, sl]
            if rope:
                qh = _rope_block(qh, cos, slo, shi)
            q_ref[g, sl] = (qh * Q_PRESCALE).astype(BF16)

        if not rope:
            ckv_ref[g, :] = ckv
            kpe_t = zkpe.T
            seq = kpet_ref.shape[2]
            for sq in range(INPROJ_ROWS // seq):
                kpet_ref[r0 // seq + sq] = kpe_t[:ROPE_DIM, sq * seq:(sq + 1) * seq]
        kp = pltpu.roll(zkpe, ROPE_LANE0, 1)
        if rope:
            kp = _rope_block(kp, cos, slo, shi)
        kvn = _dot(ckv.astype(BF16), wukv_ref[...])
        _store_kv(kvn, kp, kc_ref, v_ref, g)

        mv_ref[g, :] = proj(SEG_MV).astype(BF16)
        mk_ref[g, :] = (proj(SEG_MK) * (MLSTM_DK ** -0.5)).astype(BF16)
        mq_ref[g, :] = proj(SEG_MQ).astype(BF16)


def _inproj(x, mod, rope_tabs, gmix, w_in_p, gq, w_uq_p, gkv, w_ukv_p, bg, *, seq_len):
    n_tok = x.shape[0]
    tm = TOKEN_TILE
    tiles_per_seq = seq_len // tm
    rope = rope_tabs is not None
    tok = lambda c: pl.BlockSpec((tm, c), lambda i: (i, 0))
    in_specs = [tok(D_MODEL), pl.BlockSpec(mod.shape, lambda i: (0, 0))]
    args = [x, mod]
    if rope:
        in_specs += [pl.BlockSpec((tm, LANES), lambda i: (i % tiles_per_seq, 0))] * 3
        args += list(rope_tabs)
    win_rows = SEG_MO[0]
    in_specs += [_const_spec(gmix.shape), _const_spec((win_rows, w_in_p.shape[1]))]
    in_specs += [_const_spec(a.shape) for a in (gq, w_uq_p, gkv, w_ukv_p, bg)]
    args += [gmix, w_in_p, gq, w_uq_p, gkv, w_ukv_p, bg]
    widths = [(MLA_HEADS * HEAD_BLOCK, BF16), (MLA_HEADS * HEAD_BLOCK, BF16), (V_WIDTH, BF16),
              (MLSTM_HEADS * MLSTM_DK, BF16), (MLSTM_HEADS * MLSTM_DK, BF16), (MLSTM_HEADS * MLSTM_DV, BF16),
              (LANES, F32)]
    out_specs = [tok(c) for c, _ in widths]
    out_shape = [jax.ShapeDtypeStruct((n_tok, c), dt) for c, dt in widths]
    if not rope:
        out_specs += [tok(KV_RANK), pl.BlockSpec((tm // seq_len, ROPE_DIM, seq_len), lambda i: (i, 0, 0))]
        out_shape += [jax.ShapeDtypeStruct((n_tok, KV_RANK), F32),
                      jax.ShapeDtypeStruct((n_tok // seq_len, ROPE_DIM, seq_len), F32)]
    return pl.pallas_call(
        functools.partial(_inproj_kernel, rope=rope, mod_row0=1 if rope else 0,
                          tiles_per_seq=tiles_per_seq if rope else None),
        grid=(n_tok // tm,),
        in_specs=in_specs,
        out_specs=out_specs,
        out_shape=out_shape,
        compiler_params=_params(("parallel",)),
        name="inproj_lat" if rope else "inproj_ctx",
    )(*args)


def _cachekv_kernel(ckv_ref, kp_ref, wukv_ref, kc_ref, v_ref):
    kvn = _dot(ckv_ref[...].astype(BF16), wukv_ref[...])
    _store_kv(kvn, kp_ref[...], kc_ref, v_ref)


def _cachekv(cache_ckv, krope_blk, w_ukv_p):
    b, past, _ = cache_ckv.shape
    return pl.pallas_call(
        _cachekv_kernel,
        grid=(b,),
        in_specs=[pl.BlockSpec((None, past, KV_RANK), lambda i: (i, 0, 0)),
                  pl.BlockSpec((None, past, LANES), lambda i: (i, 0, 0)),
                  _const_spec(w_ukv_p.shape)],
        out_specs=[pl.BlockSpec((None, past, MLA_HEADS * HEAD_BLOCK), lambda i: (i, 0, 0)),
                   pl.BlockSpec((None, past, V_WIDTH), lambda i: (i, 0, 0))],
        out_shape=[jax.ShapeDtypeStruct((b, past, MLA_HEADS * HEAD_BLOCK), BF16),
                   jax.ShapeDtypeStruct((b, past, V_WIDTH), BF16)],
        compiler_params=_params(("parallel",)),
        name="cachekv",
    )(cache_ckv, krope_blk, w_ukv_p)


def _attn_kernel(*refs, has_cache, n_cast, has_adaln):
    n_in = 5 if has_cache else 3
    if has_cache:
        q_ref, k_ref, v_ref, kc_ref, vc_ref = refs[:n_in]
    else:
        q_ref, k_ref, v_ref = refs[:n_in]
    n_all_in = n_in + n_cast + (4 if has_adaln else 0)
    o_ref = refs[n_all_in]
    for w_ref, wb_ref in zip(refs[n_in:n_in + n_cast], refs[n_all_in + 1:n_all_in + 1 + n_cast]):
        wb_ref[...] = w_ref[...].astype(BF16)
    if has_adaln:
        _adaln_block(*refs[n_in + n_cast:n_all_in], refs[n_all_in + 1 + n_cast])
    tq = q_ref.shape[1]
    lane = lax.broadcasted_iota(jnp.int32, (tq, LANES), 1)
    for sq in range(q_ref.shape[0]):
        hsl = [slice(hd * HEAD_BLOCK, (hd + 1) * HEAD_BLOCK) for hd in range(MLA_HEADS)]
        vsl = [slice(hd // 2 * V_SLAB, (hd // 2 + 1) * V_SLAB) for hd in range(MLA_HEADS)]
        s = [_dot_nt(q_ref[sq, :, sl], k_ref[sq, :, sl]) for sl in hsl]
        m = [jnp.max(si, axis=-1, keepdims=True) for si in s]
        if has_cache:
            sc = [_dot_nt(q_ref[sq, :, sl], kc_ref[sq, :, sl]) for sl in hsl]
            m = [jnp.maximum(mi, jnp.max(ci, axis=-1, keepdims=True)) for mi, ci in zip(m, sc)]
        p = [jnp.exp2(si - mi).astype(BF16) for si, mi in zip(s, m)]
        o = [_dot(pi, v_ref[sq, :, sl]) for pi, sl in zip(p, vsl)]
        if has_cache:
            pc = [jnp.exp2(ci - mi).astype(BF16) for ci, mi in zip(sc, m)]
            o = [oi + _dot(pi, vc_ref[sq, :, sl]) for oi, pi, sl in zip(o, pc, vsl)]
        outs = [oi[:, :LANES] * (1.0 / oi[:, LANES:]) for oi in o]
        for pair in range(MLA_HEADS // 2):
            o_ref[sq, :, pair * LANES:(pair + 1) * LANES] = jnp.where(
                lane < MLA_V_DIM, outs[2 * pair], outs[2 * pair + 1]).astype(BF16)


def _attention(q, k, v, cache=None, cast_weights=(), adaln_tail=None, seqs_per_step=1):
    nb, s, _ = q.shape
    tq = Q_TILE
    ns = seqs_per_step
    b = nb // ns
    n_steps = b * (s // tq)
    kw, vw, ow = MLA_HEADS * HEAD_BLOCK, V_WIDTH, MLA_HEADS * MLA_V_DIM
    in_specs = [pl.BlockSpec((ns, tq, kw), lambda i, j: (i, j, 0)),
                pl.BlockSpec((ns, s, kw), lambda i, j: (i, 0, 0)),
                pl.BlockSpec((ns, s, vw), lambda i, j: (i, 0, 0))]
    args = [q, k, v]
    if cache is not None:
        past = cache[0].shape[1]
        in_specs += [pl.BlockSpec((ns, past, kw), lambda i, j: (i, 0, 0)),
                     pl.BlockSpec((ns, past, vw), lambda i, j: (i, 0, 0))]
        args += list(cache)
    step = lambda i, j: i * (s // tq) + j
    w_specs = [pl.BlockSpec((w.shape[0] // n_steps, w.shape[1]), lambda i, j: (step(i, j), 0)) for w in cast_weights]
    extra_in, extra_out, extra_shape = [], [], []
    if adaln_tail is not None:
        tn = N_MOD_TAIL * D_MODEL // n_steps
        col0 = N_MOD_MIXER * D_MODEL // tn
        extra_in = [pl.BlockSpec(adaln_tail[0].shape, lambda i, j: (0, 0)),
                    pl.BlockSpec(adaln_tail[1].shape, lambda i, j: (0, 0)),
                    pl.BlockSpec((D_MODEL, tn), lambda i, j: (0, col0 + step(i, j))),
                    pl.BlockSpec((1, tn), lambda i, j: (0, col0 + step(i, j)))]
        extra_out = [pl.BlockSpec((COND_ROWS, tn), lambda i, j: (0, step(i, j)))]
        extra_shape = [jax.ShapeDtypeStruct((COND_ROWS, N_MOD_TAIL * D_MODEL), F32)]
    outs = pl.pallas_call(
        functools.partial(_attn_kernel, has_cache=cache is not None, n_cast=len(cast_weights),
                          has_adaln=adaln_tail is not None),
        grid=(b, s // tq),
        in_specs=in_specs + w_specs + extra_in,
        out_specs=[pl.BlockSpec((ns, tq, ow), lambda i, j: (i, j, 0))] + w_specs + extra_out,
        out_shape=[jax.ShapeDtypeStruct((nb, s, ow), BF16)]
                  + [jax.ShapeDtypeStruct(w.shape, BF16) for w in cast_weights] + extra_shape,
        compiler_params=_params(("parallel", "arbitrary")),
        name="attn_lat" if cache is not None else "attn_ctx",
    )(*args, *cast_weights, *(adaln_tail or ()))
    return outs[0] if len(outs) == 1 else outs


def _lane_bcast(x, k, width=None):
    y = jnp.broadcast_to(x[:, k:k + 1], x.shape)
    reps = (width or LANES) // LANES
    return y if reps == 1 else jnp.concatenate([y] * reps, axis=1)


def _prefix_sum_rows(x):
    n = x.shape[0]
    r = lax.broadcasted_iota(jnp.int32, (n, n), 0)
    c = lax.broadcasted_iota(jnp.int32, (n, n), 1)
    tri = jnp.where(c <= r, 1.0, 0.0).astype(BF16)
    hi = x.astype(BF16)
    r1 = x - hi.astype(F32)
    mid = r1.astype(BF16)
    lo = (r1 - mid.astype(F32)).astype(BF16)
    return _dot(tri, hi) + _dot(tri, mid) + _dot(tri, lo)


def _gate_prep(g):
    n = g.shape[0]
    lane = lax.broadcasted_iota(jnp.int32, (n, LANES), 1)
    fpre = pltpu.roll(g, LANES - N_DIRHEAD, 1)
    lf = jnp.minimum(fpre, 0.0) - jnp.log(1.0 + jnp.exp(-jnp.abs(fpre)))
    binc = _prefix_sum_rows(lf)
    tot = binc[n - 1:n, :]
    b = jnp.where(lane < MLSTM_HEADS, binc, tot - binc + lf)
    return b, g - b, tot


def _chunk_setup(b, u, tot, m_prev):
    g_last = jnp.maximum(jnp.max(u, axis=0, keepdims=True), m_prev)
    u2 = u * LOG2E
    w = jnp.exp2(u2 - g_last * LOG2E)
    return u2, (b * LOG2E).T, w.T, tot + g_last, jnp.exp(m_prev - g_last)


def _state_update(vt, k, wt, kk):
    c_new = _dot((vt * wt[kk:kk + 1, :]).astype(BF16), k)
    n_new = _dot(wt[:N_DIRHEAD, :].astype(BF16), k)[kk:kk + 1, :]
    return c_new, n_new


def _mlstm_dir_t(s0t, vt, u_b, bt_row, m_row, *, fwd, inter=None):
    sdt, den, g = _dir_weights(s0t, u_b, m_row, fwd=fwd)
    return _dir_finish(_dot(vt, sdt), den, g, bt_row, m_row, inter)


def _dir_weights(s0t, u_b, m_row, *, fwd):
    n = s0t.shape[0]
    r = lax.broadcasted_iota(jnp.int32, (n, n), 0)
    c = lax.broadcasted_iota(jnp.int32, (n, n), 1)
    a = jnp.where((r <= c) if fwd else (r >= c), u_b, -jnp.inf)
    g = jnp.maximum(jnp.max(a, axis=0, keepdims=True), m_row)
    sdt = s0t * jnp.exp2(a - g)
    return sdt.astype(BF16), jnp.sum(sdt, axis=0, keepdims=True), g


def _dir_finish(ht, den, g, bt_row, m_row, inter=None):
    if inter is not None:
        w_inter = jnp.exp2(m_row - g)
        ht = ht + w_inter * inter[0]
        den = den + w_inter * inter[1]
    floor = jnp.exp2(-(bt_row + g))
    return ht * (1.0 / jnp.maximum(jnp.abs(den), floor))


def _mlstm_ctx_kernel(q_ref, k_ref, v_ref, g_ref, ht_ref, c_ref, n_ref, m_ref):
    n = q_ref.shape[1]
    zero_lanes = jnp.zeros((1, LANES), F32)
    zero_row = jnp.zeros((1, n), F32)
    dirs = ((True, 0), (False, MLSTM_HEADS))
    setup = [_chunk_setup(*_gate_prep(g_ref[sq]), zero_lanes) for sq in range(q_ref.shape[0])]
    units = [(sq, hd) for sq in range(q_ref.shape[0]) for hd in range(MLSTM_HEADS)]
    ks = [k_ref[sq, :, hd * MLSTM_DK:(hd + 1) * MLSTM_DK] for sq, hd in units]
    s0ts = [_dot_nt(k, q_ref[sq, :, hd * MLSTM_DK:(hd + 1) * MLSTM_DK]) for k, (sq, hd) in zip(ks, units)]
    vts = [v_ref[sq, :, hd * MLSTM_DV:(hd + 1) * MLSTM_DV].T for sq, hd in units]
    wts = [[_dir_weights(s0t, _lane_bcast(setup[sq][0], off + hd, n), zero_row, fwd=fwd) for fwd, off in dirs]
           for s0t, (sq, hd) in zip(s0ts, units)]
    hts = [[_dot(vt, sdt) for sdt, _, _ in wt] for vt, wt in zip(vts, wts)]
    for (sq, hd), k, vt, wt, ht2 in zip(units, ks, vts, wts, hts):
        _, bt2, w_t, _, _ = setup[sq]
        ht_ref[sq, hd * MLSTM_DV:(hd + 1) * MLSTM_DV, :] = sum(
            _dir_finish(ht_d, den, g, bt2[off + hd:off + hd + 1, :], zero_row)
            for ht_d, (_, den, g), (_, off) in zip(ht2, wt, dirs))
        for _, off in dirs:
            c_ref[sq, off + hd], n_ref[sq, off + hd:off + hd + 1, :] = _state_update(vt, k, w_t, off + hd)
    for sq in range(q_ref.shape[0]):
        m_ref[sq] = setup[sq][3]


def _mlstm_ctx(mq, mk, mv, gates):
    b, s, _ = mq.shape
    ns = CTX_SEQS_PER_STEP
    seq = lambda c: pl.BlockSpec((ns, s, c), lambda i: (i, 0, 0))
    return pl.pallas_call(
        _mlstm_ctx_kernel,
        grid=(b // ns,),
        in_specs=[seq(MLSTM_HEADS * MLSTM_DK), seq(MLSTM_HEADS * MLSTM_DK), seq(MLSTM_HEADS * MLSTM_DV), seq(LANES)],
        out_specs=[pl.BlockSpec((ns, MLSTM_HEADS * MLSTM_DV, s), lambda i: (i, 0, 0)),
                   pl.BlockSpec((ns, N_DIRHEAD, MLSTM_DV, MLSTM_DK), lambda i: (i, 0, 0, 0)),
                   pl.BlockSpec((ns, N_DIRHEAD, MLSTM_DK), lambda i: (i, 0, 0)),
                   pl.BlockSpec((ns, 1, LANES), lambda i: (i, 0, 0))],
        out_shape=[jax.ShapeDtypeStruct((b, MLSTM_HEADS * MLSTM_DV, s), F32),
                   jax.ShapeDtypeStruct((b, N_DIRHEAD, MLSTM_DV, MLSTM_DK), F32),
                   jax.ShapeDtypeStruct((b, N_DIRHEAD, MLSTM_DK), F32),
                   jax.ShapeDtypeStruct((b, 1, LANES), F32)],
        compiler_params=_params(("parallel",)),
        name="mlstm_ctx",
    )(mq, mk, mv, gates)


def _mlstm_lat_kernel(qf_ref, kf_ref, vf_ref, gf_ref, qb_ref, kb_ref, vb_ref, gb_ref, c0_ref, n0_ref, m0_ref,
                      ht_ref, c_ref, n_ref, m_ref, *, n_chunks):
    step = pl.program_id(1)

    @pl.when(step == 0)
    def _():
        c_ref[...] = c0_ref[...]
        n_ref[...] = n0_ref[...]
        m_ref[...] = m0_ref[...]
        ht_ref[...] = jnp.zeros_like(ht_ref)

    n = qf_ref.shape[0]
    gate_f, gate_b = _gate_prep(gf_ref[...]), _gate_prep(gb_ref[...])
    is_f = lax.broadcasted_iota(jnp.int32, (n, LANES), 1) < MLSTM_HEADS
    b, u, tot = (jnp.where(is_f[:x.shape[0]], x, y) for x, y in zip(gate_f, gate_b))
    m_prev = m_ref[...]
    u2, bt2, w_t, m_new, decay = _chunk_setup(b, u, tot, m_prev)
    m2_prev = m_prev * LOG2E
    units = [(fwd, hd, hd if fwd else MLSTM_HEADS + hd) for fwd in (True, False) for hd in range(MLSTM_HEADS)]
    refs = {True: (qf_ref, kf_ref, vf_ref), False: (qb_ref, kb_ref, vb_ref)}
    qs = [refs[fwd][0][:, hd * MLSTM_DK:(hd + 1) * MLSTM_DK] for fwd, hd, _ in units]
    ks = [refs[fwd][1][:, hd * MLSTM_DK:(hd + 1) * MLSTM_DK] for fwd, hd, _ in units]
    vts = [refs[fwd][2][:, hd * MLSTM_DV:(hd + 1) * MLSTM_DV].T for fwd, hd, _ in units]
    s0ts = [_dot_nt(k, q) for k, q in zip(ks, qs)]
    c_prevs = [c_ref[kk] for _, _, kk in units]
    n_prevs = [n_ref[kk:kk + 1, :] for _, _, kk in units]
    inters = [(_dot_nt(c.astype(BF16), q), _dot_nt(jnp.broadcast_to(nv, (8, MLSTM_DK)).astype(BF16), q)[0:1, :])
              for c, nv, q in zip(c_prevs, n_prevs, qs)]
    m_rows = [_lane_bcast(m2_prev, kk, n) for _, _, kk in units]
    wts = [_dir_weights(s0t, _lane_bcast(u2, kk, n), m_row, fwd=fwd)
           for s0t, m_row, (fwd, _, kk) in zip(s0ts, m_rows, units)]
    hts = [_dot(vt, sdt) for vt, (sdt, _, _) in zip(vts, wts)]
    for (fwd, hd, kk), k, vt, (_, den, g), ht, inter, m_row, c_prev, n_prev in zip(
            units, ks, vts, wts, hts, inters, m_rows, c_prevs, n_prevs):
        chunk = step if fwd else n_chunks - 1 - step
        ht_ref[chunk, hd * MLSTM_DV:(hd + 1) * MLSTM_DV, :] += _dir_finish(ht, den, g, bt2[kk:kk + 1, :], m_row, inter)
        c_new, n_new = _state_update(vt, k, w_t, kk)
        dk = _lane_bcast(decay, kk)
        c_ref[kk] = dk * c_prev + c_new
        n_ref[kk:kk + 1, :] = dk * n_prev + n_new
    m_ref[...] = m_new


def _mlstm_lat(mq, mk, mv, gates, c0, n0, m0):
    b, s, _ = mq.shape
    lc = MLSTM_CHUNK
    nc = s // lc
    fw = lambda c: pl.BlockSpec((None, lc, c), lambda i, j: (i, j, 0))
    bw = lambda c: pl.BlockSpec((None, lc, c), lambda i, j: (i, nc - 1 - j, 0))
    dk, dv = MLSTM_HEADS * MLSTM_DK, MLSTM_HEADS * MLSTM_DV
    return pl.pallas_call(
        functools.partial(_mlstm_lat_kernel, n_chunks=nc),
        grid=(b, nc),
        in_specs=[fw(dk), fw(dk), fw(dv), fw(LANES), bw(dk), bw(dk), bw(dv), bw(LANES),
                  pl.BlockSpec((None, N_DIRHEAD, MLSTM_DV, MLSTM_DK), lambda i, j: (i, 0, 0, 0)),
                  pl.BlockSpec((None, N_DIRHEAD, MLSTM_DK), lambda i, j: (i, 0, 0)),
                  pl.BlockSpec((None, 1, LANES), lambda i, j: (i, 0, 0))],
        out_specs=pl.BlockSpec((nc, dv, lc), lambda i, j: (i, 0, 0)),
        out_shape=jax.ShapeDtypeStruct((b * nc, dv, lc), F32),
        scratch_shapes=[pltpu.VMEM((N_DIRHEAD, MLSTM_DV, MLSTM_DK), F32),
                        pltpu.VMEM((N_DIRHEAD, MLSTM_DK), F32),
                        pltpu.VMEM((1, LANES), F32)],
        compiler_params=_params(("parallel", "arbitrary")),
        name="mlstm_lat",
    )(mq, mk, mv, gates, mq, mk, mv, gates, c0, n0, m0)


def _tail_kernel(x_ref, modin_ref, mod_ref, att_ref, hst_ref, gmix_ref, wgate_ref, gml_ref, womla_ref, womlstm_ref,
                 wout_ref, gffn_ref, wfin_ref, wfout_ref, gfin_ref, y_ref, *, mod_row0, tiles_per_seq):
    shift_mix, scale_mix = _mod_chunks(modin_ref, mod_row0, tiles_per_seq)
    gate_mix, shift_ffn, scale_ffn, gate_ffn = _mod_chunks(mod_ref, mod_row0, tiles_per_seq)
    gml = gml_ref[...]
    tm = x_ref.shape[0]
    groups = [slice(r, r + TAIL_ROWS) for r in range(0, tm, TAIL_ROWS)]
    dv = MLSTM_HEADS * MLSTM_DV
    hb = [_mixer_input(x_ref[g, :], gmix_ref[...], shift_mix, scale_mix) for g in groups]
    smo = [_sigmoid(_dot_nt(h, wgate_ref[:dv, :])) for h in hb]
    g_a = [_sigmoid(_dot_nt(h, wgate_ref[dv:dv + D_MODEL, :])) for h in hb]
    g_b = [_sigmoid(_dot_nt(h, wgate_ref[dv + D_MODEL:, :])) for h in hb]
    a = [_dot(att_ref[g, :], womla_ref[...]) for g in groups]
    hm = []
    for gi in range(len(groups)):
        hs = hst_ref[gi].T
        parts = []
        for hd in range(MLSTM_HEADS):
            sl = slice(hd * MLSTM_DV, (hd + 1) * MLSTM_DV)
            parts.append((_rms(hs[:, sl], gml[:, sl]) * smo[gi][:, sl]).astype(BF16))
        hm.append(jnp.concatenate(parts, axis=1))
    bm = [_dot(h, womlstm_ref[...]) for h in hm]
    merged = [(ga_i * ai + gb_i * bi).astype(BF16) for ga_i, gb_i, ai, bi in zip(g_a, g_b, a, bm)]
    x1 = [x_ref[g, :] + gate_mix * _dot(mi, wout_ref[...]) for g, mi in zip(groups, merged)]
    h2 = [(_rms(xi, gffn_ref[...]) * (1.0 + scale_ffn) + shift_ffn).astype(BF16) for xi in x1]
    ga = [_dot(hi, wfin_ref[:, :FFN_HIDDEN]) for hi in h2]
    gu = [_dot(hi, wfin_ref[:, FFN_HIDDEN:]) for hi in h2]
    act = [(gi * _sigmoid(gi) * ui).astype(BF16) for gi, ui in zip(ga, gu)]
    for g, xi, ci in zip(groups, x1, act):
        y_ref[g, :] = _rms(xi + gate_ffn * _dot(ci, wfout_ref[...]), gfin_ref[...])


def _tail(x, mod_in, mod, att, hst, gmix, w_in_p, gml, w_o_mla, w_o_mlstm, w_out, gffn, w_ffn_in, w_ffn_out, gfin, *,
          seq_len, latent):
    n_tok = x.shape[0]
    tm = TOKEN_TILE
    assert hst.shape[0] * TAIL_ROWS == n_tok and hst.shape[2] == TAIL_ROWS
    assert SEG_MO[1] == SEG_BR[0]
    tok = lambda c: pl.BlockSpec((tm, c), lambda i: (i, 0))
    whole = lambda a: pl.BlockSpec(a.shape, lambda i: (0, 0))
    gate_rows = pl.BlockSpec((pl.Element(SEG_BR[1] - SEG_MO[0]), pl.Element(w_in_p.shape[1])),
                             lambda i: (SEG_MO[0], 0), pipeline_mode=pl.Buffered(1))
    consts = (gml, w_o_mla, w_o_mlstm, w_out, gffn, w_ffn_in, w_ffn_out, gfin)
    return pl.pallas_call(
        functools.partial(_tail_kernel, mod_row0=1 if latent else 0, tiles_per_seq=seq_len // tm if latent else None),
        grid=(n_tok // tm,),
        in_specs=[tok(D_MODEL), whole(mod_in), whole(mod), tok(att.shape[1]),
                  pl.BlockSpec((tm // TAIL_ROWS,) + hst.shape[1:], lambda i: (i, 0, 0)),
                  _const_spec(gmix.shape), gate_rows]
                 + [_const_spec(a.shape) for a in consts],
        out_specs=tok(D_MODEL),
        out_shape=jax.ShapeDtypeStruct((n_tok, D_MODEL), F32),
        compiler_params=_params(("parallel",)),
        name="tail_lat" if latent else "tail_ctx",
    )(x, mod_in, mod, att, hst, gmix, w_in_p, *consts)


_IN_SIZES = (Q_RANK, KV_RANK, ROPE_DIM, MLSTM_HEADS * MLSTM_DK, MLSTM_HEADS * MLSTM_DK, MLSTM_HEADS * MLSTM_DV,
             4 * MLSTM_HEADS, MLSTM_HEADS * MLSTM_DV, 2 * D_MODEL)
_IN_OFF = tuple(sum(_IN_SIZES[:i]) for i in range(len(_IN_SIZES) + 1))
PACK_ROWS = LANES
_PAD_AFTER = tuple((_SEG_OFF[i + 1] // PACK_ROWS, _SEG_WIDTHS[i] - _IN_SIZES[i], _IN_SIZES[i] % PACK_ROWS)
                   for i in range(len(_IN_SIZES)) if _SEG_WIDTHS[i] != _IN_SIZES[i])


PACK_PIECES = 8
N_PACK_STEPS = pl.cdiv(N_IN_PACKED // PACK_ROWS, PACK_PIECES)
_N_IN = _IN_OFF[-1]


def _pack_src_row(p):
    src = p * PACK_ROWS
    for first_piece_after, pad, _ in _PAD_AFTER:
        src = src - jnp.where(p >= first_piece_after, pad, 0)
    src = jnp.minimum(src, _N_IN - PACK_ROWS)
    return pl.multiple_of(src, math.gcd(PACK_ROWS, _N_IN, *(pad for _, pad, _ in _PAD_AFTER)))


def _pack_in_kernel(*refs):
    w_refs, o_ref = refs[:PACK_PIECES], refs[PACK_PIECES]
    row = lax.broadcasted_iota(jnp.int32, w_refs[0].shape, 0)
    for j, w_ref in enumerate(w_refs):
        p = pl.program_id(0) * PACK_PIECES + j
        valid = jnp.where(p < N_IN_PACKED // PACK_ROWS, PACK_ROWS, 0)
        for first_piece_after, _, width in _PAD_AFTER:
            valid = jnp.where(p == first_piece_after - 1, width, valid)
        o_ref[j * PACK_ROWS:(j + 1) * PACK_ROWS, :] = jnp.where(row < valid, w_ref[...], 0.0).astype(BF16)


def _pack_in(w_in_t):
    n, k = w_in_t.shape
    piece = lambda j: pl.BlockSpec((pl.Element(PACK_ROWS), pl.Element(k)),
                                   lambda i: (_pack_src_row(i * PACK_PIECES + j), 0))
    return pl.pallas_call(
        _pack_in_kernel,
        grid=(N_PACK_STEPS,),
        in_specs=[piece(j) for j in range(PACK_PIECES)],
        out_specs=pl.BlockSpec((PACK_PIECES * PACK_ROWS, k), lambda i: (i, 0)),
        out_shape=jax.ShapeDtypeStruct((N_PACK_STEPS * PACK_PIECES * PACK_ROWS, k), BF16),
        compiler_params=_params(("parallel",)),
        name="pack_in",
    )(*([w_in_t] * PACK_PIECES))


def _pack_small(w_uq, w_ukv, b_gates):
    qd = NOPE_DIM + ROPE_DIM
    w_uq_p = jnp.pad(w_uq.reshape(Q_RANK, MLA_HEADS, qd), ((0, 0), (0, 0), (0, HEAD_BLOCK - qd)))
    w_uq_p = w_uq_p.reshape(Q_RANK, MLA_HEADS * HEAD_BLOCK).astype(BF16)
    kv = w_ukv.reshape(KV_RANK, MLA_HEADS, NOPE_DIM + MLA_V_DIM)
    wk = jnp.pad(kv[:, :, :NOPE_DIM], ((0, 0), (0, 0), (0, HEAD_BLOCK - NOPE_DIM))).reshape(KV_RANK, -1)
    wv = kv[:, :, NOPE_DIM:].reshape(KV_RANK, -1)
    w_ukv_p = jnp.concatenate([wk, wv], axis=1).astype(BF16)
    bg = jnp.pad(b_gates, ((0, 0), (0, LANES - b_gates.shape[1])))
    return w_uq_p, w_ukv_p, bg


def _rope_tables(n_tokens):
    pos = np.arange(n_tokens)
    row = (pos // GRID_W).astype(np.float64)[:, None]
    col = (pos % GRID_W).astype(np.float64)[:, None]
    half = ROPE_DIM // 2
    inv = (np.float32(ROPE_BASE) ** (-np.arange(0, half, 2, dtype=np.float32) / np.float32(half))).astype(np.float64)
    r = np.arange(LANES) - ROPE_LANE0
    in_rope = (r >= 0) & (r < ROPE_DIM)
    rr = np.clip(r, 0, ROPE_DIM - 1)
    freq = inv[rr % (half // 2)][None, :]
    ang = np.where((rr // half == 0)[None, :], row * freq, col * freq).astype(np.float32).astype(np.float64)
    first = (rr % half) < (half // 2)
    cos = np.where(in_rope[None, :], np.cos(ang), 1.0)
    sin = np.sin(ang)
    sin_lo = np.where((in_rope & first)[None, :], -sin, 0.0)
    sin_hi = np.where((in_rope & ~first)[None, :], sin, 0.0)
    return tuple(jnp.asarray(t, dtype=F32) for t in (cos, sin_lo, sin_hi))


def kernel(x_prompt, x_sample, cache_ckv, cache_krope, state_C, state_n, state_m, c, c_ctx, w_mod, b_mod, g_norm_mix,
           w_in, b_gates, g_q_norm, w_uq, g_kv_norm, w_ukv, g_mlstm_norm, w_o_mla, w_o_mlstm, w_out, g_norm_ffn,
           w_ffn_in, w_ffn_out, g_final):
    bp, sp, d = x_prompt.shape
    bs, ss, _ = x_sample.shape
    layer = 0
    assert w_mod.shape[0] == 1 and sp == MLSTM_CHUNK and ss % MLSTM_CHUNK == 0

    adaln_args = (c_ctx[None, :], c, w_mod[layer], b_mod[layer][None, :])
    mod_in = _adaln(*adaln_args)

    w_in_p = _pack_in(w_in[layer].T)
    w_uq_p, w_ukv_p, bg = _pack_small(w_uq[layer], w_ukv[layer], b_gates[layer][None, :])
    row = lambda g: g[layer][None, :]
    shared_in = (row(g_norm_mix), w_in_p, row(g_q_norm), w_uq_p, row(g_kv_norm), w_ukv_p, bg)
    seq3 = lambda a, b_, s_: a.reshape(b_, s_, a.shape[-1])

    xl = x_sample.reshape(bs * ss, d)
    q, kc, v, mq_l, mk_l, mv_l, gates_l = _inproj(xl, mod_in, _rope_tables(ss), *shared_in, seq_len=ss)
    krope_blk = jnp.pad(cache_krope[:, layer], ((0, 0), (0, 0), (ROPE_LANE0, LANES - ROPE_LANE0 - ROPE_DIM)))
    cache = _cachekv(cache_ckv[:, layer], krope_blk, w_ukv_p)
    att_l, womla_b, womlstm_b, wout_b, wfin_b, wfout_b, mod_tail = _attention(
        seq3(q, bs, ss), seq3(kc, bs, ss), seq3(v, bs, ss), cache,
        cast_weights=(w_o_mla[layer], w_o_mlstm[layer], w_out[layer], w_ffn_in[layer], w_ffn_out[layer]),
        adaln_tail=adaln_args)
    tail_w = (row(g_norm_mix), w_in_p, row(g_mlstm_norm), womla_b, womlstm_b, wout_b, row(g_norm_ffn), wfin_b, wfout_b,
              g_final[None, :])

    xc = x_prompt.reshape(bp * sp, d)
    q, kc, v, mq, mk, mv, gates, ckv, kpe = _inproj(xc, mod_in, None, *shared_in, seq_len=sp)
    att = _attention(seq3(q, bp, sp), seq3(kc, bp, sp), seq3(v, bp, sp), seqs_per_step=CTX_SEQS_PER_STEP)
    hs, new_c, new_n, new_m = _mlstm_ctx(seq3(mq, bp, sp), seq3(mk, bp, sp), seq3(mv, bp, sp), seq3(gates, bp, sp))
    y_prompt = _tail(xc, mod_in, mod_tail, att.reshape(bp * sp, -1), hs, *tail_w,
                     seq_len=sp, latent=False).reshape(bp, sp, d)
    new_ckv = ckv.reshape(bp, 1, sp, KV_RANK)
    new_krope = jnp.swapaxes(kpe, 1, 2).reshape(bp, 1, sp, ROPE_DIM)
    new_C = new_c.reshape(bp, 1, 2, MLSTM_HEADS, MLSTM_DV, MLSTM_DK)
    new_N = new_n.reshape(bp, 1, 2, MLSTM_HEADS, MLSTM_DK)
    new_M = new_m[:, 0, :N_DIRHEAD].reshape(bp, 1, 2, MLSTM_HEADS)

    c0 = state_C[:, layer].reshape(bs, N_DIRHEAD, MLSTM_DV, MLSTM_DK)
    n0 = state_n[:, layer].reshape(bs, N_DIRHEAD, MLSTM_DK)
    m0 = jnp.pad(state_m[:, layer].reshape(bs, 1, N_DIRHEAD), ((0, 0), (0, 0), (0, LANES - N_DIRHEAD)))
    hs = _mlstm_lat(seq3(mq_l, bs, ss), seq3(mk_l, bs, ss), seq3(mv_l, bs, ss), seq3(gates_l, bs, ss), c0, n0, m0)
    y_sample = _tail(xl, mod_in, mod_tail, att_l.reshape(bs * ss, -1), hs, *tail_w,
                     seq_len=ss, latent=True).reshape(bs, ss, d)
    return (y_prompt, y_sample, new_ckv, new_krope, new_C, new_N, new_M)
```

```python
import functools
import math

import jax
import jax.numpy as jnp
import numpy as np
from jax import lax
from jax.experimental import pallas as pl
from jax.experimental.pallas import tpu as pltpu

F32 = jnp.float32
BF16 = jnp.bfloat16

D_MODEL = 1024
GRID_W = 64
MLA_HEADS = 8
Q_RANK = 384
KV_RANK = 256
NOPE_DIM = 64
ROPE_DIM = 32
MLA_V_DIM = 64
ROPE_BASE = 10000.0
MLA_SCALE = (NOPE_DIM + ROPE_DIM) ** -0.5
MLSTM_HEADS = 4
MLSTM_DK = 128
MLSTM_DV = 256
FFN_HIDDEN = ((8 * D_MODEL // 3 + 255) // 256) * 256
EPS = 1e-6

LANES = 128
HEAD_BLOCK = LANES
ROPE_LANE0 = NOPE_DIM
N_DIRHEAD = 2 * MLSTM_HEADS
MLSTM_CHUNK = 256
LOG2E = math.log2(math.e)
Q_PRESCALE = MLA_SCALE * LOG2E
V_SLAB = 2 * LANES
V_WIDTH = (MLA_HEADS // 2) * V_SLAB
TOKEN_TILE = 512
TAIL_ROWS = 256
INPROJ_ROWS = TOKEN_TILE
Q_TILE = 256
CTX_SEQS_PER_STEP = 4
VMEM_LIMIT = 56 * 1024 * 1024

_SEG_WIDTHS = (Q_RANK, KV_RANK, LANES, MLSTM_HEADS * MLSTM_DK, MLSTM_HEADS * MLSTM_DK,
               MLSTM_HEADS * MLSTM_DV, LANES, MLSTM_HEADS * MLSTM_DV, 2 * D_MODEL)
_SEG_OFF = tuple(sum(_SEG_WIDTHS[:i]) for i in range(len(_SEG_WIDTHS) + 1))
SEG_Q, SEG_KV, SEG_KPE, SEG_MQ, SEG_MK, SEG_MV, SEG_GATE, SEG_MO, SEG_BR = (
    (_SEG_OFF[i], _SEG_OFF[i + 1]) for i in range(9))
N_IN_PACKED = _SEG_OFF[-1]


def _dot(a, b):
    return jnp.dot(a, b, preferred_element_type=F32)


def _dot_nt(a, b):
    return lax.dot_general(a, b, (((1,), (1,)), ((), ())), preferred_element_type=F32)


def _dot_tn(a, b):
    return lax.dot_general(a, b, (((0,), (0,)), ((), ())), preferred_element_type=F32)


def _rms(x, g):
    ms = jnp.mean(x * x, axis=-1, keepdims=True)
    return x * lax.rsqrt(ms + EPS) * g


def _sigmoid(x):
    return 1.0 / (1.0 + jnp.exp(-x))


def _const_spec(shape):
    nd = len(shape)
    return pl.BlockSpec(shape, lambda *_: (0,) * nd, pipeline_mode=pl.Buffered(1))


def _mod_chunks(mod_ref, row0, tiles_per_seq):
    row = row0 if tiles_per_seq is None else row0 + pl.program_id(0) // tiles_per_seq
    m = mod_ref[pl.ds(row, 1), :]
    return [m[:, k * D_MODEL:(k + 1) * D_MODEL] for k in range(m.shape[1] // D_MODEL)]


def _params(sem, flags=None, vmem_mib=None):
    limit = VMEM_LIMIT if vmem_mib is None else vmem_mib * 1024 * 1024
    return pltpu.CompilerParams(dimension_semantics=sem, vmem_limit_bytes=limit, flags=flags)


N_MOD_MIXER = 2
N_MOD_TAIL = 4


COND_ROWS = 8


def _adaln_block(cctx_ref, c_ref, w_ref, b_ref, o_ref):
    pad = jnp.zeros((COND_ROWS - 1 - c_ref.shape[0], D_MODEL), F32)
    c = jnp.concatenate([cctx_ref[...], c_ref[...], pad], axis=0)
    s = c * _sigmoid(c)
    o_ref[...] = _dot(s.astype(BF16), w_ref[...].astype(BF16)) + b_ref[...]


def _adaln(c_ctx, c, w_mod, b_mod):
    tn = D_MODEL
    return pl.pallas_call(
        _adaln_block,
        grid=(N_MOD_MIXER,),
        in_specs=[pl.BlockSpec(c_ctx.shape, lambda j: (0, 0)),
                  pl.BlockSpec(c.shape, lambda j: (0, 0)),
                  pl.BlockSpec((D_MODEL, tn), lambda j: (0, j)),
                  pl.BlockSpec((1, tn), lambda j: (0, j))],
        out_specs=pl.BlockSpec((COND_ROWS, tn), lambda j: (0, j)),
        out_shape=jax.ShapeDtypeStruct((COND_ROWS, N_MOD_MIXER * D_MODEL), F32),
        compiler_params=_params(("arbitrary",)),
        name="adaln",
    )(c_ctx, c, w_mod, b_mod)


def _rope_block(x, cos, sin_lo, sin_hi):
    return x * cos + pltpu.roll(x, LANES - 8, 1) * sin_lo + pltpu.roll(x, 8, 1) * sin_hi


def _store_kv(kvn, kp, kc_ref, v_ref, g=slice(None)):
    for hd in range(MLA_HEADS):
        sl = slice(hd * HEAD_BLOCK, (hd + 1) * HEAD_BLOCK)
        kc_ref[g, sl] = (kvn[:, sl] + kp).astype(BF16)
    v0 = MLA_HEADS * HEAD_BLOCK
    ones = jnp.ones((kvn.shape[0], LANES), BF16)
    for pair in range(MLA_HEADS // 2):
        v_ref[g, pair * V_SLAB:pair * V_SLAB + LANES] = kvn[:, v0 + pair * LANES:v0 + (pair + 1) * LANES].astype(BF16)
        v_ref[g, pair * V_SLAB + LANES:(pair + 1) * V_SLAB] = ones


def _mixer_input(x, g, shift, scale):
    return (_rms(x, g) * (1.0 + scale) + shift).astype(BF16)


def _inproj_kernel(*refs, rope, mod_row0, tiles_per_seq):
    if rope:
        (x_ref, mod_ref, cos_ref, slo_ref, shi_ref, gmix_ref, win_ref, gq_ref, wuq_ref, gkv_ref, wukv_ref,
         bg_ref, q_ref, kc_ref, v_ref, mq_ref, mk_ref, mv_ref, gate_ref) = refs
    else:
        (x_ref, mod_ref, gmix_ref, win_ref, gq_ref, wuq_ref, gkv_ref, wukv_ref,
         bg_ref, q_ref, kc_ref, v_ref, mq_ref, mk_ref, mv_ref, gate_ref, ckv_ref, kpet_ref) = refs
    shift, scale = _mod_chunks(mod_ref, mod_row0, tiles_per_seq)
    for r0 in range(0, x_ref.shape[0], INPROJ_ROWS):
        g = slice(r0, r0 + INPROJ_ROWS)
        if rope:
            cos, slo, shi = cos_ref[g, :], slo_ref[g, :], shi_ref[g, :]
        hb = _mixer_input(x_ref[g, :], gmix_ref[...], shift, scale)

        def proj(seg):
            return _dot_nt(hb, win_ref[seg[0]:seg[1], :])

        qn = _rms(proj(SEG_Q), gq_ref[...]).astype(BF16)
        ckv = _rms(proj(SEG_KV), gkv_ref[...])
        zkpe = proj(SEG_KPE)
        gate_ref[g, :] = proj(SEG_GATE) + bg_ref[...]

        q = _dot(qn, wuq_ref[...])
        for hd in range(MLA_HEADS):
            sl = slice(hd * HEAD_BLOCK, (hd + 1) * HEAD_BLOCK)
            qh = q[:, sl]
            if rope:
                qh = _rope_block(qh, cos, slo, shi)
            q_ref[g, sl] = (qh * Q_PRESCALE).astype(BF16)

        if not rope:
            ckv_ref[g, :] = ckv
            kpe_t = zkpe.T
            seq = kpet_ref.shape[2]
            for sq in range(INPROJ_ROWS // seq):
                kpet_ref[r0 // seq + sq] = kpe_t[:ROPE_DIM, sq * seq:(sq + 1) * seq]
        kp = pltpu.roll(zkpe, ROPE_LANE0, 1)
        if rope:
            kp = _rope_block(kp, cos, slo, shi)
        kvn = _dot(ckv.astype(BF16), wukv_ref[...])
        _store_kv(kvn, kp, kc_ref, v_ref, g)

        mv_ref[g, :] = proj(SEG_MV).astype(BF16)
        mk_ref[g, :] = (proj(SEG_MK) * (MLSTM_DK ** -0.5)).astype(BF16)
        mq_ref[g, :] = proj(SEG_MQ).astype(BF16)


def _inproj(x, mod, rope_tabs, gmix, w_in_p, gq, w_uq_p, gkv, w_ukv_p, bg, *, seq_len):
    n_tok = x.shape[0]
    tm = TOKEN_TILE
    tiles_per_seq = seq_len // tm
    rope = rope_tabs is not None
    tok = lambda c: pl.BlockSpec((tm, c), lambda i: (i, 0))
    in_specs = [tok(D_MODEL), pl.BlockSpec(mod.shape, lambda i: (0, 0))]
    args = [x, mod]
    if rope:
        in_specs += [pl.BlockSpec((tm, LANES), lambda i: (i % tiles_per_seq, 0))] * 3
        args += list(rope_tabs)
    win_rows = SEG_MO[0]
    in_specs += [_const_spec(gmix.shape), _const_spec((win_rows, w_in_p.shape[1]))]
    in_specs += [_const_spec(a.shape) for a in (gq, w_uq_p, gkv, w_ukv_p, bg)]
    args += [gmix, w_in_p, gq, w_uq_p, gkv, w_ukv_p, bg]
    widths = [(MLA_HEADS * HEAD_BLOCK, BF16), (MLA_HEADS * HEAD_BLOCK, BF16), (V_WIDTH, BF16),
              (MLSTM_HEADS * MLSTM_DK, BF16), (MLSTM_HEADS * MLSTM_DK, BF16), (MLSTM_HEADS * MLSTM_DV, BF16),
              (LANES, F32)]
    out_specs = [tok(c) for c, _ in widths]
    out_shape = [jax.ShapeDtypeStruct((n_tok, c), dt) for c, dt in widths]
    if not rope:
        out_specs += [tok(KV_RANK), pl.BlockSpec((tm // seq_len, ROPE_DIM, seq_len), lambda i: (i, 0, 0))]
        out_shape += [jax.ShapeDtypeStruct((n_tok, KV_RANK), F32),
                      jax.ShapeDtypeStruct((n_tok // seq_len, ROPE_DIM, seq_len), F32)]
    return pl.pallas_call(
        functools.partial(_inproj_kernel, rope=rope, mod_row0=1 if rope else 0,
                          tiles_per_seq=tiles_per_seq if rope else None),
        grid=(n_tok // tm,),
        in_specs=in_specs,
        out_specs=out_specs,
        out_shape=out_shape,
        compiler_params=_params(("parallel",)),
        name="inproj_lat" if rope else "inproj_ctx",
    )(*args)


def _cachekv_kernel(ckv_ref, kp_ref, wukv_ref, kc_ref, v_ref):
    kvn = _dot(ckv_ref[...].astype(BF16), wukv_ref[...])
    _store_kv(kvn, kp_ref[...], kc_ref, v_ref)


def _cachekv(cache_ckv, krope_blk, w_ukv_p):
    b, past, _ = cache_ckv.shape
    return pl.pallas_call(
        _cachekv_kernel,
        grid=(b,),
        in_specs=[pl.BlockSpec((None, past, KV_RANK), lambda i: (i, 0, 0)),
                  pl.BlockSpec((None, past, LANES), lambda i: (i, 0, 0)),
                  _const_spec(w_ukv_p.shape)],
        out_specs=[pl.BlockSpec((None, past, MLA_HEADS * HEAD_BLOCK), lambda i: (i, 0, 0)),
                   pl.BlockSpec((None, past, V_WIDTH), lambda i: (i, 0, 0))],
        out_shape=[jax.ShapeDtypeStruct((b, past, MLA_HEADS * HEAD_BLOCK), BF16),
                   jax.ShapeDtypeStruct((b, past, V_WIDTH), BF16)],
        compiler_params=_params(("parallel",)),
        name="cachekv",
    )(cache_ckv, krope_blk, w_ukv_p)


def _attn_kernel(*refs, has_cache, n_cast, has_adaln):
    n_in = 5 if has_cache else 3
    if has_cache:
        q_ref, k_ref, v_ref, kc_ref, vc_ref = refs[:n_in]
    else:
        q_ref, k_ref, v_ref = refs[:n_in]
    n_all_in = n_in + n_cast + (4 if has_adaln else 0)
    o_ref = refs[n_all_in]
    for w_ref, wb_ref in zip(refs[n_in:n_in + n_cast], refs[n_all_in + 1:n_all_in + 1 + n_cast]):
        wb_ref[...] = w_ref[...].astype(BF16)
    if has_adaln:
        _adaln_block(*refs[n_in + n_cast:n_all_in], refs[n_all_in + 1 + n_cast])
    tq = q_ref.shape[1]
    lane = lax.broadcasted_iota(jnp.int32, (tq, LANES), 1)
    for sq in range(q_ref.shape[0]):
        hsl = [slice(hd * HEAD_BLOCK, (hd + 1) * HEAD_BLOCK) for hd in range(MLA_HEADS)]
        vsl = [slice(hd // 2 * V_SLAB, (hd // 2 + 1) * V_SLAB) for hd in range(MLA_HEADS)]
        s = [_dot_nt(q_ref[sq, :, sl], k_ref[sq, :, sl]) for sl in hsl]
        m = [jnp.max(si, axis=-1, keepdims=True) for si in s]
        if has_cache:
            sc = [_dot_nt(q_ref[sq, :, sl], kc_ref[sq, :, sl]) for sl in hsl]
            m = [jnp.maximum(mi, jnp.max(ci, axis=-1, keepdims=True)) for mi, ci in zip(m, sc)]
        p = [jnp.exp2(si - mi).astype(BF16) for si, mi in zip(s, m)]
        o = [_dot(pi, v_ref[sq, :, sl]) for pi, sl in zip(p, vsl)]
        if has_cache:
            pc = [jnp.exp2(ci - mi).astype(BF16) for ci, mi in zip(sc, m)]
            o = [oi + _dot(pi, vc_ref[sq, :, sl]) for oi, pi, sl in zip(o, pc, vsl)]
        outs = [oi[:, :LANES] * (1.0 / oi[:, LANES:]) for oi in o]
        for pair in range(MLA_HEADS // 2):
            o_ref[sq, :, pair * LANES:(pair + 1) * LANES] = jnp.where(
                lane < MLA_V_DIM, outs[2 * pair], outs[2 * pair + 1]).astype(BF16)


def _attention(q, k, v, cache=None, cast_weights=(), adaln_tail=None, seqs_per_step=1):
    nb, s, _ = q.shape
    tq = Q_TILE
    ns = seqs_per_step
    b = nb // ns
    n_steps = b * (s // tq)
    kw, vw, ow = MLA_HEADS * HEAD_BLOCK, V_WIDTH, MLA_HEADS * MLA_V_DIM
    in_specs = [pl.BlockSpec((ns, tq, kw), lambda i, j: (i, j, 0)),
                pl.BlockSpec((ns, s, kw), lambda i, j: (i, 0, 0)),
                pl.BlockSpec((ns, s, vw), lambda i, j: (i, 0, 0))]
    args = [q, k, v]
    if cache is not None:
        past = cache[0].shape[1]
        in_specs += [pl.BlockSpec((ns, past, kw), lambda i, j: (i, 0, 0)),
                     pl.BlockSpec((ns, past, vw), lambda i, j: (i, 0, 0))]
        args += list(cache)
    step = lambda i, j: i * (s // tq) + j
    w_specs = [pl.BlockSpec((w.shape[0] // n_steps, w.shape[1]), lambda i, j: (step(i, j), 0)) for w in cast_weights]
    extra_in, extra_out, extra_shape = [], [], []
    if adaln_tail is not None:
        tn = N_MOD_TAIL * D_MODEL // n_steps
        col0 = N_MOD_MIXER * D_MODEL // tn
        extra_in = [pl.BlockSpec(adaln_tail[0].shape, lambda i, j: (0, 0)),
                    pl.BlockSpec(adaln_tail[1].shape, lambda i, j: (0, 0)),
                    pl.BlockSpec((D_MODEL, tn), lambda i, j: (0, col0 + step(i, j))),
                    pl.BlockSpec((1, tn), lambda i, j: (0, col0 + step(i, j)))]
        extra_out = [pl.BlockSpec((COND_ROWS, tn), lambda i, j: (0, step(i, j)))]
        extra_shape = [jax.ShapeDtypeStruct((COND_ROWS, N_MOD_TAIL * D_MODEL), F32)]
    outs = pl.pallas_call(
        functools.partial(_attn_kernel, has_cache=cache is not None, n_cast=len(cast_weights),
                          has_adaln=adaln_tail is not None),
        grid=(b, s // tq),
        in_specs=in_specs + w_specs + extra_in,
        out_specs=[pl.BlockSpec((ns, tq, ow), lambda i, j: (i, j, 0))] + w_specs + extra_out,
        out_shape=[jax.ShapeDtypeStruct((nb, s, ow), BF16)]
                  + [jax.ShapeDtypeStruct(w.shape, BF16) for w in cast_weights] + extra_shape,
        compiler_params=_params(("parallel", "arbitrary")),
        name="attn_lat" if cache is not None else "attn_ctx",
    )(*args, *cast_weights, *(adaln_tail or ()))
    return outs[0] if len(outs) == 1 else outs


def _lane_bcast(x, k, width=None):
    y = jnp.broadcast_to(x[:, k:k + 1], x.shape)
    reps = (width or LANES) // LANES
    return y if reps == 1 else jnp.concatenate([y] * reps, axis=1)


def _prefix_sum_rows(x):
    n = x.shape[0]
    r = lax.broadcasted_iota(jnp.int32, (n, n), 0)
    c = lax.broadcasted_iota(jnp.int32, (n, n), 1)
    tri = jnp.where(c <= r, 1.0, 0.0).astype(BF16)
    hi = x.astype(BF16)
    r1 = x - hi.astype(F32)
    mid = r1.astype(BF16)
    lo = (r1 - mid.astype(F32)).astype(BF16)
    return _dot(tri, hi) + _dot(tri, mid) + _dot(tri, lo)


def _gate_prep(g):
    n = g.shape[0]
    lane = lax.broadcasted_iota(jnp.int32, (n, LANES), 1)
    fpre = pltpu.roll(g, LANES - N_DIRHEAD, 1)
    lf = jnp.minimum(fpre, 0.0) - jnp.log(1.0 + jnp.exp(-jnp.abs(fpre)))
    binc = _prefix_sum_rows(lf)
    tot = binc[n - 1:n, :]
    b = jnp.where(lane < MLSTM_HEADS, binc, tot - binc + lf)
    return b, g - b, tot


def _chunk_setup(b, u, tot, m_prev):
    g_last = jnp.maximum(jnp.max(u, axis=0, keepdims=True), m_prev)
    u2 = u * LOG2E
    w = jnp.exp2(u2 - g_last * LOG2E)
    return u2, (b * LOG2E).T, w.T, tot + g_last, jnp.exp(m_prev - g_last)


def _state_update(vt, k, wt, kk):
    c_new = _dot((vt * wt[kk:kk + 1, :]).astype(BF16), k)
    n_new = _dot(wt[:N_DIRHEAD, :].astype(BF16), k)[kk:kk + 1, :]
    return c_new, n_new


def _mlstm_dir_t(s0t, vt, u_b, bt_row, m_row, *, fwd, inter=None):
    sdt, den, g = _dir_weights(s0t, u_b, m_row, fwd=fwd)
    return _dir_finish(_dot(vt, sdt), den, g, bt_row, m_row, inter)


def _dir_weights(s0t, u_b, m_row, *, fwd):
    n = s0t.shape[0]
    r = lax.broadcasted_iota(jnp.int32, (n, n), 0)
    c = lax.broadcasted_iota(jnp.int32, (n, n), 1)
    a = jnp.where((r <= c) if fwd else (r >= c), u_b, -jnp.inf)
    g = jnp.maximum(jnp.max(a, axis=0, keepdims=True), m_row)
    sdt = s0t * jnp.exp2(a - g)
    return sdt.astype(BF16), jnp.sum(sdt, axis=0, keepdims=True), g


def _dir_finish(ht, den, g, bt_row, m_row, inter=None):
    if inter is not None:
        w_inter = jnp.exp2(m_row - g)
        ht = ht + w_inter * inter[0]
        den = den + w_inter * inter[1]
    floor = jnp.exp2(-(bt_row + g))
    return ht * (1.0 / jnp.maximum(jnp.abs(den), floor))


def _mlstm_ctx_kernel(q_ref, k_ref, v_ref, g_ref, ht_ref, c_ref, n_ref, m_ref):
    n = q_ref.shape[1]
    zero_lanes = jnp.zeros((1, LANES), F32)
    zero_row = jnp.zeros((1, n), F32)
    dirs = ((True, 0), (False, MLSTM_HEADS))
    setup = [_chunk_setup(*_gate_prep(g_ref[sq]), zero_lanes) for sq in range(q_ref.shape[0])]
    units = [(sq, hd) for sq in range(q_ref.shape[0]) for hd in range(MLSTM_HEADS)]
    ks = [k_ref[sq, :, hd * MLSTM_DK:(hd + 1) * MLSTM_DK] for sq, hd in units]
    s0ts = [_dot_nt(k, q_ref[sq, :, hd * MLSTM_DK:(hd + 1) * MLSTM_DK]) for k, (sq, hd) in zip(ks, units)]
    vts = [v_ref[sq, :, hd * MLSTM_DV:(hd + 1) * MLSTM_DV].T for sq, hd in units]
    wts = [[_dir_weights(s0t, _lane_bcast(setup[sq][0], off + hd, n), zero_row, fwd=fwd) for fwd, off in dirs]
           for s0t, (sq, hd) in zip(s0ts, units)]
    hts = [[_dot(vt, sdt) for sdt, _, _ in wt] for vt, wt in zip(vts, wts)]
    for (sq, hd), k, vt, wt, ht2 in zip(units, ks, vts, wts, hts):
        _, bt2, w_t, _, _ = setup[sq]
        ht_ref[sq, hd * MLSTM_DV:(hd + 1) * MLSTM_DV, :] = sum(
            _dir_finish(ht_d, den, g, bt2[off + hd:off + hd + 1, :], zero_row)
            for ht_d, (_, den, g), (_, off) in zip(ht2, wt, dirs)).astype(ht_ref.dtype)
        for _, off in dirs:
            c_ref[sq, off + hd], n_ref[sq, off + hd:off + hd + 1, :] = _state_update(vt, k, w_t, off + hd)
    for sq in range(q_ref.shape[0]):
        m_ref[sq] = setup[sq][3]


def _mlstm_ctx(mq, mk, mv, gates):
    b, s, _ = mq.shape
    ns = CTX_SEQS_PER_STEP
    seq = lambda c: pl.BlockSpec((ns, s, c), lambda i: (i, 0, 0))
    return pl.pallas_call(
        _mlstm_ctx_kernel,
        grid=(b // ns,),
        in_specs=[seq(MLSTM_HEADS * MLSTM_DK), seq(MLSTM_HEADS * MLSTM_DK), seq(MLSTM_HEADS * MLSTM_DV), seq(LANES)],
        out_specs=[pl.BlockSpec((ns, MLSTM_HEADS * MLSTM_DV, s), lambda i: (i, 0, 0)),
                   pl.BlockSpec((ns, N_DIRHEAD, MLSTM_DV, MLSTM_DK), lambda i: (i, 0, 0, 0)),
                   pl.BlockSpec((ns, N_DIRHEAD, MLSTM_DK), lambda i: (i, 0, 0)),
                   pl.BlockSpec((ns, 1, LANES), lambda i: (i, 0, 0))],
        out_shape=[jax.ShapeDtypeStruct((b, MLSTM_HEADS * MLSTM_DV, s), BF16),
                   jax.ShapeDtypeStruct((b, N_DIRHEAD, MLSTM_DV, MLSTM_DK), F32),
                   jax.ShapeDtypeStruct((b, N_DIRHEAD, MLSTM_DK), F32),
                   jax.ShapeDtypeStruct((b, 1, LANES), F32)],
        compiler_params=_params(("parallel",)),
        name="mlstm_ctx",
    )(mq, mk, mv, gates)


def _mlstm_lat_kernel(qf_ref, kf_ref, vf_ref, gf_ref, qb_ref, kb_ref, vb_ref, gb_ref, c0_ref, n0_ref, m0_ref,
                      ht_ref, c_ref, n_ref, m_ref, acc_ref, *, n_chunks):
    step = pl.program_id(1)

    @pl.when(step == 0)
    def _():
        c_ref[...] = c0_ref[...]
        n_ref[...] = n0_ref[...]
        m_ref[...] = m0_ref[...]
        acc_ref[...] = jnp.zeros_like(acc_ref)

    n = qf_ref.shape[0]
    gate_f, gate_b = _gate_prep(gf_ref[...]), _gate_prep(gb_ref[...])
    is_f = lax.broadcasted_iota(jnp.int32, (n, LANES), 1) < MLSTM_HEADS
    b, u, tot = (jnp.where(is_f[:x.shape[0]], x, y) for x, y in zip(gate_f, gate_b))
    m_prev = m_ref[...]
    u2, bt2, w_t, m_new, decay = _chunk_setup(b, u, tot, m_prev)
    m2_prev = m_prev * LOG2E
    units = [(fwd, hd, hd if fwd else MLSTM_HEADS + hd) for fwd in (True, False) for hd in range(MLSTM_HEADS)]
    refs = {True: (qf_ref, kf_ref, vf_ref), False: (qb_ref, kb_ref, vb_ref)}
    qs = [refs[fwd][0][:, hd * MLSTM_DK:(hd + 1) * MLSTM_DK] for fwd, hd, _ in units]
    ks = [refs[fwd][1][:, hd * MLSTM_DK:(hd + 1) * MLSTM_DK] for fwd, hd, _ in units]
    vts = [refs[fwd][2][:, hd * MLSTM_DV:(hd + 1) * MLSTM_DV].T for fwd, hd, _ in units]
    s0ts = [_dot_nt(k, q) for k, q in zip(ks, qs)]
    c_prevs = [c_ref[kk] for _, _, kk in units]
    n_prevs = [n_ref[kk:kk + 1, :] for _, _, kk in units]
    inters = [(_dot_nt(c.astype(BF16), q), _dot_nt(jnp.broadcast_to(nv, (8, MLSTM_DK)).astype(BF16), q)[0:1, :])
              for c, nv, q in zip(c_prevs, n_prevs, qs)]
    m_rows = [_lane_bcast(m2_prev, kk, n) for _, _, kk in units]
    wts = [_dir_weights(s0t, _lane_bcast(u2, kk, n), m_row, fwd=fwd)
           for s0t, m_row, (fwd, _, kk) in zip(s0ts, m_rows, units)]
    hts = [_dot(vt, sdt) for vt, (sdt, _, _) in zip(vts, wts)]
    for (fwd, hd, kk), k, vt, (_, den, g), ht, inter, m_row, c_prev, n_prev in zip(
            units, ks, vts, wts, hts, inters, m_rows, c_prevs, n_prevs):
        chunk = step if fwd else n_chunks - 1 - step
        acc_ref[chunk, hd * MLSTM_DV:(hd + 1) * MLSTM_DV, :] += _dir_finish(ht, den, g, bt2[kk:kk + 1, :], m_row, inter)
        c_new, n_new = _state_update(vt, k, w_t, kk)
        dk = _lane_bcast(decay, kk)
        c_ref[kk] = dk * c_prev + c_new
        n_ref[kk:kk + 1, :] = dk * n_prev + n_new
    m_ref[...] = m_new

    @pl.when(step == n_chunks - 1)
    def _():
        ht_ref[...] = acc_ref[...].astype(ht_ref.dtype)


def _mlstm_lat(mq, mk, mv, gates, c0, n0, m0):
    b, s, _ = mq.shape
    lc = MLSTM_CHUNK
    nc = s // lc
    fw = lambda c: pl.BlockSpec((None, lc, c), lambda i, j: (i, j, 0))
    bw = lambda c: pl.BlockSpec((None, lc, c), lambda i, j: (i, nc - 1 - j, 0))
    dk, dv = MLSTM_HEADS * MLSTM_DK, MLSTM_HEADS * MLSTM_DV
    return pl.pallas_call(
        functools.partial(_mlstm_lat_kernel, n_chunks=nc),
        grid=(b, nc),
        in_specs=[fw(dk), fw(dk), fw(dv), fw(LANES), bw(dk), bw(dk), bw(dv), bw(LANES),
                  pl.BlockSpec((None, N_DIRHEAD, MLSTM_DV, MLSTM_DK), lambda i, j: (i, 0, 0, 0)),
                  pl.BlockSpec((None, N_DIRHEAD, MLSTM_DK), lambda i, j: (i, 0, 0)),
                  pl.BlockSpec((None, 1, LANES), lambda i, j: (i, 0, 0))],
        out_specs=pl.BlockSpec((nc, dv, lc), lambda i, j: (i, 0, 0)),
        out_shape=jax.ShapeDtypeStruct((b * nc, dv, lc), BF16),
        scratch_shapes=[pltpu.VMEM((N_DIRHEAD, MLSTM_DV, MLSTM_DK), F32),
                        pltpu.VMEM((N_DIRHEAD, MLSTM_DK), F32),
                        pltpu.VMEM((1, LANES), F32),
                        pltpu.VMEM((nc, dv, lc), F32)],
        compiler_params=_params(("parallel", "arbitrary")),
        name="mlstm_lat",
    )(mq, mk, mv, gates, mq, mk, mv, gates, c0, n0, m0)


def _tail_kernel(x_ref, modin_ref, mod_ref, att_ref, hst_ref, gmix_ref, wgate_ref, gml_ref, womla_ref, womlstm_ref,
                 wout_ref, gffn_ref, wfin_ref, wfout_ref, gfin_ref, y_ref, *, mod_row0, tiles_per_seq):
    shift_mix, scale_mix = _mod_chunks(modin_ref, mod_row0, tiles_per_seq)
    gate_mix, shift_ffn, scale_ffn, gate_ffn = _mod_chunks(mod_ref, mod_row0, tiles_per_seq)
    gml = gml_ref[...]
    tm = x_ref.shape[0]
    groups = [slice(r, r + TAIL_ROWS) for r in range(0, tm, TAIL_ROWS)]
    dv = MLSTM_HEADS * MLSTM_DV
    hb = [_mixer_input(x_ref[g, :], gmix_ref[...], shift_mix, scale_mix) for g in groups]
    smo = [_sigmoid(_dot_nt(h, wgate_ref[:dv, :])) for h in hb]
    g_a = [_sigmoid(_dot_nt(h, wgate_ref[dv:dv + D_MODEL, :])) for h in hb]
    g_b = [_sigmoid(_dot_nt(h, wgate_ref[dv + D_MODEL:, :])) for h in hb]
    a = [_dot(att_ref[g, :], womla_ref[...]) for g in groups]
    hm = []
    for gi in range(len(groups)):
        hs = hst_ref[gi].T.astype(F32)
        parts = []
        for hd in range(MLSTM_HEADS):
            sl = slice(hd * MLSTM_DV, (hd + 1) * MLSTM_DV)
            parts.append((_rms(hs[:, sl], gml[:, sl]) * smo[gi][:, sl]).astype(BF16))
        hm.append(jnp.concatenate(parts, axis=1))
    bm = [_dot(h, womlstm_ref[...]) for h in hm]
    merged = [(ga_i * ai + gb_i * bi).astype(BF16) for ga_i, gb_i, ai, bi in zip(g_a, g_b, a, bm)]
    x1 = [x_ref[g, :] + gate_mix * _dot(mi, wout_ref[...]) for g, mi in zip(groups, merged)]
    h2 = [(_rms(xi, gffn_ref[...]) * (1.0 + scale_ffn) + shift_ffn).astype(BF16) for xi in x1]
    ga = [_dot(hi, wfin_ref[:, :FFN_HIDDEN]) for hi in h2]
    gu = [_dot(hi, wfin_ref[:, FFN_HIDDEN:]) for hi in h2]
    act = [(gi * _sigmoid(gi) * ui).astype(BF16) for gi, ui in zip(ga, gu)]
    for g, xi, ci in zip(groups, x1, act):
        y_ref[g, :] = _rms(xi + gate_ffn * _dot(ci, wfout_ref[...]), gfin_ref[...])


def _tail(x, mod_in, mod, att, hst, gmix, w_in_p, gml, w_o_mla, w_o_mlstm, w_out, gffn, w_ffn_in, w_ffn_out, gfin, *,
          seq_len, latent):
    n_tok = x.shape[0]
    tm = TOKEN_TILE
    assert hst.shape[0] * TAIL_ROWS == n_tok and hst.shape[2] == TAIL_ROWS
    assert SEG_MO[1] == SEG_BR[0]
    tok = lambda c: pl.BlockSpec((tm, c), lambda i: (i, 0))
    whole = lambda a: pl.BlockSpec(a.shape, lambda i: (0, 0))
    gate_rows = pl.BlockSpec((pl.Element(SEG_BR[1] - SEG_MO[0]), pl.Element(w_in_p.shape[1])),
                             lambda i: (SEG_MO[0], 0), pipeline_mode=pl.Buffered(1))
    consts = (gml, w_o_mla, w_o_mlstm, w_out, gffn, w_ffn_in, w_ffn_out, gfin)
    return pl.pallas_call(
        functools.partial(_tail_kernel, mod_row0=1 if latent else 0, tiles_per_seq=seq_len // tm if latent else None),
        grid=(n_tok // tm,),
        in_specs=[tok(D_MODEL), whole(mod_in), whole(mod), tok(att.shape[1]),
                  pl.BlockSpec((tm // TAIL_ROWS,) + hst.shape[1:], lambda i: (i, 0, 0)),
                  _const_spec(gmix.shape), gate_rows]
                 + [_const_spec(a.shape) for a in consts],
        out_specs=tok(D_MODEL),
        out_shape=jax.ShapeDtypeStruct((n_tok, D_MODEL), F32),
        compiler_params=_params(("parallel",)),
        name="tail_lat" if latent else "tail_ctx",
    )(x, mod_in, mod, att, hst, gmix, w_in_p, *consts)


_IN_SIZES = (Q_RANK, KV_RANK, ROPE_DIM, MLSTM_HEADS * MLSTM_DK, MLSTM_HEADS * MLSTM_DK, MLSTM_HEADS * MLSTM_DV,
             4 * MLSTM_HEADS, MLSTM_HEADS * MLSTM_DV, 2 * D_MODEL)
_IN_OFF = tuple(sum(_IN_SIZES[:i]) for i in range(len(_IN_SIZES) + 1))
PACK_ROWS = LANES
_PAD_AFTER = tuple((_SEG_OFF[i + 1] // PACK_ROWS, _SEG_WIDTHS[i] - _IN_SIZES[i], _IN_SIZES[i] % PACK_ROWS)
                   for i in range(len(_IN_SIZES)) if _SEG_WIDTHS[i] != _IN_SIZES[i])


PACK_PIECES = 8
N_PACK_STEPS = pl.cdiv(N_IN_PACKED // PACK_ROWS, PACK_PIECES)
_N_IN = _IN_OFF[-1]


def _pack_src_row(p):
    src = p * PACK_ROWS
    for first_piece_after, pad, _ in _PAD_AFTER:
        src = src - jnp.where(p >= first_piece_after, pad, 0)
    src = jnp.minimum(src, _N_IN - PACK_ROWS)
    return pl.multiple_of(src, math.gcd(PACK_ROWS, _N_IN, *(pad for _, pad, _ in _PAD_AFTER)))


def _pack_in_kernel(*refs):
    w_refs, o_ref = refs[:PACK_PIECES], refs[PACK_PIECES]
    row = lax.broadcasted_iota(jnp.int32, w_refs[0].shape, 0)
    for j, w_ref in enumerate(w_refs):
        p = pl.program_id(0) * PACK_PIECES + j
        valid = jnp.where(p < N_IN_PACKED // PACK_ROWS, PACK_ROWS, 0)
        for first_piece_after, _, width in _PAD_AFTER:
            valid = jnp.where(p == first_piece_after - 1, width, valid)
        o_ref[j * PACK_ROWS:(j + 1) * PACK_ROWS, :] = jnp.where(row < valid, w_ref[...], 0.0).astype(BF16)


def _pack_in(w_in_t):
    n, k = w_in_t.shape
    piece = lambda j: pl.BlockSpec((pl.Element(PACK_ROWS), pl.Element(k)),
                                   lambda i: (_pack_src_row(i * PACK_PIECES + j), 0))
    return pl.pallas_call(
        _pack_in_kernel,
        grid=(N_PACK_STEPS,),
        in_specs=[piece(j) for j in range(PACK_PIECES)],
        out_specs=pl.BlockSpec((PACK_PIECES * PACK_ROWS, k), lambda i: (i, 0)),
        out_shape=jax.ShapeDtypeStruct((N_PACK_STEPS * PACK_PIECES * PACK_ROWS, k), BF16),
        compiler_params=_params(("parallel",)),
        name="pack_in",
    )(*([w_in_t] * PACK_PIECES))


def _pack_small(w_uq, w_ukv, b_gates):
    qd = NOPE_DIM + ROPE_DIM
    w_uq_p = jnp.pad(w_uq.reshape(Q_RANK, MLA_HEADS, qd), ((0, 0), (0, 0), (0, HEAD_BLOCK - qd)))
    w_uq_p = w_uq_p.reshape(Q_RANK, MLA_HEADS * HEAD_BLOCK).astype(BF16)
    kv = w_ukv.reshape(KV_RANK, MLA_HEADS, NOPE_DIM + MLA_V_DIM)
    wk = jnp.pad(kv[:, :, :NOPE_DIM], ((0, 0), (0, 0), (0, HEAD_BLOCK - NOPE_DIM))).reshape(KV_RANK, -1)
    wv = kv[:, :, NOPE_DIM:].reshape(KV_RANK, -1)
    w_ukv_p = jnp.concatenate([wk, wv], axis=1).astype(BF16)
    bg = jnp.pad(b_gates, ((0, 0), (0, LANES - b_gates.shape[1])))
    return w_uq_p, w_ukv_p, bg


def _rope_tables(n_tokens):
    pos = np.arange(n_tokens)
    row = (pos // GRID_W).astype(np.float64)[:, None]
    col = (pos % GRID_W).astype(np.float64)[:, None]
    half = ROPE_DIM // 2
    inv = (np.float32(ROPE_BASE) ** (-np.arange(0, half, 2, dtype=np.float32) / np.float32(half))).astype(np.float64)
    r = np.arange(LANES) - ROPE_LANE0
    in_rope = (r >= 0) & (r < ROPE_DIM)
    rr = np.clip(r, 0, ROPE_DIM - 1)
    freq = inv[rr % (half // 2)][None, :]
    ang = np.where((rr // half == 0)[None, :], row * freq, col * freq).astype(np.float32).astype(np.float64)
    first = (rr % half) < (half // 2)
    cos = np.where(in_rope[None, :], np.cos(ang), 1.0)
    sin = np.sin(ang)
    sin_lo = np.where((in_rope & first)[None, :], -sin, 0.0)
    sin_hi = np.where((in_rope & ~first)[None, :], sin, 0.0)
    return tuple(jnp.asarray(t, dtype=F32) for t in (cos, sin_lo, sin_hi))


def kernel(x_prompt, x_sample, cache_ckv, cache_krope, state_C, state_n, state_m, c, c_ctx, w_mod, b_mod, g_norm_mix,
           w_in, b_gates, g_q_norm, w_uq, g_kv_norm, w_ukv, g_mlstm_norm, w_o_mla, w_o_mlstm, w_out, g_norm_ffn,
           w_ffn_in, w_ffn_out, g_final):
    bp, sp, d = x_prompt.shape
    bs, ss, _ = x_sample.shape
    layer = 0
    assert w_mod.shape[0] == 1 and sp == MLSTM_CHUNK and ss % MLSTM_CHUNK == 0

    adaln_args = (c_ctx[None, :], c, w_mod[layer], b_mod[layer][None, :])
    mod_in = _adaln(*adaln_args)

    w_in_p = _pack_in(w_in[layer].T)
    w_uq_p, w_ukv_p, bg = _pack_small(w_uq[layer], w_ukv[layer], b_gates[layer][None, :])
    row = lambda g: g[layer][None, :]
    shared_in = (row(g_norm_mix), w_in_p, row(g_q_norm), w_uq_p, row(g_kv_norm), w_ukv_p, bg)
    seq3 = lambda a, b_, s_: a.reshape(b_, s_, a.shape[-1])

    xl = x_sample.reshape(bs * ss, d)
    q, kc, v, mq_l, mk_l, mv_l, gates_l = _inproj(xl, mod_in, _rope_tables(ss), *shared_in, seq_len=ss)
    krope_blk = jnp.pad(cache_krope[:, layer], ((0, 0), (0, 0), (ROPE_LANE0, LANES - ROPE_LANE0 - ROPE_DIM)))
    cache = _cachekv(cache_ckv[:, layer], krope_blk, w_ukv_p)
    att_l, womla_b, womlstm_b, wout_b, wfin_b, wfout_b, mod_tail = _attention(
        seq3(q, bs, ss), seq3(kc, bs, ss), seq3(v, bs, ss), cache,
        cast_weights=(w_o_mla[layer], w_o_mlstm[layer], w_out[layer], w_ffn_in[layer], w_ffn_out[layer]),
        adaln_tail=adaln_args)
    tail_w = (row(g_norm_mix), w_in_p, row(g_mlstm_norm), womla_b, womlstm_b, wout_b, row(g_norm_ffn), wfin_b, wfout_b,
              g_final[None, :])

    xc = x_prompt.reshape(bp * sp, d)
    q, kc, v, mq, mk, mv, gates, ckv, kpe = _inproj(xc, mod_in, None, *shared_in, seq_len=sp)
    att = _attention(seq3(q, bp, sp), seq3(kc, bp, sp), seq3(v, bp, sp), seqs_per_step=CTX_SEQS_PER_STEP)
    hs, new_c, new_n, new_m = _mlstm_ctx(seq3(mq, bp, sp), seq3(mk, bp, sp), seq3(mv, bp, sp), seq3(gates, bp, sp))
    y_prompt = _tail(xc, mod_in, mod_tail, att.reshape(bp * sp, -1), hs, *tail_w,
                     seq_len=sp, latent=False).reshape(bp, sp, d)
    new_ckv = ckv.reshape(bp, 1, sp, KV_RANK)
    new_krope = jnp.swapaxes(kpe, 1, 2).reshape(bp, 1, sp, ROPE_DIM)
    new_C = new_c.reshape(bp, 1, 2, MLSTM_HEADS, MLSTM_DV, MLSTM_DK)
    new_N = new_n.reshape(bp, 1, 2, MLSTM_HEADS, MLSTM_DK)
    new_M = new_m[:, 0, :N_DIRHEAD].reshape(bp, 1, 2, MLSTM_HEADS)

    c0 = state_C[:, layer].reshape(bs, N_DIRHEAD, MLSTM_DV, MLSTM_DK)
    n0 = state_n[:, layer].reshape(bs, N_DIRHEAD, MLSTM_DK)
    m0 = jnp.pad(state_m[:, layer].reshape(bs, 1, N_DIRHEAD), ((0, 0), (0, 0), (0, LANES - N_DIRHEAD)))
    hs = _mlstm_lat(seq3(mq_l, bs, ss), seq3(mk_l, bs, ss), seq3(mv_l, bs, ss), seq3(gates_l, bs, ss), c0, n0, m0)
    y_sample = _tail(xl, mod_in, mod_tail, att_l.reshape(bs * ss, -1), hs, *tail_w,
                     seq_len=ss, latent=True).reshape(bs, ss, d)
    return (y_prompt, y_sample, new_ckv, new_krope, new_C, new_N, new_M)
```

```python
import functools
import math

import jax
import jax.numpy as jnp
import numpy as np
from jax import lax
from jax.experimental import pallas as pl
from jax.experimental.pallas import tpu as pltpu

F32 = jnp.float32
BF16 = jnp.bfloat16

D_MODEL = 1024
GRID_W = 64
MLA_HEADS = 8
Q_RANK = 384
KV_RANK = 256
NOPE_DIM = 64
ROPE_DIM = 32
MLA_V_DIM = 64
ROPE_BASE = 10000.0
MLA_SCALE = (NOPE_DIM + ROPE_DIM) ** -0.5
MLSTM_HEADS = 4
MLSTM_DK = 128
MLSTM_DV = 256
FFN_HIDDEN = ((8 * D_MODEL // 3 + 255) // 256) * 256
EPS = 1e-6

LANES = 128
HEAD_BLOCK = LANES
ROPE_LANE0 = NOPE_DIM
N_DIRHEAD = 2 * MLSTM_HEADS
MLSTM_CHUNK = 256
LOG2E = math.log2(math.e)
Q_PRESCALE = MLA_SCALE * LOG2E
V_SLAB = 2 * LANES
V_WIDTH = (MLA_HEADS // 2) * V_SLAB
TOKEN_TILE = 512
TAIL_ROWS = 256
INPROJ_ROWS = TOKEN_TILE
Q_TILE = 256
CTX_SEQS_PER_STEP = 4
VMEM_LIMIT = 56 * 1024 * 1024

_SEG_WIDTHS = (Q_RANK, KV_RANK, LANES, MLSTM_HEADS * MLSTM_DK, MLSTM_HEADS * MLSTM_DK,
               MLSTM_HEADS * MLSTM_DV, LANES, MLSTM_HEADS * MLSTM_DV, 2 * D_MODEL)
_SEG_OFF = tuple(sum(_SEG_WIDTHS[:i]) for i in range(len(_SEG_WIDTHS) + 1))
SEG_Q, SEG_KV, SEG_KPE, SEG_MQ, SEG_MK, SEG_MV, SEG_GATE, SEG_MO, SEG_BR = (
    (_SEG_OFF[i], _SEG_OFF[i + 1]) for i in range(9))
N_IN_PACKED = _SEG_OFF[-1]


def _dot(a, b):
    return jnp.dot(a, b, preferred_element_type=F32)


def _dot_nt(a, b):
    return lax.dot_general(a, b, (((1,), (1,)), ((), ())), preferred_element_type=F32)


def _dot_tn(a, b):
    return lax.dot_general(a, b, (((0,), (0,)), ((), ())), preferred_element_type=F32)


def _rms(x, g):
    ms = jnp.mean(x * x, axis=-1, keepdims=True)
    return x * lax.rsqrt(ms + EPS) * g


def _sigmoid(x):
    return 1.0 / (1.0 + jnp.exp(-x))


def _const_spec(shape):
    nd = len(shape)
    return pl.BlockSpec(shape, lambda *_: (0,) * nd, pipeline_mode=pl.Buffered(1))


def _mod_chunks(mod_ref, row0, tiles_per_seq):
    row = row0 if tiles_per_seq is None else row0 + pl.program_id(0) // tiles_per_seq
    m = mod_ref[pl.ds(row, 1), :]
    return [m[:, k * D_MODEL:(k + 1) * D_MODEL] for k in range(m.shape[1] // D_MODEL)]


def _params(sem, flags=None, vmem_mib=None):
    limit = VMEM_LIMIT if vmem_mib is None else vmem_mib * 1024 * 1024
    return pltpu.CompilerParams(dimension_semantics=sem, vmem_limit_bytes=limit, flags=flags)


N_MOD_MIXER = 2
N_MOD_TAIL = 4


COND_ROWS = 8


def _adaln_block(cctx_ref, c_ref, w_ref, b_ref, o_ref):
    pad = jnp.zeros((COND_ROWS - 1 - c_ref.shape[0], D_MODEL), F32)
    c = jnp.concatenate([cctx_ref[...], c_ref[...], pad], axis=0)
    s = c * _sigmoid(c)
    o_ref[...] = _dot(s.astype(BF16), w_ref[...].astype(BF16)) + b_ref[...]


def _adaln(c_ctx, c, w_mod, b_mod):
    tn = D_MODEL
    return pl.pallas_call(
        _adaln_block,
        grid=(N_MOD_MIXER,),
        in_specs=[pl.BlockSpec(c_ctx.shape, lambda j: (0, 0)),
                  pl.BlockSpec(c.shape, lambda j: (0, 0)),
                  pl.BlockSpec((D_MODEL, tn), lambda j: (0, j)),
                  pl.BlockSpec((1, tn), lambda j: (0, j))],
        out_specs=pl.BlockSpec((COND_ROWS, tn), lambda j: (0, j)),
        out_shape=jax.ShapeDtypeStruct((COND_ROWS, N_MOD_MIXER * D_MODEL), F32),
        compiler_params=_params(("arbitrary",)),
        name="adaln",
    )(c_ctx, c, w_mod, b_mod)


def _rope_block(x, cos, sin_lo, sin_hi):
    return x * cos + pltpu.roll(x, LANES - 8, 1) * sin_lo + pltpu.roll(x, 8, 1) * sin_hi


def _store_kv(kvn, kp, kc_ref, v_ref, g=slice(None)):
    for hd in range(MLA_HEADS):
        sl = slice(hd * HEAD_BLOCK, (hd + 1) * HEAD_BLOCK)
        kc_ref[g, sl] = (kvn[:, sl] + kp).astype(BF16)
    v0 = MLA_HEADS * HEAD_BLOCK
    ones = jnp.ones((kvn.shape[0], LANES), BF16)
    for pair in range(MLA_HEADS // 2):
        v_ref[g, pair * V_SLAB:pair * V_SLAB + LANES] = kvn[:, v0 + pair * LANES:v0 + (pair + 1) * LANES].astype(BF16)
        v_ref[g, pair * V_SLAB + LANES:(pair + 1) * V_SLAB] = ones


def _mixer_input(x, g, shift, scale):
    return (_rms(x, g) * (1.0 + scale) + shift).astype(BF16)


def _inproj_kernel(*refs, rope, mod_row0, tiles_per_seq):
    if rope:
        (x_ref, mod_ref, cos_ref, slo_ref, shi_ref, gmix_ref, win_ref, gq_ref, wuq_ref, gkv_ref, wukv_ref,
         bg_ref, q_ref, kc_ref, v_ref, mq_ref, mk_ref, mv_ref, gate_ref) = refs
    else:
        (x_ref, mod_ref, gmix_ref, win_ref, gq_ref, wuq_ref, gkv_ref, wukv_ref,
         bg_ref, q_ref, kc_ref, v_ref, mq_ref, mk_ref, mv_ref, gate_ref, ckv_ref, kpet_ref) = refs
    shift, scale = _mod_chunks(mod_ref, mod_row0, tiles_per_seq)
    for r0 in range(0, x_ref.shape[0], INPROJ_ROWS):
        g = slice(r0, r0 + INPROJ_ROWS)
        if rope:
            cos, slo, shi = cos_ref[g, :], slo_ref[g, :], shi_ref[g, :]
        hb = _mixer_input(x_ref[g, :], gmix_ref[...], shift, scale)

        def proj(seg):
            return _dot_nt(hb, win_ref[seg[0]:seg[1], :])

        qn = _rms(proj(SEG_Q), gq_ref[...]).astype(BF16)
        ckv = _rms(proj(SEG_KV), gkv_ref[...])
        zkpe = proj(SEG_KPE)
        gate_ref[g, :] = proj(SEG_GATE) + bg_ref[...]

        q = _dot(qn, wuq_ref[...])
        for hd in range(MLA_HEADS):
            sl = slice(hd * HEAD_BLOCK, (hd + 1) * HEAD_BLOCK)
            qh = q[:, sl]
            if rope:
                qh = _rope_block(qh, cos, slo, shi)
            q_ref[g, sl] = (qh * Q_PRESCALE).astype(BF16)

        if not rope:
            ckv_ref[g, :] = ckv
            kpe_t = zkpe.T
            seq = kpet_ref.shape[2]
            for sq in range(INPROJ_ROWS // seq):
                kpet_ref[r0 // seq + sq] = kpe_t[:ROPE_DIM, sq * seq:(sq + 1) * seq]
        kp = pltpu.roll(zkpe, ROPE_LANE0, 1)
        if rope:
            kp = _rope_block(kp, cos, slo, shi)
        kvn = _dot(ckv.astype(BF16), wukv_ref[...])
        _store_kv(kvn, kp, kc_ref, v_ref, g)

        mv_ref[g, :] = proj(SEG_MV).astype(BF16)
        mk_ref[g, :] = (proj(SEG_MK) * (MLSTM_DK ** -0.5)).astype(BF16)
        mq_ref[g, :] = proj(SEG_MQ).astype(BF16)


def _inproj(x, mod, rope_tabs, gmix, w_in_p, gq, w_uq_p, gkv, w_ukv_p, bg, *, seq_len):
    n_tok = x.shape[0]
    tm = TOKEN_TILE
    tiles_per_seq = seq_len // tm
    rope = rope_tabs is not None
    tok = lambda c: pl.BlockSpec((tm, c), lambda i: (i, 0))
    in_specs = [tok(D_MODEL), pl.BlockSpec(mod.shape, lambda i: (0, 0))]
    args = [x, mod]
    if rope:
        in_specs += [pl.BlockSpec((tm, LANES), lambda i: (i % tiles_per_seq, 0))] * 3
        args += list(rope_tabs)
    win_rows = SEG_MO[0]
    in_specs += [_const_spec(gmix.shape), _const_spec((win_rows, w_in_p.shape[1]))]
    in_specs += [_const_spec(a.shape) for a in (gq, w_uq_p, gkv, w_ukv_p, bg)]
    args += [gmix, w_in_p, gq, w_uq_p, gkv, w_ukv_p, bg]
    widths = [(MLA_HEADS * HEAD_BLOCK, BF16), (MLA_HEADS * HEAD_BLOCK, BF16), (V_WIDTH, BF16),
              (MLSTM_HEADS * MLSTM_DK, BF16), (MLSTM_HEADS * MLSTM_DK, BF16), (MLSTM_HEADS * MLSTM_DV, BF16),
              (LANES, F32)]
    out_specs = [tok(c) for c, _ in widths]
    out_shape = [jax.ShapeDtypeStruct((n_tok, c), dt) for c, dt in widths]
    if not rope:
        out_specs += [tok(KV_RANK), pl.BlockSpec((tm // seq_len, ROPE_DIM, seq_len), lambda i: (i, 0, 0))]
        out_shape += [jax.ShapeDtypeStruct((n_tok, KV_RANK), F32),
                      jax.ShapeDtypeStruct((n_tok // seq_len, ROPE_DIM, seq_len), F32)]
    return pl.pallas_call(
        functools.partial(_inproj_kernel, rope=rope, mod_row0=1 if rope else 0,
                          tiles_per_seq=tiles_per_seq if rope else None),
        grid=(n_tok // tm,),
        in_specs=in_specs,
        out_specs=out_specs,
        out_shape=out_shape,
        compiler_params=_params(("parallel",)),
        name="inproj_lat" if rope else "inproj_ctx",
    )(*args)


def _cachekv_kernel(ckv_ref, kp_ref, wukv_ref, kc_ref, v_ref):
    kvn = _dot(ckv_ref[...].astype(BF16), wukv_ref[...])
    _store_kv(kvn, kp_ref[...], kc_ref, v_ref)


def _cachekv(cache_ckv, krope_blk, w_ukv_p):
    b, past, _ = cache_ckv.shape
    return pl.pallas_call(
        _cachekv_kernel,
        grid=(b,),
        in_specs=[pl.BlockSpec((None, past, KV_RANK), lambda i: (i, 0, 0)),
                  pl.BlockSpec((None, past, LANES), lambda i: (i, 0, 0)),
                  _const_spec(w_ukv_p.shape)],
        out_specs=[pl.BlockSpec((None, past, MLA_HEADS * HEAD_BLOCK), lambda i: (i, 0, 0)),
                   pl.BlockSpec((None, past, V_WIDTH), lambda i: (i, 0, 0))],
        out_shape=[jax.ShapeDtypeStruct((b, past, MLA_HEADS * HEAD_BLOCK), BF16),
                   jax.ShapeDtypeStruct((b, past, V_WIDTH), BF16)],
        compiler_params=_params(("parallel",)),
        name="cachekv",
    )(cache_ckv, krope_blk, w_ukv_p)


def _attn_kernel(*refs, has_cache, n_cast, has_adaln):
    n_in = 5 if has_cache else 3
    if has_cache:
        q_ref, k_ref, v_ref, kc_ref, vc_ref = refs[:n_in]
    else:
        q_ref, k_ref, v_ref = refs[:n_in]
    n_all_in = n_in + n_cast + (4 if has_adaln else 0)
    o_ref = refs[n_all_in]
    for w_ref, wb_ref in zip(refs[n_in:n_in + n_cast], refs[n_all_in + 1:n_all_in + 1 + n_cast]):
        wb_ref[...] = w_ref[...].astype(BF16)
    if has_adaln:
        _adaln_block(*refs[n_in + n_cast:n_all_in], refs[n_all_in + 1 + n_cast])
    tq = q_ref.shape[1]
    lane = lax.broadcasted_iota(jnp.int32, (tq, LANES), 1)
    for sq in range(q_ref.shape[0]):
        hsl = [slice(hd * HEAD_BLOCK, (hd + 1) * HEAD_BLOCK) for hd in range(MLA_HEADS)]
        vsl = [slice(hd // 2 * V_SLAB, (hd // 2 + 1) * V_SLAB) for hd in range(MLA_HEADS)]
        s = [_dot_nt(q_ref[sq, :, sl], k_ref[sq, :, sl]) for sl in hsl]
        m = [jnp.max(si, axis=-1, keepdims=True) for si in s]
        if has_cache:
            sc = [_dot_nt(q_ref[sq, :, sl], kc_ref[sq, :, sl]) for sl in hsl]
            m = [jnp.maximum(mi, jnp.max(ci, axis=-1, keepdims=True)) for mi, ci in zip(m, sc)]
        p = [jnp.exp2(si - mi).astype(BF16) for si, mi in zip(s, m)]
        o = [_dot(pi, v_ref[sq, :, sl]) for pi, sl in zip(p, vsl)]
        if has_cache:
            pc = [jnp.exp2(ci - mi).astype(BF16) for ci, mi in zip(sc, m)]
            o = [oi + _dot(pi, vc_ref[sq, :, sl]) for oi, pi, sl in zip(o, pc, vsl)]
        outs = [oi[:, :LANES] * (1.0 / oi[:, LANES:]) for oi in o]
        for pair in range(MLA_HEADS // 2):
            o_ref[sq, :, pair * LANES:(pair + 1) * LANES] = jnp.where(
                lane < MLA_V_DIM, outs[2 * pair], outs[2 * pair + 1]).astype(BF16)


def _attention(q, k, v, cache=None, cast_weights=(), adaln_tail=None, seqs_per_step=1):
    nb, s, _ = q.shape
    tq = Q_TILE
    ns = seqs_per_step
    b = nb // ns
    n_steps = b * (s // tq)
    kw, vw, ow = MLA_HEADS * HEAD_BLOCK, V_WIDTH, MLA_HEADS * MLA_V_DIM
    in_specs = [pl.BlockSpec((ns, tq, kw), lambda i, j: (i, j, 0)),
                pl.BlockSpec((ns, s, kw), lambda i, j: (i, 0, 0)),
                pl.BlockSpec((ns, s, vw), lambda i, j: (i, 0, 0))]
    args = [q, k, v]
    if cache is not None:
        past = cache[0].shape[1]
        in_specs += [pl.BlockSpec((ns, past, kw), lambda i, j: (i, 0, 0)),
                     pl.BlockSpec((ns, past, vw), lambda i, j: (i, 0, 0))]
        args += list(cache)
    step = lambda i, j: i * (s // tq) + j
    casts = [w if isinstance(w, tuple) else (w, 0, w.shape[0]) for w in cast_weights]
    cast_weights = [w for w, _, _ in casts]

    def cast_in_spec(w, row0, n_rows):
        rows = n_rows // n_steps
        if (row0, n_rows) == (0, w.shape[0]):
            return pl.BlockSpec((rows, w.shape[1]), lambda i, j: (step(i, j), 0))
        return pl.BlockSpec((pl.Element(rows), pl.Element(w.shape[1])),
                            lambda i, j: (pl.multiple_of(row0 + step(i, j) * rows, math.gcd(row0, rows)), 0))

    w_in_specs = [cast_in_spec(*c) for c in casts]
    w_specs = [pl.BlockSpec((n // n_steps, w.shape[1]), lambda i, j: (step(i, j), 0)) for w, _, n in casts]
    extra_in, extra_out, extra_shape = [], [], []
    if adaln_tail is not None:
        tn = N_MOD_TAIL * D_MODEL // n_steps
        col0 = N_MOD_MIXER * D_MODEL // tn
        extra_in = [pl.BlockSpec(adaln_tail[0].shape, lambda i, j: (0, 0)),
                    pl.BlockSpec(adaln_tail[1].shape, lambda i, j: (0, 0)),
                    pl.BlockSpec((D_MODEL, tn), lambda i, j: (0, col0 + step(i, j))),
                    pl.BlockSpec((1, tn), lambda i, j: (0, col0 + step(i, j)))]
        extra_out = [pl.BlockSpec((COND_ROWS, tn), lambda i, j: (0, step(i, j)))]
        extra_shape = [jax.ShapeDtypeStruct((COND_ROWS, N_MOD_TAIL * D_MODEL), F32)]
    outs = pl.pallas_call(
        functools.partial(_attn_kernel, has_cache=cache is not None, n_cast=len(cast_weights),
                          has_adaln=adaln_tail is not None),
        grid=(b, s // tq),
        in_specs=in_specs + w_in_specs + extra_in,
        out_specs=[pl.BlockSpec((ns, tq, ow), lambda i, j: (i, j, 0))] + w_specs + extra_out,
        out_shape=[jax.ShapeDtypeStruct((nb, s, ow), BF16)]
                  + [jax.ShapeDtypeStruct((n, w.shape[1]), BF16) for w, _, n in casts] + extra_shape,
        compiler_params=_params(("parallel", "arbitrary")),
        name="attn_lat" if cache is not None else "attn_ctx",
    )(*args, *cast_weights, *(adaln_tail or ()))
    return outs[0] if len(outs) == 1 else outs


def _lane_bcast(x, k, width=None):
    y = jnp.broadcast_to(x[:, k:k + 1], x.shape)
    reps = (width or LANES) // LANES
    return y if reps == 1 else jnp.concatenate([y] * reps, axis=1)


def _prefix_sum_rows(x):
    n = x.shape[0]
    r = lax.broadcasted_iota(jnp.int32, (n, n), 0)
    c = lax.broadcasted_iota(jnp.int32, (n, n), 1)
    tri = jnp.where(c <= r, 1.0, 0.0).astype(BF16)
    hi = x.astype(BF16)
    r1 = x - hi.astype(F32)
    mid = r1.astype(BF16)
    lo = (r1 - mid.astype(F32)).astype(BF16)
    return _dot(tri, hi) + _dot(tri, mid) + _dot(tri, lo)


def _gate_prep(g):
    n = g.shape[0]
    lane = lax.broadcasted_iota(jnp.int32, (n, LANES), 1)
    fpre = pltpu.roll(g, LANES - N_DIRHEAD, 1)
    lf = jnp.minimum(fpre, 0.0) - jnp.log(1.0 + jnp.exp(-jnp.abs(fpre)))
    binc = _prefix_sum_rows(lf)
    tot = binc[n - 1:n, :]
    b = jnp.where(lane < MLSTM_HEADS, binc, tot - binc + lf)
    return b, g - b, tot


def _chunk_setup(b, u, tot, m_prev):
    g_last = jnp.maximum(jnp.max(u, axis=0, keepdims=True), m_prev)
    u2 = u * LOG2E
    w = jnp.exp2(u2 - g_last * LOG2E)
    return u2, (b * LOG2E).T, w.T, tot + g_last, jnp.exp(m_prev - g_last)


def _state_update(vt, k, wt, kk):
    c_new = _dot((vt * wt[kk:kk + 1, :]).astype(BF16), k)
    n_new = _dot(wt[:N_DIRHEAD, :].astype(BF16), k)[kk:kk + 1, :]
    return c_new, n_new


def _mlstm_dir_t(s0t, vt, u_b, bt_row, m_row, *, fwd, inter=None):
    sdt, den, g = _dir_weights(s0t, u_b, m_row, fwd=fwd)
    return _dir_finish(_dot(vt, sdt), den, g, bt_row, m_row, inter)


def _dir_weights(s0t, u_b, m_row, *, fwd):
    n = s0t.shape[0]
    r = lax.broadcasted_iota(jnp.int32, (n, n), 0)
    c = lax.broadcasted_iota(jnp.int32, (n, n), 1)
    a = jnp.where((r <= c) if fwd else (r >= c), u_b, -jnp.inf)
    g = jnp.maximum(jnp.max(a, axis=0, keepdims=True), m_row)
    sdt = s0t * jnp.exp2(a - g)
    return sdt.astype(BF16), jnp.sum(sdt, axis=0, keepdims=True), g


def _dir_finish(ht, den, g, bt_row, m_row, inter=None):
    if inter is not None:
        w_inter = jnp.exp2(m_row - g)
        ht = ht + w_inter * inter[0]
        den = den + w_inter * inter[1]
    floor = jnp.exp2(-(bt_row + g))
    return ht * (1.0 / jnp.maximum(jnp.abs(den), floor))


def _mlstm_ctx_kernel(q_ref, k_ref, v_ref, g_ref, ht_ref, c_ref, n_ref, m_ref):
    n = q_ref.shape[1]
    zero_lanes = jnp.zeros((1, LANES), F32)
    zero_row = jnp.zeros((1, n), F32)
    dirs = ((True, 0), (False, MLSTM_HEADS))
    setup = [_chunk_setup(*_gate_prep(g_ref[sq]), zero_lanes) for sq in range(q_ref.shape[0])]
    units = [(sq, hd) for sq in range(q_ref.shape[0]) for hd in range(MLSTM_HEADS)]
    ks = [k_ref[sq, :, hd * MLSTM_DK:(hd + 1) * MLSTM_DK] for sq, hd in units]
    s0ts = [_dot_nt(k, q_ref[sq, :, hd * MLSTM_DK:(hd + 1) * MLSTM_DK]) for k, (sq, hd) in zip(ks, units)]
    vts = [v_ref[sq, :, hd * MLSTM_DV:(hd + 1) * MLSTM_DV].T for sq, hd in units]
    wts = [[_dir_weights(s0t, _lane_bcast(setup[sq][0], off + hd, n), zero_row, fwd=fwd) for fwd, off in dirs]
           for s0t, (sq, hd) in zip(s0ts, units)]
    hts = [[_dot(vt, sdt) for sdt, _, _ in wt] for vt, wt in zip(vts, wts)]
    for (sq, hd), k, vt, wt, ht2 in zip(units, ks, vts, wts, hts):
        _, bt2, w_t, _, _ = setup[sq]
        ht_ref[sq, hd * MLSTM_DV:(hd + 1) * MLSTM_DV, :] = sum(
            _dir_finish(ht_d, den, g, bt2[off + hd:off + hd + 1, :], zero_row)
            for ht_d, (_, den, g), (_, off) in zip(ht2, wt, dirs))
        for _, off in dirs:
            c_ref[sq, off + hd], n_ref[sq, off + hd:off + hd + 1, :] = _state_update(vt, k, w_t, off + hd)
    for sq in range(q_ref.shape[0]):
        m_ref[sq] = setup[sq][3]


def _mlstm_ctx(mq, mk, mv, gates):
    b, s, _ = mq.shape
    ns = CTX_SEQS_PER_STEP
    seq = lambda c: pl.BlockSpec((ns, s, c), lambda i: (i, 0, 0))
    return pl.pallas_call(
        _mlstm_ctx_kernel,
        grid=(b // ns,),
        in_specs=[seq(MLSTM_HEADS * MLSTM_DK), seq(MLSTM_HEADS * MLSTM_DK), seq(MLSTM_HEADS * MLSTM_DV), seq(LANES)],
        out_specs=[pl.BlockSpec((ns, MLSTM_HEADS * MLSTM_DV, s), lambda i: (i, 0, 0)),
                   pl.BlockSpec((ns, N_DIRHEAD, MLSTM_DV, MLSTM_DK), lambda i: (i, 0, 0, 0)),
                   pl.BlockSpec((ns, N_DIRHEAD, MLSTM_DK), lambda i: (i, 0, 0)),
                   pl.BlockSpec((ns, 1, LANES), lambda i: (i, 0, 0))],
        out_shape=[jax.ShapeDtypeStruct((b, MLSTM_HEADS * MLSTM_DV, s), F32),
                   jax.ShapeDtypeStruct((b, N_DIRHEAD, MLSTM_DV, MLSTM_DK), F32),
                   jax.ShapeDtypeStruct((b, N_DIRHEAD, MLSTM_DK), F32),
                   jax.ShapeDtypeStruct((b, 1, LANES), F32)],
        compiler_params=_params(("parallel",)),
        name="mlstm_ctx",
    )(mq, mk, mv, gates)


def _mlstm_lat_kernel(qf_ref, kf_ref, vf_ref, gf_ref, qb_ref, kb_ref, vb_ref, gb_ref, c0_ref, n0_ref, m0_ref,
                      ht_ref, c_ref, n_ref, m_ref, *, n_chunks):
    step = pl.program_id(1)

    @pl.when(step == 0)
    def _():
        c_ref[...] = c0_ref[...]
        n_ref[...] = n0_ref[...]
        m_ref[...] = m0_ref[...]
        ht_ref[...] = jnp.zeros_like(ht_ref)

    n = qf_ref.shape[0]
    gate_f, gate_b = _gate_prep(gf_ref[...]), _gate_prep(gb_ref[...])
    is_f = lax.broadcasted_iota(jnp.int32, (n, LANES), 1) < MLSTM_HEADS
    b, u, tot = (jnp.where(is_f[:x.shape[0]], x, y) for x, y in zip(gate_f, gate_b))
    m_prev = m_ref[...]
    u2, bt2, w_t, m_new, decay = _chunk_setup(b, u, tot, m_prev)
    m2_prev = m_prev * LOG2E
    units = [(fwd, hd, hd if fwd else MLSTM_HEADS + hd) for fwd in (True, False) for hd in range(MLSTM_HEADS)]
    refs = {True: (qf_ref, kf_ref, vf_ref), False: (qb_ref, kb_ref, vb_ref)}
    qs = [refs[fwd][0][:, hd * MLSTM_DK:(hd + 1) * MLSTM_DK] for fwd, hd, _ in units]
    ks = [refs[fwd][1][:, hd * MLSTM_DK:(hd + 1) * MLSTM_DK] for fwd, hd, _ in units]
    vts = [refs[fwd][2][:, hd * MLSTM_DV:(hd + 1) * MLSTM_DV].T for fwd, hd, _ in units]
    s0ts = [_dot_nt(k, q) for k, q in zip(ks, qs)]
    c_prevs = [c_ref[kk] for _, _, kk in units]
    n_prevs = [n_ref[kk:kk + 1, :] for _, _, kk in units]
    inters = [(_dot_nt(c.astype(BF16), q), _dot_nt(jnp.broadcast_to(nv, (8, MLSTM_DK)).astype(BF16), q)[0:1, :])
              for c, nv, q in zip(c_prevs, n_prevs, qs)]
    m_rows = [_lane_bcast(m2_prev, kk, n) for _, _, kk in units]
    wts = [_dir_weights(s0t, _lane_bcast(u2, kk, n), m_row, fwd=fwd)
           for s0t, m_row, (fwd, _, kk) in zip(s0ts, m_rows, units)]
    hts = [_dot(vt, sdt) for vt, (sdt, _, _) in zip(vts, wts)]
    for (fwd, hd, kk), k, vt, (_, den, g), ht, inter, m_row, c_prev, n_prev in zip(
            units, ks, vts, wts, hts, inters, m_rows, c_prevs, n_prevs):
        chunk = step if fwd else n_chunks - 1 - step
        ht_ref[chunk, hd * MLSTM_DV:(hd + 1) * MLSTM_DV, :] += _dir_finish(ht, den, g, bt2[kk:kk + 1, :], m_row, inter)
        c_new, n_new = _state_update(vt, k, w_t, kk)
        dk = _lane_bcast(decay, kk)
        c_ref[kk] = dk * c_prev + c_new
        n_ref[kk:kk + 1, :] = dk * n_prev + n_new
    m_ref[...] = m_new


def _mlstm_lat(mq, mk, mv, gates, c0, n0, m0):
    b, s, _ = mq.shape
    lc = MLSTM_CHUNK
    nc = s // lc
    fw = lambda c: pl.BlockSpec((None, lc, c), lambda i, j: (i, j, 0))
    bw = lambda c: pl.BlockSpec((None, lc, c), lambda i, j: (i, nc - 1 - j, 0))
    dk, dv = MLSTM_HEADS * MLSTM_DK, MLSTM_HEADS * MLSTM_DV
    return pl.pallas_call(
        functools.partial(_mlstm_lat_kernel, n_chunks=nc),
        grid=(b, nc),
        in_specs=[fw(dk), fw(dk), fw(dv), fw(LANES), bw(dk), bw(dk), bw(dv), bw(LANES),
                  pl.BlockSpec((None, N_DIRHEAD, MLSTM_DV, MLSTM_DK), lambda i, j: (i, 0, 0, 0)),
                  pl.BlockSpec((None, N_DIRHEAD, MLSTM_DK), lambda i, j: (i, 0, 0)),
                  pl.BlockSpec((None, 1, LANES), lambda i, j: (i, 0, 0))],
        out_specs=pl.BlockSpec((nc, dv, lc), lambda i, j: (i, 0, 0)),
        out_shape=jax.ShapeDtypeStruct((b * nc, dv, lc), F32),
        scratch_shapes=[pltpu.VMEM((N_DIRHEAD, MLSTM_DV, MLSTM_DK), F32),
                        pltpu.VMEM((N_DIRHEAD, MLSTM_DK), F32),
                        pltpu.VMEM((1, LANES), F32)],
        compiler_params=_params(("parallel", "arbitrary")),
        name="mlstm_lat",
    )(mq, mk, mv, gates, mq, mk, mv, gates, c0, n0, m0)


def _tail_kernel(x_ref, modin_ref, mod_ref, att_ref, hst_ref, gmix_ref, wgate_ref, gml_ref, womla_ref, womlstm_ref,
                 wout_ref, gffn_ref, wfin_ref, wfout_ref, gfin_ref, y_ref, *, mod_row0, tiles_per_seq):
    shift_mix, scale_mix = _mod_chunks(modin_ref, mod_row0, tiles_per_seq)
    gate_mix, shift_ffn, scale_ffn, gate_ffn = _mod_chunks(mod_ref, mod_row0, tiles_per_seq)
    gml = gml_ref[...]
    tm = x_ref.shape[0]
    groups = [slice(r, r + TAIL_ROWS) for r in range(0, tm, TAIL_ROWS)]
    dv = MLSTM_HEADS * MLSTM_DV
    hb = [_mixer_input(x_ref[g, :], gmix_ref[...], shift_mix, scale_mix) for g in groups]
    smo = [_sigmoid(_dot_nt(h, wgate_ref[:dv, :])) for h in hb]
    g_a = [_sigmoid(_dot_nt(h, wgate_ref[dv:dv + D_MODEL, :])) for h in hb]
    g_b = [_sigmoid(_dot_nt(h, wgate_ref[dv + D_MODEL:, :])) for h in hb]
    a = [_dot(att_ref[g, :], womla_ref[...]) for g in groups]
    hm = []
    for gi in range(len(groups)):
        hs = hst_ref[gi].T
        parts = []
        for hd in range(MLSTM_HEADS):
            sl = slice(hd * MLSTM_DV, (hd + 1) * MLSTM_DV)
            parts.append((_rms(hs[:, sl], gml[:, sl]) * smo[gi][:, sl]).astype(BF16))
        hm.append(jnp.concatenate(parts, axis=1))
    bm = [_dot(h, womlstm_ref[...]) for h in hm]
    merged = [(ga_i * ai + gb_i * bi).astype(BF16) for ga_i, gb_i, ai, bi in zip(g_a, g_b, a, bm)]
    x1 = [x_ref[g, :] + gate_mix * _dot(mi, wout_ref[...]) for g, mi in zip(groups, merged)]
    h2 = [(_rms(xi, gffn_ref[...]) * (1.0 + scale_ffn) + shift_ffn).astype(BF16) for xi in x1]
    ga = [_dot(hi, wfin_ref[:, :FFN_HIDDEN]) for hi in h2]
    gu = [_dot(hi, wfin_ref[:, FFN_HIDDEN:]) for hi in h2]
    act = [(gi * _sigmoid(gi) * ui).astype(BF16) for gi, ui in zip(ga, gu)]
    for g, xi, ci in zip(groups, x1, act):
        y_ref[g, :] = _rms(xi + gate_ffn * _dot(ci, wfout_ref[...]), gfin_ref[...])


def _tail(x, mod_in, mod, att, hst, gmix, w_gate, gml, w_o_mla, w_o_mlstm, w_out, gffn, w_ffn_in, w_ffn_out, gfin, *,
          seq_len, latent):
    n_tok = x.shape[0]
    tm = TOKEN_TILE
    assert hst.shape[0] * TAIL_ROWS == n_tok and hst.shape[2] == TAIL_ROWS
    tok = lambda c: pl.BlockSpec((tm, c), lambda i: (i, 0))
    whole = lambda a: pl.BlockSpec(a.shape, lambda i: (0, 0))
    consts = (gmix, w_gate, gml, w_o_mla, w_o_mlstm, w_out, gffn, w_ffn_in, w_ffn_out, gfin)
    return pl.pallas_call(
        functools.partial(_tail_kernel, mod_row0=1 if latent else 0, tiles_per_seq=seq_len // tm if latent else None),
        grid=(n_tok // tm,),
        in_specs=[tok(D_MODEL), whole(mod_in), whole(mod), tok(att.shape[1]),
                  pl.BlockSpec((tm // TAIL_ROWS,) + hst.shape[1:], lambda i: (i, 0, 0))]
                 + [_const_spec(a.shape) for a in consts],
        out_specs=tok(D_MODEL),
        out_shape=jax.ShapeDtypeStruct((n_tok, D_MODEL), F32),
        compiler_params=_params(("parallel",)),
        name="tail_lat" if latent else "tail_ctx",
    )(x, mod_in, mod, att, hst, *consts)


_IN_SIZES = (Q_RANK, KV_RANK, ROPE_DIM, MLSTM_HEADS * MLSTM_DK, MLSTM_HEADS * MLSTM_DK, MLSTM_HEADS * MLSTM_DV,
             4 * MLSTM_HEADS, MLSTM_HEADS * MLSTM_DV, 2 * D_MODEL)
_IN_OFF = tuple(sum(_IN_SIZES[:i]) for i in range(len(_IN_SIZES) + 1))
PACK_ROWS = LANES
_PAD_AFTER = tuple((_SEG_OFF[i + 1] // PACK_ROWS, _SEG_WIDTHS[i] - _IN_SIZES[i], _IN_SIZES[i] % PACK_ROWS)
                   for i in range(len(_IN_SIZES)) if _SEG_WIDTHS[i] != _IN_SIZES[i])


PACK_PIECES = 8
N_PACK_STEPS = pl.cdiv(SEG_MO[0] // PACK_ROWS, PACK_PIECES)
_N_IN = _IN_OFF[-1]


def _pack_src_row(p):
    src = p * PACK_ROWS
    for first_piece_after, pad, _ in _PAD_AFTER:
        src = src - jnp.where(p >= first_piece_after, pad, 0)
    src = jnp.minimum(src, _N_IN - PACK_ROWS)
    return pl.multiple_of(src, math.gcd(PACK_ROWS, _N_IN, *(pad for _, pad, _ in _PAD_AFTER)))


def _pack_in_kernel(*refs):
    w_refs, o_ref = refs[:PACK_PIECES], refs[PACK_PIECES]
    row = lax.broadcasted_iota(jnp.int32, w_refs[0].shape, 0)
    for j, w_ref in enumerate(w_refs):
        p = pl.program_id(0) * PACK_PIECES + j
        valid = jnp.where(p < N_IN_PACKED // PACK_ROWS, PACK_ROWS, 0)
        for first_piece_after, _, width in _PAD_AFTER:
            valid = jnp.where(p == first_piece_after - 1, width, valid)
        o_ref[j * PACK_ROWS:(j + 1) * PACK_ROWS, :] = jnp.where(row < valid, w_ref[...], 0.0).astype(BF16)


def _pack_in(w_in_t):
    n, k = w_in_t.shape
    piece = lambda j: pl.BlockSpec((pl.Element(PACK_ROWS), pl.Element(k)),
                                   lambda i: (_pack_src_row(i * PACK_PIECES + j), 0))
    return pl.pallas_call(
        _pack_in_kernel,
        grid=(N_PACK_STEPS,),
        in_specs=[piece(j) for j in range(PACK_PIECES)],
        out_specs=pl.BlockSpec((PACK_PIECES * PACK_ROWS, k), lambda i: (i, 0)),
        out_shape=jax.ShapeDtypeStruct((N_PACK_STEPS * PACK_PIECES * PACK_ROWS, k), BF16),
        compiler_params=_params(("parallel",)),
        name="pack_in",
    )(*([w_in_t] * PACK_PIECES))


def _pack_small(w_uq, w_ukv, b_gates):
    qd = NOPE_DIM + ROPE_DIM
    w_uq_p = jnp.pad(w_uq.reshape(Q_RANK, MLA_HEADS, qd), ((0, 0), (0, 0), (0, HEAD_BLOCK - qd)))
    w_uq_p = w_uq_p.reshape(Q_RANK, MLA_HEADS * HEAD_BLOCK).astype(BF16)
    kv = w_ukv.reshape(KV_RANK, MLA_HEADS, NOPE_DIM + MLA_V_DIM)
    wk = jnp.pad(kv[:, :, :NOPE_DIM], ((0, 0), (0, 0), (0, HEAD_BLOCK - NOPE_DIM))).reshape(KV_RANK, -1)
    wv = kv[:, :, NOPE_DIM:].reshape(KV_RANK, -1)
    w_ukv_p = jnp.concatenate([wk, wv], axis=1).astype(BF16)
    bg = jnp.pad(b_gates, ((0, 0), (0, LANES - b_gates.shape[1])))
    return w_uq_p, w_ukv_p, bg


def _rope_tables(n_tokens):
    pos = np.arange(n_tokens)
    row = (pos // GRID_W).astype(np.float64)[:, None]
    col = (pos % GRID_W).astype(np.float64)[:, None]
    half = ROPE_DIM // 2
    inv = (np.float32(ROPE_BASE) ** (-np.arange(0, half, 2, dtype=np.float32) / np.float32(half))).astype(np.float64)
    r = np.arange(LANES) - ROPE_LANE0
    in_rope = (r >= 0) & (r < ROPE_DIM)
    rr = np.clip(r, 0, ROPE_DIM - 1)
    freq = inv[rr % (half // 2)][None, :]
    ang = np.where((rr // half == 0)[None, :], row * freq, col * freq).astype(np.float32).astype(np.float64)
    first = (rr % half) < (half // 2)
    cos = np.where(in_rope[None, :], np.cos(ang), 1.0)
    sin = np.sin(ang)
    sin_lo = np.where((in_rope & first)[None, :], -sin, 0.0)
    sin_hi = np.where((in_rope & ~first)[None, :], sin, 0.0)
    return tuple(jnp.asarray(t, dtype=F32) for t in (cos, sin_lo, sin_hi))


def kernel(x_prompt, x_sample, cache_ckv, cache_krope, state_C, state_n, state_m, c, c_ctx, w_mod, b_mod, g_norm_mix,
           w_in, b_gates, g_q_norm, w_uq, g_kv_norm, w_ukv, g_mlstm_norm, w_o_mla, w_o_mlstm, w_out, g_norm_ffn,
           w_ffn_in, w_ffn_out, g_final):
    bp, sp, d = x_prompt.shape
    bs, ss, _ = x_sample.shape
    layer = 0
    assert w_mod.shape[0] == 1 and sp == MLSTM_CHUNK and ss % MLSTM_CHUNK == 0

    adaln_args = (c_ctx[None, :], c, w_mod[layer], b_mod[layer][None, :])
    mod_in = _adaln(*adaln_args)

    w_in_t = w_in[layer].T
    w_in_p = _pack_in(w_in_t)
    w_uq_p, w_ukv_p, bg = _pack_small(w_uq[layer], w_ukv[layer], b_gates[layer][None, :])
    row = lambda g: g[layer][None, :]
    shared_in = (row(g_norm_mix), w_in_p, row(g_q_norm), w_uq_p, row(g_kv_norm), w_ukv_p, bg)
    seq3 = lambda a, b_, s_: a.reshape(b_, s_, a.shape[-1])

    xl = x_sample.reshape(bs * ss, d)
    q, kc, v, mq_l, mk_l, mv_l, gates_l = _inproj(xl, mod_in, _rope_tables(ss), *shared_in, seq_len=ss)
    krope_blk = jnp.pad(cache_krope[:, layer], ((0, 0), (0, 0), (ROPE_LANE0, LANES - ROPE_LANE0 - ROPE_DIM)))
    cache = _cachekv(cache_ckv[:, layer], krope_blk, w_ukv_p)
    gate_cols = (w_in_t, _IN_OFF[7], _IN_OFF[9] - _IN_OFF[7])
    att_l, wgate_b, womla_b, womlstm_b, wout_b, wfin_b, wfout_b, mod_tail = _attention(
        seq3(q, bs, ss), seq3(kc, bs, ss), seq3(v, bs, ss), cache,
        cast_weights=(gate_cols, w_o_mla[layer], w_o_mlstm[layer], w_out[layer], w_ffn_in[layer], w_ffn_out[layer]),
        adaln_tail=adaln_args)
    tail_w = (row(g_norm_mix), wgate_b, row(g_mlstm_norm), womla_b, womlstm_b, wout_b, row(g_norm_ffn), wfin_b, wfout_b,
              g_final[None, :])

    xc = x_prompt.reshape(bp * sp, d)
    q, kc, v, mq, mk, mv, gates, ckv, kpe = _inproj(xc, mod_in, None, *shared_in, seq_len=sp)
    att = _attention(seq3(q, bp, sp), seq3(kc, bp, sp), seq3(v, bp, sp), seqs_per_step=CTX_SEQS_PER_STEP)
    hs, new_c, new_n, new_m = _mlstm_ctx(seq3(mq, bp, sp), seq3(mk, bp, sp), seq3(mv, bp, sp), seq3(gates, bp, sp))
    y_prompt = _tail(xc, mod_in, mod_tail, att.reshape(bp * sp, -1), hs, *tail_w,
                     seq_len=sp, latent=False).reshape(bp, sp, d)
    new_ckv = ckv.reshape(bp, 1, sp, KV_RANK)
    new_krope = jnp.swapaxes(kpe, 1, 2).reshape(bp, 1, sp, ROPE_DIM)
    new_C = new_c.reshape(bp, 1, 2, MLSTM_HEADS, MLSTM_DV, MLSTM_DK)
    new_N = new_n.reshape(bp, 1, 2, MLSTM_HEADS, MLSTM_DK)
    new_M = new_m[:, 0, :N_DIRHEAD].reshape(bp, 1, 2, MLSTM_HEADS)

    c0 = state_C[:, layer].reshape(bs, N_DIRHEAD, MLSTM_DV, MLSTM_DK)
    n0 = state_n[:, layer].reshape(bs, N_DIRHEAD, MLSTM_DK)
    m0 = jnp.pad(state_m[:, layer].reshape(bs, 1, N_DIRHEAD), ((0, 0), (0, 0), (0, LANES - N_DIRHEAD)))
    hs = _mlstm_lat(seq3(mq_l, bs, ss), seq3(mk_l, bs, ss), seq3(mv_l, bs, ss), seq3(gates_l, bs, ss), c0, n0, m0)
    y_sample = _tail(xl, mod_in, mod_tail, att_l.reshape(bs * ss, -1), hs, *tail_w,
                     seq_len=ss, latent=True).reshape(bs, ss, d)
    return (y_prompt, y_sample, new_ckv, new_krope, new_C, new_N, new_M)
```

```python
import functools
import math

import jax
import jax.numpy as jnp
import numpy as np
from jax import lax
from jax.experimental import pallas as pl
from jax.experimental.pallas import tpu as pltpu

F32 = jnp.float32
BF16 = jnp.bfloat16

D_MODEL = 1024
GRID_W = 64
MLA_HEADS = 8
Q_RANK = 384
KV_RANK = 256
NOPE_DIM = 64
ROPE_DIM = 32
MLA_V_DIM = 64
ROPE_BASE = 10000.0
MLA_SCALE = (NOPE_DIM + ROPE_DIM) ** -0.5
MLSTM_HEADS = 4
MLSTM_DK = 128
MLSTM_DV = 256
FFN_HIDDEN = ((8 * D_MODEL // 3 + 255) // 256) * 256
EPS = 1e-6

LANES = 128
HEAD_BLOCK = LANES
ROPE_LANE0 = NOPE_DIM
N_DIRHEAD = 2 * MLSTM_HEADS
MLSTM_CHUNK = 256
LOG2E = math.log2(math.e)
Q_PRESCALE = MLA_SCALE * LOG2E
V_SLAB = 2 * LANES
V_WIDTH = (MLA_HEADS // 2) * V_SLAB
TOKEN_TILE = 512
TAIL_ROWS = 256
FFN_CHUNK = 256
INPROJ_ROWS = TOKEN_TILE
Q_TILE = 256
CTX_SEQS_PER_STEP = 4
VMEM_LIMIT = 56 * 1024 * 1024

_SEG_WIDTHS = (Q_RANK, KV_RANK, LANES, MLSTM_HEADS * MLSTM_DK, MLSTM_HEADS * MLSTM_DK,
               MLSTM_HEADS * MLSTM_DV, LANES, MLSTM_HEADS * MLSTM_DV, 2 * D_MODEL)
_SEG_OFF = tuple(sum(_SEG_WIDTHS[:i]) for i in range(len(_SEG_WIDTHS) + 1))
SEG_Q, SEG_KV, SEG_KPE, SEG_MQ, SEG_MK, SEG_MV, SEG_GATE, SEG_MO, SEG_BR = (
    (_SEG_OFF[i], _SEG_OFF[i + 1]) for i in range(9))
N_IN_PACKED = _SEG_OFF[-1]


def _dot(a, b):
    return jnp.dot(a, b, preferred_element_type=F32)


def _dot_nt(a, b):
    return lax.dot_general(a, b, (((1,), (1,)), ((), ())), preferred_element_type=F32)


def _dot_tn(a, b):
    return lax.dot_general(a, b, (((0,), (0,)), ((), ())), preferred_element_type=F32)


def _rms(x, g):
    ms = jnp.mean(x * x, axis=-1, keepdims=True)
    return x * lax.rsqrt(ms + EPS) * g


def _sigmoid(x):
    return 1.0 / (1.0 + jnp.exp(-x))


def _const_spec(shape):
    nd = len(shape)
    return pl.BlockSpec(shape, lambda *_: (0,) * nd, pipeline_mode=pl.Buffered(1))


def _mod_chunks(mod_ref, row0, tiles_per_seq):
    row = row0 if tiles_per_seq is None else row0 + pl.program_id(0) // tiles_per_seq
    m = mod_ref[pl.ds(row, 1), :]
    return [m[:, k * D_MODEL:(k + 1) * D_MODEL] for k in range(m.shape[1] // D_MODEL)]


def _params(sem, flags=None, vmem_mib=None):
    limit = VMEM_LIMIT if vmem_mib is None else vmem_mib * 1024 * 1024
    return pltpu.CompilerParams(dimension_semantics=sem, vmem_limit_bytes=limit, flags=flags)


N_MOD_MIXER = 2
N_MOD_TAIL = 4


COND_ROWS = 8


def _adaln_block(cctx_ref, c_ref, w_ref, b_ref, o_ref):
    pad = jnp.zeros((COND_ROWS - 1 - c_ref.shape[0], D_MODEL), F32)
    c = jnp.concatenate([cctx_ref[...], c_ref[...], pad], axis=0)
    s = c * _sigmoid(c)
    o_ref[...] = _dot(s.astype(BF16), w_ref[...].astype(BF16)) + b_ref[...]


def _adaln(c_ctx, c, w_mod, b_mod):
    tn = D_MODEL
    return pl.pallas_call(
        _adaln_block,
        grid=(N_MOD_MIXER,),
        in_specs=[pl.BlockSpec(c_ctx.shape, lambda j: (0, 0)),
                  pl.BlockSpec(c.shape, lambda j: (0, 0)),
                  pl.BlockSpec((D_MODEL, tn), lambda j: (0, j)),
                  pl.BlockSpec((1, tn), lambda j: (0, j))],
        out_specs=pl.BlockSpec((COND_ROWS, tn), lambda j: (0, j)),
        out_shape=jax.ShapeDtypeStruct((COND_ROWS, N_MOD_MIXER * D_MODEL), F32),
        compiler_params=_params(("arbitrary",)),
        name="adaln",
    )(c_ctx, c, w_mod, b_mod)


def _rope_block(x, cos, sin_lo, sin_hi):
    return x * cos + pltpu.roll(x, LANES - 8, 1) * sin_lo + pltpu.roll(x, 8, 1) * sin_hi


def _store_kv(kvn, kp, kc_ref, v_ref, g=slice(None)):
    for hd in range(MLA_HEADS):
        sl = slice(hd * HEAD_BLOCK, (hd + 1) * HEAD_BLOCK)
        kc_ref[g, sl] = (kvn[:, sl] + kp).astype(BF16)
    v0 = MLA_HEADS * HEAD_BLOCK
    ones = jnp.ones((kvn.shape[0], LANES), BF16)
    for pair in range(MLA_HEADS // 2):
        v_ref[g, pair * V_SLAB:pair * V_SLAB + LANES] = kvn[:, v0 + pair * LANES:v0 + (pair + 1) * LANES].astype(BF16)
        v_ref[g, pair * V_SLAB + LANES:(pair + 1) * V_SLAB] = ones


def _mixer_input(x, g, shift, scale):
    return (_rms(x, g) * (1.0 + scale) + shift).astype(BF16)


def _inproj_kernel(*refs, rope, mod_row0, tiles_per_seq):
    if rope:
        (x_ref, mod_ref, cos_ref, slo_ref, shi_ref, gmix_ref, win_ref, gq_ref, wuq_ref, gkv_ref, wukv_ref,
         bg_ref, q_ref, kc_ref, v_ref, mq_ref, mk_ref, mv_ref, gate_ref) = refs
    else:
        (x_ref, mod_ref, gmix_ref, win_ref, gq_ref, wuq_ref, gkv_ref, wukv_ref,
         bg_ref, q_ref, kc_ref, v_ref, mq_ref, mk_ref, mv_ref, gate_ref, ckv_ref, kpet_ref) = refs
    shift, scale = _mod_chunks(mod_ref, mod_row0, tiles_per_seq)
    for r0 in range(0, x_ref.shape[0], INPROJ_ROWS):
        g = slice(r0, r0 + INPROJ_ROWS)
        if rope:
            cos, slo, shi = cos_ref[g, :], slo_ref[g, :], shi_ref[g, :]
        hb = _mixer_input(x_ref[g, :], gmix_ref[...], shift, scale)

        def proj(seg):
            return _dot_nt(hb, win_ref[seg[0]:seg[1], :])

        qn = _rms(proj(SEG_Q), gq_ref[...]).astype(BF16)
        ckv = _rms(proj(SEG_KV), gkv_ref[...])
        zkpe = proj(SEG_KPE)
        gate_ref[g, :] = proj(SEG_GATE) + bg_ref[...]

        q = _dot(qn, wuq_ref[...])
        for hd in range(MLA_HEADS):
            sl = slice(hd * HEAD_BLOCK, (hd + 1) * HEAD_BLOCK)
            qh = q[:, sl]
            if rope:
                qh = _rope_block(qh, cos, slo, shi)
            q_ref[g, sl] = (qh * Q_PRESCALE).astype(BF16)

        if not rope:
            ckv_ref[g, :] = ckv
            kpe_t = zkpe.T
            seq = kpet_ref.shape[2]
            for sq in range(INPROJ_ROWS // seq):
                kpet_ref[r0 // seq + sq] = kpe_t[:ROPE_DIM, sq * seq:(sq + 1) * seq]
        kp = pltpu.roll(zkpe, ROPE_LANE0, 1)
        if rope:
            kp = _rope_block(kp, cos, slo, shi)
        kvn = _dot(ckv.astype(BF16), wukv_ref[...])
        _store_kv(kvn, kp, kc_ref, v_ref, g)

        mv_ref[g, :] = proj(SEG_MV).astype(BF16)
        mk_ref[g, :] = (proj(SEG_MK) * (MLSTM_DK ** -0.5)).astype(BF16)
        mq_ref[g, :] = proj(SEG_MQ).astype(BF16)


def _inproj(x, mod, rope_tabs, gmix, w_in_p, gq, w_uq_p, gkv, w_ukv_p, bg, *, seq_len):
    n_tok = x.shape[0]
    tm = TOKEN_TILE
    tiles_per_seq = seq_len // tm
    rope = rope_tabs is not None
    tok = lambda c: pl.BlockSpec((tm, c), lambda i: (i, 0))
    in_specs = [tok(D_MODEL), pl.BlockSpec(mod.shape, lambda i: (0, 0))]
    args = [x, mod]
    if rope:
        in_specs += [pl.BlockSpec((tm, LANES), lambda i: (i % tiles_per_seq, 0))] * 3
        args += list(rope_tabs)
    win_rows = SEG_MO[0]
    in_specs += [_const_spec(gmix.shape), _const_spec((win_rows, w_in_p.shape[1]))]
    in_specs += [_const_spec(a.shape) for a in (gq, w_uq_p, gkv, w_ukv_p, bg)]
    args += [gmix, w_in_p, gq, w_uq_p, gkv, w_ukv_p, bg]
    widths = [(MLA_HEADS * HEAD_BLOCK, BF16), (MLA_HEADS * HEAD_BLOCK, BF16), (V_WIDTH, BF16),
              (MLSTM_HEADS * MLSTM_DK, BF16), (MLSTM_HEADS * MLSTM_DK, BF16), (MLSTM_HEADS * MLSTM_DV, BF16),
              (LANES, F32)]
    out_specs = [tok(c) for c, _ in widths]
    out_shape = [jax.ShapeDtypeStruct((n_tok, c), dt) for c, dt in widths]
    if not rope:
        out_specs += [tok(KV_RANK), pl.BlockSpec((tm // seq_len, ROPE_DIM, seq_len), lambda i: (i, 0, 0))]
        out_shape += [jax.ShapeDtypeStruct((n_tok, KV_RANK), F32),
                      jax.ShapeDtypeStruct((n_tok // seq_len, ROPE_DIM, seq_len), F32)]
    return pl.pallas_call(
        functools.partial(_inproj_kernel, rope=rope, mod_row0=1 if rope else 0,
                          tiles_per_seq=tiles_per_seq if rope else None),
        grid=(n_tok // tm,),
        in_specs=in_specs,
        out_specs=out_specs,
        out_shape=out_shape,
        compiler_params=_params(("parallel",)),
        name="inproj_lat" if rope else "inproj_ctx",
    )(*args)


def _cachekv_kernel(ckv_ref, kp_ref, wukv_ref, kc_ref, v_ref):
    kvn = _dot(ckv_ref[...].astype(BF16), wukv_ref[...])
    _store_kv(kvn, kp_ref[...], kc_ref, v_ref)


def _cachekv(cache_ckv, krope_blk, w_ukv_p):
    b, past, _ = cache_ckv.shape
    return pl.pallas_call(
        _cachekv_kernel,
        grid=(b,),
        in_specs=[pl.BlockSpec((None, past, KV_RANK), lambda i: (i, 0, 0)),
                  pl.BlockSpec((None, past, LANES), lambda i: (i, 0, 0)),
                  _const_spec(w_ukv_p.shape)],
        out_specs=[pl.BlockSpec((None, past, MLA_HEADS * HEAD_BLOCK), lambda i: (i, 0, 0)),
                   pl.BlockSpec((None, past, V_WIDTH), lambda i: (i, 0, 0))],
        out_shape=[jax.ShapeDtypeStruct((b, past, MLA_HEADS * HEAD_BLOCK), BF16),
                   jax.ShapeDtypeStruct((b, past, V_WIDTH), BF16)],
        compiler_params=_params(("parallel",)),
        name="cachekv",
    )(cache_ckv, krope_blk, w_ukv_p)


def _attn_kernel(*refs, has_cache, n_cast, has_adaln):
    n_in = 5 if has_cache else 3
    if has_cache:
        q_ref, k_ref, v_ref, kc_ref, vc_ref = refs[:n_in]
    else:
        q_ref, k_ref, v_ref = refs[:n_in]
    n_all_in = n_in + n_cast + (4 if has_adaln else 0)
    o_ref = refs[n_all_in]
    for w_ref, wb_ref in zip(refs[n_in:n_in + n_cast], refs[n_all_in + 1:n_all_in + 1 + n_cast]):
        wb_ref[...] = w_ref[...].astype(BF16)
    if has_adaln:
        _adaln_block(*refs[n_in + n_cast:n_all_in], refs[n_all_in + 1 + n_cast])
    tq = q_ref.shape[1]
    lane = lax.broadcasted_iota(jnp.int32, (tq, LANES), 1)
    for sq in range(q_ref.shape[0]):
        hsl = [slice(hd * HEAD_BLOCK, (hd + 1) * HEAD_BLOCK) for hd in range(MLA_HEADS)]
        vsl = [slice(hd // 2 * V_SLAB, (hd // 2 + 1) * V_SLAB) for hd in range(MLA_HEADS)]
        s = [_dot_nt(q_ref[sq, :, sl], k_ref[sq, :, sl]) for sl in hsl]
        m = [jnp.max(si, axis=-1, keepdims=True) for si in s]
        if has_cache:
            sc = [_dot_nt(q_ref[sq, :, sl], kc_ref[sq, :, sl]) for sl in hsl]
            m = [jnp.maximum(mi, jnp.max(ci, axis=-1, keepdims=True)) for mi, ci in zip(m, sc)]
        p = [jnp.exp2(si - mi).astype(BF16) for si, mi in zip(s, m)]
        o = [_dot(pi, v_ref[sq, :, sl]) for pi, sl in zip(p, vsl)]
        if has_cache:
            pc = [jnp.exp2(ci - mi).astype(BF16) for ci, mi in zip(sc, m)]
            o = [oi + _dot(pi, vc_ref[sq, :, sl]) for oi, pi, sl in zip(o, pc, vsl)]
        outs = [oi[:, :LANES] * (1.0 / oi[:, LANES:]) for oi in o]
        for pair in range(MLA_HEADS // 2):
            o_ref[sq, :, pair * LANES:(pair + 1) * LANES] = jnp.where(
                lane < MLA_V_DIM, outs[2 * pair], outs[2 * pair + 1]).astype(BF16)


def _attention(q, k, v, cache=None, cast_weights=(), adaln_tail=None, seqs_per_step=1):
    nb, s, _ = q.shape
    tq = Q_TILE
    ns = seqs_per_step
    b = nb // ns
    n_steps = b * (s // tq)
    kw, vw, ow = MLA_HEADS * HEAD_BLOCK, V_WIDTH, MLA_HEADS * MLA_V_DIM
    in_specs = [pl.BlockSpec((ns, tq, kw), lambda i, j: (i, j, 0)),
                pl.BlockSpec((ns, s, kw), lambda i, j: (i, 0, 0)),
                pl.BlockSpec((ns, s, vw), lambda i, j: (i, 0, 0))]
    args = [q, k, v]
    if cache is not None:
        past = cache[0].shape[1]
        in_specs += [pl.BlockSpec((ns, past, kw), lambda i, j: (i, 0, 0)),
                     pl.BlockSpec((ns, past, vw), lambda i, j: (i, 0, 0))]
        args += list(cache)
    step = lambda i, j: i * (s // tq) + j
    casts = [w if isinstance(w, tuple) else (w, 0, w.shape[0]) for w in cast_weights]
    cast_weights = [w for w, _, _ in casts]

    def cast_in_spec(w, row0, n_rows):
        rows = n_rows // n_steps
        if (row0, n_rows) == (0, w.shape[0]):
            return pl.BlockSpec((rows, w.shape[1]), lambda i, j: (step(i, j), 0))
        return pl.BlockSpec((pl.Element(rows), pl.Element(w.shape[1])),
                            lambda i, j: (pl.multiple_of(row0 + step(i, j) * rows, math.gcd(row0, rows)), 0))

    w_in_specs = [cast_in_spec(*c) for c in casts]
    w_specs = [pl.BlockSpec((n // n_steps, w.shape[1]), lambda i, j: (step(i, j), 0)) for w, _, n in casts]
    extra_in, extra_out, extra_shape = [], [], []
    if adaln_tail is not None:
        tn = N_MOD_TAIL * D_MODEL // n_steps
        col0 = N_MOD_MIXER * D_MODEL // tn
        extra_in = [pl.BlockSpec(adaln_tail[0].shape, lambda i, j: (0, 0)),
                    pl.BlockSpec(adaln_tail[1].shape, lambda i, j: (0, 0)),
                    pl.BlockSpec((D_MODEL, tn), lambda i, j: (0, col0 + step(i, j))),
                    pl.BlockSpec((1, tn), lambda i, j: (0, col0 + step(i, j)))]
        extra_out = [pl.BlockSpec((COND_ROWS, tn), lambda i, j: (0, step(i, j)))]
        extra_shape = [jax.ShapeDtypeStruct((COND_ROWS, N_MOD_TAIL * D_MODEL), F32)]
    outs = pl.pallas_call(
        functools.partial(_attn_kernel, has_cache=cache is not None, n_cast=len(cast_weights),
                          has_adaln=adaln_tail is not None),
        grid=(b, s // tq),
        in_specs=in_specs + w_in_specs + extra_in,
        out_specs=[pl.BlockSpec((ns, tq, ow), lambda i, j: (i, j, 0))] + w_specs + extra_out,
        out_shape=[jax.ShapeDtypeStruct((nb, s, ow), BF16)]
                  + [jax.ShapeDtypeStruct((n, w.shape[1]), BF16) for w, _, n in casts] + extra_shape,
        compiler_params=_params(("parallel", "arbitrary")),
        name="attn_lat" if cache is not None else "attn_ctx",
    )(*args, *cast_weights, *(adaln_tail or ()))
    return outs[0] if len(outs) == 1 else outs


def _lane_bcast(x, k, width=None):
    y = jnp.broadcast_to(x[:, k:k + 1], x.shape)
    reps = (width or LANES) // LANES
    return y if reps == 1 else jnp.concatenate([y] * reps, axis=1)


def _prefix_sum_rows(x):
    n = x.shape[0]
    r = lax.broadcasted_iota(jnp.int32, (n, n), 0)
    c = lax.broadcasted_iota(jnp.int32, (n, n), 1)
    tri = jnp.where(c <= r, 1.0, 0.0).astype(BF16)
    hi = x.astype(BF16)
    r1 = x - hi.astype(F32)
    mid = r1.astype(BF16)
    lo = (r1 - mid.astype(F32)).astype(BF16)
    return _dot(tri, hi) + _dot(tri, mid) + _dot(tri, lo)


def _gate_prep(g):
    n = g.shape[0]
    lane = lax.broadcasted_iota(jnp.int32, (n, LANES), 1)
    fpre = pltpu.roll(g, LANES - N_DIRHEAD, 1)
    lf = jnp.minimum(fpre, 0.0) - jnp.log(1.0 + jnp.exp(-jnp.abs(fpre)))
    binc = _prefix_sum_rows(lf)
    tot = binc[n - 1:n, :]
    b = jnp.where(lane < MLSTM_HEADS, binc, tot - binc + lf)
    return b, g - b, tot


def _chunk_setup(b, u, tot, m_prev):
    g_last = jnp.maximum(jnp.max(u, axis=0, keepdims=True), m_prev)
    u2 = u * LOG2E
    w = jnp.exp2(u2 - g_last * LOG2E)
    return u2, (b * LOG2E).T, w.T, tot + g_last, jnp.exp(m_prev - g_last)


def _state_update(vt, k, wt, kk):
    c_new = _dot((vt * wt[kk:kk + 1, :]).astype(BF16), k)
    n_new = _dot(wt[:N_DIRHEAD, :].astype(BF16), k)[kk:kk + 1, :]
    return c_new, n_new


def _mlstm_dir_t(s0t, vt, u_b, bt_row, m_row, *, fwd, inter=None):
    sdt, den, g = _dir_weights(s0t, u_b, m_row, fwd=fwd)
    return _dir_finish(_dot(vt, sdt), den, g, bt_row, m_row, inter)


def _dir_weights(s0t, u_b, m_row, *, fwd):
    n = s0t.shape[0]
    r = lax.broadcasted_iota(jnp.int32, (n, n), 0)
    c = lax.broadcasted_iota(jnp.int32, (n, n), 1)
    a = jnp.where((r <= c) if fwd else (r >= c), u_b, -jnp.inf)
    g = jnp.maximum(jnp.max(a, axis=0, keepdims=True), m_row)
    sdt = s0t * jnp.exp2(a - g)
    return sdt.astype(BF16), jnp.sum(sdt, axis=0, keepdims=True), g


def _dir_finish(ht, den, g, bt_row, m_row, inter=None):
    if inter is not None:
        w_inter = jnp.exp2(m_row - g)
        ht = ht + w_inter * inter[0]
        den = den + w_inter * inter[1]
    floor = jnp.exp2(-(bt_row + g))
    return ht * (1.0 / jnp.maximum(jnp.abs(den), floor))


def _mlstm_ctx_kernel(q_ref, k_ref, v_ref, g_ref, ht_ref, c_ref, n_ref, m_ref):
    n = q_ref.shape[1]
    zero_lanes = jnp.zeros((1, LANES), F32)
    zero_row = jnp.zeros((1, n), F32)
    dirs = ((True, 0), (False, MLSTM_HEADS))
    setup = [_chunk_setup(*_gate_prep(g_ref[sq]), zero_lanes) for sq in range(q_ref.shape[0])]
    units = [(sq, hd) for sq in range(q_ref.shape[0]) for hd in range(MLSTM_HEADS)]
    ks = [k_ref[sq, :, hd * MLSTM_DK:(hd + 1) * MLSTM_DK] for sq, hd in units]
    s0ts = [_dot_nt(k, q_ref[sq, :, hd * MLSTM_DK:(hd + 1) * MLSTM_DK]) for k, (sq, hd) in zip(ks, units)]
    vts = [v_ref[sq, :, hd * MLSTM_DV:(hd + 1) * MLSTM_DV].T for sq, hd in units]
    wts = [[_dir_weights(s0t, _lane_bcast(setup[sq][0], off + hd, n), zero_row, fwd=fwd) for fwd, off in dirs]
           for s0t, (sq, hd) in zip(s0ts, units)]
    hts = [[_dot(vt, sdt) for sdt, _, _ in wt] for vt, wt in zip(vts, wts)]
    for (sq, hd), k, vt, wt, ht2 in zip(units, ks, vts, wts, hts):
        _, bt2, w_t, _, _ = setup[sq]
        ht_ref[sq, hd * MLSTM_DV:(hd + 1) * MLSTM_DV, :] = sum(
            _dir_finish(ht_d, den, g, bt2[off + hd:off + hd + 1, :], zero_row)
            for ht_d, (_, den, g), (_, off) in zip(ht2, wt, dirs))
        for _, off in dirs:
            c_ref[sq, off + hd], n_ref[sq, off + hd:off + hd + 1, :] = _state_update(vt, k, w_t, off + hd)
    for sq in range(q_ref.shape[0]):
        m_ref[sq] = setup[sq][3]


def _mlstm_ctx(mq, mk, mv, gates):
    b, s, _ = mq.shape
    ns = CTX_SEQS_PER_STEP
    seq = lambda c: pl.BlockSpec((ns, s, c), lambda i: (i, 0, 0))
    return pl.pallas_call(
        _mlstm_ctx_kernel,
        grid=(b // ns,),
        in_specs=[seq(MLSTM_HEADS * MLSTM_DK), seq(MLSTM_HEADS * MLSTM_DK), seq(MLSTM_HEADS * MLSTM_DV), seq(LANES)],
        out_specs=[pl.BlockSpec((ns, MLSTM_HEADS * MLSTM_DV, s), lambda i: (i, 0, 0)),
                   pl.BlockSpec((ns, N_DIRHEAD, MLSTM_DV, MLSTM_DK), lambda i: (i, 0, 0, 0)),
                   pl.BlockSpec((ns, N_DIRHEAD, MLSTM_DK), lambda i: (i, 0, 0)),
                   pl.BlockSpec((ns, 1, LANES), lambda i: (i, 0, 0))],
        out_shape=[jax.ShapeDtypeStruct((b, MLSTM_HEADS * MLSTM_DV, s), F32),
                   jax.ShapeDtypeStruct((b, N_DIRHEAD, MLSTM_DV, MLSTM_DK), F32),
                   jax.ShapeDtypeStruct((b, N_DIRHEAD, MLSTM_DK), F32),
                   jax.ShapeDtypeStruct((b, 1, LANES), F32)],
        compiler_params=_params(("parallel",)),
        name="mlstm_ctx",
    )(mq, mk, mv, gates)


class _SeqView:
    def __init__(self, ref, seq_len):
        self.ref, self.seq_len = ref, seq_len
        self.shape = (ref.shape[0] // seq_len, seq_len, ref.shape[1])

    def __getitem__(self, idx):
        sq, rows, cols = idx if isinstance(idx, tuple) else (idx, slice(None), slice(None))
        assert rows == slice(None)
        return self.ref[sq * self.seq_len:(sq + 1) * self.seq_len, cols]


N_CTX_CONSTS = 7


def _ctx_mixer_kernel(*refs, seq_len):
    x_ref, mod_ref = refs[:2]
    consts = refs[2:2 + N_CTX_CONSTS]
    att_ref, ht_ref, c_ref, n_ref, m_ref, ckv_ref, kpet_ref = refs[2 + N_CTX_CONSTS:9 + N_CTX_CONSTS]
    q_s, kc_s, v_s, mq_s, mk_s, mv_s, gate_s = refs[9 + N_CTX_CONSTS:]
    _inproj_kernel(x_ref, mod_ref, *consts, q_s, kc_s, v_s, mq_s, mk_s, mv_s, gate_s, ckv_ref, kpet_ref,
                   rope=False, mod_row0=0, tiles_per_seq=None)
    q3, k3, v3, mq3, mk3, mv3, g3 = (_SeqView(r, seq_len) for r in (q_s, kc_s, v_s, mq_s, mk_s, mv_s, gate_s))
    _attn_kernel(q3, k3, v3, att_ref, has_cache=False, n_cast=0, has_adaln=False)
    _mlstm_ctx_kernel(mq3, mk3, mv3, g3, ht_ref, c_ref, n_ref, m_ref)


def _ctx_mixer(x, mod, gmix, w_in_p, gq, w_uq_p, gkv, w_ukv_p, bg, *, seq_len):
    n_tok = x.shape[0]
    ns, s = CTX_SEQS_PER_STEP, seq_len
    nseq, tm = n_tok // s, ns * s
    dv = MLSTM_HEADS * MLSTM_DV
    consts = (gmix, w_in_p, gq, w_uq_p, gkv, w_ukv_p, bg)
    const_specs = [_const_spec(gmix.shape), _const_spec((SEG_MO[0], w_in_p.shape[1]))]
    const_specs += [_const_spec(a.shape) for a in consts[2:]]
    seq_blk = lambda *dims: pl.BlockSpec((ns,) + dims, lambda i: (i,) + (0,) * len(dims))
    return pl.pallas_call(
        functools.partial(_ctx_mixer_kernel, seq_len=s),
        grid=(nseq // ns,),
        in_specs=[pl.BlockSpec((tm, D_MODEL), lambda i: (i, 0)), pl.BlockSpec(mod.shape, lambda i: (0, 0))]
                 + const_specs,
        out_specs=[seq_blk(s, MLA_HEADS * MLA_V_DIM), seq_blk(dv, s), seq_blk(N_DIRHEAD, MLSTM_DV, MLSTM_DK),
                   seq_blk(N_DIRHEAD, MLSTM_DK), seq_blk(1, LANES),
                   pl.BlockSpec((tm, KV_RANK), lambda i: (i, 0)), seq_blk(ROPE_DIM, s)],
        out_shape=[jax.ShapeDtypeStruct((nseq, s, MLA_HEADS * MLA_V_DIM), BF16),
                   jax.ShapeDtypeStruct((nseq, dv, s), F32),
                   jax.ShapeDtypeStruct((nseq, N_DIRHEAD, MLSTM_DV, MLSTM_DK), F32),
                   jax.ShapeDtypeStruct((nseq, N_DIRHEAD, MLSTM_DK), F32),
                   jax.ShapeDtypeStruct((nseq, 1, LANES), F32),
                   jax.ShapeDtypeStruct((n_tok, KV_RANK), F32),
                   jax.ShapeDtypeStruct((nseq, ROPE_DIM, s), F32)],
        scratch_shapes=[pltpu.VMEM((tm, MLA_HEADS * HEAD_BLOCK), BF16), pltpu.VMEM((tm, MLA_HEADS * HEAD_BLOCK), BF16),
                        pltpu.VMEM((tm, V_WIDTH), BF16), pltpu.VMEM((tm, MLSTM_HEADS * MLSTM_DK), BF16),
                        pltpu.VMEM((tm, MLSTM_HEADS * MLSTM_DK), BF16), pltpu.VMEM((tm, dv), BF16),
                        pltpu.VMEM((tm, LANES), F32)],
        compiler_params=_params(("parallel",)),
        name="ctx_mixer",
    )(x, mod, *consts)


def _mlstm_lat_kernel(qf_ref, kf_ref, vf_ref, gf_ref, qb_ref, kb_ref, vb_ref, gb_ref, c0_ref, n0_ref, m0_ref,
                      ht_ref, c_ref, n_ref, m_ref, *, n_chunks):
    step = pl.program_id(1)

    @pl.when(step == 0)
    def _():
        c_ref[...] = c0_ref[...]
        n_ref[...] = n0_ref[...]
        m_ref[...] = m0_ref[...]
        ht_ref[...] = jnp.zeros_like(ht_ref)

    n = qf_ref.shape[0]
    gate_f, gate_b = _gate_prep(gf_ref[...]), _gate_prep(gb_ref[...])
    is_f = lax.broadcasted_iota(jnp.int32, (n, LANES), 1) < MLSTM_HEADS
    b, u, tot = (jnp.where(is_f[:x.shape[0]], x, y) for x, y in zip(gate_f, gate_b))
    m_prev = m_ref[...]
    u2, bt2, w_t, m_new, decay = _chunk_setup(b, u, tot, m_prev)
    m2_prev = m_prev * LOG2E
    units = [(fwd, hd, hd if fwd else MLSTM_HEADS + hd) for fwd in (True, False) for hd in range(MLSTM_HEADS)]
    refs = {True: (qf_ref, kf_ref, vf_ref), False: (qb_ref, kb_ref, vb_ref)}
    qs = [refs[fwd][0][:, hd * MLSTM_DK:(hd + 1) * MLSTM_DK] for fwd, hd, _ in units]
    ks = [refs[fwd][1][:, hd * MLSTM_DK:(hd + 1) * MLSTM_DK] for fwd, hd, _ in units]
    vts = [refs[fwd][2][:, hd * MLSTM_DV:(hd + 1) * MLSTM_DV].T for fwd, hd, _ in units]
    s0ts = [_dot_nt(k, q) for k, q in zip(ks, qs)]
    c_prevs = [c_ref[kk] for _, _, kk in units]
    n_prevs = [n_ref[kk:kk + 1, :] for _, _, kk in units]
    inters = [(_dot_nt(c.astype(BF16), q), _dot_nt(jnp.broadcast_to(nv, (8, MLSTM_DK)).astype(BF16), q)[0:1, :])
              for c, nv, q in zip(c_prevs, n_prevs, qs)]
    m_rows = [_lane_bcast(m2_prev, kk, n) for _, _, kk in units]
    wts = [_dir_weights(s0t, _lane_bcast(u2, kk, n), m_row, fwd=fwd)
           for s0t, m_row, (fwd, _, kk) in zip(s0ts, m_rows, units)]
    hts = [_dot(vt, sdt) for vt, (sdt, _, _) in zip(vts, wts)]
    for (fwd, hd, kk), k, vt, (_, den, g), ht, inter, m_row, c_prev, n_prev in zip(
            units, ks, vts, wts, hts, inters, m_rows, c_prevs, n_prevs):
        chunk = step if fwd else n_chunks - 1 - step
        ht_ref[chunk, hd * MLSTM_DV:(hd + 1) * MLSTM_DV, :] += _dir_finish(ht, den, g, bt2[kk:kk + 1, :], m_row, inter)
        c_new, n_new = _state_update(vt, k, w_t, kk)
        dk = _lane_bcast(decay, kk)
        c_ref[kk] = dk * c_prev + c_new
        n_ref[kk:kk + 1, :] = dk * n_prev + n_new
    m_ref[...] = m_new


def _mlstm_lat(mq, mk, mv, gates, c0, n0, m0):
    b, s, _ = mq.shape
    lc = MLSTM_CHUNK
    nc = s // lc
    fw = lambda c: pl.BlockSpec((None, lc, c), lambda i, j: (i, j, 0))
    bw = lambda c: pl.BlockSpec((None, lc, c), lambda i, j: (i, nc - 1 - j, 0))
    dk, dv = MLSTM_HEADS * MLSTM_DK, MLSTM_HEADS * MLSTM_DV
    return pl.pallas_call(
        functools.partial(_mlstm_lat_kernel, n_chunks=nc),
        grid=(b, nc),
        in_specs=[fw(dk), fw(dk), fw(dv), fw(LANES), bw(dk), bw(dk), bw(dv), bw(LANES),
                  pl.BlockSpec((None, N_DIRHEAD, MLSTM_DV, MLSTM_DK), lambda i, j: (i, 0, 0, 0)),
                  pl.BlockSpec((None, N_DIRHEAD, MLSTM_DK), lambda i, j: (i, 0, 0)),
                  pl.BlockSpec((None, 1, LANES), lambda i, j: (i, 0, 0))],
        out_specs=pl.BlockSpec((nc, dv, lc), lambda i, j: (i, 0, 0)),
        out_shape=jax.ShapeDtypeStruct((b * nc, dv, lc), F32),
        scratch_shapes=[pltpu.VMEM((N_DIRHEAD, MLSTM_DV, MLSTM_DK), F32),
                        pltpu.VMEM((N_DIRHEAD, MLSTM_DK), F32),
                        pltpu.VMEM((1, LANES), F32)],
        compiler_params=_params(("parallel", "arbitrary")),
        name="mlstm_lat",
    )(mq, mk, mv, gates, mq, mk, mv, gates, c0, n0, m0)


def _tail_kernel(x_ref, modin_ref, mod_ref, att_ref, hst_ref, gmix_ref, wgate_ref, gml_ref, womla_ref, womlstm_ref,
                 wout_ref, gffn_ref, wfin_ref, wfout_ref, gfin_ref, y_ref, *, mod_row0, tiles_per_seq):
    shift_mix, scale_mix = _mod_chunks(modin_ref, mod_row0, tiles_per_seq)
    gate_mix, shift_ffn, scale_ffn, gate_ffn = _mod_chunks(mod_ref, mod_row0, tiles_per_seq)
    gml = gml_ref[...]
    tm = x_ref.shape[0]
    groups = [slice(r, r + TAIL_ROWS) for r in range(0, tm, TAIL_ROWS)]
    dv = MLSTM_HEADS * MLSTM_DV
    hb = [_mixer_input(x_ref[g, :], gmix_ref[...], shift_mix, scale_mix) for g in groups]
    smo = [_sigmoid(_dot_nt(h, wgate_ref[:dv, :])) for h in hb]
    g_a = [_sigmoid(_dot_nt(h, wgate_ref[dv:dv + D_MODEL, :])) for h in hb]
    g_b = [_sigmoid(_dot_nt(h, wgate_ref[dv + D_MODEL:, :])) for h in hb]
    a = [_dot(att_ref[g, :], womla_ref[...]) for g in groups]
    hm = []
    for gi in range(len(groups)):
        hs = hst_ref[gi].T
        parts = []
        for hd in range(MLSTM_HEADS):
            sl = slice(hd * MLSTM_DV, (hd + 1) * MLSTM_DV)
            parts.append((_rms(hs[:, sl], gml[:, sl]) * smo[gi][:, sl]).astype(BF16))
        hm.append(jnp.concatenate(parts, axis=1))
    bm = [_dot(h, womlstm_ref[...]) for h in hm]
    merged = [(ga_i * ai + gb_i * bi).astype(BF16) for ga_i, gb_i, ai, bi in zip(g_a, g_b, a, bm)]
    x1 = [x_ref[g, :] + gate_mix * _dot(mi, wout_ref[...]) for g, mi in zip(groups, merged)]
    h2 = [(_rms(xi, gffn_ref[...]) * (1.0 + scale_ffn) + shift_ffn).astype(BF16) for xi in x1]
    chunks = [[] for _ in groups]
    for c0 in range(0, FFN_HIDDEN, FFN_CHUNK):
        c1 = min(c0 + FFN_CHUNK, FFN_HIDDEN)
        for hi, parts in zip(h2, chunks):
            ga = _dot(hi, wfin_ref[:, c0:c1])
            gu = _dot(hi, wfin_ref[:, FFN_HIDDEN + c0:FFN_HIDDEN + c1])
            parts.append((ga * _sigmoid(ga) * gu).astype(BF16))
    act = [jnp.concatenate(parts, axis=1) for parts in chunks]
    for g, xi, ci in zip(groups, x1, act):
        y_ref[g, :] = _rms(xi + gate_ffn * _dot(ci, wfout_ref[...]), gfin_ref[...])


def _tail(x, mod_in, mod, att, hst, gmix, w_gate, gml, w_o_mla, w_o_mlstm, w_out, gffn, w_ffn_in, w_ffn_out, gfin, *,
          seq_len, latent):
    n_tok = x.shape[0]
    tm = TOKEN_TILE
    assert hst.shape[0] * TAIL_ROWS == n_tok and hst.shape[2] == TAIL_ROWS
    tok = lambda c: pl.BlockSpec((tm, c), lambda i: (i, 0))
    whole = lambda a: pl.BlockSpec(a.shape, lambda i: (0, 0))
    consts = (gmix, w_gate, gml, w_o_mla, w_o_mlstm, w_out, gffn, w_ffn_in, w_ffn_out, gfin)
    return pl.pallas_call(
        functools.partial(_tail_kernel, mod_row0=1 if latent else 0, tiles_per_seq=seq_len // tm if latent else None),
        grid=(n_tok // tm,),
        in_specs=[tok(D_MODEL), whole(mod_in), whole(mod), tok(att.shape[1]),
                  pl.BlockSpec((tm // TAIL_ROWS,) + hst.shape[1:], lambda i: (i, 0, 0))]
                 + [_const_spec(a.shape) for a in consts],
        out_specs=tok(D_MODEL),
        out_shape=jax.ShapeDtypeStruct((n_tok, D_MODEL), F32),
        compiler_params=_params(("parallel",)),
        name="tail_lat" if latent else "tail_ctx",
    )(x, mod_in, mod, att, hst, *consts)


_IN_SIZES = (Q_RANK, KV_RANK, ROPE_DIM, MLSTM_HEADS * MLSTM_DK, MLSTM_HEADS * MLSTM_DK, MLSTM_HEADS * MLSTM_DV,
             4 * MLSTM_HEADS, MLSTM_HEADS * MLSTM_DV, 2 * D_MODEL)
_IN_OFF = tuple(sum(_IN_SIZES[:i]) for i in range(len(_IN_SIZES) + 1))
PACK_ROWS = LANES
_PAD_AFTER = tuple((_SEG_OFF[i + 1] // PACK_ROWS, _SEG_WIDTHS[i] - _IN_SIZES[i], _IN_SIZES[i] % PACK_ROWS)
                   for i in range(len(_IN_SIZES)) if _SEG_WIDTHS[i] != _IN_SIZES[i])


PACK_PIECES = 8
N_PACK_STEPS = pl.cdiv(SEG_MO[0] // PACK_ROWS, PACK_PIECES)
_N_IN = _IN_OFF[-1]


def _pack_src_row(p):
    src = p * PACK_ROWS
    for first_piece_after, pad, _ in _PAD_AFTER:
        src = src - jnp.where(p >= first_piece_after, pad, 0)
    src = jnp.minimum(src, _N_IN - PACK_ROWS)
    return pl.multiple_of(src, math.gcd(PACK_ROWS, _N_IN, *(pad for _, pad, _ in _PAD_AFTER)))


def _pack_in_kernel(*refs):
    w_refs, o_ref = refs[:PACK_PIECES], refs[PACK_PIECES]
    row = lax.broadcasted_iota(jnp.int32, w_refs[0].shape, 0)
    for j, w_ref in enumerate(w_refs):
        p = pl.program_id(0) * PACK_PIECES + j
        valid = jnp.where(p < N_IN_PACKED // PACK_ROWS, PACK_ROWS, 0)
        for first_piece_after, _, width in _PAD_AFTER:
            valid = jnp.where(p == first_piece_after - 1, width, valid)
        o_ref[j * PACK_ROWS:(j + 1) * PACK_ROWS, :] = jnp.where(row < valid, w_ref[...], 0.0).astype(BF16)


def _pack_in(w_in_t):
    n, k = w_in_t.shape
    piece = lambda j: pl.BlockSpec((pl.Element(PACK_ROWS), pl.Element(k)),
                                   lambda i: (_pack_src_row(i * PACK_PIECES + j), 0))
    return pl.pallas_call(
        _pack_in_kernel,
        grid=(N_PACK_STEPS,),
        in_specs=[piece(j) for j in range(PACK_PIECES)],
        out_specs=pl.BlockSpec((PACK_PIECES * PACK_ROWS, k), lambda i: (i, 0)),
        out_shape=jax.ShapeDtypeStruct((N_PACK_STEPS * PACK_PIECES * PACK_ROWS, k), BF16),
        compiler_params=_params(("parallel",)),
        name="pack_in",
    )(*([w_in_t] * PACK_PIECES))


def _pack_small(w_uq, w_ukv, b_gates):
    qd = NOPE_DIM + ROPE_DIM
    w_uq_p = jnp.pad(w_uq.reshape(Q_RANK, MLA_HEADS, qd), ((0, 0), (0, 0), (0, HEAD_BLOCK - qd)))
    w_uq_p = w_uq_p.reshape(Q_RANK, MLA_HEADS * HEAD_BLOCK).astype(BF16)
    kv = w_ukv.reshape(KV_RANK, MLA_HEADS, NOPE_DIM + MLA_V_DIM)
    wk = jnp.pad(kv[:, :, :NOPE_DIM], ((0, 0), (0, 0), (0, HEAD_BLOCK - NOPE_DIM))).reshape(KV_RANK, -1)
    wv = kv[:, :, NOPE_DIM:].reshape(KV_RANK, -1)
    w_ukv_p = jnp.concatenate([wk, wv], axis=1).astype(BF16)
    bg = jnp.pad(b_gates, ((0, 0), (0, LANES - b_gates.shape[1])))
    return w_uq_p, w_ukv_p, bg


def _rope_tables(n_tokens):
    pos = np.arange(n_tokens)
    row = (pos // GRID_W).astype(np.float64)[:, None]
    col = (pos % GRID_W).astype(np.float64)[:, None]
    half = ROPE_DIM // 2
    inv = (np.float32(ROPE_BASE) ** (-np.arange(0, half, 2, dtype=np.float32) / np.float32(half))).astype(np.float64)
    r = np.arange(LANES) - ROPE_LANE0
    in_rope = (r >= 0) & (r < ROPE_DIM)
    rr = np.clip(r, 0, ROPE_DIM - 1)
    freq = inv[rr % (half // 2)][None, :]
    ang = np.where((rr // half == 0)[None, :], row * freq, col * freq).astype(np.float32).astype(np.float64)
    first = (rr % half) < (half // 2)
    cos = np.where(in_rope[None, :], np.cos(ang), 1.0)
    sin = np.sin(ang)
    sin_lo = np.where((in_rope & first)[None, :], -sin, 0.0)
    sin_hi = np.where((in_rope & ~first)[None, :], sin, 0.0)
    return tuple(jnp.asarray(t, dtype=F32) for t in (cos, sin_lo, sin_hi))


def kernel(x_prompt, x_sample, cache_ckv, cache_krope, state_C, state_n, state_m, c, c_ctx, w_mod, b_mod, g_norm_mix,
           w_in, b_gates, g_q_norm, w_uq, g_kv_norm, w_ukv, g_mlstm_norm, w_o_mla, w_o_mlstm, w_out, g_norm_ffn,
           w_ffn_in, w_ffn_out, g_final):
    bp, sp, d = x_prompt.shape
    bs, ss, _ = x_sample.shape
    layer = 0
    assert w_mod.shape[0] == 1 and sp == MLSTM_CHUNK and ss % MLSTM_CHUNK == 0

    adaln_args = (c_ctx[None, :], c, w_mod[layer], b_mod[layer][None, :])
    mod_in = _adaln(*adaln_args)

    w_in_t = w_in[layer].T
    w_in_p = _pack_in(w_in_t)
    w_uq_p, w_ukv_p, bg = _pack_small(w_uq[layer], w_ukv[layer], b_gates[layer][None, :])
    row = lambda g: g[layer][None, :]
    shared_in = (row(g_norm_mix), w_in_p, row(g_q_norm), w_uq_p, row(g_kv_norm), w_ukv_p, bg)
    seq3 = lambda a, b_, s_: a.reshape(b_, s_, a.shape[-1])

    xl = x_sample.reshape(bs * ss, d)
    q, kc, v, mq_l, mk_l, mv_l, gates_l = _inproj(xl, mod_in, _rope_tables(ss), *shared_in, seq_len=ss)
    krope_blk = jnp.pad(cache_krope[:, layer], ((0, 0), (0, 0), (ROPE_LANE0, LANES - ROPE_LANE0 - ROPE_DIM)))
    cache = _cachekv(cache_ckv[:, layer], krope_blk, w_ukv_p)
    gate_cols = (w_in_t, _IN_OFF[7], _IN_OFF[9] - _IN_OFF[7])
    att_l, wgate_b, womla_b, womlstm_b, wout_b, wfin_b, wfout_b, mod_tail = _attention(
        seq3(q, bs, ss), seq3(kc, bs, ss), seq3(v, bs, ss), cache,
        cast_weights=(gate_cols, w_o_mla[layer], w_o_mlstm[layer], w_out[layer], w_ffn_in[layer], w_ffn_out[layer]),
        adaln_tail=adaln_args)
    tail_w = (row(g_norm_mix), wgate_b, row(g_mlstm_norm), womla_b, womlstm_b, wout_b, row(g_norm_ffn), wfin_b, wfout_b,
              g_final[None, :])

    xc = x_prompt.reshape(bp * sp, d)
    att, hs, new_c, new_n, new_m, ckv, kpe = _ctx_mixer(xc, mod_in, *shared_in, seq_len=sp)
    y_prompt = _tail(xc, mod_in, mod_tail, att.reshape(bp * sp, -1), hs, *tail_w,
                     seq_len=sp, latent=False).reshape(bp, sp, d)
    new_ckv = ckv.reshape(bp, 1, sp, KV_RANK)
    new_krope = jnp.swapaxes(kpe, 1, 2).reshape(bp, 1, sp, ROPE_DIM)
    new_C = new_c.reshape(bp, 1, 2, MLSTM_HEADS, MLSTM_DV, MLSTM_DK)
    new_N = new_n.reshape(bp, 1, 2, MLSTM_HEADS, MLSTM_DK)
    new_M = new_m[:, 0, :N_DIRHEAD].reshape(bp, 1, 2, MLSTM_HEADS)

    c0 = state_C[:, layer].reshape(bs, N_DIRHEAD, MLSTM_DV, MLSTM_DK)
    n0 = state_n[:, layer].reshape(bs, N_DIRHEAD, MLSTM_DK)
    m0 = jnp.pad(state_m[:, layer].reshape(bs, 1, N_DIRHEAD), ((0, 0), (0, 0), (0, LANES - N_DIRHEAD)))
    hs = _mlstm_lat(seq3(mq_l, bs, ss), seq3(mk_l, bs, ss), seq3(mv_l, bs, ss), seq3(gates_l, bs, ss), c0, n0, m0)
    y_sample = _tail(xl, mod_in, mod_tail, att_l.reshape(bs * ss, -1), hs, *tail_w,
                     seq_len=ss, latent=True).reshape(bs, ss, d)
    return (y_prompt, y_sample, new_ckv, new_krope, new_C, new_N, new_M)
```

```python
import functools
import math

import jax
import jax.numpy as jnp
import numpy as np
from jax import lax
from jax.experimental import pallas as pl
from jax.experimental.pallas import tpu as pltpu

F32 = jnp.float32
BF16 = jnp.bfloat16

D_MODEL = 1024
GRID_W = 64
MLA_HEADS = 8
Q_RANK = 384
KV_RANK = 256
NOPE_DIM = 64
ROPE_DIM = 32
MLA_V_DIM = 64
ROPE_BASE = 10000.0
MLA_SCALE = (NOPE_DIM + ROPE_DIM) ** -0.5
MLSTM_HEADS = 4
MLSTM_DK = 128
MLSTM_DV = 256
FFN_HIDDEN = ((8 * D_MODEL // 3 + 255) // 256) * 256
EPS = 1e-6

LANES = 128
HEAD_BLOCK = LANES
ROPE_LANE0 = NOPE_DIM
N_DIRHEAD = 2 * MLSTM_HEADS
MLSTM_CHUNK = 256
LOG2E = math.log2(math.e)
Q_PRESCALE = MLA_SCALE * LOG2E
V_SLAB = 2 * LANES
V_WIDTH = (MLA_HEADS // 2) * V_SLAB
TOKEN_TILE = 512
TAIL_ROWS = 256
INPROJ_ROWS = TOKEN_TILE
Q_TILE = 256
CTX_SEQS_PER_STEP = 4
VMEM_LIMIT = 56 * 1024 * 1024

_SEG_WIDTHS = (Q_RANK, KV_RANK, LANES, MLSTM_HEADS * MLSTM_DK, MLSTM_HEADS * MLSTM_DK,
               MLSTM_HEADS * MLSTM_DV, LANES, MLSTM_HEADS * MLSTM_DV, 2 * D_MODEL)
_SEG_OFF = tuple(sum(_SEG_WIDTHS[:i]) for i in range(len(_SEG_WIDTHS) + 1))
SEG_Q, SEG_KV, SEG_KPE, SEG_MQ, SEG_MK, SEG_MV, SEG_GATE, SEG_MO, SEG_BR = (
    (_SEG_OFF[i], _SEG_OFF[i + 1]) for i in range(9))
N_IN_PACKED = _SEG_OFF[-1]


def _dot(a, b):
    return jnp.dot(a, b, preferred_element_type=F32)


def _dot_nt(a, b):
    return lax.dot_general(a, b, (((1,), (1,)), ((), ())), preferred_element_type=F32)


def _dot_tn(a, b):
    return lax.dot_general(a, b, (((0,), (0,)), ((), ())), preferred_element_type=F32)


def _rms(x, g):
    ms = jnp.mean(x * x, axis=-1, keepdims=True)
    return x * lax.rsqrt(ms + EPS) * g


def _sigmoid(x):
    return 1.0 / (1.0 + jnp.exp(-x))


def _const_spec(shape):
    nd = len(shape)
    return pl.BlockSpec(shape, lambda *_: (0,) * nd, pipeline_mode=pl.Buffered(1))


def _mod_chunks(mod_ref, row0, tiles_per_seq):
    row = row0 if tiles_per_seq is None else row0 + pl.program_id(0) // tiles_per_seq
    m = mod_ref[pl.ds(row, 1), :]
    return [m[:, k * D_MODEL:(k + 1) * D_MODEL] for k in range(m.shape[1] // D_MODEL)]


def _params(sem, flags=None, vmem_mib=None):
    limit = VMEM_LIMIT if vmem_mib is None else vmem_mib * 1024 * 1024
    return pltpu.CompilerParams(dimension_semantics=sem, vmem_limit_bytes=limit, flags=flags)


N_MOD_MIXER = 2
N_MOD_TAIL = 4


COND_ROWS = 8


def _adaln_block(cctx_ref, c_ref, w_ref, b_ref, o_ref):
    pad = jnp.zeros((COND_ROWS - 1 - c_ref.shape[0], D_MODEL), F32)
    c = jnp.concatenate([cctx_ref[...], c_ref[...], pad], axis=0)
    s = c * _sigmoid(c)
    o_ref[...] = _dot(s.astype(BF16), w_ref[...].astype(BF16)) + b_ref[...]


def _adaln(c_ctx, c, w_mod, b_mod):
    tn = D_MODEL
    return pl.pallas_call(
        _adaln_block,
        grid=(N_MOD_MIXER,),
        in_specs=[pl.BlockSpec(c_ctx.shape, lambda j: (0, 0)),
                  pl.BlockSpec(c.shape, lambda j: (0, 0)),
                  pl.BlockSpec((D_MODEL, tn), lambda j: (0, j)),
                  pl.BlockSpec((1, tn), lambda j: (0, j))],
        out_specs=pl.BlockSpec((COND_ROWS, tn), lambda j: (0, j)),
        out_shape=jax.ShapeDtypeStruct((COND_ROWS, N_MOD_MIXER * D_MODEL), F32),
        compiler_params=_params(("arbitrary",)),
        name="adaln",
    )(c_ctx, c, w_mod, b_mod)


def _rope_block(x, cos, sin_lo, sin_hi):
    return x * cos + pltpu.roll(x, LANES - 8, 1) * sin_lo + pltpu.roll(x, 8, 1) * sin_hi


def _store_kv(kvn, kp, kc_ref, v_ref, g=slice(None)):
    for hd in range(MLA_HEADS):
        sl = slice(hd * HEAD_BLOCK, (hd + 1) * HEAD_BLOCK)
        kc_ref[g, sl] = (kvn[:, sl] + kp).astype(BF16)
    v0 = MLA_HEADS * HEAD_BLOCK
    ones = jnp.ones((kvn.shape[0], LANES), BF16)
    for pair in range(MLA_HEADS // 2):
        v_ref[g, pair * V_SLAB:pair * V_SLAB + LANES] = kvn[:, v0 + pair * LANES:v0 + (pair + 1) * LANES].astype(BF16)
        v_ref[g, pair * V_SLAB + LANES:(pair + 1) * V_SLAB] = ones


def _mixer_input(x, g, shift, scale):
    return (_rms(x, g) * (1.0 + scale) + shift).astype(BF16)


def _inproj_kernel(*refs, rope, mod_row0, tiles_per_seq):
    if rope:
        (x_ref, mod_ref, cos_ref, slo_ref, shi_ref, gmix_ref, win_ref, gq_ref, wuq_ref, gkv_ref, wukv_ref,
         bg_ref, q_ref, kc_ref, v_ref, mq_ref, mk_ref, mv_ref, gb_ref, gu_ref) = refs
    else:
        (x_ref, mod_ref, gmix_ref, win_ref, gq_ref, wuq_ref, gkv_ref, wukv_ref,
         bg_ref, q_ref, kc_ref, v_ref, mq_ref, mk_ref, mv_ref, gate_ref, ckv_ref, kpet_ref) = refs
    shift, scale = _mod_chunks(mod_ref, mod_row0, tiles_per_seq)
    for r0 in range(0, x_ref.shape[0], INPROJ_ROWS):
        g = slice(r0, r0 + INPROJ_ROWS)
        if rope:
            cos, slo, shi = cos_ref[g, :], slo_ref[g, :], shi_ref[g, :]
        hb = _mixer_input(x_ref[g, :], gmix_ref[...], shift, scale)

        def proj(seg):
            return _dot_nt(hb, win_ref[seg[0]:seg[1], :])

        qn = _rms(proj(SEG_Q), gq_ref[...]).astype(BF16)
        ckv = _rms(proj(SEG_KV), gkv_ref[...])
        zkpe = proj(SEG_KPE)
        gates = proj(SEG_GATE) + bg_ref[...]
        if rope:
            for c0 in range(0, INPROJ_ROWS, MLSTM_CHUNK):
                b, u, _ = _gate_prep(gates[c0:c0 + MLSTM_CHUNK, :])
                gb_ref[r0 + c0:r0 + c0 + MLSTM_CHUNK, :] = b
                gu_ref[r0 + c0:r0 + c0 + MLSTM_CHUNK, :] = u
        else:
            gate_ref[g, :] = gates

        q = _dot(qn, wuq_ref[...])
        for hd in range(MLA_HEADS):
            sl = slice(hd * HEAD_BLOCK, (hd + 1) * HEAD_BLOCK)
            qh = q[:, sl]
            if rope:
                qh = _rope_block(qh, cos, slo, shi)
            q_ref[g, sl] = (qh * Q_PRESCALE).astype(BF16)

        if not rope:
            ckv_ref[g, :] = ckv
            kpe_t = zkpe.T
            seq = kpet_ref.shape[2]
            for sq in range(INPROJ_ROWS // seq):
                kpet_ref[r0 // seq + sq] = kpe_t[:ROPE_DIM, sq * seq:(sq + 1) * seq]
        kp = pltpu.roll(zkpe, ROPE_LANE0, 1)
        if rope:
            kp = _rope_block(kp, cos, slo, shi)
        kvn = _dot(ckv.astype(BF16), wukv_ref[...])
        _store_kv(kvn, kp, kc_ref, v_ref, g)

        mv_ref[g, :] = proj(SEG_MV).astype(BF16)
        mk_ref[g, :] = (proj(SEG_MK) * (MLSTM_DK ** -0.5)).astype(BF16)
        mq_ref[g, :] = proj(SEG_MQ).astype(BF16)


def _inproj(x, mod, rope_tabs, gmix, w_in_p, gq, w_uq_p, gkv, w_ukv_p, bg, *, seq_len):
    n_tok = x.shape[0]
    tm = TOKEN_TILE
    tiles_per_seq = seq_len // tm
    rope = rope_tabs is not None
    tok = lambda c: pl.BlockSpec((tm, c), lambda i: (i, 0))
    in_specs = [tok(D_MODEL), pl.BlockSpec(mod.shape, lambda i: (0, 0))]
    args = [x, mod]
    if rope:
        in_specs += [pl.BlockSpec((tm, LANES), lambda i: (i % tiles_per_seq, 0))] * 3
        args += list(rope_tabs)
    win_rows = SEG_MO[0]
    in_specs += [_const_spec(gmix.shape), _const_spec((win_rows, w_in_p.shape[1]))]
    in_specs += [_const_spec(a.shape) for a in (gq, w_uq_p, gkv, w_ukv_p, bg)]
    args += [gmix, w_in_p, gq, w_uq_p, gkv, w_ukv_p, bg]
    widths = [(MLA_HEADS * HEAD_BLOCK, BF16), (MLA_HEADS * HEAD_BLOCK, BF16), (V_WIDTH, BF16),
              (MLSTM_HEADS * MLSTM_DK, BF16), (MLSTM_HEADS * MLSTM_DK, BF16), (MLSTM_HEADS * MLSTM_DV, BF16),
              (LANES, F32)]
    if rope:
        widths.append((LANES, F32))
    out_specs = [tok(c) for c, _ in widths]
    out_shape = [jax.ShapeDtypeStruct((n_tok, c), dt) for c, dt in widths]
    if not rope:
        out_specs += [tok(KV_RANK), pl.BlockSpec((tm // seq_len, ROPE_DIM, seq_len), lambda i: (i, 0, 0))]
        out_shape += [jax.ShapeDtypeStruct((n_tok, KV_RANK), F32),
                      jax.ShapeDtypeStruct((n_tok // seq_len, ROPE_DIM, seq_len), F32)]
    return pl.pallas_call(
        functools.partial(_inproj_kernel, rope=rope, mod_row0=1 if rope else 0,
                          tiles_per_seq=tiles_per_seq if rope else None),
        grid=(n_tok // tm,),
        in_specs=in_specs,
        out_specs=out_specs,
        out_shape=out_shape,
        compiler_params=_params(("parallel",)),
        name="inproj_lat" if rope else "inproj_ctx",
    )(*args)


def _cachekv_kernel(ckv_ref, kp_ref, wukv_ref, kc_ref, v_ref):
    kvn = _dot(ckv_ref[...].astype(BF16), wukv_ref[...])
    _store_kv(kvn, kp_ref[...], kc_ref, v_ref)


def _cachekv(cache_ckv, krope_blk, w_ukv_p):
    b, past, _ = cache_ckv.shape
    return pl.pallas_call(
        _cachekv_kernel,
        grid=(b,),
        in_specs=[pl.BlockSpec((None, past, KV_RANK), lambda i: (i, 0, 0)),
                  pl.BlockSpec((None, past, LANES), lambda i: (i, 0, 0)),
                  _const_spec(w_ukv_p.shape)],
        out_specs=[pl.BlockSpec((None, past, MLA_HEADS * HEAD_BLOCK), lambda i: (i, 0, 0)),
                   pl.BlockSpec((None, past, V_WIDTH), lambda i: (i, 0, 0))],
        out_shape=[jax.ShapeDtypeStruct((b, past, MLA_HEADS * HEAD_BLOCK), BF16),
                   jax.ShapeDtypeStruct((b, past, V_WIDTH), BF16)],
        compiler_params=_params(("parallel",)),
        name="cachekv",
    )(cache_ckv, krope_blk, w_ukv_p)


def _attn_kernel(*refs, has_cache, n_cast, has_adaln):
    n_in = 5 if has_cache else 3
    if has_cache:
        q_ref, k_ref, v_ref, kc_ref, vc_ref = refs[:n_in]
    else:
        q_ref, k_ref, v_ref = refs[:n_in]
    n_all_in = n_in + n_cast + (4 if has_adaln else 0)
    o_ref = refs[n_all_in]
    for w_ref, wb_ref in zip(refs[n_in:n_in + n_cast], refs[n_all_in + 1:n_all_in + 1 + n_cast]):
        wb_ref[...] = w_ref[...].astype(BF16)
    if has_adaln:
        _adaln_block(*refs[n_in + n_cast:n_all_in], refs[n_all_in + 1 + n_cast])
    tq = q_ref.shape[1]
    lane = lax.broadcasted_iota(jnp.int32, (tq, LANES), 1)
    for sq in range(q_ref.shape[0]):
        hsl = [slice(hd * HEAD_BLOCK, (hd + 1) * HEAD_BLOCK) for hd in range(MLA_HEADS)]
        vsl = [slice(hd // 2 * V_SLAB, (hd // 2 + 1) * V_SLAB) for hd in range(MLA_HEADS)]
        s = [_dot_nt(q_ref[sq, :, sl], k_ref[sq, :, sl]) for sl in hsl]
        m = [jnp.max(si, axis=-1, keepdims=True) for si in s]
        if has_cache:
            sc = [_dot_nt(q_ref[sq, :, sl], kc_ref[sq, :, sl]) for sl in hsl]
            m = [jnp.maximum(mi, jnp.max(ci, axis=-1, keepdims=True)) for mi, ci in zip(m, sc)]
        p = [jnp.exp2(si - mi).astype(BF16) for si, mi in zip(s, m)]
        o = [_dot(pi, v_ref[sq, :, sl]) for pi, sl in zip(p, vsl)]
        if has_cache:
            pc = [jnp.exp2(ci - mi).astype(BF16) for ci, mi in zip(sc, m)]
            o = [oi + _dot(pi, vc_ref[sq, :, sl]) for oi, pi, sl in zip(o, pc, vsl)]
        outs = [oi[:, :LANES] * (1.0 / oi[:, LANES:]) for oi in o]
        for pair in range(MLA_HEADS // 2):
            o_ref[sq, :, pair * LANES:(pair + 1) * LANES] = jnp.where(
                lane < MLA_V_DIM, outs[2 * pair], outs[2 * pair + 1]).astype(BF16)


def _attention(q, k, v, cache=None, cast_weights=(), adaln_tail=None, seqs_per_step=1):
    nb, s, _ = q.shape
    tq = Q_TILE
    ns = seqs_per_step
    b = nb // ns
    n_steps = b * (s // tq)
    kw, vw, ow = MLA_HEADS * HEAD_BLOCK, V_WIDTH, MLA_HEADS * MLA_V_DIM
    in_specs = [pl.BlockSpec((ns, tq, kw), lambda i, j: (i, j, 0)),
                pl.BlockSpec((ns, s, kw), lambda i, j: (i, 0, 0)),
                pl.BlockSpec((ns, s, vw), lambda i, j: (i, 0, 0))]
    args = [q, k, v]
    if cache is not None:
        past = cache[0].shape[1]
        in_specs += [pl.BlockSpec((ns, past, kw), lambda i, j: (i, 0, 0)),
                     pl.BlockSpec((ns, past, vw), lambda i, j: (i, 0, 0))]
        args += list(cache)
    step = lambda i, j: i * (s // tq) + j
    casts = [w if isinstance(w, tuple) else (w, 0, w.shape[0]) for w in cast_weights]
    cast_weights = [w for w, _, _ in casts]

    def cast_in_spec(w, row0, n_rows):
        rows = n_rows // n_steps
        if (row0, n_rows) == (0, w.shape[0]):
            return pl.BlockSpec((rows, w.shape[1]), lambda i, j: (step(i, j), 0))
        return pl.BlockSpec((pl.Element(rows), pl.Element(w.shape[1])),
                            lambda i, j: (pl.multiple_of(row0 + step(i, j) * rows, math.gcd(row0, rows)), 0))

    w_in_specs = [cast_in_spec(*c) for c in casts]
    w_specs = [pl.BlockSpec((n // n_steps, w.shape[1]), lambda i, j: (step(i, j), 0)) for w, _, n in casts]
    extra_in, extra_out, extra_shape = [], [], []
    if adaln_tail is not None:
        tn = N_MOD_TAIL * D_MODEL // n_steps
        col0 = N_MOD_MIXER * D_MODEL // tn
        extra_in = [pl.BlockSpec(adaln_tail[0].shape, lambda i, j: (0, 0)),
                    pl.BlockSpec(adaln_tail[1].shape, lambda i, j: (0, 0)),
                    pl.BlockSpec((D_MODEL, tn), lambda i, j: (0, col0 + step(i, j))),
                    pl.BlockSpec((1, tn), lambda i, j: (0, col0 + step(i, j)))]
        extra_out = [pl.BlockSpec((COND_ROWS, tn), lambda i, j: (0, step(i, j)))]
        extra_shape = [jax.ShapeDtypeStruct((COND_ROWS, N_MOD_TAIL * D_MODEL), F32)]
    outs = pl.pallas_call(
        functools.partial(_attn_kernel, has_cache=cache is not None, n_cast=len(cast_weights),
                          has_adaln=adaln_tail is not None),
        grid=(b, s // tq),
        in_specs=in_specs + w_in_specs + extra_in,
        out_specs=[pl.BlockSpec((ns, tq, ow), lambda i, j: (i, j, 0))] + w_specs + extra_out,
        out_shape=[jax.ShapeDtypeStruct((nb, s, ow), BF16)]
                  + [jax.ShapeDtypeStruct((n, w.shape[1]), BF16) for w, _, n in casts] + extra_shape,
        compiler_params=_params(("parallel", "arbitrary")),
        name="attn_lat" if cache is not None else "attn_ctx",
    )(*args, *cast_weights, *(adaln_tail or ()))
    return outs[0] if len(outs) == 1 else outs


def _lane_bcast(x, k, width=None):
    y = jnp.broadcast_to(x[:, k:k + 1], x.shape)
    reps = (width or LANES) // LANES
    return y if reps == 1 else jnp.concatenate([y] * reps, axis=1)


def _prefix_sum_rows(x):
    n = x.shape[0]
    r = lax.broadcasted_iota(jnp.int32, (n, n), 0)
    c = lax.broadcasted_iota(jnp.int32, (n, n), 1)
    tri = jnp.where(c <= r, 1.0, 0.0).astype(BF16)
    hi = x.astype(BF16)
    r1 = x - hi.astype(F32)
    mid = r1.astype(BF16)
    lo = (r1 - mid.astype(F32)).astype(BF16)
    return _dot(tri, hi) + _dot(tri, mid) + _dot(tri, lo)


def _gate_prep(g):
    n = g.shape[0]
    lane = lax.broadcasted_iota(jnp.int32, (n, LANES), 1)
    fpre = pltpu.roll(g, LANES - N_DIRHEAD, 1)
    lf = jnp.minimum(fpre, 0.0) - jnp.log(1.0 + jnp.exp(-jnp.abs(fpre)))
    binc = _prefix_sum_rows(lf)
    tot = binc[n - 1:n, :]
    b = jnp.where(lane < MLSTM_HEADS, binc, tot - binc + lf)
    return b, g - b, tot


def _chunk_setup(b, u, tot, m_prev):
    g_last = jnp.maximum(jnp.max(u, axis=0, keepdims=True), m_prev)
    u2 = u * LOG2E
    w = jnp.exp2(u2 - g_last * LOG2E)
    return u2, (b * LOG2E).T, w.T, tot + g_last, jnp.exp(m_prev - g_last)


def _state_update(vt, k, wt, kk):
    c_new = _dot((vt * wt[kk:kk + 1, :]).astype(BF16), k)
    n_new = _dot(wt[:N_DIRHEAD, :].astype(BF16), k)[kk:kk + 1, :]
    return c_new, n_new


def _mlstm_dir_t(s0t, vt, u_b, bt_row, m_row, *, fwd, inter=None):
    sdt, den, g = _dir_weights(s0t, u_b, m_row, fwd=fwd)
    return _dir_finish(_dot(vt, sdt), den, g, bt_row, m_row, inter)


def _dir_weights(s0t, u_b, m_row, *, fwd):
    n = s0t.shape[0]
    r = lax.broadcasted_iota(jnp.int32, (n, n), 0)
    c = lax.broadcasted_iota(jnp.int32, (n, n), 1)
    a = jnp.where((r <= c) if fwd else (r >= c), u_b, -jnp.inf)
    g = jnp.maximum(jnp.max(a, axis=0, keepdims=True), m_row)
    sdt = s0t * jnp.exp2(a - g)
    return sdt.astype(BF16), jnp.sum(sdt, axis=0, keepdims=True), g


def _dir_finish(ht, den, g, bt_row, m_row, inter=None):
    if inter is not None:
        w_inter = jnp.exp2(m_row - g)
        ht = ht + w_inter * inter[0]
        den = den + w_inter * inter[1]
    floor = jnp.exp2(-(bt_row + g))
    return ht * (1.0 / jnp.maximum(jnp.abs(den), floor))


def _mlstm_ctx_kernel(q_ref, k_ref, v_ref, g_ref, ht_ref, c_ref, n_ref, m_ref):
    n = q_ref.shape[1]
    zero_lanes = jnp.zeros((1, LANES), F32)
    zero_row = jnp.zeros((1, n), F32)
    dirs = ((True, 0), (False, MLSTM_HEADS))
    setup = [_chunk_setup(*_gate_prep(g_ref[sq]), zero_lanes) for sq in range(q_ref.shape[0])]
    units = [(sq, hd) for sq in range(q_ref.shape[0]) for hd in range(MLSTM_HEADS)]
    ks = [k_ref[sq, :, hd * MLSTM_DK:(hd + 1) * MLSTM_DK] for sq, hd in units]
    s0ts = [_dot_nt(k, q_ref[sq, :, hd * MLSTM_DK:(hd + 1) * MLSTM_DK]) for k, (sq, hd) in zip(ks, units)]
    vts = [v_ref[sq, :, hd * MLSTM_DV:(hd + 1) * MLSTM_DV].T for sq, hd in units]
    wts = [[_dir_weights(s0t, _lane_bcast(setup[sq][0], off + hd, n), zero_row, fwd=fwd) for fwd, off in dirs]
           for s0t, (sq, hd) in zip(s0ts, units)]
    hts = [[_dot(vt, sdt) for sdt, _, _ in wt] for vt, wt in zip(vts, wts)]
    for (sq, hd), k, vt, wt, ht2 in zip(units, ks, vts, wts, hts):
        _, bt2, w_t, _, _ = setup[sq]
        ht_ref[sq, hd * MLSTM_DV:(hd + 1) * MLSTM_DV, :] = sum(
            _dir_finish(ht_d, den, g, bt2[off + hd:off + hd + 1, :], zero_row)
            for ht_d, (_, den, g), (_, off) in zip(ht2, wt, dirs))
        for _, off in dirs:
            c_ref[sq, off + hd], n_ref[sq, off + hd:off + hd + 1, :] = _state_update(vt, k, w_t, off + hd)
    for sq in range(q_ref.shape[0]):
        m_ref[sq] = setup[sq][3]


def _mlstm_ctx(mq, mk, mv, gates):
    b, s, _ = mq.shape
    ns = CTX_SEQS_PER_STEP
    seq = lambda c: pl.BlockSpec((ns, s, c), lambda i: (i, 0, 0))
    return pl.pallas_call(
        _mlstm_ctx_kernel,
        grid=(b // ns,),
        in_specs=[seq(MLSTM_HEADS * MLSTM_DK), seq(MLSTM_HEADS * MLSTM_DK), seq(MLSTM_HEADS * MLSTM_DV), seq(LANES)],
        out_specs=[pl.BlockSpec((ns, MLSTM_HEADS * MLSTM_DV, s), lambda i: (i, 0, 0)),
                   pl.BlockSpec((ns, N_DIRHEAD, MLSTM_DV, MLSTM_DK), lambda i: (i, 0, 0, 0)),
                   pl.BlockSpec((ns, N_DIRHEAD, MLSTM_DK), lambda i: (i, 0, 0)),
                   pl.BlockSpec((ns, 1, LANES), lambda i: (i, 0, 0))],
        out_shape=[jax.ShapeDtypeStruct((b, MLSTM_HEADS * MLSTM_DV, s), F32),
                   jax.ShapeDtypeStruct((b, N_DIRHEAD, MLSTM_DV, MLSTM_DK), F32),
                   jax.ShapeDtypeStruct((b, N_DIRHEAD, MLSTM_DK), F32),
                   jax.ShapeDtypeStruct((b, 1, LANES), F32)],
        compiler_params=_params(("parallel",)),
        name="mlstm_ctx",
    )(mq, mk, mv, gates)


class _SeqView:
    def __init__(self, ref, seq_len):
        self.ref, self.seq_len = ref, seq_len
        self.shape = (ref.shape[0] // seq_len, seq_len, ref.shape[1])

    def __getitem__(self, idx):
        sq, rows, cols = idx if isinstance(idx, tuple) else (idx, slice(None), slice(None))
        assert rows == slice(None)
        return self.ref[sq * self.seq_len:(sq + 1) * self.seq_len, cols]


N_CTX_CONSTS = 7


def _ctx_mixer_kernel(*refs, seq_len):
    x_ref, mod_ref = refs[:2]
    consts = refs[2:2 + N_CTX_CONSTS]
    att_ref, ht_ref, c_ref, n_ref, m_ref, ckv_ref, kpet_ref = refs[2 + N_CTX_CONSTS:9 + N_CTX_CONSTS]
    q_s, kc_s, v_s, mq_s, mk_s, mv_s, gate_s = refs[9 + N_CTX_CONSTS:]
    _inproj_kernel(x_ref, mod_ref, *consts, q_s, kc_s, v_s, mq_s, mk_s, mv_s, gate_s, ckv_ref, kpet_ref,
                   rope=False, mod_row0=0, tiles_per_seq=None)
    q3, k3, v3, mq3, mk3, mv3, g3 = (_SeqView(r, seq_len) for r in (q_s, kc_s, v_s, mq_s, mk_s, mv_s, gate_s))
    _attn_kernel(q3, k3, v3, att_ref, has_cache=False, n_cast=0, has_adaln=False)
    _mlstm_ctx_kernel(mq3, mk3, mv3, g3, ht_ref, c_ref, n_ref, m_ref)


def _ctx_mixer(x, mod, gmix, w_in_p, gq, w_uq_p, gkv, w_ukv_p, bg, *, seq_len):
    n_tok = x.shape[0]
    ns, s = CTX_SEQS_PER_STEP, seq_len
    nseq, tm = n_tok // s, ns * s
    dv = MLSTM_HEADS * MLSTM_DV
    consts = (gmix, w_in_p, gq, w_uq_p, gkv, w_ukv_p, bg)
    const_specs = [_const_spec(gmix.shape), _const_spec((SEG_MO[0], w_in_p.shape[1]))]
    const_specs += [_const_spec(a.shape) for a in consts[2:]]
    seq_blk = lambda *dims: pl.BlockSpec((ns,) + dims, lambda i: (i,) + (0,) * len(dims))
    return pl.pallas_call(
        functools.partial(_ctx_mixer_kernel, seq_len=s),
        grid=(nseq // ns,),
        in_specs=[pl.BlockSpec((tm, D_MODEL), lambda i: (i, 0)), pl.BlockSpec(mod.shape, lambda i: (0, 0))]
                 + const_specs,
        out_specs=[seq_blk(s, MLA_HEADS * MLA_V_DIM), seq_blk(dv, s), seq_blk(N_DIRHEAD, MLSTM_DV, MLSTM_DK),
                   seq_blk(N_DIRHEAD, MLSTM_DK), seq_blk(1, LANES),
                   pl.BlockSpec((tm, KV_RANK), lambda i: (i, 0)), seq_blk(ROPE_DIM, s)],
        out_shape=[jax.ShapeDtypeStruct((nseq, s, MLA_HEADS * MLA_V_DIM), BF16),
                   jax.ShapeDtypeStruct((nseq, dv, s), F32),
                   jax.ShapeDtypeStruct((nseq, N_DIRHEAD, MLSTM_DV, MLSTM_DK), F32),
                   jax.ShapeDtypeStruct((nseq, N_DIRHEAD, MLSTM_DK), F32),
                   jax.ShapeDtypeStruct((nseq, 1, LANES), F32),
                   jax.ShapeDtypeStruct((n_tok, KV_RANK), F32),
                   jax.ShapeDtypeStruct((nseq, ROPE_DIM, s), F32)],
        scratch_shapes=[pltpu.VMEM((tm, MLA_HEADS * HEAD_BLOCK), BF16), pltpu.VMEM((tm, MLA_HEADS * HEAD_BLOCK), BF16),
                        pltpu.VMEM((tm, V_WIDTH), BF16), pltpu.VMEM((tm, MLSTM_HEADS * MLSTM_DK), BF16),
                        pltpu.VMEM((tm, MLSTM_HEADS * MLSTM_DK), BF16), pltpu.VMEM((tm, dv), BF16),
                        pltpu.VMEM((tm, LANES), F32)],
        compiler_params=_params(("parallel",)),
        name="ctx_mixer",
    )(x, mod, *consts)


def _mlstm_lat_kernel(qf_ref, kf_ref, vf_ref, bf_ref, uf_ref, qb_ref, kb_ref, vb_ref, bb_ref, ub_ref,
                      c0_ref, n0_ref, m0_ref,
                      ht_ref, c_ref, n_ref, m_ref, *, n_chunks):
    step = pl.program_id(1)

    @pl.when(step == 0)
    def _():
        c_ref[...] = c0_ref[...]
        n_ref[...] = n0_ref[...]
        m_ref[...] = m0_ref[...]
        ht_ref[...] = jnp.zeros_like(ht_ref)

    n = qf_ref.shape[0]
    is_f = lax.broadcasted_iota(jnp.int32, (n, LANES), 1) < MLSTM_HEADS
    b = jnp.where(is_f, bf_ref[...], bb_ref[...])
    u = jnp.where(is_f, uf_ref[...], ub_ref[...])
    tot = jnp.where(is_f[:1], bf_ref[n - 1:n, :], bb_ref[0:1, :])
    m_prev = m_ref[...]
    u2, bt2, w_t, m_new, decay = _chunk_setup(b, u, tot, m_prev)
    m2_prev = m_prev * LOG2E
    units = [(fwd, hd, hd if fwd else MLSTM_HEADS + hd) for fwd in (True, False) for hd in range(MLSTM_HEADS)]
    refs = {True: (qf_ref, kf_ref, vf_ref), False: (qb_ref, kb_ref, vb_ref)}
    qs = [refs[fwd][0][:, hd * MLSTM_DK:(hd + 1) * MLSTM_DK] for fwd, hd, _ in units]
    ks = [refs[fwd][1][:, hd * MLSTM_DK:(hd + 1) * MLSTM_DK] for fwd, hd, _ in units]
    vts = [refs[fwd][2][:, hd * MLSTM_DV:(hd + 1) * MLSTM_DV].T for fwd, hd, _ in units]
    s0ts = [_dot_nt(k, q) for k, q in zip(ks, qs)]
    c_prevs = [c_ref[kk] for _, _, kk in units]
    n_prevs = [n_ref[kk:kk + 1, :] for _, _, kk in units]
    inters = [(_dot_nt(c.astype(BF16), q), _dot_nt(jnp.broadcast_to(nv, (8, MLSTM_DK)).astype(BF16), q)[0:1, :])
              for c, nv, q in zip(c_prevs, n_prevs, qs)]
    m_rows = [_lane_bcast(m2_prev, kk, n) for _, _, kk in units]
    wts = [_dir_weights(s0t, _lane_bcast(u2, kk, n), m_row, fwd=fwd)
           for s0t, m_row, (fwd, _, kk) in zip(s0ts, m_rows, units)]
    hts = [_dot(vt, sdt) for vt, (sdt, _, _) in zip(vts, wts)]
    for (fwd, hd, kk), k, vt, (_, den, g), ht, inter, m_row, c_prev, n_prev in zip(
            units, ks, vts, wts, hts, inters, m_rows, c_prevs, n_prevs):
        chunk = step if fwd else n_chunks - 1 - step
        ht_ref[chunk, hd * MLSTM_DV:(hd + 1) * MLSTM_DV, :] += _dir_finish(ht, den, g, bt2[kk:kk + 1, :], m_row, inter)
        c_new, n_new = _state_update(vt, k, w_t, kk)
        dk = _lane_bcast(decay, kk)
        c_ref[kk] = dk * c_prev + c_new
        n_ref[kk:kk + 1, :] = dk * n_prev + n_new
    m_ref[...] = m_new


def _mlstm_lat(mq, mk, mv, gate_b, gate_u, c0, n0, m0):
    b, s, _ = mq.shape
    lc = MLSTM_CHUNK
    nc = s // lc
    fw = lambda c: pl.BlockSpec((None, lc, c), lambda i, j: (i, j, 0))
    bw = lambda c: pl.BlockSpec((None, lc, c), lambda i, j: (i, nc - 1 - j, 0))
    dk, dv = MLSTM_HEADS * MLSTM_DK, MLSTM_HEADS * MLSTM_DV
    return pl.pallas_call(
        functools.partial(_mlstm_lat_kernel, n_chunks=nc),
        grid=(b, nc),
        in_specs=[fw(dk), fw(dk), fw(dv), fw(LANES), fw(LANES), bw(dk), bw(dk), bw(dv), bw(LANES), bw(LANES),
                  pl.BlockSpec((None, N_DIRHEAD, MLSTM_DV, MLSTM_DK), lambda i, j: (i, 0, 0, 0)),
                  pl.BlockSpec((None, N_DIRHEAD, MLSTM_DK), lambda i, j: (i, 0, 0)),
                  pl.BlockSpec((None, 1, LANES), lambda i, j: (i, 0, 0))],
        out_specs=pl.BlockSpec((nc, dv, lc), lambda i, j: (i, 0, 0)),
        out_shape=jax.ShapeDtypeStruct((b * nc, dv, lc), F32),
        scratch_shapes=[pltpu.VMEM((N_DIRHEAD, MLSTM_DV, MLSTM_DK), F32),
                        pltpu.VMEM((N_DIRHEAD, MLSTM_DK), F32),
                        pltpu.VMEM((1, LANES), F32)],
        compiler_params=_params(("parallel", "arbitrary")),
        name="mlstm_lat",
    )(mq, mk, mv, gate_b, gate_u, mq, mk, mv, gate_b, gate_u, c0, n0, m0)


def _tail_kernel(x_ref, modin_ref, mod_ref, att_ref, hst_ref, gmix_ref, wgate_ref, gml_ref, womla_ref, womlstm_ref,
                 wout_ref, gffn_ref, wfin_ref, wfout_ref, gfin_ref, y_ref, *, mod_row0, tiles_per_seq):
    shift_mix, scale_mix = _mod_chunks(modin_ref, mod_row0, tiles_per_seq)
    gate_mix, shift_ffn, scale_ffn, gate_ffn = _mod_chunks(mod_ref, mod_row0, tiles_per_seq)
    gml = gml_ref[...]
    tm = x_ref.shape[0]
    groups = [slice(r, r + TAIL_ROWS) for r in range(0, tm, TAIL_ROWS)]
    dv = MLSTM_HEADS * MLSTM_DV
    hb = [_mixer_input(x_ref[g, :], gmix_ref[...], shift_mix, scale_mix) for g in groups]
    smo = [_sigmoid(_dot_nt(h, wgate_ref[:dv, :])) for h in hb]
    g_a = [_sigmoid(_dot_nt(h, wgate_ref[dv:dv + D_MODEL, :])) for h in hb]
    g_b = [_sigmoid(_dot_nt(h, wgate_ref[dv + D_MODEL:, :])) for h in hb]
    a = [_dot(att_ref[g, :], womla_ref[...]) for g in groups]
    hm = []
    for gi in range(len(groups)):
        hs = hst_ref[gi].T
        parts = []
        for hd in range(MLSTM_HEADS):
            sl = slice(hd * MLSTM_DV, (hd + 1) * MLSTM_DV)
            parts.append((_rms(hs[:, sl], gml[:, sl]) * smo[gi][:, sl]).astype(BF16))
        hm.append(jnp.concatenate(parts, axis=1))
    bm = [_dot(h, womlstm_ref[...]) for h in hm]
    merged = [(ga_i * ai + gb_i * bi).astype(BF16) for ga_i, gb_i, ai, bi in zip(g_a, g_b, a, bm)]
    x1 = [x_ref[g, :] + gate_mix * _dot(mi, wout_ref[...]) for g, mi in zip(groups, merged)]
    h2 = [(_rms(xi, gffn_ref[...]) * (1.0 + scale_ffn) + shift_ffn).astype(BF16) for xi in x1]
    ga = [_dot(hi, wfin_ref[:, :FFN_HIDDEN]) for hi in h2]
    gu = [_dot(hi, wfin_ref[:, FFN_HIDDEN:]) for hi in h2]
    act = [(gi * _sigmoid(gi) * ui).astype(BF16) for gi, ui in zip(ga, gu)]
    for g, xi, ci in zip(groups, x1, act):
        y_ref[g, :] = _rms(xi + gate_ffn * _dot(ci, wfout_ref[...]), gfin_ref[...])


def _tail(x, mod_in, mod, att, hst, gmix, w_gate, gml, w_o_mla, w_o_mlstm, w_out, gffn, w_ffn_in, w_ffn_out, gfin, *,
          seq_len, latent):
    n_tok = x.shape[0]
    tm = TOKEN_TILE
    assert hst.shape[0] * TAIL_ROWS == n_tok and hst.shape[2] == TAIL_ROWS
    tok = lambda c: pl.BlockSpec((tm, c), lambda i: (i, 0))
    whole = lambda a: pl.BlockSpec(a.shape, lambda i: (0, 0))
    consts = (gmix, w_gate, gml, w_o_mla, w_o_mlstm, w_out, gffn, w_ffn_in, w_ffn_out, gfin)
    return pl.pallas_call(
        functools.partial(_tail_kernel, mod_row0=1 if latent else 0, tiles_per_seq=seq_len // tm if latent else None),
        grid=(n_tok // tm,),
        in_specs=[tok(D_MODEL), whole(mod_in), whole(mod), tok(att.shape[1]),
                  pl.BlockSpec((tm // TAIL_ROWS,) + hst.shape[1:], lambda i: (i, 0, 0))]
                 + [_const_spec(a.shape) for a in consts],
        out_specs=tok(D_MODEL),
        out_shape=jax.ShapeDtypeStruct((n_tok, D_MODEL), F32),
        compiler_params=_params(("parallel",)),
        name="tail_lat" if latent else "tail_ctx",
    )(x, mod_in, mod, att, hst, *consts)


_IN_SIZES = (Q_RANK, KV_RANK, ROPE_DIM, MLSTM_HEADS * MLSTM_DK, MLSTM_HEADS * MLSTM_DK, MLSTM_HEADS * MLSTM_DV,
             4 * MLSTM_HEADS, MLSTM_HEADS * MLSTM_DV, 2 * D_MODEL)
_IN_OFF = tuple(sum(_IN_SIZES[:i]) for i in range(len(_IN_SIZES) + 1))
PACK_ROWS = LANES
_PAD_AFTER = tuple((_SEG_OFF[i + 1] // PACK_ROWS, _SEG_WIDTHS[i] - _IN_SIZES[i], _IN_SIZES[i] % PACK_ROWS)
                   for i in range(len(_IN_SIZES)) if _SEG_WIDTHS[i] != _IN_SIZES[i])


PACK_PIECES = 8
N_PACK_STEPS = pl.cdiv(SEG_MO[0] // PACK_ROWS, PACK_PIECES)
_N_IN = _IN_OFF[-1]


def _pack_src_row(p):
    src = p * PACK_ROWS
    for first_piece_after, pad, _ in _PAD_AFTER:
        src = src - jnp.where(p >= first_piece_after, pad, 0)
    src = jnp.minimum(src, _N_IN - PACK_ROWS)
    return pl.multiple_of(src, math.gcd(PACK_ROWS, _N_IN, *(pad for _, pad, _ in _PAD_AFTER)))


def _pack_in_kernel(*refs):
    w_refs, o_ref = refs[:PACK_PIECES], refs[PACK_PIECES]
    row = lax.broadcasted_iota(jnp.int32, w_refs[0].shape, 0)
    for j, w_ref in enumerate(w_refs):
        p = pl.program_id(0) * PACK_PIECES + j
        valid = jnp.where(p < N_IN_PACKED // PACK_ROWS, PACK_ROWS, 0)
        for first_piece_after, _, width in _PAD_AFTER:
            valid = jnp.where(p == first_piece_after - 1, width, valid)
        o_ref[j * PACK_ROWS:(j + 1) * PACK_ROWS, :] = jnp.where(row < valid, w_ref[...], 0.0).astype(BF16)


def _pack_in(w_in_t):
    n, k = w_in_t.shape
    piece = lambda j: pl.BlockSpec((pl.Element(PACK_ROWS), pl.Element(k)),
                                   lambda i: (_pack_src_row(i * PACK_PIECES + j), 0))
    return pl.pallas_call(
        _pack_in_kernel,
        grid=(N_PACK_STEPS,),
        in_specs=[piece(j) for j in range(PACK_PIECES)],
        out_specs=pl.BlockSpec((PACK_PIECES * PACK_ROWS, k), lambda i: (i, 0)),
        out_shape=jax.ShapeDtypeStruct((N_PACK_STEPS * PACK_PIECES * PACK_ROWS, k), BF16),
        compiler_params=_params(("parallel",)),
        name="pack_in",
    )(*([w_in_t] * PACK_PIECES))


def _pack_small(w_uq, w_ukv, b_gates):
    qd = NOPE_DIM + ROPE_DIM
    w_uq_p = jnp.pad(w_uq.reshape(Q_RANK, MLA_HEADS, qd), ((0, 0), (0, 0), (0, HEAD_BLOCK - qd)))
    w_uq_p = w_uq_p.reshape(Q_RANK, MLA_HEADS * HEAD_BLOCK).astype(BF16)
    kv = w_ukv.reshape(KV_RANK, MLA_HEADS, NOPE_DIM + MLA_V_DIM)
    wk = jnp.pad(kv[:, :, :NOPE_DIM], ((0, 0), (0, 0), (0, HEAD_BLOCK - NOPE_DIM))).reshape(KV_RANK, -1)
    wv = kv[:, :, NOPE_DIM:].reshape(KV_RANK, -1)
    w_ukv_p = jnp.concatenate([wk, wv], axis=1).astype(BF16)
    bg = jnp.pad(b_gates, ((0, 0), (0, LANES - b_gates.shape[1])))
    return w_uq_p, w_ukv_p, bg


def _rope_tables(n_tokens):
    pos = np.arange(n_tokens)
    row = (pos // GRID_W).astype(np.float64)[:, None]
    col = (pos % GRID_W).astype(np.float64)[:, None]
    half = ROPE_DIM // 2
    inv = (np.float32(ROPE_BASE) ** (-np.arange(0, half, 2, dtype=np.float32) / np.float32(half))).astype(np.float64)
    r = np.arange(LANES) - ROPE_LANE0
    in_rope = (r >= 0) & (r < ROPE_DIM)
    rr = np.clip(r, 0, ROPE_DIM - 1)
    freq = inv[rr % (half // 2)][None, :]
    ang = np.where((rr // half == 0)[None, :], row * freq, col * freq).astype(np.float32).astype(np.float64)
    first = (rr % half) < (half // 2)
    cos = np.where(in_rope[None, :], np.cos(ang), 1.0)
    sin = np.sin(ang)
    sin_lo = np.where((in_rope & first)[None, :], -sin, 0.0)
    sin_hi = np.where((in_rope & ~first)[None, :], sin, 0.0)
    return tuple(jnp.asarray(t, dtype=F32) for t in (cos, sin_lo, sin_hi))


def kernel(x_prompt, x_sample, cache_ckv, cache_krope, state_C, state_n, state_m, c, c_ctx, w_mod, b_mod, g_norm_mix,
           w_in, b_gates, g_q_norm, w_uq, g_kv_norm, w_ukv, g_mlstm_norm, w_o_mla, w_o_mlstm, w_out, g_norm_ffn,
           w_ffn_in, w_ffn_out, g_final):
    bp, sp, d = x_prompt.shape
    bs, ss, _ = x_sample.shape
    layer = 0
    assert w_mod.shape[0] == 1 and sp == MLSTM_CHUNK and ss % MLSTM_CHUNK == 0

    adaln_args = (c_ctx[None, :], c, w_mod[layer], b_mod[layer][None, :])
    mod_in = _adaln(*adaln_args)

    w_in_t = w_in[layer].T
    w_in_p = _pack_in(w_in_t)
    w_uq_p, w_ukv_p, bg = _pack_small(w_uq[layer], w_ukv[layer], b_gates[layer][None, :])
    row = lambda g: g[layer][None, :]
    shared_in = (row(g_norm_mix), w_in_p, row(g_q_norm), w_uq_p, row(g_kv_norm), w_ukv_p, bg)
    seq3 = lambda a, b_, s_: a.reshape(b_, s_, a.shape[-1])

    xl = x_sample.reshape(bs * ss, d)
    q, kc, v, mq_l, mk_l, mv_l, gb_l, gu_l = _inproj(xl, mod_in, _rope_tables(ss), *shared_in, seq_len=ss)
    krope_blk = jnp.pad(cache_krope[:, layer], ((0, 0), (0, 0), (ROPE_LANE0, LANES - ROPE_LANE0 - ROPE_DIM)))
    cache = _cachekv(cache_ckv[:, layer], krope_blk, w_ukv_p)
    gate_cols = (w_in_t, _IN_OFF[7], _IN_OFF[9] - _IN_OFF[7])
    att_l, wgate_b, womla_b, womlstm_b, wout_b, wfin_b, wfout_b, mod_tail = _attention(
        seq3(q, bs, ss), seq3(kc, bs, ss), seq3(v, bs, ss), cache,
        cast_weights=(gate_cols, w_o_mla[layer], w_o_mlstm[layer], w_out[layer], w_ffn_in[layer], w_ffn_out[layer]),
        adaln_tail=adaln_args)
    tail_w = (row(g_norm_mix), wgate_b, row(g_mlstm_norm), womla_b, womlstm_b, wout_b, row(g_norm_ffn), wfin_b, wfout_b,
              g_final[None, :])

    xc = x_prompt.reshape(bp * sp, d)
    att, hs, new_c, new_n, new_m, ckv, kpe = _ctx_mixer(xc, mod_in, *shared_in, seq_len=sp)
    y_prompt = _tail(xc, mod_in, mod_tail, att.reshape(bp * sp, -1), hs, *tail_w,
                     seq_len=sp, latent=False).reshape(bp, sp, d)
    new_ckv = ckv.reshape(bp, 1, sp, KV_RANK)
    new_krope = jnp.swapaxes(kpe, 1, 2).reshape(bp, 1, sp, ROPE_DIM)
    new_C = new_c.reshape(bp, 1, 2, MLSTM_HEADS, MLSTM_DV, MLSTM_DK)
    new_N = new_n.reshape(bp, 1, 2, MLSTM_HEADS, MLSTM_DK)
    new_M = new_m[:, 0, :N_DIRHEAD].reshape(bp, 1, 2, MLSTM_HEADS)

    c0 = state_C[:, layer].reshape(bs, N_DIRHEAD, MLSTM_DV, MLSTM_DK)
    n0 = state_n[:, layer].reshape(bs, N_DIRHEAD, MLSTM_DK)
    m0 = jnp.pad(state_m[:, layer].reshape(bs, 1, N_DIRHEAD), ((0, 0), (0, 0), (0, LANES - N_DIRHEAD)))
    hs = _mlstm_lat(seq3(mq_l, bs, ss), seq3(mk_l, bs, ss), seq3(mv_l, bs, ss), seq3(gb_l, bs, ss), seq3(gu_l, bs, ss),
                    c0, n0, m0)
    y_sample = _tail(xl, mod_in, mod_tail, att_l.reshape(bs * ss, -1), hs, *tail_w,
                     seq_len=ss, latent=True).reshape(bs, ss, d)
    return (y_prompt, y_sample, new_ckv, new_krope, new_C, new_N, new_M)
```

```python
import functools
import math

import jax
import jax.numpy as jnp
import numpy as np
from jax import lax
from jax.experimental import pallas as pl
from jax.experimental.pallas import tpu as pltpu

F32 = jnp.float32
BF16 = jnp.bfloat16

D_MODEL = 1024
GRID_W = 64
MLA_HEADS = 8
Q_RANK = 384
KV_RANK = 256
NOPE_DIM = 64
ROPE_DIM = 32
MLA_V_DIM = 64
ROPE_BASE = 10000.0
MLA_SCALE = (NOPE_DIM + ROPE_DIM) ** -0.5
MLSTM_HEADS = 4
MLSTM_DK = 128
MLSTM_DV = 256
FFN_HIDDEN = ((8 * D_MODEL // 3 + 255) // 256) * 256
EPS = 1e-6

LANES = 128
HEAD_BLOCK = LANES
ROPE_LANE0 = NOPE_DIM
N_DIRHEAD = 2 * MLSTM_HEADS
MLSTM_CHUNK = 256
LOG2E = math.log2(math.e)
Q_PRESCALE = MLA_SCALE * LOG2E
V_SLAB = 2 * LANES
V_WIDTH = (MLA_HEADS // 2) * V_SLAB
TOKEN_TILE = 512
TAIL_ROWS = 256
INPROJ_ROWS = TOKEN_TILE
Q_TILE = 256
CTX_SEQS_PER_STEP = 4
VMEM_LIMIT = 56 * 1024 * 1024

_SEG_WIDTHS = (Q_RANK, KV_RANK, LANES, MLSTM_HEADS * MLSTM_DK, MLSTM_HEADS * MLSTM_DK,
               MLSTM_HEADS * MLSTM_DV, LANES, MLSTM_HEADS * MLSTM_DV, 2 * D_MODEL)
_SEG_OFF = tuple(sum(_SEG_WIDTHS[:i]) for i in range(len(_SEG_WIDTHS) + 1))
SEG_Q, SEG_KV, SEG_KPE, SEG_MQ, SEG_MK, SEG_MV, SEG_GATE, SEG_MO, SEG_BR = (
    (_SEG_OFF[i], _SEG_OFF[i + 1]) for i in range(9))
N_IN_PACKED = _SEG_OFF[-1]


def _dot(a, b):
    return jnp.dot(a, b, preferred_element_type=F32)


def _dot_nt(a, b):
    return lax.dot_general(a, b, (((1,), (1,)), ((), ())), preferred_element_type=F32)


def _dot_tn(a, b):
    return lax.dot_general(a, b, (((0,), (0,)), ((), ())), preferred_element_type=F32)


def _rms(x, g):
    ms = jnp.mean(x * x, axis=-1, keepdims=True)
    return x * lax.rsqrt(ms + EPS) * g


def _sigmoid(x):
    return 1.0 / (1.0 + jnp.exp(-x))


def _const_spec(shape):
    nd = len(shape)
    return pl.BlockSpec(shape, lambda *_: (0,) * nd, pipeline_mode=pl.Buffered(1))


def _mod_chunks(mod_ref, row0, tiles_per_seq):
    row = row0 if tiles_per_seq is None else row0 + pl.program_id(0) // tiles_per_seq
    m = mod_ref[pl.ds(row, 1), :]
    return [m[:, k * D_MODEL:(k + 1) * D_MODEL] for k in range(m.shape[1] // D_MODEL)]


def _params(sem, flags=None, vmem_mib=None):
    limit = VMEM_LIMIT if vmem_mib is None else vmem_mib * 1024 * 1024
    return pltpu.CompilerParams(dimension_semantics=sem, vmem_limit_bytes=limit, flags=flags)


N_MOD_MIXER = 2
N_MOD_TAIL = 4


COND_ROWS = 8


def _adaln_block(cctx_ref, c_ref, w_ref, b_ref, o_ref):
    pad = jnp.zeros((COND_ROWS - 1 - c_ref.shape[0], D_MODEL), F32)
    c = jnp.concatenate([cctx_ref[...], c_ref[...], pad], axis=0)
    s = c * _sigmoid(c)
    o_ref[...] = _dot(s.astype(BF16), w_ref[...].astype(BF16)) + b_ref[...]


def _adaln(c_ctx, c, w_mod, b_mod):
    tn = D_MODEL
    return pl.pallas_call(
        _adaln_block,
        grid=(N_MOD_MIXER,),
        in_specs=[pl.BlockSpec(c_ctx.shape, lambda j: (0, 0)),
                  pl.BlockSpec(c.shape, lambda j: (0, 0)),
                  pl.BlockSpec((D_MODEL, tn), lambda j: (0, j)),
                  pl.BlockSpec((1, tn), lambda j: (0, j))],
        out_specs=pl.BlockSpec((COND_ROWS, tn), lambda j: (0, j)),
        out_shape=jax.ShapeDtypeStruct((COND_ROWS, N_MOD_MIXER * D_MODEL), F32),
        compiler_params=_params(("arbitrary",)),
        name="adaln",
    )(c_ctx, c, w_mod, b_mod)


def _rope_block(x, cos, sin_lo, sin_hi):
    return x * cos + pltpu.roll(x, LANES - 8, 1) * sin_lo + pltpu.roll(x, 8, 1) * sin_hi


def _store_kv(kvn, kp, kc_ref, v_ref, g=slice(None)):
    for hd in range(MLA_HEADS):
        sl = slice(hd * HEAD_BLOCK, (hd + 1) * HEAD_BLOCK)
        kc_ref[g, sl] = (kvn[:, sl] + kp).astype(BF16)
    v0 = MLA_HEADS * HEAD_BLOCK
    ones = jnp.ones((kvn.shape[0], LANES), BF16)
    for pair in range(MLA_HEADS // 2):
        v_ref[g, pair * V_SLAB:pair * V_SLAB + LANES] = kvn[:, v0 + pair * LANES:v0 + (pair + 1) * LANES].astype(BF16)
        v_ref[g, pair * V_SLAB + LANES:(pair + 1) * V_SLAB] = ones


def _mixer_input(x, g, shift, scale):
    return (_rms(x, g) * (1.0 + scale) + shift).astype(BF16)


def _inproj_kernel(*refs, rope, mod_row0, tiles_per_seq):
    if rope:
        (x_ref, mod_ref, cos_ref, slo_ref, shi_ref, gmix_ref, win_ref, gq_ref, wuq_ref, gkv_ref, wukv_ref,
         bg_ref, q_ref, kc_ref, v_ref, mq_ref, mk_ref, mv_ref, gb_ref, gu_ref) = refs
    else:
        (x_ref, mod_ref, gmix_ref, win_ref, gq_ref, wuq_ref, gkv_ref, wukv_ref,
         bg_ref, q_ref, kc_ref, v_ref, mq_ref, mk_ref, mv_ref, gate_ref, ckv_ref, kpet_ref) = refs
    shift, scale = _mod_chunks(mod_ref, mod_row0, tiles_per_seq)
    for r0 in range(0, x_ref.shape[0], INPROJ_ROWS):
        g = slice(r0, r0 + INPROJ_ROWS)
        if rope:
            cos, slo, shi = cos_ref[g, :], slo_ref[g, :], shi_ref[g, :]
        hb = _mixer_input(x_ref[g, :], gmix_ref[...], shift, scale)

        def proj(seg):
            return _dot_nt(hb, win_ref[seg[0]:seg[1], :])

        qn = _rms(proj(SEG_Q), gq_ref[...]).astype(BF16)
        ckv = _rms(proj(SEG_KV), gkv_ref[...])
        zkpe = proj(SEG_KPE)
        gates = proj(SEG_GATE) + bg_ref[...]
        if rope:
            for c0 in range(0, INPROJ_ROWS, MLSTM_CHUNK):
                b, u, _ = _gate_prep(gates[c0:c0 + MLSTM_CHUNK, :])
                gb_ref[r0 + c0:r0 + c0 + MLSTM_CHUNK, :] = b
                gu_ref[r0 + c0:r0 + c0 + MLSTM_CHUNK, :] = u
        else:
            gate_ref[g, :] = gates

        q = _dot(qn, wuq_ref[...])
        for hd in range(MLA_HEADS):
            sl = slice(hd * HEAD_BLOCK, (hd + 1) * HEAD_BLOCK)
            qh = q[:, sl]
            if rope:
                qh = _rope_block(qh, cos, slo, shi)
            q_ref[g, sl] = (qh * Q_PRESCALE).astype(BF16)

        if not rope:
            ckv_ref[g, :] = ckv
            kpe_t = zkpe.T
            seq = kpet_ref.shape[2]
            for sq in range(INPROJ_ROWS // seq):
                kpet_ref[r0 // seq + sq] = kpe_t[:ROPE_DIM, sq * seq:(sq + 1) * seq]
        kp = pltpu.roll(zkpe, ROPE_LANE0, 1)
        if rope:
            kp = _rope_block(kp, cos, slo, shi)
        kvn = _dot(ckv.astype(BF16), wukv_ref[...])
        _store_kv(kvn, kp, kc_ref, v_ref, g)

        mv_ref[g, :] = proj(SEG_MV).astype(BF16)
        mk_ref[g, :] = (proj(SEG_MK) * (MLSTM_DK ** -0.5)).astype(BF16)
        mq_ref[g, :] = proj(SEG_MQ).astype(BF16)


def _inproj(x, mod, rope_tabs, gmix, w_in_p, gq, w_uq_p, gkv, w_ukv_p, bg, *, seq_len):
    n_tok = x.shape[0]
    tm = TOKEN_TILE
    tiles_per_seq = seq_len // tm
    rope = rope_tabs is not None
    tok = lambda c: pl.BlockSpec((tm, c), lambda i: (i, 0))
    in_specs = [tok(D_MODEL), pl.BlockSpec(mod.shape, lambda i: (0, 0))]
    args = [x, mod]
    if rope:
        in_specs += [pl.BlockSpec((tm, LANES), lambda i: (i % tiles_per_seq, 0))] * 3
        args += list(rope_tabs)
    win_rows = SEG_MO[0]
    in_specs += [_const_spec(gmix.shape), _const_spec((win_rows, w_in_p.shape[1]))]
    in_specs += [_const_spec(a.shape) for a in (gq, w_uq_p, gkv, w_ukv_p, bg)]
    args += [gmix, w_in_p, gq, w_uq_p, gkv, w_ukv_p, bg]
    widths = [(MLA_HEADS * HEAD_BLOCK, BF16), (MLA_HEADS * HEAD_BLOCK, BF16), (V_WIDTH, BF16),
              (MLSTM_HEADS * MLSTM_DK, BF16), (MLSTM_HEADS * MLSTM_DK, BF16), (MLSTM_HEADS * MLSTM_DV, BF16),
              (LANES, F32)]
    if rope:
        widths.append((LANES, F32))
    out_specs = [tok(c) for c, _ in widths]
    out_shape = [jax.ShapeDtypeStruct((n_tok, c), dt) for c, dt in widths]
    if not rope:
        out_specs += [tok(KV_RANK), pl.BlockSpec((tm // seq_len, ROPE_DIM, seq_len), lambda i: (i, 0, 0))]
        out_shape += [jax.ShapeDtypeStruct((n_tok, KV_RANK), F32),
                      jax.ShapeDtypeStruct((n_tok // seq_len, ROPE_DIM, seq_len), F32)]
    return pl.pallas_call(
        functools.partial(_inproj_kernel, rope=rope, mod_row0=1 if rope else 0,
                          tiles_per_seq=tiles_per_seq if rope else None),
        grid=(n_tok // tm,),
        in_specs=in_specs,
        out_specs=out_specs,
        out_shape=out_shape,
        compiler_params=_params(("parallel",)),
        name="inproj_lat" if rope else "inproj_ctx",
    )(*args)


def _cachekv_kernel(ckv_ref, kp_ref, wukv_ref, kc_ref, v_ref):
    kvn = _dot(ckv_ref[...].astype(BF16), wukv_ref[...])
    _store_kv(kvn, kp_ref[...], kc_ref, v_ref)


def _cachekv(cache_ckv, krope_blk, w_ukv_p):
    b, past, _ = cache_ckv.shape
    return pl.pallas_call(
        _cachekv_kernel,
        grid=(b,),
        in_specs=[pl.BlockSpec((None, past, KV_RANK), lambda i: (i, 0, 0)),
                  pl.BlockSpec((None, past, LANES), lambda i: (i, 0, 0)),
                  _const_spec(w_ukv_p.shape)],
        out_specs=[pl.BlockSpec((None, past, MLA_HEADS * HEAD_BLOCK), lambda i: (i, 0, 0)),
                   pl.BlockSpec((None, past, V_WIDTH), lambda i: (i, 0, 0))],
        out_shape=[jax.ShapeDtypeStruct((b, past, MLA_HEADS * HEAD_BLOCK), BF16),
                   jax.ShapeDtypeStruct((b, past, V_WIDTH), BF16)],
        compiler_params=_params(("parallel",)),
        name="cachekv",
    )(cache_ckv, krope_blk, w_ukv_p)


def _attn_kernel(*refs, has_cache, n_cast, has_adaln):
    n_in = 5 if has_cache else 3
    if has_cache:
        q_ref, k_ref, v_ref, kc_ref, vc_ref = refs[:n_in]
    else:
        q_ref, k_ref, v_ref = refs[:n_in]
    n_all_in = n_in + n_cast + (4 if has_adaln else 0)
    o_ref = refs[n_all_in]
    for w_ref, wb_ref in zip(refs[n_in:n_in + n_cast], refs[n_all_in + 1:n_all_in + 1 + n_cast]):
        wb_ref[...] = w_ref[...].astype(BF16)
    if has_adaln:
        _adaln_block(*refs[n_in + n_cast:n_all_in], refs[n_all_in + 1 + n_cast])
    tq = q_ref.shape[1]
    lane = lax.broadcasted_iota(jnp.int32, (tq, LANES), 1)
    for sq in range(q_ref.shape[0]):
        hsl = [slice(hd * HEAD_BLOCK, (hd + 1) * HEAD_BLOCK) for hd in range(MLA_HEADS)]
        vsl = [slice(hd // 2 * V_SLAB, (hd // 2 + 1) * V_SLAB) for hd in range(MLA_HEADS)]
        s = [_dot_nt(q_ref[sq, :, sl], k_ref[sq, :, sl]) for sl in hsl]
        m = [jnp.max(si, axis=-1, keepdims=True) for si in s]
        if has_cache:
            sc = [_dot_nt(q_ref[sq, :, sl], kc_ref[sq, :, sl]) for sl in hsl]
            m = [jnp.maximum(mi, jnp.max(ci, axis=-1, keepdims=True)) for mi, ci in zip(m, sc)]
        p = [jnp.exp2(si - mi).astype(BF16) for si, mi in zip(s, m)]
        o = [_dot(pi, v_ref[sq, :, sl]) for pi, sl in zip(p, vsl)]
        if has_cache:
            pc = [jnp.exp2(ci - mi).astype(BF16) for ci, mi in zip(sc, m)]
            o = [oi + _dot(pi, vc_ref[sq, :, sl]) for oi, pi, sl in zip(o, pc, vsl)]
        outs = [oi[:, :LANES] * (1.0 / oi[:, LANES:]) for oi in o]
        for pair in range(MLA_HEADS // 2):
            o_ref[sq, :, pair * LANES:(pair + 1) * LANES] = jnp.where(
                lane < MLA_V_DIM, outs[2 * pair], outs[2 * pair + 1]).astype(BF16)


def _attention(q, k, v, cache=None, cast_weights=(), adaln_tail=None, seqs_per_step=1):
    nb, s, _ = q.shape
    tq = Q_TILE
    ns = seqs_per_step
    b = nb // ns
    n_steps = b * (s // tq)
    kw, vw, ow = MLA_HEADS * HEAD_BLOCK, V_WIDTH, MLA_HEADS * MLA_V_DIM
    in_specs = [pl.BlockSpec((ns, tq, kw), lambda i, j: (i, j, 0)),
                pl.BlockSpec((ns, s, kw), lambda i, j: (i, 0, 0)),
                pl.BlockSpec((ns, s, vw), lambda i, j: (i, 0, 0))]
    args = [q, k, v]
    if cache is not None:
        past = cache[0].shape[1]
        in_specs += [pl.BlockSpec((ns, past, kw), lambda i, j: (i, 0, 0)),
                     pl.BlockSpec((ns, past, vw), lambda i, j: (i, 0, 0))]
        args += list(cache)
    step = lambda i, j: i * (s // tq) + j
    casts = [w if isinstance(w, tuple) else (w, 0, w.shape[0]) for w in cast_weights]
    cast_weights = [w for w, _, _ in casts]

    def cast_in_spec(w, row0, n_rows):
        rows = n_rows // n_steps
        if (row0, n_rows) == (0, w.shape[0]):
            return pl.BlockSpec((rows, w.shape[1]), lambda i, j: (step(i, j), 0))
        return pl.BlockSpec((pl.Element(rows), pl.Element(w.shape[1])),
                            lambda i, j: (pl.multiple_of(row0 + step(i, j) * rows, math.gcd(row0, rows)), 0))

    w_in_specs = [cast_in_spec(*c) for c in casts]
    w_specs = [pl.BlockSpec((n // n_steps, w.shape[1]), lambda i, j: (step(i, j), 0)) for w, _, n in casts]
    extra_in, extra_out, extra_shape = [], [], []
    if adaln_tail is not None:
        tn = N_MOD_TAIL * D_MODEL // n_steps
        col0 = N_MOD_MIXER * D_MODEL // tn
        extra_in = [pl.BlockSpec(adaln_tail[0].shape, lambda i, j: (0, 0)),
                    pl.BlockSpec(adaln_tail[1].shape, lambda i, j: (0, 0)),
                    pl.BlockSpec((D_MODEL, tn), lambda i, j: (0, col0 + step(i, j))),
                    pl.BlockSpec((1, tn), lambda i, j: (0, col0 + step(i, j)))]
        extra_out = [pl.BlockSpec((COND_ROWS, tn), lambda i, j: (0, step(i, j)))]
        extra_shape = [jax.ShapeDtypeStruct((COND_ROWS, N_MOD_TAIL * D_MODEL), F32)]
    outs = pl.pallas_call(
        functools.partial(_attn_kernel, has_cache=cache is not None, n_cast=len(cast_weights),
                          has_adaln=adaln_tail is not None),
        grid=(b, s // tq),
        in_specs=in_specs + w_in_specs + extra_in,
        out_specs=[pl.BlockSpec((ns, tq, ow), lambda i, j: (i, j, 0))] + w_specs + extra_out,
        out_shape=[jax.ShapeDtypeStruct((nb, s, ow), BF16)]
                  + [jax.ShapeDtypeStruct((n, w.shape[1]), BF16) for w, _, n in casts] + extra_shape,
        compiler_params=_params(("parallel", "arbitrary")),
        name="attn_lat" if cache is not None else "attn_ctx",
    )(*args, *cast_weights, *(adaln_tail or ()))
    return outs[0] if len(outs) == 1 else outs


def _lane_bcast(x, k, width=None):
    y = jnp.broadcast_to(x[:, k:k + 1], x.shape)
    reps = (width or LANES) // LANES
    return y if reps == 1 else jnp.concatenate([y] * reps, axis=1)


def _prefix_sum_rows(x):
    n = x.shape[0]
    row = lax.broadcasted_iota(jnp.int32, x.shape, 0)
    shift = 1
    while shift < n:
        x = x + jnp.where(row >= shift, pltpu.roll(x, shift, 0), 0.0)
        shift *= 2
    return x


def _gate_prep(g):
    n = g.shape[0]
    lane = lax.broadcasted_iota(jnp.int32, (n, LANES), 1)
    fpre = pltpu.roll(g, LANES - N_DIRHEAD, 1)
    lf = jnp.minimum(fpre, 0.0) - jnp.log(1.0 + jnp.exp(-jnp.abs(fpre)))
    binc = _prefix_sum_rows(lf)
    tot = binc[n - 1:n, :]
    b = jnp.where(lane < MLSTM_HEADS, binc, tot - binc + lf)
    return b, g - b, tot


def _chunk_setup(b, u, tot, m_prev):
    g_last = jnp.maximum(jnp.max(u, axis=0, keepdims=True), m_prev)
    u2 = u * LOG2E
    w = jnp.exp2(u2 - g_last * LOG2E)
    return u2, (b * LOG2E).T, w.T, tot + g_last, jnp.exp(m_prev - g_last)


def _state_update(vt, k, wt, kk):
    c_new = _dot((vt * wt[kk:kk + 1, :]).astype(BF16), k)
    n_new = _dot(wt[:N_DIRHEAD, :].astype(BF16), k)[kk:kk + 1, :]
    return c_new, n_new


def _mlstm_dir_t(s0t, vt, u_b, bt_row, m_row, *, fwd, inter=None):
    sdt, den, g = _dir_weights(s0t, u_b, m_row, fwd=fwd)
    return _dir_finish(_dot(vt, sdt), den, g, bt_row, m_row, inter)


def _dir_weights(s0t, u_b, m_row, *, fwd):
    n = s0t.shape[0]
    r = lax.broadcasted_iota(jnp.int32, (n, n), 0)
    c = lax.broadcasted_iota(jnp.int32, (n, n), 1)
    a = jnp.where((r <= c) if fwd else (r >= c), u_b, -jnp.inf)
    g = jnp.maximum(jnp.max(a, axis=0, keepdims=True), m_row)
    sdt = s0t * jnp.exp2(a - g)
    return sdt.astype(BF16), jnp.sum(sdt, axis=0, keepdims=True), g


def _dir_finish(ht, den, g, bt_row, m_row, inter=None):
    if inter is not None:
        w_inter = jnp.exp2(m_row - g)
        ht = ht + w_inter * inter[0]
        den = den + w_inter * inter[1]
    floor = jnp.exp2(-(bt_row + g))
    return ht * (1.0 / jnp.maximum(jnp.abs(den), floor))


def _mlstm_ctx_kernel(q_ref, k_ref, v_ref, g_ref, ht_ref, c_ref, n_ref, m_ref):
    n = q_ref.shape[1]
    zero_lanes = jnp.zeros((1, LANES), F32)
    zero_row = jnp.zeros((1, n), F32)
    dirs = ((True, 0), (False, MLSTM_HEADS))
    setup = [_chunk_setup(*_gate_prep(g_ref[sq]), zero_lanes) for sq in range(q_ref.shape[0])]
    units = [(sq, hd) for sq in range(q_ref.shape[0]) for hd in range(MLSTM_HEADS)]
    ks = [k_ref[sq, :, hd * MLSTM_DK:(hd + 1) * MLSTM_DK] for sq, hd in units]
    s0ts = [_dot_nt(k, q_ref[sq, :, hd * MLSTM_DK:(hd + 1) * MLSTM_DK]) for k, (sq, hd) in zip(ks, units)]
    vts = [v_ref[sq, :, hd * MLSTM_DV:(hd + 1) * MLSTM_DV].T for sq, hd in units]
    wts = [[_dir_weights(s0t, _lane_bcast(setup[sq][0], off + hd, n), zero_row, fwd=fwd) for fwd, off in dirs]
           for s0t, (sq, hd) in zip(s0ts, units)]
    hts = [[_dot(vt, sdt) for sdt, _, _ in wt] for vt, wt in zip(vts, wts)]
    for (sq, hd), k, vt, wt, ht2 in zip(units, ks, vts, wts, hts):
        _, bt2, w_t, _, _ = setup[sq]
        ht_ref[sq, hd * MLSTM_DV:(hd + 1) * MLSTM_DV, :] = sum(
            _dir_finish(ht_d, den, g, bt2[off + hd:off + hd + 1, :], zero_row)
            for ht_d, (_, den, g), (_, off) in zip(ht2, wt, dirs))
        for _, off in dirs:
            c_ref[sq, off + hd], n_ref[sq, off + hd:off + hd + 1, :] = _state_update(vt, k, w_t, off + hd)
    for sq in range(q_ref.shape[0]):
        m_ref[sq] = setup[sq][3]


def _mlstm_ctx(mq, mk, mv, gates):
    b, s, _ = mq.shape
    ns = CTX_SEQS_PER_STEP
    seq = lambda c: pl.BlockSpec((ns, s, c), lambda i: (i, 0, 0))
    return pl.pallas_call(
        _mlstm_ctx_kernel,
        grid=(b // ns,),
        in_specs=[seq(MLSTM_HEADS * MLSTM_DK), seq(MLSTM_HEADS * MLSTM_DK), seq(MLSTM_HEADS * MLSTM_DV), seq(LANES)],
        out_specs=[pl.BlockSpec((ns, MLSTM_HEADS * MLSTM_DV, s), lambda i: (i, 0, 0)),
                   pl.BlockSpec((ns, N_DIRHEAD, MLSTM_DV, MLSTM_DK), lambda i: (i, 0, 0, 0)),
                   pl.BlockSpec((ns, N_DIRHEAD, MLSTM_DK), lambda i: (i, 0, 0)),
                   pl.BlockSpec((ns, 1, LANES), lambda i: (i, 0, 0))],
        out_shape=[jax.ShapeDtypeStruct((b, MLSTM_HEADS * MLSTM_DV, s), F32),
                   jax.ShapeDtypeStruct((b, N_DIRHEAD, MLSTM_DV, MLSTM_DK), F32),
                   jax.ShapeDtypeStruct((b, N_DIRHEAD, MLSTM_DK), F32),
                   jax.ShapeDtypeStruct((b, 1, LANES), F32)],
        compiler_params=_params(("parallel",)),
        name="mlstm_ctx",
    )(mq, mk, mv, gates)


class _SeqView:
    def __init__(self, ref, seq_len):
        self.ref, self.seq_len = ref, seq_len
        self.shape = (ref.shape[0] // seq_len, seq_len, ref.shape[1])

    def __getitem__(self, idx):
        sq, rows, cols = idx if isinstance(idx, tuple) else (idx, slice(None), slice(None))
        assert rows == slice(None)
        return self.ref[sq * self.seq_len:(sq + 1) * self.seq_len, cols]


N_CTX_CONSTS = 7


def _ctx_mixer_kernel(*refs, seq_len):
    x_ref, mod_ref = refs[:2]
    consts = refs[2:2 + N_CTX_CONSTS]
    att_ref, ht_ref, c_ref, n_ref, m_ref, ckv_ref, kpet_ref = refs[2 + N_CTX_CONSTS:9 + N_CTX_CONSTS]
    q_s, kc_s, v_s, mq_s, mk_s, mv_s, gate_s = refs[9 + N_CTX_CONSTS:]
    _inproj_kernel(x_ref, mod_ref, *consts, q_s, kc_s, v_s, mq_s, mk_s, mv_s, gate_s, ckv_ref, kpet_ref,
                   rope=False, mod_row0=0, tiles_per_seq=None)
    q3, k3, v3, mq3, mk3, mv3, g3 = (_SeqView(r, seq_len) for r in (q_s, kc_s, v_s, mq_s, mk_s, mv_s, gate_s))
    _attn_kernel(q3, k3, v3, att_ref, has_cache=False, n_cast=0, has_adaln=False)
    _mlstm_ctx_kernel(mq3, mk3, mv3, g3, ht_ref, c_ref, n_ref, m_ref)


def _ctx_mixer(x, mod, gmix, w_in_p, gq, w_uq_p, gkv, w_ukv_p, bg, *, seq_len):
    n_tok = x.shape[0]
    ns, s = CTX_SEQS_PER_STEP, seq_len
    nseq, tm = n_tok // s, ns * s
    dv = MLSTM_HEADS * MLSTM_DV
    consts = (gmix, w_in_p, gq, w_uq_p, gkv, w_ukv_p, bg)
    const_specs = [_const_spec(gmix.shape), _const_spec((SEG_MO[0], w_in_p.shape[1]))]
    const_specs += [_const_spec(a.shape) for a in consts[2:]]
    seq_blk = lambda *dims: pl.BlockSpec((ns,) + dims, lambda i: (i,) + (0,) * len(dims))
    return pl.pallas_call(
        functools.partial(_ctx_mixer_kernel, seq_len=s),
        grid=(nseq // ns,),
        in_specs=[pl.BlockSpec((tm, D_MODEL), lambda i: (i, 0)), pl.BlockSpec(mod.shape, lambda i: (0, 0))]
                 + const_specs,
        out_specs=[seq_blk(s, MLA_HEADS * MLA_V_DIM), seq_blk(dv, s), seq_blk(N_DIRHEAD, MLSTM_DV, MLSTM_DK),
                   seq_blk(N_DIRHEAD, MLSTM_DK), seq_blk(1, LANES),
                   pl.BlockSpec((tm, KV_RANK), lambda i: (i, 0)), seq_blk(ROPE_DIM, s)],
        out_shape=[jax.ShapeDtypeStruct((nseq, s, MLA_HEADS * MLA_V_DIM), BF16),
                   jax.ShapeDtypeStruct((nseq, dv, s), F32),
                   jax.ShapeDtypeStruct((nseq, N_DIRHEAD, MLSTM_DV, MLSTM_DK), F32),
                   jax.ShapeDtypeStruct((nseq, N_DIRHEAD, MLSTM_DK), F32),
                   jax.ShapeDtypeStruct((nseq, 1, LANES), F32),
                   jax.ShapeDtypeStruct((n_tok, KV_RANK), F32),
                   jax.ShapeDtypeStruct((nseq, ROPE_DIM, s), F32)],
        scratch_shapes=[pltpu.VMEM((tm, MLA_HEADS * HEAD_BLOCK), BF16), pltpu.VMEM((tm, MLA_HEADS * HEAD_BLOCK), BF16),
                        pltpu.VMEM((tm, V_WIDTH), BF16), pltpu.VMEM((tm, MLSTM_HEADS * MLSTM_DK), BF16),
                        pltpu.VMEM((tm, MLSTM_HEADS * MLSTM_DK), BF16), pltpu.VMEM((tm, dv), BF16),
                        pltpu.VMEM((tm, LANES), F32)],
        compiler_params=_params(("parallel",)),
        name="ctx_mixer",
    )(x, mod, *consts)


def _mlstm_lat_kernel(qf_ref, kf_ref, vf_ref, bf_ref, uf_ref, qb_ref, kb_ref, vb_ref, bb_ref, ub_ref,
                      c0_ref, n0_ref, m0_ref,
                      ht_ref, c_ref, n_ref, m_ref, *, n_chunks):
    step = pl.program_id(1)

    @pl.when(step == 0)
    def _():
        c_ref[...] = c0_ref[...]
        n_ref[...] = n0_ref[...]
        m_ref[...] = m0_ref[...]
        ht_ref[...] = jnp.zeros_like(ht_ref)

    n = qf_ref.shape[0]
    is_f = lax.broadcasted_iota(jnp.int32, (n, LANES), 1) < MLSTM_HEADS
    b = jnp.where(is_f, bf_ref[...], bb_ref[...])
    u = jnp.where(is_f, uf_ref[...], ub_ref[...])
    tot = jnp.where(is_f[:1], bf_ref[n - 1:n, :], bb_ref[0:1, :])
    m_prev = m_ref[...]
    u2, bt2, w_t, m_new, decay = _chunk_setup(b, u, tot, m_prev)
    m2_prev = m_prev * LOG2E
    units = [(fwd, hd, hd if fwd else MLSTM_HEADS + hd) for fwd in (True, False) for hd in range(MLSTM_HEADS)]
    refs = {True: (qf_ref, kf_ref, vf_ref), False: (qb_ref, kb_ref, vb_ref)}
    qs = [refs[fwd][0][:, hd * MLSTM_DK:(hd + 1) * MLSTM_DK] for fwd, hd, _ in units]
    ks = [refs[fwd][1][:, hd * MLSTM_DK:(hd + 1) * MLSTM_DK] for fwd, hd, _ in units]
    vts = [refs[fwd][2][:, hd * MLSTM_DV:(hd + 1) * MLSTM_DV].T for fwd, hd, _ in units]
    s0ts = [_dot_nt(k, q) for k, q in zip(ks, qs)]
    c_prevs = [c_ref[kk] for _, _, kk in units]
    n_prevs = [n_ref[kk:kk + 1, :] for _, _, kk in units]
    inters = [(_dot_nt(c.astype(BF16), q), _dot_nt(jnp.broadcast_to(nv, (8, MLSTM_DK)).astype(BF16), q)[0:1, :])
              for c, nv, q in zip(c_prevs, n_prevs, qs)]
    m_rows = [_lane_bcast(m2_prev, kk, n) for _, _, kk in units]
    wts = [_dir_weights(s0t, _lane_bcast(u2, kk, n), m_row, fwd=fwd)
           for s0t, m_row, (fwd, _, kk) in zip(s0ts, m_rows, units)]
    hts = [_dot(vt, sdt) for vt, (sdt, _, _) in zip(vts, wts)]
    for (fwd, hd, kk), k, vt, (_, den, g), ht, inter, m_row, c_prev, n_prev in zip(
            units, ks, vts, wts, hts, inters, m_rows, c_prevs, n_prevs):
        chunk = step if fwd else n_chunks - 1 - step
        ht_ref[chunk, hd * MLSTM_DV:(hd + 1) * MLSTM_DV, :] += _dir_finish(ht, den, g, bt2[kk:kk + 1, :], m_row, inter)
        c_new, n_new = _state_update(vt, k, w_t, kk)
        dk = _lane_bcast(decay, kk)
        c_ref[kk] = dk * c_prev + c_new
        n_ref[kk:kk + 1, :] = dk * n_prev + n_new
    m_ref[...] = m_new


def _mlstm_lat(mq, mk, mv, gate_b, gate_u, c0, n0, m0):
    b, s, _ = mq.shape
    lc = MLSTM_CHUNK
    nc = s // lc
    fw = lambda c: pl.BlockSpec((None, lc, c), lambda i, j: (i, j, 0))
    bw = lambda c: pl.BlockSpec((None, lc, c), lambda i, j: (i, nc - 1 - j, 0))
    dk, dv = MLSTM_HEADS * MLSTM_DK, MLSTM_HEADS * MLSTM_DV
    return pl.pallas_call(
        functools.partial(_mlstm_lat_kernel, n_chunks=nc),
        grid=(b, nc),
        in_specs=[fw(dk), fw(dk), fw(dv), fw(LANES), fw(LANES), bw(dk), bw(dk), bw(dv), bw(LANES), bw(LANES),
                  pl.BlockSpec((None, N_DIRHEAD, MLSTM_DV, MLSTM_DK), lambda i, j: (i, 0, 0, 0)),
                  pl.BlockSpec((None, N_DIRHEAD, MLSTM_DK), lambda i, j: (i, 0, 0)),
                  pl.BlockSpec((None, 1, LANES), lambda i, j: (i, 0, 0))],
        out_specs=pl.BlockSpec((nc, dv, lc), lambda i, j: (i, 0, 0)),
        out_shape=jax.ShapeDtypeStruct((b * nc, dv, lc), F32),
        scratch_shapes=[pltpu.VMEM((N_DIRHEAD, MLSTM_DV, MLSTM_DK), F32),
                        pltpu.VMEM((N_DIRHEAD, MLSTM_DK), F32),
                        pltpu.VMEM((1, LANES), F32)],
        compiler_params=_params(("parallel", "arbitrary")),
        name="mlstm_lat",
    )(mq, mk, mv, gate_b, gate_u, mq, mk, mv, gate_b, gate_u, c0, n0, m0)


def _tail_kernel(x_ref, modin_ref, mod_ref, att_ref, hst_ref, gmix_ref, wgate_ref, gml_ref, womla_ref, womlstm_ref,
                 wout_ref, gffn_ref, wfin_ref, wfout_ref, gfin_ref, y_ref, *, mod_row0, tiles_per_seq):
    shift_mix, scale_mix = _mod_chunks(modin_ref, mod_row0, tiles_per_seq)
    gate_mix, shift_ffn, scale_ffn, gate_ffn = _mod_chunks(mod_ref, mod_row0, tiles_per_seq)
    gml = gml_ref[...]
    tm = x_ref.shape[0]
    groups = [slice(r, r + TAIL_ROWS) for r in range(0, tm, TAIL_ROWS)]
    dv = MLSTM_HEADS * MLSTM_DV
    hb = [_mixer_input(x_ref[g, :], gmix_ref[...], shift_mix, scale_mix) for g in groups]
    smo = [_sigmoid(_dot_nt(h, wgate_ref[:dv, :])) for h in hb]
    g_a = [_sigmoid(_dot_nt(h, wgate_ref[dv:dv + D_MODEL, :])) for h in hb]
    g_b = [_sigmoid(_dot_nt(h, wgate_ref[dv + D_MODEL:, :])) for h in hb]
    a = [_dot(att_ref[g, :], womla_ref[...]) for g in groups]
    hm = []
    for gi in range(len(groups)):
        hs = hst_ref[gi].T
        parts = []
        for hd in range(MLSTM_HEADS):
            sl = slice(hd * MLSTM_DV, (hd + 1) * MLSTM_DV)
            parts.append((_rms(hs[:, sl], gml[:, sl]) * smo[gi][:, sl]).astype(BF16))
        hm.append(jnp.concatenate(parts, axis=1))
    bm = [_dot(h, womlstm_ref[...]) for h in hm]
    merged = [(ga_i * ai + gb_i * bi).astype(BF16) for ga_i, gb_i, ai, bi in zip(g_a, g_b, a, bm)]
    x1 = [x_ref[g, :] + gate_mix * _dot(mi, wout_ref[...]) for g, mi in zip(groups, merged)]
    h2 = [(_rms(xi, gffn_ref[...]) * (1.0 + scale_ffn) + shift_ffn).astype(BF16) for xi in x1]
    ga = [_dot(hi, wfin_ref[:, :FFN_HIDDEN]) for hi in h2]
    gu = [_dot(hi, wfin_ref[:, FFN_HIDDEN:]) for hi in h2]
    act = [(gi * _sigmoid(gi) * ui).astype(BF16) for gi, ui in zip(ga, gu)]
    for g, xi, ci in zip(groups, x1, act):
        y_ref[g, :] = _rms(xi + gate_ffn * _dot(ci, wfout_ref[...]), gfin_ref[...])


def _tail(x, mod_in, mod, att, hst, gmix, w_gate, gml, w_o_mla, w_o_mlstm, w_out, gffn, w_ffn_in, w_ffn_out, gfin, *,
          seq_len, latent):
    n_tok = x.shape[0]
    tm = TOKEN_TILE
    assert hst.shape[0] * TAIL_ROWS == n_tok and hst.shape[2] == TAIL_ROWS
    tok = lambda c: pl.BlockSpec((tm, c), lambda i: (i, 0))
    whole = lambda a: pl.BlockSpec(a.shape, lambda i: (0, 0))
    consts = (gmix, w_gate, gml, w_o_mla, w_o_mlstm, w_out, gffn, w_ffn_in, w_ffn_out, gfin)
    return pl.pallas_call(
        functools.partial(_tail_kernel, mod_row0=1 if latent else 0, tiles_per_seq=seq_len // tm if latent else None),
        grid=(n_tok // tm,),
        in_specs=[tok(D_MODEL), whole(mod_in), whole(mod), tok(att.shape[1]),
                  pl.BlockSpec((tm // TAIL_ROWS,) + hst.shape[1:], lambda i: (i, 0, 0))]
                 + [_const_spec(a.shape) for a in consts],
        out_specs=tok(D_MODEL),
        out_shape=jax.ShapeDtypeStruct((n_tok, D_MODEL), F32),
        compiler_params=_params(("parallel",)),
        name="tail_lat" if latent else "tail_ctx",
    )(x, mod_in, mod, att, hst, *consts)


_IN_SIZES = (Q_RANK, KV_RANK, ROPE_DIM, MLSTM_HEADS * MLSTM_DK, MLSTM_HEADS * MLSTM_DK, MLSTM_HEADS * MLSTM_DV,
             4 * MLSTM_HEADS, MLSTM_HEADS * MLSTM_DV, 2 * D_MODEL)
_IN_OFF = tuple(sum(_IN_SIZES[:i]) for i in range(len(_IN_SIZES) + 1))
PACK_ROWS = LANES
_PAD_AFTER = tuple((_SEG_OFF[i + 1] // PACK_ROWS, _SEG_WIDTHS[i] - _IN_SIZES[i], _IN_SIZES[i] % PACK_ROWS)
                   for i in range(len(_IN_SIZES)) if _SEG_WIDTHS[i] != _IN_SIZES[i])


PACK_PIECES = 8
N_PACK_STEPS = pl.cdiv(SEG_MO[0] // PACK_ROWS, PACK_PIECES)
_N_IN = _IN_OFF[-1]


def _pack_src_row(p):
    src = p * PACK_ROWS
    for first_piece_after, pad, _ in _PAD_AFTER:
        src = src - jnp.where(p >= first_piece_after, pad, 0)
    src = jnp.minimum(src, _N_IN - PACK_ROWS)
    return pl.multiple_of(src, math.gcd(PACK_ROWS, _N_IN, *(pad for _, pad, _ in _PAD_AFTER)))


def _pack_in_kernel(*refs):
    w_refs, o_ref = refs[:PACK_PIECES], refs[PACK_PIECES]
    row = lax.broadcasted_iota(jnp.int32, w_refs[0].shape, 0)
    for j, w_ref in enumerate(w_refs):
        p = pl.program_id(0) * PACK_PIECES + j
        valid = jnp.where(p < N_IN_PACKED // PACK_ROWS, PACK_ROWS, 0)
        for first_piece_after, _, width in _PAD_AFTER:
            valid = jnp.where(p == first_piece_after - 1, width, valid)
        o_ref[j * PACK_ROWS:(j + 1) * PACK_ROWS, :] = jnp.where(row < valid, w_ref[...], 0.0).astype(BF16)


def _pack_in(w_in_t):
    n, k = w_in_t.shape
    piece = lambda j: pl.BlockSpec((pl.Element(PACK_ROWS), pl.Element(k)),
                                   lambda i: (_pack_src_row(i * PACK_PIECES + j), 0))
    return pl.pallas_call(
        _pack_in_kernel,
        grid=(N_PACK_STEPS,),
        in_specs=[piece(j) for j in range(PACK_PIECES)],
        out_specs=pl.BlockSpec((PACK_PIECES * PACK_ROWS, k), lambda i: (i, 0)),
        out_shape=jax.ShapeDtypeStruct((N_PACK_STEPS * PACK_PIECES * PACK_ROWS, k), BF16),
        compiler_params=_params(("parallel",)),
        name="pack_in",
    )(*([w_in_t] * PACK_PIECES))


def _pack_small(w_uq, w_ukv, b_gates):
    qd = NOPE_DIM + ROPE_DIM
    w_uq_p = jnp.pad(w_uq.reshape(Q_RANK, MLA_HEADS, qd), ((0, 0), (0, 0), (0, HEAD_BLOCK - qd)))
    w_uq_p = w_uq_p.reshape(Q_RANK, MLA_HEADS * HEAD_BLOCK).astype(BF16)
    kv = w_ukv.reshape(KV_RANK, MLA_HEADS, NOPE_DIM + MLA_V_DIM)
    wk = jnp.pad(kv[:, :, :NOPE_DIM], ((0, 0), (0, 0), (0, HEAD_BLOCK - NOPE_DIM))).reshape(KV_RANK, -1)
    wv = kv[:, :, NOPE_DIM:].reshape(KV_RANK, -1)
    w_ukv_p = jnp.concatenate([wk, wv], axis=1).astype(BF16)
    bg = jnp.pad(b_gates, ((0, 0), (0, LANES - b_gates.shape[1])))
    return w_uq_p, w_ukv_p, bg


def _rope_tables(n_tokens):
    pos = np.arange(n_tokens)
    row = (pos // GRID_W).astype(np.float64)[:, None]
    col = (pos % GRID_W).astype(np.float64)[:, None]
    half = ROPE_DIM // 2
    inv = (np.float32(ROPE_BASE) ** (-np.arange(0, half, 2, dtype=np.float32) / np.float32(half))).astype(np.float64)
    r = np.arange(LANES) - ROPE_LANE0
    in_rope = (r >= 0) & (r < ROPE_DIM)
    rr = np.clip(r, 0, ROPE_DIM - 1)
    freq = inv[rr % (half // 2)][None, :]
    ang = np.where((rr // half == 0)[None, :], row * freq, col * freq).astype(np.float32).astype(np.float64)
    first = (rr % half) < (half // 2)
    cos = np.where(in_rope[None, :], np.cos(ang), 1.0)
    sin = np.sin(ang)
    sin_lo = np.where((in_rope & first)[None, :], -sin, 0.0)
    sin_hi = np.where((in_rope & ~first)[None, :], sin, 0.0)
    return tuple(jnp.asarray(t, dtype=F32) for t in (cos, sin_lo, sin_hi))


def kernel(x_prompt, x_sample, cache_ckv, cache_krope, state_C, state_n, state_m, c, c_ctx, w_mod, b_mod, g_norm_mix,
           w_in, b_gates, g_q_norm, w_uq, g_kv_norm, w_ukv, g_mlstm_norm, w_o_mla, w_o_mlstm, w_out, g_norm_ffn,
           w_ffn_in, w_ffn_out, g_final):
    bp, sp, d = x_prompt.shape
    bs, ss, _ = x_sample.shape
    layer = 0
    assert w_mod.shape[0] == 1 and sp == MLSTM_CHUNK and ss % MLSTM_CHUNK == 0

    adaln_args = (c_ctx[None, :], c, w_mod[layer], b_mod[layer][None, :])
    mod_in = _adaln(*adaln_args)

    w_in_t = w_in[layer].T
    w_in_p = _pack_in(w_in_t)
    w_uq_p, w_ukv_p, bg = _pack_small(w_uq[layer], w_ukv[layer], b_gates[layer][None, :])
    row = lambda g: g[layer][None, :]
    shared_in = (row(g_norm_mix), w_in_p, row(g_q_norm), w_uq_p, row(g_kv_norm), w_ukv_p, bg)
    seq3 = lambda a, b_, s_: a.reshape(b_, s_, a.shape[-1])

    xl = x_sample.reshape(bs * ss, d)
    q, kc, v, mq_l, mk_l, mv_l, gb_l, gu_l = _inproj(xl, mod_in, _rope_tables(ss), *shared_in, seq_len=ss)
    krope_blk = jnp.pad(cache_krope[:, layer], ((0, 0), (0, 0), (ROPE_LANE0, LANES - ROPE_LANE0 - ROPE_DIM)))
    cache = _cachekv(cache_ckv[:, layer], krope_blk, w_ukv_p)
    gate_cols = (w_in_t, _IN_OFF[7], _IN_OFF[9] - _IN_OFF[7])
    att_l, wgate_b, womla_b, womlstm_b, wout_b, wfin_b, wfout_b, mod_tail = _attention(
        seq3(q, bs, ss), seq3(kc, bs, ss), seq3(v, bs, ss), cache,
        cast_weights=(gate_cols, w_o_mla[layer], w_o_mlstm[layer], w_out[layer], w_ffn_in[layer], w_ffn_out[layer]),
        adaln_tail=adaln_args)
    tail_w = (row(g_norm_mix), wgate_b, row(g_mlstm_norm), womla_b, womlstm_b, wout_b, row(g_norm_ffn), wfin_b, wfout_b,
              g_final[None, :])

    xc = x_prompt.reshape(bp * sp, d)
    att, hs, new_c, new_n, new_m, ckv, kpe = _ctx_mixer(xc, mod_in, *shared_in, seq_len=sp)
    y_prompt = _tail(xc, mod_in, mod_tail, att.reshape(bp * sp, -1), hs, *tail_w,
                     seq_len=sp, latent=False).reshape(bp, sp, d)
    new_ckv = ckv.reshape(bp, 1, sp, KV_RANK)
    new_krope = jnp.swapaxes(kpe, 1, 2).reshape(bp, 1, sp, ROPE_DIM)
    new_C = new_c.reshape(bp, 1, 2, MLSTM_HEADS, MLSTM_DV, MLSTM_DK)
    new_N = new_n.reshape(bp, 1, 2, MLSTM_HEADS, MLSTM_DK)
    new_M = new_m[:, 0, :N_DIRHEAD].reshape(bp, 1, 2, MLSTM_HEADS)

    c0 = state_C[:, layer].reshape(bs, N_DIRHEAD, MLSTM_DV, MLSTM_DK)
    n0 = state_n[:, layer].reshape(bs, N_DIRHEAD, MLSTM_DK)
    m0 = jnp.pad(state_m[:, layer].reshape(bs, 1, N_DIRHEAD), ((0, 0), (0, 0), (0, LANES - N_DIRHEAD)))
    hs = _mlstm_lat(seq3(mq_l, bs, ss), seq3(mk_l, bs, ss), seq3(mv_l, bs, ss), seq3(gb_l, bs, ss), seq3(gu_l, bs, ss),
                    c0, n0, m0)
    y_sample = _tail(xl, mod_in, mod_tail, att_l.reshape(bs * ss, -1), hs, *tail_w,
                     seq_len=ss, latent=True).reshape(bs, ss, d)
    return (y_prompt, y_sample, new_ckv, new_krope, new_C, new_N, new_M)
```

```python
import functools
import math

import jax
import jax.numpy as jnp
import numpy as np
from jax import lax
from jax.experimental import pallas as pl
from jax.experimental.pallas import tpu as pltpu

F32 = jnp.float32
BF16 = jnp.bfloat16

D_MODEL = 1024
GRID_W = 64
MLA_HEADS = 8
Q_RANK = 384
KV_RANK = 256
NOPE_DIM = 64
ROPE_DIM = 32
MLA_V_DIM = 64
ROPE_BASE = 10000.0
MLA_SCALE = (NOPE_DIM + ROPE_DIM) ** -0.5
MLSTM_HEADS = 4
MLSTM_DK = 128
MLSTM_DV = 256
FFN_HIDDEN = ((8 * D_MODEL // 3 + 255) // 256) * 256
EPS = 1e-6

LANES = 128
HEAD_BLOCK = LANES
ROPE_LANE0 = NOPE_DIM
N_DIRHEAD = 2 * MLSTM_HEADS
MLSTM_CHUNK = 256
LOG2E = math.log2(math.e)
Q_PRESCALE = MLA_SCALE * LOG2E
V_SLAB = 2 * LANES
V_WIDTH = (MLA_HEADS // 2) * V_SLAB
TOKEN_TILE = 512
TAIL_ROWS = 256
INPROJ_ROWS = TOKEN_TILE
Q_TILE = 256
CTX_SEQS_PER_STEP = 4
VMEM_LIMIT = 56 * 1024 * 1024

_SEG_WIDTHS = (Q_RANK, KV_RANK, LANES, MLSTM_HEADS * MLSTM_DK, MLSTM_HEADS * MLSTM_DK,
               MLSTM_HEADS * MLSTM_DV, LANES, MLSTM_HEADS * MLSTM_DV, 2 * D_MODEL)
_SEG_OFF = tuple(sum(_SEG_WIDTHS[:i]) for i in range(len(_SEG_WIDTHS) + 1))
SEG_Q, SEG_KV, SEG_KPE, SEG_MQ, SEG_MK, SEG_MV, SEG_GATE, SEG_MO, SEG_BR = (
    (_SEG_OFF[i], _SEG_OFF[i + 1]) for i in range(9))
N_IN_PACKED = _SEG_OFF[-1]


def _dot(a, b):
    return jnp.dot(a, b, preferred_element_type=F32)


def _dot_nt(a, b):
    return lax.dot_general(a, b, (((1,), (1,)), ((), ())), preferred_element_type=F32)


def _rms(x, g):
    ms = jnp.mean(x * x, axis=-1, keepdims=True)
    return x * lax.rsqrt(ms + EPS) * g


def _sigmoid(x):
    return 1.0 / (1.0 + jnp.exp(-x))


def _const_spec(shape):
    nd = len(shape)
    return pl.BlockSpec(shape, lambda *_: (0,) * nd, pipeline_mode=pl.Buffered(1))


def _mod_chunks(mod_ref, row0, tiles_per_seq):
    row = row0 if tiles_per_seq is None else row0 + pl.program_id(0) // tiles_per_seq
    m = mod_ref[pl.ds(row, 1), :]
    return [m[:, k * D_MODEL:(k + 1) * D_MODEL] for k in range(m.shape[1] // D_MODEL)]


def _params(sem, flags=None, vmem_mib=None):
    limit = VMEM_LIMIT if vmem_mib is None else vmem_mib * 1024 * 1024
    return pltpu.CompilerParams(dimension_semantics=sem, vmem_limit_bytes=limit, flags=flags)


N_MOD_MIXER = 2
N_MOD_TAIL = 4


COND_ROWS = 8


def _adaln_block(cctx_ref, c_ref, w_ref, b_ref, o_ref):
    pad = jnp.zeros((COND_ROWS - 1 - c_ref.shape[0], D_MODEL), F32)
    c = jnp.concatenate([cctx_ref[...], c_ref[...], pad], axis=0)
    s = c * _sigmoid(c)
    o_ref[...] = _dot(s.astype(BF16), w_ref[...].astype(BF16)) + b_ref[...]


def _adaln(c_ctx, c, w_mod, b_mod):
    tn = D_MODEL
    return pl.pallas_call(
        _adaln_block,
        grid=(N_MOD_MIXER,),
        in_specs=[pl.BlockSpec(c_ctx.shape, lambda j: (0, 0)),
                  pl.BlockSpec(c.shape, lambda j: (0, 0)),
                  pl.BlockSpec((D_MODEL, tn), lambda j: (0, j)),
                  pl.BlockSpec((1, tn), lambda j: (0, j))],
        out_specs=pl.BlockSpec((COND_ROWS, tn), lambda j: (0, j)),
        out_shape=jax.ShapeDtypeStruct((COND_ROWS, N_MOD_MIXER * D_MODEL), F32),
        compiler_params=_params(("arbitrary",)),
        name="adaln",
    )(c_ctx, c, w_mod, b_mod)


def _rope_block(x, cos, sin_lo, sin_hi):
    return x * cos + pltpu.roll(x, LANES - 8, 1) * sin_lo + pltpu.roll(x, 8, 1) * sin_hi


def _store_kv(kvn, kp, kc_ref, v_ref, g=slice(None)):
    for hd in range(MLA_HEADS):
        sl = slice(hd * HEAD_BLOCK, (hd + 1) * HEAD_BLOCK)
        kc_ref[g, sl] = (kvn[:, sl] + kp).astype(BF16)
    v0 = MLA_HEADS * HEAD_BLOCK
    ones = jnp.ones((kvn.shape[0], LANES), BF16)
    for pair in range(MLA_HEADS // 2):
        v_ref[g, pair * V_SLAB:pair * V_SLAB + LANES] = kvn[:, v0 + pair * LANES:v0 + (pair + 1) * LANES].astype(BF16)
        v_ref[g, pair * V_SLAB + LANES:(pair + 1) * V_SLAB] = ones


def _mixer_input(x, g, shift, scale):
    return (_rms(x, g) * (1.0 + scale) + shift).astype(BF16)


def _inproj_kernel(*refs, rope, mod_row0, tiles_per_seq):
    if rope:
        (x_ref, mod_ref, cos_ref, slo_ref, shi_ref, gmix_ref, win_ref, gq_ref, wuq_ref, gkv_ref, wukv_ref,
         bg_ref, q_ref, kc_ref, v_ref, mq_ref, mk_ref, mv_ref, gb_ref, gu_ref) = refs
    else:
        (x_ref, mod_ref, gmix_ref, win_ref, gq_ref, wuq_ref, gkv_ref, wukv_ref,
         bg_ref, q_ref, kc_ref, v_ref, mq_ref, mk_ref, mv_ref, gate_ref, ckv_ref, kpet_ref) = refs
    shift, scale = _mod_chunks(mod_ref, mod_row0, tiles_per_seq)
    for r0 in range(0, x_ref.shape[0], INPROJ_ROWS):
        g = slice(r0, r0 + INPROJ_ROWS)
        if rope:
            cos, slo, shi = cos_ref[g, :], slo_ref[g, :], shi_ref[g, :]
        hb = _mixer_input(x_ref[g, :], gmix_ref[...], shift, scale)

        def proj(seg):
            return _dot_nt(hb, win_ref[seg[0]:seg[1], :])

        qn = _rms(proj(SEG_Q), gq_ref[...]).astype(BF16)
        ckv = _rms(proj(SEG_KV), gkv_ref[...])
        zkpe = proj(SEG_KPE)
        gates = proj(SEG_GATE) + bg_ref[...]
        if rope:
            for c0 in range(0, INPROJ_ROWS, MLSTM_CHUNK):
                b, u, _ = _gate_prep(gates[c0:c0 + MLSTM_CHUNK, :])
                gb_ref[r0 + c0:r0 + c0 + MLSTM_CHUNK, :] = b
                gu_ref[r0 + c0:r0 + c0 + MLSTM_CHUNK, :] = u
        else:
            gate_ref[g, :] = gates

        q = _dot(qn, wuq_ref[...])
        for hd in range(MLA_HEADS):
            sl = slice(hd * HEAD_BLOCK, (hd + 1) * HEAD_BLOCK)
            qh = q[:, sl]
            if rope:
                qh = _rope_block(qh, cos, slo, shi)
            q_ref[g, sl] = (qh * Q_PRESCALE).astype(BF16)

        if not rope:
            ckv_ref[g, :] = ckv
            kpe_t = zkpe.T
            seq = kpet_ref.shape[2]
            for sq in range(INPROJ_ROWS // seq):
                kpet_ref[r0 // seq + sq] = kpe_t[:ROPE_DIM, sq * seq:(sq + 1) * seq]
        kp = pltpu.roll(zkpe, ROPE_LANE0, 1)
        if rope:
            kp = _rope_block(kp, cos, slo, shi)
        kvn = _dot(ckv.astype(BF16), wukv_ref[...])
        _store_kv(kvn, kp, kc_ref, v_ref, g)

        mv_ref[g, :] = proj(SEG_MV).astype(BF16)
        mk_ref[g, :] = (proj(SEG_MK) * (MLSTM_DK ** -0.5)).astype(BF16)
        mq_ref[g, :] = proj(SEG_MQ).astype(BF16)


def _inproj(x, mod, rope_tabs, gmix, w_in_p, gq, w_uq_p, gkv, w_ukv_p, bg, *, seq_len):
    n_tok = x.shape[0]
    tm = TOKEN_TILE
    tiles_per_seq = seq_len // tm
    rope = rope_tabs is not None
    tok = lambda c: pl.BlockSpec((tm, c), lambda i: (i, 0))
    in_specs = [tok(D_MODEL), pl.BlockSpec(mod.shape, lambda i: (0, 0))]
    args = [x, mod]
    if rope:
        in_specs += [pl.BlockSpec((tm, LANES), lambda i: (i % tiles_per_seq, 0))] * 3
        args += list(rope_tabs)
    win_rows = SEG_MO[0]
    in_specs += [_const_spec(gmix.shape), _const_spec((win_rows, w_in_p.shape[1]))]
    in_specs += [_const_spec(a.shape) for a in (gq, w_uq_p, gkv, w_ukv_p, bg)]
    args += [gmix, w_in_p, gq, w_uq_p, gkv, w_ukv_p, bg]
    widths = [(MLA_HEADS * HEAD_BLOCK, BF16), (MLA_HEADS * HEAD_BLOCK, BF16), (V_WIDTH, BF16),
              (MLSTM_HEADS * MLSTM_DK, BF16), (MLSTM_HEADS * MLSTM_DK, BF16), (MLSTM_HEADS * MLSTM_DV, BF16),
              (LANES, F32)]
    if rope:
        widths.append((LANES, F32))
    out_specs = [tok(c) for c, _ in widths]
    out_shape = [jax.ShapeDtypeStruct((n_tok, c), dt) for c, dt in widths]
    if not rope:
        out_specs += [tok(KV_RANK), pl.BlockSpec((tm // seq_len, ROPE_DIM, seq_len), lambda i: (i, 0, 0))]
        out_shape += [jax.ShapeDtypeStruct((n_tok, KV_RANK), F32),
                      jax.ShapeDtypeStruct((n_tok // seq_len, ROPE_DIM, seq_len), F32)]
    return pl.pallas_call(
        functools.partial(_inproj_kernel, rope=rope, mod_row0=1 if rope else 0,
                          tiles_per_seq=tiles_per_seq if rope else None),
        grid=(n_tok // tm,),
        in_specs=in_specs,
        out_specs=out_specs,
        out_shape=out_shape,
        compiler_params=_params(("parallel",)),
        name="inproj_lat" if rope else "inproj_ctx",
    )(*args)


def _cachekv_kernel(ckv_ref, kp_ref, wukv_ref, kc_ref, v_ref):
    kvn = _dot(ckv_ref[...].astype(BF16), wukv_ref[...])
    _store_kv(kvn, kp_ref[...], kc_ref, v_ref)


def _cachekv(cache_ckv, krope_blk, w_ukv_p):
    b, past, _ = cache_ckv.shape
    return pl.pallas_call(
        _cachekv_kernel,
        grid=(b,),
        in_specs=[pl.BlockSpec((None, past, KV_RANK), lambda i: (i, 0, 0)),
                  pl.BlockSpec((None, past, LANES), lambda i: (i, 0, 0)),
                  _const_spec(w_ukv_p.shape)],
        out_specs=[pl.BlockSpec((None, past, MLA_HEADS * HEAD_BLOCK), lambda i: (i, 0, 0)),
                   pl.BlockSpec((None, past, V_WIDTH), lambda i: (i, 0, 0))],
        out_shape=[jax.ShapeDtypeStruct((b, past, MLA_HEADS * HEAD_BLOCK), BF16),
                   jax.ShapeDtypeStruct((b, past, V_WIDTH), BF16)],
        compiler_params=_params(("parallel",)),
        name="cachekv",
    )(cache_ckv, krope_blk, w_ukv_p)


def _attn_kernel(*refs, has_cache, n_cast, has_adaln):
    n_in = 5 if has_cache else 3
    if has_cache:
        q_ref, k_ref, v_ref, kc_ref, vc_ref = refs[:n_in]
    else:
        q_ref, k_ref, v_ref = refs[:n_in]
    n_all_in = n_in + n_cast + (4 if has_adaln else 0)
    o_ref = refs[n_all_in]
    for w_ref, wb_ref in zip(refs[n_in:n_in + n_cast], refs[n_all_in + 1:n_all_in + 1 + n_cast]):
        wb_ref[...] = w_ref[...].astype(BF16)
    if has_adaln:
        _adaln_block(*refs[n_in + n_cast:n_all_in], refs[n_all_in + 1 + n_cast])
    tq = q_ref.shape[1]
    lane = lax.broadcasted_iota(jnp.int32, (tq, LANES), 1)
    for sq in range(q_ref.shape[0]):
        hsl = [slice(hd * HEAD_BLOCK, (hd + 1) * HEAD_BLOCK) for hd in range(MLA_HEADS)]
        vsl = [slice(hd // 2 * V_SLAB, (hd // 2 + 1) * V_SLAB) for hd in range(MLA_HEADS)]
        s = [_dot_nt(q_ref[sq, :, sl], k_ref[sq, :, sl]) for sl in hsl]
        m = [jnp.max(si, axis=-1, keepdims=True) for si in s]
        if has_cache:
            sc = [_dot_nt(q_ref[sq, :, sl], kc_ref[sq, :, sl]) for sl in hsl]
            m = [jnp.maximum(mi, jnp.max(ci, axis=-1, keepdims=True)) for mi, ci in zip(m, sc)]
        p = [jnp.exp2(si - mi).astype(BF16) for si, mi in zip(s, m)]
        o = [_dot(pi, v_ref[sq, :, sl]) for pi, sl in zip(p, vsl)]
        if has_cache:
            pc = [jnp.exp2(ci - mi).astype(BF16) for ci, mi in zip(sc, m)]
            o = [oi + _dot(pi, vc_ref[sq, :, sl]) for oi, pi, sl in zip(o, pc, vsl)]
        outs = [oi[:, :LANES] * (1.0 / oi[:, LANES:]) for oi in o]
        for pair in range(MLA_HEADS // 2):
            o_ref[sq, :, pair * LANES:(pair + 1) * LANES] = jnp.where(
                lane < MLA_V_DIM, outs[2 * pair], outs[2 * pair + 1]).astype(BF16)


def _attention(q, k, v, cache=None, cast_weights=(), adaln_tail=None):
    nb, s, _ = q.shape
    tq = Q_TILE
    ns = 1
    b = nb // ns
    n_steps = b * (s // tq)
    kw, vw, ow = MLA_HEADS * HEAD_BLOCK, V_WIDTH, MLA_HEADS * MLA_V_DIM
    in_specs = [pl.BlockSpec((ns, tq, kw), lambda i, j: (i, j, 0)),
                pl.BlockSpec((ns, s, kw), lambda i, j: (i, 0, 0)),
                pl.BlockSpec((ns, s, vw), lambda i, j: (i, 0, 0))]
    args = [q, k, v]
    if cache is not None:
        past = cache[0].shape[1]
        in_specs += [pl.BlockSpec((ns, past, kw), lambda i, j: (i, 0, 0)),
                     pl.BlockSpec((ns, past, vw), lambda i, j: (i, 0, 0))]
        args += list(cache)
    step = lambda i, j: i * (s // tq) + j
    casts = [w if isinstance(w, tuple) else (w, 0, w.shape[0]) for w in cast_weights]
    cast_weights = [w for w, _, _ in casts]

    def cast_in_spec(w, row0, n_rows):
        rows = n_rows // n_steps
        if (row0, n_rows) == (0, w.shape[0]):
            return pl.BlockSpec((rows, w.shape[1]), lambda i, j: (step(i, j), 0))
        return pl.BlockSpec((pl.Element(rows), pl.Element(w.shape[1])),
                            lambda i, j: (pl.multiple_of(row0 + step(i, j) * rows, math.gcd(row0, rows)), 0))

    w_in_specs = [cast_in_spec(*c) for c in casts]
    w_specs = [pl.BlockSpec((n // n_steps, w.shape[1]), lambda i, j: (step(i, j), 0)) for w, _, n in casts]
    extra_in, extra_out, extra_shape = [], [], []
    if adaln_tail is not None:
        tn = N_MOD_TAIL * D_MODEL // n_steps
        col0 = N_MOD_MIXER * D_MODEL // tn
        extra_in = [pl.BlockSpec(adaln_tail[0].shape, lambda i, j: (0, 0)),
                    pl.BlockSpec(adaln_tail[1].shape, lambda i, j: (0, 0)),
                    pl.BlockSpec((D_MODEL, tn), lambda i, j: (0, col0 + step(i, j))),
                    pl.BlockSpec((1, tn), lambda i, j: (0, col0 + step(i, j)))]
        extra_out = [pl.BlockSpec((COND_ROWS, tn), lambda i, j: (0, step(i, j)))]
        extra_shape = [jax.ShapeDtypeStruct((COND_ROWS, N_MOD_TAIL * D_MODEL), F32)]
    outs = pl.pallas_call(
        functools.partial(_attn_kernel, has_cache=cache is not None, n_cast=len(cast_weights),
                          has_adaln=adaln_tail is not None),
        grid=(b, s // tq),
        in_specs=in_specs + w_in_specs + extra_in,
        out_specs=[pl.BlockSpec((ns, tq, ow), lambda i, j: (i, j, 0))] + w_specs + extra_out,
        out_shape=[jax.ShapeDtypeStruct((nb, s, ow), BF16)]
                  + [jax.ShapeDtypeStruct((n, w.shape[1]), BF16) for w, _, n in casts] + extra_shape,
        compiler_params=_params(("parallel", "arbitrary")),
        name="attn_lat" if cache is not None else "attn_ctx",
    )(*args, *cast_weights, *(adaln_tail or ()))
    return outs[0] if len(outs) == 1 else outs


def _lane_bcast(x, k, width=None):
    y = jnp.broadcast_to(x[:, k:k + 1], x.shape)
    reps = (width or LANES) // LANES
    return y if reps == 1 else jnp.concatenate([y] * reps, axis=1)


def _prefix_sum_rows(x):
    n = x.shape[0]
    row = lax.broadcasted_iota(jnp.int32, x.shape, 0)
    shift = 1
    while shift < n:
        x = x + jnp.where(row >= shift, pltpu.roll(x, shift, 0), 0.0)
        shift *= 2
    return x


def _gate_prep(g):
    n = g.shape[0]
    lane = lax.broadcasted_iota(jnp.int32, (n, LANES), 1)
    fpre = pltpu.roll(g, LANES - N_DIRHEAD, 1)
    lf = jnp.minimum(fpre, 0.0) - jnp.log(1.0 + jnp.exp(-jnp.abs(fpre)))
    binc = _prefix_sum_rows(lf)
    tot = binc[n - 1:n, :]
    b = jnp.where(lane < MLSTM_HEADS, binc, tot - binc + lf)
    return b, g - b, tot


def _chunk_setup(b, u, tot, m_prev):
    g_last = jnp.maximum(jnp.max(u, axis=0, keepdims=True), m_prev)
    u2 = u * LOG2E
    w = jnp.exp2(u2 - g_last * LOG2E)
    return u2, (b * LOG2E).T, w.T, tot + g_last, jnp.exp(m_prev - g_last)


def _state_update(vt, k, wt, kk):
    c_new = _dot((vt * wt[kk:kk + 1, :]).astype(BF16), k)
    n_new = _dot(wt[:N_DIRHEAD, :].astype(BF16), k)[kk:kk + 1, :]
    return c_new, n_new


def _dir_weights(s0t, u_b, m_row, *, fwd):
    n = s0t.shape[0]
    r = lax.broadcasted_iota(jnp.int32, (n, n), 0)
    c = lax.broadcasted_iota(jnp.int32, (n, n), 1)
    a = jnp.where((r <= c) if fwd else (r >= c), u_b, -jnp.inf)
    g = jnp.maximum(jnp.max(a, axis=0, keepdims=True), m_row)
    sdt = s0t * jnp.exp2(a - g)
    return sdt.astype(BF16), jnp.sum(sdt, axis=0, keepdims=True), g


def _dir_finish(ht, den, g, bt_row, m_row, inter=None):
    if inter is not None:
        w_inter = jnp.exp2(m_row - g)
        ht = ht + w_inter * inter[0]
        den = den + w_inter * inter[1]
    floor = jnp.exp2(-(bt_row + g))
    return ht * (1.0 / jnp.maximum(jnp.abs(den), floor))


def _mlstm_ctx_kernel(q_ref, k_ref, v_ref, g_ref, ht_ref, c_ref, n_ref, m_ref):
    n = q_ref.shape[1]
    zero_lanes = jnp.zeros((1, LANES), F32)
    zero_row = jnp.zeros((1, n), F32)
    dirs = ((True, 0), (False, MLSTM_HEADS))
    setup = [_chunk_setup(*_gate_prep(g_ref[sq]), zero_lanes) for sq in range(q_ref.shape[0])]
    units = [(sq, hd) for sq in range(q_ref.shape[0]) for hd in range(MLSTM_HEADS)]
    ks = [k_ref[sq, :, hd * MLSTM_DK:(hd + 1) * MLSTM_DK] for sq, hd in units]
    s0ts = [_dot_nt(k, q_ref[sq, :, hd * MLSTM_DK:(hd + 1) * MLSTM_DK]) for k, (sq, hd) in zip(ks, units)]
    vts = [v_ref[sq, :, hd * MLSTM_DV:(hd + 1) * MLSTM_DV].T for sq, hd in units]
    wts = [[_dir_weights(s0t, _lane_bcast(setup[sq][0], off + hd, n), zero_row, fwd=fwd) for fwd, off in dirs]
           for s0t, (sq, hd) in zip(s0ts, units)]
    hts = [[_dot(vt, sdt) for sdt, _, _ in wt] for vt, wt in zip(vts, wts)]
    for (sq, hd), k, vt, wt, ht2 in zip(units, ks, vts, wts, hts):
        _, bt2, w_t, _, _ = setup[sq]
        ht_ref[sq, hd * MLSTM_DV:(hd + 1) * MLSTM_DV, :] = sum(
            _dir_finish(ht_d, den, g, bt2[off + hd:off + hd + 1, :], zero_row)
            for ht_d, (_, den, g), (_, off) in zip(ht2, wt, dirs))
        for _, off in dirs:
            c_ref[sq, off + hd], n_ref[sq, off + hd:off + hd + 1, :] = _state_update(vt, k, w_t, off + hd)
    for sq in range(q_ref.shape[0]):
        m_ref[sq] = setup[sq][3]


class _SeqView:
    def __init__(self, ref, seq_len):
        self.ref, self.seq_len = ref, seq_len
        self.shape = (ref.shape[0] // seq_len, seq_len, ref.shape[1])

    def __getitem__(self, idx):
        sq, rows, cols = idx if isinstance(idx, tuple) else (idx, slice(None), slice(None))
        assert rows == slice(None)
        return self.ref[sq * self.seq_len:(sq + 1) * self.seq_len, cols]


N_CTX_CONSTS = 7


def _ctx_mixer_kernel(*refs, seq_len):
    x_ref, mod_ref = refs[:2]
    consts = refs[2:2 + N_CTX_CONSTS]
    att_ref, ht_ref, c_ref, n_ref, m_ref, ckv_ref, kpet_ref = refs[2 + N_CTX_CONSTS:9 + N_CTX_CONSTS]
    q_s, kc_s, v_s, mq_s, mk_s, mv_s, gate_s = refs[9 + N_CTX_CONSTS:]
    _inproj_kernel(x_ref, mod_ref, *consts, q_s, kc_s, v_s, mq_s, mk_s, mv_s, gate_s, ckv_ref, kpet_ref,
                   rope=False, mod_row0=0, tiles_per_seq=None)
    q3, k3, v3, mq3, mk3, mv3, g3 = (_SeqView(r, seq_len) for r in (q_s, kc_s, v_s, mq_s, mk_s, mv_s, gate_s))
    _attn_kernel(q3, k3, v3, att_ref, has_cache=False, n_cast=0, has_adaln=False)
    _mlstm_ctx_kernel(mq3, mk3, mv3, g3, ht_ref, c_ref, n_ref, m_ref)


def _ctx_mixer(x, mod, gmix, w_in_p, gq, w_uq_p, gkv, w_ukv_p, bg, *, seq_len):
    n_tok = x.shape[0]
    ns, s = CTX_SEQS_PER_STEP, seq_len
    nseq, tm = n_tok // s, ns * s
    dv = MLSTM_HEADS * MLSTM_DV
    consts = (gmix, w_in_p, gq, w_uq_p, gkv, w_ukv_p, bg)
    const_specs = [_const_spec(gmix.shape), _const_spec((SEG_MO[0], w_in_p.shape[1]))]
    const_specs += [_const_spec(a.shape) for a in consts[2:]]
    seq_blk = lambda *dims: pl.BlockSpec((ns,) + dims, lambda i: (i,) + (0,) * len(dims))
    return pl.pallas_call(
        functools.partial(_ctx_mixer_kernel, seq_len=s),
        grid=(nseq // ns,),
        in_specs=[pl.BlockSpec((tm, D_MODEL), lambda i: (i, 0)), pl.BlockSpec(mod.shape, lambda i: (0, 0))]
                 + const_specs,
        out_specs=[seq_blk(s, MLA_HEADS * MLA_V_DIM), seq_blk(dv, s), seq_blk(N_DIRHEAD, MLSTM_DV, MLSTM_DK),
                   seq_blk(N_DIRHEAD, MLSTM_DK), seq_blk(1, LANES),
                   pl.BlockSpec((tm, KV_RANK), lambda i: (i, 0)), seq_blk(ROPE_DIM, s)],
        out_shape=[jax.ShapeDtypeStruct((nseq, s, MLA_HEADS * MLA_V_DIM), BF16),
                   jax.ShapeDtypeStruct((nseq, dv, s), F32),
                   jax.ShapeDtypeStruct((nseq, N_DIRHEAD, MLSTM_DV, MLSTM_DK), F32),
                   jax.ShapeDtypeStruct((nseq, N_DIRHEAD, MLSTM_DK), F32),
                   jax.ShapeDtypeStruct((nseq, 1, LANES), F32),
                   jax.ShapeDtypeStruct((n_tok, KV_RANK), F32),
                   jax.ShapeDtypeStruct((nseq, ROPE_DIM, s), F32)],
        scratch_shapes=[pltpu.VMEM((tm, MLA_HEADS * HEAD_BLOCK), BF16), pltpu.VMEM((tm, MLA_HEADS * HEAD_BLOCK), BF16),
                        pltpu.VMEM((tm, V_WIDTH), BF16), pltpu.VMEM((tm, MLSTM_HEADS * MLSTM_DK), BF16),
                        pltpu.VMEM((tm, MLSTM_HEADS * MLSTM_DK), BF16), pltpu.VMEM((tm, dv), BF16),
                        pltpu.VMEM((tm, LANES), F32)],
        compiler_params=_params(("parallel",)),
        name="ctx_mixer",
    )(x, mod, *consts)


def _mlstm_lat_kernel(qf_ref, kf_ref, vf_ref, bf_ref, uf_ref, qb_ref, kb_ref, vb_ref, bb_ref, ub_ref,
                      c0_ref, n0_ref, m0_ref,
                      ht_ref, c_ref, n_ref, m_ref, *, n_chunks):
    step = pl.program_id(1)

    @pl.when(step == 0)
    def _():
        c_ref[...] = c0_ref[...]
        n_ref[...] = n0_ref[...]
        m_ref[...] = m0_ref[...]
        ht_ref[...] = jnp.zeros_like(ht_ref)

    n = qf_ref.shape[0]
    is_f = lax.broadcasted_iota(jnp.int32, (n, LANES), 1) < MLSTM_HEADS
    b = jnp.where(is_f, bf_ref[...], bb_ref[...])
    u = jnp.where(is_f, uf_ref[...], ub_ref[...])
    tot = jnp.where(is_f[:1], bf_ref[n - 1:n, :], bb_ref[0:1, :])
    m_prev = m_ref[...]
    u2, bt2, w_t, m_new, decay = _chunk_setup(b, u, tot, m_prev)
    m2_prev = m_prev * LOG2E
    units = [(fwd, hd, hd if fwd else MLSTM_HEADS + hd) for fwd in (True, False) for hd in range(MLSTM_HEADS)]
    refs = {True: (qf_ref, kf_ref, vf_ref), False: (qb_ref, kb_ref, vb_ref)}
    qs = [refs[fwd][0][:, hd * MLSTM_DK:(hd + 1) * MLSTM_DK] for fwd, hd, _ in units]
    ks = [refs[fwd][1][:, hd * MLSTM_DK:(hd + 1) * MLSTM_DK] for fwd, hd, _ in units]
    vts = [refs[fwd][2][:, hd * MLSTM_DV:(hd + 1) * MLSTM_DV].T for fwd, hd, _ in units]
    s0ts = [_dot_nt(k, q) for k, q in zip(ks, qs)]
    c_prevs = [c_ref[kk] for _, _, kk in units]
    n_prevs = [n_ref[kk:kk + 1, :] for _, _, kk in units]
    inters = [(_dot_nt(c.astype(BF16), q), _dot_nt(jnp.broadcast_to(nv, (8, MLSTM_DK)).astype(BF16), q)[0:1, :])
              for c, nv, q in zip(c_prevs, n_prevs, qs)]
    m_rows = [_lane_bcast(m2_prev, kk, n) for _, _, kk in units]
    wts = [_dir_weights(s0t, _lane_bcast(u2, kk, n), m_row, fwd=fwd)
           for s0t, m_row, (fwd, _, kk) in zip(s0ts, m_rows, units)]
    hts = [_dot(vt, sdt) for vt, (sdt, _, _) in zip(vts, wts)]
    for (fwd, hd, kk), k, vt, (_, den, g), ht, inter, m_row, c_prev, n_prev in zip(
            units, ks, vts, wts, hts, inters, m_rows, c_prevs, n_prevs):
        chunk = step if fwd else n_chunks - 1 - step
        ht_ref[chunk, hd * MLSTM_DV:(hd + 1) * MLSTM_DV, :] += _dir_finish(ht, den, g, bt2[kk:kk + 1, :], m_row, inter)
        c_new, n_new = _state_update(vt, k, w_t, kk)
        dk = _lane_bcast(decay, kk)
        c_ref[kk] = dk * c_prev + c_new
        n_ref[kk:kk + 1, :] = dk * n_prev + n_new
    m_ref[...] = m_new


def _mlstm_lat(mq, mk, mv, gate_b, gate_u, c0, n0, m0):
    b, s, _ = mq.shape
    lc = MLSTM_CHUNK
    nc = s // lc
    fw = lambda c: pl.BlockSpec((None, lc, c), lambda i, j: (i, j, 0))
    bw = lambda c: pl.BlockSpec((None, lc, c), lambda i, j: (i, nc - 1 - j, 0))
    dk, dv = MLSTM_HEADS * MLSTM_DK, MLSTM_HEADS * MLSTM_DV
    return pl.pallas_call(
        functools.partial(_mlstm_lat_kernel, n_chunks=nc),
        grid=(b, nc),
        in_specs=[fw(dk), fw(dk), fw(dv), fw(LANES), fw(LANES), bw(dk), bw(dk), bw(dv), bw(LANES), bw(LANES),
                  pl.BlockSpec((None, N_DIRHEAD, MLSTM_DV, MLSTM_DK), lambda i, j: (i, 0, 0, 0)),
                  pl.BlockSpec((None, N_DIRHEAD, MLSTM_DK), lambda i, j: (i, 0, 0)),
                  pl.BlockSpec((None, 1, LANES), lambda i, j: (i, 0, 0))],
        out_specs=pl.BlockSpec((nc, dv, lc), lambda i, j: (i, 0, 0)),
        out_shape=jax.ShapeDtypeStruct((b * nc, dv, lc), F32),
        scratch_shapes=[pltpu.VMEM((N_DIRHEAD, MLSTM_DV, MLSTM_DK), F32),
                        pltpu.VMEM((N_DIRHEAD, MLSTM_DK), F32),
                        pltpu.VMEM((1, LANES), F32)],
        compiler_params=_params(("parallel", "arbitrary")),
        name="mlstm_lat",
    )(mq, mk, mv, gate_b, gate_u, mq, mk, mv, gate_b, gate_u, c0, n0, m0)


def _tail_kernel(x_ref, modin_ref, mod_ref, att_ref, hst_ref, gmix_ref, wgate_ref, gml_ref, womla_ref, womlstm_ref,
                 wout_ref, gffn_ref, wfin_ref, wfout_ref, gfin_ref, y_ref, *, mod_row0, tiles_per_seq):
    shift_mix, scale_mix = _mod_chunks(modin_ref, mod_row0, tiles_per_seq)
    gate_mix, shift_ffn, scale_ffn, gate_ffn = _mod_chunks(mod_ref, mod_row0, tiles_per_seq)
    gml = gml_ref[...]
    tm = x_ref.shape[0]
    groups = [slice(r, r + TAIL_ROWS) for r in range(0, tm, TAIL_ROWS)]
    dv = MLSTM_HEADS * MLSTM_DV
    hb = [_mixer_input(x_ref[g, :], gmix_ref[...], shift_mix, scale_mix) for g in groups]
    smo = [_sigmoid(_dot_nt(h, wgate_ref[:dv, :])) for h in hb]
    g_a = [_sigmoid(_dot_nt(h, wgate_ref[dv:dv + D_MODEL, :])) for h in hb]
    g_b = [_sigmoid(_dot_nt(h, wgate_ref[dv + D_MODEL:, :])) for h in hb]
    a = [_dot(att_ref[g, :], womla_ref[...]) for g in groups]
    hm = []
    for gi in range(len(groups)):
        hs = hst_ref[gi].T
        parts = []
        for hd in range(MLSTM_HEADS):
            sl = slice(hd * MLSTM_DV, (hd + 1) * MLSTM_DV)
            parts.append((_rms(hs[:, sl], gml[:, sl]) * smo[gi][:, sl]).astype(BF16))
        hm.append(jnp.concatenate(parts, axis=1))
    bm = [_dot(h, womlstm_ref[...]) for h in hm]
    merged = [(ga_i * ai + gb_i * bi).astype(BF16) for ga_i, gb_i, ai, bi in zip(g_a, g_b, a, bm)]
    x1 = [x_ref[g, :] + gate_mix * _dot(mi, wout_ref[...]) for g, mi in zip(groups, merged)]
    h2 = [(_rms(xi, gffn_ref[...]) * (1.0 + scale_ffn) + shift_ffn).astype(BF16) for xi in x1]
    ga = [_dot(hi, wfin_ref[:, :FFN_HIDDEN]) for hi in h2]
    gu = [_dot(hi, wfin_ref[:, FFN_HIDDEN:]) for hi in h2]
    act = [(gi * _sigmoid(gi) * ui).astype(BF16) for gi, ui in zip(ga, gu)]
    for g, xi, ci in zip(groups, x1, act):
        y_ref[g, :] = _rms(xi + gate_ffn * _dot(ci, wfout_ref[...]), gfin_ref[...])


def _tail(x, mod_in, mod, att, hst, gmix, w_gate, gml, w_o_mla, w_o_mlstm, w_out, gffn, w_ffn_in, w_ffn_out, gfin, *,
          seq_len, latent):
    n_tok = x.shape[0]
    tm = TOKEN_TILE
    assert hst.shape[0] * TAIL_ROWS == n_tok and hst.shape[2] == TAIL_ROWS
    tok = lambda c: pl.BlockSpec((tm, c), lambda i: (i, 0))
    whole = lambda a: pl.BlockSpec(a.shape, lambda i: (0, 0))
    consts = (gmix, w_gate, gml, w_o_mla, w_o_mlstm, w_out, gffn, w_ffn_in, w_ffn_out, gfin)
    return pl.pallas_call(
        functools.partial(_tail_kernel, mod_row0=1 if latent else 0, tiles_per_seq=seq_len // tm if latent else None),
        grid=(n_tok // tm,),
        in_specs=[tok(D_MODEL), whole(mod_in), whole(mod), tok(att.shape[1]),
                  pl.BlockSpec((tm // TAIL_ROWS,) + hst.shape[1:], lambda i: (i, 0, 0))]
                 + [_const_spec(a.shape) for a in consts],
        out_specs=tok(D_MODEL),
        out_shape=jax.ShapeDtypeStruct((n_tok, D_MODEL), F32),
        compiler_params=_params(("arbitrary",)),
        name="tail_lat" if latent else "tail_ctx",
    )(x, mod_in, mod, att, hst, *consts)


_IN_SIZES = (Q_RANK, KV_RANK, ROPE_DIM, MLSTM_HEADS * MLSTM_DK, MLSTM_HEADS * MLSTM_DK, MLSTM_HEADS * MLSTM_DV,
             4 * MLSTM_HEADS, MLSTM_HEADS * MLSTM_DV, 2 * D_MODEL)
_IN_OFF = tuple(sum(_IN_SIZES[:i]) for i in range(len(_IN_SIZES) + 1))
PACK_ROWS = LANES
_PAD_AFTER = tuple((_SEG_OFF[i + 1] // PACK_ROWS, _SEG_WIDTHS[i] - _IN_SIZES[i], _IN_SIZES[i] % PACK_ROWS)
                   for i in range(len(_IN_SIZES)) if _SEG_WIDTHS[i] != _IN_SIZES[i])


PACK_PIECES = 8
N_PACK_STEPS = pl.cdiv(SEG_MO[0] // PACK_ROWS, PACK_PIECES)
_N_IN = _IN_OFF[-1]


def _pack_src_row(p):
    src = p * PACK_ROWS
    for first_piece_after, pad, _ in _PAD_AFTER:
        src = src - jnp.where(p >= first_piece_after, pad, 0)
    src = jnp.minimum(src, _N_IN - PACK_ROWS)
    return pl.multiple_of(src, math.gcd(PACK_ROWS, _N_IN, *(pad for _, pad, _ in _PAD_AFTER)))


def _pack_in_kernel(*refs):
    w_refs, o_ref = refs[:PACK_PIECES], refs[PACK_PIECES]
    row = lax.broadcasted_iota(jnp.int32, w_refs[0].shape, 0)
    for j, w_ref in enumerate(w_refs):
        p = pl.program_id(0) * PACK_PIECES + j
        valid = jnp.where(p < N_IN_PACKED // PACK_ROWS, PACK_ROWS, 0)
        for first_piece_after, _, width in _PAD_AFTER:
            valid = jnp.where(p == first_piece_after - 1, width, valid)
        o_ref[j * PACK_ROWS:(j + 1) * PACK_ROWS, :] = jnp.where(row < valid, w_ref[...], 0.0).astype(BF16)


def _pack_in(w_in_t):
    n, k = w_in_t.shape
    piece = lambda j: pl.BlockSpec((pl.Element(PACK_ROWS), pl.Element(k)),
                                   lambda i: (_pack_src_row(i * PACK_PIECES + j), 0))
    return pl.pallas_call(
        _pack_in_kernel,
        grid=(N_PACK_STEPS,),
        in_specs=[piece(j) for j in range(PACK_PIECES)],
        out_specs=pl.BlockSpec((PACK_PIECES * PACK_ROWS, k), lambda i: (i, 0)),
        out_shape=jax.ShapeDtypeStruct((N_PACK_STEPS * PACK_PIECES * PACK_ROWS, k), BF16),
        compiler_params=_params(("parallel",)),
        name="pack_in",
    )(*([w_in_t] * PACK_PIECES))


def _pack_small(w_uq, w_ukv, b_gates):
    qd = NOPE_DIM + ROPE_DIM
    w_uq_p = jnp.pad(w_uq.reshape(Q_RANK, MLA_HEADS, qd), ((0, 0), (0, 0), (0, HEAD_BLOCK - qd)))
    w_uq_p = w_uq_p.reshape(Q_RANK, MLA_HEADS * HEAD_BLOCK).astype(BF16)
    kv = w_ukv.reshape(KV_RANK, MLA_HEADS, NOPE_DIM + MLA_V_DIM)
    wk = jnp.pad(kv[:, :, :NOPE_DIM], ((0, 0), (0, 0), (0, HEAD_BLOCK - NOPE_DIM))).reshape(KV_RANK, -1)
    wv = kv[:, :, NOPE_DIM:].reshape(KV_RANK, -1)
    w_ukv_p = jnp.concatenate([wk, wv], axis=1).astype(BF16)
    bg = jnp.pad(b_gates, ((0, 0), (0, LANES - b_gates.shape[1])))
    return w_uq_p, w_ukv_p, bg


def _rope_tables(n_tokens):
    pos = np.arange(n_tokens)
    row = (pos // GRID_W).astype(np.float64)[:, None]
    col = (pos % GRID_W).astype(np.float64)[:, None]
    half = ROPE_DIM // 2
    inv = (np.float32(ROPE_BASE) ** (-np.arange(0, half, 2, dtype=np.float32) / np.float32(half))).astype(np.float64)
    r = np.arange(LANES) - ROPE_LANE0
    in_rope = (r >= 0) & (r < ROPE_DIM)
    rr = np.clip(r, 0, ROPE_DIM - 1)
    freq = inv[rr % (half // 2)][None, :]
    ang = np.where((rr // half == 0)[None, :], row * freq, col * freq).astype(np.float32).astype(np.float64)
    first = (rr % half) < (half // 2)
    cos = np.where(in_rope[None, :], np.cos(ang), 1.0)
    sin = np.sin(ang)
    sin_lo = np.where((in_rope & first)[None, :], -sin, 0.0)
    sin_hi = np.where((in_rope & ~first)[None, :], sin, 0.0)
    return tuple(jnp.asarray(t, dtype=F32) for t in (cos, sin_lo, sin_hi))


def kernel(x_prompt, x_sample, cache_ckv, cache_krope, state_C, state_n, state_m, c, c_ctx, w_mod, b_mod, g_norm_mix,
           w_in, b_gates, g_q_norm, w_uq, g_kv_norm, w_ukv, g_mlstm_norm, w_o_mla, w_o_mlstm, w_out, g_norm_ffn,
           w_ffn_in, w_ffn_out, g_final):
    bp, sp, d = x_prompt.shape
    bs, ss, _ = x_sample.shape
    layer = 0
    assert w_mod.shape[0] == 1 and sp == MLSTM_CHUNK and ss % MLSTM_CHUNK == 0

    adaln_args = (c_ctx[None, :], c, w_mod[layer], b_mod[layer][None, :])
    mod_in = _adaln(*adaln_args)

    w_in_t = w_in[layer].T
    w_in_p = _pack_in(w_in_t)
    w_uq_p, w_ukv_p, bg = _pack_small(w_uq[layer], w_ukv[layer], b_gates[layer][None, :])
    row = lambda g: g[layer][None, :]
    shared_in = (row(g_norm_mix), w_in_p, row(g_q_norm), w_uq_p, row(g_kv_norm), w_ukv_p, bg)
    seq3 = lambda a, b_, s_: a.reshape(b_, s_, a.shape[-1])

    xl = x_sample.reshape(bs * ss, d)
    q, kc, v, mq_l, mk_l, mv_l, gb_l, gu_l = _inproj(xl, mod_in, _rope_tables(ss), *shared_in, seq_len=ss)
    krope_blk = jnp.pad(cache_krope[:, layer], ((0, 0), (0, 0), (ROPE_LANE0, LANES - ROPE_LANE0 - ROPE_DIM)))
    cache = _cachekv(cache_ckv[:, layer], krope_blk, w_ukv_p)
    gate_cols = (w_in_t, _IN_OFF[7], _IN_OFF[9] - _IN_OFF[7])
    att_l, wgate_b, womla_b, womlstm_b, wout_b, wfin_b, wfout_b, mod_tail = _attention(
        seq3(q, bs, ss), seq3(kc, bs, ss), seq3(v, bs, ss), cache,
        cast_weights=(gate_cols, w_o_mla[layer], w_o_mlstm[layer], w_out[layer], w_ffn_in[layer], w_ffn_out[layer]),
        adaln_tail=adaln_args)
    tail_w = (row(g_norm_mix), wgate_b, row(g_mlstm_norm), womla_b, womlstm_b, wout_b, row(g_norm_ffn), wfin_b, wfout_b,
              g_final[None, :])

    xc = x_prompt.reshape(bp * sp, d)
    att, hs, new_c, new_n, new_m, ckv, kpe = _ctx_mixer(xc, mod_in, *shared_in, seq_len=sp)
    y_prompt = _tail(xc, mod_in, mod_tail, att.reshape(bp * sp, -1), hs, *tail_w,
                     seq_len=sp, latent=False).reshape(bp, sp, d)
    new_ckv = ckv.reshape(bp, 1, sp, KV_RANK)
    new_krope = jnp.swapaxes(kpe, 1, 2).reshape(bp, 1, sp, ROPE_DIM)
    new_C = new_c.reshape(bp, 1, 2, MLSTM_HEADS, MLSTM_DV, MLSTM_DK)
    new_N = new_n.reshape(bp, 1, 2, MLSTM_HEADS, MLSTM_DK)
    new_M = new_m[:, 0, :N_DIRHEAD].reshape(bp, 1, 2, MLSTM_HEADS)

    c0 = state_C[:, layer].reshape(bs, N_DIRHEAD, MLSTM_DV, MLSTM_DK)
    n0 = state_n[:, layer].reshape(bs, N_DIRHEAD, MLSTM_DK)
    m0 = jnp.pad(state_m[:, layer].reshape(bs, 1, N_DIRHEAD), ((0, 0), (0, 0), (0, LANES - N_DIRHEAD)))
    hs = _mlstm_lat(seq3(mq_l, bs, ss), seq3(mk_l, bs, ss), seq3(mv_l, bs, ss), seq3(gb_l, bs, ss), seq3(gu_l, bs, ss),
                    c0, n0, m0)
    y_sample = _tail(xl, mod_in, mod_tail, att_l.reshape(bs * ss, -1), hs, *tail_w,
                     seq_len=ss, latent=True).reshape(bs, ss, d)
    return (y_prompt, y_sample, new_ckv, new_krope, new_C, new_N, new_M)
```

```python
import functools
import math

import jax
import jax.numpy as jnp
import numpy as np
from jax import lax
from jax.experimental import pallas as pl
from jax.experimental.pallas import tpu as pltpu

F32 = jnp.float32
BF16 = jnp.bfloat16

D_MODEL = 1024
GRID_W = 64
MLA_HEADS = 8
Q_RANK = 384
KV_RANK = 256
NOPE_DIM = 64
ROPE_DIM = 32
MLA_V_DIM = 64
ROPE_BASE = 10000.0
MLA_SCALE = (NOPE_DIM + ROPE_DIM) ** -0.5
MLSTM_HEADS = 4
MLSTM_DK = 128
MLSTM_DV = 256
FFN_HIDDEN = ((8 * D_MODEL // 3 + 255) // 256) * 256
EPS = 1e-6

LANES = 128
HEAD_BLOCK = LANES
ROPE_LANE0 = NOPE_DIM
N_DIRHEAD = 2 * MLSTM_HEADS
MLSTM_CHUNK = 256
LOG2E = math.log2(math.e)
Q_PRESCALE = MLA_SCALE * LOG2E
V_SLAB = 2 * LANES
V_WIDTH = (MLA_HEADS // 2) * V_SLAB
TOKEN_TILE = 512
TAIL_ROWS = 256
INPROJ_ROWS = TOKEN_TILE
Q_TILE = 256
CTX_SEQS_PER_STEP = 4
VMEM_LIMIT = 56 * 1024 * 1024

_SEG_WIDTHS = (Q_RANK, KV_RANK, LANES, MLSTM_HEADS * MLSTM_DK, MLSTM_HEADS * MLSTM_DK,
               MLSTM_HEADS * MLSTM_DV, LANES, MLSTM_HEADS * MLSTM_DV, 2 * D_MODEL)
_SEG_OFF = tuple(sum(_SEG_WIDTHS[:i]) for i in range(len(_SEG_WIDTHS) + 1))
SEG_Q, SEG_KV, SEG_KPE, SEG_MQ, SEG_MK, SEG_MV, SEG_GATE, SEG_MO, SEG_BR = (
    (_SEG_OFF[i], _SEG_OFF[i + 1]) for i in range(9))
N_IN_PACKED = _SEG_OFF[-1]


def _dot(a, b):
    return jnp.dot(a, b, preferred_element_type=F32)


def _dot_nt(a, b):
    return lax.dot_general(a, b, (((1,), (1,)), ((), ())), preferred_element_type=F32)


def _rms(x, g):
    ms = jnp.mean(x * x, axis=-1, keepdims=True)
    return x * lax.rsqrt(ms + EPS) * g


def _sigmoid(x):
    return 1.0 / (1.0 + jnp.exp(-x))


def _const_spec(shape):
    nd = len(shape)
    return pl.BlockSpec(shape, lambda *_: (0,) * nd, pipeline_mode=pl.Buffered(1))


def _mod_chunks(mod_ref, row0, tiles_per_seq):
    row = row0 if tiles_per_seq is None else row0 + pl.program_id(0) // tiles_per_seq
    m = mod_ref[pl.ds(row, 1), :]
    return [m[:, k * D_MODEL:(k + 1) * D_MODEL] for k in range(m.shape[1] // D_MODEL)]


def _params(sem, flags=None, vmem_mib=None):
    limit = VMEM_LIMIT if vmem_mib is None else vmem_mib * 1024 * 1024
    return pltpu.CompilerParams(dimension_semantics=sem, vmem_limit_bytes=limit, flags=flags)


N_MOD_MIXER = 2
N_MOD_TAIL = 4


COND_ROWS = 8


def _adaln_block(cctx_ref, c_ref, w_ref, b_ref, o_ref):
    pad = jnp.zeros((COND_ROWS - 1 - c_ref.shape[0], D_MODEL), F32)
    c = jnp.concatenate([cctx_ref[...], c_ref[...], pad], axis=0)
    s = c * _sigmoid(c)
    o_ref[...] = _dot(s.astype(BF16), w_ref[...].astype(BF16)) + b_ref[...]


def _adaln(c_ctx, c, w_mod, b_mod):
    tn = D_MODEL
    return pl.pallas_call(
        _adaln_block,
        grid=(N_MOD_MIXER,),
        in_specs=[pl.BlockSpec(c_ctx.shape, lambda j: (0, 0)),
                  pl.BlockSpec(c.shape, lambda j: (0, 0)),
                  pl.BlockSpec((D_MODEL, tn), lambda j: (0, j)),
                  pl.BlockSpec((1, tn), lambda j: (0, j))],
        out_specs=pl.BlockSpec((COND_ROWS, tn), lambda j: (0, j)),
        out_shape=jax.ShapeDtypeStruct((COND_ROWS, N_MOD_MIXER * D_MODEL), F32),
        compiler_params=_params(("arbitrary",)),
        name="adaln",
    )(c_ctx, c, w_mod, b_mod)


def _rope_block(x, cos, sin_lo, sin_hi):
    return x * cos + pltpu.roll(x, LANES - 8, 1) * sin_lo + pltpu.roll(x, 8, 1) * sin_hi


def _store_kv(kvn, kp, kc_ref, v_ref, g=slice(None)):
    for hd in range(MLA_HEADS):
        sl = slice(hd * HEAD_BLOCK, (hd + 1) * HEAD_BLOCK)
        kc_ref[g, sl] = (kvn[:, sl] + kp).astype(BF16)
    v0 = MLA_HEADS * HEAD_BLOCK
    ones = jnp.ones((kvn.shape[0], LANES), BF16)
    for pair in range(MLA_HEADS // 2):
        v_ref[g, pair * V_SLAB:pair * V_SLAB + LANES] = kvn[:, v0 + pair * LANES:v0 + (pair + 1) * LANES].astype(BF16)
        v_ref[g, pair * V_SLAB + LANES:(pair + 1) * V_SLAB] = ones


def _mixer_input(x, g, shift, scale):
    return (_rms(x, g) * (1.0 + scale) + shift).astype(BF16)


def _inproj_kernel(*refs, rope, mod_row0, tiles_per_seq):
    if rope:
        (x_ref, mod_ref, cos_ref, slo_ref, shi_ref, gmix_ref, win_ref, gq_ref, wuq_ref, gkv_ref, wukv_ref,
         bg_ref, q_ref, kc_ref, v_ref, mq_ref, mk_ref, mv_ref, gb_ref, gu_ref) = refs
    else:
        (x_ref, mod_ref, gmix_ref, win_ref, gq_ref, wuq_ref, gkv_ref, wukv_ref,
         bg_ref, q_ref, kc_ref, v_ref, mq_ref, mk_ref, mv_ref, gate_ref, ckv_ref, kpet_ref) = refs
    shift, scale = _mod_chunks(mod_ref, mod_row0, tiles_per_seq)
    for r0 in range(0, x_ref.shape[0], INPROJ_ROWS):
        g = slice(r0, r0 + INPROJ_ROWS)
        if rope:
            cos, slo, shi = cos_ref[g, :], slo_ref[g, :], shi_ref[g, :]
        hb = _mixer_input(x_ref[g, :], gmix_ref[...], shift, scale)

        def proj(seg):
            return _dot_nt(hb, win_ref[seg[0]:seg[1], :])

        qn = _rms(proj(SEG_Q), gq_ref[...]).astype(BF16)
        ckv = _rms(proj(SEG_KV), gkv_ref[...])
        zkpe = proj(SEG_KPE)
        gates = proj(SEG_GATE) + bg_ref[...]
        if rope:
            for c0 in range(0, INPROJ_ROWS, MLSTM_CHUNK):
                b, u, _ = _gate_prep(gates[c0:c0 + MLSTM_CHUNK, :])
                gb_ref[r0 + c0:r0 + c0 + MLSTM_CHUNK, :] = b
                gu_ref[r0 + c0:r0 + c0 + MLSTM_CHUNK, :] = u
        else:
            gate_ref[g, :] = gates

        q = _dot(qn, wuq_ref[...])
        for hd in range(MLA_HEADS):
            sl = slice(hd * HEAD_BLOCK, (hd + 1) * HEAD_BLOCK)
            qh = q[:, sl]
            if rope:
                qh = _rope_block(qh, cos, slo, shi)
            q_ref[g, sl] = (qh * Q_PRESCALE).astype(BF16)

        if not rope:
            ckv_ref[g, :] = ckv
            kpe_t = zkpe.T
            seq = kpet_ref.shape[2]
            for sq in range(INPROJ_ROWS // seq):
                kpet_ref[r0 // seq + sq] = kpe_t[:ROPE_DIM, sq * seq:(sq + 1) * seq]
        kp = pltpu.roll(zkpe, ROPE_LANE0, 1)
        if rope:
            kp = _rope_block(kp, cos, slo, shi)
        kvn = _dot(ckv.astype(BF16), wukv_ref[...])
        _store_kv(kvn, kp, kc_ref, v_ref, g)

        mv_ref[g, :] = proj(SEG_MV).astype(BF16)
        mk_ref[g, :] = (proj(SEG_MK) * (MLSTM_DK ** -0.5)).astype(BF16)
        mq_ref[g, :] = proj(SEG_MQ).astype(BF16)


def _inproj(x, mod, rope_tabs, gmix, w_in_p, gq, w_uq_p, gkv, w_ukv_p, bg, *, seq_len):
    n_tok = x.shape[0]
    tm = TOKEN_TILE
    tiles_per_seq = seq_len // tm
    rope = rope_tabs is not None
    tok = lambda c: pl.BlockSpec((tm, c), lambda i: (i, 0))
    in_specs = [tok(D_MODEL), pl.BlockSpec(mod.shape, lambda i: (0, 0))]
    args = [x, mod]
    if rope:
        in_specs += [pl.BlockSpec((tm, LANES), lambda i: (i % tiles_per_seq, 0))] * 3
        args += list(rope_tabs)
    win_rows = SEG_MO[0]
    in_specs += [_const_spec(gmix.shape), _const_spec((win_rows, w_in_p.shape[1]))]
    in_specs += [_const_spec(a.shape) for a in (gq, w_uq_p, gkv, w_ukv_p, bg)]
    args += [gmix, w_in_p, gq, w_uq_p, gkv, w_ukv_p, bg]
    widths = [(MLA_HEADS * HEAD_BLOCK, BF16), (MLA_HEADS * HEAD_BLOCK, BF16), (V_WIDTH, BF16),
              (MLSTM_HEADS * MLSTM_DK, BF16), (MLSTM_HEADS * MLSTM_DK, BF16), (MLSTM_HEADS * MLSTM_DV, BF16),
              (LANES, F32)]
    if rope:
        widths.append((LANES, F32))
    out_specs = [tok(c) for c, _ in widths]
    out_shape = [jax.ShapeDtypeStruct((n_tok, c), dt) for c, dt in widths]
    if not rope:
        out_specs += [tok(KV_RANK), pl.BlockSpec((tm // seq_len, ROPE_DIM, seq_len), lambda i: (i, 0, 0))]
        out_shape += [jax.ShapeDtypeStruct((n_tok, KV_RANK), F32),
                      jax.ShapeDtypeStruct((n_tok // seq_len, ROPE_DIM, seq_len), F32)]
    return pl.pallas_call(
        functools.partial(_inproj_kernel, rope=rope, mod_row0=1 if rope else 0,
                          tiles_per_seq=tiles_per_seq if rope else None),
        grid=(n_tok // tm,),
        in_specs=in_specs,
        out_specs=out_specs,
        out_shape=out_shape,
        compiler_params=_params(("parallel",)),
        name="inproj_lat" if rope else "inproj_ctx",
    )(*args)


def _attn_kernel(*refs, has_cache, n_cast, has_adaln):
    n_in = 6 if has_cache else 3
    if has_cache:
        q_ref, k_ref, v_ref, ckv_ref, kp_ref, wukv_ref = refs[:n_in]
        kc_ref, vc_ref = refs[-2:]

        @pl.when(pl.program_id(1) == 0)
        def _():
            kvn = _dot(ckv_ref[0].astype(BF16), wukv_ref[...])
            _store_kv(kvn, kp_ref[0], kc_ref.at[0], vc_ref.at[0])
    else:
        q_ref, k_ref, v_ref = refs[:n_in]
    n_all_in = n_in + n_cast + (4 if has_adaln else 0)
    o_ref = refs[n_all_in]
    for w_ref, wb_ref in zip(refs[n_in:n_in + n_cast], refs[n_all_in + 1:n_all_in + 1 + n_cast]):
        wb_ref[...] = w_ref[...].astype(BF16)
    if has_adaln:
        _adaln_block(*refs[n_in + n_cast:n_all_in], refs[n_all_in + 1 + n_cast])
    tq = q_ref.shape[1]
    lane = lax.broadcasted_iota(jnp.int32, (tq, LANES), 1)
    for sq in range(q_ref.shape[0]):
        hsl = [slice(hd * HEAD_BLOCK, (hd + 1) * HEAD_BLOCK) for hd in range(MLA_HEADS)]
        vsl = [slice(hd // 2 * V_SLAB, (hd // 2 + 1) * V_SLAB) for hd in range(MLA_HEADS)]
        s = [_dot_nt(q_ref[sq, :, sl], k_ref[sq, :, sl]) for sl in hsl]
        m = [jnp.max(si, axis=-1, keepdims=True) for si in s]
        if has_cache:
            sc = [_dot_nt(q_ref[sq, :, sl], kc_ref[sq, :, sl]) for sl in hsl]
            m = [jnp.maximum(mi, jnp.max(ci, axis=-1, keepdims=True)) for mi, ci in zip(m, sc)]
        p = [jnp.exp2(si - mi).astype(BF16) for si, mi in zip(s, m)]
        o = [_dot(pi, v_ref[sq, :, sl]) for pi, sl in zip(p, vsl)]
        if has_cache:
            pc = [jnp.exp2(ci - mi).astype(BF16) for ci, mi in zip(sc, m)]
            o = [oi + _dot(pi, vc_ref[sq, :, sl]) for oi, pi, sl in zip(o, pc, vsl)]
        outs = [oi[:, :LANES] * (1.0 / oi[:, LANES:]) for oi in o]
        for pair in range(MLA_HEADS // 2):
            o_ref[sq, :, pair * LANES:(pair + 1) * LANES] = jnp.where(
                lane < MLA_V_DIM, outs[2 * pair], outs[2 * pair + 1]).astype(BF16)


def _attention(q, k, v, cache=None, cast_weights=(), adaln_tail=None):
    nb, s, _ = q.shape
    tq = Q_TILE
    ns = 1
    b = nb // ns
    n_steps = b * (s // tq)
    kw, vw, ow = MLA_HEADS * HEAD_BLOCK, V_WIDTH, MLA_HEADS * MLA_V_DIM
    in_specs = [pl.BlockSpec((ns, tq, kw), lambda i, j: (i, j, 0)),
                pl.BlockSpec((ns, s, kw), lambda i, j: (i, 0, 0)),
                pl.BlockSpec((ns, s, vw), lambda i, j: (i, 0, 0))]
    args = [q, k, v]
    scratch = []
    if cache is not None:
        past = cache[0].shape[1]
        in_specs += [pl.BlockSpec((ns, past, KV_RANK), lambda i, j: (i, 0, 0)),
                     pl.BlockSpec((ns, past, LANES), lambda i, j: (i, 0, 0)),
                     _const_spec(cache[2].shape)]
        args += list(cache)
        scratch = [pltpu.VMEM((ns, past, kw), BF16), pltpu.VMEM((ns, past, vw), BF16)]
    step = lambda i, j: i * (s // tq) + j
    casts = [w if isinstance(w, tuple) else (w, 0, w.shape[0]) for w in cast_weights]
    cast_weights = [w for w, _, _ in casts]

    def cast_in_spec(w, row0, n_rows):
        rows = n_rows // n_steps
        if (row0, n_rows) == (0, w.shape[0]):
            return pl.BlockSpec((rows, w.shape[1]), lambda i, j: (step(i, j), 0))
        return pl.BlockSpec((pl.Element(rows), pl.Element(w.shape[1])),
                            lambda i, j: (pl.multiple_of(row0 + step(i, j) * rows, math.gcd(row0, rows)), 0))

    w_in_specs = [cast_in_spec(*c) for c in casts]
    w_specs = [pl.BlockSpec((n // n_steps, w.shape[1]), lambda i, j: (step(i, j), 0)) for w, _, n in casts]
    extra_in, extra_out, extra_shape = [], [], []
    if adaln_tail is not None:
        tn = N_MOD_TAIL * D_MODEL // n_steps
        col0 = N_MOD_MIXER * D_MODEL // tn
        extra_in = [pl.BlockSpec(adaln_tail[0].shape, lambda i, j: (0, 0)),
                    pl.BlockSpec(adaln_tail[1].shape, lambda i, j: (0, 0)),
                    pl.BlockSpec((D_MODEL, tn), lambda i, j: (0, col0 + step(i, j))),
                    pl.BlockSpec((1, tn), lambda i, j: (0, col0 + step(i, j)))]
        extra_out = [pl.BlockSpec((COND_ROWS, tn), lambda i, j: (0, step(i, j)))]
        extra_shape = [jax.ShapeDtypeStruct((COND_ROWS, N_MOD_TAIL * D_MODEL), F32)]
    outs = pl.pallas_call(
        functools.partial(_attn_kernel, has_cache=cache is not None, n_cast=len(cast_weights),
                          has_adaln=adaln_tail is not None),
        grid=(b, s // tq),
        in_specs=in_specs + w_in_specs + extra_in,
        out_specs=[pl.BlockSpec((ns, tq, ow), lambda i, j: (i, j, 0))] + w_specs + extra_out,
        out_shape=[jax.ShapeDtypeStruct((nb, s, ow), BF16)]
                  + [jax.ShapeDtypeStruct((n, w.shape[1]), BF16) for w, _, n in casts] + extra_shape,
        scratch_shapes=scratch,
        compiler_params=_params(("parallel", "arbitrary")),
        name="attn_lat" if cache is not None else "attn_ctx",
    )(*args, *cast_weights, *(adaln_tail or ()))
    return outs[0] if len(outs) == 1 else outs


def _lane_bcast(x, k, width=None):
    y = jnp.broadcast_to(x[:, k:k + 1], x.shape)
    reps = (width or LANES) // LANES
    return y if reps == 1 else jnp.concatenate([y] * reps, axis=1)


def _prefix_sum_rows(x):
    n = x.shape[0]
    row = lax.broadcasted_iota(jnp.int32, x.shape, 0)
    shift = 1
    while shift < n:
        x = x + jnp.where(row >= shift, pltpu.roll(x, shift, 0), 0.0)
        shift *= 2
    return x


def _gate_prep(g):
    n = g.shape[0]
    lane = lax.broadcasted_iota(jnp.int32, (n, LANES), 1)
    fpre = pltpu.roll(g, LANES - N_DIRHEAD, 1)
    lf = jnp.minimum(fpre, 0.0) - jnp.log(1.0 + jnp.exp(-jnp.abs(fpre)))
    binc = _prefix_sum_rows(lf)
    tot = binc[n - 1:n, :]
    b = jnp.where(lane < MLSTM_HEADS, binc, tot - binc + lf)
    return b, g - b, tot


def _chunk_setup(b, u, tot, m_prev):
    g_last = jnp.maximum(jnp.max(u, axis=0, keepdims=True), m_prev)
    u2 = u * LOG2E
    w = jnp.exp2(u2 - g_last * LOG2E)
    return u2, (b * LOG2E).T, w.T, tot + g_last, jnp.exp(m_prev - g_last)


def _state_update(vt, k, wt, kk):
    c_new = _dot((vt * wt[kk:kk + 1, :]).astype(BF16), k)
    n_new = _dot(wt[:N_DIRHEAD, :].astype(BF16), k)[kk:kk + 1, :]
    return c_new, n_new


def _dir_weights(s0t, u_b, m_row, *, fwd):
    n = s0t.shape[0]
    r = lax.broadcasted_iota(jnp.int32, (n, n), 0)
    c = lax.broadcasted_iota(jnp.int32, (n, n), 1)
    a = jnp.where((r <= c) if fwd else (r >= c), u_b, -jnp.inf)
    g = jnp.maximum(jnp.max(a, axis=0, keepdims=True), m_row)
    sdt = s0t * jnp.exp2(a - g)
    return sdt.astype(BF16), jnp.sum(sdt, axis=0, keepdims=True), g


def _dir_finish(ht, den, g, bt_row, m_row, inter=None):
    if inter is not None:
        w_inter = jnp.exp2(m_row - g)
        ht = ht + w_inter * inter[0]
        den = den + w_inter * inter[1]
    floor = jnp.exp2(-(bt_row + g))
    return ht * (1.0 / jnp.maximum(jnp.abs(den), floor))


def _mlstm_ctx_kernel(q_ref, k_ref, v_ref, g_ref, ht_ref, c_ref, n_ref, m_ref):
    n = q_ref.shape[1]
    zero_lanes = jnp.zeros((1, LANES), F32)
    zero_row = jnp.zeros((1, n), F32)
    dirs = ((True, 0), (False, MLSTM_HEADS))
    setup = [_chunk_setup(*_gate_prep(g_ref[sq]), zero_lanes) for sq in range(q_ref.shape[0])]
    units = [(sq, hd) for sq in range(q_ref.shape[0]) for hd in range(MLSTM_HEADS)]
    ks = [k_ref[sq, :, hd * MLSTM_DK:(hd + 1) * MLSTM_DK] for sq, hd in units]
    s0ts = [_dot_nt(k, q_ref[sq, :, hd * MLSTM_DK:(hd + 1) * MLSTM_DK]) for k, (sq, hd) in zip(ks, units)]
    vts = [v_ref[sq, :, hd * MLSTM_DV:(hd + 1) * MLSTM_DV].T for sq, hd in units]
    wts = [[_dir_weights(s0t, _lane_bcast(setup[sq][0], off + hd, n), zero_row, fwd=fwd) for fwd, off in dirs]
           for s0t, (sq, hd) in zip(s0ts, units)]
    hts = [[_dot(vt, sdt) for sdt, _, _ in wt] for vt, wt in zip(vts, wts)]
    for (sq, hd), k, vt, wt, ht2 in zip(units, ks, vts, wts, hts):
        _, bt2, w_t, _, _ = setup[sq]
        ht_ref[sq, hd * MLSTM_DV:(hd + 1) * MLSTM_DV, :] = sum(
            _dir_finish(ht_d, den, g, bt2[off + hd:off + hd + 1, :], zero_row)
            for ht_d, (_, den, g), (_, off) in zip(ht2, wt, dirs))
        for _, off in dirs:
            c_ref[sq, off + hd], n_ref[sq, off + hd:off + hd + 1, :] = _state_update(vt, k, w_t, off + hd)
    for sq in range(q_ref.shape[0]):
        m_ref[sq] = setup[sq][3]


class _SeqView:
    def __init__(self, ref, seq_len):
        self.ref, self.seq_len = ref, seq_len
        self.shape = (ref.shape[0] // seq_len, seq_len, ref.shape[1])

    def __getitem__(self, idx):
        sq, rows, cols = idx if isinstance(idx, tuple) else (idx, slice(None), slice(None))
        assert rows == slice(None)
        return self.ref[sq * self.seq_len:(sq + 1) * self.seq_len, cols]


N_CTX_CONSTS = 7


def _ctx_mixer_kernel(*refs, seq_len):
    x_ref, mod_ref = refs[:2]
    consts = refs[2:2 + N_CTX_CONSTS]
    att_ref, ht_ref, c_ref, n_ref, m_ref, ckv_ref, kpet_ref = refs[2 + N_CTX_CONSTS:9 + N_CTX_CONSTS]
    q_s, kc_s, v_s, mq_s, mk_s, mv_s, gate_s = refs[9 + N_CTX_CONSTS:]
    _inproj_kernel(x_ref, mod_ref, *consts, q_s, kc_s, v_s, mq_s, mk_s, mv_s, gate_s, ckv_ref, kpet_ref,
                   rope=False, mod_row0=0, tiles_per_seq=None)
    q3, k3, v3, mq3, mk3, mv3, g3 = (_SeqView(r, seq_len) for r in (q_s, kc_s, v_s, mq_s, mk_s, mv_s, gate_s))
    _attn_kernel(q3, k3, v3, att_ref, has_cache=False, n_cast=0, has_adaln=False)
    _mlstm_ctx_kernel(mq3, mk3, mv3, g3, ht_ref, c_ref, n_ref, m_ref)


def _ctx_mixer(x, mod, gmix, w_in_p, gq, w_uq_p, gkv, w_ukv_p, bg, *, seq_len):
    n_tok = x.shape[0]
    ns, s = CTX_SEQS_PER_STEP, seq_len
    nseq, tm = n_tok // s, ns * s
    dv = MLSTM_HEADS * MLSTM_DV
    consts = (gmix, w_in_p, gq, w_uq_p, gkv, w_ukv_p, bg)
    const_specs = [_const_spec(gmix.shape), _const_spec((SEG_MO[0], w_in_p.shape[1]))]
    const_specs += [_const_spec(a.shape) for a in consts[2:]]
    seq_blk = lambda *dims: pl.BlockSpec((ns,) + dims, lambda i: (i,) + (0,) * len(dims))
    return pl.pallas_call(
        functools.partial(_ctx_mixer_kernel, seq_len=s),
        grid=(nseq // ns,),
        in_specs=[pl.BlockSpec((tm, D_MODEL), lambda i: (i, 0)), pl.BlockSpec(mod.shape, lambda i: (0, 0))]
                 + const_specs,
        out_specs=[seq_blk(s, MLA_HEADS * MLA_V_DIM), seq_blk(dv, s), seq_blk(N_DIRHEAD, MLSTM_DV, MLSTM_DK),
                   seq_blk(N_DIRHEAD, MLSTM_DK), seq_blk(1, LANES),
                   pl.BlockSpec((tm, KV_RANK), lambda i: (i, 0)), seq_blk(ROPE_DIM, s)],
        out_shape=[jax.ShapeDtypeStruct((nseq, s, MLA_HEADS * MLA_V_DIM), BF16),
                   jax.ShapeDtypeStruct((nseq, dv, s), F32),
                   jax.ShapeDtypeStruct((nseq, N_DIRHEAD, MLSTM_DV, MLSTM_DK), F32),
                   jax.ShapeDtypeStruct((nseq, N_DIRHEAD, MLSTM_DK), F32),
                   jax.ShapeDtypeStruct((nseq, 1, LANES), F32),
                   jax.ShapeDtypeStruct((n_tok, KV_RANK), F32),
                   jax.ShapeDtypeStruct((nseq, ROPE_DIM, s), F32)],
        scratch_shapes=[pltpu.VMEM((tm, MLA_HEADS * HEAD_BLOCK), BF16), pltpu.VMEM((tm, MLA_HEADS * HEAD_BLOCK), BF16),
                        pltpu.VMEM((tm, V_WIDTH), BF16), pltpu.VMEM((tm, MLSTM_HEADS * MLSTM_DK), BF16),
                        pltpu.VMEM((tm, MLSTM_HEADS * MLSTM_DK), BF16), pltpu.VMEM((tm, dv), BF16),
                        pltpu.VMEM((tm, LANES), F32)],
        compiler_params=_params(("parallel",)),
        name="ctx_mixer",
    )(x, mod, *consts)


def _mlstm_lat_kernel(qf_ref, kf_ref, vf_ref, bf_ref, uf_ref, qb_ref, kb_ref, vb_ref, bb_ref, ub_ref,
                      c0_ref, n0_ref, m0_ref,
                      ht_ref, c_ref, n_ref, m_ref, *, n_chunks):
    step = pl.program_id(1)

    @pl.when(step == 0)
    def _():
        c_ref[...] = c0_ref[...]
        n_ref[...] = n0_ref[...]
        m_ref[...] = m0_ref[...]
        ht_ref[...] = jnp.zeros_like(ht_ref)

    n = qf_ref.shape[0]
    is_f = lax.broadcasted_iota(jnp.int32, (n, LANES), 1) < MLSTM_HEADS
    b = jnp.where(is_f, bf_ref[...], bb_ref[...])
    u = jnp.where(is_f, uf_ref[...], ub_ref[...])
    tot = jnp.where(is_f[:1], bf_ref[n - 1:n, :], bb_ref[0:1, :])
    m_prev = m_ref[...]
    u2, bt2, w_t, m_new, decay = _chunk_setup(b, u, tot, m_prev)
    m2_prev = m_prev * LOG2E
    units = [(fwd, hd, hd if fwd else MLSTM_HEADS + hd) for fwd in (True, False) for hd in range(MLSTM_HEADS)]
    refs = {True: (qf_ref, kf_ref, vf_ref), False: (qb_ref, kb_ref, vb_ref)}
    qs = [refs[fwd][0][:, hd * MLSTM_DK:(hd + 1) * MLSTM_DK] for fwd, hd, _ in units]
    ks = [refs[fwd][1][:, hd * MLSTM_DK:(hd + 1) * MLSTM_DK] for fwd, hd, _ in units]
    vts = [refs[fwd][2][:, hd * MLSTM_DV:(hd + 1) * MLSTM_DV].T for fwd, hd, _ in units]
    s0ts = [_dot_nt(k, q) for k, q in zip(ks, qs)]
    c_prevs = [c_ref[kk] for _, _, kk in units]
    n_prevs = [n_ref[kk:kk + 1, :] for _, _, kk in units]
    inters = [(_dot_nt(c.astype(BF16), q), _dot_nt(jnp.broadcast_to(nv, (8, MLSTM_DK)).astype(BF16), q)[0:1, :])
              for c, nv, q in zip(c_prevs, n_prevs, qs)]
    m_rows = [_lane_bcast(m2_prev, kk, n) for _, _, kk in units]
    wts = [_dir_weights(s0t, _lane_bcast(u2, kk, n), m_row, fwd=fwd)
           for s0t, m_row, (fwd, _, kk) in zip(s0ts, m_rows, units)]
    hts = [_dot(vt, sdt) for vt, (sdt, _, _) in zip(vts, wts)]
    for (fwd, hd, kk), k, vt, (_, den, g), ht, inter, m_row, c_prev, n_prev in zip(
            units, ks, vts, wts, hts, inters, m_rows, c_prevs, n_prevs):
        chunk = step if fwd else n_chunks - 1 - step
        ht_ref[chunk, hd * MLSTM_DV:(hd + 1) * MLSTM_DV, :] += _dir_finish(ht, den, g, bt2[kk:kk + 1, :], m_row, inter)
        c_new, n_new = _state_update(vt, k, w_t, kk)
        dk = _lane_bcast(decay, kk)
        c_ref[kk] = dk * c_prev + c_new
        n_ref[kk:kk + 1, :] = dk * n_prev + n_new
    m_ref[...] = m_new


def _mlstm_lat(mq, mk, mv, gate_b, gate_u, c0, n0, m0):
    b, s, _ = mq.shape
    lc = MLSTM_CHUNK
    nc = s // lc
    fw = lambda c: pl.BlockSpec((None, lc, c), lambda i, j: (i, j, 0))
    bw = lambda c: pl.BlockSpec((None, lc, c), lambda i, j: (i, nc - 1 - j, 0))
    dk, dv = MLSTM_HEADS * MLSTM_DK, MLSTM_HEADS * MLSTM_DV
    return pl.pallas_call(
        functools.partial(_mlstm_lat_kernel, n_chunks=nc),
        grid=(b, nc),
        in_specs=[fw(dk), fw(dk), fw(dv), fw(LANES), fw(LANES), bw(dk), bw(dk), bw(dv), bw(LANES), bw(LANES),
                  pl.BlockSpec((None, N_DIRHEAD, MLSTM_DV, MLSTM_DK), lambda i, j: (i, 0, 0, 0)),
                  pl.BlockSpec((None, N_DIRHEAD, MLSTM_DK), lambda i, j: (i, 0, 0)),
                  pl.BlockSpec((None, 1, LANES), lambda i, j: (i, 0, 0))],
        out_specs=pl.BlockSpec((nc, dv, lc), lambda i, j: (i, 0, 0)),
        out_shape=jax.ShapeDtypeStruct((b * nc, dv, lc), F32),
        scratch_shapes=[pltpu.VMEM((N_DIRHEAD, MLSTM_DV, MLSTM_DK), F32),
                        pltpu.VMEM((N_DIRHEAD, MLSTM_DK), F32),
                        pltpu.VMEM((1, LANES), F32)],
        compiler_params=_params(("parallel", "arbitrary")),
        name="mlstm_lat",
    )(mq, mk, mv, gate_b, gate_u, mq, mk, mv, gate_b, gate_u, c0, n0, m0)


def _tail_kernel(x_ref, modin_ref, mod_ref, att_ref, hst_ref, gmix_ref, wgate_ref, gml_ref, womla_ref, womlstm_ref,
                 wout_ref, gffn_ref, wfin_ref, wfout_ref, gfin_ref, y_ref, *, mod_row0, tiles_per_seq):
    shift_mix, scale_mix = _mod_chunks(modin_ref, mod_row0, tiles_per_seq)
    gate_mix, shift_ffn, scale_ffn, gate_ffn = _mod_chunks(mod_ref, mod_row0, tiles_per_seq)
    gml = gml_ref[...]
    tm = x_ref.shape[0]
    groups = [slice(r, r + TAIL_ROWS) for r in range(0, tm, TAIL_ROWS)]
    dv = MLSTM_HEADS * MLSTM_DV
    hb = [_mixer_input(x_ref[g, :], gmix_ref[...], shift_mix, scale_mix) for g in groups]
    smo = [_sigmoid(_dot_nt(h, wgate_ref[:dv, :])) for h in hb]
    g_a = [_sigmoid(_dot_nt(h, wgate_ref[dv:dv + D_MODEL, :])) for h in hb]
    g_b = [_sigmoid(_dot_nt(h, wgate_ref[dv + D_MODEL:, :])) for h in hb]
    a = [_dot(att_ref[g, :], womla_ref[...]) for g in groups]
    hm = []
    for gi in range(len(groups)):
        hs = hst_ref[gi].T
        parts = []
        for hd in range(MLSTM_HEADS):
            sl = slice(hd * MLSTM_DV, (hd + 1) * MLSTM_DV)
            parts.append((_rms(hs[:, sl], gml[:, sl]) * smo[gi][:, sl]).astype(BF16))
        hm.append(jnp.concatenate(parts, axis=1))
    bm = [_dot(h, womlstm_ref[...]) for h in hm]
    merged = [(ga_i * ai + gb_i * bi).astype(BF16) for ga_i, gb_i, ai, bi in zip(g_a, g_b, a, bm)]
    x1 = [x_ref[g, :] + gate_mix * _dot(mi, wout_ref[...]) for g, mi in zip(groups, merged)]
    h2 = [(_rms(xi, gffn_ref[...]) * (1.0 + scale_ffn) + shift_ffn).astype(BF16) for xi in x1]
    ga = [_dot(hi, wfin_ref[:, :FFN_HIDDEN]) for hi in h2]
    gu = [_dot(hi, wfin_ref[:, FFN_HIDDEN:]) for hi in h2]
    act = [(gi * _sigmoid(gi) * ui).astype(BF16) for gi, ui in zip(ga, gu)]
    for g, xi, ci in zip(groups, x1, act):
        y_ref[g, :] = _rms(xi + gate_ffn * _dot(ci, wfout_ref[...]), gfin_ref[...])


def _tail(x, mod_in, mod, att, hst, gmix, w_gate, gml, w_o_mla, w_o_mlstm, w_out, gffn, w_ffn_in, w_ffn_out, gfin, *,
          seq_len, latent):
    n_tok = x.shape[0]
    tm = TOKEN_TILE
    assert hst.shape[0] * TAIL_ROWS == n_tok and hst.shape[2] == TAIL_ROWS
    tok = lambda c: pl.BlockSpec((tm, c), lambda i: (i, 0))
    whole = lambda a: pl.BlockSpec(a.shape, lambda i: (0, 0))
    consts = (gmix, w_gate, gml, w_o_mla, w_o_mlstm, w_out, gffn, w_ffn_in, w_ffn_out, gfin)
    return pl.pallas_call(
        functools.partial(_tail_kernel, mod_row0=1 if latent else 0, tiles_per_seq=seq_len // tm if latent else None),
        grid=(n_tok // tm,),
        in_specs=[tok(D_MODEL), whole(mod_in), whole(mod), tok(att.shape[1]),
                  pl.BlockSpec((tm // TAIL_ROWS,) + hst.shape[1:], lambda i: (i, 0, 0))]
                 + [_const_spec(a.shape) for a in consts],
        out_specs=tok(D_MODEL),
        out_shape=jax.ShapeDtypeStruct((n_tok, D_MODEL), F32),
        compiler_params=_params(("parallel",)),
        name="tail_lat" if latent else "tail_ctx",
    )(x, mod_in, mod, att, hst, *consts)


_IN_SIZES = (Q_RANK, KV_RANK, ROPE_DIM, MLSTM_HEADS * MLSTM_DK, MLSTM_HEADS * MLSTM_DK, MLSTM_HEADS * MLSTM_DV,
             4 * MLSTM_HEADS, MLSTM_HEADS * MLSTM_DV, 2 * D_MODEL)
_IN_OFF = tuple(sum(_IN_SIZES[:i]) for i in range(len(_IN_SIZES) + 1))
PACK_ROWS = LANES
_PAD_AFTER = tuple((_SEG_OFF[i + 1] // PACK_ROWS, _SEG_WIDTHS[i] - _IN_SIZES[i], _IN_SIZES[i] % PACK_ROWS)
                   for i in range(len(_IN_SIZES)) if _SEG_WIDTHS[i] != _IN_SIZES[i])


PACK_PIECES = 8
N_PACK_STEPS = pl.cdiv(SEG_MO[0] // PACK_ROWS, PACK_PIECES)
_N_IN = _IN_OFF[-1]


def _pack_src_row(p):
    src = p * PACK_ROWS
    for first_piece_after, pad, _ in _PAD_AFTER:
        src = src - jnp.where(p >= first_piece_after, pad, 0)
    src = jnp.minimum(src, _N_IN - PACK_ROWS)
    return pl.multiple_of(src, math.gcd(PACK_ROWS, _N_IN, *(pad for _, pad, _ in _PAD_AFTER)))


def _pack_in_kernel(*refs):
    w_refs, o_ref = refs[:PACK_PIECES], refs[PACK_PIECES]
    row = lax.broadcasted_iota(jnp.int32, w_refs[0].shape, 0)
    for j, w_ref in enumerate(w_refs):
        p = pl.program_id(0) * PACK_PIECES + j
        valid = jnp.where(p < N_IN_PACKED // PACK_ROWS, PACK_ROWS, 0)
        for first_piece_after, _, width in _PAD_AFTER:
            valid = jnp.where(p == first_piece_after - 1, width, valid)
        o_ref[j * PACK_ROWS:(j + 1) * PACK_ROWS, :] = jnp.where(row < valid, w_ref[...], 0.0).astype(BF16)


def _pack_in(w_in_t):
    n, k = w_in_t.shape
    piece = lambda j: pl.BlockSpec((pl.Element(PACK_ROWS), pl.Element(k)),
                                   lambda i: (_pack_src_row(i * PACK_PIECES + j), 0))
    return pl.pallas_call(
        _pack_in_kernel,
        grid=(N_PACK_STEPS,),
        in_specs=[piece(j) for j in range(PACK_PIECES)],
        out_specs=pl.BlockSpec((PACK_PIECES * PACK_ROWS, k), lambda i: (i, 0)),
        out_shape=jax.ShapeDtypeStruct((N_PACK_STEPS * PACK_PIECES * PACK_ROWS, k), BF16),
        compiler_params=_params(("parallel",)),
        name="pack_in",
    )(*([w_in_t] * PACK_PIECES))


def _pack_small(w_uq, w_ukv, b_gates):
    qd = NOPE_DIM + ROPE_DIM
    w_uq_p = jnp.pad(w_uq.reshape(Q_RANK, MLA_HEADS, qd), ((0, 0), (0, 0), (0, HEAD_BLOCK - qd)))
    w_uq_p = w_uq_p.reshape(Q_RANK, MLA_HEADS * HEAD_BLOCK).astype(BF16)
    kv = w_ukv.reshape(KV_RANK, MLA_HEADS, NOPE_DIM + MLA_V_DIM)
    wk = jnp.pad(kv[:, :, :NOPE_DIM], ((0, 0), (0, 0), (0, HEAD_BLOCK - NOPE_DIM))).reshape(KV_RANK, -1)
    wv = kv[:, :, NOPE_DIM:].reshape(KV_RANK, -1)
    w_ukv_p = jnp.concatenate([wk, wv], axis=1).astype(BF16)
    bg = jnp.pad(b_gates, ((0, 0), (0, LANES - b_gates.shape[1])))
    return w_uq_p, w_ukv_p, bg


def _rope_tables(n_tokens):
    pos = np.arange(n_tokens)
    row = (pos // GRID_W).astype(np.float64)[:, None]
    col = (pos % GRID_W).astype(np.float64)[:, None]
    half = ROPE_DIM // 2
    inv = (np.float32(ROPE_BASE) ** (-np.arange(0, half, 2, dtype=np.float32) / np.float32(half))).astype(np.float64)
    r = np.arange(LANES) - ROPE_LANE0
    in_rope = (r >= 0) & (r < ROPE_DIM)
    rr = np.clip(r, 0, ROPE_DIM - 1)
    freq = inv[rr % (half // 2)][None, :]
    ang = np.where((rr // half == 0)[None, :], row * freq, col * freq).astype(np.float32).astype(np.float64)
    first = (rr % half) < (half // 2)
    cos = np.where(in_rope[None, :], np.cos(ang), 1.0)
    sin = np.sin(ang)
    sin_lo = np.where((in_rope & first)[None, :], -sin, 0.0)
    sin_hi = np.where((in_rope & ~first)[None, :], sin, 0.0)
    return tuple(jnp.asarray(t, dtype=F32) for t in (cos, sin_lo, sin_hi))


def kernel(x_prompt, x_sample, cache_ckv, cache_krope, state_C, state_n, state_m, c, c_ctx, w_mod, b_mod, g_norm_mix,
           w_in, b_gates, g_q_norm, w_uq, g_kv_norm, w_ukv, g_mlstm_norm, w_o_mla, w_o_mlstm, w_out, g_norm_ffn,
           w_ffn_in, w_ffn_out, g_final):
    bp, sp, d = x_prompt.shape
    bs, ss, _ = x_sample.shape
    layer = 0
    assert w_mod.shape[0] == 1 and sp == MLSTM_CHUNK and ss % MLSTM_CHUNK == 0

    adaln_args = (c_ctx[None, :], c, w_mod[layer], b_mod[layer][None, :])
    mod_in = _adaln(*adaln_args)

    w_in_t = w_in[layer].T
    w_in_p = _pack_in(w_in_t)
    w_uq_p, w_ukv_p, bg = _pack_small(w_uq[layer], w_ukv[layer], b_gates[layer][None, :])
    row = lambda g: g[layer][None, :]
    shared_in = (row(g_norm_mix), w_in_p, row(g_q_norm), w_uq_p, row(g_kv_norm), w_ukv_p, bg)
    seq3 = lambda a, b_, s_: a.reshape(b_, s_, a.shape[-1])

    xl = x_sample.reshape(bs * ss, d)
    q, kc, v, mq_l, mk_l, mv_l, gb_l, gu_l = _inproj(xl, mod_in, _rope_tables(ss), *shared_in, seq_len=ss)
    krope_blk = jnp.pad(cache_krope[:, layer], ((0, 0), (0, 0), (ROPE_LANE0, LANES - ROPE_LANE0 - ROPE_DIM)))
    cache = (cache_ckv[:, layer], krope_blk, w_ukv_p)
    gate_cols = (w_in_t, _IN_OFF[7], _IN_OFF[9] - _IN_OFF[7])
    att_l, wgate_b, womla_b, womlstm_b, wout_b, wfin_b, wfout_b, mod_tail = _attention(
        seq3(q, bs, ss), seq3(kc, bs, ss), seq3(v, bs, ss), cache,
        cast_weights=(gate_cols, w_o_mla[layer], w_o_mlstm[layer], w_out[layer], w_ffn_in[layer], w_ffn_out[layer]),
        adaln_tail=adaln_args)
    tail_w = (row(g_norm_mix), wgate_b, row(g_mlstm_norm), womla_b, womlstm_b, wout_b, row(g_norm_ffn), wfin_b, wfout_b,
              g_final[None, :])

    xc = x_prompt.reshape(bp * sp, d)
    att, hs, new_c, new_n, new_m, ckv, kpe = _ctx_mixer(xc, mod_in, *shared_in, seq_len=sp)
    y_prompt = _tail(xc, mod_in, mod_tail, att.reshape(bp * sp, -1), hs, *tail_w,
                     seq_len=sp, latent=False).reshape(bp, sp, d)
    new_ckv = ckv.reshape(bp, 1, sp, KV_RANK)
    new_krope = jnp.swapaxes(kpe, 1, 2).reshape(bp, 1, sp, ROPE_DIM)
    new_C = new_c.reshape(bp, 1, 2, MLSTM_HEADS, MLSTM_DV, MLSTM_DK)
    new_N = new_n.reshape(bp, 1, 2, MLSTM_HEADS, MLSTM_DK)
    new_M = new_m[:, 0, :N_DIRHEAD].reshape(bp, 1, 2, MLSTM_HEADS)

    c0 = state_C[:, layer].reshape(bs, N_DIRHEAD, MLSTM_DV, MLSTM_DK)
    n0 = state_n[:, layer].reshape(bs, N_DIRHEAD, MLSTM_DK)
    m0 = jnp.pad(state_m[:, layer].reshape(bs, 1, N_DIRHEAD), ((0, 0), (0, 0), (0, LANES - N_DIRHEAD)))
    hs = _mlstm_lat(seq3(mq_l, bs, ss), seq3(mk_l, bs, ss), seq3(mv_l, bs, ss), seq3(gb_l, bs, ss), seq3(gu_l, bs, ss),
                    c0, n0, m0)
    y_sample = _tail(xl, mod_in, mod_tail, att_l.reshape(bs * ss, -1), hs, *tail_w,
                     seq_len=ss, latent=True).reshape(bs, ss, d)
    return (y_prompt, y_sample, new_ckv, new_krope, new_C, new_N, new_M)
```

```python
import functools
import math

import jax
import jax.numpy as jnp
import numpy as np
from jax import lax
from jax.experimental import pallas as pl
from jax.experimental.pallas import tpu as pltpu

F32 = jnp.float32
BF16 = jnp.bfloat16

D_MODEL = 1024
GRID_W = 64
MLA_HEADS = 8
Q_RANK = 384
KV_RANK = 256
NOPE_DIM = 64
ROPE_DIM = 32
MLA_V_DIM = 64
ROPE_BASE = 10000.0
MLA_SCALE = (NOPE_DIM + ROPE_DIM) ** -0.5
MLSTM_HEADS = 4
MLSTM_DK = 128
MLSTM_DV = 256
FFN_HIDDEN = ((8 * D_MODEL // 3 + 255) // 256) * 256
EPS = 1e-6

LANES = 128
HEAD_BLOCK = LANES
ROPE_LANE0 = NOPE_DIM
N_DIRHEAD = 2 * MLSTM_HEADS
MLSTM_CHUNK = 256
LOG2E = math.log2(math.e)
Q_PRESCALE = MLA_SCALE * LOG2E
V_SLAB = 2 * LANES
V_WIDTH = (MLA_HEADS // 2) * V_SLAB
TOKEN_TILE = 512
TAIL_ROWS = 256
INPROJ_ROWS = TOKEN_TILE
Q_TILE = 256
CTX_SEQS_PER_STEP = 4
VMEM_LIMIT = 56 * 1024 * 1024

_SEG_WIDTHS = (Q_RANK, KV_RANK, LANES, MLSTM_HEADS * MLSTM_DK, MLSTM_HEADS * MLSTM_DK,
               MLSTM_HEADS * MLSTM_DV, LANES, MLSTM_HEADS * MLSTM_DV, 2 * D_MODEL)
_SEG_OFF = tuple(sum(_SEG_WIDTHS[:i]) for i in range(len(_SEG_WIDTHS) + 1))
SEG_Q, SEG_KV, SEG_KPE, SEG_MQ, SEG_MK, SEG_MV, SEG_GATE, SEG_MO, SEG_BR = (
    (_SEG_OFF[i], _SEG_OFF[i + 1]) for i in range(9))
N_IN_PACKED = _SEG_OFF[-1]


def _dot(a, b):
    return jnp.dot(a, b, preferred_element_type=F32)


def _dot_nt(a, b):
    return lax.dot_general(a, b, (((1,), (1,)), ((), ())), preferred_element_type=F32)


def _rms(x, g):
    ms = jnp.mean(x * x, axis=-1, keepdims=True)
    return x * lax.rsqrt(ms + EPS) * g


def _sigmoid(x):
    return 1.0 / (1.0 + jnp.exp(-x))


def _const_spec(shape):
    nd = len(shape)
    return pl.BlockSpec(shape, lambda *_: (0,) * nd, pipeline_mode=pl.Buffered(1))


def _mod_chunks(mod_ref, row0, tiles_per_seq):
    row = row0 if tiles_per_seq is None else row0 + pl.program_id(0) // tiles_per_seq
    m = mod_ref[pl.ds(row, 1), :]
    return [m[:, k * D_MODEL:(k + 1) * D_MODEL] for k in range(m.shape[1] // D_MODEL)]


def _params(sem, flags=None, vmem_mib=None):
    limit = VMEM_LIMIT if vmem_mib is None else vmem_mib * 1024 * 1024
    return pltpu.CompilerParams(dimension_semantics=sem, vmem_limit_bytes=limit, flags=flags)


N_MOD_MIXER = 2
N_MOD_TAIL = 4


COND_ROWS = 8


def _adaln_block(cctx_ref, c_ref, w_ref, b_ref, o_ref):
    pad = jnp.zeros((COND_ROWS - 1 - c_ref.shape[0], D_MODEL), F32)
    c = jnp.concatenate([cctx_ref[...], c_ref[...], pad], axis=0)
    s = c * _sigmoid(c)
    o_ref[...] = _dot(s.astype(BF16), w_ref[...].astype(BF16)) + b_ref[...]


def _adaln(c_ctx, c, w_mod, b_mod):
    tn = D_MODEL
    return pl.pallas_call(
        _adaln_block,
        grid=(N_MOD_MIXER,),
        in_specs=[pl.BlockSpec(c_ctx.shape, lambda j: (0, 0)),
                  pl.BlockSpec(c.shape, lambda j: (0, 0)),
                  pl.BlockSpec((D_MODEL, tn), lambda j: (0, j)),
                  pl.BlockSpec((1, tn), lambda j: (0, j))],
        out_specs=pl.BlockSpec((COND_ROWS, tn), lambda j: (0, j)),
        out_shape=jax.ShapeDtypeStruct((COND_ROWS, N_MOD_MIXER * D_MODEL), F32),
        compiler_params=_params(("arbitrary",)),
        name="adaln",
    )(c_ctx, c, w_mod, b_mod)


def _rope_block(x, cos, sin_lo, sin_hi):
    return x * cos + pltpu.roll(x, LANES - 8, 1) * sin_lo + pltpu.roll(x, 8, 1) * sin_hi


def _store_kv(kvn, kp, kc_ref, v_ref, g=slice(None)):
    for hd in range(MLA_HEADS):
        sl = slice(hd * HEAD_BLOCK, (hd + 1) * HEAD_BLOCK)
        kc_ref[g, sl] = (kvn[:, sl] + kp).astype(BF16)
    v0 = MLA_HEADS * HEAD_BLOCK
    ones = jnp.ones((kvn.shape[0], LANES), BF16)
    for pair in range(MLA_HEADS // 2):
        v_ref[g, pair * V_SLAB:pair * V_SLAB + LANES] = kvn[:, v0 + pair * LANES:v0 + (pair + 1) * LANES].astype(BF16)
        v_ref[g, pair * V_SLAB + LANES:(pair + 1) * V_SLAB] = ones


def _mixer_input(x, g, shift, scale):
    return (_rms(x, g) * (1.0 + scale) + shift).astype(BF16)


def _inproj_kernel(*refs, rope, mod_row0, tiles_per_seq):
    if rope:
        (x_ref, mod_ref, cos_ref, slo_ref, shi_ref, gmix_ref, win_ref, gq_ref, wuq_ref, gkv_ref, wukv_ref,
         bg_ref, q_ref, kc_ref, v_ref, mq_ref, mk_ref, mv_ref, gb_ref, gu_ref) = refs
    else:
        (x_ref, mod_ref, gmix_ref, win_ref, gq_ref, wuq_ref, gkv_ref, wukv_ref,
         bg_ref, q_ref, kc_ref, v_ref, mq_ref, mk_ref, mv_ref, gate_ref, ckv_ref, kpet_ref) = refs
    shift, scale = _mod_chunks(mod_ref, mod_row0, tiles_per_seq)
    for r0 in range(0, x_ref.shape[0], INPROJ_ROWS):
        g = slice(r0, r0 + INPROJ_ROWS)
        if rope:
            cos, slo, shi = cos_ref[g, :], slo_ref[g, :], shi_ref[g, :]
        hb = _mixer_input(x_ref[g, :], gmix_ref[...], shift, scale)

        def proj(seg):
            return _dot_nt(hb, win_ref[seg[0]:seg[1], :])

        low = proj((SEG_Q[0], SEG_KPE[1]))
        qn = _rms(low[:, SEG_Q[0]:SEG_Q[1]], gq_ref[...]).astype(BF16)
        ckv = _rms(low[:, SEG_KV[0]:SEG_KV[1]], gkv_ref[...])
        zkpe = low[:, SEG_KPE[0]:SEG_KPE[1]]
        gates = proj(SEG_GATE) + bg_ref[...]
        if rope:
            for c0 in range(0, INPROJ_ROWS, MLSTM_CHUNK):
                b, u, _ = _gate_prep(gates[c0:c0 + MLSTM_CHUNK, :])
                gb_ref[r0 + c0:r0 + c0 + MLSTM_CHUNK, :] = b
                gu_ref[r0 + c0:r0 + c0 + MLSTM_CHUNK, :] = u
        else:
            gate_ref[g, :] = gates

        q = _dot(qn, wuq_ref[...])
        for hd in range(MLA_HEADS):
            sl = slice(hd * HEAD_BLOCK, (hd + 1) * HEAD_BLOCK)
            qh = q[:, sl]
            if rope:
                qh = _rope_block(qh, cos, slo, shi)
            q_ref[g, sl] = (qh * Q_PRESCALE).astype(BF16)

        if not rope:
            ckv_ref[g, :] = ckv
            kpe_t = zkpe.T
            seq = kpet_ref.shape[2]
            for sq in range(INPROJ_ROWS // seq):
                kpet_ref[r0 // seq + sq] = kpe_t[:ROPE_DIM, sq * seq:(sq + 1) * seq]
        kp = pltpu.roll(zkpe, ROPE_LANE0, 1)
        if rope:
            kp = _rope_block(kp, cos, slo, shi)
        kvn = _dot(ckv.astype(BF16), wukv_ref[...])
        _store_kv(kvn, kp, kc_ref, v_ref, g)

        mv_ref[g, :] = proj(SEG_MV).astype(BF16)
        mk_ref[g, :] = (proj(SEG_MK) * (MLSTM_DK ** -0.5)).astype(BF16)
        mq_ref[g, :] = proj(SEG_MQ).astype(BF16)


def _inproj(x, mod, rope_tabs, gmix, w_in_p, gq, w_uq_p, gkv, w_ukv_p, bg, *, seq_len):
    n_tok = x.shape[0]
    tm = TOKEN_TILE
    tiles_per_seq = seq_len // tm
    rope = rope_tabs is not None
    tok = lambda c: pl.BlockSpec((tm, c), lambda i: (i, 0))
    in_specs = [tok(D_MODEL), pl.BlockSpec(mod.shape, lambda i: (0, 0))]
    args = [x, mod]
    if rope:
        in_specs += [pl.BlockSpec((tm, LANES), lambda i: (i % tiles_per_seq, 0))] * 3
        args += list(rope_tabs)
    win_rows = SEG_MO[0]
    in_specs += [_const_spec(gmix.shape), _const_spec((win_rows, w_in_p.shape[1]))]
    in_specs += [_const_spec(a.shape) for a in (gq, w_uq_p, gkv, w_ukv_p, bg)]
    args += [gmix, w_in_p, gq, w_uq_p, gkv, w_ukv_p, bg]
    widths = [(MLA_HEADS * HEAD_BLOCK, BF16), (MLA_HEADS * HEAD_BLOCK, BF16), (V_WIDTH, BF16),
              (MLSTM_HEADS * MLSTM_DK, BF16), (MLSTM_HEADS * MLSTM_DK, BF16), (MLSTM_HEADS * MLSTM_DV, BF16),
              (LANES, F32)]
    if rope:
        widths.append((LANES, F32))
    out_specs = [tok(c) for c, _ in widths]
    out_shape = [jax.ShapeDtypeStruct((n_tok, c), dt) for c, dt in widths]
    if not rope:
        out_specs += [tok(KV_RANK), pl.BlockSpec((tm // seq_len, ROPE_DIM, seq_len), lambda i: (i, 0, 0))]
        out_shape += [jax.ShapeDtypeStruct((n_tok, KV_RANK), F32),
                      jax.ShapeDtypeStruct((n_tok // seq_len, ROPE_DIM, seq_len), F32)]
    return pl.pallas_call(
        functools.partial(_inproj_kernel, rope=rope, mod_row0=1 if rope else 0,
                          tiles_per_seq=tiles_per_seq if rope else None),
        grid=(n_tok // tm,),
        in_specs=in_specs,
        out_specs=out_specs,
        out_shape=out_shape,
        compiler_params=_params(("parallel",)),
        name="inproj_lat" if rope else "inproj_ctx",
    )(*args)


def _attn_kernel(*refs, has_cache, n_cast, has_adaln):
    n_in = 6 if has_cache else 3
    if has_cache:
        q_ref, k_ref, v_ref, ckv_ref, kp_ref, wukv_ref = refs[:n_in]
        kc_ref, vc_ref = refs[-2:]

        @pl.when(pl.program_id(1) == 0)
        def _():
            kvn = _dot(ckv_ref[0].astype(BF16), wukv_ref[...])
            _store_kv(kvn, kp_ref[0], kc_ref.at[0], vc_ref.at[0])
    else:
        q_ref, k_ref, v_ref = refs[:n_in]
    n_all_in = n_in + n_cast + (4 if has_adaln else 0)
    o_ref = refs[n_all_in]
    for w_ref, wb_ref in zip(refs[n_in:n_in + n_cast], refs[n_all_in + 1:n_all_in + 1 + n_cast]):
        wb_ref[...] = w_ref[...].astype(BF16)
    if has_adaln:
        _adaln_block(*refs[n_in + n_cast:n_all_in], refs[n_all_in + 1 + n_cast])
    tq = q_ref.shape[1]
    lane = lax.broadcasted_iota(jnp.int32, (tq, LANES), 1)
    for sq in range(q_ref.shape[0]):
        hsl = [slice(hd * HEAD_BLOCK, (hd + 1) * HEAD_BLOCK) for hd in range(MLA_HEADS)]
        vsl = [slice(hd // 2 * V_SLAB, (hd // 2 + 1) * V_SLAB) for hd in range(MLA_HEADS)]
        s = [_dot_nt(q_ref[sq, :, sl], k_ref[sq, :, sl]) for sl in hsl]
        m = [jnp.max(si, axis=-1, keepdims=True) for si in s]
        if has_cache:
            sc = [_dot_nt(q_ref[sq, :, sl], kc_ref[sq, :, sl]) for sl in hsl]
            m = [jnp.maximum(mi, jnp.max(ci, axis=-1, keepdims=True)) for mi, ci in zip(m, sc)]
        p = [jnp.exp2(si - mi).astype(BF16) for si, mi in zip(s, m)]
        o = [_dot(pi, v_ref[sq, :, sl]) for pi, sl in zip(p, vsl)]
        if has_cache:
            pc = [jnp.exp2(ci - mi).astype(BF16) for ci, mi in zip(sc, m)]
            o = [oi + _dot(pi, vc_ref[sq, :, sl]) for oi, pi, sl in zip(o, pc, vsl)]
        outs = [oi[:, :LANES] * (1.0 / oi[:, LANES:]) for oi in o]
        for pair in range(MLA_HEADS // 2):
            o_ref[sq, :, pair * LANES:(pair + 1) * LANES] = jnp.where(
                lane < MLA_V_DIM, outs[2 * pair], outs[2 * pair + 1]).astype(BF16)


def _attention(q, k, v, cache=None, cast_weights=(), adaln_tail=None):
    nb, s, _ = q.shape
    tq = Q_TILE
    ns = 1
    b = nb // ns
    n_steps = b * (s // tq)
    kw, vw, ow = MLA_HEADS * HEAD_BLOCK, V_WIDTH, MLA_HEADS * MLA_V_DIM
    in_specs = [pl.BlockSpec((ns, tq, kw), lambda i, j: (i, j, 0)),
                pl.BlockSpec((ns, s, kw), lambda i, j: (i, 0, 0)),
                pl.BlockSpec((ns, s, vw), lambda i, j: (i, 0, 0))]
    args = [q, k, v]
    scratch = []
    if cache is not None:
        past = cache[0].shape[1]
        in_specs += [pl.BlockSpec((ns, past, KV_RANK), lambda i, j: (i, 0, 0)),
                     pl.BlockSpec((ns, past, LANES), lambda i, j: (i, 0, 0)),
                     _const_spec(cache[2].shape)]
        args += list(cache)
        scratch = [pltpu.VMEM((ns, past, kw), BF16), pltpu.VMEM((ns, past, vw), BF16)]
    step = lambda i, j: i * (s // tq) + j
    casts = [w if isinstance(w, tuple) else (w, 0, w.shape[0]) for w in cast_weights]
    cast_weights = [w for w, _, _ in casts]

    def cast_in_spec(w, row0, n_rows):
        rows = n_rows // n_steps
        if (row0, n_rows) == (0, w.shape[0]):
            return pl.BlockSpec((rows, w.shape[1]), lambda i, j: (step(i, j), 0))
        return pl.BlockSpec((pl.Element(rows), pl.Element(w.shape[1])),
                            lambda i, j: (pl.multiple_of(row0 + step(i, j) * rows, math.gcd(row0, rows)), 0))

    w_in_specs = [cast_in_spec(*c) for c in casts]
    w_specs = [pl.BlockSpec((n // n_steps, w.shape[1]), lambda i, j: (step(i, j), 0)) for w, _, n in casts]
    extra_in, extra_out, extra_shape = [], [], []
    if adaln_tail is not None:
        tn = N_MOD_TAIL * D_MODEL // n_steps
        col0 = N_MOD_MIXER * D_MODEL // tn
        extra_in = [pl.BlockSpec(adaln_tail[0].shape, lambda i, j: (0, 0)),
                    pl.BlockSpec(adaln_tail[1].shape, lambda i, j: (0, 0)),
                    pl.BlockSpec((D_MODEL, tn), lambda i, j: (0, col0 + step(i, j))),
                    pl.BlockSpec((1, tn), lambda i, j: (0, col0 + step(i, j)))]
        extra_out = [pl.BlockSpec((COND_ROWS, tn), lambda i, j: (0, step(i, j)))]
        extra_shape = [jax.ShapeDtypeStruct((COND_ROWS, N_MOD_TAIL * D_MODEL), F32)]
    outs = pl.pallas_call(
        functools.partial(_attn_kernel, has_cache=cache is not None, n_cast=len(cast_weights),
                          has_adaln=adaln_tail is not None),
        grid=(b, s // tq),
        in_specs=in_specs + w_in_specs + extra_in,
        out_specs=[pl.BlockSpec((ns, tq, ow), lambda i, j: (i, j, 0))] + w_specs + extra_out,
        out_shape=[jax.ShapeDtypeStruct((nb, s, ow), BF16)]
                  + [jax.ShapeDtypeStruct((n, w.shape[1]), BF16) for w, _, n in casts] + extra_shape,
        scratch_shapes=scratch,
        compiler_params=_params(("parallel", "arbitrary")),
        name="attn_lat" if cache is not None else "attn_ctx",
    )(*args, *cast_weights, *(adaln_tail or ()))
    return outs[0] if len(outs) == 1 else outs


def _lane_bcast(x, k, width=None):
    y = jnp.broadcast_to(x[:, k:k + 1], x.shape)
    reps = (width or LANES) // LANES
    return y if reps == 1 else jnp.concatenate([y] * reps, axis=1)


def _prefix_sum_rows(x):
    n = x.shape[0]
    row = lax.broadcasted_iota(jnp.int32, x.shape, 0)
    shift = 1
    while shift < n:
        x = x + jnp.where(row >= shift, pltpu.roll(x, shift, 0), 0.0)
        shift *= 2
    return x


def _gate_prep(g):
    n = g.shape[0]
    lane = lax.broadcasted_iota(jnp.int32, (n, LANES), 1)
    fpre = pltpu.roll(g, LANES - N_DIRHEAD, 1)
    lf = jnp.minimum(fpre, 0.0) - jnp.log(1.0 + jnp.exp(-jnp.abs(fpre)))
    binc = _prefix_sum_rows(lf)
    tot = binc[n - 1:n, :]
    b = jnp.where(lane < MLSTM_HEADS, binc, tot - binc + lf)
    return b, g - b, tot


def _chunk_setup(b, u, tot, m_prev):
    g_last = jnp.maximum(jnp.max(u, axis=0, keepdims=True), m_prev)
    u2 = u * LOG2E
    w = jnp.exp2(u2 - g_last * LOG2E)
    return u2, (b * LOG2E).T, w.T, tot + g_last, jnp.exp(m_prev - g_last)


def _state_update(vt, k, wt, kk):
    c_new = _dot((vt * wt[kk:kk + 1, :]).astype(BF16), k)
    n_new = _dot(wt[:N_DIRHEAD, :].astype(BF16), k)[kk:kk + 1, :]
    return c_new, n_new


def _dir_weights(s0t, u_b, m_row, *, fwd):
    n = s0t.shape[0]
    r = lax.broadcasted_iota(jnp.int32, (n, n), 0)
    c = lax.broadcasted_iota(jnp.int32, (n, n), 1)
    a = jnp.where((r <= c) if fwd else (r >= c), u_b, -jnp.inf)
    g = jnp.maximum(jnp.max(a, axis=0, keepdims=True), m_row)
    sdt = s0t * jnp.exp2(a - g)
    return sdt.astype(BF16), jnp.sum(sdt, axis=0, keepdims=True), g


def _dir_finish(ht, den, g, bt_row, m_row, inter=None):
    if inter is not None:
        w_inter = jnp.exp2(m_row - g)
        ht = ht + w_inter * inter[0]
        den = den + w_inter * inter[1]
    floor = jnp.exp2(-(bt_row + g))
    return ht * (1.0 / jnp.maximum(jnp.abs(den), floor))


def _mlstm_ctx_kernel(q_ref, k_ref, v_ref, g_ref, ht_ref, c_ref, n_ref, m_ref):
    n = q_ref.shape[1]
    zero_lanes = jnp.zeros((1, LANES), F32)
    zero_row = jnp.zeros((1, n), F32)
    dirs = ((True, 0), (False, MLSTM_HEADS))
    setup = [_chunk_setup(*_gate_prep(g_ref[sq]), zero_lanes) for sq in range(q_ref.shape[0])]
    units = [(sq, hd) for sq in range(q_ref.shape[0]) for hd in range(MLSTM_HEADS)]
    ks = [k_ref[sq, :, hd * MLSTM_DK:(hd + 1) * MLSTM_DK] for sq, hd in units]
    s0ts = [_dot_nt(k, q_ref[sq, :, hd * MLSTM_DK:(hd + 1) * MLSTM_DK]) for k, (sq, hd) in zip(ks, units)]
    vts = [v_ref[sq, :, hd * MLSTM_DV:(hd + 1) * MLSTM_DV].T for sq, hd in units]
    wts = [[_dir_weights(s0t, _lane_bcast(setup[sq][0], off + hd, n), zero_row, fwd=fwd) for fwd, off in dirs]
           for s0t, (sq, hd) in zip(s0ts, units)]
    hts = [[_dot(vt, sdt) for sdt, _, _ in wt] for vt, wt in zip(vts, wts)]
    for (sq, hd), k, vt, wt, ht2 in zip(units, ks, vts, wts, hts):
        _, bt2, w_t, _, _ = setup[sq]
        ht_ref[sq, hd * MLSTM_DV:(hd + 1) * MLSTM_DV, :] = sum(
            _dir_finish(ht_d, den, g, bt2[off + hd:off + hd + 1, :], zero_row)
            for ht_d, (_, den, g), (_, off) in zip(ht2, wt, dirs))
        for _, off in dirs:
            c_ref[sq, off + hd], n_ref[sq, off + hd:off + hd + 1, :] = _state_update(vt, k, w_t, off + hd)
    for sq in range(q_ref.shape[0]):
        m_ref[sq] = setup[sq][3]


class _SeqView:
    def __init__(self, ref, seq_len):
        self.ref, self.seq_len = ref, seq_len
        self.shape = (ref.shape[0] // seq_len, seq_len, ref.shape[1])

    def __getitem__(self, idx):
        sq, rows, cols = idx if isinstance(idx, tuple) else (idx, slice(None), slice(None))
        assert rows == slice(None)
        return self.ref[sq * self.seq_len:(sq + 1) * self.seq_len, cols]


N_CTX_CONSTS = 7


def _ctx_mixer_kernel(*refs, seq_len):
    x_ref, mod_ref = refs[:2]
    consts = refs[2:2 + N_CTX_CONSTS]
    att_ref, ht_ref, c_ref, n_ref, m_ref, ckv_ref, kpet_ref = refs[2 + N_CTX_CONSTS:9 + N_CTX_CONSTS]
    q_s, kc_s, v_s, mq_s, mk_s, mv_s, gate_s = refs[9 + N_CTX_CONSTS:]
    _inproj_kernel(x_ref, mod_ref, *consts, q_s, kc_s, v_s, mq_s, mk_s, mv_s, gate_s, ckv_ref, kpet_ref,
                   rope=False, mod_row0=0, tiles_per_seq=None)
    q3, k3, v3, mq3, mk3, mv3, g3 = (_SeqView(r, seq_len) for r in (q_s, kc_s, v_s, mq_s, mk_s, mv_s, gate_s))
    _attn_kernel(q3, k3, v3, att_ref, has_cache=False, n_cast=0, has_adaln=False)
    _mlstm_ctx_kernel(mq3, mk3, mv3, g3, ht_ref, c_ref, n_ref, m_ref)


def _ctx_mixer(x, mod, gmix, w_in_p, gq, w_uq_p, gkv, w_ukv_p, bg, *, seq_len):
    n_tok = x.shape[0]
    ns, s = CTX_SEQS_PER_STEP, seq_len
    nseq, tm = n_tok // s, ns * s
    dv = MLSTM_HEADS * MLSTM_DV
    consts = (gmix, w_in_p, gq, w_uq_p, gkv, w_ukv_p, bg)
    const_specs = [_const_spec(gmix.shape), _const_spec((SEG_MO[0], w_in_p.shape[1]))]
    const_specs += [_const_spec(a.shape) for a in consts[2:]]
    seq_blk = lambda *dims: pl.BlockSpec((ns,) + dims, lambda i: (i,) + (0,) * len(dims))
    return pl.pallas_call(
        functools.partial(_ctx_mixer_kernel, seq_len=s),
        grid=(nseq // ns,),
        in_specs=[pl.BlockSpec((tm, D_MODEL), lambda i: (i, 0)), pl.BlockSpec(mod.shape, lambda i: (0, 0))]
                 + const_specs,
        out_specs=[seq_blk(s, MLA_HEADS * MLA_V_DIM), seq_blk(dv, s), seq_blk(N_DIRHEAD, MLSTM_DV, MLSTM_DK),
                   seq_blk(N_DIRHEAD, MLSTM_DK), seq_blk(1, LANES),
                   pl.BlockSpec((tm, KV_RANK), lambda i: (i, 0)), seq_blk(ROPE_DIM, s)],
        out_shape=[jax.ShapeDtypeStruct((nseq, s, MLA_HEADS * MLA_V_DIM), BF16),
                   jax.ShapeDtypeStruct((nseq, dv, s), F32),
                   jax.ShapeDtypeStruct((nseq, N_DIRHEAD, MLSTM_DV, MLSTM_DK), F32),
                   jax.ShapeDtypeStruct((nseq, N_DIRHEAD, MLSTM_DK), F32),
                   jax.ShapeDtypeStruct((nseq, 1, LANES), F32),
                   jax.ShapeDtypeStruct((n_tok, KV_RANK), F32),
                   jax.ShapeDtypeStruct((nseq, ROPE_DIM, s), F32)],
        scratch_shapes=[pltpu.VMEM((tm, MLA_HEADS * HEAD_BLOCK), BF16), pltpu.VMEM((tm, MLA_HEADS * HEAD_BLOCK), BF16),
                        pltpu.VMEM((tm, V_WIDTH), BF16), pltpu.VMEM((tm, MLSTM_HEADS * MLSTM_DK), BF16),
                        pltpu.VMEM((tm, MLSTM_HEADS * MLSTM_DK), BF16), pltpu.VMEM((tm, dv), BF16),
                        pltpu.VMEM((tm, LANES), F32)],
        compiler_params=_params(("parallel",)),
        name="ctx_mixer",
    )(x, mod, *consts)


def _mlstm_lat_kernel(qf_ref, kf_ref, vf_ref, bf_ref, uf_ref, qb_ref, kb_ref, vb_ref, bb_ref, ub_ref,
                      c0_ref, n0_ref, m0_ref,
                      ht_ref, c_ref, n_ref, m_ref, *, n_chunks):
    step = pl.program_id(1)

    @pl.when(step == 0)
    def _():
        c_ref[...] = c0_ref[...]
        n_ref[...] = n0_ref[...]
        m_ref[...] = m0_ref[...]
        ht_ref[...] = jnp.zeros_like(ht_ref)

    n = qf_ref.shape[0]
    is_f = lax.broadcasted_iota(jnp.int32, (n, LANES), 1) < MLSTM_HEADS
    b = jnp.where(is_f, bf_ref[...], bb_ref[...])
    u = jnp.where(is_f, uf_ref[...], ub_ref[...])
    tot = jnp.where(is_f[:1], bf_ref[n - 1:n, :], bb_ref[0:1, :])
    m_prev = m_ref[...]
    u2, bt2, w_t, m_new, decay = _chunk_setup(b, u, tot, m_prev)
    m2_prev = m_prev * LOG2E
    units = [(fwd, hd, hd if fwd else MLSTM_HEADS + hd) for fwd in (True, False) for hd in range(MLSTM_HEADS)]
    refs = {True: (qf_ref, kf_ref, vf_ref), False: (qb_ref, kb_ref, vb_ref)}
    qs = [refs[fwd][0][:, hd * MLSTM_DK:(hd + 1) * MLSTM_DK] for fwd, hd, _ in units]
    ks = [refs[fwd][1][:, hd * MLSTM_DK:(hd + 1) * MLSTM_DK] for fwd, hd, _ in units]
    vts = [refs[fwd][2][:, hd * MLSTM_DV:(hd + 1) * MLSTM_DV].T for fwd, hd, _ in units]
    s0ts = [_dot_nt(k, q) for k, q in zip(ks, qs)]
    c_prevs = [c_ref[kk] for _, _, kk in units]
    n_prevs = [n_ref[kk:kk + 1, :] for _, _, kk in units]
    inters = [(_dot_nt(c.astype(BF16), q), _dot_nt(jnp.broadcast_to(nv, (8, MLSTM_DK)).astype(BF16), q)[0:1, :])
              for c, nv, q in zip(c_prevs, n_prevs, qs)]
    m_rows = [_lane_bcast(m2_prev, kk, n) for _, _, kk in units]
    wts = [_dir_weights(s0t, _lane_bcast(u2, kk, n), m_row, fwd=fwd)
           for s0t, m_row, (fwd, _, kk) in zip(s0ts, m_rows, units)]
    hts = [_dot(vt, sdt) for vt, (sdt, _, _) in zip(vts, wts)]
    for (fwd, hd, kk), k, vt, (_, den, g), ht, inter, m_row, c_prev, n_prev in zip(
            units, ks, vts, wts, hts, inters, m_rows, c_prevs, n_prevs):
        chunk = step if fwd else n_chunks - 1 - step
        ht_ref[chunk, hd * MLSTM_DV:(hd + 1) * MLSTM_DV, :] += _dir_finish(ht, den, g, bt2[kk:kk + 1, :], m_row, inter)
        c_new, n_new = _state_update(vt, k, w_t, kk)
        dk = _lane_bcast(decay, kk)
        c_ref[kk] = dk * c_prev + c_new
        n_ref[kk:kk + 1, :] = dk * n_prev + n_new
    m_ref[...] = m_new


def _mlstm_lat(mq, mk, mv, gate_b, gate_u, c0, n0, m0):
    b, s, _ = mq.shape
    lc = MLSTM_CHUNK
    nc = s // lc
    fw = lambda c: pl.BlockSpec((None, lc, c), lambda i, j: (i, j, 0))
    bw = lambda c: pl.BlockSpec((None, lc, c), lambda i, j: (i, nc - 1 - j, 0))
    dk, dv = MLSTM_HEADS * MLSTM_DK, MLSTM_HEADS * MLSTM_DV
    return pl.pallas_call(
        functools.partial(_mlstm_lat_kernel, n_chunks=nc),
        grid=(b, nc),
        in_specs=[fw(dk), fw(dk), fw(dv), fw(LANES), fw(LANES), bw(dk), bw(dk), bw(dv), bw(LANES), bw(LANES),
                  pl.BlockSpec((None, N_DIRHEAD, MLSTM_DV, MLSTM_DK), lambda i, j: (i, 0, 0, 0)),
                  pl.BlockSpec((None, N_DIRHEAD, MLSTM_DK), lambda i, j: (i, 0, 0)),
                  pl.BlockSpec((None, 1, LANES), lambda i, j: (i, 0, 0))],
        out_specs=pl.BlockSpec((nc, dv, lc), lambda i, j: (i, 0, 0)),
        out_shape=jax.ShapeDtypeStruct((b * nc, dv, lc), F32),
        scratch_shapes=[pltpu.VMEM((N_DIRHEAD, MLSTM_DV, MLSTM_DK), F32),
                        pltpu.VMEM((N_DIRHEAD, MLSTM_DK), F32),
                        pltpu.VMEM((1, LANES), F32)],
        compiler_params=_params(("parallel", "arbitrary")),
        name="mlstm_lat",
    )(mq, mk, mv, gate_b, gate_u, mq, mk, mv, gate_b, gate_u, c0, n0, m0)


def _tail_kernel(x_ref, modin_ref, mod_ref, att_ref, hst_ref, gmix_ref, wgate_ref, gml_ref, womla_ref, womlstm_ref,
                 wout_ref, gffn_ref, wfin_ref, wfout_ref, gfin_ref, y_ref, *, mod_row0, tiles_per_seq):
    shift_mix, scale_mix = _mod_chunks(modin_ref, mod_row0, tiles_per_seq)
    gate_mix, shift_ffn, scale_ffn, gate_ffn = _mod_chunks(mod_ref, mod_row0, tiles_per_seq)
    gml = gml_ref[...]
    tm = x_ref.shape[0]
    groups = [slice(r, r + TAIL_ROWS) for r in range(0, tm, TAIL_ROWS)]
    dv = MLSTM_HEADS * MLSTM_DV
    hb = [_mixer_input(x_ref[g, :], gmix_ref[...], shift_mix, scale_mix) for g in groups]
    smo = [_sigmoid(_dot_nt(h, wgate_ref[:dv, :])) for h in hb]
    g_a = [_sigmoid(_dot_nt(h, wgate_ref[dv:dv + D_MODEL, :])) for h in hb]
    g_b = [_sigmoid(_dot_nt(h, wgate_ref[dv + D_MODEL:, :])) for h in hb]
    a = [_dot(att_ref[g, :], womla_ref[...]) for g in groups]
    hm = []
    for gi in range(len(groups)):
        hs = hst_ref[gi].T
        parts = []
        for hd in range(MLSTM_HEADS):
            sl = slice(hd * MLSTM_DV, (hd + 1) * MLSTM_DV)
            parts.append((_rms(hs[:, sl], gml[:, sl]) * smo[gi][:, sl]).astype(BF16))
        hm.append(jnp.concatenate(parts, axis=1))
    bm = [_dot(h, womlstm_ref[...]) for h in hm]
    merged = [(ga_i * ai + gb_i * bi).astype(BF16) for ga_i, gb_i, ai, bi in zip(g_a, g_b, a, bm)]
    x1 = [x_ref[g, :] + gate_mix * _dot(mi, wout_ref[...]) for g, mi in zip(groups, merged)]
    h2 = [(_rms(xi, gffn_ref[...]) * (1.0 + scale_ffn) + shift_ffn).astype(BF16) for xi in x1]
    ga = [_dot(hi, wfin_ref[:, :FFN_HIDDEN]) for hi in h2]
    gu = [_dot(hi, wfin_ref[:, FFN_HIDDEN:]) for hi in h2]
    act = [(gi * _sigmoid(gi) * ui).astype(BF16) for gi, ui in zip(ga, gu)]
    for g, xi, ci in zip(groups, x1, act):
        y_ref[g, :] = _rms(xi + gate_ffn * _dot(ci, wfout_ref[...]), gfin_ref[...])


def _tail(x, mod_in, mod, att, hst, gmix, w_gate, gml, w_o_mla, w_o_mlstm, w_out, gffn, w_ffn_in, w_ffn_out, gfin, *,
          seq_len, latent):
    n_tok = x.shape[0]
    tm = TOKEN_TILE
    assert hst.shape[0] * TAIL_ROWS == n_tok and hst.shape[2] == TAIL_ROWS
    tok = lambda c: pl.BlockSpec((tm, c), lambda i: (i, 0))
    whole = lambda a: pl.BlockSpec(a.shape, lambda i: (0, 0))
    consts = (gmix, w_gate, gml, w_o_mla, w_o_mlstm, w_out, gffn, w_ffn_in, w_ffn_out, gfin)
    return pl.pallas_call(
        functools.partial(_tail_kernel, mod_row0=1 if latent else 0, tiles_per_seq=seq_len // tm if latent else None),
        grid=(n_tok // tm,),
        in_specs=[tok(D_MODEL), whole(mod_in), whole(mod), tok(att.shape[1]),
                  pl.BlockSpec((tm // TAIL_ROWS,) + hst.shape[1:], lambda i: (i, 0, 0))]
                 + [_const_spec(a.shape) for a in consts],
        out_specs=tok(D_MODEL),
        out_shape=jax.ShapeDtypeStruct((n_tok, D_MODEL), F32),
        compiler_params=_params(("parallel",)),
        name="tail_lat" if latent else "tail_ctx",
    )(x, mod_in, mod, att, hst, *consts)


_IN_SIZES = (Q_RANK, KV_RANK, ROPE_DIM, MLSTM_HEADS * MLSTM_DK, MLSTM_HEADS * MLSTM_DK, MLSTM_HEADS * MLSTM_DV,
             4 * MLSTM_HEADS, MLSTM_HEADS * MLSTM_DV, 2 * D_MODEL)
_IN_OFF = tuple(sum(_IN_SIZES[:i]) for i in range(len(_IN_SIZES) + 1))
PACK_ROWS = LANES
_PAD_AFTER = tuple((_SEG_OFF[i + 1] // PACK_ROWS, _SEG_WIDTHS[i] - _IN_SIZES[i], _IN_SIZES[i] % PACK_ROWS)
                   for i in range(len(_IN_SIZES)) if _SEG_WIDTHS[i] != _IN_SIZES[i])


PACK_PIECES = 8
N_PACK_STEPS = pl.cdiv(SEG_MO[0] // PACK_ROWS, PACK_PIECES)
_N_IN = _IN_OFF[-1]


def _pack_src_row(p):
    src = p * PACK_ROWS
    for first_piece_after, pad, _ in _PAD_AFTER:
        src = src - jnp.where(p >= first_piece_after, pad, 0)
    src = jnp.minimum(src, _N_IN - PACK_ROWS)
    return pl.multiple_of(src, math.gcd(PACK_ROWS, _N_IN, *(pad for _, pad, _ in _PAD_AFTER)))


def _pack_in_kernel(*refs):
    w_refs, o_ref = refs[:PACK_PIECES], refs[PACK_PIECES]
    row = lax.broadcasted_iota(jnp.int32, w_refs[0].shape, 0)
    for j, w_ref in enumerate(w_refs):
        p = pl.program_id(0) * PACK_PIECES + j
        valid = jnp.where(p < N_IN_PACKED // PACK_ROWS, PACK_ROWS, 0)
        for first_piece_after, _, width in _PAD_AFTER:
            valid = jnp.where(p == first_piece_after - 1, width, valid)
        o_ref[j * PACK_ROWS:(j + 1) * PACK_ROWS, :] = jnp.where(row < valid, w_ref[...], 0.0).astype(BF16)


def _pack_in(w_in_t):
    n, k = w_in_t.shape
    piece = lambda j: pl.BlockSpec((pl.Element(PACK_ROWS), pl.Element(k)),
                                   lambda i: (_pack_src_row(i * PACK_PIECES + j), 0))
    return pl.pallas_call(
        _pack_in_kernel,
        grid=(N_PACK_STEPS,),
        in_specs=[piece(j) for j in range(PACK_PIECES)],
        out_specs=pl.BlockSpec((PACK_PIECES * PACK_ROWS, k), lambda i: (i, 0)),
        out_shape=jax.ShapeDtypeStruct((N_PACK_STEPS * PACK_PIECES * PACK_ROWS, k), BF16),
        compiler_params=_params(("parallel",)),
        name="pack_in",
    )(*([w_in_t] * PACK_PIECES))


def _pack_small(w_uq, w_ukv, b_gates):
    qd = NOPE_DIM + ROPE_DIM
    w_uq_p = jnp.pad(w_uq.reshape(Q_RANK, MLA_HEADS, qd), ((0, 0), (0, 0), (0, HEAD_BLOCK - qd)))
    w_uq_p = w_uq_p.reshape(Q_RANK, MLA_HEADS * HEAD_BLOCK).astype(BF16)
    kv = w_ukv.reshape(KV_RANK, MLA_HEADS, NOPE_DIM + MLA_V_DIM)
    wk = jnp.pad(kv[:, :, :NOPE_DIM], ((0, 0), (0, 0), (0, HEAD_BLOCK - NOPE_DIM))).reshape(KV_RANK, -1)
    wv = kv[:, :, NOPE_DIM:].reshape(KV_RANK, -1)
    w_ukv_p = jnp.concatenate([wk, wv], axis=1).astype(BF16)
    bg = jnp.pad(b_gates, ((0, 0), (0, LANES - b_gates.shape[1])))
    return w_uq_p, w_ukv_p, bg


def _rope_tables(n_tokens):
    pos = np.arange(n_tokens)
    row = (pos // GRID_W).astype(np.float64)[:, None]
    col = (pos % GRID_W).astype(np.float64)[:, None]
    half = ROPE_DIM // 2
    inv = (np.float32(ROPE_BASE) ** (-np.arange(0, half, 2, dtype=np.float32) / np.float32(half))).astype(np.float64)
    r = np.arange(LANES) - ROPE_LANE0
    in_rope = (r >= 0) & (r < ROPE_DIM)
    rr = np.clip(r, 0, ROPE_DIM - 1)
    freq = inv[rr % (half // 2)][None, :]
    ang = np.where((rr // half == 0)[None, :], row * freq, col * freq).astype(np.float32).astype(np.float64)
    first = (rr % half) < (half // 2)
    cos = np.where(in_rope[None, :], np.cos(ang), 1.0)
    sin = np.sin(ang)
    sin_lo = np.where((in_rope & first)[None, :], -sin, 0.0)
    sin_hi = np.where((in_rope & ~first)[None, :], sin, 0.0)
    return tuple(jnp.asarray(t, dtype=F32) for t in (cos, sin_lo, sin_hi))


def kernel(x_prompt, x_sample, cache_ckv, cache_krope, state_C, state_n, state_m, c, c_ctx, w_mod, b_mod, g_norm_mix,
           w_in, b_gates, g_q_norm, w_uq, g_kv_norm, w_ukv, g_mlstm_norm, w_o_mla, w_o_mlstm, w_out, g_norm_ffn,
           w_ffn_in, w_ffn_out, g_final):
    bp, sp, d = x_prompt.shape
    bs, ss, _ = x_sample.shape
    layer = 0
    assert w_mod.shape[0] == 1 and sp == MLSTM_CHUNK and ss % MLSTM_CHUNK == 0

    adaln_args = (c_ctx[None, :], c, w_mod[layer], b_mod[layer][None, :])
    mod_in = _adaln(*adaln_args)

    w_in_t = w_in[layer].T
    w_in_p = _pack_in(w_in_t)
    w_uq_p, w_ukv_p, bg = _pack_small(w_uq[layer], w_ukv[layer], b_gates[layer][None, :])
    row = lambda g: g[layer][None, :]
    shared_in = (row(g_norm_mix), w_in_p, row(g_q_norm), w_uq_p, row(g_kv_norm), w_ukv_p, bg)
    seq3 = lambda a, b_, s_: a.reshape(b_, s_, a.shape[-1])

    xl = x_sample.reshape(bs * ss, d)
    q, kc, v, mq_l, mk_l, mv_l, gb_l, gu_l = _inproj(xl, mod_in, _rope_tables(ss), *shared_in, seq_len=ss)
    krope_blk = jnp.pad(cache_krope[:, layer], ((0, 0), (0, 0), (ROPE_LANE0, LANES - ROPE_LANE0 - ROPE_DIM)))
    cache = (cache_ckv[:, layer], krope_blk, w_ukv_p)
    gate_cols = (w_in_t, _IN_OFF[7], _IN_OFF[9] - _IN_OFF[7])
    att_l, wgate_b, womla_b, womlstm_b, wout_b, wfin_b, wfout_b, mod_tail = _attention(
        seq3(q, bs, ss), seq3(kc, bs, ss), seq3(v, bs, ss), cache,
        cast_weights=(gate_cols, w_o_mla[layer], w_o_mlstm[layer], w_out[layer], w_ffn_in[layer], w_ffn_out[layer]),
        adaln_tail=adaln_args)
    tail_w = (row(g_norm_mix), wgate_b, row(g_mlstm_norm), womla_b, womlstm_b, wout_b, row(g_norm_ffn), wfin_b, wfout_b,
              g_final[None, :])

    xc = x_prompt.reshape(bp * sp, d)
    att, hs, new_c, new_n, new_m, ckv, kpe = _ctx_mixer(xc, mod_in, *shared_in, seq_len=sp)
    y_prompt = _tail(xc, mod_in, mod_tail, att.reshape(bp * sp, -1), hs, *tail_w,
                     seq_len=sp, latent=False).reshape(bp, sp, d)
    new_ckv = ckv.reshape(bp, 1, sp, KV_RANK)
    new_krope = jnp.swapaxes(kpe, 1, 2).reshape(bp, 1, sp, ROPE_DIM)
    new_C = new_c.reshape(bp, 1, 2, MLSTM_HEADS, MLSTM_DV, MLSTM_DK)
    new_N = new_n.reshape(bp, 1, 2, MLSTM_HEADS, MLSTM_DK)
    new_M = new_m[:, 0, :N_DIRHEAD].reshape(bp, 1, 2, MLSTM_HEADS)

    c0 = state_C[:, layer].reshape(bs, N_DIRHEAD, MLSTM_DV, MLSTM_DK)
    n0 = state_n[:, layer].reshape(bs, N_DIRHEAD, MLSTM_DK)
    m0 = jnp.pad(state_m[:, layer].reshape(bs, 1, N_DIRHEAD), ((0, 0), (0, 0), (0, LANES - N_DIRHEAD)))
    hs = _mlstm_lat(seq3(mq_l, bs, ss), seq3(mk_l, bs, ss), seq3(mv_l, bs, ss), seq3(gb_l, bs, ss), seq3(gu_l, bs, ss),
                    c0, n0, m0)
    y_sample = _tail(xl, mod_in, mod_tail, att_l.reshape(bs * ss, -1), hs, *tail_w,
                     seq_len=ss, latent=True).reshape(bs, ss, d)
    return (y_prompt, y_sample, new_ckv, new_krope, new_C, new_N, new_M)
```

```python
import functools
import math

import jax
import jax.numpy as jnp
import numpy as np
from jax import lax
from jax.experimental import pallas as pl
from jax.experimental.pallas import tpu as pltpu

F32 = jnp.float32
BF16 = jnp.bfloat16

D_MODEL = 1024
GRID_W = 64
MLA_HEADS = 8
Q_RANK = 384
KV_RANK = 256
NOPE_DIM = 64
ROPE_DIM = 32
MLA_V_DIM = 64
ROPE_BASE = 10000.0
MLA_SCALE = (NOPE_DIM + ROPE_DIM) ** -0.5
MLSTM_HEADS = 4
MLSTM_DK = 128
MLSTM_DV = 256
FFN_HIDDEN = ((8 * D_MODEL // 3 + 255) // 256) * 256
EPS = 1e-6

LANES = 128
HEAD_BLOCK = LANES
ROPE_LANE0 = NOPE_DIM
N_DIRHEAD = 2 * MLSTM_HEADS
MLSTM_CHUNK = 256
LOG2E = math.log2(math.e)
Q_PRESCALE = MLA_SCALE * LOG2E
V_SLAB = 2 * LANES
V_WIDTH = (MLA_HEADS // 2) * V_SLAB
TOKEN_TILE = 512
TAIL_ROWS = 256
INPROJ_ROWS = TOKEN_TILE
Q_TILE = 256
CTX_SEQS_PER_STEP = 4
VMEM_LIMIT = 56 * 1024 * 1024

_SEG_WIDTHS = (Q_RANK, KV_RANK, LANES, MLSTM_HEADS * MLSTM_DK, MLSTM_HEADS * MLSTM_DK,
               MLSTM_HEADS * MLSTM_DV, LANES, MLSTM_HEADS * MLSTM_DV, 2 * D_MODEL)
_SEG_OFF = tuple(sum(_SEG_WIDTHS[:i]) for i in range(len(_SEG_WIDTHS) + 1))
SEG_Q, SEG_KV, SEG_KPE, SEG_MQ, SEG_MK, SEG_MV, SEG_GATE, SEG_MO, SEG_BR = (
    (_SEG_OFF[i], _SEG_OFF[i + 1]) for i in range(9))
N_IN_PACKED = _SEG_OFF[-1]


def _dot(a, b):
    return jnp.dot(a, b, preferred_element_type=F32)


def _dot_nt(a, b):
    return lax.dot_general(a, b, (((1,), (1,)), ((), ())), preferred_element_type=F32)


def _rms(x, g):
    ms = jnp.mean(x * x, axis=-1, keepdims=True)
    return x * lax.rsqrt(ms + EPS) * g


def _sigmoid(x):
    return 1.0 / (1.0 + jnp.exp(-x))


def _const_spec(shape):
    nd = len(shape)
    return pl.BlockSpec(shape, lambda *_: (0,) * nd, pipeline_mode=pl.Buffered(1))


def _mod_chunks(mod_ref, row0, tiles_per_seq):
    row = row0 if tiles_per_seq is None else row0 + pl.program_id(0) // tiles_per_seq
    m = mod_ref[pl.ds(row, 1), :]
    return [m[:, k * D_MODEL:(k + 1) * D_MODEL] for k in range(m.shape[1] // D_MODEL)]


def _params(sem, flags=None, vmem_mib=None):
    limit = VMEM_LIMIT if vmem_mib is None else vmem_mib * 1024 * 1024
    return pltpu.CompilerParams(dimension_semantics=sem, vmem_limit_bytes=limit, flags=flags)


N_MOD_MIXER = 2
N_MOD_TAIL = 4


COND_ROWS = 8


def _adaln_block(cctx_ref, c_ref, w_ref, b_ref, o_ref):
    pad = jnp.zeros((COND_ROWS - 1 - c_ref.shape[0], D_MODEL), F32)
    c = jnp.concatenate([cctx_ref[...], c_ref[...], pad], axis=0)
    s = c * _sigmoid(c)
    o_ref[...] = _dot(s.astype(BF16), w_ref[...].astype(BF16)) + b_ref[...]


def _adaln(c_ctx, c, w_mod, b_mod):
    tn = D_MODEL
    return pl.pallas_call(
        _adaln_block,
        grid=(N_MOD_MIXER,),
        in_specs=[pl.BlockSpec(c_ctx.shape, lambda j: (0, 0)),
                  pl.BlockSpec(c.shape, lambda j: (0, 0)),
                  pl.BlockSpec((D_MODEL, tn), lambda j: (0, j)),
                  pl.BlockSpec((1, tn), lambda j: (0, j))],
        out_specs=pl.BlockSpec((COND_ROWS, tn), lambda j: (0, j)),
        out_shape=jax.ShapeDtypeStruct((COND_ROWS, N_MOD_MIXER * D_MODEL), F32),
        compiler_params=_params(("arbitrary",)),
        name="adaln",
    )(c_ctx, c, w_mod, b_mod)


def _rope_block(x, cos, sin_lo, sin_hi):
    return x * cos + pltpu.roll(x, LANES - 8, 1) * sin_lo + pltpu.roll(x, 8, 1) * sin_hi


def _store_kv(kvn, kp, kc_ref, v_ref, g=slice(None)):
    for hd in range(MLA_HEADS):
        sl = slice(hd * HEAD_BLOCK, (hd + 1) * HEAD_BLOCK)
        kc_ref[g, sl] = (kvn[:, sl] + kp).astype(BF16)
    v0 = MLA_HEADS * HEAD_BLOCK
    ones = jnp.ones((kvn.shape[0], LANES), BF16)
    for pair in range(MLA_HEADS // 2):
        v_ref[g, pair * V_SLAB:pair * V_SLAB + LANES] = kvn[:, v0 + pair * LANES:v0 + (pair + 1) * LANES].astype(BF16)
        v_ref[g, pair * V_SLAB + LANES:(pair + 1) * V_SLAB] = ones


def _mixer_input(x, g, shift, scale):
    return (_rms(x, g) * (1.0 + scale) + shift).astype(BF16)


def _inproj_kernel(*refs, rope, mod_row0, tiles_per_seq):
    if rope:
        (x_ref, mod_ref, cos_ref, slo_ref, shi_ref, gmix_ref, win_ref, gq_ref, wuq_ref, gkv_ref, wukv_ref,
         bg_ref, q_ref, kc_ref, v_ref, mq_ref, mk_ref, mv_ref, gb_ref, gu_ref) = refs
    else:
        (x_ref, mod_ref, gmix_ref, win_ref, gq_ref, wuq_ref, gkv_ref, wukv_ref,
         bg_ref, q_ref, kc_ref, v_ref, mq_ref, mk_ref, mv_ref, gate_ref, ckv_ref, kpet_ref) = refs
    shift, scale = _mod_chunks(mod_ref, mod_row0, tiles_per_seq)
    for r0 in range(0, x_ref.shape[0], INPROJ_ROWS):
        g = slice(r0, r0 + INPROJ_ROWS)
        if rope:
            cos, slo, shi = cos_ref[g, :], slo_ref[g, :], shi_ref[g, :]
        hb = _mixer_input(x_ref[g, :], gmix_ref[...], shift, scale)

        def proj(seg):
            return _dot_nt(hb, win_ref[seg[0]:seg[1], :])

        low = proj((SEG_Q[0], SEG_KPE[1]))
        qn = _rms(low[:, SEG_Q[0]:SEG_Q[1]], gq_ref[...]).astype(BF16)
        ckv = _rms(low[:, SEG_KV[0]:SEG_KV[1]], gkv_ref[...])
        zkpe = low[:, SEG_KPE[0]:SEG_KPE[1]]
        gates = proj(SEG_GATE) + bg_ref[...]
        if rope:
            for c0 in range(0, INPROJ_ROWS, MLSTM_CHUNK):
                b, u, _ = _gate_prep(gates[c0:c0 + MLSTM_CHUNK, :])
                gb_ref[r0 + c0:r0 + c0 + MLSTM_CHUNK, :] = b
                gu_ref[r0 + c0:r0 + c0 + MLSTM_CHUNK, :] = u
        else:
            gate_ref[g, :] = gates

        q = _dot(qn, wuq_ref[...])
        for hd in range(MLA_HEADS):
            sl = slice(hd * HEAD_BLOCK, (hd + 1) * HEAD_BLOCK)
            qh = q[:, sl]
            if rope:
                qh = _rope_block(qh, cos, slo, shi)
            q_ref[g, sl] = (qh * Q_PRESCALE).astype(BF16)

        if not rope:
            ckv_ref[g, :] = ckv
            kpe_t = zkpe.T
            seq = kpet_ref.shape[2]
            for sq in range(INPROJ_ROWS // seq):
                kpet_ref[r0 // seq + sq] = kpe_t[:ROPE_DIM, sq * seq:(sq + 1) * seq]
        kp = pltpu.roll(zkpe, ROPE_LANE0, 1)
        if rope:
            kp = _rope_block(kp, cos, slo, shi)
        kvn = _dot(ckv.astype(BF16), wukv_ref[...])
        _store_kv(kvn, kp, kc_ref, v_ref, g)

        mv_ref[g, :] = proj(SEG_MV).astype(BF16)
        mk_ref[g, :] = (proj(SEG_MK) * (MLSTM_DK ** -0.5)).astype(BF16)
        mq_ref[g, :] = proj(SEG_MQ).astype(BF16)


def _inproj(x, mod, rope_tabs, gmix, w_in_p, gq, w_uq_p, gkv, w_ukv_p, bg, *, seq_len):
    n_tok = x.shape[0]
    tm = 2 * INPROJ_ROWS
    tiles_per_seq = seq_len // tm
    rope = rope_tabs is not None
    tok = lambda c: pl.BlockSpec((tm, c), lambda i: (i, 0))
    in_specs = [tok(D_MODEL), pl.BlockSpec(mod.shape, lambda i: (0, 0))]
    args = [x, mod]
    if rope:
        in_specs += [pl.BlockSpec((tm, LANES), lambda i: (i % tiles_per_seq, 0))] * 3
        args += list(rope_tabs)
    win_rows = SEG_MO[0]
    in_specs += [_const_spec(gmix.shape), _const_spec((win_rows, w_in_p.shape[1]))]
    in_specs += [_const_spec(a.shape) for a in (gq, w_uq_p, gkv, w_ukv_p, bg)]
    args += [gmix, w_in_p, gq, w_uq_p, gkv, w_ukv_p, bg]
    widths = [(MLA_HEADS * HEAD_BLOCK, BF16), (MLA_HEADS * HEAD_BLOCK, BF16), (V_WIDTH, BF16),
              (MLSTM_HEADS * MLSTM_DK, BF16), (MLSTM_HEADS * MLSTM_DK, BF16), (MLSTM_HEADS * MLSTM_DV, BF16),
              (LANES, F32)]
    if rope:
        widths.append((LANES, F32))
    out_specs = [tok(c) for c, _ in widths]
    out_shape = [jax.ShapeDtypeStruct((n_tok, c), dt) for c, dt in widths]
    if not rope:
        out_specs += [tok(KV_RANK), pl.BlockSpec((tm // seq_len, ROPE_DIM, seq_len), lambda i: (i, 0, 0))]
        out_shape += [jax.ShapeDtypeStruct((n_tok, KV_RANK), F32),
                      jax.ShapeDtypeStruct((n_tok // seq_len, ROPE_DIM, seq_len), F32)]
    return pl.pallas_call(
        functools.partial(_inproj_kernel, rope=rope, mod_row0=1 if rope else 0,
                          tiles_per_seq=tiles_per_seq if rope else None),
        grid=(n_tok // tm,),
        in_specs=in_specs,
        out_specs=out_specs,
        out_shape=out_shape,
        compiler_params=_params(("parallel",)),
        name="inproj_lat" if rope else "inproj_ctx",
    )(*args)


def _attn_kernel(*refs, has_cache, n_cast, has_adaln):
    n_in = 6 if has_cache else 3
    if has_cache:
        q_ref, k_ref, v_ref, ckv_ref, kp_ref, wukv_ref = refs[:n_in]
        kc_ref, vc_ref = refs[-2:]

        @pl.when(pl.program_id(1) == 0)
        def _():
            kvn = _dot(ckv_ref[0].astype(BF16), wukv_ref[...])
            _store_kv(kvn, kp_ref[0], kc_ref.at[0], vc_ref.at[0])
    else:
        q_ref, k_ref, v_ref = refs[:n_in]
    n_all_in = n_in + n_cast + (4 if has_adaln else 0)
    o_ref = refs[n_all_in]
    for w_ref, wb_ref in zip(refs[n_in:n_in + n_cast], refs[n_all_in + 1:n_all_in + 1 + n_cast]):
        wb_ref[...] = w_ref[...].astype(BF16)
    if has_adaln:
        _adaln_block(*refs[n_in + n_cast:n_all_in], refs[n_all_in + 1 + n_cast])
    tq = q_ref.shape[1]
    lane = lax.broadcasted_iota(jnp.int32, (tq, LANES), 1)
    for sq in range(q_ref.shape[0]):
        hsl = [slice(hd * HEAD_BLOCK, (hd + 1) * HEAD_BLOCK) for hd in range(MLA_HEADS)]
        vsl = [slice(hd // 2 * V_SLAB, (hd // 2 + 1) * V_SLAB) for hd in range(MLA_HEADS)]
        s = [_dot_nt(q_ref[sq, :, sl], k_ref[sq, :, sl]) for sl in hsl]
        m = [jnp.max(si, axis=-1, keepdims=True) for si in s]
        if has_cache:
            sc = [_dot_nt(q_ref[sq, :, sl], kc_ref[sq, :, sl]) for sl in hsl]
            m = [jnp.maximum(mi, jnp.max(ci, axis=-1, keepdims=True)) for mi, ci in zip(m, sc)]
        p = [jnp.exp2(si - mi).astype(BF16) for si, mi in zip(s, m)]
        o = [_dot(pi, v_ref[sq, :, sl]) for pi, sl in zip(p, vsl)]
        if has_cache:
            pc = [jnp.exp2(ci - mi).astype(BF16) for ci, mi in zip(sc, m)]
            o = [oi + _dot(pi, vc_ref[sq, :, sl]) for oi, pi, sl in zip(o, pc, vsl)]
        outs = [oi[:, :LANES] * (1.0 / oi[:, LANES:]) for oi in o]
        for pair in range(MLA_HEADS // 2):
            o_ref[sq, :, pair * LANES:(pair + 1) * LANES] = jnp.where(
                lane < MLA_V_DIM, outs[2 * pair], outs[2 * pair + 1]).astype(BF16)


def _attention(q, k, v, cache=None, cast_weights=(), adaln_tail=None):
    nb, s, _ = q.shape
    tq = Q_TILE
    ns = 1
    b = nb // ns
    n_steps = b * (s // tq)
    kw, vw, ow = MLA_HEADS * HEAD_BLOCK, V_WIDTH, MLA_HEADS * MLA_V_DIM
    in_specs = [pl.BlockSpec((ns, tq, kw), lambda i, j: (i, j, 0)),
                pl.BlockSpec((ns, s, kw), lambda i, j: (i, 0, 0)),
                pl.BlockSpec((ns, s, vw), lambda i, j: (i, 0, 0))]
    args = [q, k, v]
    scratch = []
    if cache is not None:
        past = cache[0].shape[1]
        in_specs += [pl.BlockSpec((ns, past, KV_RANK), lambda i, j: (i, 0, 0)),
                     pl.BlockSpec((ns, past, LANES), lambda i, j: (i, 0, 0)),
                     _const_spec(cache[2].shape)]
        args += list(cache)
        scratch = [pltpu.VMEM((ns, past, kw), BF16), pltpu.VMEM((ns, past, vw), BF16)]
    step = lambda i, j: i * (s // tq) + j
    casts = [w if isinstance(w, tuple) else (w, 0, w.shape[0]) for w in cast_weights]
    cast_weights = [w for w, _, _ in casts]

    def cast_in_spec(w, row0, n_rows):
        rows = n_rows // n_steps
        if (row0, n_rows) == (0, w.shape[0]):
            return pl.BlockSpec((rows, w.shape[1]), lambda i, j: (step(i, j), 0))
        return pl.BlockSpec((pl.Element(rows), pl.Element(w.shape[1])),
                            lambda i, j: (pl.multiple_of(row0 + step(i, j) * rows, math.gcd(row0, rows)), 0))

    w_in_specs = [cast_in_spec(*c) for c in casts]
    w_specs = [pl.BlockSpec((n // n_steps, w.shape[1]), lambda i, j: (step(i, j), 0)) for w, _, n in casts]
    extra_in, extra_out, extra_shape = [], [], []
    if adaln_tail is not None:
        tn = N_MOD_TAIL * D_MODEL // n_steps
        col0 = N_MOD_MIXER * D_MODEL // tn
        extra_in = [pl.BlockSpec(adaln_tail[0].shape, lambda i, j: (0, 0)),
                    pl.BlockSpec(adaln_tail[1].shape, lambda i, j: (0, 0)),
                    pl.BlockSpec((D_MODEL, tn), lambda i, j: (0, col0 + step(i, j))),
                    pl.BlockSpec((1, tn), lambda i, j: (0, col0 + step(i, j)))]
        extra_out = [pl.BlockSpec((COND_ROWS, tn), lambda i, j: (0, step(i, j)))]
        extra_shape = [jax.ShapeDtypeStruct((COND_ROWS, N_MOD_TAIL * D_MODEL), F32)]
    outs = pl.pallas_call(
        functools.partial(_attn_kernel, has_cache=cache is not None, n_cast=len(cast_weights),
                          has_adaln=adaln_tail is not None),
        grid=(b, s // tq),
        in_specs=in_specs + w_in_specs + extra_in,
        out_specs=[pl.BlockSpec((ns, tq, ow), lambda i, j: (i, j, 0))] + w_specs + extra_out,
        out_shape=[jax.ShapeDtypeStruct((nb, s, ow), BF16)]
                  + [jax.ShapeDtypeStruct((n, w.shape[1]), BF16) for w, _, n in casts] + extra_shape,
        scratch_shapes=scratch,
        compiler_params=_params(("parallel", "arbitrary")),
        name="attn_lat" if cache is not None else "attn_ctx",
    )(*args, *cast_weights, *(adaln_tail or ()))
    return outs[0] if len(outs) == 1 else outs


def _lane_bcast(x, k, width=None):
    y = jnp.broadcast_to(x[:, k:k + 1], x.shape)
    reps = (width or LANES) // LANES
    return y if reps == 1 else jnp.concatenate([y] * reps, axis=1)


def _prefix_sum_rows(x):
    n = x.shape[0]
    row = lax.broadcasted_iota(jnp.int32, x.shape, 0)
    shift = 1
    while shift < n:
        x = x + jnp.where(row >= shift, pltpu.roll(x, shift, 0), 0.0)
        shift *= 2
    return x


def _gate_prep(g):
    n = g.shape[0]
    lane = lax.broadcasted_iota(jnp.int32, (n, LANES), 1)
    fpre = pltpu.roll(g, LANES - N_DIRHEAD, 1)
    lf = jnp.minimum(fpre, 0.0) - jnp.log(1.0 + jnp.exp(-jnp.abs(fpre)))
    binc = _prefix_sum_rows(lf)
    tot = binc[n - 1:n, :]
    b = jnp.where(lane < MLSTM_HEADS, binc, tot - binc + lf)
    return b, g - b, tot


def _chunk_setup(b, u, tot, m_prev):
    g_last = jnp.maximum(jnp.max(u, axis=0, keepdims=True), m_prev)
    u2 = u * LOG2E
    w = jnp.exp2(u2 - g_last * LOG2E)
    return u2, (b * LOG2E).T, w.T, tot + g_last, jnp.exp(m_prev - g_last)


def _state_update(vt, k, wt, kk):
    c_new = _dot((vt * wt[kk:kk + 1, :]).astype(BF16), k)
    n_new = _dot(wt[:N_DIRHEAD, :].astype(BF16), k)[kk:kk + 1, :]
    return c_new, n_new


def _dir_weights(s0t, u_b, m_row, *, fwd):
    n = s0t.shape[0]
    r = lax.broadcasted_iota(jnp.int32, (n, n), 0)
    c = lax.broadcasted_iota(jnp.int32, (n, n), 1)
    a = jnp.where((r <= c) if fwd else (r >= c), u_b, -jnp.inf)
    g = jnp.maximum(jnp.max(a, axis=0, keepdims=True), m_row)
    sdt = s0t * jnp.exp2(a - g)
    return sdt.astype(BF16), jnp.sum(sdt, axis=0, keepdims=True), g


def _dir_finish(ht, den, g, bt_row, m_row, inter=None):
    if inter is not None:
        w_inter = jnp.exp2(m_row - g)
        ht = ht + w_inter * inter[0]
        den = den + w_inter * inter[1]
    floor = jnp.exp2(-(bt_row + g))
    return ht * (1.0 / jnp.maximum(jnp.abs(den), floor))


def _mlstm_ctx_kernel(q_ref, k_ref, v_ref, g_ref, ht_ref, c_ref, n_ref, m_ref):
    n = q_ref.shape[1]
    zero_lanes = jnp.zeros((1, LANES), F32)
    zero_row = jnp.zeros((1, n), F32)
    dirs = ((True, 0), (False, MLSTM_HEADS))
    setup = [_chunk_setup(*_gate_prep(g_ref[sq]), zero_lanes) for sq in range(q_ref.shape[0])]
    units = [(sq, hd) for sq in range(q_ref.shape[0]) for hd in range(MLSTM_HEADS)]
    ks = [k_ref[sq, :, hd * MLSTM_DK:(hd + 1) * MLSTM_DK] for sq, hd in units]
    s0ts = [_dot_nt(k, q_ref[sq, :, hd * MLSTM_DK:(hd + 1) * MLSTM_DK]) for k, (sq, hd) in zip(ks, units)]
    vts = [v_ref[sq, :, hd * MLSTM_DV:(hd + 1) * MLSTM_DV].T for sq, hd in units]
    wts = [[_dir_weights(s0t, _lane_bcast(setup[sq][0], off + hd, n), zero_row, fwd=fwd) for fwd, off in dirs]
           for s0t, (sq, hd) in zip(s0ts, units)]
    hts = [[_dot(vt, sdt) for sdt, _, _ in wt] for vt, wt in zip(vts, wts)]
    for (sq, hd), k, vt, wt, ht2 in zip(units, ks, vts, wts, hts):
        _, bt2, w_t, _, _ = setup[sq]
        ht_ref[sq, hd * MLSTM_DV:(hd + 1) * MLSTM_DV, :] = sum(
            _dir_finish(ht_d, den, g, bt2[off + hd:off + hd + 1, :], zero_row)
            for ht_d, (_, den, g), (_, off) in zip(ht2, wt, dirs))
        for _, off in dirs:
            c_ref[sq, off + hd], n_ref[sq, off + hd:off + hd + 1, :] = _state_update(vt, k, w_t, off + hd)
    for sq in range(q_ref.shape[0]):
        m_ref[sq] = setup[sq][3]


class _SeqView:
    def __init__(self, ref, seq_len):
        self.ref, self.seq_len = ref, seq_len
        self.shape = (ref.shape[0] // seq_len, seq_len, ref.shape[1])

    def __getitem__(self, idx):
        sq, rows, cols = idx if isinstance(idx, tuple) else (idx, slice(None), slice(None))
        assert rows == slice(None)
        return self.ref[sq * self.seq_len:(sq + 1) * self.seq_len, cols]


N_CTX_CONSTS = 7


def _ctx_mixer_kernel(*refs, seq_len):
    x_ref, mod_ref = refs[:2]
    consts = refs[2:2 + N_CTX_CONSTS]
    att_ref, ht_ref, c_ref, n_ref, m_ref, ckv_ref, kpet_ref = refs[2 + N_CTX_CONSTS:9 + N_CTX_CONSTS]
    q_s, kc_s, v_s, mq_s, mk_s, mv_s, gate_s = refs[9 + N_CTX_CONSTS:]
    _inproj_kernel(x_ref, mod_ref, *consts, q_s, kc_s, v_s, mq_s, mk_s, mv_s, gate_s, ckv_ref, kpet_ref,
                   rope=False, mod_row0=0, tiles_per_seq=None)
    q3, k3, v3, mq3, mk3, mv3, g3 = (_SeqView(r, seq_len) for r in (q_s, kc_s, v_s, mq_s, mk_s, mv_s, gate_s))
    _attn_kernel(q3, k3, v3, att_ref, has_cache=False, n_cast=0, has_adaln=False)
    _mlstm_ctx_kernel(mq3, mk3, mv3, g3, ht_ref, c_ref, n_ref, m_ref)


def _ctx_mixer(x, mod, gmix, w_in_p, gq, w_uq_p, gkv, w_ukv_p, bg, *, seq_len):
    n_tok = x.shape[0]
    ns, s = CTX_SEQS_PER_STEP, seq_len
    nseq, tm = n_tok // s, ns * s
    dv = MLSTM_HEADS * MLSTM_DV
    consts = (gmix, w_in_p, gq, w_uq_p, gkv, w_ukv_p, bg)
    const_specs = [_const_spec(gmix.shape), _const_spec((SEG_MO[0], w_in_p.shape[1]))]
    const_specs += [_const_spec(a.shape) for a in consts[2:]]
    seq_blk = lambda *dims: pl.BlockSpec((ns,) + dims, lambda i: (i,) + (0,) * len(dims))
    return pl.pallas_call(
        functools.partial(_ctx_mixer_kernel, seq_len=s),
        grid=(nseq // ns,),
        in_specs=[pl.BlockSpec((tm, D_MODEL), lambda i: (i, 0)), pl.BlockSpec(mod.shape, lambda i: (0, 0))]
                 + const_specs,
        out_specs=[seq_blk(s, MLA_HEADS * MLA_V_DIM), seq_blk(dv, s), seq_blk(N_DIRHEAD, MLSTM_DV, MLSTM_DK),
                   seq_blk(N_DIRHEAD, MLSTM_DK), seq_blk(1, LANES),
                   pl.BlockSpec((tm, KV_RANK), lambda i: (i, 0)), seq_blk(ROPE_DIM, s)],
        out_shape=[jax.ShapeDtypeStruct((nseq, s, MLA_HEADS * MLA_V_DIM), BF16),
                   jax.ShapeDtypeStruct((nseq, dv, s), F32),
                   jax.ShapeDtypeStruct((nseq, N_DIRHEAD, MLSTM_DV, MLSTM_DK), F32),
                   jax.ShapeDtypeStruct((nseq, N_DIRHEAD, MLSTM_DK), F32),
                   jax.ShapeDtypeStruct((nseq, 1, LANES), F32),
                   jax.ShapeDtypeStruct((n_tok, KV_RANK), F32),
                   jax.ShapeDtypeStruct((nseq, ROPE_DIM, s), F32)],
        scratch_shapes=[pltpu.VMEM((tm, MLA_HEADS * HEAD_BLOCK), BF16), pltpu.VMEM((tm, MLA_HEADS * HEAD_BLOCK), BF16),
                        pltpu.VMEM((tm, V_WIDTH), BF16), pltpu.VMEM((tm, MLSTM_HEADS * MLSTM_DK), BF16),
                        pltpu.VMEM((tm, MLSTM_HEADS * MLSTM_DK), BF16), pltpu.VMEM((tm, dv), BF16),
                        pltpu.VMEM((tm, LANES), F32)],
        compiler_params=_params(("parallel",)),
        name="ctx_mixer",
    )(x, mod, *consts)


def _mlstm_lat_kernel(qf_ref, kf_ref, vf_ref, bf_ref, uf_ref, qb_ref, kb_ref, vb_ref, bb_ref, ub_ref,
                      c0_ref, n0_ref, m0_ref,
                      ht_ref, c_ref, n_ref, m_ref, *, n_chunks):
    step = pl.program_id(1)

    @pl.when(step == 0)
    def _():
        c_ref[...] = c0_ref[...]
        n_ref[...] = n0_ref[...]
        m_ref[...] = m0_ref[...]
        ht_ref[...] = jnp.zeros_like(ht_ref)

    n = qf_ref.shape[0]
    is_f = lax.broadcasted_iota(jnp.int32, (n, LANES), 1) < MLSTM_HEADS
    b = jnp.where(is_f, bf_ref[...], bb_ref[...])
    u = jnp.where(is_f, uf_ref[...], ub_ref[...])
    tot = jnp.where(is_f[:1], bf_ref[n - 1:n, :], bb_ref[0:1, :])
    m_prev = m_ref[...]
    u2, bt2, w_t, m_new, decay = _chunk_setup(b, u, tot, m_prev)
    m2_prev = m_prev * LOG2E
    units = [(fwd, hd, hd if fwd else MLSTM_HEADS + hd) for fwd in (True, False) for hd in range(MLSTM_HEADS)]
    refs = {True: (qf_ref, kf_ref, vf_ref), False: (qb_ref, kb_ref, vb_ref)}
    qs = [refs[fwd][0][:, hd * MLSTM_DK:(hd + 1) * MLSTM_DK] for fwd, hd, _ in units]
    ks = [refs[fwd][1][:, hd * MLSTM_DK:(hd + 1) * MLSTM_DK] for fwd, hd, _ in units]
    vts = [refs[fwd][2][:, hd * MLSTM_DV:(hd + 1) * MLSTM_DV].T for fwd, hd, _ in units]
    s0ts = [_dot_nt(k, q) for k, q in zip(ks, qs)]
    c_prevs = [c_ref[kk] for _, _, kk in units]
    n_prevs = [n_ref[kk:kk + 1, :] for _, _, kk in units]
    inters = [(_dot_nt(c.astype(BF16), q), _dot_nt(jnp.broadcast_to(nv, (8, MLSTM_DK)).astype(BF16), q)[0:1, :])
              for c, nv, q in zip(c_prevs, n_prevs, qs)]
    m_rows = [_lane_bcast(m2_prev, kk, n) for _, _, kk in units]
    wts = [_dir_weights(s0t, _lane_bcast(u2, kk, n), m_row, fwd=fwd)
           for s0t, m_row, (fwd, _, kk) in zip(s0ts, m_rows, units)]
    hts = [_dot(vt, sdt) for vt, (sdt, _, _) in zip(vts, wts)]
    for (fwd, hd, kk), k, vt, (_, den, g), ht, inter, m_row, c_prev, n_prev in zip(
            units, ks, vts, wts, hts, inters, m_rows, c_prevs, n_prevs):
        chunk = step if fwd else n_chunks - 1 - step
        ht_ref[chunk, hd * MLSTM_DV:(hd + 1) * MLSTM_DV, :] += _dir_finish(ht, den, g, bt2[kk:kk + 1, :], m_row, inter)
        c_new, n_new = _state_update(vt, k, w_t, kk)
        dk = _lane_bcast(decay, kk)
        c_ref[kk] = dk * c_prev + c_new
        n_ref[kk:kk + 1, :] = dk * n_prev + n_new
    m_ref[...] = m_new


def _mlstm_lat(mq, mk, mv, gate_b, gate_u, c0, n0, m0):
    b, s, _ = mq.shape
    lc = MLSTM_CHUNK
    nc = s // lc
    fw = lambda c: pl.BlockSpec((None, lc, c), lambda i, j: (i, j, 0))
    bw = lambda c: pl.BlockSpec((None, lc, c), lambda i, j: (i, nc - 1 - j, 0))
    dk, dv = MLSTM_HEADS * MLSTM_DK, MLSTM_HEADS * MLSTM_DV
    return pl.pallas_call(
        functools.partial(_mlstm_lat_kernel, n_chunks=nc),
        grid=(b, nc),
        in_specs=[fw(dk), fw(dk), fw(dv), fw(LANES), fw(LANES), bw(dk), bw(dk), bw(dv), bw(LANES), bw(LANES),
                  pl.BlockSpec((None, N_DIRHEAD, MLSTM_DV, MLSTM_DK), lambda i, j: (i, 0, 0, 0)),
                  pl.BlockSpec((None, N_DIRHEAD, MLSTM_DK), lambda i, j: (i, 0, 0)),
                  pl.BlockSpec((None, 1, LANES), lambda i, j: (i, 0, 0))],
        out_specs=pl.BlockSpec((nc, dv, lc), lambda i, j: (i, 0, 0)),
        out_shape=jax.ShapeDtypeStruct((b * nc, dv, lc), F32),
        scratch_shapes=[pltpu.VMEM((N_DIRHEAD, MLSTM_DV, MLSTM_DK), F32),
                        pltpu.VMEM((N_DIRHEAD, MLSTM_DK), F32),
                        pltpu.VMEM((1, LANES), F32)],
        compiler_params=_params(("parallel", "arbitrary")),
        name="mlstm_lat",
    )(mq, mk, mv, gate_b, gate_u, mq, mk, mv, gate_b, gate_u, c0, n0, m0)


def _tail_kernel(x_ref, modin_ref, mod_ref, att_ref, hst_ref, gmix_ref, wgate_ref, gml_ref, womla_ref, womlstm_ref,
                 wout_ref, gffn_ref, wfin_ref, wfout_ref, gfin_ref, y_ref, *, mod_row0, tiles_per_seq):
    shift_mix, scale_mix = _mod_chunks(modin_ref, mod_row0, tiles_per_seq)
    gate_mix, shift_ffn, scale_ffn, gate_ffn = _mod_chunks(mod_ref, mod_row0, tiles_per_seq)
    gml = gml_ref[...]
    tm = x_ref.shape[0]
    groups = [slice(r, r + TAIL_ROWS) for r in range(0, tm, TAIL_ROWS)]
    dv = MLSTM_HEADS * MLSTM_DV
    hb = [_mixer_input(x_ref[g, :], gmix_ref[...], shift_mix, scale_mix) for g in groups]
    smo = [_sigmoid(_dot_nt(h, wgate_ref[:dv, :])) for h in hb]
    g_a = [_sigmoid(_dot_nt(h, wgate_ref[dv:dv + D_MODEL, :])) for h in hb]
    g_b = [_sigmoid(_dot_nt(h, wgate_ref[dv + D_MODEL:, :])) for h in hb]
    a = [_dot(att_ref[g, :], womla_ref[...]) for g in groups]
    hm = []
    for gi in range(len(groups)):
        hs = hst_ref[gi].T
        parts = []
        for hd in range(MLSTM_HEADS):
            sl = slice(hd * MLSTM_DV, (hd + 1) * MLSTM_DV)
            parts.append((_rms(hs[:, sl], gml[:, sl]) * smo[gi][:, sl]).astype(BF16))
        hm.append(jnp.concatenate(parts, axis=1))
    bm = [_dot(h, womlstm_ref[...]) for h in hm]
    merged = [(ga_i * ai + gb_i * bi).astype(BF16) for ga_i, gb_i, ai, bi in zip(g_a, g_b, a, bm)]
    x1 = [x_ref[g, :] + gate_mix * _dot(mi, wout_ref[...]) for g, mi in zip(groups, merged)]
    h2 = [(_rms(xi, gffn_ref[...]) * (1.0 + scale_ffn) + shift_ffn).astype(BF16) for xi in x1]
    ga = [_dot(hi, wfin_ref[:, :FFN_HIDDEN]) for hi in h2]
    gu = [_dot(hi, wfin_ref[:, FFN_HIDDEN:]) for hi in h2]
    act = [(gi * _sigmoid(gi) * ui).astype(BF16) for gi, ui in zip(ga, gu)]
    for g, xi, ci in zip(groups, x1, act):
        y_ref[g, :] = _rms(xi + gate_ffn * _dot(ci, wfout_ref[...]), gfin_ref[...])


def _tail(x, mod_in, mod, att, hst, gmix, w_gate, gml, w_o_mla, w_o_mlstm, w_out, gffn, w_ffn_in, w_ffn_out, gfin, *,
          seq_len, latent):
    n_tok = x.shape[0]
    tm = TOKEN_TILE
    assert hst.shape[0] * TAIL_ROWS == n_tok and hst.shape[2] == TAIL_ROWS
    tok = lambda c: pl.BlockSpec((tm, c), lambda i: (i, 0))
    whole = lambda a: pl.BlockSpec(a.shape, lambda i: (0, 0))
    consts = (gmix, w_gate, gml, w_o_mla, w_o_mlstm, w_out, gffn, w_ffn_in, w_ffn_out, gfin)
    return pl.pallas_call(
        functools.partial(_tail_kernel, mod_row0=1 if latent else 0, tiles_per_seq=seq_len // tm if latent else None),
        grid=(n_tok // tm,),
        in_specs=[tok(D_MODEL), whole(mod_in), whole(mod), tok(att.shape[1]),
                  pl.BlockSpec((tm // TAIL_ROWS,) + hst.shape[1:], lambda i: (i, 0, 0))]
                 + [_const_spec(a.shape) for a in consts],
        out_specs=tok(D_MODEL),
        out_shape=jax.ShapeDtypeStruct((n_tok, D_MODEL), F32),
        compiler_params=_params(("parallel",)),
        name="tail_lat" if latent else "tail_ctx",
    )(x, mod_in, mod, att, hst, *consts)


_IN_SIZES = (Q_RANK, KV_RANK, ROPE_DIM, MLSTM_HEADS * MLSTM_DK, MLSTM_HEADS * MLSTM_DK, MLSTM_HEADS * MLSTM_DV,
             4 * MLSTM_HEADS, MLSTM_HEADS * MLSTM_DV, 2 * D_MODEL)
_IN_OFF = tuple(sum(_IN_SIZES[:i]) for i in range(len(_IN_SIZES) + 1))
PACK_ROWS = LANES
_PAD_AFTER = tuple((_SEG_OFF[i + 1] // PACK_ROWS, _SEG_WIDTHS[i] - _IN_SIZES[i], _IN_SIZES[i] % PACK_ROWS)
                   for i in range(len(_IN_SIZES)) if _SEG_WIDTHS[i] != _IN_SIZES[i])


PACK_PIECES = 8
N_PACK_STEPS = pl.cdiv(SEG_MO[0] // PACK_ROWS, PACK_PIECES)
_N_IN = _IN_OFF[-1]


def _pack_src_row(p):
    src = p * PACK_ROWS
    for first_piece_after, pad, _ in _PAD_AFTER:
        src = src - jnp.where(p >= first_piece_after, pad, 0)
    src = jnp.minimum(src, _N_IN - PACK_ROWS)
    return pl.multiple_of(src, math.gcd(PACK_ROWS, _N_IN, *(pad for _, pad, _ in _PAD_AFTER)))


def _pack_in_kernel(*refs):
    w_refs, o_ref = refs[:PACK_PIECES], refs[PACK_PIECES]
    row = lax.broadcasted_iota(jnp.int32, w_refs[0].shape, 0)
    for j, w_ref in enumerate(w_refs):
        p = pl.program_id(0) * PACK_PIECES + j
        valid = jnp.where(p < N_IN_PACKED // PACK_ROWS, PACK_ROWS, 0)
        for first_piece_after, _, width in _PAD_AFTER:
            valid = jnp.where(p == first_piece_after - 1, width, valid)
        o_ref[j * PACK_ROWS:(j + 1) * PACK_ROWS, :] = jnp.where(row < valid, w_ref[...], 0.0).astype(BF16)


def _pack_in(w_in_t):
    n, k = w_in_t.shape
    piece = lambda j: pl.BlockSpec((pl.Element(PACK_ROWS), pl.Element(k)),
                                   lambda i: (_pack_src_row(i * PACK_PIECES + j), 0))
    return pl.pallas_call(
        _pack_in_kernel,
        grid=(N_PACK_STEPS,),
        in_specs=[piece(j) for j in range(PACK_PIECES)],
        out_specs=pl.BlockSpec((PACK_PIECES * PACK_ROWS, k), lambda i: (i, 0)),
        out_shape=jax.ShapeDtypeStruct((N_PACK_STEPS * PACK_PIECES * PACK_ROWS, k), BF16),
        compiler_params=_params(("parallel",)),
        name="pack_in",
    )(*([w_in_t] * PACK_PIECES))


def _pack_small(w_uq, w_ukv, b_gates):
    qd = NOPE_DIM + ROPE_DIM
    w_uq_p = jnp.pad(w_uq.reshape(Q_RANK, MLA_HEADS, qd), ((0, 0), (0, 0), (0, HEAD_BLOCK - qd)))
    w_uq_p = w_uq_p.reshape(Q_RANK, MLA_HEADS * HEAD_BLOCK).astype(BF16)
    kv = w_ukv.reshape(KV_RANK, MLA_HEADS, NOPE_DIM + MLA_V_DIM)
    wk = jnp.pad(kv[:, :, :NOPE_DIM], ((0, 0), (0, 0), (0, HEAD_BLOCK - NOPE_DIM))).reshape(KV_RANK, -1)
    wv = kv[:, :, NOPE_DIM:].reshape(KV_RANK, -1)
    w_ukv_p = jnp.concatenate([wk, wv], axis=1).astype(BF16)
    bg = jnp.pad(b_gates, ((0, 0), (0, LANES - b_gates.shape[1])))
    return w_uq_p, w_ukv_p, bg


def _rope_tables(n_tokens):
    pos = np.arange(n_tokens)
    row = (pos // GRID_W).astype(np.float64)[:, None]
    col = (pos % GRID_W).astype(np.float64)[:, None]
    half = ROPE_DIM // 2
    inv = (np.float32(ROPE_BASE) ** (-np.arange(0, half, 2, dtype=np.float32) / np.float32(half))).astype(np.float64)
    r = np.arange(LANES) - ROPE_LANE0
    in_rope = (r >= 0) & (r < ROPE_DIM)
    rr = np.clip(r, 0, ROPE_DIM - 1)
    freq = inv[rr % (half // 2)][None, :]
    ang = np.where((rr // half == 0)[None, :], row * freq, col * freq).astype(np.float32).astype(np.float64)
    first = (rr % half) < (half // 2)
    cos = np.where(in_rope[None, :], np.cos(ang), 1.0)
    sin = np.sin(ang)
    sin_lo = np.where((in_rope & first)[None, :], -sin, 0.0)
    sin_hi = np.where((in_rope & ~first)[None, :], sin, 0.0)
    return tuple(jnp.asarray(t, dtype=F32) for t in (cos, sin_lo, sin_hi))


def kernel(x_prompt, x_sample, cache_ckv, cache_krope, state_C, state_n, state_m, c, c_ctx, w_mod, b_mod, g_norm_mix,
           w_in, b_gates, g_q_norm, w_uq, g_kv_norm, w_ukv, g_mlstm_norm, w_o_mla, w_o_mlstm, w_out, g_norm_ffn,
           w_ffn_in, w_ffn_out, g_final):
    bp, sp, d = x_prompt.shape
    bs, ss, _ = x_sample.shape
    layer = 0
    assert w_mod.shape[0] == 1 and sp == MLSTM_CHUNK and ss % MLSTM_CHUNK == 0

    adaln_args = (c_ctx[None, :], c, w_mod[layer], b_mod[layer][None, :])
    mod_in = _adaln(*adaln_args)

    w_in_t = w_in[layer].T
    w_in_p = _pack_in(w_in_t)
    w_uq_p, w_ukv_p, bg = _pack_small(w_uq[layer], w_ukv[layer], b_gates[layer][None, :])
    row = lambda g: g[layer][None, :]
    shared_in = (row(g_norm_mix), w_in_p, row(g_q_norm), w_uq_p, row(g_kv_norm), w_ukv_p, bg)
    seq3 = lambda a, b_, s_: a.reshape(b_, s_, a.shape[-1])

    xl = x_sample.reshape(bs * ss, d)
    q, kc, v, mq_l, mk_l, mv_l, gb_l, gu_l = _inproj(xl, mod_in, _rope_tables(ss), *shared_in, seq_len=ss)
    krope_blk = jnp.pad(cache_krope[:, layer], ((0, 0), (0, 0), (ROPE_LANE0, LANES - ROPE_LANE0 - ROPE_DIM)))
    cache = (cache_ckv[:, layer], krope_blk, w_ukv_p)
    gate_cols = (w_in_t, _IN_OFF[7], _IN_OFF[9] - _IN_OFF[7])
    att_l, wgate_b, womla_b, womlstm_b, wout_b, wfin_b, wfout_b, mod_tail = _attention(
        seq3(q, bs, ss), seq3(kc, bs, ss), seq3(v, bs, ss), cache,
        cast_weights=(gate_cols, w_o_mla[layer], w_o_mlstm[layer], w_out[layer], w_ffn_in[layer], w_ffn_out[layer]),
        adaln_tail=adaln_args)
    tail_w = (row(g_norm_mix), wgate_b, row(g_mlstm_norm), womla_b, womlstm_b, wout_b, row(g_norm_ffn), wfin_b, wfout_b,
              g_final[None, :])

    xc = x_prompt.reshape(bp * sp, d)
    att, hs, new_c, new_n, new_m, ckv, kpe = _ctx_mixer(xc, mod_in, *shared_in, seq_len=sp)
    y_prompt = _tail(xc, mod_in, mod_tail, att.reshape(bp * sp, -1), hs, *tail_w,
                     seq_len=sp, latent=False).reshape(bp, sp, d)
    new_ckv = ckv.reshape(bp, 1, sp, KV_RANK)
    new_krope = jnp.swapaxes(kpe, 1, 2).reshape(bp, 1, sp, ROPE_DIM)
    new_C = new_c.reshape(bp, 1, 2, MLSTM_HEADS, MLSTM_DV, MLSTM_DK)
    new_N = new_n.reshape(bp, 1, 2, MLSTM_HEADS, MLSTM_DK)
    new_M = new_m[:, 0, :N_DIRHEAD].reshape(bp, 1, 2, MLSTM_HEADS)

    c0 = state_C[:, layer].reshape(bs, N_DIRHEAD, MLSTM_DV, MLSTM_DK)
    n0 = state_n[:, layer].reshape(bs, N_DIRHEAD, MLSTM_DK)
    m0 = jnp.pad(state_m[:, layer].reshape(bs, 1, N_DIRHEAD), ((0, 0), (0, 0), (0, LANES - N_DIRHEAD)))
    hs = _mlstm_lat(seq3(mq_l, bs, ss), seq3(mk_l, bs, ss), seq3(mv_l, bs, ss), seq3(gb_l, bs, ss), seq3(gu_l, bs, ss),
                    c0, n0, m0)
    y_sample = _tail(xl, mod_in, mod_tail, att_l.reshape(bs * ss, -1), hs, *tail_w,
                     seq_len=ss, latent=True).reshape(bs, ss, d)
    return (y_prompt, y_sample, new_ckv, new_krope, new_C, new_N, new_M)
```

```python
import functools
import math

import jax
import jax.numpy as jnp
import numpy as np
from jax import lax
from jax.experimental import pallas as pl
from jax.experimental.pallas import tpu as pltpu

F32 = jnp.float32
BF16 = jnp.bfloat16

D_MODEL = 1024
GRID_W = 64
MLA_HEADS = 8
Q_RANK = 384
KV_RANK = 256
NOPE_DIM = 64
ROPE_DIM = 32
MLA_V_DIM = 64
ROPE_BASE = 10000.0
MLA_SCALE = (NOPE_DIM + ROPE_DIM) ** -0.5
MLSTM_HEADS = 4
MLSTM_DK = 128
MLSTM_DV = 256
FFN_HIDDEN = ((8 * D_MODEL // 3 + 255) // 256) * 256
EPS = 1e-6

LANES = 128
HEAD_BLOCK = LANES
ROPE_LANE0 = NOPE_DIM
N_DIRHEAD = 2 * MLSTM_HEADS
MLSTM_CHUNK = 256
LOG2E = math.log2(math.e)
Q_PRESCALE = MLA_SCALE * LOG2E
V_SLAB = 2 * LANES
V_WIDTH = (MLA_HEADS // 2) * V_SLAB
TOKEN_TILE = 512
TAIL_ROWS = 256
INPROJ_ROWS = TOKEN_TILE
Q_TILE = 256
CTX_SEQS_PER_STEP = 4
VMEM_LIMIT = 56 * 1024 * 1024

_SEG_WIDTHS = (Q_RANK, KV_RANK, LANES, MLSTM_HEADS * MLSTM_DK, MLSTM_HEADS * MLSTM_DK,
               MLSTM_HEADS * MLSTM_DV, LANES, MLSTM_HEADS * MLSTM_DV, 2 * D_MODEL)
_SEG_OFF = tuple(sum(_SEG_WIDTHS[:i]) for i in range(len(_SEG_WIDTHS) + 1))
SEG_Q, SEG_KV, SEG_KPE, SEG_MQ, SEG_MK, SEG_MV, SEG_GATE, SEG_MO, SEG_BR = (
    (_SEG_OFF[i], _SEG_OFF[i + 1]) for i in range(9))
N_IN_PACKED = _SEG_OFF[-1]


def _dot(a, b):
    return jnp.dot(a, b, preferred_element_type=F32)


def _dot_nt(a, b):
    return lax.dot_general(a, b, (((1,), (1,)), ((), ())), preferred_element_type=F32)


def _rms(x, g):
    ms = jnp.mean(x * x, axis=-1, keepdims=True)
    return x * lax.rsqrt(ms + EPS) * g


def _sigmoid(x):
    return 1.0 / (1.0 + jnp.exp(-x))


def _const_spec(shape):
    nd = len(shape)
    return pl.BlockSpec(shape, lambda *_: (0,) * nd, pipeline_mode=pl.Buffered(1))


def _mod_chunks(mod_ref, row0, tiles_per_seq):
    row = row0 if tiles_per_seq is None else row0 + pl.program_id(0) // tiles_per_seq
    m = mod_ref[pl.ds(row, 1), :]
    return [m[:, k * D_MODEL:(k + 1) * D_MODEL] for k in range(m.shape[1] // D_MODEL)]


def _params(sem, flags=None, vmem_mib=None):
    limit = VMEM_LIMIT if vmem_mib is None else vmem_mib * 1024 * 1024
    return pltpu.CompilerParams(dimension_semantics=sem, vmem_limit_bytes=limit, flags=flags)


N_MOD_MIXER = 2
N_MOD_TAIL = 4


COND_ROWS = 8


def _adaln_block(cctx_ref, c_ref, w_ref, b_ref, o_ref):
    pad = jnp.zeros((COND_ROWS - 1 - c_ref.shape[0], D_MODEL), F32)
    c = jnp.concatenate([cctx_ref[...], c_ref[...], pad], axis=0)
    s = c * _sigmoid(c)
    o_ref[...] = _dot(s.astype(BF16), w_ref[...].astype(BF16)) + b_ref[...]


def _rope_block(x, cos, sin_lo, sin_hi):
    return x * cos + pltpu.roll(x, LANES - 8, 1) * sin_lo + pltpu.roll(x, 8, 1) * sin_hi


def _store_kv(kvn, kp, kc_ref, v_ref, g=slice(None)):
    for hd in range(MLA_HEADS):
        sl = slice(hd * HEAD_BLOCK, (hd + 1) * HEAD_BLOCK)
        kc_ref[g, sl] = (kvn[:, sl] + kp).astype(BF16)
    v0 = MLA_HEADS * HEAD_BLOCK
    ones = jnp.ones((kvn.shape[0], LANES), BF16)
    for pair in range(MLA_HEADS // 2):
        v_ref[g, pair * V_SLAB:pair * V_SLAB + LANES] = kvn[:, v0 + pair * LANES:v0 + (pair + 1) * LANES].astype(BF16)
        v_ref[g, pair * V_SLAB + LANES:(pair + 1) * V_SLAB] = ones


def _mixer_input(x, g, shift, scale):
    return (_rms(x, g) * (1.0 + scale) + shift).astype(BF16)


def _inproj_kernel(*refs, rope, mod_row0, tiles_per_seq):
    if rope:
        (x_ref, mod_ref, cos_ref, slo_ref, shi_ref, gmix_ref, win_ref, gq_ref, wuq_ref, gkv_ref, wukv_ref,
         bg_ref, q_ref, kc_ref, v_ref, mq_ref, mk_ref, mv_ref, gb_ref, gu_ref) = refs
    else:
        (x_ref, mod_ref, gmix_ref, win_ref, gq_ref, wuq_ref, gkv_ref, wukv_ref,
         bg_ref, q_ref, kc_ref, v_ref, mq_ref, mk_ref, mv_ref, gate_ref, ckv_ref, kpet_ref) = refs
    shift, scale = _mod_chunks(mod_ref, mod_row0, tiles_per_seq)
    for r0 in range(0, x_ref.shape[0], INPROJ_ROWS):
        g = slice(r0, r0 + INPROJ_ROWS)
        if rope:
            cos, slo, shi = cos_ref[g, :], slo_ref[g, :], shi_ref[g, :]
        hb = _mixer_input(x_ref[g, :], gmix_ref[...], shift, scale)

        def proj(seg):
            return _dot_nt(hb, win_ref[seg[0]:seg[1], :])

        low = proj((SEG_Q[0], SEG_KPE[1]))
        qn = _rms(low[:, SEG_Q[0]:SEG_Q[1]], gq_ref[...]).astype(BF16)
        ckv = _rms(low[:, SEG_KV[0]:SEG_KV[1]], gkv_ref[...])
        zkpe = low[:, SEG_KPE[0]:SEG_KPE[1]]
        gates = proj(SEG_GATE) + bg_ref[...]
        if rope:
            for c0 in range(0, INPROJ_ROWS, MLSTM_CHUNK):
                b, u, _ = _gate_prep(gates[c0:c0 + MLSTM_CHUNK, :])
                gb_ref[r0 + c0:r0 + c0 + MLSTM_CHUNK, :] = b
                gu_ref[r0 + c0:r0 + c0 + MLSTM_CHUNK, :] = u
        else:
            gate_ref[g, :] = gates

        q = _dot(qn, wuq_ref[...])
        for hd in range(MLA_HEADS):
            sl = slice(hd * HEAD_BLOCK, (hd + 1) * HEAD_BLOCK)
            qh = q[:, sl]
            if rope:
                qh = _rope_block(qh, cos, slo, shi)
            q_ref[g, sl] = (qh * Q_PRESCALE).astype(BF16)

        if not rope:
            ckv_ref[g, :] = ckv
            kpe_t = zkpe.T
            seq = kpet_ref.shape[2]
            for sq in range(INPROJ_ROWS // seq):
                kpet_ref[r0 // seq + sq] = kpe_t[:ROPE_DIM, sq * seq:(sq + 1) * seq]
        kp = pltpu.roll(zkpe, ROPE_LANE0, 1)
        if rope:
            kp = _rope_block(kp, cos, slo, shi)
        kvn = _dot(ckv.astype(BF16), wukv_ref[...])
        _store_kv(kvn, kp, kc_ref, v_ref, g)

        mv_ref[g, :] = proj(SEG_MV).astype(BF16)
        mk_ref[g, :] = (proj(SEG_MK) * (MLSTM_DK ** -0.5)).astype(BF16)
        mq_ref[g, :] = proj(SEG_MQ).astype(BF16)


def _inproj(x, mod, rope_tabs, gmix, w_in_p, gq, w_uq_p, gkv, w_ukv_p, bg, *, seq_len):
    n_tok = x.shape[0]
    tm = TOKEN_TILE
    tiles_per_seq = seq_len // tm
    rope = rope_tabs is not None
    tok = lambda c: pl.BlockSpec((tm, c), lambda i: (i, 0))
    in_specs = [tok(D_MODEL), pl.BlockSpec(mod.shape, lambda i: (0, 0))]
    args = [x, mod]
    if rope:
        in_specs += [pl.BlockSpec((tm, LANES), lambda i: (i % tiles_per_seq, 0))] * 3
        args += list(rope_tabs)
    win_rows = SEG_MO[0]
    in_specs += [_const_spec(gmix.shape), _const_spec((win_rows, w_in_p.shape[1]))]
    in_specs += [_const_spec(a.shape) for a in (gq, w_uq_p, gkv, w_ukv_p, bg)]
    args += [gmix, w_in_p, gq, w_uq_p, gkv, w_ukv_p, bg]
    widths = [(MLA_HEADS * HEAD_BLOCK, BF16), (MLA_HEADS * HEAD_BLOCK, BF16), (V_WIDTH, BF16),
              (MLSTM_HEADS * MLSTM_DK, BF16), (MLSTM_HEADS * MLSTM_DK, BF16), (MLSTM_HEADS * MLSTM_DV, BF16),
              (LANES, F32)]
    if rope:
        widths.append((LANES, F32))
    out_specs = [tok(c) for c, _ in widths]
    out_shape = [jax.ShapeDtypeStruct((n_tok, c), dt) for c, dt in widths]
    if not rope:
        out_specs += [tok(KV_RANK), pl.BlockSpec((tm // seq_len, ROPE_DIM, seq_len), lambda i: (i, 0, 0))]
        out_shape += [jax.ShapeDtypeStruct((n_tok, KV_RANK), F32),
                      jax.ShapeDtypeStruct((n_tok // seq_len, ROPE_DIM, seq_len), F32)]
    return pl.pallas_call(
        functools.partial(_inproj_kernel, rope=rope, mod_row0=1 if rope else 0,
                          tiles_per_seq=tiles_per_seq if rope else None),
        grid=(n_tok // tm,),
        in_specs=in_specs,
        out_specs=out_specs,
        out_shape=out_shape,
        compiler_params=_params(("parallel",)),
        name="inproj_lat" if rope else "inproj_ctx",
    )(*args)


def _attn_kernel(*refs, has_cache, n_cast, has_adaln):
    n_in = 6 if has_cache else 3
    if has_cache:
        q_ref, k_ref, v_ref, ckv_ref, kp_ref, wukv_ref = refs[:n_in]
        kc_ref, vc_ref = refs[-2:]

        @pl.when(pl.program_id(1) == 0)
        def _():
            kvn = _dot(ckv_ref[0].astype(BF16), wukv_ref[...])
            _store_kv(kvn, kp_ref[0], kc_ref.at[0], vc_ref.at[0])
    else:
        q_ref, k_ref, v_ref = refs[:n_in]
    n_all_in = n_in + n_cast + (4 if has_adaln else 0)
    o_ref = refs[n_all_in]
    for w_ref, wb_ref in zip(refs[n_in:n_in + n_cast], refs[n_all_in + 1:n_all_in + 1 + n_cast]):
        wb_ref[...] = w_ref[...].astype(BF16)
    if has_adaln:
        _adaln_block(*refs[n_in + n_cast:n_all_in], refs[n_all_in + 1 + n_cast])
    tq = q_ref.shape[1]
    lane = lax.broadcasted_iota(jnp.int32, (tq, LANES), 1)
    for sq in range(q_ref.shape[0]):
        hsl = [slice(hd * HEAD_BLOCK, (hd + 1) * HEAD_BLOCK) for hd in range(MLA_HEADS)]
        vsl = [slice(hd // 2 * V_SLAB, (hd // 2 + 1) * V_SLAB) for hd in range(MLA_HEADS)]
        s = [_dot_nt(q_ref[sq, :, sl], k_ref[sq, :, sl]) for sl in hsl]
        m = [jnp.max(si, axis=-1, keepdims=True) for si in s]
        if has_cache:
            sc = [_dot_nt(q_ref[sq, :, sl], kc_ref[sq, :, sl]) for sl in hsl]
            m = [jnp.maximum(mi, jnp.max(ci, axis=-1, keepdims=True)) for mi, ci in zip(m, sc)]
        p = [jnp.exp2(si - mi).astype(BF16) for si, mi in zip(s, m)]
        o = [_dot(pi, v_ref[sq, :, sl]) for pi, sl in zip(p, vsl)]
        if has_cache:
            pc = [jnp.exp2(ci - mi).astype(BF16) for ci, mi in zip(sc, m)]
            o = [oi + _dot(pi, vc_ref[sq, :, sl]) for oi, pi, sl in zip(o, pc, vsl)]
        outs = [oi[:, :LANES] * (1.0 / oi[:, LANES:]) for oi in o]
        for pair in range(MLA_HEADS // 2):
            o_ref[sq, :, pair * LANES:(pair + 1) * LANES] = jnp.where(
                lane < MLA_V_DIM, outs[2 * pair], outs[2 * pair + 1]).astype(BF16)


def _attention(q, k, v, cache=None, cast_weights=(), adaln_tail=None):
    nb, s, _ = q.shape
    tq = Q_TILE
    ns = 1
    b = nb // ns
    n_steps = b * (s // tq)
    kw, vw, ow = MLA_HEADS * HEAD_BLOCK, V_WIDTH, MLA_HEADS * MLA_V_DIM
    in_specs = [pl.BlockSpec((ns, tq, kw), lambda i, j: (i, j, 0)),
                pl.BlockSpec((ns, s, kw), lambda i, j: (i, 0, 0)),
                pl.BlockSpec((ns, s, vw), lambda i, j: (i, 0, 0))]
    args = [q, k, v]
    scratch = []
    if cache is not None:
        past = cache[0].shape[1]
        in_specs += [pl.BlockSpec((ns, past, KV_RANK), lambda i, j: (i, 0, 0)),
                     pl.BlockSpec((ns, past, LANES), lambda i, j: (i, 0, 0)),
                     _const_spec(cache[2].shape)]
        args += list(cache)
        scratch = [pltpu.VMEM((ns, past, kw), BF16), pltpu.VMEM((ns, past, vw), BF16)]
    step = lambda i, j: i * (s // tq) + j
    casts = [w if isinstance(w, tuple) else (w, 0, w.shape[0]) for w in cast_weights]
    cast_weights = [w for w, _, _ in casts]

    def cast_in_spec(w, row0, n_rows):
        rows = n_rows // n_steps
        if (row0, n_rows) == (0, w.shape[0]):
            return pl.BlockSpec((rows, w.shape[1]), lambda i, j: (step(i, j), 0))
        return pl.BlockSpec((pl.Element(rows), pl.Element(w.shape[1])),
                            lambda i, j: (pl.multiple_of(row0 + step(i, j) * rows, math.gcd(row0, rows)), 0))

    w_in_specs = [cast_in_spec(*c) for c in casts]
    w_specs = [pl.BlockSpec((n // n_steps, w.shape[1]), lambda i, j: (step(i, j), 0)) for w, _, n in casts]
    extra_in, extra_out, extra_shape = [], [], []
    if adaln_tail is not None:
        tn = N_MOD_TAIL * D_MODEL // n_steps
        col0 = N_MOD_MIXER * D_MODEL // tn
        extra_in = [pl.BlockSpec(adaln_tail[0].shape, lambda i, j: (0, 0)),
                    pl.BlockSpec(adaln_tail[1].shape, lambda i, j: (0, 0)),
                    pl.BlockSpec((D_MODEL, tn), lambda i, j: (0, col0 + step(i, j))),
                    pl.BlockSpec((1, tn), lambda i, j: (0, col0 + step(i, j)))]
        extra_out = [pl.BlockSpec((COND_ROWS, tn), lambda i, j: (0, step(i, j)))]
        extra_shape = [jax.ShapeDtypeStruct((COND_ROWS, N_MOD_TAIL * D_MODEL), F32)]
    outs = pl.pallas_call(
        functools.partial(_attn_kernel, has_cache=cache is not None, n_cast=len(cast_weights),
                          has_adaln=adaln_tail is not None),
        grid=(b, s // tq),
        in_specs=in_specs + w_in_specs + extra_in,
        out_specs=[pl.BlockSpec((ns, tq, ow), lambda i, j: (i, j, 0))] + w_specs + extra_out,
        out_shape=[jax.ShapeDtypeStruct((nb, s, ow), BF16)]
                  + [jax.ShapeDtypeStruct((n, w.shape[1]), BF16) for w, _, n in casts] + extra_shape,
        scratch_shapes=scratch,
        compiler_params=_params(("parallel", "arbitrary")),
        name="attn_lat" if cache is not None else "attn_ctx",
    )(*args, *cast_weights, *(adaln_tail or ()))
    return outs[0] if len(outs) == 1 else outs


def _lane_bcast(x, k, width=None):
    y = jnp.broadcast_to(x[:, k:k + 1], x.shape)
    reps = (width or LANES) // LANES
    return y if reps == 1 else jnp.concatenate([y] * reps, axis=1)


def _prefix_sum_rows(x):
    n = x.shape[0]
    row = lax.broadcasted_iota(jnp.int32, x.shape, 0)
    shift = 1
    while shift < n:
        x = x + jnp.where(row >= shift, pltpu.roll(x, shift, 0), 0.0)
        shift *= 2
    return x


def _gate_prep(g):
    n = g.shape[0]
    lane = lax.broadcasted_iota(jnp.int32, (n, LANES), 1)
    fpre = pltpu.roll(g, LANES - N_DIRHEAD, 1)
    lf = jnp.minimum(fpre, 0.0) - jnp.log(1.0 + jnp.exp(-jnp.abs(fpre)))
    binc = _prefix_sum_rows(lf)
    tot = binc[n - 1:n, :]
    b = jnp.where(lane < MLSTM_HEADS, binc, tot - binc + lf)
    return b, g - b, tot


def _chunk_setup(b, u, tot, m_prev):
    g_last = jnp.maximum(jnp.max(u, axis=0, keepdims=True), m_prev)
    u2 = u * LOG2E
    w = jnp.exp2(u2 - g_last * LOG2E)
    return u2, (b * LOG2E).T, w.T, tot + g_last, jnp.exp(m_prev - g_last)


def _state_update(vt, k, wt, kk):
    c_new = _dot((vt * wt[kk:kk + 1, :]).astype(BF16), k)
    n_new = _dot(wt[:N_DIRHEAD, :].astype(BF16), k)[kk:kk + 1, :]
    return c_new, n_new


def _dir_weights(s0t, u_b, m_row, *, fwd):
    n = s0t.shape[0]
    r = lax.broadcasted_iota(jnp.int32, (n, n), 0)
    c = lax.broadcasted_iota(jnp.int32, (n, n), 1)
    a = jnp.where((r <= c) if fwd else (r >= c), u_b, -jnp.inf)
    g = jnp.maximum(jnp.max(a, axis=0, keepdims=True), m_row)
    sdt = s0t * jnp.exp2(a - g)
    return sdt.astype(BF16), jnp.sum(sdt, axis=0, keepdims=True), g


def _dir_finish(ht, den, g, bt_row, m_row, inter=None):
    if inter is not None:
        w_inter = jnp.exp2(m_row - g)
        ht = ht + w_inter * inter[0]
        den = den + w_inter * inter[1]
    floor = jnp.exp2(-(bt_row + g))
    return ht * (1.0 / jnp.maximum(jnp.abs(den), floor))


def _mlstm_ctx_kernel(q_ref, k_ref, v_ref, g_ref, ht_ref, c_ref, n_ref, m_ref):
    n = q_ref.shape[1]
    zero_lanes = jnp.zeros((1, LANES), F32)
    zero_row = jnp.zeros((1, n), F32)
    dirs = ((True, 0), (False, MLSTM_HEADS))
    setup = [_chunk_setup(*_gate_prep(g_ref[sq]), zero_lanes) for sq in range(q_ref.shape[0])]
    units = [(sq, hd) for sq in range(q_ref.shape[0]) for hd in range(MLSTM_HEADS)]
    ks = [k_ref[sq, :, hd * MLSTM_DK:(hd + 1) * MLSTM_DK] for sq, hd in units]
    s0ts = [_dot_nt(k, q_ref[sq, :, hd * MLSTM_DK:(hd + 1) * MLSTM_DK]) for k, (sq, hd) in zip(ks, units)]
    vts = [v_ref[sq, :, hd * MLSTM_DV:(hd + 1) * MLSTM_DV].T for sq, hd in units]
    wts = [[_dir_weights(s0t, _lane_bcast(setup[sq][0], off + hd, n), zero_row, fwd=fwd) for fwd, off in dirs]
           for s0t, (sq, hd) in zip(s0ts, units)]
    hts = [[_dot(vt, sdt) for sdt, _, _ in wt] for vt, wt in zip(vts, wts)]
    for (sq, hd), k, vt, wt, ht2 in zip(units, ks, vts, wts, hts):
        _, bt2, w_t, _, _ = setup[sq]
        ht_ref[sq, hd * MLSTM_DV:(hd + 1) * MLSTM_DV, :] = sum(
            _dir_finish(ht_d, den, g, bt2[off + hd:off + hd + 1, :], zero_row)
            for ht_d, (_, den, g), (_, off) in zip(ht2, wt, dirs))
        for _, off in dirs:
            c_ref[sq, off + hd], n_ref[sq, off + hd:off + hd + 1, :] = _state_update(vt, k, w_t, off + hd)
    for sq in range(q_ref.shape[0]):
        m_ref[sq] = setup[sq][3]


class _SeqView:
    def __init__(self, ref, seq_len):
        self.ref, self.seq_len = ref, seq_len
        self.shape = (ref.shape[0] // seq_len, seq_len, ref.shape[1])

    def __getitem__(self, idx):
        sq, rows, cols = idx if isinstance(idx, tuple) else (idx, slice(None), slice(None))
        assert rows == slice(None)
        return self.ref[sq * self.seq_len:(sq + 1) * self.seq_len, cols]


N_CTX_CONSTS = 7


def _ctx_mixer_kernel(*refs, seq_len):
    x_ref, mod_ref = refs[:2]
    consts = refs[2:2 + N_CTX_CONSTS]
    att_ref, ht_ref, c_ref, n_ref, m_ref, ckv_ref, kpet_ref = refs[2 + N_CTX_CONSTS:9 + N_CTX_CONSTS]
    q_s, kc_s, v_s, mq_s, mk_s, mv_s, gate_s = refs[9 + N_CTX_CONSTS:]
    _inproj_kernel(x_ref, mod_ref, *consts, q_s, kc_s, v_s, mq_s, mk_s, mv_s, gate_s, ckv_ref, kpet_ref,
                   rope=False, mod_row0=0, tiles_per_seq=None)
    q3, k3, v3, mq3, mk3, mv3, g3 = (_SeqView(r, seq_len) for r in (q_s, kc_s, v_s, mq_s, mk_s, mv_s, gate_s))
    _attn_kernel(q3, k3, v3, att_ref, has_cache=False, n_cast=0, has_adaln=False)
    _mlstm_ctx_kernel(mq3, mk3, mv3, g3, ht_ref, c_ref, n_ref, m_ref)


def _ctx_mixer(x, mod, gmix, w_in_p, gq, w_uq_p, gkv, w_ukv_p, bg, *, seq_len):
    n_tok = x.shape[0]
    ns, s = CTX_SEQS_PER_STEP, seq_len
    nseq, tm = n_tok // s, ns * s
    dv = MLSTM_HEADS * MLSTM_DV
    consts = (gmix, w_in_p, gq, w_uq_p, gkv, w_ukv_p, bg)
    const_specs = [_const_spec(gmix.shape), _const_spec((SEG_MO[0], w_in_p.shape[1]))]
    const_specs += [_const_spec(a.shape) for a in consts[2:]]
    seq_blk = lambda *dims: pl.BlockSpec((ns,) + dims, lambda i: (i,) + (0,) * len(dims))
    return pl.pallas_call(
        functools.partial(_ctx_mixer_kernel, seq_len=s),
        grid=(nseq // ns,),
        in_specs=[pl.BlockSpec((tm, D_MODEL), lambda i: (i, 0)), pl.BlockSpec(mod.shape, lambda i: (0, 0))]
                 + const_specs,
        out_specs=[seq_blk(s, MLA_HEADS * MLA_V_DIM), seq_blk(dv, s), seq_blk(N_DIRHEAD, MLSTM_DV, MLSTM_DK),
                   seq_blk(N_DIRHEAD, MLSTM_DK), seq_blk(1, LANES),
                   pl.BlockSpec((tm, KV_RANK), lambda i: (i, 0)), seq_blk(ROPE_DIM, s)],
        out_shape=[jax.ShapeDtypeStruct((nseq, s, MLA_HEADS * MLA_V_DIM), BF16),
                   jax.ShapeDtypeStruct((nseq, dv, s), F32),
                   jax.ShapeDtypeStruct((nseq, N_DIRHEAD, MLSTM_DV, MLSTM_DK), F32),
                   jax.ShapeDtypeStruct((nseq, N_DIRHEAD, MLSTM_DK), F32),
                   jax.ShapeDtypeStruct((nseq, 1, LANES), F32),
                   jax.ShapeDtypeStruct((n_tok, KV_RANK), F32),
                   jax.ShapeDtypeStruct((nseq, ROPE_DIM, s), F32)],
        scratch_shapes=[pltpu.VMEM((tm, MLA_HEADS * HEAD_BLOCK), BF16), pltpu.VMEM((tm, MLA_HEADS * HEAD_BLOCK), BF16),
                        pltpu.VMEM((tm, V_WIDTH), BF16), pltpu.VMEM((tm, MLSTM_HEADS * MLSTM_DK), BF16),
                        pltpu.VMEM((tm, MLSTM_HEADS * MLSTM_DK), BF16), pltpu.VMEM((tm, dv), BF16),
                        pltpu.VMEM((tm, LANES), F32)],
        compiler_params=_params(("parallel",)),
        name="ctx_mixer",
    )(x, mod, *consts)


def _mlstm_lat_kernel(qf_ref, kf_ref, vf_ref, bf_ref, uf_ref, qb_ref, kb_ref, vb_ref, bb_ref, ub_ref,
                      c0_ref, n0_ref, m0_ref,
                      ht_ref, c_ref, n_ref, m_ref, *, n_chunks):
    step = pl.program_id(1)

    @pl.when(step == 0)
    def _():
        c_ref[...] = c0_ref[...]
        n_ref[...] = n0_ref[...]
        m_ref[...] = m0_ref[...]
        ht_ref[...] = jnp.zeros_like(ht_ref)

    n = qf_ref.shape[0]
    is_f = lax.broadcasted_iota(jnp.int32, (n, LANES), 1) < MLSTM_HEADS
    b = jnp.where(is_f, bf_ref[...], bb_ref[...])
    u = jnp.where(is_f, uf_ref[...], ub_ref[...])
    tot = jnp.where(is_f[:1], bf_ref[n - 1:n, :], bb_ref[0:1, :])
    m_prev = m_ref[...]
    u2, bt2, w_t, m_new, decay = _chunk_setup(b, u, tot, m_prev)
    m2_prev = m_prev * LOG2E
    units = [(fwd, hd, hd if fwd else MLSTM_HEADS + hd) for fwd in (True, False) for hd in range(MLSTM_HEADS)]
    refs = {True: (qf_ref, kf_ref, vf_ref), False: (qb_ref, kb_ref, vb_ref)}
    qs = [refs[fwd][0][:, hd * MLSTM_DK:(hd + 1) * MLSTM_DK] for fwd, hd, _ in units]
    ks = [refs[fwd][1][:, hd * MLSTM_DK:(hd + 1) * MLSTM_DK] for fwd, hd, _ in units]
    vts = [refs[fwd][2][:, hd * MLSTM_DV:(hd + 1) * MLSTM_DV].T for fwd, hd, _ in units]
    s0ts = [_dot_nt(k, q) for k, q in zip(ks, qs)]
    c_prevs = [c_ref[kk] for _, _, kk in units]
    n_prevs = [n_ref[kk:kk + 1, :] for _, _, kk in units]
    inters = [(_dot_nt(c.astype(BF16), q), _dot_nt(jnp.broadcast_to(nv, (8, MLSTM_DK)).astype(BF16), q)[0:1, :])
              for c, nv, q in zip(c_prevs, n_prevs, qs)]
    m_rows = [_lane_bcast(m2_prev, kk, n) for _, _, kk in units]
    wts = [_dir_weights(s0t, _lane_bcast(u2, kk, n), m_row, fwd=fwd)
           for s0t, m_row, (fwd, _, kk) in zip(s0ts, m_rows, units)]
    hts = [_dot(vt, sdt) for vt, (sdt, _, _) in zip(vts, wts)]
    for (fwd, hd, kk), k, vt, (_, den, g), ht, inter, m_row, c_prev, n_prev in zip(
            units, ks, vts, wts, hts, inters, m_rows, c_prevs, n_prevs):
        chunk = step if fwd else n_chunks - 1 - step
        ht_ref[chunk, hd * MLSTM_DV:(hd + 1) * MLSTM_DV, :] += _dir_finish(ht, den, g, bt2[kk:kk + 1, :], m_row, inter)
        c_new, n_new = _state_update(vt, k, w_t, kk)
        dk = _lane_bcast(decay, kk)
        c_ref[kk] = dk * c_prev + c_new
        n_ref[kk:kk + 1, :] = dk * n_prev + n_new
    m_ref[...] = m_new


def _mlstm_lat(mq, mk, mv, gate_b, gate_u, c0, n0, m0):
    b, s, _ = mq.shape
    lc = MLSTM_CHUNK
    nc = s // lc
    fw = lambda c: pl.BlockSpec((None, lc, c), lambda i, j: (i, j, 0))
    bw = lambda c: pl.BlockSpec((None, lc, c), lambda i, j: (i, nc - 1 - j, 0))
    dk, dv = MLSTM_HEADS * MLSTM_DK, MLSTM_HEADS * MLSTM_DV
    return pl.pallas_call(
        functools.partial(_mlstm_lat_kernel, n_chunks=nc),
        grid=(b, nc),
        in_specs=[fw(dk), fw(dk), fw(dv), fw(LANES), fw(LANES), bw(dk), bw(dk), bw(dv), bw(LANES), bw(LANES),
                  pl.BlockSpec((None, N_DIRHEAD, MLSTM_DV, MLSTM_DK), lambda i, j: (i, 0, 0, 0)),
                  pl.BlockSpec((None, N_DIRHEAD, MLSTM_DK), lambda i, j: (i, 0, 0)),
                  pl.BlockSpec((None, 1, LANES), lambda i, j: (i, 0, 0))],
        out_specs=pl.BlockSpec((nc, dv, lc), lambda i, j: (i, 0, 0)),
        out_shape=jax.ShapeDtypeStruct((b * nc, dv, lc), F32),
        scratch_shapes=[pltpu.VMEM((N_DIRHEAD, MLSTM_DV, MLSTM_DK), F32),
                        pltpu.VMEM((N_DIRHEAD, MLSTM_DK), F32),
                        pltpu.VMEM((1, LANES), F32)],
        compiler_params=_params(("parallel", "arbitrary")),
        name="mlstm_lat",
    )(mq, mk, mv, gate_b, gate_u, mq, mk, mv, gate_b, gate_u, c0, n0, m0)


def _tail_kernel(x_ref, modin_ref, mod_ref, att_ref, hst_ref, gmix_ref, wgate_ref, gml_ref, womla_ref, womlstm_ref,
                 wout_ref, gffn_ref, wfin_ref, wfout_ref, gfin_ref, y_ref, *, mod_row0, tiles_per_seq):
    shift_mix, scale_mix = _mod_chunks(modin_ref, mod_row0, tiles_per_seq)
    gate_mix, shift_ffn, scale_ffn, gate_ffn = _mod_chunks(mod_ref, mod_row0, tiles_per_seq)
    gml = gml_ref[...]
    tm = x_ref.shape[0]
    groups = [slice(r, r + TAIL_ROWS) for r in range(0, tm, TAIL_ROWS)]
    dv = MLSTM_HEADS * MLSTM_DV
    hb = [_mixer_input(x_ref[g, :], gmix_ref[...], shift_mix, scale_mix) for g in groups]
    smo = [_sigmoid(_dot_nt(h, wgate_ref[:dv, :])) for h in hb]
    g_a = [_sigmoid(_dot_nt(h, wgate_ref[dv:dv + D_MODEL, :])) for h in hb]
    g_b = [_sigmoid(_dot_nt(h, wgate_ref[dv + D_MODEL:, :])) for h in hb]
    a = [_dot(att_ref[g, :], womla_ref[...]) for g in groups]
    hm = []
    for gi in range(len(groups)):
        hs = hst_ref[gi].T
        parts = []
        for hd in range(MLSTM_HEADS):
            sl = slice(hd * MLSTM_DV, (hd + 1) * MLSTM_DV)
            parts.append((_rms(hs[:, sl], gml[:, sl]) * smo[gi][:, sl]).astype(BF16))
        hm.append(jnp.concatenate(parts, axis=1))
    bm = [_dot(h, womlstm_ref[...]) for h in hm]
    merged = [(ga_i * ai + gb_i * bi).astype(BF16) for ga_i, gb_i, ai, bi in zip(g_a, g_b, a, bm)]
    x1 = [x_ref[g, :] + gate_mix * _dot(mi, wout_ref[...]) for g, mi in zip(groups, merged)]
    h2 = [(_rms(xi, gffn_ref[...]) * (1.0 + scale_ffn) + shift_ffn).astype(BF16) for xi in x1]
    ga = [_dot(hi, wfin_ref[:, :FFN_HIDDEN]) for hi in h2]
    gu = [_dot(hi, wfin_ref[:, FFN_HIDDEN:]) for hi in h2]
    act = [(gi * _sigmoid(gi) * ui).astype(BF16) for gi, ui in zip(ga, gu)]
    for g, xi, ci in zip(groups, x1, act):
        y_ref[g, :] = _rms(xi + gate_ffn * _dot(ci, wfout_ref[...]), gfin_ref[...])


def _tail(x, mod_in, mod, att, hst, gmix, w_gate, gml, w_o_mla, w_o_mlstm, w_out, gffn, w_ffn_in, w_ffn_out, gfin, *,
          seq_len, latent):
    n_tok = x.shape[0]
    tm = TOKEN_TILE
    assert hst.shape[0] * TAIL_ROWS == n_tok and hst.shape[2] == TAIL_ROWS
    tok = lambda c: pl.BlockSpec((tm, c), lambda i: (i, 0))
    whole = lambda a: pl.BlockSpec(a.shape, lambda i: (0, 0))
    consts = (gmix, w_gate, gml, w_o_mla, w_o_mlstm, w_out, gffn, w_ffn_in, w_ffn_out, gfin)
    return pl.pallas_call(
        functools.partial(_tail_kernel, mod_row0=1 if latent else 0, tiles_per_seq=seq_len // tm if latent else None),
        grid=(n_tok // tm,),
        in_specs=[tok(D_MODEL), whole(mod_in), whole(mod), tok(att.shape[1]),
                  pl.BlockSpec((tm // TAIL_ROWS,) + hst.shape[1:], lambda i: (i, 0, 0))]
                 + [_const_spec(a.shape) for a in consts],
        out_specs=tok(D_MODEL),
        out_shape=jax.ShapeDtypeStruct((n_tok, D_MODEL), F32),
        compiler_params=_params(("parallel",)),
        name="tail_lat" if latent else "tail_ctx",
    )(x, mod_in, mod, att, hst, *consts)


_IN_SIZES = (Q_RANK, KV_RANK, ROPE_DIM, MLSTM_HEADS * MLSTM_DK, MLSTM_HEADS * MLSTM_DK, MLSTM_HEADS * MLSTM_DV,
             4 * MLSTM_HEADS, MLSTM_HEADS * MLSTM_DV, 2 * D_MODEL)
_IN_OFF = tuple(sum(_IN_SIZES[:i]) for i in range(len(_IN_SIZES) + 1))
PACK_ROWS = LANES
_PAD_AFTER = tuple((_SEG_OFF[i + 1] // PACK_ROWS, _SEG_WIDTHS[i] - _IN_SIZES[i], _IN_SIZES[i] % PACK_ROWS)
                   for i in range(len(_IN_SIZES)) if _SEG_WIDTHS[i] != _IN_SIZES[i])


PACK_PIECES = 6
N_PACK_STEPS = pl.cdiv(SEG_MO[0] // PACK_ROWS, PACK_PIECES)
_N_IN = _IN_OFF[-1]


def _pack_src_row(p):
    src = p * PACK_ROWS
    for first_piece_after, pad, _ in _PAD_AFTER:
        src = src - jnp.where(p >= first_piece_after, pad, 0)
    src = jnp.minimum(src, _N_IN - PACK_ROWS)
    return pl.multiple_of(src, math.gcd(PACK_ROWS, _N_IN, *(pad for _, pad, _ in _PAD_AFTER)))


def _pack_in_kernel(*refs):
    w_refs, adaln_in, (o_ref, mod_ref) = refs[:PACK_PIECES], refs[PACK_PIECES:PACK_PIECES + 4], refs[PACK_PIECES + 4:]
    _adaln_block(*adaln_in, mod_ref)
    row =lax.broadcasted_iota(jnp.int32, w_refs[0].shape, 0)
    for j, w_ref in enumerate(w_refs):
        p = pl.program_id(0) * PACK_PIECES + j
        valid = jnp.where(p < N_IN_PACKED // PACK_ROWS, PACK_ROWS, 0)
        for first_piece_after, _, width in _PAD_AFTER:
            valid = jnp.where(p == first_piece_after - 1, width, valid)
        o_ref[j * PACK_ROWS:(j + 1) * PACK_ROWS, :] = jnp.where(row < valid, w_ref[...], 0.0).astype(BF16)


def _pack_in(w_in_t, c_ctx, c, w_mod, b_mod):
    n, k = w_in_t.shape
    piece = lambda j: pl.BlockSpec((pl.Element(PACK_ROWS), pl.Element(k)),
                                   lambda i: (_pack_src_row(i * PACK_PIECES + j), 0))
    tn = N_MOD_MIXER * D_MODEL // N_PACK_STEPS
    return pl.pallas_call(
        _pack_in_kernel,
        grid=(N_PACK_STEPS,),
        in_specs=[piece(j) for j in range(PACK_PIECES)]
                 + [pl.BlockSpec(c_ctx.shape, lambda i: (0, 0)), pl.BlockSpec(c.shape, lambda i: (0, 0)),
                    pl.BlockSpec((D_MODEL, tn), lambda i: (0, i)), pl.BlockSpec((1, tn), lambda i: (0, i))],
        out_specs=[pl.BlockSpec((PACK_PIECES * PACK_ROWS, k), lambda i: (i, 0)),
                   pl.BlockSpec((COND_ROWS, tn), lambda i: (0, i))],
        out_shape=[jax.ShapeDtypeStruct((N_PACK_STEPS * PACK_PIECES * PACK_ROWS, k), BF16),
                   jax.ShapeDtypeStruct((COND_ROWS, N_MOD_MIXER * D_MODEL), F32)],
        compiler_params=_params(("parallel",)),
        name="pack_in",
    )(*([w_in_t] * PACK_PIECES), c_ctx, c, w_mod, b_mod)


def _pack_small(w_uq, w_ukv, b_gates):
    qd = NOPE_DIM + ROPE_DIM
    w_uq_p = jnp.pad(w_uq.reshape(Q_RANK, MLA_HEADS, qd), ((0, 0), (0, 0), (0, HEAD_BLOCK - qd)))
    w_uq_p = w_uq_p.reshape(Q_RANK, MLA_HEADS * HEAD_BLOCK).astype(BF16)
    kv = w_ukv.reshape(KV_RANK, MLA_HEADS, NOPE_DIM + MLA_V_DIM)
    wk = jnp.pad(kv[:, :, :NOPE_DIM], ((0, 0), (0, 0), (0, HEAD_BLOCK - NOPE_DIM))).reshape(KV_RANK, -1)
    wv = kv[:, :, NOPE_DIM:].reshape(KV_RANK, -1)
    w_ukv_p = jnp.concatenate([wk, wv], axis=1).astype(BF16)
    bg = jnp.pad(b_gates, ((0, 0), (0, LANES - b_gates.shape[1])))
    return w_uq_p, w_ukv_p, bg


def _rope_tables(n_tokens):
    pos = np.arange(n_tokens)
    row = (pos // GRID_W).astype(np.float64)[:, None]
    col = (pos % GRID_W).astype(np.float64)[:, None]
    half = ROPE_DIM // 2
    inv = (np.float32(ROPE_BASE) ** (-np.arange(0, half, 2, dtype=np.float32) / np.float32(half))).astype(np.float64)
    r = np.arange(LANES) - ROPE_LANE0
    in_rope = (r >= 0) & (r < ROPE_DIM)
    rr = np.clip(r, 0, ROPE_DIM - 1)
    freq = inv[rr % (half // 2)][None, :]
    ang = np.where((rr // half == 0)[None, :], row * freq, col * freq).astype(np.float32).astype(np.float64)
    first = (rr % half) < (half // 2)
    cos = np.where(in_rope[None, :], np.cos(ang), 1.0)
    sin = np.sin(ang)
    sin_lo = np.where((in_rope & first)[None, :], -sin, 0.0)
    sin_hi = np.where((in_rope & ~first)[None, :], sin, 0.0)
    return tuple(jnp.asarray(t, dtype=F32) for t in (cos, sin_lo, sin_hi))


def kernel(x_prompt, x_sample, cache_ckv, cache_krope, state_C, state_n, state_m, c, c_ctx, w_mod, b_mod, g_norm_mix,
           w_in, b_gates, g_q_norm, w_uq, g_kv_norm, w_ukv, g_mlstm_norm, w_o_mla, w_o_mlstm, w_out, g_norm_ffn,
           w_ffn_in, w_ffn_out, g_final):
    bp, sp, d = x_prompt.shape
    bs, ss, _ = x_sample.shape
    layer = 0
    assert w_mod.shape[0] == 1 and sp == MLSTM_CHUNK and ss % MLSTM_CHUNK == 0

    adaln_args = (c_ctx[None, :], c, w_mod[layer], b_mod[layer][None, :])
    w_in_t = w_in[layer].T
    w_in_p, mod_in = _pack_in(w_in_t, *adaln_args)
    w_uq_p, w_ukv_p, bg = _pack_small(w_uq[layer], w_ukv[layer], b_gates[layer][None, :])
    row = lambda g: g[layer][None, :]
    shared_in = (row(g_norm_mix), w_in_p, row(g_q_norm), w_uq_p, row(g_kv_norm), w_ukv_p, bg)
    seq3 = lambda a, b_, s_: a.reshape(b_, s_, a.shape[-1])

    xl = x_sample.reshape(bs * ss, d)
    q, kc, v, mq_l, mk_l, mv_l, gb_l, gu_l = _inproj(xl, mod_in, _rope_tables(ss), *shared_in, seq_len=ss)
    krope_blk = jnp.pad(cache_krope[:, layer], ((0, 0), (0, 0), (ROPE_LANE0, LANES - ROPE_LANE0 - ROPE_DIM)))
    cache = (cache_ckv[:, layer], krope_blk, w_ukv_p)
    gate_cols = (w_in_t, _IN_OFF[7], _IN_OFF[9] - _IN_OFF[7])
    att_l, wgate_b, womla_b, womlstm_b, wout_b, wfin_b, wfout_b, mod_tail = _attention(
        seq3(q, bs, ss), seq3(kc, bs, ss), seq3(v, bs, ss), cache,
        cast_weights=(gate_cols, w_o_mla[layer], w_o_mlstm[layer], w_out[layer], w_ffn_in[layer], w_ffn_out[layer]),
        adaln_tail=adaln_args)
    tail_w = (row(g_norm_mix), wgate_b, row(g_mlstm_norm), womla_b, womlstm_b, wout_b, row(g_norm_ffn), wfin_b, wfout_b,
              g_final[None, :])

    xc = x_prompt.reshape(bp * sp, d)
    att, hs, new_c, new_n, new_m, ckv, kpe = _ctx_mixer(xc, mod_in, *shared_in, seq_len=sp)
    y_prompt = _tail(xc, mod_in, mod_tail, att.reshape(bp * sp, -1), hs, *tail_w,
                     seq_len=sp, latent=False).reshape(bp, sp, d)
    new_ckv = ckv.reshape(bp, 1, sp, KV_RANK)
    new_krope = jnp.swapaxes(kpe, 1, 2).reshape(bp, 1, sp, ROPE_DIM)
    new_C = new_c.reshape(bp, 1, 2, MLSTM_HEADS, MLSTM_DV, MLSTM_DK)
    new_N = new_n.reshape(bp, 1, 2, MLSTM_HEADS, MLSTM_DK)
    new_M = new_m[:, 0, :N_DIRHEAD].reshape(bp, 1, 2, MLSTM_HEADS)

    c0 = state_C[:, layer].reshape(bs, N_DIRHEAD, MLSTM_DV, MLSTM_DK)
    n0 = state_n[:, layer].reshape(bs, N_DIRHEAD, MLSTM_DK)
    m0 = jnp.pad(state_m[:, layer].reshape(bs, 1, N_DIRHEAD), ((0, 0), (0, 0), (0, LANES - N_DIRHEAD)))
    hs = _mlstm_lat(seq3(mq_l, bs, ss), seq3(mk_l, bs, ss), seq3(mv_l, bs, ss), seq3(gb_l, bs, ss), seq3(gu_l, bs, ss),
                    c0, n0, m0)
    y_sample = _tail(xl, mod_in, mod_tail, att_l.reshape(bs * ss, -1), hs, *tail_w,
                     seq_len=ss, latent=True).reshape(bs, ss, d)
    return (y_prompt, y_sample, new_ckv, new_krope, new_C, new_N, new_M)
```

```python
import functools
import math

import jax
import jax.numpy as jnp
import numpy as np
from jax import lax
from jax.experimental import pallas as pl
from jax.experimental.pallas import tpu as pltpu

F32 = jnp.float32
BF16 = jnp.bfloat16

D_MODEL = 1024
GRID_W = 64
MLA_HEADS = 8
Q_RANK = 384
KV_RANK = 256
NOPE_DIM = 64
ROPE_DIM = 32
MLA_V_DIM = 64
ROPE_BASE = 10000.0
MLA_SCALE = (NOPE_DIM + ROPE_DIM) ** -0.5
MLSTM_HEADS = 4
MLSTM_DK = 128
MLSTM_DV = 256
FFN_HIDDEN = ((8 * D_MODEL // 3 + 255) // 256) * 256
EPS = 1e-6

LANES = 128
HEAD_BLOCK = LANES
ROPE_LANE0 = NOPE_DIM
N_DIRHEAD = 2 * MLSTM_HEADS
MLSTM_CHUNK = 256
LOG2E = math.log2(math.e)
Q_PRESCALE = MLA_SCALE * LOG2E
V_SLAB = 2 * LANES
V_WIDTH = (MLA_HEADS // 2) * V_SLAB
TOKEN_TILE = 512
TAIL_ROWS = 256
INPROJ_ROWS = TOKEN_TILE
Q_TILE = 256
CTX_SEQS_PER_STEP = 4
VMEM_LIMIT = 56 * 1024 * 1024

_SEG_WIDTHS = (Q_RANK, KV_RANK, LANES, MLSTM_HEADS * MLSTM_DK, MLSTM_HEADS * MLSTM_DK,
               MLSTM_HEADS * MLSTM_DV, LANES, MLSTM_HEADS * MLSTM_DV, 2 * D_MODEL)
_SEG_OFF = tuple(sum(_SEG_WIDTHS[:i]) for i in range(len(_SEG_WIDTHS) + 1))
SEG_Q, SEG_KV, SEG_KPE, SEG_MQ, SEG_MK, SEG_MV, SEG_GATE, SEG_MO, SEG_BR = (
    (_SEG_OFF[i], _SEG_OFF[i + 1]) for i in range(9))
N_IN_PACKED = _SEG_OFF[-1]


def _dot(a, b):
    return jnp.dot(a, b, preferred_element_type=F32)


def _dot_nt(a, b):
    return lax.dot_general(a, b, (((1,), (1,)), ((), ())), preferred_element_type=F32)


def _rms(x, g):
    ms = jnp.mean(x * x, axis=-1, keepdims=True)
    return x * lax.rsqrt(ms + EPS) * g


def _sigmoid(x):
    return 1.0 / (1.0 + jnp.exp(-x))


def _const_spec(shape):
    nd = len(shape)
    return pl.BlockSpec(shape, lambda *_: (0,) * nd, pipeline_mode=pl.Buffered(1))


def _mod_chunks(mod_ref, row0, tiles_per_seq):
    row = row0 if tiles_per_seq is None else row0 + pl.program_id(0) // tiles_per_seq
    m = mod_ref[pl.ds(row, 1), :]
    return [m[:, k * D_MODEL:(k + 1) * D_MODEL] for k in range(m.shape[1] // D_MODEL)]


def _params(sem, flags=None, vmem_mib=None):
    limit = VMEM_LIMIT if vmem_mib is None else vmem_mib * 1024 * 1024
    return pltpu.CompilerParams(dimension_semantics=sem, vmem_limit_bytes=limit, flags=flags)


N_MOD_MIXER = 2
N_MOD_TAIL = 4


COND_ROWS = 8


def _adaln_block(cctx_ref, c_ref, w_ref, b_ref, o_ref):
    pad = jnp.zeros((COND_ROWS - 1 - c_ref.shape[0], D_MODEL), F32)
    c = jnp.concatenate([cctx_ref[...], c_ref[...], pad], axis=0)
    s = c * _sigmoid(c)
    o_ref[...] = _dot(s.astype(BF16), w_ref[...].astype(BF16)) + b_ref[...]


def _rope_block(x, cos, sin_lo, sin_hi):
    return x * cos + pltpu.roll(x, LANES - 8, 1) * sin_lo + pltpu.roll(x, 8, 1) * sin_hi


def _store_kv(kvn, kp, kc_ref, v_ref, g=slice(None)):
    for hd in range(MLA_HEADS):
        sl = slice(hd * HEAD_BLOCK, (hd + 1) * HEAD_BLOCK)
        kc_ref[g, sl] = (kvn[:, sl] + kp).astype(BF16)
    v0 = MLA_HEADS * HEAD_BLOCK
    ones = jnp.ones((kvn.shape[0], LANES), BF16)
    for pair in range(MLA_HEADS // 2):
        v_ref[g, pair * V_SLAB:pair * V_SLAB + LANES] = kvn[:, v0 + pair * LANES:v0 + (pair + 1) * LANES].astype(BF16)
        v_ref[g, pair * V_SLAB + LANES:(pair + 1) * V_SLAB] = ones


def _mixer_input(x, g, shift, scale):
    return (_rms(x, g) * (1.0 + scale) + shift).astype(BF16)


def _inproj_kernel(*refs, rope, mod_row0, tiles_per_seq):
    if rope:
        (x_ref, mod_ref, cos_ref, slo_ref, shi_ref, gmix_ref, win_ref, gq_ref, wuq_ref, gkv_ref, wukv_ref,
         bg_ref, q_ref, kc_ref, v_ref, mq_ref, mk_ref, mv_ref, gb_ref, gu_ref) = refs
    else:
        (x_ref, mod_ref, gmix_ref, win_ref, gq_ref, wuq_ref, gkv_ref, wukv_ref,
         bg_ref, q_ref, kc_ref, v_ref, mq_ref, mk_ref, mv_ref, gate_ref, ckv_ref, kpet_ref) = refs
    shift, scale = _mod_chunks(mod_ref, mod_row0, tiles_per_seq)
    for r0 in range(0, x_ref.shape[0], INPROJ_ROWS):
        g = slice(r0, r0 + INPROJ_ROWS)
        if rope:
            cos, slo, shi = cos_ref[g, :], slo_ref[g, :], shi_ref[g, :]
        hb = _mixer_input(x_ref[g, :], gmix_ref[...], shift, scale)

        def proj(seg):
            return _dot_nt(hb, win_ref[seg[0]:seg[1], :])

        low = proj((SEG_Q[0], SEG_KPE[1]))
        qn = _rms(low[:, SEG_Q[0]:SEG_Q[1]], gq_ref[...]).astype(BF16)
        ckv = _rms(low[:, SEG_KV[0]:SEG_KV[1]], gkv_ref[...])
        zkpe = low[:, SEG_KPE[0]:SEG_KPE[1]]
        gates = proj(SEG_GATE) + bg_ref[...]
        if rope:
            for c0 in range(0, INPROJ_ROWS, MLSTM_CHUNK):
                b, u, _ = _gate_prep(gates[c0:c0 + MLSTM_CHUNK, :])
                gb_ref[r0 + c0:r0 + c0 + MLSTM_CHUNK, :] = b
                gu_ref[r0 + c0:r0 + c0 + MLSTM_CHUNK, :] = u
        else:
            gate_ref[g, :] = gates

        q = _dot(qn, wuq_ref[...])
        for hd in range(MLA_HEADS):
            sl = slice(hd * HEAD_BLOCK, (hd + 1) * HEAD_BLOCK)
            qh = q[:, sl]
            if rope:
                qh = _rope_block(qh, cos, slo, shi)
            q_ref[g, sl] = (qh * Q_PRESCALE).astype(BF16)

        if not rope:
            ckv_ref[g, :] = ckv
            kpe_t = zkpe.T
            seq = kpet_ref.shape[2]
            for sq in range(INPROJ_ROWS // seq):
                kpet_ref[r0 // seq + sq] = kpe_t[:ROPE_DIM, sq * seq:(sq + 1) * seq]
        kp = pltpu.roll(zkpe, ROPE_LANE0, 1)
        if rope:
            kp = _rope_block(kp, cos, slo, shi)
        kvn = _dot(ckv.astype(BF16), wukv_ref[...])
        _store_kv(kvn, kp, kc_ref, v_ref, g)

        mv_ref[g, :] = proj(SEG_MV).astype(BF16)
        mk_ref[g, :] = (proj(SEG_MK) * (MLSTM_DK ** -0.5)).astype(BF16)
        mq_ref[g, :] = proj(SEG_MQ).astype(BF16)


def _inproj(x, mod, rope_tabs, gmix, w_in_p, gq, w_uq_p, gkv, w_ukv_p, bg, *, seq_len):
    n_tok = x.shape[0]
    tm = TOKEN_TILE
    tiles_per_seq = seq_len // tm
    rope = rope_tabs is not None
    tok = lambda c: pl.BlockSpec((tm, c), lambda i: (i, 0))
    in_specs = [tok(D_MODEL), pl.BlockSpec(mod.shape, lambda i: (0, 0))]
    args = [x, mod]
    if rope:
        in_specs += [pl.BlockSpec((tm, LANES), lambda i: (i % tiles_per_seq, 0))] * 3
        args += list(rope_tabs)
    win_rows = SEG_MO[0]
    in_specs += [_const_spec(gmix.shape), _const_spec((win_rows, w_in_p.shape[1]))]
    in_specs += [_const_spec(a.shape) for a in (gq, w_uq_p, gkv, w_ukv_p, bg)]
    args += [gmix, w_in_p, gq, w_uq_p, gkv, w_ukv_p, bg]
    widths = [(MLA_HEADS * HEAD_BLOCK, BF16), (MLA_HEADS * HEAD_BLOCK, BF16), (V_WIDTH, BF16),
              (MLSTM_HEADS * MLSTM_DK, BF16), (MLSTM_HEADS * MLSTM_DK, BF16), (MLSTM_HEADS * MLSTM_DV, BF16),
              (LANES, F32)]
    if rope:
        widths.append((LANES, F32))
    out_specs = [tok(c) for c, _ in widths]
    out_shape = [jax.ShapeDtypeStruct((n_tok, c), dt) for c, dt in widths]
    if not rope:
        out_specs += [tok(KV_RANK), pl.BlockSpec((tm // seq_len, ROPE_DIM, seq_len), lambda i: (i, 0, 0))]
        out_shape += [jax.ShapeDtypeStruct((n_tok, KV_RANK), F32),
                      jax.ShapeDtypeStruct((n_tok // seq_len, ROPE_DIM, seq_len), F32)]
    return pl.pallas_call(
        functools.partial(_inproj_kernel, rope=rope, mod_row0=1 if rope else 0,
                          tiles_per_seq=tiles_per_seq if rope else None),
        grid=(n_tok // tm,),
        in_specs=in_specs,
        out_specs=out_specs,
        out_shape=out_shape,
        compiler_params=_params(("parallel",)),
        name="inproj_lat" if rope else "inproj_ctx",
    )(*args)


def _attn_kernel(*refs, has_cache, n_cast, has_adaln):
    n_in = 6 if has_cache else 3
    if has_cache:
        q_ref, k_ref, v_ref, ckv_ref, kp_ref, wukv_ref = refs[:n_in]
        kc_ref, vc_ref = refs[-2:]

        @pl.when(pl.program_id(1) == 0)
        def _():
            kvn = _dot(ckv_ref[0].astype(BF16), wukv_ref[...])
            past = kp_ref.shape[1]
            kp = jnp.concatenate([jnp.zeros((past, ROPE_LANE0), F32), kp_ref[0],
                                  jnp.zeros((past, LANES - ROPE_LANE0 - ROPE_DIM), F32)], axis=1)
            _store_kv(kvn, kp, kc_ref.at[0], vc_ref.at[0])
    else:
        q_ref, k_ref, v_ref = refs[:n_in]
    n_all_in = n_in + n_cast + (4 if has_adaln else 0)
    o_ref = refs[n_all_in]
    for w_ref, wb_ref in zip(refs[n_in:n_in + n_cast], refs[n_all_in + 1:n_all_in + 1 + n_cast]):
        wb_ref[...] = w_ref[...].astype(BF16)
    if has_adaln:
        _adaln_block(*refs[n_in + n_cast:n_all_in], refs[n_all_in + 1 + n_cast])
    tq = q_ref.shape[1]
    lane = lax.broadcasted_iota(jnp.int32, (tq, LANES), 1)
    for sq in range(q_ref.shape[0]):
        hsl = [slice(hd * HEAD_BLOCK, (hd + 1) * HEAD_BLOCK) for hd in range(MLA_HEADS)]
        vsl = [slice(hd // 2 * V_SLAB, (hd // 2 + 1) * V_SLAB) for hd in range(MLA_HEADS)]
        s = [_dot_nt(q_ref[sq, :, sl], k_ref[sq, :, sl]) for sl in hsl]
        m = [jnp.max(si, axis=-1, keepdims=True) for si in s]
        if has_cache:
            sc = [_dot_nt(q_ref[sq, :, sl], kc_ref[sq, :, sl]) for sl in hsl]
            m = [jnp.maximum(mi, jnp.max(ci, axis=-1, keepdims=True)) for mi, ci in zip(m, sc)]
        p = [jnp.exp2(si - mi).astype(BF16) for si, mi in zip(s, m)]
        o = [_dot(pi, v_ref[sq, :, sl]) for pi, sl in zip(p, vsl)]
        if has_cache:
            pc = [jnp.exp2(ci - mi).astype(BF16) for ci, mi in zip(sc, m)]
            o = [oi + _dot(pi, vc_ref[sq, :, sl]) for oi, pi, sl in zip(o, pc, vsl)]
        outs = [oi[:, :LANES] * (1.0 / oi[:, LANES:]) for oi in o]
        for pair in range(MLA_HEADS // 2):
            o_ref[sq, :, pair * LANES:(pair + 1) * LANES] = jnp.where(
                lane < MLA_V_DIM, outs[2 * pair], outs[2 * pair + 1]).astype(BF16)


def _attention(q, k, v, cache=None, cast_weights=(), adaln_tail=None):
    nb, s, _ = q.shape
    tq = Q_TILE
    ns = 1
    b = nb // ns
    n_steps = b * (s // tq)
    kw, vw, ow = MLA_HEADS * HEAD_BLOCK, V_WIDTH, MLA_HEADS * MLA_V_DIM
    in_specs = [pl.BlockSpec((ns, tq, kw), lambda i, j: (i, j, 0)),
                pl.BlockSpec((ns, s, kw), lambda i, j: (i, 0, 0)),
                pl.BlockSpec((ns, s, vw), lambda i, j: (i, 0, 0))]
    args = [q, k, v]
    scratch = []
    if cache is not None:
        past = cache[0].shape[1]
        in_specs += [pl.BlockSpec((ns, past, KV_RANK), lambda i, j: (i, 0, 0)),
                     pl.BlockSpec((ns, past, ROPE_DIM), lambda i, j: (i, 0, 0)),
                     _const_spec(cache[2].shape)]
        args += list(cache)
        scratch = [pltpu.VMEM((ns, past, kw), BF16), pltpu.VMEM((ns, past, vw), BF16)]
    step = lambda i, j: i * (s // tq) + j
    casts = [w if isinstance(w, tuple) else (w, 0, w.shape[0]) for w in cast_weights]
    cast_weights = [w for w, _, _ in casts]

    def cast_in_spec(w, row0, n_rows):
        rows = n_rows // n_steps
        if (row0, n_rows) == (0, w.shape[0]):
            return pl.BlockSpec((rows, w.shape[1]), lambda i, j: (step(i, j), 0))
        return pl.BlockSpec((pl.Element(rows), pl.Element(w.shape[1])),
                            lambda i, j: (pl.multiple_of(row0 + step(i, j) * rows, math.gcd(row0, rows)), 0))

    w_in_specs = [cast_in_spec(*c) for c in casts]
    w_specs = [pl.BlockSpec((n // n_steps, w.shape[1]), lambda i, j: (step(i, j), 0)) for w, _, n in casts]
    extra_in, extra_out, extra_shape = [], [], []
    if adaln_tail is not None:
        tn = N_MOD_TAIL * D_MODEL // n_steps
        col0 = N_MOD_MIXER * D_MODEL // tn
        extra_in = [pl.BlockSpec(adaln_tail[0].shape, lambda i, j: (0, 0)),
                    pl.BlockSpec(adaln_tail[1].shape, lambda i, j: (0, 0)),
                    pl.BlockSpec((D_MODEL, tn), lambda i, j: (0, col0 + step(i, j))),
                    pl.BlockSpec((1, tn), lambda i, j: (0, col0 + step(i, j)))]
        extra_out = [pl.BlockSpec((COND_ROWS, tn), lambda i, j: (0, step(i, j)))]
        extra_shape = [jax.ShapeDtypeStruct((COND_ROWS, N_MOD_TAIL * D_MODEL), F32)]
    outs = pl.pallas_call(
        functools.partial(_attn_kernel, has_cache=cache is not None, n_cast=len(cast_weights),
                          has_adaln=adaln_tail is not None),
        grid=(b, s // tq),
        in_specs=in_specs + w_in_specs + extra_in,
        out_specs=[pl.BlockSpec((ns, tq, ow), lambda i, j: (i, j, 0))] + w_specs + extra_out,
        out_shape=[jax.ShapeDtypeStruct((nb, s, ow), BF16)]
                  + [jax.ShapeDtypeStruct((n, w.shape[1]), BF16) for w, _, n in casts] + extra_shape,
        scratch_shapes=scratch,
        compiler_params=_params(("parallel", "arbitrary")),
        name="attn_lat" if cache is not None else "attn_ctx",
    )(*args, *cast_weights, *(adaln_tail or ()))
    return outs[0] if len(outs) == 1 else outs


def _lane_bcast(x, k, width=None):
    y = jnp.broadcast_to(x[:, k:k + 1], x.shape)
    reps = (width or LANES) // LANES
    return y if reps == 1 else jnp.concatenate([y] * reps, axis=1)


def _prefix_sum_rows(x):
    n = x.shape[0]
    row = lax.broadcasted_iota(jnp.int32, x.shape, 0)
    shift = 1
    while shift < n:
        x = x + jnp.where(row >= shift, pltpu.roll(x, shift, 0), 0.0)
        shift *= 2
    return x


def _gate_prep(g):
    n = g.shape[0]
    lane = lax.broadcasted_iota(jnp.int32, (n, LANES), 1)
    fpre = pltpu.roll(g, LANES - N_DIRHEAD, 1)
    lf = jnp.minimum(fpre, 0.0) - jnp.log(1.0 + jnp.exp(-jnp.abs(fpre)))
    binc = _prefix_sum_rows(lf)
    tot = binc[n - 1:n, :]
    b = jnp.where(lane < MLSTM_HEADS, binc, tot - binc + lf)
    return b, g - b, tot


def _chunk_setup(b, u, tot, m_prev):
    g_last = jnp.maximum(jnp.max(u, axis=0, keepdims=True), m_prev)
    u2 = u * LOG2E
    w = jnp.exp2(u2 - g_last * LOG2E)
    return u2, (b * LOG2E).T, w.T, tot + g_last, jnp.exp(m_prev - g_last)


def _state_update(vt, k, wt, kk):
    c_new = _dot((vt * wt[kk:kk + 1, :]).astype(BF16), k)
    n_new = _dot(wt[:N_DIRHEAD, :].astype(BF16), k)[kk:kk + 1, :]
    return c_new, n_new


def _dir_weights(s0t, u_b, m_row, *, fwd):
    n = s0t.shape[0]
    r = lax.broadcasted_iota(jnp.int32, (n, n), 0)
    c = lax.broadcasted_iota(jnp.int32, (n, n), 1)
    a = jnp.where((r <= c) if fwd else (r >= c), u_b, -jnp.inf)
    g = jnp.maximum(jnp.max(a, axis=0, keepdims=True), m_row)
    sdt = s0t * jnp.exp2(a - g)
    return sdt.astype(BF16), jnp.sum(sdt, axis=0, keepdims=True), g


def _dir_finish(ht, den, g, bt_row, m_row, inter=None):
    if inter is not None:
        w_inter = jnp.exp2(m_row - g)
        ht = ht + w_inter * inter[0]
        den = den + w_inter * inter[1]
    floor = jnp.exp2(-(bt_row + g))
    return ht * (1.0 / jnp.maximum(jnp.abs(den), floor))


def _mlstm_ctx_kernel(q_ref, k_ref, v_ref, g_ref, ht_ref, c_ref, n_ref, m_ref):
    n = q_ref.shape[1]
    zero_lanes = jnp.zeros((1, LANES), F32)
    zero_row = jnp.zeros((1, n), F32)
    dirs = ((True, 0), (False, MLSTM_HEADS))
    setup = [_chunk_setup(*_gate_prep(g_ref[sq]), zero_lanes) for sq in range(q_ref.shape[0])]
    units = [(sq, hd) for sq in range(q_ref.shape[0]) for hd in range(MLSTM_HEADS)]
    ks = [k_ref[sq, :, hd * MLSTM_DK:(hd + 1) * MLSTM_DK] for sq, hd in units]
    s0ts = [_dot_nt(k, q_ref[sq, :, hd * MLSTM_DK:(hd + 1) * MLSTM_DK]) for k, (sq, hd) in zip(ks, units)]
    vts = [v_ref[sq, :, hd * MLSTM_DV:(hd + 1) * MLSTM_DV].T for sq, hd in units]
    wts = [[_dir_weights(s0t, _lane_bcast(setup[sq][0], off + hd, n), zero_row, fwd=fwd) for fwd, off in dirs]
           for s0t, (sq, hd) in zip(s0ts, units)]
    hts = [[_dot(vt, sdt) for sdt, _, _ in wt] for vt, wt in zip(vts, wts)]
    for (sq, hd), k, vt, wt, ht2 in zip(units, ks, vts, wts, hts):
        _, bt2, w_t, _, _ = setup[sq]
        ht_ref[sq, hd * MLSTM_DV:(hd + 1) * MLSTM_DV, :] = sum(
            _dir_finish(ht_d, den, g, bt2[off + hd:off + hd + 1, :], zero_row)
            for ht_d, (_, den, g), (_, off) in zip(ht2, wt, dirs))
        for _, off in dirs:
            c_ref[sq, off + hd], n_ref[sq, off + hd:off + hd + 1, :] = _state_update(vt, k, w_t, off + hd)
    for sq in range(q_ref.shape[0]):
        m_ref[sq] = setup[sq][3]


class _SeqView:
    def __init__(self, ref, seq_len):
        self.ref, self.seq_len = ref, seq_len
        self.shape = (ref.shape[0] // seq_len, seq_len, ref.shape[1])

    def __getitem__(self, idx):
        sq, rows, cols = idx if isinstance(idx, tuple) else (idx, slice(None), slice(None))
        assert rows == slice(None)
        return self.ref[sq * self.seq_len:(sq + 1) * self.seq_len, cols]


N_CTX_CONSTS = 7


def _ctx_mixer_kernel(*refs, seq_len):
    x_ref, mod_ref = refs[:2]
    consts = refs[2:2 + N_CTX_CONSTS]
    att_ref, ht_ref, c_ref, n_ref, m_ref, ckv_ref, kpet_ref = refs[2 + N_CTX_CONSTS:9 + N_CTX_CONSTS]
    q_s, kc_s, v_s, mq_s, mk_s, mv_s, gate_s = refs[9 + N_CTX_CONSTS:]
    _inproj_kernel(x_ref, mod_ref, *consts, q_s, kc_s, v_s, mq_s, mk_s, mv_s, gate_s, ckv_ref, kpet_ref,
                   rope=False, mod_row0=0, tiles_per_seq=None)
    q3, k3, v3, mq3, mk3, mv3, g3 = (_SeqView(r, seq_len) for r in (q_s, kc_s, v_s, mq_s, mk_s, mv_s, gate_s))
    _attn_kernel(q3, k3, v3, att_ref, has_cache=False, n_cast=0, has_adaln=False)
    _mlstm_ctx_kernel(mq3, mk3, mv3, g3, ht_ref, c_ref, n_ref, m_ref)


def _ctx_mixer(x, mod, gmix, w_in_p, gq, w_uq_p, gkv, w_ukv_p, bg, *, seq_len):
    n_tok = x.shape[0]
    ns, s = CTX_SEQS_PER_STEP, seq_len
    nseq, tm = n_tok // s, ns * s
    dv = MLSTM_HEADS * MLSTM_DV
    consts = (gmix, w_in_p, gq, w_uq_p, gkv, w_ukv_p, bg)
    const_specs = [_const_spec(gmix.shape), _const_spec((SEG_MO[0], w_in_p.shape[1]))]
    const_specs += [_const_spec(a.shape) for a in consts[2:]]
    seq_blk = lambda *dims: pl.BlockSpec((ns,) + dims, lambda i: (i,) + (0,) * len(dims))
    return pl.pallas_call(
        functools.partial(_ctx_mixer_kernel, seq_len=s),
        grid=(nseq // ns,),
        in_specs=[pl.BlockSpec((tm, D_MODEL), lambda i: (i, 0)), pl.BlockSpec(mod.shape, lambda i: (0, 0))]
                 + const_specs,
        out_specs=[seq_blk(s, MLA_HEADS * MLA_V_DIM), seq_blk(dv, s), seq_blk(N_DIRHEAD, MLSTM_DV, MLSTM_DK),
                   seq_blk(N_DIRHEAD, MLSTM_DK), seq_blk(1, LANES),
                   pl.BlockSpec((tm, KV_RANK), lambda i: (i, 0)), seq_blk(ROPE_DIM, s)],
        out_shape=[jax.ShapeDtypeStruct((nseq, s, MLA_HEADS * MLA_V_DIM), BF16),
                   jax.ShapeDtypeStruct((nseq, dv, s), F32),
                   jax.ShapeDtypeStruct((nseq, N_DIRHEAD, MLSTM_DV, MLSTM_DK), F32),
                   jax.ShapeDtypeStruct((nseq, N_DIRHEAD, MLSTM_DK), F32),
                   jax.ShapeDtypeStruct((nseq, 1, LANES), F32),
                   jax.ShapeDtypeStruct((n_tok, KV_RANK), F32),
                   jax.ShapeDtypeStruct((nseq, ROPE_DIM, s), F32)],
        scratch_shapes=[pltpu.VMEM((tm, MLA_HEADS * HEAD_BLOCK), BF16), pltpu.VMEM((tm, MLA_HEADS * HEAD_BLOCK), BF16),
                        pltpu.VMEM((tm, V_WIDTH), BF16), pltpu.VMEM((tm, MLSTM_HEADS * MLSTM_DK), BF16),
                        pltpu.VMEM((tm, MLSTM_HEADS * MLSTM_DK), BF16), pltpu.VMEM((tm, dv), BF16),
                        pltpu.VMEM((tm, LANES), F32)],
        compiler_params=_params(("parallel",)),
        name="ctx_mixer",
    )(x, mod, *consts)


def _mlstm_lat_kernel(qf_ref, kf_ref, vf_ref, bf_ref, uf_ref, qb_ref, kb_ref, vb_ref, bb_ref, ub_ref,
                      c0_ref, n0_ref, m0_ref,
                      ht_ref, c_ref, n_ref, m_ref, *, n_chunks):
    step = pl.program_id(1)

    @pl.when(step == 0)
    def _():
        c_ref[...] = c0_ref[...]
        n_ref[...] = n0_ref[...]
        m_ref[...] = m0_ref[...]
        ht_ref[...] = jnp.zeros_like(ht_ref)

    n = qf_ref.shape[0]
    is_f = lax.broadcasted_iota(jnp.int32, (n, LANES), 1) < MLSTM_HEADS
    b = jnp.where(is_f, bf_ref[...], bb_ref[...])
    u = jnp.where(is_f, uf_ref[...], ub_ref[...])
    tot = jnp.where(is_f[:1], bf_ref[n - 1:n, :], bb_ref[0:1, :])
    m_prev = m_ref[...]
    u2, bt2, w_t, m_new, decay = _chunk_setup(b, u, tot, m_prev)
    m2_prev = m_prev * LOG2E
    units = [(fwd, hd, hd if fwd else MLSTM_HEADS + hd) for fwd in (True, False) for hd in range(MLSTM_HEADS)]
    refs = {True: (qf_ref, kf_ref, vf_ref), False: (qb_ref, kb_ref, vb_ref)}
    qs = [refs[fwd][0][:, hd * MLSTM_DK:(hd + 1) * MLSTM_DK] for fwd, hd, _ in units]
    ks = [refs[fwd][1][:, hd * MLSTM_DK:(hd + 1) * MLSTM_DK] for fwd, hd, _ in units]
    vts = [refs[fwd][2][:, hd * MLSTM_DV:(hd + 1) * MLSTM_DV].T for fwd, hd, _ in units]
    s0ts = [_dot_nt(k, q) for k, q in zip(ks, qs)]
    c_prevs = [c_ref[kk] for _, _, kk in units]
    n_prevs = [n_ref[kk:kk + 1, :] for _, _, kk in units]
    inters = [(_dot_nt(c.astype(BF16), q), _dot_nt(jnp.broadcast_to(nv, (8, MLSTM_DK)).astype(BF16), q)[0:1, :])
              for c, nv, q in zip(c_prevs, n_prevs, qs)]
    m_rows = [_lane_bcast(m2_prev, kk, n) for _, _, kk in units]
    wts = [_dir_weights(s0t, _lane_bcast(u2, kk, n), m_row, fwd=fwd)
           for s0t, m_row, (fwd, _, kk) in zip(s0ts, m_rows, units)]
    hts = [_dot(vt, sdt) for vt, (sdt, _, _) in zip(vts, wts)]
    for (fwd, hd, kk), k, vt, (_, den, g), ht, inter, m_row, c_prev, n_prev in zip(
            units, ks, vts, wts, hts, inters, m_rows, c_prevs, n_prevs):
        chunk = step if fwd else n_chunks - 1 - step
        ht_ref[chunk, hd * MLSTM_DV:(hd + 1) * MLSTM_DV, :] += _dir_finish(ht, den, g, bt2[kk:kk + 1, :], m_row, inter)
        c_new, n_new = _state_update(vt, k, w_t, kk)
        dk = _lane_bcast(decay, kk)
        c_ref[kk] = dk * c_prev + c_new
        n_ref[kk:kk + 1, :] = dk * n_prev + n_new
    m_ref[...] = m_new


def _mlstm_lat(mq, mk, mv, gate_b, gate_u, c0, n0, m0):
    b, s, _ = mq.shape
    lc = MLSTM_CHUNK
    nc = s // lc
    fw = lambda c: pl.BlockSpec((None, lc, c), lambda i, j: (i, j, 0))
    bw = lambda c: pl.BlockSpec((None, lc, c), lambda i, j: (i, nc - 1 - j, 0))
    dk, dv = MLSTM_HEADS * MLSTM_DK, MLSTM_HEADS * MLSTM_DV
    return pl.pallas_call(
        functools.partial(_mlstm_lat_kernel, n_chunks=nc),
        grid=(b, nc),
        in_specs=[fw(dk), fw(dk), fw(dv), fw(LANES), fw(LANES), bw(dk), bw(dk), bw(dv), bw(LANES), bw(LANES),
                  pl.BlockSpec((None, N_DIRHEAD, MLSTM_DV, MLSTM_DK), lambda i, j: (i, 0, 0, 0)),
                  pl.BlockSpec((None, N_DIRHEAD, MLSTM_DK), lambda i, j: (i, 0, 0)),
                  pl.BlockSpec((None, 1, LANES), lambda i, j: (i, 0, 0))],
        out_specs=pl.BlockSpec((nc, dv, lc), lambda i, j: (i, 0, 0)),
        out_shape=jax.ShapeDtypeStruct((b * nc, dv, lc), F32),
        scratch_shapes=[pltpu.VMEM((N_DIRHEAD, MLSTM_DV, MLSTM_DK), F32),
                        pltpu.VMEM((N_DIRHEAD, MLSTM_DK), F32),
                        pltpu.VMEM((1, LANES), F32)],
        compiler_params=_params(("parallel", "arbitrary")),
        name="mlstm_lat",
    )(mq, mk, mv, gate_b, gate_u, mq, mk, mv, gate_b, gate_u, c0, n0, m0)


def _tail_kernel(x_ref, modin_ref, mod_ref, att_ref, hst_ref, gmix_ref, wgate_ref, gml_ref, womla_ref, womlstm_ref,
                 wout_ref, gffn_ref, wfin_ref, wfout_ref, gfin_ref, y_ref, *, mod_row0, tiles_per_seq):
    shift_mix, scale_mix = _mod_chunks(modin_ref, mod_row0, tiles_per_seq)
    gate_mix, shift_ffn, scale_ffn, gate_ffn = _mod_chunks(mod_ref, mod_row0, tiles_per_seq)
    gml = gml_ref[...]
    tm = x_ref.shape[0]
    groups = [slice(r, r + TAIL_ROWS) for r in range(0, tm, TAIL_ROWS)]
    dv = MLSTM_HEADS * MLSTM_DV
    hb = [_mixer_input(x_ref[g, :], gmix_ref[...], shift_mix, scale_mix) for g in groups]
    smo = [_sigmoid(_dot_nt(h, wgate_ref[:dv, :])) for h in hb]
    g_a = [_sigmoid(_dot_nt(h, wgate_ref[dv:dv + D_MODEL, :])) for h in hb]
    g_b = [_sigmoid(_dot_nt(h, wgate_ref[dv + D_MODEL:, :])) for h in hb]
    a = [_dot(att_ref[g, :], womla_ref[...]) for g in groups]
    hm = []
    for gi in range(len(groups)):
        hs = hst_ref[gi].T
        parts = []
        for hd in range(MLSTM_HEADS):
            sl = slice(hd * MLSTM_DV, (hd + 1) * MLSTM_DV)
            parts.append((_rms(hs[:, sl], gml[:, sl]) * smo[gi][:, sl]).astype(BF16))
        hm.append(jnp.concatenate(parts, axis=1))
    bm = [_dot(h, womlstm_ref[...]) for h in hm]
    merged = [(ga_i * ai + gb_i * bi).astype(BF16) for ga_i, gb_i, ai, bi in zip(g_a, g_b, a, bm)]
    x1 = [x_ref[g, :] + gate_mix * _dot(mi, wout_ref[...]) for g, mi in zip(groups, merged)]
    h2 = [(_rms(xi, gffn_ref[...]) * (1.0 + scale_ffn) + shift_ffn).astype(BF16) for xi in x1]
    ga = [_dot(hi, wfin_ref[:, :FFN_HIDDEN]) for hi in h2]
    gu = [_dot(hi, wfin_ref[:, FFN_HIDDEN:]) for hi in h2]
    act = [(gi * _sigmoid(gi) * ui).astype(BF16) for gi, ui in zip(ga, gu)]
    for g, xi, ci in zip(groups, x1, act):
        y_ref[g, :] = _rms(xi + gate_ffn * _dot(ci, wfout_ref[...]), gfin_ref[...])


def _tail(x, mod_in, mod, att, hst, gmix, w_gate, gml, w_o_mla, w_o_mlstm, w_out, gffn, w_ffn_in, w_ffn_out, gfin, *,
          seq_len, latent):
    n_tok = x.shape[0]
    tm = TOKEN_TILE
    assert hst.shape[0] * TAIL_ROWS == n_tok and hst.shape[2] == TAIL_ROWS
    tok = lambda c: pl.BlockSpec((tm, c), lambda i: (i, 0))
    whole = lambda a: pl.BlockSpec(a.shape, lambda i: (0, 0))
    consts = (gmix, w_gate, gml, w_o_mla, w_o_mlstm, w_out, gffn, w_ffn_in, w_ffn_out, gfin)
    return pl.pallas_call(
        functools.partial(_tail_kernel, mod_row0=1 if latent else 0, tiles_per_seq=seq_len // tm if latent else None),
        grid=(n_tok // tm,),
        in_specs=[tok(D_MODEL), whole(mod_in), whole(mod), tok(att.shape[1]),
                  pl.BlockSpec((tm // TAIL_ROWS,) + hst.shape[1:], lambda i: (i, 0, 0))]
                 + [_const_spec(a.shape) for a in consts],
        out_specs=tok(D_MODEL),
        out_shape=jax.ShapeDtypeStruct((n_tok, D_MODEL), F32),
        compiler_params=_params(("parallel",)),
        name="tail_lat" if latent else "tail_ctx",
    )(x, mod_in, mod, att, hst, *consts)


_IN_SIZES = (Q_RANK, KV_RANK, ROPE_DIM, MLSTM_HEADS * MLSTM_DK, MLSTM_HEADS * MLSTM_DK, MLSTM_HEADS * MLSTM_DV,
             4 * MLSTM_HEADS, MLSTM_HEADS * MLSTM_DV, 2 * D_MODEL)
_IN_OFF = tuple(sum(_IN_SIZES[:i]) for i in range(len(_IN_SIZES) + 1))
PACK_ROWS = LANES
_PAD_AFTER = tuple((_SEG_OFF[i + 1] // PACK_ROWS, _SEG_WIDTHS[i] - _IN_SIZES[i], _IN_SIZES[i] % PACK_ROWS)
                   for i in range(len(_IN_SIZES)) if _SEG_WIDTHS[i] != _IN_SIZES[i])


PACK_PIECES = 6
N_PACK_STEPS = pl.cdiv(SEG_MO[0] // PACK_ROWS, PACK_PIECES)
_N_IN = _IN_OFF[-1]


def _pack_src_row(p):
    src = p * PACK_ROWS
    for first_piece_after, pad, _ in _PAD_AFTER:
        src = src - jnp.where(p >= first_piece_after, pad, 0)
    src = jnp.minimum(src, _N_IN - PACK_ROWS)
    return pl.multiple_of(src, math.gcd(PACK_ROWS, _N_IN, *(pad for _, pad, _ in _PAD_AFTER)))


def _pack_in_kernel(*refs):
    w_refs, adaln_in, (o_ref, mod_ref) = refs[:PACK_PIECES], refs[PACK_PIECES:PACK_PIECES + 4], refs[PACK_PIECES + 4:]
    _adaln_block(*adaln_in, mod_ref)
    row =lax.broadcasted_iota(jnp.int32, w_refs[0].shape, 0)
    for j, w_ref in enumerate(w_refs):
        p = pl.program_id(0) * PACK_PIECES + j
        valid = jnp.where(p < N_IN_PACKED // PACK_ROWS, PACK_ROWS, 0)
        for first_piece_after, _, width in _PAD_AFTER:
            valid = jnp.where(p == first_piece_after - 1, width, valid)
        o_ref[j * PACK_ROWS:(j + 1) * PACK_ROWS, :] = jnp.where(row < valid, w_ref[...], 0.0).astype(BF16)


def _pack_in(w_in_t, c_ctx, c, w_mod, b_mod):
    n, k = w_in_t.shape
    piece = lambda j: pl.BlockSpec((pl.Element(PACK_ROWS), pl.Element(k)),
                                   lambda i: (_pack_src_row(i * PACK_PIECES + j), 0))
    tn = N_MOD_MIXER * D_MODEL // N_PACK_STEPS
    return pl.pallas_call(
        _pack_in_kernel,
        grid=(N_PACK_STEPS,),
        in_specs=[piece(j) for j in range(PACK_PIECES)]
                 + [pl.BlockSpec(c_ctx.shape, lambda i: (0, 0)), pl.BlockSpec(c.shape, lambda i: (0, 0)),
                    pl.BlockSpec((D_MODEL, tn), lambda i: (0, i)), pl.BlockSpec((1, tn), lambda i: (0, i))],
        out_specs=[pl.BlockSpec((PACK_PIECES * PACK_ROWS, k), lambda i: (i, 0)),
                   pl.BlockSpec((COND_ROWS, tn), lambda i: (0, i))],
        out_shape=[jax.ShapeDtypeStruct((N_PACK_STEPS * PACK_PIECES * PACK_ROWS, k), BF16),
                   jax.ShapeDtypeStruct((COND_ROWS, N_MOD_MIXER * D_MODEL), F32)],
        compiler_params=_params(("parallel",)),
        name="pack_in",
    )(*([w_in_t] * PACK_PIECES), c_ctx, c, w_mod, b_mod)


def _pack_small(w_uq, w_ukv, b_gates):
    qd = NOPE_DIM + ROPE_DIM
    w_uq_p = jnp.pad(w_uq.reshape(Q_RANK, MLA_HEADS, qd), ((0, 0), (0, 0), (0, HEAD_BLOCK - qd)))
    w_uq_p = w_uq_p.reshape(Q_RANK, MLA_HEADS * HEAD_BLOCK).astype(BF16)
    kv = w_ukv.reshape(KV_RANK, MLA_HEADS, NOPE_DIM + MLA_V_DIM)
    wk = jnp.pad(kv[:, :, :NOPE_DIM], ((0, 0), (0, 0), (0, HEAD_BLOCK - NOPE_DIM))).reshape(KV_RANK, -1)
    wv = kv[:, :, NOPE_DIM:].reshape(KV_RANK, -1)
    w_ukv_p = jnp.concatenate([wk, wv], axis=1).astype(BF16)
    bg = jnp.pad(b_gates, ((0, 0), (0, LANES - b_gates.shape[1])))
    return w_uq_p, w_ukv_p, bg


def _rope_tables(n_tokens):
    pos = np.arange(n_tokens)
    row = (pos // GRID_W).astype(np.float64)[:, None]
    col = (pos % GRID_W).astype(np.float64)[:, None]
    half = ROPE_DIM // 2
    inv = (np.float32(ROPE_BASE) ** (-np.arange(0, half, 2, dtype=np.float32) / np.float32(half))).astype(np.float64)
    r = np.arange(LANES) - ROPE_LANE0
    in_rope = (r >= 0) & (r < ROPE_DIM)
    rr = np.clip(r, 0, ROPE_DIM - 1)
    freq = inv[rr % (half // 2)][None, :]
    ang = np.where((rr // half == 0)[None, :], row * freq, col * freq).astype(np.float32).astype(np.float64)
    first = (rr % half) < (half // 2)
    cos = np.where(in_rope[None, :], np.cos(ang), 1.0)
    sin = np.sin(ang)
    sin_lo = np.where((in_rope & first)[None, :], -sin, 0.0)
    sin_hi = np.where((in_rope & ~first)[None, :], sin, 0.0)
    return tuple(jnp.asarray(t, dtype=F32) for t in (cos, sin_lo, sin_hi))


def kernel(x_prompt, x_sample, cache_ckv, cache_krope, state_C, state_n, state_m, c, c_ctx, w_mod, b_mod, g_norm_mix,
           w_in, b_gates, g_q_norm, w_uq, g_kv_norm, w_ukv, g_mlstm_norm, w_o_mla, w_o_mlstm, w_out, g_norm_ffn,
           w_ffn_in, w_ffn_out, g_final):
    bp, sp, d = x_prompt.shape
    bs, ss, _ = x_sample.shape
    layer = 0
    assert w_mod.shape[0] == 1 and sp == MLSTM_CHUNK and ss % MLSTM_CHUNK == 0

    adaln_args = (c_ctx[None, :], c, w_mod[layer], b_mod[layer][None, :])
    w_in_t = w_in[layer].T
    w_in_p, mod_in = _pack_in(w_in_t, *adaln_args)
    w_uq_p, w_ukv_p, bg = _pack_small(w_uq[layer], w_ukv[layer], b_gates[layer][None, :])
    row = lambda g: g[layer][None, :]
    shared_in = (row(g_norm_mix), w_in_p, row(g_q_norm), w_uq_p, row(g_kv_norm), w_ukv_p, bg)
    seq3 = lambda a, b_, s_: a.reshape(b_, s_, a.shape[-1])

    xl = x_sample.reshape(bs * ss, d)
    q, kc, v, mq_l, mk_l, mv_l, gb_l, gu_l = _inproj(xl, mod_in, _rope_tables(ss), *shared_in, seq_len=ss)
    cache = (cache_ckv[:, layer], cache_krope[:, layer], w_ukv_p)
    gate_cols = (w_in_t, _IN_OFF[7], _IN_OFF[9] - _IN_OFF[7])
    att_l, wgate_b, womla_b, womlstm_b, wout_b, wfin_b, wfout_b, mod_tail = _attention(
        seq3(q, bs, ss), seq3(kc, bs, ss), seq3(v, bs, ss), cache,
        cast_weights=(gate_cols, w_o_mla[layer], w_o_mlstm[layer], w_out[layer], w_ffn_in[layer], w_ffn_out[layer]),
        adaln_tail=adaln_args)
    tail_w = (row(g_norm_mix), wgate_b, row(g_mlstm_norm), womla_b, womlstm_b, wout_b, row(g_norm_ffn), wfin_b, wfout_b,
              g_final[None, :])

    xc = x_prompt.reshape(bp * sp, d)
    att, hs, new_c, new_n, new_m, ckv, kpe = _ctx_mixer(xc, mod_in, *shared_in, seq_len=sp)
    y_prompt = _tail(xc, mod_in, mod_tail, att.reshape(bp * sp, -1), hs, *tail_w,
                     seq_len=sp, latent=False).reshape(bp, sp, d)
    new_ckv = ckv.reshape(bp, 1, sp, KV_RANK)
    new_krope = jnp.swapaxes(kpe, 1, 2).reshape(bp, 1, sp, ROPE_DIM)
    new_C = new_c.reshape(bp, 1, 2, MLSTM_HEADS, MLSTM_DV, MLSTM_DK)
    new_N = new_n.reshape(bp, 1, 2, MLSTM_HEADS, MLSTM_DK)
    new_M = new_m[:, 0, :N_DIRHEAD].reshape(bp, 1, 2, MLSTM_HEADS)

    c0 = state_C[:, layer].reshape(bs, N_DIRHEAD, MLSTM_DV, MLSTM_DK)
    n0 = state_n[:, layer].reshape(bs, N_DIRHEAD, MLSTM_DK)
    m0 = jnp.pad(state_m[:, layer].reshape(bs, 1, N_DIRHEAD), ((0, 0), (0, 0), (0, LANES - N_DIRHEAD)))
    hs = _mlstm_lat(seq3(mq_l, bs, ss), seq3(mk_l, bs, ss), seq3(mv_l, bs, ss), seq3(gb_l, bs, ss), seq3(gu_l, bs, ss),
                    c0, n0, m0)
    y_sample = _tail(xl, mod_in, mod_tail, att_l.reshape(bs * ss, -1), hs, *tail_w,
                     seq_len=ss, latent=True).reshape(bs, ss, d)
    return (y_prompt, y_sample, new_ckv, new_krope, new_C, new_N, new_M)
```
